```python
import math
import jax, jax.numpy as jnp
from jax import lax
import numpy as np

D_MODEL = 1024
BATCH = 2
SEQ = 8192
DEPTH = 4
DEC_BATCH = 8
DEC_SEQ = 8192
PAST_LEN = 128

N_MIXERS = 4
N_HY_LAYERS = (DEPTH + 3) // N_MIXERS
N_GA_LAYERS = (DEPTH + 2) // N_MIXERS
N_ML_LAYERS = (DEPTH + 1) // N_MIXERS
N_DA_LAYERS = DEPTH // N_MIXERS
GRID_W = 64
Q_BLOCK = 128
NORM_EPS = 1e-6

HY_BANDS = 16
HY_EMB = 1 + 2 * HY_BANDS
HY_FILTER_HIDDEN = 64
HY_DECAY_TARGET = 1e-2
HY_FAST_DECAY_PCT = 0.3
HY_SLOW_DECAY_PCT = 1.5

GA_HEADS = 8
GA_KV_HEADS = 2
GA_GROUP = GA_HEADS // GA_KV_HEADS
GA_HEAD_DIM = 128
AXIAL_THETA = 10000.0

ML_INNER = 2 * D_MODEL
ML_HEADS = 4
ML_HEAD_DIM = ML_INNER // ML_HEADS
ML_QKV_BLOCK = 4
ML_N_BLOCKS = ML_INNER // ML_QKV_BLOCK
ML_CHUNK = 64

DA_GROUPS = ((128, 1), (512, 4), (2048, 16))
DA_HEADS_PER_GROUP = 4
DA_HEADS = DA_HEADS_PER_GROUP * len(DA_GROUPS)
DA_HEAD_DIM = 128
ROPE_THETA = 500000.0
ROPE_DIMS = DA_HEAD_DIM // 4

N_EXPERTS = 16
EXPERT_FF = 1024
EC_CAPACITY = 2

kernel_name = "hybrid_bidir_hyena_gqa_mlstm_dilated_ec_moe"

F32 = jnp.float32


def rms_norm(x, g):
    xf = x.astype(F32)
    y = xf * lax.rsqrt(jnp.mean(xf * xf, axis=-1, keepdims=True) + NORM_EPS)
    return (y * g.astype(F32)).astype(x.dtype)


def conv3_centred(u, w, b):
    up = jnp.pad(u, ((0, 0), (1, 1), (0, 0)))
    return up[:, :-2] * w[0] + u * w[1] + up[:, 2:] * w[2] + b


def rotate_pairs(x, ang):
    cos = jnp.cos(ang).astype(x.dtype)[None, :, None, :]
    sin = jnp.sin(ang).astype(x.dtype)[None, :, None, :]
    x1, x2 = jnp.split(x, 2, axis=-1)
    return jnp.concatenate([x1 * cos - x2 * sin, x2 * cos + x1 * sin], axis=-1)


def hyena_filter_spectrum(L, f_w1, f_b1, f_w2, f_b2, f_w3, f_b3, f_freq, decay):
    t = jnp.linspace(0.0, 1.0, L, dtype=F32)[:, None]
    w_ang = 2.0 * math.pi * jnp.arange(L, dtype=F32)[:, None] / L
    bands = jnp.linspace(1e-4, HY_BANDS - 1, HY_BANDS, dtype=F32)[None, :]
    z = jnp.concatenate([t, jnp.cos(bands * w_ang), -jnp.sin(bands * w_ang)], axis=-1)
    fq = f_freq.astype(F32)
    h = jnp.sin(fq[0] * (z @ f_w1.astype(F32) + f_b1.astype(F32)))
    h = jnp.sin(fq[1] * (h @ f_w2.astype(F32) + f_b2.astype(F32)))
    h = h @ f_w3.astype(F32) + f_b3.astype(F32)
    h = h.reshape(L, 2, D_MODEL) * jnp.exp(-t[:, :, None] * jnp.abs(decay.astype(F32))[None])
    h_fwd, h_bwd = h[:, 0], h[:, 1]
    k = jnp.concatenate([h_fwd, jnp.zeros((1, D_MODEL), F32), h_bwd[1:][::-1]], axis=0)
    k = k / jnp.sum(jnp.abs(k), axis=0, keepdims=True)
    return jnp.fft.rfft(k, axis=0)


def bidir_long_conv(u, spec):
    L = u.shape[1]
    U = jnp.fft.rfft(u.astype(F32), n=2 * L, axis=1)
    return jnp.fft.irfft(U * spec[None], n=2 * L, axis=1)[:, :L]


def hyena_mixer(x, w_in, b_in, conv_w, conv_b, f_w1, f_b1, f_w2, f_b2, f_w3, f_b3, f_freq, decay, skip, w_out, b_out):
    L = x.shape[1]
    u = conv3_centred(x @ w_in + b_in, conv_w, conv_b)
    x0, x1, v = jnp.split(u, 3, axis=-1)
    spec = hyena_filter_spectrum(L, f_w1, f_b1, f_w2, f_b2, f_w3, f_b3, f_freq, decay)
    v = v * x1
    v = bidir_long_conv(v, spec).astype(x.dtype) + v * skip
    return (v * x0) @ w_out + b_out


def axial_angles(L):
    rows = L // GRID_W
    r = jnp.repeat(jnp.arange(rows), GRID_W).astype(F32)
    c = jnp.tile(jnp.arange(GRID_W), rows).astype(F32)
    nf = GA_HEAD_DIM // 4
    inv = AXIAL_THETA ** (-(2.0 * jnp.arange(nf, dtype=F32)) / (2 * nf))
    return r[:, None] * inv[None], c[:, None] * inv[None]


def gqa_axial_mixer(x, w_qkv, q_gain, k_gain, w_o):
    B, L, _ = x.shape
    qkv = x @ w_qkv
    q, k, v = jnp.split(qkv, [GA_HEADS * GA_HEAD_DIM, (GA_HEADS + GA_KV_HEADS) * GA_HEAD_DIM], axis=-1)
    q = rms_norm(q.reshape(B, L, GA_HEADS, GA_HEAD_DIM), q_gain)
    k = rms_norm(k.reshape(B, L, GA_KV_HEADS, GA_HEAD_DIM), k_gain)
    v = v.reshape(B, L, GA_KV_HEADS, GA_HEAD_DIM)
    ang_r, ang_c = axial_angles(L)
    half = GA_HEAD_DIM // 2
    q = jnp.concatenate([rotate_pairs(q[..., :half], ang_r), rotate_pairs(q[..., half:], ang_c)], axis=-1)
    k = jnp.concatenate([rotate_pairs(k[..., :half], ang_r), rotate_pairs(k[..., half:], ang_c)], axis=-1)
    nb = L // Q_BLOCK
    qb = jnp.moveaxis(q.reshape(B, nb, Q_BLOCK, GA_KV_HEADS, GA_GROUP, GA_HEAD_DIM), 1, 0)
    scale = GA_HEAD_DIM ** -0.5

    def block(qblk):
        s = jnp.einsum('bqkgd,bskd->bkgqs', qblk, k, preferred_element_type=F32) * scale
        p = jax.nn.softmax(s, axis=-1).astype(v.dtype)
        return jnp.einsum('bkgqs,bskd->bqkgd', p, v)

    o = lax.map(block, qb)
    o = jnp.moveaxis(o, 0, 1).reshape(B, L, GA_HEADS * GA_HEAD_DIM)
    return o @ w_o


def headwise(x, w):
    B, L, _ = x.shape
    y = jnp.einsum('blnc,ncd->blnd', x.reshape(B, L, ML_N_BLOCKS, ML_QKV_BLOCK), w)
    return y.reshape(B, L, ML_INNER)


def mlstm_chunkwise(q, k, v, i_pre, f_pre):
    B, L, H, dh = q.shape
    nc = L // ML_CHUNK

    def chunks(a):
        return jnp.moveaxis(a.reshape((B, nc, ML_CHUNK) + a.shape[2:]), 1, 0)

    qc = chunks(q.astype(F32))
    kc = chunks(k.astype(F32) * (dh ** -0.5))
    vc = chunks(v.astype(F32))
    ic = chunks(i_pre)
    lfc = chunks(jax.nn.log_sigmoid(f_pre))
    tri = jnp.tril(jnp.ones((ML_CHUNK, ML_CHUNK), dtype=bool))

    def step(carry, xs):
        C, n, m = carry
        qj, kj, vj, ij, lfj = xs
        g = jnp.cumsum(lfj, axis=1).transpose(0, 2, 1)
        ig = ij.transpose(0, 2, 1)
        G = g[..., -1]
        dmat = jnp.where(tri, g[..., :, None] - g[..., None, :] + ig[..., None, :], -jnp.inf)
        inter = g + m[..., None]
        m_q = jnp.maximum(inter, jnp.max(dmat, axis=-1))
        a = jnp.einsum('bjhd,bshd->bhjs', qj, kj) * jnp.exp(dmat - m_q[..., None])
        w_int = jnp.exp(inter - m_q)
        num = (jnp.einsum('bhjs,bshd->bjhd', a, vj)
               + jnp.einsum('bjhd,bhde->bjhe', qj, C) * w_int.transpose(0, 2, 1)[..., None])
        den = jnp.sum(a, axis=-1) + w_int * jnp.einsum('bjhd,bhd->bhj', qj, n)
        den = jnp.maximum(jnp.abs(den), jnp.exp(-m_q))
        h = num / den.transpose(0, 2, 1)[..., None]
        a_s = G[..., None] - g + ig
        m_new = jnp.maximum(G + m, jnp.max(a_s, axis=-1))
        ws = jnp.exp(a_s - m_new[..., None]).transpose(0, 2, 1)[..., None]
        dec = jnp.exp(G + m - m_new)
        C = dec[..., None, None] * C + jnp.einsum('bshd,bshe->bhde', kj * ws, vj)
        n = dec[..., None] * n + jnp.sum(kj * ws, axis=1).reshape(B, H, dh)
        return (C, n, m_new), h

    init = (jnp.zeros((B, H, dh, dh), F32), jnp.zeros((B, H, dh), F32), jnp.zeros((B, H), F32))
    _, hs = lax.scan(step, init, (qc, kc, vc, ic, lfc))
    return jnp.moveaxis(hs, 0, 1).reshape(B, L, H, dh)


def mlstm_mixer(x, w_up, conv_w, conv_b, w_q, w_k, w_v, w_gate, b_gate, norm_gain, skip, w_down):
    B, L, _ = x.shape
    xm, z = jnp.split(x @ w_up, 2, axis=-1)
    xc = jax.nn.silu(conv3_centred(xm, conv_w, conv_b))
    q = headwise(xc, w_q)
    k = headwise(xc, w_k)
    v = headwise(xm, w_v)
    gp = (jnp.einsum('bli,dih->dblh', q, w_gate[:, 0], preferred_element_type=F32)
          + jnp.einsum('bli,dih->dblh', k, w_gate[:, 1], preferred_element_type=F32)
          + jnp.einsum('bli,dih->dblh', v, w_gate[:, 2], preferred_element_type=F32)
          + b_gate.astype(F32)[:, None, None, :])
    shp = (B, L, ML_HEADS, ML_HEAD_DIM)
    qh, kh, vh = q.reshape(shp), k.reshape(shp), v.reshape(shp)
    h_fwd = mlstm_chunkwise(qh, kh, vh, gp[0, ..., :ML_HEADS], gp[0, ..., ML_HEADS:])
    flip = lambda a: jnp.flip(a, axis=1)
    h_bwd = flip(mlstm_chunkwise(flip(qh), flip(kh), flip(vh),
                                 flip(gp[1, ..., :ML_HEADS]), flip(gp[1, ..., ML_HEADS:])))
    h = rms_norm((h_fwd + h_bwd).astype(x.dtype), norm_gain.reshape(ML_HEADS, ML_HEAD_DIM))
    h = h.reshape(B, L, ML_INNER) + skip * xc
    return (h * jax.nn.silu(z)) @ w_down


def band_attention(q, k, v, n):
    Bq, S, h, hd = q.shape
    nb = -(-S // Q_BLOCK)
    Sp = nb * Q_BLOCK
    pad = Sp - S
    W = Q_BLOCK + 2 * n
    qb = jnp.pad(q, ((0, 0), (0, pad), (0, 0), (0, 0))).reshape(Bq, nb, Q_BLOCK, h, hd)
    kp = jnp.pad(k, ((0, 0), (n, n + pad), (0, 0), (0, 0)))
    vp = jnp.pad(v, ((0, 0), (n, n + pad), (0, 0), (0, 0)))
    starts = jnp.arange(nb) * Q_BLOCK
    idx = starts[:, None] + jnp.arange(W)[None, :]
    kb, vb = kp[:, idx], vp[:, idx]
    s = jnp.einsum('bnqhd,bnwhd->bnhqw', qb, kb, preferred_element_type=F32) * (hd ** -0.5)
    key_pos = idx - n
    q_pos = starts[:, None] + jnp.arange(Q_BLOCK)[None, :]
    rel = key_pos[:, None, :] - q_pos[:, :, None]
    valid = (jnp.abs(rel) <= n) & (key_pos >= 0)[:, None, :] & (key_pos < S)[:, None, :]
    valid = valid | (q_pos >= S)[:, :, None]
    s = jnp.where(valid[None, :, None], s, -jnp.inf)
    lse = jax.nn.logsumexp(s, axis=-1)
    p = jnp.exp(s - lse[..., None]).astype(v.dtype)
    o = jnp.einsum('bnhqw,bnwhd->bnqhd', p, vb).reshape(Bq, Sp, h, hd)[:, :S]
    lse = lse.transpose(0, 1, 3, 2).reshape(Bq, Sp, h)[:, :S]
    return o, lse


def dilated_group_attention(q, k, v, window, dil):
    B, L, h, hd = q.shape
    n = window // (2 * dil)
    S = L // dil
    split = lambda a: a.reshape(B, S, dil, h, hd).transpose(0, 2, 1, 3, 4).reshape(B * dil, S, h, hd)
    o, lse = band_attention(split(q), split(k), split(v), n)
    o = o.reshape(B, dil, S, h, hd).transpose(0, 2, 1, 3, 4).reshape(B, L, h, hd)
    lse = lse.reshape(B, dil, S, h).transpose(0, 2, 1, 3).reshape(B, L, h)
    return o, lse


def dilated_mixer(x, w_qkv, q_gain, k_gain, w_o):
    B, L, _ = x.shape
    q, k, v = jnp.split(x @ w_qkv, 3, axis=-1)
    shp = (B, L, DA_HEADS, DA_HEAD_DIM)
    q = rms_norm(q.reshape(shp), q_gain)
    k = rms_norm(k.reshape(shp), k_gain)
    v = v.reshape(shp)
    inv = ROPE_THETA ** (-(2.0 * jnp.arange(ROPE_DIMS // 2, dtype=F32)) / ROPE_DIMS)
    ang = jnp.arange(L, dtype=F32)[:, None] * inv[None]
    q = jnp.concatenate([rotate_pairs(q[..., :ROPE_DIMS], ang), q[..., ROPE_DIMS:]], axis=-1)
    k = jnp.concatenate([rotate_pairs(k[..., :ROPE_DIMS], ang), k[..., ROPE_DIMS:]], axis=-1)
    outs, lses = [], []
    for gi, (window, dil) in enumerate(DA_GROUPS):
        sl = slice(gi * DA_HEADS_PER_GROUP, (gi + 1) * DA_HEADS_PER_GROUP)
        o, lse = dilated_group_attention(q[:, :, sl], k[:, :, sl], v[:, :, sl], window, dil)
        outs.append(o)
        lses.append(lse)
    alpha = jax.nn.softmax(jnp.stack(lses, axis=0), axis=0)
    o = jnp.sum(alpha.astype(x.dtype)[..., None] * jnp.stack(outs, axis=0), axis=0)
    return o.reshape(B, L, DA_HEADS_PER_GROUP * DA_HEAD_DIM) @ w_o


def expert_choice_moe(x, w_router, w1, w3, w2):
    B, L, D = x.shape
    N = B * L
    xf = x.reshape(N, D)
    aff = jax.nn.softmax(jnp.einsum('nd,de->ne', xf, w_router, preferred_element_type=F32), axis=-1)
    cap = EC_CAPACITY * N // N_EXPERTS
    gates, idx = lax.top_k(aff.T, cap)
    xe = xf[idx]
    hid = jax.nn.silu(jnp.einsum('ecd,edf->ecf', xe, w1)) * jnp.einsum('ecd,edf->ecf', xe, w3)
    ye = jnp.einsum('ecf,efd->ecd', hid, w2) * gates.astype(x.dtype)[..., None]
    out = jnp.zeros_like(xf).at[idx.reshape(-1)].add(ye.reshape(-1, D))
    return out.reshape(B, L, D)


def trunk(x, p):
    for i in range(DEPTH):
        mixer, j = i % N_MIXERS, i // N_MIXERS
        h = rms_norm(x, p['norm_gain'][i, 0])
        if mixer == 0:
            y = hyena_mixer(h, p['hy_w_in'][j], p['hy_b_in'][j], p['hy_conv_w'][j], p['hy_conv_b'][j],
                            p['hy_f_w1'][j], p['hy_f_b1'][j], p['hy_f_w2'][j], p['hy_f_b2'][j],
                            p['hy_f_w3'][j], p['hy_f_b3'][j], p['hy_f_freq'][j], p['hy_decay'][j],
                            p['hy_skip'][j], p['hy_w_out'][j], p['hy_b_out'][j])
        elif mixer == 1:
            y = gqa_axial_mixer(h, p['ga_w_qkv'][j], p['ga_q_gain'][j], p['ga_k_gain'][j], p['ga_w_o'][j])
        elif mixer == 2:
            y = mlstm_mixer(h, p['ml_w_up'][j], p['ml_conv_w'][j], p['ml_conv_b'][j], p['ml_w_q'][j],
                            p['ml_w_k'][j], p['ml_w_v'][j], p['ml_w_gate'][j], p['ml_b_gate'][j],
                            p['ml_norm_gain'][j], p['ml_skip'][j], p['ml_w_down'][j])
        else:
            y = dilated_mixer(h, p['da_w_qkv'][j], p['da_q_gain'][j], p['da_k_gain'][j], p['da_w_o'][j])
        x = x + y
        x = x + expert_choice_moe(rms_norm(x, p['norm_gain'][i, 1]), p['moe_w_router'][i],
                                  p['moe_w1'][i], p['moe_w3'][i], p['moe_w2'][i])
    return x


def setup_inputs(seed: int = 0) -> dict:
    key = jax.random.key(seed)
    ks = iter(jax.random.split(key, 48))
    D = D_MODEL

    def nrm(shape, scale):
        return jax.random.normal(next(ks), shape, F32) * scale

    def gain(shape):
        return 1.0 + nrm(shape, 0.01)

    nA, nB, nC, nD = N_HY_LAYERS, N_GA_LAYERS, N_ML_LAYERS, N_DA_LAYERS
    min_dec = abs(math.log(HY_DECAY_TARGET) / HY_SLOW_DECAY_PCT)
    max_dec = abs(math.log(HY_DECAY_TARGET) / HY_FAST_DECAY_PCT)
    dec_base = jnp.linspace(min_dec, max_dec, D, dtype=F32)
    f_bias = jnp.linspace(3.0, 6.0, ML_HEADS, dtype=F32)
    inp = {}
    inp['x_prompt'] = nrm((BATCH, SEQ, D), 1.0)
    inp['x_sample'] = nrm((DEC_BATCH, DEC_SEQ, D), 1.0)
    inp['norm_gain'] = gain((DEPTH, 2, D))
    inp['hy_w_in'] = nrm((nA, D, 3 * D), D ** -0.5)
    inp['hy_b_in'] = nrm((nA, 3 * D), 0.01)
    inp['hy_conv_w'] = nrm((nA, 3, 3 * D), 3 ** -0.5)
    inp['hy_conv_b'] = nrm((nA, 3 * D), 0.01)
    inp['hy_f_w1'] = nrm((nA, HY_EMB, HY_FILTER_HIDDEN), HY_EMB ** -0.5)
    inp['hy_f_b1'] = nrm((nA, HY_FILTER_HIDDEN), 0.1)
    inp['hy_f_w2'] = nrm((nA, HY_FILTER_HIDDEN, HY_FILTER_HIDDEN), HY_FILTER_HIDDEN ** -0.5)
    inp['hy_f_b2'] = nrm((nA, HY_FILTER_HIDDEN), 0.1)
    inp['hy_f_w3'] = nrm((nA, HY_FILTER_HIDDEN, 2 * D), HY_FILTER_HIDDEN ** -0.5)
    inp['hy_f_b3'] = nrm((nA, 2 * D), 0.1)
    inp['hy_f_freq'] = 1.0 + nrm((nA, 2, HY_FILTER_HIDDEN), 0.01)
    inp['hy_decay'] = dec_base * (1.0 + nrm((nA, 2, D), 0.05))
    inp['hy_skip'] = nrm((nA, D), 1.0)
    inp['hy_w_out'] = nrm((nA, D, D), D ** -0.5)
    inp['hy_b_out'] = nrm((nA, D), 0.01)
    inp['ga_w_qkv'] = nrm((nB, D, (GA_HEADS + 2 * GA_KV_HEADS) * GA_HEAD_DIM), D ** -0.5)
    inp['ga_q_gain'] = gain((nB, GA_HEAD_DIM))
    inp['ga_k_gain'] = gain((nB, GA_HEAD_DIM))
    inp['ga_w_o'] = nrm((nB, GA_HEADS * GA_HEAD_DIM, D), (GA_HEADS * GA_HEAD_DIM) ** -0.5)
    inp['ml_w_up'] = nrm((nC, D, 2 * ML_INNER), D ** -0.5)
    inp['ml_conv_w'] = nrm((nC, 3, ML_INNER), 3 ** -0.5)
    inp['ml_conv_b'] = nrm((nC, ML_INNER), 0.01)
    inp['ml_w_q'] = nrm((nC, ML_N_BLOCKS, ML_QKV_BLOCK, ML_QKV_BLOCK), ML_QKV_BLOCK ** -0.5)
    inp['ml_w_k'] = nrm((nC, ML_N_BLOCKS, ML_QKV_BLOCK, ML_QKV_BLOCK), ML_QKV_BLOCK ** -0.5)
    inp['ml_w_v'] = nrm((nC, ML_N_BLOCKS, ML_QKV_BLOCK, ML_QKV_BLOCK), ML_QKV_BLOCK ** -0.5)
    inp['ml_w_gate'] = nrm((nC, 2, 3, ML_INNER, 2 * ML_HEADS), 0.3 * (3 * ML_INNER) ** -0.5)
    inp['ml_b_gate'] = jnp.concatenate([nrm((nC, 2, ML_HEADS), 0.1),
                                        f_bias + nrm((nC, 2, ML_HEADS), 0.1)], axis=-1)
    inp['ml_norm_gain'] = gain((nC, ML_INNER))
    inp['ml_skip'] = gain((nC, ML_INNER))
    inp['ml_w_down'] = nrm((nC, ML_INNER, D), ML_INNER ** -0.5)
    inp['da_w_qkv'] = nrm((nD, D, 3 * DA_HEADS * DA_HEAD_DIM), D ** -0.5)
    inp['da_q_gain'] = gain((nD, DA_HEAD_DIM))
    inp['da_k_gain'] = gain((nD, DA_HEAD_DIM))
    inp['da_w_o'] = nrm((nD, DA_HEADS_PER_GROUP * DA_HEAD_DIM, D), (DA_HEADS_PER_GROUP * DA_HEAD_DIM) ** -0.5)
    inp['moe_w_router'] = nrm((DEPTH, D, N_EXPERTS), D ** -0.5)
    inp['moe_w1'] = nrm((DEPTH, N_EXPERTS, D, EXPERT_FF), D ** -0.5)
    inp['moe_w3'] = nrm((DEPTH, N_EXPERTS, D, EXPERT_FF), D ** -0.5)
    inp['moe_w2'] = nrm((DEPTH, N_EXPERTS, EXPERT_FF, D), EXPERT_FF ** -0.5)
    return inp


def reference(x_prompt, x_sample, norm_gain,
              hy_w_in, hy_b_in, hy_conv_w, hy_conv_b, hy_f_w1, hy_f_b1, hy_f_w2, hy_f_b2,
              hy_f_w3, hy_f_b3, hy_f_freq, hy_decay, hy_skip, hy_w_out, hy_b_out,
              ga_w_qkv, ga_q_gain, ga_k_gain, ga_w_o,
              ml_w_up, ml_conv_w, ml_conv_b, ml_w_q, ml_w_k, ml_w_v, ml_w_gate, ml_b_gate,
              ml_norm_gain, ml_skip, ml_w_down,
              da_w_qkv, da_q_gain, da_k_gain, da_w_o,
              moe_w_router, moe_w1, moe_w3, moe_w2):
    p = {
        'norm_gain': norm_gain,
        'hy_w_in': hy_w_in, 'hy_b_in': hy_b_in, 'hy_conv_w': hy_conv_w, 'hy_conv_b': hy_conv_b,
        'hy_f_w1': hy_f_w1, 'hy_f_b1': hy_f_b1, 'hy_f_w2': hy_f_w2, 'hy_f_b2': hy_f_b2,
        'hy_f_w3': hy_f_w3, 'hy_f_b3': hy_f_b3, 'hy_f_freq': hy_f_freq, 'hy_decay': hy_decay,
        'hy_skip': hy_skip, 'hy_w_out': hy_w_out, 'hy_b_out': hy_b_out,
        'ga_w_qkv': ga_w_qkv, 'ga_q_gain': ga_q_gain, 'ga_k_gain': ga_k_gain, 'ga_w_o': ga_w_o,
        'ml_w_up': ml_w_up, 'ml_conv_w': ml_conv_w, 'ml_conv_b': ml_conv_b, 'ml_w_q': ml_w_q,
        'ml_w_k': ml_w_k, 'ml_w_v': ml_w_v, 'ml_w_gate': ml_w_gate, 'ml_b_gate': ml_b_gate,
        'ml_norm_gain': ml_norm_gain, 'ml_skip': ml_skip, 'ml_w_down': ml_w_down,
        'da_w_qkv': da_w_qkv, 'da_q_gain': da_q_gain, 'da_k_gain': da_k_gain, 'da_w_o': da_w_o,
        'moe_w_router': moe_w_router, 'moe_w1': moe_w1, 'moe_w3': moe_w3, 'moe_w2': moe_w2,
    }
    y_prompt = trunk(x_prompt, p)
    y_sample = trunk(x_sample, p)
    return (y_prompt, y_sample)
```

```python
import functools
import math

import jax
import jax.numpy as jnp
from jax import lax
from jax.experimental import pallas as pl
from jax.experimental.pallas import tpu as pltpu

F32 = jnp.float32
BF16 = jnp.bfloat16
HIGHEST = lax.Precision.HIGHEST

NORM_EPS = 1e-6
GRID_W = 64
HY_BANDS = 16
GA_HEADS = 8
GA_KV_HEADS = 2
GA_GROUP = GA_HEADS // GA_KV_HEADS
HEAD_DIM = 128
AXIAL_THETA = 10000.0
ML_HEADS = 4
ML_QKV_BLOCK = 4
DA_GROUPS = ((128, 1), (512, 4), (2048, 16))
DA_HEADS_PER_GROUP = 4
DA_HEADS = DA_HEADS_PER_GROUP * len(DA_GROUPS)
ROPE_THETA = 500000.0
ROPE_DIMS = HEAD_DIM // 4
N_EXPERTS = 16
EC_CAPACITY = 2

VMEM_LIMIT_BYTES = 52 * 1024 * 1024
HALO_ROWS = 8
MXU_DIM = 256
ML_CHUNK = 256
BAND_BLOCK = 128
BAND_HALF = 64


def _params(*sem):
    return pltpu.CompilerParams(dimension_semantics=sem, vmem_limit_bytes=VMEM_LIMIT_BYTES)


def _rms(x, g):
    ms = jnp.mean(x * x, axis=-1, keepdims=True)
    return x * lax.rsqrt(ms + NORM_EPS) * g


def _const_spec(shape):
    nd = len(shape)
    return pl.BlockSpec(shape, lambda *_: (0,) * nd)


def _conv3_rows(p, pprev, pnext, cw, cb, rows, tm):
    up = jnp.where(rows == 0, pprev, pltpu.roll(p, 1, 0))
    dn = jnp.where(rows == tm - 1, pnext, pltpu.roll(p, tm - 1, 0))
    return up * cw[0:1] + p * cw[1:2] + dn * cw[2:3] + cb


def _halo_specs(tm, d, n_rows):
    hb = tm // HALO_ROWS
    last = n_rows // HALO_ROWS - 1
    prev = pl.BlockSpec((HALO_ROWS, d), lambda i: (jnp.maximum(i * hb - 1, 0), 0))
    nxt = pl.BlockSpec((HALO_ROWS, d), lambda i: (jnp.minimum((i + 1) * hb, last), 0))
    return prev, nxt


def _edge_scales(i, tm, seq_len):
    t0 = i * tm
    keep_prev = jnp.where(t0 % seq_len == 0, 0.0, 1.0).astype(F32)
    keep_next = jnp.where((t0 + tm) % seq_len == 0, 0.0, 1.0).astype(F32)
    return keep_prev, keep_next


def _mm_res_kernel(*refs, prologue, n_row, n_const, tn):
    row_refs = refs[:n_row]
    const_refs = refs[n_row:n_row + n_const]
    w_ref, b_ref, res_ref, o_ref = refs[n_row + n_const:]
    lhs = prologue(*[r[...] for r in row_refs], *[c[...] for c in const_refs]).astype(BF16)
    for j in range(o_ref.shape[1] // tn):
        sl = slice(j * tn, (j + 1) * tn)
        o_ref[:, sl] = (res_ref[:, sl] + b_ref[:, sl]
                        + jnp.dot(lhs, w_ref[:, sl], preferred_element_type=F32))


def _mm_res(prologue, rows, consts, w, b, res, tm=512, tn=512):
    n, dout = res.shape
    tm = min(tm, n)
    k = w.shape[0]
    if b is None:
        b = jnp.zeros((1, dout), F32)
    in_specs = [pl.BlockSpec((tm, r.shape[1]), lambda i: (i, 0)) for r in rows]
    in_specs += [_const_spec(c.shape) for c in consts]
    in_specs += [_const_spec((k, dout)), _const_spec((1, dout)),
                 pl.BlockSpec((tm, dout), lambda i: (i, 0))]
    return pl.pallas_call(
        functools.partial(_mm_res_kernel, prologue=prologue, n_row=len(rows),
                          n_const=len(consts), tn=min(tn, dout)),
        out_shape=jax.ShapeDtypeStruct((n, dout), F32),
        grid=(n // tm,),
        in_specs=in_specs,
        out_specs=pl.BlockSpec((tm, dout), lambda i: (i, 0)),
        compiler_params=_params("parallel"),
    )(*rows, *consts, w.astype(BF16), b.reshape(1, dout).astype(F32), res)


def _hyena_in_kernel(x_ref, xp_ref, xn_ref, g_ref, w_ref, b_ref, cw_ref, cb_ref,
                     x0_ref, vx_ref, *, seq_len, tm, d, cols):
    keep_prev, keep_next = _edge_scales(pl.program_id(0), tm, seq_len)
    g = g_ref[...]
    xb = _rms(x_ref[...], g).astype(BF16)
    hb = _rms(jnp.concatenate([xp_ref[...], xn_ref[...]], axis=0), g).astype(BF16)
    rows = lax.broadcasted_iota(jnp.int32, (tm, 1), 0)

    def conv_part(c0):
        sl = slice(c0, c0 + cols)
        w = w_ref[:, sl]
        bias = b_ref[:, sl]
        p = jnp.dot(xb, w, preferred_element_type=F32) + bias
        ph = jnp.dot(hb, w, preferred_element_type=F32) + bias
        pprev = ph[HALO_ROWS - 1:HALO_ROWS, :] * keep_prev
        pnext = ph[HALO_ROWS:HALO_ROWS + 1, :] * keep_next
        return _conv3_rows(p, pprev, pnext, cw_ref[:, sl], cb_ref[:, sl], rows, tm)

    for j in range(d // cols):
        c = j * cols
        x0_ref[:, c:c + cols] = conv_part(c)
        vx_ref[:, c:c + cols] = conv_part(2 * d + c) * conv_part(d + c)


def _hyena_in(x, g, w_in, b_in, conv_w, conv_b, seq_len, tm=256, cols=512):
    n, d = x.shape
    tm = min(tm, seq_len)
    prev, nxt = _halo_specs(tm, d, n)
    row = pl.BlockSpec((tm, d), lambda i: (i, 0))
    return pl.pallas_call(
        functools.partial(_hyena_in_kernel, seq_len=seq_len, tm=tm, d=d, cols=cols),
        out_shape=(jax.ShapeDtypeStruct((n, d), F32), jax.ShapeDtypeStruct((n, d), F32)),
        grid=(n // tm,),
        in_specs=[row, prev, nxt, _const_spec((1, d)), _const_spec((d, 3 * d)),
                  _const_spec((1, 3 * d)), _const_spec((3, 3 * d)), _const_spec((1, 3 * d))],
        out_specs=(row, row),
        compiler_params=_params("parallel"),
    )(x, x, x, g.reshape(1, d), w_in.astype(BF16), b_in.reshape(1, 3 * d),
      conv_w, conv_b.reshape(1, 3 * d))


def _hyena_filter_spectrum(L, d, f_w1, f_b1, f_w2, f_b2, f_w3, f_b3, f_freq, decay):
    t = jnp.linspace(0.0, 1.0, L, dtype=F32)[:, None]
    w_ang = 2.0 * math.pi * jnp.arange(L, dtype=F32)[:, None] / L
    bands = jnp.linspace(1e-4, HY_BANDS - 1, HY_BANDS, dtype=F32)[None, :]
    z = jnp.concatenate([t, jnp.cos(bands * w_ang), -jnp.sin(bands * w_ang)], axis=-1)
    h = jnp.sin(f_freq[0] * (z @ f_w1 + f_b1))
    h = jnp.sin(f_freq[1] * (h @ f_w2 + f_b2))
    h = h @ f_w3 + f_b3
    h = h.reshape(L, 2, d) * jnp.exp(-t[:, :, None] * jnp.abs(decay)[None])
    k = jnp.concatenate([h[:, 0], jnp.zeros((1, d), F32), h[1:, 1][::-1]], axis=0)
    k = k / jnp.sum(jnp.abs(k), axis=0, keepdims=True)
    return jnp.fft.rfft(k, axis=0)


def _hyena_out_prologue(y, vx, x0, skip):
    return (y + vx * skip) * x0


def _hyena_layer(x, batch, seq_len, g, p):
    n, d = x.shape
    x0, vx = _hyena_in(x, g, p['w_in'], p['b_in'], p['conv_w'], p['conv_b'], seq_len)
    spec = _hyena_filter_spectrum(seq_len, d, p['f_w1'], p['f_b1'], p['f_w2'], p['f_b2'],
                                  p['f_w3'], p['f_b3'], p['f_freq'], p['decay'])
    vf = jnp.fft.rfft(vx.reshape(batch, seq_len, d), n=2 * seq_len, axis=1)
    y = jnp.fft.irfft(vf * spec[None], n=2 * seq_len, axis=1)[:, :seq_len].reshape(n, d)
    return _mm_res(_hyena_out_prologue, [y, vx, x0], [p['skip'].reshape(1, d)],
                   p['w_out'], p['b_out'], x)


def _head_norm_rope(xh, gain, cos, sin, half):
    y = _rms(xh, gain)
    lane = lax.broadcasted_iota(jnp.int32, (1, HEAD_DIM), 1)
    fwd = pltpu.roll(y, HEAD_DIM - half, 1)
    bwd = pltpu.roll(y, half, 1)
    partner = jnp.where((lane % (2 * half)) < half, fwd, bwd)
    return y * cos + partner * sin


def _qkv_rope_kernel(x_ref, g_ref, w_ref, qg_ref, kg_ref, cos_ref, sin_ref,
                     q_ref, k_ref, v_ref, *, nq, nk, nv, half):
    xb = _rms(x_ref[...], g_ref[...]).astype(BF16)
    cos = cos_ref[...]
    sin = sin_ref[...]
    per = MXU_DIM // HEAD_DIM
    for h0 in range(0, nq + nk + nv, per):
        pw = jnp.dot(xb, w_ref[:, h0 * HEAD_DIM:(h0 + per) * HEAD_DIM], preferred_element_type=F32)
        for h in range(h0, h0 + per):
            ph = pw[:, (h - h0) * HEAD_DIM:(h - h0 + 1) * HEAD_DIM]
            if h < nq:
                sl = slice(h * HEAD_DIM, (h + 1) * HEAD_DIM)
                q_ref[:, sl] = _head_norm_rope(ph, qg_ref[...], cos, sin, half).astype(BF16)
            elif h < nq + nk:
                sl = slice((h - nq) * HEAD_DIM, (h - nq + 1) * HEAD_DIM)
                k_ref[:, sl] = _head_norm_rope(ph, kg_ref[...], cos, sin, half).astype(BF16)
            else:
                sl = slice((h - nq - nk) * HEAD_DIM, (h - nq - nk + 1) * HEAD_DIM)
                v_ref[:, sl] = ph.astype(BF16)


def _qkv_rope(x, g, w_qkv, q_gain, k_gain, cos, sin, nq, nk, nv, half, seq_len, tm=512):
    n, d = x.shape
    tm = min(tm, seq_len)
    pos_blocks = seq_len // tm
    f = w_qkv.shape[1]
    row = pl.BlockSpec((tm, d), lambda i: (i, 0))
    tab = pl.BlockSpec((tm, HEAD_DIM), lambda i: (i % pos_blocks, 0))
    outs = tuple(jax.ShapeDtypeStruct((n, c * HEAD_DIM), BF16) for c in (nq, nk, nv))
    return pl.pallas_call(
        functools.partial(_qkv_rope_kernel, nq=nq, nk=nk, nv=nv, half=half),
        out_shape=outs,
        grid=(n // tm,),
        in_specs=[row, _const_spec((1, d)), _const_spec((d, f)),
                  _const_spec((1, HEAD_DIM)), _const_spec((1, HEAD_DIM)), tab, tab],
        out_specs=tuple(pl.BlockSpec((tm, c * HEAD_DIM), lambda i: (i, 0)) for c in (nq, nk, nv)),
        compiler_params=_params("parallel"),
    )(x, g.reshape(1, d), w_qkv.astype(BF16), q_gain.reshape(1, HEAD_DIM),
      k_gain.reshape(1, HEAD_DIM), cos, sin)


def _axial_tables(L):
    t = jnp.arange(L)
    r = (t // GRID_W).astype(F32)
    c = (t % GRID_W).astype(F32)
    nf = HEAD_DIM // 4
    inv = AXIAL_THETA ** (-(2.0 * jnp.arange(nf, dtype=F32)) / (2 * nf))
    ar, ac = r[:, None] * inv[None], c[:, None] * inv[None]
    cos = jnp.concatenate([jnp.cos(ar), jnp.cos(ar), jnp.cos(ac), jnp.cos(ac)], axis=-1)
    sin = jnp.concatenate([-jnp.sin(ar), jnp.sin(ar), -jnp.sin(ac), jnp.sin(ac)], axis=-1)
    return cos, sin


def _flash_kernel(q_ref, k_ref, v_ref, o_ref, *, tq, tk, seq_len, group):
    c = (HEAD_DIM ** -0.5) * math.log2(math.e)
    for h in range(group):
        sl = slice(h * HEAD_DIM, (h + 1) * HEAD_DIM)
        q = q_ref[:, sl]

        def body(j, carry):
            m, l, acc = carry
            start = pl.multiple_of(j * tk, tk)
            kt = k_ref[pl.ds(start, tk), :]
            vt = v_ref[pl.ds(start, tk), :]
            s = lax.dot_general(q, kt, (((1,), (1,)), ((), ())), preferred_element_type=F32)
            m_new = jnp.maximum(m, jnp.max(s, axis=1, keepdims=True))
            alpha = jnp.exp2((m - m_new) * c)
            p = jnp.exp2(s * c - m_new * c)
            l = alpha * l + jnp.sum(p, axis=1, keepdims=True)
            acc = alpha * acc + jnp.dot(p.astype(BF16), vt, preferred_element_type=F32)
            return m_new, l, acc

        init = (jnp.full((tq, 1), -jnp.inf, F32), jnp.zeros((tq, 1), F32),
                jnp.zeros((tq, HEAD_DIM), F32))
        m, l, acc = lax.fori_loop(0, seq_len // tk, body, init)
        o_ref[:, sl] = (acc / l).astype(o_ref.dtype)


def _flash_gqa(q, k, v, batch, seq_len, tq=256, tk=512):
    tq = min(tq, seq_len)
    tk = min(tk, seq_len)
    nq = seq_len // tq
    gw = GA_GROUP * HEAD_DIM
    return pl.pallas_call(
        functools.partial(_flash_kernel, tq=tq, tk=tk, seq_len=seq_len, group=GA_GROUP),
        out_shape=jax.ShapeDtypeStruct(q.shape, BF16),
        grid=(batch, GA_KV_HEADS, nq),
        in_specs=[pl.BlockSpec((tq, gw), lambda b, kv, i: (b * nq + i, kv)),
                  pl.BlockSpec((seq_len, HEAD_DIM), lambda b, kv, i: (b, kv)),
                  pl.BlockSpec((seq_len, HEAD_DIM), lambda b, kv, i: (b, kv))],
        out_specs=pl.BlockSpec((tq, gw), lambda b, kv, i: (b * nq + i, kv)),
        compiler_params=_params("parallel", "parallel", "parallel"),
    )(q, k, v)


def _identity_prologue(o):
    return o


def _gqa_layer(x, batch, seq_len, g, p):
    cos, sin = _axial_tables(seq_len)
    q, k, v = _qkv_rope(x, g, p['w_qkv'], p['q_gain'], p['k_gain'], cos, sin,
                        GA_HEADS, GA_KV_HEADS, GA_KV_HEADS, HEAD_DIM // 4, seq_len)
    o = _flash_gqa(q, k, v, batch, seq_len)
    return _mm_res(_identity_prologue, [o], [], p['w_o'], None, x)


def _ml_in_kernel(x_ref, xp_ref, xn_ref, g_ref, w_ref, cw_ref, cb_ref, wq_ref, wk_ref, wv_ref,
                  wg_ref, bg_ref, q_ref, k_ref, v_ref, xc_ref, sz_ref, gate_ref,
                  *, seq_len, tm, inner, k_scale):
    keep_prev, keep_next = _edge_scales(pl.program_id(0), tm, seq_len)
    g = g_ref[...]
    xb = _rms(x_ref[...], g).astype(BF16)
    hb = _rms(jnp.concatenate([xp_ref[...], xn_ref[...]], axis=0), g).astype(BF16)
    rows = lax.broadcasted_iota(jnp.int32, (tm, 1), 0)
    gacc = jnp.zeros(gate_ref.shape, F32)
    for t in range(inner // MXU_DIM):
        sl = slice(t * MXU_DIM, (t + 1) * MXU_DIM)
        w = w_ref[:, sl]
        xm = jnp.dot(xb, w, preferred_element_type=F32)
        xh = jnp.dot(hb, w, preferred_element_type=F32)
        pprev = xh[HALO_ROWS - 1:HALO_ROWS, :] * keep_prev
        pnext = xh[HALO_ROWS:HALO_ROWS + 1, :] * keep_next
        xc = _conv3_rows(xm, pprev, pnext, cw_ref[:, sl], cb_ref[:, sl], rows, tm)
        xc = xc * jax.nn.sigmoid(xc)
        z = jnp.dot(xb, w_ref[:, inner + t * MXU_DIM:inner + (t + 1) * MXU_DIM],
                    preferred_element_type=F32)
        xcb = xc.astype(BF16)
        q = jnp.dot(xcb, wq_ref[t], preferred_element_type=F32)
        k = jnp.dot(xcb, wk_ref[t], preferred_element_type=F32)
        v = jnp.dot(xm.astype(BF16), wv_ref[t], preferred_element_type=F32)
        qb, kb, vb = q.astype(BF16), k.astype(BF16), v.astype(BF16)
        gacc += (jnp.dot(qb, wg_ref[0, sl, :], preferred_element_type=F32)
                 + jnp.dot(kb, wg_ref[1, sl, :], preferred_element_type=F32)
                 + jnp.dot(vb, wg_ref[2, sl, :], preferred_element_type=F32))
        q_ref[:, sl] = qb
        k_ref[:, sl] = (k * k_scale).astype(BF16)
        v_ref[:, sl] = vb
        xc_ref[:, sl] = xc
        sz_ref[:, sl] = z * jax.nn.sigmoid(z)
    gate_ref[...] = gacc + bg_ref[...]


def _block_diag_tiles(w):
    nb, c, _ = w.shape
    per = MXU_DIM // c
    wt = w.reshape(nb // per, per, c, c)
    eye = jnp.eye(per, dtype=w.dtype)
    full = jnp.einsum('tpcd,pq->tpcqd', wt, eye)
    return full.reshape(nb // per, MXU_DIM, MXU_DIM)


def _ml_in(x, g, p, seq_len, tm=256):
    n, d = x.shape
    tm = min(tm, seq_len)
    inner = p['w_up'].shape[1] // 2
    ng = 4 * ML_HEADS
    dh = inner // ML_HEADS
    wq, wk, wv = (_block_diag_tiles(p[nm]).astype(BF16) for nm in ('w_q', 'w_k', 'w_v'))
    wg = jnp.transpose(p['w_gate'], (1, 2, 0, 3)).reshape(3, inner, ng).astype(BF16)
    bg = p['b_gate'].reshape(1, ng)
    prev, nxt = _halo_specs(tm, d, n)
    row = pl.BlockSpec((tm, d), lambda i: (i, 0))
    wide = pl.BlockSpec((tm, inner), lambda i: (i, 0))
    nt = inner // MXU_DIM
    return pl.pallas_call(
        functools.partial(_ml_in_kernel, seq_len=seq_len, tm=tm, inner=inner, k_scale=dh ** -0.5),
        out_shape=(jax.ShapeDtypeStruct((n, inner), BF16),) * 3
        + (jax.ShapeDtypeStruct((n, inner), F32),) * 2
        + (jax.ShapeDtypeStruct((n, ng), F32),),
        grid=(n // tm,),
        in_specs=[row, prev, nxt, _const_spec((1, d)), _const_spec((d, 2 * inner)),
                  _const_spec((3, inner)), _const_spec((1, inner)),
                  _const_spec((nt, MXU_DIM, MXU_DIM)), _const_spec((nt, MXU_DIM, MXU_DIM)),
                  _const_spec((nt, MXU_DIM, MXU_DIM)), _const_spec((3, inner, ng)),
                  _const_spec((1, ng))],
        out_specs=(wide,) * 5 + (pl.BlockSpec((tm, ng), lambda i: (i, 0)),),
        compiler_params=_params("parallel"),
    )(x, x, x, g.reshape(1, d), p['w_up'].astype(BF16), p['conv_w'],
      p['conv_b'].reshape(1, inner), wq, wk, wv, wg, bg)


def _log_sigmoid(x):
    return jnp.minimum(x, 0.0) - jnp.log1p(jnp.exp(-jnp.abs(x)))


def _mlstm_chunk_kernel(q_ref, k_ref, v_ref, gc_ref, gr_ref, h_ref, c_scr, n_scr, m_scr, *, lc):
    d = pl.program_id(2)

    @pl.when(pl.program_id(3) == 0)
    def _():
        c_scr[...] = jnp.zeros_like(c_scr)
        n_scr[...] = jnp.zeros_like(n_scr)
        m_scr[...] = jnp.zeros_like(m_scr)

    q = q_ref[...]
    k = k_ref[...]
    v = v_ref[...]
    gc = gc_ref[...]
    gr = gr_ref[...]
    i_col, i_row = gc[:, 0:1], gr[0:1, :]
    lf_col = _log_sigmoid(gc[:, 1:2])
    lf_row = _log_sigmoid(gr[1:2, :])

    jr = lax.broadcasted_iota(jnp.int32, (lc, lc), 0)
    sc = lax.broadcasted_iota(jnp.int32, (lc, lc), 1)
    sgn = 1 - 2 * d
    seen = ((sc - jr) * sgn) <= 0
    seen_t = ((jr - sc) * sgn) <= 0
    g_col = jnp.dot(seen.astype(F32), jnp.broadcast_to(lf_col, (lc, HEAD_DIM)),
                    precision=HIGHEST, preferred_element_type=F32)[:, 0:1]
    g_row = jnp.dot(jnp.broadcast_to(lf_row, (HALO_ROWS, lc)), seen_t.astype(F32),
                    precision=HIGHEST, preferred_element_type=F32)[0:1, :]
    g_tot = jnp.sum(lf_row, axis=1, keepdims=True)
    m_old = m_scr[0:1, 0:1]

    dmat = jnp.where(seen, g_col - g_row + i_row, -jnp.inf)
    inter = g_col + m_old
    m_q = jnp.maximum(inter, jnp.max(dmat, axis=1, keepdims=True))
    s_qk = lax.dot_general(q, k, (((1,), (1,)), ((), ())), preferred_element_type=F32)
    a = s_qk * jnp.exp(dmat - m_q)
    w_int = jnp.exp(inter - m_q)
    q_c = jnp.dot(q, c_scr[...].astype(BF16), preferred_element_type=F32)
    num = jnp.dot(a.astype(BF16), v, preferred_element_type=F32) + q_c * w_int
    q_n = jnp.sum(q.astype(F32) * n_scr[...], axis=1, keepdims=True)
    den = jnp.sum(a, axis=1, keepdims=True) + w_int * q_n
    den = jnp.maximum(jnp.abs(den), jnp.exp(-m_q))
    h_ref[...] = num / den

    a_row = g_tot - g_row + i_row
    m_new = jnp.maximum(g_tot + m_old, jnp.max(a_row, axis=1, keepdims=True))
    ws_col = jnp.exp(g_tot - g_col + i_col - m_new)
    dec = jnp.exp(g_tot + m_old - m_new)
    kw = k.astype(F32) * ws_col
    upd = lax.dot_general(kw.astype(BF16), v, (((0,), (0,)), ((), ())), preferred_element_type=F32)
    c_scr[...] = dec * c_scr[...] + upd
    n_scr[...] = dec * n_scr[...] + jnp.sum(kw, axis=0, keepdims=True)
    m_scr[...] = jnp.broadcast_to(m_new, m_scr.shape)


def _mlstm_chunks(q, k, v, gates, batch, seq_len):
    n, inner = q.shape
    dh = inner // ML_HEADS
    lc = min(ML_CHUNK, seq_len)
    nc = seq_len // lc
    g4 = gates.reshape(n, 2, 2, ML_HEADS)
    gcol = jnp.transpose(g4, (1, 3, 0, 2))
    grow = jnp.transpose(g4, (1, 3, 2, 0))

    def chunk(b, c, dd):
        return b * nc + c + dd * (nc - 1 - 2 * c)

    qkv_spec = pl.BlockSpec((lc, dh), lambda b, h, dd, c: (chunk(b, c, dd), h))
    return pl.pallas_call(
        functools.partial(_mlstm_chunk_kernel, lc=lc),
        out_shape=jax.ShapeDtypeStruct((2, n, inner), F32),
        grid=(batch, ML_HEADS, 2, nc),
        in_specs=[qkv_spec, qkv_spec, qkv_spec,
                  pl.BlockSpec((None, None, lc, 2), lambda b, h, dd, c: (dd, h, chunk(b, c, dd), 0)),
                  pl.BlockSpec((None, None, 2, lc), lambda b, h, dd, c: (dd, h, 0, chunk(b, c, dd)))],
        out_specs=pl.BlockSpec((None, lc, dh), lambda b, h, dd, c: (dd, chunk(b, c, dd), h)),
        scratch_shapes=[pltpu.VMEM((dh, dh), F32), pltpu.VMEM((1, dh), F32),
                        pltpu.VMEM((HALO_ROWS, HEAD_DIM), F32)],
        compiler_params=_params("parallel", "parallel", "parallel", "arbitrary"),
    )(q, k, v, gcol, grow)


def _ml_out_prologue(hf, hb, xc, sz, gain, skip):
    h = hf + hb
    dh = h.shape[1] // ML_HEADS
    parts = []
    for i in range(ML_HEADS):
        sl = slice(i * dh, (i + 1) * dh)
        parts.append(_rms(h[:, sl], gain[:, sl]))
    hn = jnp.concatenate(parts, axis=1)
    return (hn + skip * xc) * sz


def _mlstm_layer(x, batch, seq_len, g, p):
    q, k, v, xc, sz, gates = _ml_in(x, g, p, seq_len)
    inner = q.shape[1]
    hs = _mlstm_chunks(q, k, v, gates, batch, seq_len)
    return _mm_res(_ml_out_prologue, [hs[0], hs[1], xc, sz],
                   [p['norm_gain'].reshape(1, inner), p['skip'].reshape(1, inner)],
                   p['w_down'], None, x, tm=256)


def _rope_tables(L):
    inv = ROPE_THETA ** (-(2.0 * jnp.arange(ROPE_DIMS // 2, dtype=F32)) / ROPE_DIMS)
    ang = jnp.arange(L, dtype=F32)[:, None] * inv[None]
    pad = HEAD_DIM - ROPE_DIMS
    cos = jnp.concatenate([jnp.cos(ang), jnp.cos(ang), jnp.ones((L, pad), F32)], axis=-1)
    sin = jnp.concatenate([-jnp.sin(ang), jnp.sin(ang), jnp.zeros((L, pad), F32)], axis=-1)
    return cos, sin


def _band_kernel(q_ref, kp_ref, kc_ref, kn_ref, vp_ref, vc_ref, vn_ref, o_ref, lse_ref,
                 *, s_len, heads):
    i = pl.program_id(2)
    qb = BAND_BLOCK
    w = qb + 2 * BAND_HALF
    a = lax.broadcasted_iota(jnp.int32, (qb, w), 0)
    c = lax.broadcasted_iota(jnp.int32, (qb, w), 1)
    rel = c - BAND_HALF - a
    key_pos = i * qb - BAND_HALF + c
    valid = (jnp.abs(rel) <= BAND_HALF) & (key_pos >= 0) & (key_pos < s_len)
    scale = HEAD_DIM ** -0.5
    for h in range(heads):
        sl = slice(h * HEAD_DIM, (h + 1) * HEAD_DIM)
        kw = jnp.concatenate([kp_ref[qb - BAND_HALF:, sl], kc_ref[:, sl], kn_ref[:BAND_HALF, sl]], axis=0)
        vw = jnp.concatenate([vp_ref[qb - BAND_HALF:, sl], vc_ref[:, sl], vn_ref[:BAND_HALF, sl]], axis=0)
        s = lax.dot_general(q_ref[:, sl], kw, (((1,), (1,)), ((), ())),
                            preferred_element_type=F32) * scale
        s = jnp.where(valid, s, -jnp.inf)
        m = jnp.max(s, axis=1, keepdims=True)
        p = jnp.exp(s - m)
        l = jnp.sum(p, axis=1, keepdims=True)
        o = jnp.dot(p.astype(BF16), vw, preferred_element_type=F32)
        o_ref[:, sl] = o / l
        lse_ref[:, sl] = jnp.broadcast_to(m + jnp.log(l), (qb, HEAD_DIM))


def _band_attention(q, k, v, batch, seq_len, gi, dil):
    n = q.shape[0]
    s_len = seq_len // dil
    nb = s_len // BAND_BLOCK
    hw = DA_HEADS_PER_GROUP * HEAD_DIM
    groups = len(DA_GROUPS)
    view = lambda a: a.reshape(batch, s_len, dil * a.shape[1])
    qv, kv, vv = view(q), view(k), view(v)

    def spec(shift):
        return pl.BlockSpec(
            (None, BAND_BLOCK, hw),
            lambda b, r, i: (b, jnp.clip(i + shift, 0, nb - 1), r * groups + gi))

    out_spec = pl.BlockSpec((None, BAND_BLOCK, hw), lambda b, r, i: (b, i, r))
    o, lse = pl.pallas_call(
        functools.partial(_band_kernel, s_len=s_len, heads=DA_HEADS_PER_GROUP),
        out_shape=(jax.ShapeDtypeStruct((batch, s_len, dil * hw), F32),) * 2,
        grid=(batch, dil, nb),
        in_specs=[spec(0), spec(-1), spec(0), spec(1), spec(-1), spec(0), spec(1)],
        out_specs=(out_spec, out_spec),
        compiler_params=_params("parallel", "parallel", "parallel"),
    )(qv, kv, kv, kv, vv, vv, vv)
    return o.reshape(n, hw), lse.reshape(n, hw)


def _da_out_prologue(o0, o1, o2, l0, l1, l2):
    m = jnp.maximum(jnp.maximum(l0, l1), l2)
    e0, e1, e2 = jnp.exp(l0 - m), jnp.exp(l1 - m), jnp.exp(l2 - m)
    return (e0 * o0 + e1 * o1 + e2 * o2) / (e0 + e1 + e2)


def _dilated_layer(x, batch, seq_len, g, p):
    cos, sin = _rope_tables(seq_len)
    q, k, v = _qkv_rope(x, g, p['w_qkv'], p['q_gain'], p['k_gain'], cos, sin,
                        DA_HEADS, DA_HEADS, DA_HEADS, ROPE_DIMS // 2, seq_len, tm=256)
    outs, lses = [], []
    for gi, (_, dil) in enumerate(DA_GROUPS):
        o, lse = _band_attention(q, k, v, batch, seq_len, gi, dil)
        outs.append(o)
        lses.append(lse)
    return _mm_res(_da_out_prologue, outs + lses, [], p['w_o'], None, x)


def _router_kernel(x_ref, g_ref, wr_ref, xn_ref, aff_ref):
    xn = _rms(x_ref[...], g_ref[...])
    xn_ref[...] = xn.astype(BF16)
    logits = lax.dot_general(wr_ref[...], xn, (((1,), (1,)), ((), ())),
                             precision=HIGHEST, preferred_element_type=F32)
    m = jnp.max(logits, axis=0, keepdims=True)
    e = jnp.exp(logits - m)
    aff_ref[...] = e / jnp.sum(e, axis=0, keepdims=True)


def _router(x, g, w_router, tm=512):
    n, d = x.shape
    tm = min(tm, n)
    e = w_router.shape[1]
    return pl.pallas_call(
        _router_kernel,
        out_shape=(jax.ShapeDtypeStruct((n, d), BF16), jax.ShapeDtypeStruct((e, n), F32)),
        grid=(n // tm,),
        in_specs=[pl.BlockSpec((tm, d), lambda i: (i, 0)), _const_spec((1, d)), _const_spec((e, d))],
        out_specs=(pl.BlockSpec((tm, d), lambda i: (i, 0)), pl.BlockSpec((e, tm), lambda i: (0, i))),
        compiler_params=_params("parallel"),
    )(x, g.reshape(1, d), w_router.T)


def _expert_ffn_kernel(xe_ref, gate_ref, w1_ref, w3_ref, w2_ref, ye_ref):
    xe = xe_ref[...]
    h1 = jnp.dot(xe, w1_ref[...], preferred_element_type=F32)
    h3 = jnp.dot(xe, w3_ref[...], preferred_element_type=F32)
    hid = (h1 * jax.nn.sigmoid(h1) * h3).astype(BF16)
    ye_ref[...] = jnp.dot(hid, w2_ref[...], preferred_element_type=F32) * gate_ref[...]


def _expert_ffn(xe, gates, w1, w3, w2, tm=512):
    e, c, d = xe.shape
    f = w1.shape[2]
    tm = min(tm, c)
    return pl.pallas_call(
        _expert_ffn_kernel,
        out_shape=jax.ShapeDtypeStruct((e, c, d), F32),
        grid=(e, c // tm),
        in_specs=[pl.BlockSpec((None, tm, d), lambda ei, ci: (ei, ci, 0)),
                  pl.BlockSpec((None, tm, 1), lambda ei, ci: (ei, ci, 0)),
                  pl.BlockSpec((None, d, f), lambda ei, ci: (ei, 0, 0)),
                  pl.BlockSpec((None, d, f), lambda ei, ci: (ei, 0, 0)),
                  pl.BlockSpec((None, f, d), lambda ei, ci: (ei, 0, 0))],
        out_specs=pl.BlockSpec((None, tm, d), lambda ei, ci: (ei, ci, 0)),
        compiler_params=_params("parallel", "arbitrary"),
    )(xe, gates[..., None], w1, w3, w2)


def _moe_layer(x, group_sizes, g, w_router, w1, w3, w2):
    n, d = x.shape
    xn, aff_t = _router(x, g, w_router)
    xes, gts, idxs = [], [], []
    start = 0
    for ng in group_sizes:
        cap = EC_CAPACITY * ng // N_EXPERTS
        gates, idx = lax.top_k(aff_t[:, start:start + ng], cap)
        idx = idx + start
        xes.append(xn[idx])
        gts.append(gates)
        idxs.append(idx)
        start += ng
    ye = _expert_ffn(jnp.concatenate(xes, axis=1), jnp.concatenate(gts, axis=1),
                     w1.astype(BF16), w3.astype(BF16), w2.astype(BF16))
    idx_all = jnp.concatenate(idxs, axis=1)
    return x.at[idx_all.reshape(-1)].add(ye.reshape(-1, d))


def _trunk(x, batch, seq_len, group_sizes, p):
    depth = p['norm_gain'].shape[0]
    for i in range(depth):
        mixer, j = i % 4, i // 4
        g = p['norm_gain'][i, 0]
        if mixer == 0:
            x = _hyena_layer(x, batch, seq_len, g, {k[3:]: v[j] for k, v in p.items() if k.startswith('hy_')})
        elif mixer == 1:
            x = _gqa_layer(x, batch, seq_len, g, {k[3:]: v[j] for k, v in p.items() if k.startswith('ga_')})
        elif mixer == 2:
            x = _mlstm_layer(x, batch, seq_len, g, {k[3:]: v[j] for k, v in p.items() if k.startswith('ml_')})
        else:
            x = _dilated_layer(x, batch, seq_len, g, {k[3:]: v[j] for k, v in p.items() if k.startswith('da_')})
        x = _moe_layer(x, group_sizes, p['norm_gain'][i, 1], p['moe_w_router'][i],
                       p['moe_w1'][i], p['moe_w3'][i], p['moe_w2'][i])
    return x


def kernel(x_prompt, x_sample, norm_gain, hy_w_in, hy_b_in, hy_conv_w, hy_conv_b, hy_f_w1, hy_f_b1, hy_f_w2, hy_f_b2, hy_f_w3, hy_f_b3, hy_f_freq, hy_decay, hy_skip, hy_w_out, hy_b_out, ga_w_qkv, ga_q_gain, ga_k_gain, ga_w_o, ml_w_up, ml_conv_w, ml_conv_b, ml_w_q, ml_w_k, ml_w_v, ml_w_gate, ml_b_gate, ml_norm_gain, ml_skip, ml_w_down, da_w_qkv, da_q_gain, da_k_gain, da_w_o, moe_w_router, moe_w1, moe_w3, moe_w2):
    p = dict(
        norm_gain=norm_gain,
        hy_w_in=hy_w_in, hy_b_in=hy_b_in, hy_conv_w=hy_conv_w, hy_conv_b=hy_conv_b,
        hy_f_w1=hy_f_w1, hy_f_b1=hy_f_b1, hy_f_w2=hy_f_w2, hy_f_b2=hy_f_b2,
        hy_f_w3=hy_f_w3, hy_f_b3=hy_f_b3, hy_f_freq=hy_f_freq, hy_decay=hy_decay,
        hy_skip=hy_skip, hy_w_out=hy_w_out, hy_b_out=hy_b_out,
        ga_w_qkv=ga_w_qkv, ga_q_gain=ga_q_gain, ga_k_gain=ga_k_gain, ga_w_o=ga_w_o,
        ml_w_up=ml_w_up, ml_conv_w=ml_conv_w, ml_conv_b=ml_conv_b, ml_w_q=ml_w_q,
        ml_w_k=ml_w_k, ml_w_v=ml_w_v, ml_w_gate=ml_w_gate, ml_b_gate=ml_b_gate,
        ml_norm_gain=ml_norm_gain, ml_skip=ml_skip, ml_w_down=ml_w_down,
        da_w_qkv=da_w_qkv, da_q_gain=da_q_gain, da_k_gain=da_k_gain, da_w_o=da_w_o,
        moe_w_router=moe_w_router, moe_w1=moe_w1, moe_w3=moe_w3, moe_w2=moe_w2,
    )
    bp, seq_len, d = x_prompt.shape
    bs = x_sample.shape[0]
    assert x_sample.shape[1] == seq_len
    x = jnp.concatenate([x_prompt, x_sample], axis=0).reshape((bp + bs) * seq_len, d)
    y = _trunk(x, bp + bs, seq_len, (bp * seq_len, bs * seq_len), p)
    y = y.reshape(bp + bs, seq_len, d)
    return (y[:bp], y[bp:])
```

```python
import functools
import math

import jax
import jax.numpy as jnp
from jax import lax
from jax.experimental import pallas as pl
from jax.experimental.pallas import tpu as pltpu

F32 = jnp.float32
BF16 = jnp.bfloat16
HIGHEST = lax.Precision.HIGHEST

NORM_EPS = 1e-6
GRID_W = 64
HY_BANDS = 16
GA_HEADS = 8
GA_KV_HEADS = 2
GA_GROUP = GA_HEADS // GA_KV_HEADS
HEAD_DIM = 128
AXIAL_THETA = 10000.0
ML_HEADS = 4
ML_QKV_BLOCK = 4
DA_GROUPS = ((128, 1), (512, 4), (2048, 16))
DA_HEADS_PER_GROUP = 4
DA_HEADS = DA_HEADS_PER_GROUP * len(DA_GROUPS)
ROPE_THETA = 500000.0
ROPE_DIMS = HEAD_DIM // 4
N_EXPERTS = 16
EC_CAPACITY = 2

VMEM_LIMIT_BYTES = 52 * 1024 * 1024
HALO_ROWS = 8
MXU_DIM = 256
ML_CHUNK = 256
BAND_BLOCK = 128
BAND_HALF = 64


def _params(*sem):
    return pltpu.CompilerParams(dimension_semantics=sem, vmem_limit_bytes=VMEM_LIMIT_BYTES)


def _rms(x, g):
    ms = jnp.mean(x * x, axis=-1, keepdims=True)
    return x * lax.rsqrt(ms + NORM_EPS) * g


def _const_spec(shape):
    nd = len(shape)
    return pl.BlockSpec(shape, lambda *_: (0,) * nd)


def _conv3_rows(p, pprev, pnext, cw, cb, rows, tm):
    up = jnp.where(rows == 0, pprev, pltpu.roll(p, 1, 0))
    dn = jnp.where(rows == tm - 1, pnext, pltpu.roll(p, tm - 1, 0))
    return up * cw[0:1] + p * cw[1:2] + dn * cw[2:3] + cb


def _halo_specs(tm, d, n_rows):
    hb = tm // HALO_ROWS
    last = n_rows // HALO_ROWS - 1
    prev = pl.BlockSpec((HALO_ROWS, d), lambda i: (jnp.maximum(i * hb - 1, 0), 0))
    nxt = pl.BlockSpec((HALO_ROWS, d), lambda i: (jnp.minimum((i + 1) * hb, last), 0))
    return prev, nxt


def _edge_scales(i, tm, seq_len):
    t0 = i * tm
    keep_prev = jnp.where(t0 % seq_len == 0, 0.0, 1.0).astype(F32)
    keep_next = jnp.where((t0 + tm) % seq_len == 0, 0.0, 1.0).astype(F32)
    return keep_prev, keep_next


def _mm_res_kernel(*refs, prologue, n_row, n_const, tn):
    row_refs = refs[:n_row]
    const_refs = refs[n_row:n_row + n_const]
    w_ref, b_ref, res_ref, o_ref = refs[n_row + n_const:]
    lhs = prologue(*[r[...] for r in row_refs], *[c[...] for c in const_refs]).astype(BF16)
    for j in range(o_ref.shape[1] // tn):
        sl = slice(j * tn, (j + 1) * tn)
        o_ref[:, sl] = (res_ref[:, sl] + b_ref[:, sl]
                        + jnp.dot(lhs, w_ref[:, sl], preferred_element_type=F32))


def _mm_res(prologue, rows, consts, w, b, res, tm=512, tn=512):
    n, dout = res.shape
    tm = min(tm, n)
    k = w.shape[0]
    if b is None:
        b = jnp.zeros((1, dout), F32)
    in_specs = [pl.BlockSpec((tm, r.shape[1]), lambda i: (i, 0)) for r in rows]
    in_specs += [_const_spec(c.shape) for c in consts]
    in_specs += [_const_spec((k, dout)), _const_spec((1, dout)),
                 pl.BlockSpec((tm, dout), lambda i: (i, 0))]
    return pl.pallas_call(
        functools.partial(_mm_res_kernel, prologue=prologue, n_row=len(rows),
                          n_const=len(consts), tn=min(tn, dout)),
        out_shape=jax.ShapeDtypeStruct((n, dout), F32),
        grid=(n // tm,),
        in_specs=in_specs,
        out_specs=pl.BlockSpec((tm, dout), lambda i: (i, 0)),
        compiler_params=_params("parallel"),
    )(*rows, *consts, w.astype(BF16), b.reshape(1, dout).astype(F32), res)


def _hyena_in_kernel(x_ref, xp_ref, xn_ref, g_ref, w_ref, b_ref, cw_ref, cb_ref,
                     x0_ref, vx_ref, *, seq_len, tm, d, cols):
    keep_prev, keep_next = _edge_scales(pl.program_id(0), tm, seq_len)
    g = g_ref[...]
    xb = _rms(x_ref[...], g).astype(BF16)
    hb = _rms(jnp.concatenate([xp_ref[...], xn_ref[...]], axis=0), g).astype(BF16)
    rows = lax.broadcasted_iota(jnp.int32, (tm, 1), 0)

    def conv_part(c0):
        sl = slice(c0, c0 + cols)
        w = w_ref[:, sl]
        bias = b_ref[:, sl]
        p = jnp.dot(xb, w, preferred_element_type=F32) + bias
        ph = jnp.dot(hb, w, preferred_element_type=F32) + bias
        pprev = ph[HALO_ROWS - 1:HALO_ROWS, :] * keep_prev
        pnext = ph[HALO_ROWS:HALO_ROWS + 1, :] * keep_next
        return _conv3_rows(p, pprev, pnext, cw_ref[:, sl], cb_ref[:, sl], rows, tm)

    for j in range(d // cols):
        c = j * cols
        x0_ref[:, c:c + cols] = conv_part(c)
        vx_ref[:, c:c + cols] = conv_part(2 * d + c) * conv_part(d + c)


def _hyena_in(x, g, w_in, b_in, conv_w, conv_b, seq_len, tm=256, cols=512):
    n, d = x.shape
    tm = min(tm, seq_len)
    prev, nxt = _halo_specs(tm, d, n)
    row = pl.BlockSpec((tm, d), lambda i: (i, 0))
    return pl.pallas_call(
        functools.partial(_hyena_in_kernel, seq_len=seq_len, tm=tm, d=d, cols=cols),
        out_shape=(jax.ShapeDtypeStruct((n, d), F32), jax.ShapeDtypeStruct((n, d), F32)),
        grid=(n // tm,),
        in_specs=[row, prev, nxt, _const_spec((1, d)), _const_spec((d, 3 * d)),
                  _const_spec((1, 3 * d)), _const_spec((3, 3 * d)), _const_spec((1, 3 * d))],
        out_specs=(row, row),
        compiler_params=_params("parallel"),
    )(x, x, x, g.reshape(1, d), w_in.astype(BF16), b_in.reshape(1, 3 * d),
      conv_w, conv_b.reshape(1, 3 * d))


def _hyena_filter_taps(L, d, f_w1, f_b1, f_w2, f_b2, f_w3, f_b3, f_freq, decay):
    t = jnp.linspace(0.0, 1.0, L, dtype=F32)[:, None]
    w_ang = 2.0 * math.pi * jnp.arange(L, dtype=F32)[:, None] / L
    bands = jnp.linspace(1e-4, HY_BANDS - 1, HY_BANDS, dtype=F32)[None, :]
    z = jnp.concatenate([t, jnp.cos(bands * w_ang), -jnp.sin(bands * w_ang)], axis=-1)
    h = jnp.sin(f_freq[0] * (z @ f_w1 + f_b1))
    h = jnp.sin(f_freq[1] * (h @ f_w2 + f_b2))
    h = h @ f_w3 + f_b3
    h = h.reshape(L, 2, d) * jnp.exp(-t[:, :, None] * jnp.abs(decay)[None])
    k = jnp.concatenate([h[:, 0], jnp.zeros((1, d), F32), h[1:, 1][::-1]], axis=0)
    return k / jnp.sum(jnp.abs(k), axis=0, keepdims=True)


def _fft_dims(m):
    lg = m.bit_length() - 1
    p = 1 << ((lg + 1) // 2)
    return p, m // p


def _split_bf16(x):
    hi = x.astype(BF16)
    return hi, (x - hi.astype(F32)).astype(BF16)


def _mm_split(fh, fl, x, precise):
    if not precise:
        return jnp.dot(fh, x.astype(BF16), preferred_element_type=F32)
    xh, xl = _split_bf16(x)
    return (jnp.dot(fh, xh, preferred_element_type=F32) + jnp.dot(fl, xh, preferred_element_type=F32)
            + jnp.dot(fh, xl, preferred_element_type=F32))


def _cplx_as_real(cr, ci):
    top = jnp.concatenate([cr, -ci], axis=-1)
    bot = jnp.concatenate([ci, cr], axis=-1)
    return jnp.concatenate([top, bot], axis=-2)


def _unit_circle(idx, m):
    ang = (2.0 * math.pi / m) * idx.astype(F32)
    return jnp.cos(ang), jnp.sin(ang)


def _dft_consts(p, q):
    m = p * q
    k1 = jnp.arange(p, dtype=jnp.int32)
    n1 = jnp.arange(p // 2, dtype=jnp.int32)
    c, s = _unit_circle((k1[:, None] * n1[None, :]) % p, p)
    fa = _cplx_as_real(c, -s)
    c, s = _unit_circle((n1[:, None] * k1[None, :]) % p, p)
    fd = _cplx_as_real(c / m, s / m)
    k2 = jnp.arange(q, dtype=jnp.int32)
    n2 = jnp.arange(q, dtype=jnp.int32)
    idx = (n2[None, None, :] * (k2[None, :, None] * p + k1[:, None, None])) % m
    c, s = _unit_circle(idx, m)
    gb = _cplx_as_real(c, -s)
    ct, st = jnp.swapaxes(c, 1, 2), jnp.swapaxes(s, 1, 2)
    gc = _cplx_as_real(ct, st)
    return tuple(_split_bf16(a) for a in (fa, gb, gc, fd))


def _fft_a_kernel(x_ref, fh_ref, fl_ref, o_ref, *, precise):
    _, half, d = x_ref.shape
    y = _mm_split(fh_ref[...], fl_ref[...], x_ref[...].reshape(2 * half, d), precise)
    o_ref[...] = y.reshape(o_ref.shape)


def _fft_b_kernel(a_ref, gh_ref, gl_ref, o_ref):
    _, q, d = a_ref.shape
    x = _mm_split(gh_ref[...], gl_ref[...], a_ref[...].reshape(2 * q, d), True)
    o_ref[...] = x.reshape(o_ref.shape)


def _fft_bc_kernel(a_ref, gb_ref, gc_ref, k_ref, z_ref):
    _, q, d = a_ref.shape
    x = _mm_split(gb_ref[...], None, a_ref[...].reshape(2 * q, d), False)
    xr, xi = x[:q], x[q:]
    kr, ki = k_ref[0], k_ref[1]
    y = jnp.concatenate([xr * kr - xi * ki, xr * ki + xi * kr], axis=0)
    z = _mm_split(gc_ref[...], None, y, False)
    z_ref[...] = z.reshape(z_ref.shape)


def _fft_stage_a(x4, fa, p, q, d, precise):
    pairs = x4.shape[0]
    return pl.pallas_call(
        functools.partial(_fft_a_kernel, precise=precise),
        out_shape=jax.ShapeDtypeStruct((pairs, 2, p, q * d), F32),
        grid=(pairs, q),
        in_specs=[pl.BlockSpec((None, 2, p // 2, d), lambda b, j: (b, 0, 0, j)),
                  _const_spec((2 * p, p)), _const_spec((2 * p, p))],
        out_specs=pl.BlockSpec((None, 2, p, d), lambda b, j: (b, 0, 0, j)),
        compiler_params=_params("parallel", "parallel"),
    )(x4, *fa)


def _long_conv(vx, taps, batch, seq_len):
    n, d = vx.shape
    m = 2 * seq_len
    p, q = _fft_dims(m)
    assert batch % 2 == 0
    pairs = batch // 2
    fa, gb, gc, fd = _dft_consts(p, q)
    g_spec = pl.BlockSpec((None, 2 * q, 2 * q), lambda k1, b: (k1, 0, 0))
    slab = pl.BlockSpec((None, 2, q, d), lambda k1, b: (b, 0, k1, 0))

    zeros = jnp.zeros((seq_len, d), F32)
    kin = jnp.stack([taps[:seq_len], zeros, taps[seq_len:], zeros]).reshape(2, 2, p // 2, q * d)
    ka = _fft_stage_a(kin, fa, p, q, d, True).reshape(2, 2, p * q, d)
    kx = pl.pallas_call(
        _fft_b_kernel,
        out_shape=jax.ShapeDtypeStruct((2, 2, p * q, d), F32),
        grid=(p, 2),
        in_specs=[slab, g_spec, g_spec],
        out_specs=slab,
        compiler_params=_params("parallel", "parallel"),
    )(ka, *gb)
    sign = jnp.repeat(1.0 - 2.0 * (jnp.arange(p) % 2).astype(F32), q)[None, :, None]
    kspec = kx[0] + sign * kx[1]

    xa = _fft_stage_a(vx.reshape(pairs, 2, p // 2, q * d), fa, p, q, d, False).reshape(pairs, 2, p * q, d)
    z = pl.pallas_call(
        _fft_bc_kernel,
        out_shape=jax.ShapeDtypeStruct((pairs, 2, p * q, d), F32),
        grid=(p, pairs),
        in_specs=[slab, g_spec, g_spec, pl.BlockSpec((2, q, d), lambda k1, b: (0, k1, 0))],
        out_specs=slab,
        compiler_params=_params("parallel", "parallel"),
    )(xa, gb[0], gc[0], kspec)
    y = pl.pallas_call(
        functools.partial(_fft_a_kernel, precise=False),
        out_shape=jax.ShapeDtypeStruct((pairs, 2, p // 2, q * d), F32),
        grid=(pairs, q),
        in_specs=[pl.BlockSpec((None, 2, p, d), lambda b, j: (b, 0, 0, j)),
                  _const_spec((p, 2 * p)), _const_spec((p, 2 * p))],
        out_specs=pl.BlockSpec((None, 2, p // 2, d), lambda b, j: (b, 0, 0, j)),
        compiler_params=_params("parallel", "parallel"),
    )(z.reshape(pairs, 2, p, q * d), *fd)
    return y.reshape(n, d)


def _hyena_out_prologue(y, vx, x0, skip):
    return (y + vx * skip) * x0


def _hyena_layer(x, batch, seq_len, g, p):
    n, d = x.shape
    x0, vx = _hyena_in(x, g, p['w_in'], p['b_in'], p['conv_w'], p['conv_b'], seq_len)
    taps = _hyena_filter_taps(seq_len, d, p['f_w1'], p['f_b1'], p['f_w2'], p['f_b2'],
                              p['f_w3'], p['f_b3'], p['f_freq'], p['decay'])
    y = _long_conv(vx, taps, batch, seq_len)
    return _mm_res(_hyena_out_prologue, [y, vx, x0], [p['skip'].reshape(1, d)],
                   p['w_out'], p['b_out'], x)


def _head_norm_rope(xh, gain, cos, sin, half):
    y = _rms(xh, gain)
    lane = lax.broadcasted_iota(jnp.int32, (1, HEAD_DIM), 1)
    fwd = pltpu.roll(y, HEAD_DIM - half, 1)
    bwd = pltpu.roll(y, half, 1)
    partner = jnp.where((lane % (2 * half)) < half, fwd, bwd)
    return y * cos + partner * sin


def _qkv_rope_kernel(x_ref, g_ref, w_ref, qg_ref, kg_ref, cos_ref, sin_ref,
                     q_ref, k_ref, v_ref, *, nq, nk, nv, half):
    xb = _rms(x_ref[...], g_ref[...]).astype(BF16)
    cos = cos_ref[...]
    sin = sin_ref[...]
    per = MXU_DIM // HEAD_DIM
    for h0 in range(0, nq + nk + nv, per):
        pw = jnp.dot(xb, w_ref[:, h0 * HEAD_DIM:(h0 + per) * HEAD_DIM], preferred_element_type=F32)
        for h in range(h0, h0 + per):
            ph = pw[:, (h - h0) * HEAD_DIM:(h - h0 + 1) * HEAD_DIM]
            if h < nq:
                sl = slice(h * HEAD_DIM, (h + 1) * HEAD_DIM)
                q_ref[:, sl] = _head_norm_rope(ph, qg_ref[...], cos, sin, half).astype(BF16)
            elif h < nq + nk:
                sl = slice((h - nq) * HEAD_DIM, (h - nq + 1) * HEAD_DIM)
                k_ref[:, sl] = _head_norm_rope(ph, kg_ref[...], cos, sin, half).astype(BF16)
            else:
                sl = slice((h - nq - nk) * HEAD_DIM, (h - nq - nk + 1) * HEAD_DIM)
                v_ref[:, sl] = ph.astype(BF16)


def _qkv_rope(x, g, w_qkv, q_gain, k_gain, cos, sin, nq, nk, nv, half, seq_len, tm=512):
    n, d = x.shape
    tm = min(tm, seq_len)
    pos_blocks = seq_len // tm
    f = w_qkv.shape[1]
    row = pl.BlockSpec((tm, d), lambda i: (i, 0))
    tab = pl.BlockSpec((tm, HEAD_DIM), lambda i: (i % pos_blocks, 0))
    outs = tuple(jax.ShapeDtypeStruct((n, c * HEAD_DIM), BF16) for c in (nq, nk, nv))
    return pl.pallas_call(
        functools.partial(_qkv_rope_kernel, nq=nq, nk=nk, nv=nv, half=half),
        out_shape=outs,
        grid=(n // tm,),
        in_specs=[row, _const_spec((1, d)), _const_spec((d, f)),
                  _const_spec((1, HEAD_DIM)), _const_spec((1, HEAD_DIM)), tab, tab],
        out_specs=tuple(pl.BlockSpec((tm, c * HEAD_DIM), lambda i: (i, 0)) for c in (nq, nk, nv)),
        compiler_params=_params("parallel"),
    )(x, g.reshape(1, d), w_qkv.astype(BF16), q_gain.reshape(1, HEAD_DIM),
      k_gain.reshape(1, HEAD_DIM), cos, sin)


def _axial_tables(L):
    t = jnp.arange(L)
    r = (t // GRID_W).astype(F32)
    c = (t % GRID_W).astype(F32)
    nf = HEAD_DIM // 4
    inv = AXIAL_THETA ** (-(2.0 * jnp.arange(nf, dtype=F32)) / (2 * nf))
    ar, ac = r[:, None] * inv[None], c[:, None] * inv[None]
    cos = jnp.concatenate([jnp.cos(ar), jnp.cos(ar), jnp.cos(ac), jnp.cos(ac)], axis=-1)
    sin = jnp.concatenate([-jnp.sin(ar), jnp.sin(ar), -jnp.sin(ac), jnp.sin(ac)], axis=-1)
    return cos, sin


def _flash_kernel(q_ref, k_ref, v_ref, o_ref, m_scr, l_scr, acc_scr, *, tq, tk, seq_len, group):
    c = (HEAD_DIM ** -0.5) * math.log2(math.e)
    m_scr[...] = jnp.full(m_scr.shape, -jnp.inf, F32)
    l_scr[...] = jnp.zeros_like(l_scr)
    acc_scr[...] = jnp.zeros_like(acc_scr)

    def body(j, carry):
        start = pl.multiple_of(j * tk, tk)
        kt = k_ref[pl.ds(start, tk), :]
        vt = v_ref[pl.ds(start, tk), :]
        for h in range(group):
            q = q_ref[:, h * HEAD_DIM:(h + 1) * HEAD_DIM]
            s = lax.dot_general(q, kt, (((1,), (1,)), ((), ())), preferred_element_type=F32)
            m = m_scr[h]
            m_new = jnp.maximum(m, jnp.max(s, axis=1, keepdims=True))
            alpha = jnp.exp2((m - m_new) * c)
            p = jnp.exp2(s * c - m_new * c)
            l_scr[h] = alpha * l_scr[h] + jnp.sum(p, axis=1, keepdims=True)
            acc_scr[h] = alpha * acc_scr[h] + jnp.dot(p.astype(BF16), vt, preferred_element_type=F32)
            m_scr[h] = m_new
        return carry

    lax.fori_loop(0, seq_len // tk, body, 0)
    for h in range(group):
        o_ref[:, h * HEAD_DIM:(h + 1) * HEAD_DIM] = (acc_scr[h] / l_scr[h]).astype(o_ref.dtype)


def _flash_gqa(q, k, v, batch, seq_len, tq=256, tk=512):
    tq = min(tq, seq_len)
    tk = min(tk, seq_len)
    nq = seq_len // tq
    gw = GA_GROUP * HEAD_DIM
    return pl.pallas_call(
        functools.partial(_flash_kernel, tq=tq, tk=tk, seq_len=seq_len, group=GA_GROUP),
        out_shape=jax.ShapeDtypeStruct(q.shape, BF16),
        grid=(batch, GA_KV_HEADS, nq),
        in_specs=[pl.BlockSpec((tq, gw), lambda b, kv, i: (b * nq + i, kv)),
                  pl.BlockSpec((seq_len, HEAD_DIM), lambda b, kv, i: (b, kv)),
                  pl.BlockSpec((seq_len, HEAD_DIM), lambda b, kv, i: (b, kv))],
        out_specs=pl.BlockSpec((tq, gw), lambda b, kv, i: (b * nq + i, kv)),
        scratch_shapes=[pltpu.VMEM((GA_GROUP, tq, 1), F32), pltpu.VMEM((GA_GROUP, tq, 1), F32),
                        pltpu.VMEM((GA_GROUP, tq, HEAD_DIM), F32)],
        compiler_params=_params("parallel", "parallel", "parallel"),
    )(q, k, v)


def _identity_prologue(o):
    return o


def _gqa_layer(x, batch, seq_len, g, p):
    cos, sin = _axial_tables(seq_len)
    q, k, v = _qkv_rope(x, g, p['w_qkv'], p['q_gain'], p['k_gain'], cos, sin,
                        GA_HEADS, GA_KV_HEADS, GA_KV_HEADS, HEAD_DIM // 4, seq_len)
    o = _flash_gqa(q, k, v, batch, seq_len)
    return _mm_res(_identity_prologue, [o], [], p['w_o'], None, x)


def _ml_in_kernel(x_ref, xp_ref, xn_ref, g_ref, w_ref, cw_ref, cb_ref, wq_ref, wk_ref, wv_ref,
                  wg_ref, bg_ref, q_ref, k_ref, v_ref, xc_ref, sz_ref, gate_ref,
                  *, seq_len, tm, inner, k_scale):
    keep_prev, keep_next = _edge_scales(pl.program_id(0), tm, seq_len)
    g = g_ref[...]
    xb = _rms(x_ref[...], g).astype(BF16)
    hb = _rms(jnp.concatenate([xp_ref[...], xn_ref[...]], axis=0), g).astype(BF16)
    rows = lax.broadcasted_iota(jnp.int32, (tm, 1), 0)
    gacc = jnp.zeros(gate_ref.shape, F32)
    for t in range(inner // MXU_DIM):
        sl = slice(t * MXU_DIM, (t + 1) * MXU_DIM)
        w = w_ref[:, sl]
        xm = jnp.dot(xb, w, preferred_element_type=F32)
        xh = jnp.dot(hb, w, preferred_element_type=F32)
        pprev = xh[HALO_ROWS - 1:HALO_ROWS, :] * keep_prev
        pnext = xh[HALO_ROWS:HALO_ROWS + 1, :] * keep_next
        xc = _conv3_rows(xm, pprev, pnext, cw_ref[:, sl], cb_ref[:, sl], rows, tm)
        xc = xc * jax.nn.sigmoid(xc)
        z = jnp.dot(xb, w_ref[:, inner + t * MXU_DIM:inner + (t + 1) * MXU_DIM],
                    preferred_element_type=F32)
        xcb = xc.astype(BF16)
        q = jnp.dot(xcb, wq_ref[t], preferred_element_type=F32)
        k = jnp.dot(xcb, wk_ref[t], preferred_element_type=F32)
        v = jnp.dot(xm.astype(BF16), wv_ref[t], preferred_element_type=F32)
        qb, kb, vb = q.astype(BF16), k.astype(BF16), v.astype(BF16)
        gacc += (jnp.dot(qb, wg_ref[0, sl, :], preferred_element_type=F32)
                 + jnp.dot(kb, wg_ref[1, sl, :], preferred_element_type=F32)
                 + jnp.dot(vb, wg_ref[2, sl, :], preferred_element_type=F32))
        q_ref[:, sl] = qb
        k_ref[:, sl] = (k * k_scale).astype(BF16)
        v_ref[:, sl] = vb
        xc_ref[:, sl] = xc
        sz_ref[:, sl] = z * jax.nn.sigmoid(z)
    gate_ref[...] = gacc + bg_ref[...]


def _block_diag_tiles(w):
    nb, c, _ = w.shape
    per = MXU_DIM // c
    wt = w.reshape(nb // per, per, c, c)
    eye = jnp.eye(per, dtype=w.dtype)
    full = jnp.einsum('tpcd,pq->tpcqd', wt, eye)
    return full.reshape(nb // per, MXU_DIM, MXU_DIM)


def _ml_in(x, g, p, seq_len, tm=256):
    n, d = x.shape
    tm = min(tm, seq_len)
    inner = p['w_up'].shape[1] // 2
    ng = 4 * ML_HEADS
    dh = inner // ML_HEADS
    wq, wk, wv = (_block_diag_tiles(p[nm]).astype(BF16) for nm in ('w_q', 'w_k', 'w_v'))
    wg = jnp.transpose(p['w_gate'], (1, 2, 0, 3)).reshape(3, inner, ng).astype(BF16)
    bg = p['b_gate'].reshape(1, ng)
    prev, nxt = _halo_specs(tm, d, n)
    row = pl.BlockSpec((tm, d), lambda i: (i, 0))
    wide = pl.BlockSpec((tm, inner), lambda i: (i, 0))
    nt = inner // MXU_DIM
    return pl.pallas_call(
        functools.partial(_ml_in_kernel, seq_len=seq_len, tm=tm, inner=inner, k_scale=dh ** -0.5),
        out_shape=(jax.ShapeDtypeStruct((n, inner), BF16),) * 3
        + (jax.ShapeDtypeStruct((n, inner), F32),) * 2
        + (jax.ShapeDtypeStruct((n, ng), F32),),
        grid=(n // tm,),
        in_specs=[row, prev, nxt, _const_spec((1, d)), _const_spec((d, 2 * inner)),
                  _const_spec((3, inner)), _const_spec((1, inner)),
                  _const_spec((nt, MXU_DIM, MXU_DIM)), _const_spec((nt, MXU_DIM, MXU_DIM)),
                  _const_spec((nt, MXU_DIM, MXU_DIM)), _const_spec((3, inner, ng)),
                  _const_spec((1, ng))],
        out_specs=(wide,) * 5 + (pl.BlockSpec((tm, ng), lambda i: (i, 0)),),
        compiler_params=_params("parallel"),
    )(x, x, x, g.reshape(1, d), p['w_up'].astype(BF16), p['conv_w'],
      p['conv_b'].reshape(1, inner), wq, wk, wv, wg, bg)


def _log_sigmoid(x):
    return jnp.minimum(x, 0.0) - jnp.log1p(jnp.exp(-jnp.abs(x)))


def _mlstm_chunk_kernel(q_ref, k_ref, v_ref, gc_ref, gr_ref, h_ref, c_scr, n_scr, m_scr, *, lc):
    d = pl.program_id(2)

    @pl.when(pl.program_id(3) == 0)
    def _():
        c_scr[...] = jnp.zeros_like(c_scr)
        n_scr[...] = jnp.zeros_like(n_scr)
        m_scr[...] = jnp.zeros_like(m_scr)

    q = q_ref[...]
    k = k_ref[...]
    v = v_ref[...]
    gc = gc_ref[...]
    gr = gr_ref[...]
    i_col, i_row = gc[:, 0:1], gr[0:1, :]
    lf_col = _log_sigmoid(gc[:, 1:2])
    lf_row = _log_sigmoid(gr[1:2, :])

    jr = lax.broadcasted_iota(jnp.int32, (lc, lc), 0)
    sc = lax.broadcasted_iota(jnp.int32, (lc, lc), 1)
    sgn = 1 - 2 * d
    seen = ((sc - jr) * sgn) <= 0
    seen_t = ((jr - sc) * sgn) <= 0
    g_col = jnp.dot(seen.astype(F32), jnp.broadcast_to(lf_col, (lc, HEAD_DIM)),
                    precision=HIGHEST, preferred_element_type=F32)[:, 0:1]
    g_row = jnp.dot(jnp.broadcast_to(lf_row, (HALO_ROWS, lc)), seen_t.astype(F32),
                    precision=HIGHEST, preferred_element_type=F32)[0:1, :]
    g_tot = jnp.sum(lf_row, axis=1, keepdims=True)
    m_old = m_scr[0:1, 0:1]

    dmat = jnp.where(seen, g_col - g_row + i_row, -jnp.inf)
    inter = g_col + m_old
    m_q = jnp.maximum(inter, jnp.max(dmat, axis=1, keepdims=True))
    s_qk = lax.dot_general(q, k, (((1,), (1,)), ((), ())), preferred_element_type=F32)
    a = s_qk * jnp.exp(dmat - m_q)
    w_int = jnp.exp(inter - m_q)
    q_c = jnp.dot(q, c_scr[...].astype(BF16), preferred_element_type=F32)
    num = jnp.dot(a.astype(BF16), v, preferred_element_type=F32) + q_c * w_int
    q_n = jnp.sum(q.astype(F32) * n_scr[...], axis=1, keepdims=True)
    den = jnp.sum(a, axis=1, keepdims=True) + w_int * q_n
    den = jnp.maximum(jnp.abs(den), jnp.exp(-m_q))
    h_ref[...] = num / den

    a_row = g_tot - g_row + i_row
    m_new = jnp.maximum(g_tot + m_old, jnp.max(a_row, axis=1, keepdims=True))
    ws_col = jnp.exp(g_tot - g_col + i_col - m_new)
    dec = jnp.exp(g_tot + m_old - m_new)
    kw = k.astype(F32) * ws_col
    upd = lax.dot_general(kw.astype(BF16), v, (((0,), (0,)), ((), ())), preferred_element_type=F32)
    c_scr[...] = dec * c_scr[...] + upd
    n_scr[...] = dec * n_scr[...] + jnp.sum(kw, axis=0, keepdims=True)
    m_scr[...] = jnp.broadcast_to(m_new, m_scr.shape)


def _mlstm_chunks(q, k, v, gates, batch, seq_len):
    n, inner = q.shape
    dh = inner // ML_HEADS
    lc = min(ML_CHUNK, seq_len)
    nc = seq_len // lc
    g4 = gates.reshape(n, 2, 2, ML_HEADS)
    gcol = jnp.transpose(g4, (1, 3, 0, 2))
    grow = jnp.transpose(g4, (1, 3, 2, 0))

    def chunk(b, c, dd):
        return b * nc + c + dd * (nc - 1 - 2 * c)

    qkv_spec = pl.BlockSpec((lc, dh), lambda b, h, dd, c: (chunk(b, c, dd), h))
    return pl.pallas_call(
        functools.partial(_mlstm_chunk_kernel, lc=lc),
        out_shape=jax.ShapeDtypeStruct((2, n, inner), F32),
        grid=(batch, ML_HEADS, 2, nc),
        in_specs=[qkv_spec, qkv_spec, qkv_spec,
                  pl.BlockSpec((None, None, lc, 2), lambda b, h, dd, c: (dd, h, chunk(b, c, dd), 0)),
                  pl.BlockSpec((None, None, 2, lc), lambda b, h, dd, c: (dd, h, 0, chunk(b, c, dd)))],
        out_specs=pl.BlockSpec((None, lc, dh), lambda b, h, dd, c: (dd, chunk(b, c, dd), h)),
        scratch_shapes=[pltpu.VMEM((dh, dh), F32), pltpu.VMEM((1, dh), F32),
                        pltpu.VMEM((HALO_ROWS, HEAD_DIM), F32)],
        compiler_params=_params("parallel", "parallel", "parallel", "arbitrary"),
    )(q, k, v, gcol, grow)


def _ml_out_prologue(hf, hb, xc, sz, gain, skip):
    h = hf + hb
    dh = h.shape[1] // ML_HEADS
    parts = []
    for i in range(ML_HEADS):
        sl = slice(i * dh, (i + 1) * dh)
        parts.append(_rms(h[:, sl], gain[:, sl]))
    hn = jnp.concatenate(parts, axis=1)
    return (hn + skip * xc) * sz


def _mlstm_layer(x, batch, seq_len, g, p):
    q, k, v, xc, sz, gates = _ml_in(x, g, p, seq_len)
    inner = q.shape[1]
    hs = _mlstm_chunks(q, k, v, gates, batch, seq_len)
    return _mm_res(_ml_out_prologue, [hs[0], hs[1], xc, sz],
                   [p['norm_gain'].reshape(1, inner), p['skip'].reshape(1, inner)],
                   p['w_down'], None, x, tm=256)


def _rope_tables(L):
    inv = ROPE_THETA ** (-(2.0 * jnp.arange(ROPE_DIMS // 2, dtype=F32)) / ROPE_DIMS)
    ang = jnp.arange(L, dtype=F32)[:, None] * inv[None]
    pad = HEAD_DIM - ROPE_DIMS
    cos = jnp.concatenate([jnp.cos(ang), jnp.cos(ang), jnp.ones((L, pad), F32)], axis=-1)
    sin = jnp.concatenate([-jnp.sin(ang), jnp.sin(ang), jnp.zeros((L, pad), F32)], axis=-1)
    return cos, sin


def _band_kernel(q_ref, kp_ref, kc_ref, kn_ref, vp_ref, vc_ref, vn_ref, o_ref, lse_ref,
                 *, s_len, heads):
    i = pl.program_id(2)
    qb = BAND_BLOCK
    w = qb + 2 * BAND_HALF
    a = lax.broadcasted_iota(jnp.int32, (qb, w), 0)
    c = lax.broadcasted_iota(jnp.int32, (qb, w), 1)
    rel = c - BAND_HALF - a
    key_pos = i * qb - BAND_HALF + c
    valid = (jnp.abs(rel) <= BAND_HALF) & (key_pos >= 0) & (key_pos < s_len)
    scale = HEAD_DIM ** -0.5
    for h in range(heads):
        sl = slice(h * HEAD_DIM, (h + 1) * HEAD_DIM)
        kw = jnp.concatenate([kp_ref[qb - BAND_HALF:, sl], kc_ref[:, sl], kn_ref[:BAND_HALF, sl]], axis=0)
        vw = jnp.concatenate([vp_ref[qb - BAND_HALF:, sl], vc_ref[:, sl], vn_ref[:BAND_HALF, sl]], axis=0)
        s = lax.dot_general(q_ref[:, sl], kw, (((1,), (1,)), ((), ())),
                            preferred_element_type=F32) * scale
        s = jnp.where(valid, s, -jnp.inf)
        m = jnp.max(s, axis=1, keepdims=True)
        p = jnp.exp(s - m)
        l = jnp.sum(p, axis=1, keepdims=True)
        o = jnp.dot(p.astype(BF16), vw, preferred_element_type=F32)
        o_ref[:, sl] = o / l
        lse_ref[:, sl] = jnp.broadcast_to(m + jnp.log(l), (qb, HEAD_DIM))


def _band_attention(q, k, v, batch, seq_len, gi, dil):
    n = q.shape[0]
    s_len = seq_len // dil
    nb = s_len // BAND_BLOCK
    hw = DA_HEADS_PER_GROUP * HEAD_DIM
    groups = len(DA_GROUPS)
    view = lambda a: a.reshape(batch, s_len, dil * a.shape[1])
    qv, kv, vv = view(q), view(k), view(v)

    def spec(shift):
        return pl.BlockSpec(
            (None, BAND_BLOCK, hw),
            lambda b, r, i: (b, jnp.clip(i + shift, 0, nb - 1), r * groups + gi))

    out_spec = pl.BlockSpec((None, BAND_BLOCK, hw), lambda b, r, i: (b, i, r))
    o, lse = pl.pallas_call(
        functools.partial(_band_kernel, s_len=s_len, heads=DA_HEADS_PER_GROUP),
        out_shape=(jax.ShapeDtypeStruct((batch, s_len, dil * hw), F32),) * 2,
        grid=(batch, dil, nb),
        in_specs=[spec(0), spec(-1), spec(0), spec(1), spec(-1), spec(0), spec(1)],
        out_specs=(out_spec, out_spec),
        compiler_params=_params("parallel", "parallel", "parallel"),
    )(qv, kv, kv, kv, vv, vv, vv)
    return o.reshape(n, hw), lse.reshape(n, hw)


def _da_out_prologue(o0, o1, o2, l0, l1, l2):
    m = jnp.maximum(jnp.maximum(l0, l1), l2)
    e0, e1, e2 = jnp.exp(l0 - m), jnp.exp(l1 - m), jnp.exp(l2 - m)
    return (e0 * o0 + e1 * o1 + e2 * o2) / (e0 + e1 + e2)


def _dilated_layer(x, batch, seq_len, g, p):
    cos, sin = _rope_tables(seq_len)
    q, k, v = _qkv_rope(x, g, p['w_qkv'], p['q_gain'], p['k_gain'], cos, sin,
                        DA_HEADS, DA_HEADS, DA_HEADS, ROPE_DIMS // 2, seq_len, tm=256)
    outs, lses = [], []
    for gi, (_, dil) in enumerate(DA_GROUPS):
        o, lse = _band_attention(q, k, v, batch, seq_len, gi, dil)
        outs.append(o)
        lses.append(lse)
    return _mm_res(_da_out_prologue, outs + lses, [], p['w_o'], None, x)


def _router_kernel(x_ref, g_ref, wr_ref, xn_ref, aff_ref):
    xn = _rms(x_ref[...], g_ref[...])
    xn_ref[...] = xn.astype(BF16)
    logits = lax.dot_general(wr_ref[...], xn, (((1,), (1,)), ((), ())),
                             precision=HIGHEST, preferred_element_type=F32)
    m = jnp.max(logits, axis=0, keepdims=True)
    e = jnp.exp(logits - m)
    aff_ref[...] = e / jnp.sum(e, axis=0, keepdims=True)


def _router(x, g, w_router, tm=512):
    n, d = x.shape
    tm = min(tm, n)
    e = w_router.shape[1]
    return pl.pallas_call(
        _router_kernel,
        out_shape=(jax.ShapeDtypeStruct((n, d), BF16), jax.ShapeDtypeStruct((e, n), F32)),
        grid=(n // tm,),
        in_specs=[pl.BlockSpec((tm, d), lambda i: (i, 0)), _const_spec((1, d)), _const_spec((e, d))],
        out_specs=(pl.BlockSpec((tm, d), lambda i: (i, 0)), pl.BlockSpec((e, tm), lambda i: (0, i))),
        compiler_params=_params("parallel"),
    )(x, g.reshape(1, d), w_router.T)


def _expert_ffn_kernel(xe_ref, gate_ref, w1_ref, w3_ref, w2_ref, ye_ref):
    xe = xe_ref[...]
    h1 = jnp.dot(xe, w1_ref[...], preferred_element_type=F32)
    h3 = jnp.dot(xe, w3_ref[...], preferred_element_type=F32)
    hid = (h1 * jax.nn.sigmoid(h1) * h3).astype(BF16)
    ye_ref[...] = jnp.dot(hid, w2_ref[...], preferred_element_type=F32) * gate_ref[...]


def _expert_ffn(xe, gates, w1, w3, w2, tm=512):
    e, c, d = xe.shape
    f = w1.shape[2]
    tm = min(tm, c)
    return pl.pallas_call(
        _expert_ffn_kernel,
        out_shape=jax.ShapeDtypeStruct((e, c, d), F32),
        grid=(e, c // tm),
        in_specs=[pl.BlockSpec((None, tm, d), lambda ei, ci: (ei, ci, 0)),
                  pl.BlockSpec((None, tm, 1), lambda ei, ci: (ei, ci, 0)),
                  pl.BlockSpec((None, d, f), lambda ei, ci: (ei, 0, 0)),
                  pl.BlockSpec((None, d, f), lambda ei, ci: (ei, 0, 0)),
                  pl.BlockSpec((None, f, d), lambda ei, ci: (ei, 0, 0))],
        out_specs=pl.BlockSpec((None, tm, d), lambda ei, ci: (ei, ci, 0)),
        compiler_params=_params("parallel", "arbitrary"),
    )(xe, gates[..., None], w1, w3, w2)


def _moe_layer(x, group_sizes, g, w_router, w1, w3, w2):
    n, d = x.shape
    xn, aff_t = _router(x, g, w_router)
    xes, gts, idxs = [], [], []
    start = 0
    for ng in group_sizes:
        cap = EC_CAPACITY * ng // N_EXPERTS
        gates, idx = lax.top_k(aff_t[:, start:start + ng], cap)
        idx = idx + start
        xes.append(xn[idx])
        gts.append(gates)
        idxs.append(idx)
        start += ng
    ye = _expert_ffn(jnp.concatenate(xes, axis=1), jnp.concatenate(gts, axis=1),
                     w1.astype(BF16), w3.astype(BF16), w2.astype(BF16))
    idx_all = jnp.concatenate(idxs, axis=1)
    return x.at[idx_all.reshape(-1)].add(ye.reshape(-1, d))


def _trunk(x, batch, seq_len, group_sizes, p):
    depth = p['norm_gain'].shape[0]
    for i in range(depth):
        mixer, j = i % 4, i // 4
        g = p['norm_gain'][i, 0]
        if mixer == 0:
            x = _hyena_layer(x, batch, seq_len, g, {k[3:]: v[j] for k, v in p.items() if k.startswith('hy_')})
        elif mixer == 1:
            x = _gqa_layer(x, batch, seq_len, g, {k[3:]: v[j] for k, v in p.items() if k.startswith('ga_')})
        elif mixer == 2:
            x = _mlstm_layer(x, batch, seq_len, g, {k[3:]: v[j] for k, v in p.items() if k.startswith('ml_')})
        else:
            x = _dilated_layer(x, batch, seq_len, g, {k[3:]: v[j] for k, v in p.items() if k.startswith('da_')})
        x = _moe_layer(x, group_sizes, p['norm_gain'][i, 1], p['moe_w_router'][i],
                       p['moe_w1'][i], p['moe_w3'][i], p['moe_w2'][i])
    return x


def kernel(x_prompt, x_sample, norm_gain, hy_w_in, hy_b_in, hy_conv_w, hy_conv_b, hy_f_w1, hy_f_b1, hy_f_w2, hy_f_b2, hy_f_w3, hy_f_b3, hy_f_freq, hy_decay, hy_skip, hy_w_out, hy_b_out, ga_w_qkv, ga_q_gain, ga_k_gain, ga_w_o, ml_w_up, ml_conv_w, ml_conv_b, ml_w_q, ml_w_k, ml_w_v, ml_w_gate, ml_b_gate, ml_norm_gain, ml_skip, ml_w_down, da_w_qkv, da_q_gain, da_k_gain, da_w_o, moe_w_router, moe_w1, moe_w3, moe_w2):
    p = dict(
        norm_gain=norm_gain,
        hy_w_in=hy_w_in, hy_b_in=hy_b_in, hy_conv_w=hy_conv_w, hy_conv_b=hy_conv_b,
        hy_f_w1=hy_f_w1, hy_f_b1=hy_f_b1, hy_f_w2=hy_f_w2, hy_f_b2=hy_f_b2,
        hy_f_w3=hy_f_w3, hy_f_b3=hy_f_b3, hy_f_freq=hy_f_freq, hy_decay=hy_decay,
        hy_skip=hy_skip, hy_w_out=hy_w_out, hy_b_out=hy_b_out,
        ga_w_qkv=ga_w_qkv, ga_q_gain=ga_q_gain, ga_k_gain=ga_k_gain, ga_w_o=ga_w_o,
        ml_w_up=ml_w_up, ml_conv_w=ml_conv_w, ml_conv_b=ml_conv_b, ml_w_q=ml_w_q,
        ml_w_k=ml_w_k, ml_w_v=ml_w_v, ml_w_gate=ml_w_gate, ml_b_gate=ml_b_gate,
        ml_norm_gain=ml_norm_gain, ml_skip=ml_skip, ml_w_down=ml_w_down,
        da_w_qkv=da_w_qkv, da_q_gain=da_q_gain, da_k_gain=da_k_gain, da_w_o=da_w_o,
        moe_w_router=moe_w_router, moe_w1=moe_w1, moe_w3=moe_w3, moe_w2=moe_w2,
    )
    bp, seq_len, d = x_prompt.shape
    bs = x_sample.shape[0]
    assert x_sample.shape[1] == seq_len
    x = jnp.concatenate([x_prompt, x_sample], axis=0).reshape((bp + bs) * seq_len, d)
    y = _trunk(x, bp + bs, seq_len, (bp * seq_len, bs * seq_len), p)
    y = y.reshape(bp + bs, seq_len, d)
    return (y[:bp], y[bp:])
```

```python
import functools
import math

import jax
import jax.numpy as jnp
from jax import lax
from jax.experimental import pallas as pl
from jax.experimental.pallas import tpu as pltpu

F32 = jnp.float32
BF16 = jnp.bfloat16
HIGHEST = lax.Precision.HIGHEST

NORM_EPS = 1e-6
GRID_W = 64
HY_BANDS = 16
GA_HEADS = 8
GA_KV_HEADS = 2
GA_GROUP = GA_HEADS // GA_KV_HEADS
HEAD_DIM = 128
AXIAL_THETA = 10000.0
ML_HEADS = 4
ML_QKV_BLOCK = 4
DA_GROUPS = ((128, 1), (512, 4), (2048, 16))
DA_HEADS_PER_GROUP = 4
DA_HEADS = DA_HEADS_PER_GROUP * len(DA_GROUPS)
ROPE_THETA = 500000.0
ROPE_DIMS = HEAD_DIM // 4
N_EXPERTS = 16
EC_CAPACITY = 2

VMEM_LIMIT_BYTES = 52 * 1024 * 1024
HALO_ROWS = 8
MXU_DIM = 256
ML_CHUNK = 256
BAND_BLOCK = 128
BAND_HALF = 64


def _params(*sem):
    return pltpu.CompilerParams(dimension_semantics=sem, vmem_limit_bytes=VMEM_LIMIT_BYTES)


def _rms(x, g):
    ms = jnp.mean(x * x, axis=-1, keepdims=True)
    return x * lax.rsqrt(ms + NORM_EPS) * g


def _const_spec(shape):
    nd = len(shape)
    return pl.BlockSpec(shape, lambda *_: (0,) * nd)


def _conv3_rows(p, pprev, pnext, cw, cb, rows, tm):
    up = jnp.where(rows == 0, pprev, pltpu.roll(p, 1, 0))
    dn = jnp.where(rows == tm - 1, pnext, pltpu.roll(p, tm - 1, 0))
    return up * cw[0:1] + p * cw[1:2] + dn * cw[2:3] + cb


def _halo_specs(tm, d, n_rows):
    hb = tm // HALO_ROWS
    last = n_rows // HALO_ROWS - 1
    prev = pl.BlockSpec((HALO_ROWS, d), lambda i: (jnp.maximum(i * hb - 1, 0), 0))
    nxt = pl.BlockSpec((HALO_ROWS, d), lambda i: (jnp.minimum((i + 1) * hb, last), 0))
    return prev, nxt


def _edge_scales(i, tm, seq_len):
    t0 = i * tm
    keep_prev = jnp.where(t0 % seq_len == 0, 0.0, 1.0).astype(F32)
    keep_next = jnp.where((t0 + tm) % seq_len == 0, 0.0, 1.0).astype(F32)
    return keep_prev, keep_next


def _mm_res_kernel(*refs, prologue, n_row, n_const, tn):
    row_refs = refs[:n_row]
    const_refs = refs[n_row:n_row + n_const]
    w_ref, b_ref, res_ref, o_ref = refs[n_row + n_const:]
    lhs = prologue(*[r[...] for r in row_refs], *[c[...] for c in const_refs]).astype(BF16)
    for j in range(o_ref.shape[1] // tn):
        sl = slice(j * tn, (j + 1) * tn)
        o_ref[:, sl] = (res_ref[:, sl] + b_ref[:, sl]
                        + jnp.dot(lhs, w_ref[:, sl], preferred_element_type=F32))


def _mm_res(prologue, rows, consts, w, b, res, tm=512, tn=512):
    n, dout = res.shape
    tm = min(tm, n)
    k = w.shape[0]
    if b is None:
        b = jnp.zeros((1, dout), F32)
    in_specs = [pl.BlockSpec((tm, r.shape[1]), lambda i: (i, 0)) for r in rows]
    in_specs += [_const_spec(c.shape) for c in consts]
    in_specs += [_const_spec((k, dout)), _const_spec((1, dout)),
                 pl.BlockSpec((tm, dout), lambda i: (i, 0))]
    return pl.pallas_call(
        functools.partial(_mm_res_kernel, prologue=prologue, n_row=len(rows),
                          n_const=len(consts), tn=min(tn, dout)),
        out_shape=jax.ShapeDtypeStruct((n, dout), F32),
        grid=(n // tm,),
        in_specs=in_specs,
        out_specs=pl.BlockSpec((tm, dout), lambda i: (i, 0)),
        compiler_params=_params("parallel"),
    )(*rows, *consts, w.astype(BF16), b.reshape(1, dout).astype(F32), res)


def _hyena_in_kernel(x_ref, xp_ref, xn_ref, g_ref, w_ref, b_ref, cw_ref, cb_ref,
                     x0_ref, vx_ref, *, seq_len, tm, d, cols):
    keep_prev, keep_next = _edge_scales(pl.program_id(0), tm, seq_len)
    g = g_ref[...]
    xb = _rms(x_ref[...], g).astype(BF16)
    hb = _rms(jnp.concatenate([xp_ref[...], xn_ref[...]], axis=0), g).astype(BF16)
    rows = lax.broadcasted_iota(jnp.int32, (tm, 1), 0)

    def conv_part(c0):
        sl = slice(c0, c0 + cols)
        w = w_ref[:, sl]
        bias = b_ref[:, sl]
        p = jnp.dot(xb, w, preferred_element_type=F32) + bias
        ph = jnp.dot(hb, w, preferred_element_type=F32) + bias
        pprev = ph[HALO_ROWS - 1:HALO_ROWS, :] * keep_prev
        pnext = ph[HALO_ROWS:HALO_ROWS + 1, :] * keep_next
        return _conv3_rows(p, pprev, pnext, cw_ref[:, sl], cb_ref[:, sl], rows, tm)

    for j in range(d // cols):
        c = j * cols
        x0_ref[:, c:c + cols] = conv_part(c)
        vx_ref[:, c:c + cols] = conv_part(2 * d + c) * conv_part(d + c)


def _hyena_in(x, g, w_in, b_in, conv_w, conv_b, seq_len, tm=256, cols=512):
    n, d = x.shape
    tm = min(tm, seq_len)
    prev, nxt = _halo_specs(tm, d, n)
    row = pl.BlockSpec((tm, d), lambda i: (i, 0))
    return pl.pallas_call(
        functools.partial(_hyena_in_kernel, seq_len=seq_len, tm=tm, d=d, cols=cols),
        out_shape=(jax.ShapeDtypeStruct((n, d), F32), jax.ShapeDtypeStruct((n, d), F32)),
        grid=(n // tm,),
        in_specs=[row, prev, nxt, _const_spec((1, d)), _const_spec((d, 3 * d)),
                  _const_spec((1, 3 * d)), _const_spec((3, 3 * d)), _const_spec((1, 3 * d))],
        out_specs=(row, row),
        compiler_params=_params("parallel"),
    )(x, x, x, g.reshape(1, d), w_in.astype(BF16), b_in.reshape(1, 3 * d),
      conv_w, conv_b.reshape(1, 3 * d))


def _hyena_filter_taps(L, d, f_w1, f_b1, f_w2, f_b2, f_w3, f_b3, f_freq, decay):
    t = jnp.linspace(0.0, 1.0, L, dtype=F32)[:, None]
    w_ang = 2.0 * math.pi * jnp.arange(L, dtype=F32)[:, None] / L
    bands = jnp.linspace(1e-4, HY_BANDS - 1, HY_BANDS, dtype=F32)[None, :]
    z = jnp.concatenate([t, jnp.cos(bands * w_ang), -jnp.sin(bands * w_ang)], axis=-1)
    h = jnp.sin(f_freq[0] * (z @ f_w1 + f_b1))
    h = jnp.sin(f_freq[1] * (h @ f_w2 + f_b2))
    h = h @ f_w3 + f_b3
    h = h.reshape(L, 2, d) * jnp.exp(-t[:, :, None] * jnp.abs(decay)[None])
    k = jnp.concatenate([h[:, 0], jnp.zeros((1, d), F32), h[1:, 1][::-1]], axis=0)
    return k / jnp.sum(jnp.abs(k), axis=0, keepdims=True)


def _fft_dims(m):
    lg = m.bit_length() - 1
    p = 1 << ((lg + 1) // 2)
    return p, m // p


def _split_bf16(x):
    hi = x.astype(BF16)
    return hi, (x - hi.astype(F32)).astype(BF16)


def _mm_split(fh, fl, x, precise):
    if not precise:
        return jnp.dot(fh, x.astype(BF16), preferred_element_type=F32)
    xh, xl = _split_bf16(x)
    return (jnp.dot(fh, xh, preferred_element_type=F32) + jnp.dot(fl, xh, preferred_element_type=F32)
            + jnp.dot(fh, xl, preferred_element_type=F32))


def _cplx_as_real(cr, ci):
    top = jnp.concatenate([cr, -ci], axis=-1)
    bot = jnp.concatenate([ci, cr], axis=-1)
    return jnp.concatenate([top, bot], axis=-2)


def _unit_circle(idx, m):
    ang = (2.0 * math.pi / m) * idx.astype(F32)
    return jnp.cos(ang), jnp.sin(ang)


def _dft_consts(p, q):
    m = p * q
    k1 = jnp.arange(p, dtype=jnp.int32)
    n1 = jnp.arange(p // 2, dtype=jnp.int32)
    c, s = _unit_circle((k1[:, None] * n1[None, :]) % p, p)
    fa = _cplx_as_real(c, -s)
    c, s = _unit_circle((n1[:, None] * k1[None, :]) % p, p)
    fd = _cplx_as_real(c / m, s / m)
    k2 = jnp.arange(q, dtype=jnp.int32)
    n2 = jnp.arange(q, dtype=jnp.int32)
    idx = (n2[None, None, :] * (k2[None, :, None] * p + k1[:, None, None])) % m
    c, s = _unit_circle(idx, m)
    gb = _cplx_as_real(c, -s)
    ct, st = jnp.swapaxes(c, 1, 2), jnp.swapaxes(s, 1, 2)
    gc = _cplx_as_real(ct, st)
    return tuple(_split_bf16(a) for a in (fa, gb, gc, fd))


def _fft_a_kernel(x_ref, fh_ref, fl_ref, o_ref, *, precise):
    _, half, d = x_ref.shape
    y = _mm_split(fh_ref[...], fl_ref[...], x_ref[...].reshape(2 * half, d), precise)
    o_ref[...] = y.reshape(o_ref.shape)


def _fft_b_kernel(a_ref, gh_ref, gl_ref, o_ref):
    _, q, d = a_ref.shape
    x = _mm_split(gh_ref[...], gl_ref[...], a_ref[...].reshape(2 * q, d), True)
    o_ref[...] = x.reshape(o_ref.shape)


def _fft_bc_kernel(a_ref, gb_ref, gc_ref, k_ref, z_ref):
    _, q, d = a_ref.shape
    x = _mm_split(gb_ref[...], None, a_ref[...].reshape(2 * q, d), False)
    xr, xi = x[:q], x[q:]
    kr, ki = k_ref[0], k_ref[1]
    y = jnp.concatenate([xr * kr - xi * ki, xr * ki + xi * kr], axis=0)
    z = _mm_split(gc_ref[...], None, y, False)
    z_ref[...] = z.reshape(z_ref.shape)


def _fft_stage_a(x4, fa, p, q, d, precise):
    pairs = x4.shape[0]
    return pl.pallas_call(
        functools.partial(_fft_a_kernel, precise=precise),
        out_shape=jax.ShapeDtypeStruct((pairs, 2, p, q * d), F32),
        grid=(pairs, q),
        in_specs=[pl.BlockSpec((None, 2, p // 2, d), lambda b, j: (b, 0, 0, j)),
                  _const_spec((2 * p, p)), _const_spec((2 * p, p))],
        out_specs=pl.BlockSpec((None, 2, p, d), lambda b, j: (b, 0, 0, j)),
        compiler_params=_params("parallel", "parallel"),
    )(x4, *fa)


def _long_conv(vx, taps, batch, seq_len):
    n, d = vx.shape
    m = 2 * seq_len
    p, q = _fft_dims(m)
    assert batch % 2 == 0
    pairs = batch // 2
    fa, gb, gc, fd = _dft_consts(p, q)
    g_spec = pl.BlockSpec((None, 2 * q, 2 * q), lambda k1, b: (k1, 0, 0))
    slab = pl.BlockSpec((None, 2, q, d), lambda k1, b: (b, 0, k1, 0))

    zeros = jnp.zeros((seq_len, d), F32)
    kin = jnp.stack([taps[:seq_len], zeros, taps[seq_len:], zeros]).reshape(2, 2, p // 2, q * d)
    ka = _fft_stage_a(kin, fa, p, q, d, True).reshape(2, 2, p * q, d)
    kx = pl.pallas_call(
        _fft_b_kernel,
        out_shape=jax.ShapeDtypeStruct((2, 2, p * q, d), F32),
        grid=(p, 2),
        in_specs=[slab, g_spec, g_spec],
        out_specs=slab,
        compiler_params=_params("parallel", "parallel"),
    )(ka, *gb)
    sign = jnp.repeat(1.0 - 2.0 * (jnp.arange(p) % 2).astype(F32), q)[None, :, None]
    kspec = kx[0] + sign * kx[1]

    xa = _fft_stage_a(vx.reshape(pairs, 2, p // 2, q * d), fa, p, q, d, False).reshape(pairs, 2, p * q, d)
    z = pl.pallas_call(
        _fft_bc_kernel,
        out_shape=jax.ShapeDtypeStruct((pairs, 2, p * q, d), F32),
        grid=(p, pairs),
        in_specs=[slab, g_spec, g_spec, pl.BlockSpec((2, q, d), lambda k1, b: (0, k1, 0))],
        out_specs=slab,
        compiler_params=_params("parallel", "parallel"),
    )(xa, gb[0], gc[0], kspec)
    y = pl.pallas_call(
        functools.partial(_fft_a_kernel, precise=False),
        out_shape=jax.ShapeDtypeStruct((pairs, 2, p // 2, q * d), F32),
        grid=(pairs, q),
        in_specs=[pl.BlockSpec((None, 2, p, d), lambda b, j: (b, 0, 0, j)),
                  _const_spec((p, 2 * p)), _const_spec((p, 2 * p))],
        out_specs=pl.BlockSpec((None, 2, p // 2, d), lambda b, j: (b, 0, 0, j)),
        compiler_params=_params("parallel", "parallel"),
    )(z.reshape(pairs, 2, p, q * d), *fd)
    return y.reshape(n, d)


def _hyena_out_prologue(y, vx, x0, skip):
    return (y + vx * skip) * x0


def _hyena_layer(x, batch, seq_len, g, p):
    n, d = x.shape
    x0, vx = _hyena_in(x, g, p['w_in'], p['b_in'], p['conv_w'], p['conv_b'], seq_len)
    taps = _hyena_filter_taps(seq_len, d, p['f_w1'], p['f_b1'], p['f_w2'], p['f_b2'],
                              p['f_w3'], p['f_b3'], p['f_freq'], p['decay'])
    y = _long_conv(vx, taps, batch, seq_len)
    return _mm_res(_hyena_out_prologue, [y, vx, x0], [p['skip'].reshape(1, d)],
                   p['w_out'], p['b_out'], x)


def _head_norm_rope(xh, gain, cos, sin, half):
    y = _rms(xh, gain)
    lane = lax.broadcasted_iota(jnp.int32, (1, HEAD_DIM), 1)
    fwd = pltpu.roll(y, HEAD_DIM - half, 1)
    bwd = pltpu.roll(y, half, 1)
    partner = jnp.where((lane % (2 * half)) < half, fwd, bwd)
    return y * cos + partner * sin


def _qkv_rope_kernel(x_ref, g_ref, w_ref, qg_ref, kg_ref, cos_ref, sin_ref,
                     q_ref, k_ref, v_ref, *, nq, nk, nv, half):
    xb = _rms(x_ref[...], g_ref[...]).astype(BF16)
    cos = cos_ref[...]
    sin = sin_ref[...]
    per = MXU_DIM // HEAD_DIM
    for h0 in range(0, nq + nk + nv, per):
        pw = jnp.dot(xb, w_ref[:, h0 * HEAD_DIM:(h0 + per) * HEAD_DIM], preferred_element_type=F32)
        for h in range(h0, h0 + per):
            ph = pw[:, (h - h0) * HEAD_DIM:(h - h0 + 1) * HEAD_DIM]
            if h < nq:
                sl = slice(h * HEAD_DIM, (h + 1) * HEAD_DIM)
                q_ref[:, sl] = _head_norm_rope(ph, qg_ref[...], cos, sin, half).astype(BF16)
            elif h < nq + nk:
                sl = slice((h - nq) * HEAD_DIM, (h - nq + 1) * HEAD_DIM)
                k_ref[:, sl] = _head_norm_rope(ph, kg_ref[...], cos, sin, half).astype(BF16)
            else:
                sl = slice((h - nq - nk) * HEAD_DIM, (h - nq - nk + 1) * HEAD_DIM)
                v_ref[:, sl] = ph.astype(BF16)


def _qkv_rope(x, g, w_qkv, q_gain, k_gain, cos, sin, nq, nk, nv, half, seq_len, tm=512):
    n, d = x.shape
    tm = min(tm, seq_len)
    pos_blocks = seq_len // tm
    f = w_qkv.shape[1]
    row = pl.BlockSpec((tm, d), lambda i: (i, 0))
    tab = pl.BlockSpec((tm, HEAD_DIM), lambda i: (i % pos_blocks, 0))
    outs = tuple(jax.ShapeDtypeStruct((n, c * HEAD_DIM), BF16) for c in (nq, nk, nv))
    return pl.pallas_call(
        functools.partial(_qkv_rope_kernel, nq=nq, nk=nk, nv=nv, half=half),
        out_shape=outs,
        grid=(n // tm,),
        in_specs=[row, _const_spec((1, d)), _const_spec((d, f)),
                  _const_spec((1, HEAD_DIM)), _const_spec((1, HEAD_DIM)), tab, tab],
        out_specs=tuple(pl.BlockSpec((tm, c * HEAD_DIM), lambda i: (i, 0)) for c in (nq, nk, nv)),
        compiler_params=_params("parallel"),
    )(x, g.reshape(1, d), w_qkv.astype(BF16), q_gain.reshape(1, HEAD_DIM),
      k_gain.reshape(1, HEAD_DIM), cos, sin)


def _axial_tables(L):
    t = jnp.arange(L)
    r = (t // GRID_W).astype(F32)
    c = (t % GRID_W).astype(F32)
    nf = HEAD_DIM // 4
    inv = AXIAL_THETA ** (-(2.0 * jnp.arange(nf, dtype=F32)) / (2 * nf))
    ar, ac = r[:, None] * inv[None], c[:, None] * inv[None]
    cos = jnp.concatenate([jnp.cos(ar), jnp.cos(ar), jnp.cos(ac), jnp.cos(ac)], axis=-1)
    sin = jnp.concatenate([-jnp.sin(ar), jnp.sin(ar), -jnp.sin(ac), jnp.sin(ac)], axis=-1)
    return cos, sin


FLASH_SAFE_BOUND = 40.0
FLASH_BOUND_MARGIN = 1.001


def _flash_kernel(q_ref, k_ref, v_ref, o_ref, qx_scr, acc_scr, m_scr, l_scr, kn_scr,
                  *, tq, tk, tkf, seq_len, group):
    scale = HEAD_DIM ** -0.5
    c = scale * math.log2(math.e)
    ext = 2 * HEAD_DIM

    @pl.when(pl.program_id(2) == 0)
    def _():
        def norm_step(j, mx):
            kt = k_ref[pl.ds(pl.multiple_of(j * tk, tk), tk), 0:HEAD_DIM].astype(F32)
            row = jnp.sum(kt * kt, axis=1, keepdims=True)
            return jnp.maximum(mx, jnp.max(row, axis=0, keepdims=True))

        kmax2 = lax.fori_loop(0, seq_len // tk, norm_step, jnp.zeros((1, 1), F32))
        kn_scr[...] = jnp.broadcast_to(kmax2, kn_scr.shape)

    kmax2 = kn_scr[0:1, 0:1]
    lane = lax.broadcasted_iota(jnp.int32, (tq, HEAD_DIM), 1)
    bmax = jnp.zeros((1, 1), F32)
    for h in range(group):
        q = q_ref[:, h * HEAD_DIM:(h + 1) * HEAD_DIM]
        qf = q.astype(F32)
        b = jnp.sqrt(jnp.sum(qf * qf, axis=1, keepdims=True) * kmax2) * FLASH_BOUND_MARGIN
        bmax = jnp.maximum(bmax, jnp.max(b, axis=0, keepdims=True))
        qx_scr[h, :, 0:HEAD_DIM] = q
        qx_scr[h, :, HEAD_DIM:ext] = jnp.where(lane == 0, -b, 0.0).astype(BF16)
    fast = (bmax * scale)[0, 0] <= FLASH_SAFE_BOUND

    @pl.when(fast)
    def _():
        acc_scr[...] = jnp.zeros_like(acc_scr)

        def body(j, carry):
            start = pl.multiple_of(j * tkf, tkf)
            kt = k_ref[pl.ds(start, tkf), :]
            vt = v_ref[pl.ds(start, tkf), :]
            for h in range(group):
                s = lax.dot_general(qx_scr[h], kt, (((1,), (1,)), ((), ())),
                                    preferred_element_type=F32)
                p = jnp.exp2(s * c)
                acc_scr[h] += jnp.dot(p.astype(BF16), vt, preferred_element_type=F32)
            return carry

        lax.fori_loop(0, seq_len // tkf, body, 0)
        for h in range(group):
            a = acc_scr[h]
            o_ref[:, h * HEAD_DIM:(h + 1) * HEAD_DIM] = (a[:, 0:HEAD_DIM] / a[:, HEAD_DIM:ext]).astype(o_ref.dtype)

    @pl.when(jnp.logical_not(fast))
    def _():
        m_scr[...] = jnp.full(m_scr.shape, -jnp.inf, F32)
        l_scr[...] = jnp.zeros_like(l_scr)
        acc_scr[...] = jnp.zeros_like(acc_scr)

        def body(j, carry):
            start = pl.multiple_of(j * tk, tk)
            kt = k_ref[pl.ds(start, tk), 0:HEAD_DIM]
            vt = v_ref[pl.ds(start, tk), 0:HEAD_DIM]
            for h in range(group):
                q = q_ref[:, h * HEAD_DIM:(h + 1) * HEAD_DIM]
                s = lax.dot_general(q, kt, (((1,), (1,)), ((), ())), preferred_element_type=F32)
                m = m_scr[h]
                m_new = jnp.maximum(m, jnp.max(s, axis=1, keepdims=True))
                alpha = jnp.exp2((m - m_new) * c)
                p = jnp.exp2(s * c - m_new * c)
                l_scr[h] = alpha * l_scr[h] + jnp.sum(p, axis=1, keepdims=True)
                acc_scr[h, :, 0:HEAD_DIM] = (alpha * acc_scr[h, :, 0:HEAD_DIM]
                                             + jnp.dot(p.astype(BF16), vt, preferred_element_type=F32))
                m_scr[h] = m_new
            return carry

        lax.fori_loop(0, seq_len // tk, body, 0)
        for h in range(group):
            o_ref[:, h * HEAD_DIM:(h + 1) * HEAD_DIM] = (acc_scr[h, :, 0:HEAD_DIM] / l_scr[h]).astype(o_ref.dtype)


def _extend_heads(a, heads, fill_col0_only):
    n = a.shape[0]
    if fill_col0_only:
        pad = jnp.zeros((n, heads, HEAD_DIM), a.dtype).at[:, :, 0].set(1)
    else:
        pad = jnp.ones((n, heads, HEAD_DIM), a.dtype)
    return jnp.concatenate([a.reshape(n, heads, HEAD_DIM), pad], axis=-1).reshape(n, heads * 2 * HEAD_DIM)


def _flash_gqa(q, k, v, batch, seq_len, tq=256, tk=512, tkf=1024):
    tq = min(tq, seq_len)
    tk = min(tk, seq_len)
    tkf = min(tkf, seq_len)
    nq = seq_len // tq
    gw = GA_GROUP * HEAD_DIM
    ext = 2 * HEAD_DIM
    kx = _extend_heads(k, GA_KV_HEADS, True)
    vx = _extend_heads(v, GA_KV_HEADS, False)
    return pl.pallas_call(
        functools.partial(_flash_kernel, tq=tq, tk=tk, tkf=tkf, seq_len=seq_len, group=GA_GROUP),
        out_shape=jax.ShapeDtypeStruct(q.shape, BF16),
        grid=(batch, GA_KV_HEADS, nq),
        in_specs=[pl.BlockSpec((tq, gw), lambda b, kv, i: (b * nq + i, kv)),
                  pl.BlockSpec((seq_len, ext), lambda b, kv, i: (b, kv)),
                  pl.BlockSpec((seq_len, ext), lambda b, kv, i: (b, kv))],
        out_specs=pl.BlockSpec((tq, gw), lambda b, kv, i: (b * nq + i, kv)),
        scratch_shapes=[pltpu.VMEM((GA_GROUP, tq, ext), BF16), pltpu.VMEM((GA_GROUP, tq, ext), F32),
                        pltpu.VMEM((GA_GROUP, tq, 1), F32), pltpu.VMEM((GA_GROUP, tq, 1), F32),
                        pltpu.VMEM((HALO_ROWS, HEAD_DIM), F32)],
        compiler_params=_params("parallel", "parallel", "arbitrary"),
    )(q, kx, vx)


def _identity_prologue(o):
    return o


def _gqa_layer(x, batch, seq_len, g, p):
    cos, sin = _axial_tables(seq_len)
    q, k, v = _qkv_rope(x, g, p['w_qkv'], p['q_gain'], p['k_gain'], cos, sin,
                        GA_HEADS, GA_KV_HEADS, GA_KV_HEADS, HEAD_DIM // 4, seq_len)
    o = _flash_gqa(q, k, v, batch, seq_len)
    return _mm_res(_identity_prologue, [o], [], p['w_o'], None, x)


def _ml_in_kernel(x_ref, xp_ref, xn_ref, g_ref, w_ref, cw_ref, cb_ref, wq_ref, wk_ref, wv_ref,
                  wg_ref, bg_ref, q_ref, k_ref, v_ref, xc_ref, sz_ref, gate_ref,
                  *, seq_len, tm, inner, k_scale):
    keep_prev, keep_next = _edge_scales(pl.program_id(0), tm, seq_len)
    g = g_ref[...]
    xb = _rms(x_ref[...], g).astype(BF16)
    hb = _rms(jnp.concatenate([xp_ref[...], xn_ref[...]], axis=0), g).astype(BF16)
    rows = lax.broadcasted_iota(jnp.int32, (tm, 1), 0)
    gacc = jnp.zeros(gate_ref.shape, F32)
    for t in range(inner // MXU_DIM):
        sl = slice(t * MXU_DIM, (t + 1) * MXU_DIM)
        w = w_ref[:, sl]
        xm = jnp.dot(xb, w, preferred_element_type=F32)
        xh = jnp.dot(hb, w, preferred_element_type=F32)
        pprev = xh[HALO_ROWS - 1:HALO_ROWS, :] * keep_prev
        pnext = xh[HALO_ROWS:HALO_ROWS + 1, :] * keep_next
        xc = _conv3_rows(xm, pprev, pnext, cw_ref[:, sl], cb_ref[:, sl], rows, tm)
        xc = xc * jax.nn.sigmoid(xc)
        z = jnp.dot(xb, w_ref[:, inner + t * MXU_DIM:inner + (t + 1) * MXU_DIM],
                    preferred_element_type=F32)
        xcb = xc.astype(BF16)
        q = jnp.dot(xcb, wq_ref[t], preferred_element_type=F32)
        k = jnp.dot(xcb, wk_ref[t], preferred_element_type=F32)
        v = jnp.dot(xm.astype(BF16), wv_ref[t], preferred_element_type=F32)
        qb, kb, vb = q.astype(BF16), k.astype(BF16), v.astype(BF16)
        gacc += (jnp.dot(qb, wg_ref[0, sl, :], preferred_element_type=F32)
                 + jnp.dot(kb, wg_ref[1, sl, :], preferred_element_type=F32)
                 + jnp.dot(vb, wg_ref[2, sl, :], preferred_element_type=F32))
        q_ref[:, sl] = qb
        k_ref[:, sl] = (k * k_scale).astype(BF16)
        v_ref[:, sl] = vb
        xc_ref[:, sl] = xc
        sz_ref[:, sl] = z * jax.nn.sigmoid(z)
    gate_ref[...] = gacc + bg_ref[...]


def _block_diag_tiles(w):
    nb, c, _ = w.shape
    per = MXU_DIM // c
    wt = w.reshape(nb // per, per, c, c)
    eye = jnp.eye(per, dtype=w.dtype)
    full = jnp.einsum('tpcd,pq->tpcqd', wt, eye)
    return full.reshape(nb // per, MXU_DIM, MXU_DIM)


def _ml_in(x, g, p, seq_len, tm=256):
    n, d = x.shape
    tm = min(tm, seq_len)
    inner = p['w_up'].shape[1] // 2
    ng = 4 * ML_HEADS
    dh = inner // ML_HEADS
    wq, wk, wv = (_block_diag_tiles(p[nm]).astype(BF16) for nm in ('w_q', 'w_k', 'w_v'))
    wg = jnp.transpose(p['w_gate'], (1, 2, 0, 3)).reshape(3, inner, ng).astype(BF16)
    bg = p['b_gate'].reshape(1, ng)
    prev, nxt = _halo_specs(tm, d, n)
    row = pl.BlockSpec((tm, d), lambda i: (i, 0))
    wide = pl.BlockSpec((tm, inner), lambda i: (i, 0))
    nt = inner // MXU_DIM
    return pl.pallas_call(
        functools.partial(_ml_in_kernel, seq_len=seq_len, tm=tm, inner=inner, k_scale=dh ** -0.5),
        out_shape=(jax.ShapeDtypeStruct((n, inner), BF16),) * 3
        + (jax.ShapeDtypeStruct((n, inner), F32),) * 2
        + (jax.ShapeDtypeStruct((n, ng), F32),),
        grid=(n // tm,),
        in_specs=[row, prev, nxt, _const_spec((1, d)), _const_spec((d, 2 * inner)),
                  _const_spec((3, inner)), _const_spec((1, inner)),
                  _const_spec((nt, MXU_DIM, MXU_DIM)), _const_spec((nt, MXU_DIM, MXU_DIM)),
                  _const_spec((nt, MXU_DIM, MXU_DIM)), _const_spec((3, inner, ng)),
                  _const_spec((1, ng))],
        out_specs=(wide,) * 5 + (pl.BlockSpec((tm, ng), lambda i: (i, 0)),),
        compiler_params=_params("parallel"),
    )(x, x, x, g.reshape(1, d), p['w_up'].astype(BF16), p['conv_w'],
      p['conv_b'].reshape(1, inner), wq, wk, wv, wg, bg)


def _log_sigmoid(x):
    return jnp.minimum(x, 0.0) - jnp.log1p(jnp.exp(-jnp.abs(x)))


def _mlstm_chunk_kernel(q_ref, k_ref, v_ref, gc_ref, gr_ref, h_ref, c_scr, n_scr, m_scr, *, lc):
    d = pl.program_id(2)

    @pl.when(pl.program_id(3) == 0)
    def _():
        c_scr[...] = jnp.zeros_like(c_scr)
        n_scr[...] = jnp.zeros_like(n_scr)
        m_scr[...] = jnp.zeros_like(m_scr)

    q = q_ref[...]
    k = k_ref[...]
    v = v_ref[...]
    gc = gc_ref[...]
    gr = gr_ref[...]
    i_col, g_col = gc[:, 0:1], gc[:, 1:2]
    i_row, g_row = gr[0:1, :], gr[1:2, :]

    jr = lax.broadcasted_iota(jnp.int32, (lc, lc), 0)
    sc = lax.broadcasted_iota(jnp.int32, (lc, lc), 1)
    seen = ((sc - jr) * (1 - 2 * d)) <= 0
    g_tot = jnp.where(d == 0, g_row[:, lc - 1:lc], g_row[:, 0:1])
    m_old = m_scr[0:1, 0:1]

    dmat = jnp.where(seen, g_col - g_row + i_row, -jnp.inf)
    inter = g_col + m_old
    m_q = jnp.maximum(inter, jnp.max(dmat, axis=1, keepdims=True))
    s_qk = lax.dot_general(q, k, (((1,), (1,)), ((), ())), preferred_element_type=F32)
    a = s_qk * jnp.exp(dmat - m_q)
    w_int = jnp.exp(inter - m_q)
    q_c = jnp.dot(q, c_scr[...].astype(BF16), preferred_element_type=F32)
    num = jnp.dot(a.astype(BF16), v, preferred_element_type=F32) + q_c * w_int
    q_n = jnp.sum(q.astype(F32) * n_scr[...], axis=1, keepdims=True)
    den = jnp.sum(a, axis=1, keepdims=True) + w_int * q_n
    den = jnp.maximum(jnp.abs(den), jnp.exp(-m_q))
    h_ref[...] = num / den

    a_row = g_tot - g_row + i_row
    m_new = jnp.maximum(g_tot + m_old, jnp.max(a_row, axis=1, keepdims=True))
    ws_col = jnp.exp(g_tot - g_col + i_col - m_new)
    dec = jnp.exp(g_tot + m_old - m_new)
    kw = k.astype(F32) * ws_col
    upd = lax.dot_general(kw.astype(BF16), v, (((0,), (0,)), ((), ())), preferred_element_type=F32)
    c_scr[...] = dec * c_scr[...] + upd
    n_scr[...] = dec * n_scr[...] + jnp.sum(kw, axis=0, keepdims=True)
    m_scr[...] = jnp.broadcast_to(m_new, m_scr.shape)


def _ml_gate_kernel(g_ref, o_ref, *, lc, nh):
    g = g_ref[...]
    jr = lax.broadcasted_iota(jnp.int32, (lc, lc), 0)
    sc = lax.broadcasted_iota(jnp.int32, (lc, lc), 1)
    tril = jnp.where(sc <= jr, 1.0, 0.0).astype(BF16)
    triu = jnp.where(sc >= jr, 1.0, 0.0).astype(BF16)
    col = lax.broadcasted_iota(jnp.int32, (1, 4 * nh), 1)
    lf = _log_sigmoid(g)
    x1 = lf.astype(BF16)
    r1 = lf - x1.astype(F32)
    x2 = r1.astype(BF16)
    x3 = (r1 - x2.astype(F32)).astype(BF16)
    cum = lambda m: (jnp.dot(m, x1, preferred_element_type=F32) + jnp.dot(m, x2, preferred_element_type=F32)
                     + jnp.dot(m, x3, preferred_element_type=F32))
    gsum = jnp.where(col < 2 * nh, cum(tril), cum(triu))
    o_ref[...] = jnp.where((col % (2 * nh)) >= nh, gsum, g)


def _mlstm_chunks(q, k, v, gates, batch, seq_len):
    n, inner = q.shape
    dh = inner // ML_HEADS
    lc = min(ML_CHUNK, seq_len)
    nc = seq_len // lc
    ng = gates.shape[1]
    gates = pl.pallas_call(
        functools.partial(_ml_gate_kernel, lc=lc, nh=ML_HEADS),
        out_shape=jax.ShapeDtypeStruct((n, ng), F32),
        grid=(n // lc,),
        in_specs=[pl.BlockSpec((lc, ng), lambda i: (i, 0))],
        out_specs=pl.BlockSpec((lc, ng), lambda i: (i, 0)),
        compiler_params=_params("parallel"),
    )(gates)
    g4 = gates.reshape(n, 2, 2, ML_HEADS)
    gcol = jnp.transpose(g4, (1, 3, 0, 2))
    grow = jnp.transpose(g4, (1, 3, 2, 0))

    def chunk(b, c, dd):
        return b * nc + c + dd * (nc - 1 - 2 * c)

    qkv_spec = pl.BlockSpec((lc, dh), lambda b, h, dd, c: (chunk(b, c, dd), h))
    return pl.pallas_call(
        functools.partial(_mlstm_chunk_kernel, lc=lc),
        out_shape=jax.ShapeDtypeStruct((2, n, inner), F32),
        grid=(batch, ML_HEADS, 2, nc),
        in_specs=[qkv_spec, qkv_spec, qkv_spec,
                  pl.BlockSpec((None, None, lc, 2), lambda b, h, dd, c: (dd, h, chunk(b, c, dd), 0)),
                  pl.BlockSpec((None, None, 2, lc), lambda b, h, dd, c: (dd, h, 0, chunk(b, c, dd)))],
        out_specs=pl.BlockSpec((None, lc, dh), lambda b, h, dd, c: (dd, chunk(b, c, dd), h)),
        scratch_shapes=[pltpu.VMEM((dh, dh), F32), pltpu.VMEM((1, dh), F32),
                        pltpu.VMEM((HALO_ROWS, HEAD_DIM), F32)],
        compiler_params=_params("parallel", "parallel", "parallel", "arbitrary"),
    )(q, k, v, gcol, grow)


def _ml_out_prologue(hf, hb, xc, sz, gain, skip):
    h = hf + hb
    dh = h.shape[1] // ML_HEADS
    parts = []
    for i in range(ML_HEADS):
        sl = slice(i * dh, (i + 1) * dh)
        parts.append(_rms(h[:, sl], gain[:, sl]))
    hn = jnp.concatenate(parts, axis=1)
    return (hn + skip * xc) * sz


def _mlstm_layer(x, batch, seq_len, g, p):
    q, k, v, xc, sz, gates = _ml_in(x, g, p, seq_len)
    inner = q.shape[1]
    hs = _mlstm_chunks(q, k, v, gates, batch, seq_len)
    return _mm_res(_ml_out_prologue, [hs[0], hs[1], xc, sz],
                   [p['norm_gain'].reshape(1, inner), p['skip'].reshape(1, inner)],
                   p['w_down'], None, x, tm=256)


def _rope_tables(L):
    inv = ROPE_THETA ** (-(2.0 * jnp.arange(ROPE_DIMS // 2, dtype=F32)) / ROPE_DIMS)
    ang = jnp.arange(L, dtype=F32)[:, None] * inv[None]
    pad = HEAD_DIM - ROPE_DIMS
    cos = jnp.concatenate([jnp.cos(ang), jnp.cos(ang), jnp.ones((L, pad), F32)], axis=-1)
    sin = jnp.concatenate([-jnp.sin(ang), jnp.sin(ang), jnp.zeros((L, pad), F32)], axis=-1)
    return cos, sin


def _band_kernel(q_ref, kp_ref, kc_ref, kn_ref, vp_ref, vc_ref, vn_ref, o_ref, lse_ref,
                 *, s_len, heads):
    i = pl.program_id(2)
    qb = BAND_BLOCK
    w = qb + 2 * BAND_HALF
    a = lax.broadcasted_iota(jnp.int32, (qb, w), 0)
    c = lax.broadcasted_iota(jnp.int32, (qb, w), 1)
    rel = c - BAND_HALF - a
    key_pos = i * qb - BAND_HALF + c
    valid = (jnp.abs(rel) <= BAND_HALF) & (key_pos >= 0) & (key_pos < s_len)
    scale = HEAD_DIM ** -0.5
    for h in range(heads):
        sl = slice(h * HEAD_DIM, (h + 1) * HEAD_DIM)
        kw = jnp.concatenate([kp_ref[qb - BAND_HALF:, sl], kc_ref[:, sl], kn_ref[:BAND_HALF, sl]], axis=0)
        vw = jnp.concatenate([vp_ref[qb - BAND_HALF:, sl], vc_ref[:, sl], vn_ref[:BAND_HALF, sl]], axis=0)
        s = lax.dot_general(q_ref[:, sl], kw, (((1,), (1,)), ((), ())),
                            preferred_element_type=F32) * scale
        s = jnp.where(valid, s, -jnp.inf)
        m = jnp.max(s, axis=1, keepdims=True)
        p = jnp.exp(s - m)
        l = jnp.sum(p, axis=1, keepdims=True)
        o = jnp.dot(p.astype(BF16), vw, preferred_element_type=F32)
        o_ref[:, sl] = o / l
        lse_ref[:, sl] = jnp.broadcast_to(m + jnp.log(l), (qb, HEAD_DIM))


def _band_attention(q, k, v, batch, seq_len, gi, dil):
    n = q.shape[0]
    s_len = seq_len // dil
    nb = s_len // BAND_BLOCK
    hw = DA_HEADS_PER_GROUP * HEAD_DIM
    groups = len(DA_GROUPS)
    view = lambda a: a.reshape(batch, s_len, dil * a.shape[1])
    qv, kv, vv = view(q), view(k), view(v)

    def spec(shift):
        return pl.BlockSpec(
            (None, BAND_BLOCK, hw),
            lambda b, r, i: (b, jnp.clip(i + shift, 0, nb - 1), r * groups + gi))

    out_spec = pl.BlockSpec((None, BAND_BLOCK, hw), lambda b, r, i: (b, i, r))
    o, lse = pl.pallas_call(
        functools.partial(_band_kernel, s_len=s_len, heads=DA_HEADS_PER_GROUP),
        out_shape=(jax.ShapeDtypeStruct((batch, s_len, dil * hw), F32),) * 2,
        grid=(batch, dil, nb),
        in_specs=[spec(0), spec(-1), spec(0), spec(1), spec(-1), spec(0), spec(1)],
        out_specs=(out_spec, out_spec),
        compiler_params=_params("parallel", "parallel", "parallel"),
    )(qv, kv, kv, kv, vv, vv, vv)
    return o.reshape(n, hw), lse.reshape(n, hw)


def _da_out_prologue(o0, o1, o2, l0, l1, l2):
    m = jnp.maximum(jnp.maximum(l0, l1), l2)
    e0, e1, e2 = jnp.exp(l0 - m), jnp.exp(l1 - m), jnp.exp(l2 - m)
    return (e0 * o0 + e1 * o1 + e2 * o2) / (e0 + e1 + e2)


def _dilated_layer(x, batch, seq_len, g, p):
    cos, sin = _rope_tables(seq_len)
    q, k, v = _qkv_rope(x, g, p['w_qkv'], p['q_gain'], p['k_gain'], cos, sin,
                        DA_HEADS, DA_HEADS, DA_HEADS, ROPE_DIMS // 2, seq_len, tm=256)
    outs, lses = [], []
    for gi, (_, dil) in enumerate(DA_GROUPS):
        o, lse = _band_attention(q, k, v, batch, seq_len, gi, dil)
        outs.append(o)
        lses.append(lse)
    return _mm_res(_da_out_prologue, outs + lses, [], p['w_o'], None, x)


def _router_kernel(x_ref, g_ref, wr_ref, xn_ref, aff_ref):
    xn = _rms(x_ref[...], g_ref[...])
    xn_ref[...] = xn.astype(BF16)
    logits = lax.dot_general(wr_ref[...], xn, (((1,), (1,)), ((), ())),
                             precision=HIGHEST, preferred_element_type=F32)
    m = jnp.max(logits, axis=0, keepdims=True)
    e = jnp.exp(logits - m)
    aff_ref[...] = e / jnp.sum(e, axis=0, keepdims=True)


def _router(x, g, w_router, tm=512):
    n, d = x.shape
    tm = min(tm, n)
    e = w_router.shape[1]
    return pl.pallas_call(
        _router_kernel,
        out_shape=(jax.ShapeDtypeStruct((n, d), BF16), jax.ShapeDtypeStruct((e, n), F32)),
        grid=(n // tm,),
        in_specs=[pl.BlockSpec((tm, d), lambda i: (i, 0)), _const_spec((1, d)), _const_spec((e, d))],
        out_specs=(pl.BlockSpec((tm, d), lambda i: (i, 0)), pl.BlockSpec((e, tm), lambda i: (0, i))),
        compiler_params=_params("parallel"),
    )(x, g.reshape(1, d), w_router.T)


def _expert_ffn_kernel(xe_ref, gate_ref, w1_ref, w3_ref, w2_ref, ye_ref):
    xe = xe_ref[...]
    h1 = jnp.dot(xe, w1_ref[...], preferred_element_type=F32)
    h3 = jnp.dot(xe, w3_ref[...], preferred_element_type=F32)
    hid = (h1 * jax.nn.sigmoid(h1) * h3).astype(BF16)
    ye_ref[...] = jnp.dot(hid, w2_ref[...], preferred_element_type=F32) * gate_ref[...]


def _expert_ffn(xe, gates, w1, w3, w2, tm=512):
    e, c, d = xe.shape
    f = w1.shape[2]
    tm = min(tm, c)
    return pl.pallas_call(
        _expert_ffn_kernel,
        out_shape=jax.ShapeDtypeStruct((e, c, d), F32),
        grid=(e, c // tm),
        in_specs=[pl.BlockSpec((None, tm, d), lambda ei, ci: (ei, ci, 0)),
                  pl.BlockSpec((None, tm, 1), lambda ei, ci: (ei, ci, 0)),
                  pl.BlockSpec((None, d, f), lambda ei, ci: (ei, 0, 0)),
                  pl.BlockSpec((None, d, f), lambda ei, ci: (ei, 0, 0)),
                  pl.BlockSpec((None, f, d), lambda ei, ci: (ei, 0, 0))],
        out_specs=pl.BlockSpec((None, tm, d), lambda ei, ci: (ei, ci, 0)),
        compiler_params=_params("parallel", "arbitrary"),
    )(xe, gates[..., None], w1, w3, w2)


def _moe_layer(x, group_sizes, g, w_router, w1, w3, w2):
    n, d = x.shape
    xn, aff_t = _router(x, g, w_router)
    xes, gts, idxs = [], [], []
    start = 0
    for ng in group_sizes:
        cap = EC_CAPACITY * ng // N_EXPERTS
        gates, idx = lax.top_k(aff_t[:, start:start + ng], cap)
        idx = idx + start
        xes.append(xn[idx])
        gts.append(gates)
        idxs.append(idx)
        start += ng
    ye = _expert_ffn(jnp.concatenate(xes, axis=1), jnp.concatenate(gts, axis=1),
                     w1.astype(BF16), w3.astype(BF16), w2.astype(BF16))
    idx_all = jnp.concatenate(idxs, axis=1)
    return x.at[idx_all.reshape(-1)].add(ye.reshape(-1, d))


def _trunk(x, batch, seq_len, group_sizes, p):
    depth = p['norm_gain'].shape[0]
    for i in range(depth):
        mixer, j = i % 4, i // 4
        g = p['norm_gain'][i, 0]
        if mixer == 0:
            x = _hyena_layer(x, batch, seq_len, g, {k[3:]: v[j] for k, v in p.items() if k.startswith('hy_')})
        elif mixer == 1:
            x = _gqa_layer(x, batch, seq_len, g, {k[3:]: v[j] for k, v in p.items() if k.startswith('ga_')})
        elif mixer == 2:
            x = _mlstm_layer(x, batch, seq_len, g, {k[3:]: v[j] for k, v in p.items() if k.startswith('ml_')})
        else:
            x = _dilated_layer(x, batch, seq_len, g, {k[3:]: v[j] for k, v in p.items() if k.startswith('da_')})
        x = _moe_layer(x, group_sizes, p['norm_gain'][i, 1], p['moe_w_router'][i],
                       p['moe_w1'][i], p['moe_w3'][i], p['moe_w2'][i])
    return x


def kernel(x_prompt, x_sample, norm_gain, hy_w_in, hy_b_in, hy_conv_w, hy_conv_b, hy_f_w1, hy_f_b1, hy_f_w2, hy_f_b2, hy_f_w3, hy_f_b3, hy_f_freq, hy_decay, hy_skip, hy_w_out, hy_b_out, ga_w_qkv, ga_q_gain, ga_k_gain, ga_w_o, ml_w_up, ml_conv_w, ml_conv_b, ml_w_q, ml_w_k, ml_w_v, ml_w_gate, ml_b_gate, ml_norm_gain, ml_skip, ml_w_down, da_w_qkv, da_q_gain, da_k_gain, da_w_o, moe_w_router, moe_w1, moe_w3, moe_w2):
    p = dict(
        norm_gain=norm_gain,
        hy_w_in=hy_w_in, hy_b_in=hy_b_in, hy_conv_w=hy_conv_w, hy_conv_b=hy_conv_b,
        hy_f_w1=hy_f_w1, hy_f_b1=hy_f_b1, hy_f_w2=hy_f_w2, hy_f_b2=hy_f_b2,
        hy_f_w3=hy_f_w3, hy_f_b3=hy_f_b3, hy_f_freq=hy_f_freq, hy_decay=hy_decay,
        hy_skip=hy_skip, hy_w_out=hy_w_out, hy_b_out=hy_b_out,
        ga_w_qkv=ga_w_qkv, ga_q_gain=ga_q_gain, ga_k_gain=ga_k_gain, ga_w_o=ga_w_o,
        ml_w_up=ml_w_up, ml_conv_w=ml_conv_w, ml_conv_b=ml_conv_b, ml_w_q=ml_w_q,
        ml_w_k=ml_w_k, ml_w_v=ml_w_v, ml_w_gate=ml_w_gate, ml_b_gate=ml_b_gate,
        ml_norm_gain=ml_norm_gain, ml_skip=ml_skip, ml_w_down=ml_w_down,
        da_w_qkv=da_w_qkv, da_q_gain=da_q_gain, da_k_gain=da_k_gain, da_w_o=da_w_o,
        moe_w_router=moe_w_router, moe_w1=moe_w1, moe_w3=moe_w3, moe_w2=moe_w2,
    )
    bp, seq_len, d = x_prompt.shape
    bs = x_sample.shape[0]
    assert x_sample.shape[1] == seq_len
    x = jnp.concatenate([x_prompt, x_sample], axis=0).reshape((bp + bs) * seq_len, d)
    y = _trunk(x, bp + bs, seq_len, (bp * seq_len, bs * seq_len), p)
    y = y.reshape(bp + bs, seq_len, d)
    return (y[:bp], y[bp:])
```

```python
import functools
import math

import jax
import jax.numpy as jnp
from jax import lax
from jax.experimental import pallas as pl
from jax.experimental.pallas import tpu as pltpu

F32 = jnp.float32
BF16 = jnp.bfloat16
HIGHEST = lax.Precision.HIGHEST

NORM_EPS = 1e-6
GRID_W = 64
HY_BANDS = 16
GA_HEADS = 8
GA_KV_HEADS = 2
GA_GROUP = GA_HEADS // GA_KV_HEADS
HEAD_DIM = 128
AXIAL_THETA = 10000.0
ML_HEADS = 4
ML_QKV_BLOCK = 4
DA_GROUPS = ((128, 1), (512, 4), (2048, 16))
DA_HEADS_PER_GROUP = 4
DA_HEADS = DA_HEADS_PER_GROUP * len(DA_GROUPS)
ROPE_THETA = 500000.0
ROPE_DIMS = HEAD_DIM // 4
N_EXPERTS = 16
EC_CAPACITY = 2

VMEM_LIMIT_BYTES = 52 * 1024 * 1024
HALO_ROWS = 8
MXU_DIM = 256
ML_CHUNK = 256
BAND_BLOCK = 128
BAND_HALF = 64


def _params(*sem):
    return pltpu.CompilerParams(dimension_semantics=sem, vmem_limit_bytes=VMEM_LIMIT_BYTES)


def _rms(x, g):
    ms = jnp.mean(x * x, axis=-1, keepdims=True)
    return x * lax.rsqrt(ms + NORM_EPS) * g


def _const_spec(shape):
    nd = len(shape)
    return pl.BlockSpec(shape, lambda *_: (0,) * nd)


def _conv3_rows(p, pprev, pnext, cw, cb, rows, tm):
    up = jnp.where(rows == 0, pprev, pltpu.roll(p, 1, 0))
    dn = jnp.where(rows == tm - 1, pnext, pltpu.roll(p, tm - 1, 0))
    return up * cw[0:1] + p * cw[1:2] + dn * cw[2:3] + cb


def _halo_specs(tm, d, n_rows):
    hb = tm // HALO_ROWS
    last = n_rows // HALO_ROWS - 1
    prev = pl.BlockSpec((HALO_ROWS, d), lambda i: (jnp.maximum(i * hb - 1, 0), 0))
    nxt = pl.BlockSpec((HALO_ROWS, d), lambda i: (jnp.minimum((i + 1) * hb, last), 0))
    return prev, nxt


def _edge_scales(i, tm, seq_len):
    t0 = i * tm
    keep_prev = jnp.where(t0 % seq_len == 0, 0.0, 1.0).astype(F32)
    keep_next = jnp.where((t0 + tm) % seq_len == 0, 0.0, 1.0).astype(F32)
    return keep_prev, keep_next


def _mm_res_kernel(*refs, prologue, n_row, n_const, tn):
    row_refs = refs[:n_row]
    const_refs = refs[n_row:n_row + n_const]
    w_ref, b_ref, res_ref, o_ref = refs[n_row + n_const:]
    lhs = prologue(*[r[...] for r in row_refs], *[c[...] for c in const_refs]).astype(BF16)
    for j in range(o_ref.shape[1] // tn):
        sl = slice(j * tn, (j + 1) * tn)
        o_ref[:, sl] = (res_ref[:, sl] + b_ref[:, sl]
                        + jnp.dot(lhs, w_ref[:, sl], preferred_element_type=F32))


def _mm_res(prologue, rows, consts, w, b, res, tm=512, tn=512):
    n, dout = res.shape
    tm = min(tm, n)
    k = w.shape[0]
    if b is None:
        b = jnp.zeros((1, dout), F32)
    in_specs = [pl.BlockSpec((tm, r.shape[1]), lambda i: (i, 0)) for r in rows]
    in_specs += [_const_spec(c.shape) for c in consts]
    in_specs += [_const_spec((k, dout)), _const_spec((1, dout)),
                 pl.BlockSpec((tm, dout), lambda i: (i, 0))]
    return pl.pallas_call(
        functools.partial(_mm_res_kernel, prologue=prologue, n_row=len(rows),
                          n_const=len(consts), tn=min(tn, dout)),
        out_shape=jax.ShapeDtypeStruct((n, dout), F32),
        grid=(n // tm,),
        in_specs=in_specs,
        out_specs=pl.BlockSpec((tm, dout), lambda i: (i, 0)),
        compiler_params=_params("parallel"),
    )(*rows, *consts, w.astype(BF16), b.reshape(1, dout).astype(F32), res)


def _hyena_in_kernel(x_ref, xp_ref, xn_ref, g_ref, w_ref, b_ref, cw_ref, cb_ref,
                     x0_ref, vx_ref, *, seq_len, tm, d, cols):
    keep_prev, keep_next = _edge_scales(pl.program_id(0), tm, seq_len)
    g = g_ref[...]
    xb = _rms(x_ref[...], g).astype(BF16)
    hb = _rms(jnp.concatenate([xp_ref[...], xn_ref[...]], axis=0), g).astype(BF16)
    rows = lax.broadcasted_iota(jnp.int32, (tm, 1), 0)

    def conv_part(c0):
        sl = slice(c0, c0 + cols)
        w = w_ref[:, sl]
        bias = b_ref[:, sl]
        p = jnp.dot(xb, w, preferred_element_type=F32) + bias
        ph = jnp.dot(hb, w, preferred_element_type=F32) + bias
        pprev = ph[HALO_ROWS - 1:HALO_ROWS, :] * keep_prev
        pnext = ph[HALO_ROWS:HALO_ROWS + 1, :] * keep_next
        return _conv3_rows(p, pprev, pnext, cw_ref[:, sl], cb_ref[:, sl], rows, tm)

    for j in range(d // cols):
        c = j * cols
        x0_ref[:, c:c + cols] = conv_part(c)
        vx_ref[:, c:c + cols] = conv_part(2 * d + c) * conv_part(d + c)


def _hyena_in(x, g, w_in, b_in, conv_w, conv_b, seq_len, tm=256, cols=512):
    n, d = x.shape
    tm = min(tm, seq_len)
    prev, nxt = _halo_specs(tm, d, n)
    row = pl.BlockSpec((tm, d), lambda i: (i, 0))
    return pl.pallas_call(
        functools.partial(_hyena_in_kernel, seq_len=seq_len, tm=tm, d=d, cols=cols),
        out_shape=(jax.ShapeDtypeStruct((n, d), F32), jax.ShapeDtypeStruct((n, d), F32)),
        grid=(n // tm,),
        in_specs=[row, prev, nxt, _const_spec((1, d)), _const_spec((d, 3 * d)),
                  _const_spec((1, 3 * d)), _const_spec((3, 3 * d)), _const_spec((1, 3 * d))],
        out_specs=(row, row),
        compiler_params=_params("parallel"),
    )(x, x, x, g.reshape(1, d), w_in.astype(BF16), b_in.reshape(1, 3 * d),
      conv_w, conv_b.reshape(1, 3 * d))


def _hyena_filter_taps(L, d, f_w1, f_b1, f_w2, f_b2, f_w3, f_b3, f_freq, decay):
    t = jnp.linspace(0.0, 1.0, L, dtype=F32)[:, None]
    w_ang = 2.0 * math.pi * jnp.arange(L, dtype=F32)[:, None] / L
    bands = jnp.linspace(1e-4, HY_BANDS - 1, HY_BANDS, dtype=F32)[None, :]
    z = jnp.concatenate([t, jnp.cos(bands * w_ang), -jnp.sin(bands * w_ang)], axis=-1)
    h = jnp.sin(f_freq[0] * (z @ f_w1 + f_b1))
    h = jnp.sin(f_freq[1] * (h @ f_w2 + f_b2))
    h = h @ f_w3 + f_b3
    h = h.reshape(L, 2, d) * jnp.exp(-t[:, :, None] * jnp.abs(decay)[None])
    k = jnp.concatenate([h[:, 0], jnp.zeros((1, d), F32), h[1:, 1][::-1]], axis=0)
    return k / jnp.sum(jnp.abs(k), axis=0, keepdims=True)


def _fft_dims(m):
    lg = m.bit_length() - 1
    p = 1 << ((lg + 1) // 2)
    return p, m // p


def _split_bf16(x):
    hi = x.astype(BF16)
    return hi, (x - hi.astype(F32)).astype(BF16)


def _mm_split(fh, fl, x, precise):
    if not precise:
        return jnp.dot(fh, x.astype(BF16), preferred_element_type=F32)
    xh, xl = _split_bf16(x)
    return (jnp.dot(fh, xh, preferred_element_type=F32) + jnp.dot(fl, xh, preferred_element_type=F32)
            + jnp.dot(fh, xl, preferred_element_type=F32))


def _cplx_as_real(cr, ci):
    top = jnp.concatenate([cr, -ci], axis=-1)
    bot = jnp.concatenate([ci, cr], axis=-1)
    return jnp.concatenate([top, bot], axis=-2)


def _unit_circle(idx, m):
    ang = (2.0 * math.pi / m) * idx.astype(F32)
    return jnp.cos(ang), jnp.sin(ang)


def _dft_consts(p, q):
    m = p * q
    k1 = jnp.arange(p, dtype=jnp.int32)
    n1 = jnp.arange(p // 2, dtype=jnp.int32)
    c, s = _unit_circle((k1[:, None] * n1[None, :]) % p, p)
    fa = _cplx_as_real(c, -s)
    c, s = _unit_circle((n1[:, None] * k1[None, :]) % p, p)
    fd = _cplx_as_real(c / m, s / m)
    k2 = jnp.arange(q, dtype=jnp.int32)
    n2 = jnp.arange(q, dtype=jnp.int32)
    idx = (n2[None, None, :] * (k2[None, :, None] * p + k1[:, None, None])) % m
    c, s = _unit_circle(idx, m)
    gb = _cplx_as_real(c, -s)
    ct, st = jnp.swapaxes(c, 1, 2), jnp.swapaxes(s, 1, 2)
    gc = _cplx_as_real(ct, st)
    return tuple(_split_bf16(a) for a in (fa, gb, gc, fd))


def _fft_a_kernel(x_ref, fh_ref, fl_ref, o_ref, *, precise):
    _, half, d = x_ref.shape
    y = _mm_split(fh_ref[...], fl_ref[...], x_ref[...].reshape(2 * half, d), precise)
    o_ref[...] = y.reshape(o_ref.shape)


def _fft_b_kernel(a_ref, gh_ref, gl_ref, o_ref):
    _, q, d = a_ref.shape
    x = _mm_split(gh_ref[...], gl_ref[...], a_ref[...].reshape(2 * q, d), True)
    o_ref[...] = x.reshape(o_ref.shape)


def _fft_bc_kernel(a_ref, gb_ref, gc_ref, k_ref, z_ref):
    _, q, d = a_ref.shape
    x = _mm_split(gb_ref[...], None, a_ref[...].reshape(2 * q, d), False)
    xr, xi = x[:q], x[q:]
    kr, ki = k_ref[0], k_ref[1]
    y = jnp.concatenate([xr * kr - xi * ki, xr * ki + xi * kr], axis=0)
    z = _mm_split(gc_ref[...], None, y, False)
    z_ref[...] = z.reshape(z_ref.shape)


def _fft_stage_a(x4, fa, p, q, d, precise):
    pairs = x4.shape[0]
    return pl.pallas_call(
        functools.partial(_fft_a_kernel, precise=precise),
        out_shape=jax.ShapeDtypeStruct((pairs, 2, p, q * d), F32),
        grid=(pairs, q),
        in_specs=[pl.BlockSpec((None, 2, p // 2, d), lambda b, j: (b, 0, 0, j)),
                  _const_spec((2 * p, p)), _const_spec((2 * p, p))],
        out_specs=pl.BlockSpec((None, 2, p, d), lambda b, j: (b, 0, 0, j)),
        compiler_params=_params("parallel", "parallel"),
    )(x4, *fa)


def _long_conv(vx, taps, batch, seq_len):
    n, d = vx.shape
    m = 2 * seq_len
    p, q = _fft_dims(m)
    assert batch % 2 == 0
    pairs = batch // 2
    fa, gb, gc, fd = _dft_consts(p, q)
    g_spec = pl.BlockSpec((None, 2 * q, 2 * q), lambda k1, b: (k1, 0, 0))
    slab = pl.BlockSpec((None, 2, q, d), lambda k1, b: (b, 0, k1, 0))

    zeros = jnp.zeros((seq_len, d), F32)
    kin = jnp.stack([taps[:seq_len], zeros, taps[seq_len:], zeros]).reshape(2, 2, p // 2, q * d)
    ka = _fft_stage_a(kin, fa, p, q, d, True).reshape(2, 2, p * q, d)
    kx = pl.pallas_call(
        _fft_b_kernel,
        out_shape=jax.ShapeDtypeStruct((2, 2, p * q, d), F32),
        grid=(p, 2),
        in_specs=[slab, g_spec, g_spec],
        out_specs=slab,
        compiler_params=_params("parallel", "parallel"),
    )(ka, *gb)
    sign = jnp.repeat(1.0 - 2.0 * (jnp.arange(p) % 2).astype(F32), q)[None, :, None]
    kspec = kx[0] + sign * kx[1]

    xa = _fft_stage_a(vx.reshape(pairs, 2, p // 2, q * d), fa, p, q, d, False).reshape(pairs, 2, p * q, d)
    z = pl.pallas_call(
        _fft_bc_kernel,
        out_shape=jax.ShapeDtypeStruct((pairs, 2, p * q, d), F32),
        grid=(p, pairs),
        in_specs=[slab, g_spec, g_spec, pl.BlockSpec((2, q, d), lambda k1, b: (0, k1, 0))],
        out_specs=slab,
        compiler_params=_params("parallel", "parallel"),
    )(xa, gb[0], gc[0], kspec)
    y = pl.pallas_call(
        functools.partial(_fft_a_kernel, precise=False),
        out_shape=jax.ShapeDtypeStruct((pairs, 2, p // 2, q * d), F32),
        grid=(pairs, q),
        in_specs=[pl.BlockSpec((None, 2, p, d), lambda b, j: (b, 0, 0, j)),
                  _const_spec((p, 2 * p)), _const_spec((p, 2 * p))],
        out_specs=pl.BlockSpec((None, 2, p // 2, d), lambda b, j: (b, 0, 0, j)),
        compiler_params=_params("parallel", "parallel"),
    )(z.reshape(pairs, 2, p, q * d), *fd)
    return y.reshape(n, d)


def _hyena_out_prologue(y, vx, x0, skip):
    return (y + vx * skip) * x0


def _hyena_layer(x, batch, seq_len, g, p):
    n, d = x.shape
    x0, vx = _hyena_in(x, g, p['w_in'], p['b_in'], p['conv_w'], p['conv_b'], seq_len)
    taps = _hyena_filter_taps(seq_len, d, p['f_w1'], p['f_b1'], p['f_w2'], p['f_b2'],
                              p['f_w3'], p['f_b3'], p['f_freq'], p['decay'])
    y = _long_conv(vx, taps, batch, seq_len)
    return _mm_res(_hyena_out_prologue, [y, vx, x0], [p['skip'].reshape(1, d)],
                   p['w_out'], p['b_out'], x)


def _head_norm_rope(xh, gain, cos, sin, half):
    y = _rms(xh, gain)
    lane = lax.broadcasted_iota(jnp.int32, (1, HEAD_DIM), 1)
    fwd = pltpu.roll(y, HEAD_DIM - half, 1)
    bwd = pltpu.roll(y, half, 1)
    partner = jnp.where((lane % (2 * half)) < half, fwd, bwd)
    return y * cos + partner * sin


def _qkv_rope_kernel(x_ref, g_ref, w_ref, qg_ref, kg_ref, cos_ref, sin_ref,
                     q_ref, k_ref, v_ref, *, nq, nk, nv, half):
    xb = _rms(x_ref[...], g_ref[...]).astype(BF16)
    cos = cos_ref[...]
    sin = sin_ref[...]
    per = MXU_DIM // HEAD_DIM
    for h0 in range(0, nq + nk + nv, per):
        pw = jnp.dot(xb, w_ref[:, h0 * HEAD_DIM:(h0 + per) * HEAD_DIM], preferred_element_type=F32)
        for h in range(h0, h0 + per):
            ph = pw[:, (h - h0) * HEAD_DIM:(h - h0 + 1) * HEAD_DIM]
            if h < nq:
                sl = slice(h * HEAD_DIM, (h + 1) * HEAD_DIM)
                q_ref[:, sl] = _head_norm_rope(ph, qg_ref[...], cos, sin, half).astype(BF16)
            elif h < nq + nk:
                sl = slice((h - nq) * HEAD_DIM, (h - nq + 1) * HEAD_DIM)
                k_ref[:, sl] = _head_norm_rope(ph, kg_ref[...], cos, sin, half).astype(BF16)
            else:
                sl = slice((h - nq - nk) * HEAD_DIM, (h - nq - nk + 1) * HEAD_DIM)
                v_ref[:, sl] = ph.astype(BF16)


def _qkv_rope(x, g, w_qkv, q_gain, k_gain, cos, sin, nq, nk, nv, half, seq_len, tm=512):
    n, d = x.shape
    tm = min(tm, seq_len)
    pos_blocks = seq_len // tm
    f = w_qkv.shape[1]
    row = pl.BlockSpec((tm, d), lambda i: (i, 0))
    tab = pl.BlockSpec((tm, HEAD_DIM), lambda i: (i % pos_blocks, 0))
    outs = tuple(jax.ShapeDtypeStruct((n, c * HEAD_DIM), BF16) for c in (nq, nk, nv))
    return pl.pallas_call(
        functools.partial(_qkv_rope_kernel, nq=nq, nk=nk, nv=nv, half=half),
        out_shape=outs,
        grid=(n // tm,),
        in_specs=[row, _const_spec((1, d)), _const_spec((d, f)),
                  _const_spec((1, HEAD_DIM)), _const_spec((1, HEAD_DIM)), tab, tab],
        out_specs=tuple(pl.BlockSpec((tm, c * HEAD_DIM), lambda i: (i, 0)) for c in (nq, nk, nv)),
        compiler_params=_params("parallel"),
    )(x, g.reshape(1, d), w_qkv.astype(BF16), q_gain.reshape(1, HEAD_DIM),
      k_gain.reshape(1, HEAD_DIM), cos, sin)


def _axial_tables(L):
    t = jnp.arange(L)
    r = (t // GRID_W).astype(F32)
    c = (t % GRID_W).astype(F32)
    nf = HEAD_DIM // 4
    inv = AXIAL_THETA ** (-(2.0 * jnp.arange(nf, dtype=F32)) / (2 * nf))
    ar, ac = r[:, None] * inv[None], c[:, None] * inv[None]
    cos = jnp.concatenate([jnp.cos(ar), jnp.cos(ar), jnp.cos(ac), jnp.cos(ac)], axis=-1)
    sin = jnp.concatenate([-jnp.sin(ar), jnp.sin(ar), -jnp.sin(ac), jnp.sin(ac)], axis=-1)
    return cos, sin


FLASH_SAFE_BOUND = 40.0
FLASH_BOUND_MARGIN = 1.001


def _flash_kernel(q_ref, k_ref, v_ref, o_ref, qx_scr, acc_scr, m_scr, l_scr, kn_scr,
                  *, tq, tk, tkf, seq_len, group):
    scale = HEAD_DIM ** -0.5
    c = scale * math.log2(math.e)
    ext = 2 * HEAD_DIM

    @pl.when(pl.program_id(2) == 0)
    def _():
        def norm_step(j, mx):
            kt = k_ref[pl.ds(pl.multiple_of(j * tk, tk), tk), 0:HEAD_DIM].astype(F32)
            row = jnp.sum(kt * kt, axis=1, keepdims=True)
            return jnp.maximum(mx, jnp.max(row, axis=0, keepdims=True))

        kmax2 = lax.fori_loop(0, seq_len // tk, norm_step, jnp.zeros((1, 1), F32))
        kn_scr[...] = jnp.broadcast_to(kmax2, kn_scr.shape)

    kmax2 = kn_scr[0:1, 0:1]
    lane = lax.broadcasted_iota(jnp.int32, (tq, HEAD_DIM), 1)
    bmax = jnp.zeros((1, 1), F32)
    for h in range(group):
        q = q_ref[:, h * HEAD_DIM:(h + 1) * HEAD_DIM]
        qf = q.astype(F32)
        b = jnp.sqrt(jnp.sum(qf * qf, axis=1, keepdims=True) * kmax2) * FLASH_BOUND_MARGIN
        bmax = jnp.maximum(bmax, jnp.max(b, axis=0, keepdims=True))
        qx_scr[h, :, 0:HEAD_DIM] = q
        qx_scr[h, :, HEAD_DIM:ext] = jnp.where(lane == 0, -b, 0.0).astype(BF16)
    fast = (bmax * scale)[0, 0] <= FLASH_SAFE_BOUND

    @pl.when(fast)
    def _():
        acc_scr[...] = jnp.zeros_like(acc_scr)

        def body(j, carry):
            start = pl.multiple_of(j * tkf, tkf)
            kt = k_ref[pl.ds(start, tkf), :]
            vt = v_ref[pl.ds(start, tkf), :]
            for h in range(group):
                s = lax.dot_general(qx_scr[h], kt, (((1,), (1,)), ((), ())),
                                    preferred_element_type=F32)
                p = jnp.exp2(s * c)
                acc_scr[h] += jnp.dot(p.astype(BF16), vt, preferred_element_type=F32)
            return carry

        lax.fori_loop(0, seq_len // tkf, body, 0)
        for h in range(group):
            a = acc_scr[h]
            o_ref[:, h * HEAD_DIM:(h + 1) * HEAD_DIM] = (a[:, 0:HEAD_DIM] / a[:, HEAD_DIM:ext]).astype(o_ref.dtype)

    @pl.when(jnp.logical_not(fast))
    def _():
        m_scr[...] = jnp.full(m_scr.shape, -jnp.inf, F32)
        l_scr[...] = jnp.zeros_like(l_scr)
        acc_scr[...] = jnp.zeros_like(acc_scr)

        def body(j, carry):
            start = pl.multiple_of(j * tk, tk)
            kt = k_ref[pl.ds(start, tk), 0:HEAD_DIM]
            vt = v_ref[pl.ds(start, tk), 0:HEAD_DIM]
            for h in range(group):
                q = q_ref[:, h * HEAD_DIM:(h + 1) * HEAD_DIM]
                s = lax.dot_general(q, kt, (((1,), (1,)), ((), ())), preferred_element_type=F32)
                m = m_scr[h]
                m_new = jnp.maximum(m, jnp.max(s, axis=1, keepdims=True))
                alpha = jnp.exp2((m - m_new) * c)
                p = jnp.exp2(s * c - m_new * c)
                l_scr[h] = alpha * l_scr[h] + jnp.sum(p, axis=1, keepdims=True)
                acc_scr[h, :, 0:HEAD_DIM] = (alpha * acc_scr[h, :, 0:HEAD_DIM]
                                             + jnp.dot(p.astype(BF16), vt, preferred_element_type=F32))
                m_scr[h] = m_new
            return carry

        lax.fori_loop(0, seq_len // tk, body, 0)
        for h in range(group):
            o_ref[:, h * HEAD_DIM:(h + 1) * HEAD_DIM] = (acc_scr[h, :, 0:HEAD_DIM] / l_scr[h]).astype(o_ref.dtype)


def _extend_heads(a, heads, fill_col0_only):
    n = a.shape[0]
    if fill_col0_only:
        pad = jnp.zeros((n, heads, HEAD_DIM), a.dtype).at[:, :, 0].set(1)
    else:
        pad = jnp.ones((n, heads, HEAD_DIM), a.dtype)
    return jnp.concatenate([a.reshape(n, heads, HEAD_DIM), pad], axis=-1).reshape(n, heads * 2 * HEAD_DIM)


def _flash_gqa(q, k, v, batch, seq_len, tq=256, tk=512, tkf=1024):
    tq = min(tq, seq_len)
    tk = min(tk, seq_len)
    tkf = min(tkf, seq_len)
    nq = seq_len // tq
    gw = GA_GROUP * HEAD_DIM
    ext = 2 * HEAD_DIM
    kx = _extend_heads(k, GA_KV_HEADS, True)
    vx = _extend_heads(v, GA_KV_HEADS, False)
    return pl.pallas_call(
        functools.partial(_flash_kernel, tq=tq, tk=tk, tkf=tkf, seq_len=seq_len, group=GA_GROUP),
        out_shape=jax.ShapeDtypeStruct(q.shape, BF16),
        grid=(batch, GA_KV_HEADS, nq),
        in_specs=[pl.BlockSpec((tq, gw), lambda b, kv, i: (b * nq + i, kv)),
                  pl.BlockSpec((seq_len, ext), lambda b, kv, i: (b, kv)),
                  pl.BlockSpec((seq_len, ext), lambda b, kv, i: (b, kv))],
        out_specs=pl.BlockSpec((tq, gw), lambda b, kv, i: (b * nq + i, kv)),
        scratch_shapes=[pltpu.VMEM((GA_GROUP, tq, ext), BF16), pltpu.VMEM((GA_GROUP, tq, ext), F32),
                        pltpu.VMEM((GA_GROUP, tq, 1), F32), pltpu.VMEM((GA_GROUP, tq, 1), F32),
                        pltpu.VMEM((HALO_ROWS, HEAD_DIM), F32)],
        compiler_params=_params("parallel", "parallel", "arbitrary"),
    )(q, kx, vx)


def _identity_prologue(o):
    return o


def _gqa_layer(x, batch, seq_len, g, p):
    cos, sin = _axial_tables(seq_len)
    q, k, v = _qkv_rope(x, g, p['w_qkv'], p['q_gain'], p['k_gain'], cos, sin,
                        GA_HEADS, GA_KV_HEADS, GA_KV_HEADS, HEAD_DIM // 4, seq_len)
    o = _flash_gqa(q, k, v, batch, seq_len)
    return _mm_res(_identity_prologue, [o], [], p['w_o'], None, x)


def _ml_in_kernel(x_ref, xp_ref, xn_ref, g_ref, w_ref, cw_ref, cb_ref, wq_ref, wk_ref, wv_ref,
                  wg_ref, bg_ref, q_ref, k_ref, v_ref, xc_ref, sz_ref, gate_ref,
                  *, seq_len, tm, inner, k_scale):
    keep_prev, keep_next = _edge_scales(pl.program_id(0), tm, seq_len)
    g = g_ref[...]
    xb = _rms(x_ref[...], g).astype(BF16)
    hb = _rms(jnp.concatenate([xp_ref[...], xn_ref[...]], axis=0), g).astype(BF16)
    rows = lax.broadcasted_iota(jnp.int32, (tm, 1), 0)
    gacc = jnp.zeros(gate_ref.shape, F32)
    for t in range(inner // MXU_DIM):
        sl = slice(t * MXU_DIM, (t + 1) * MXU_DIM)
        w = w_ref[:, sl]
        xm = jnp.dot(xb, w, preferred_element_type=F32)
        xh = jnp.dot(hb, w, preferred_element_type=F32)
        pprev = xh[HALO_ROWS - 1:HALO_ROWS, :] * keep_prev
        pnext = xh[HALO_ROWS:HALO_ROWS + 1, :] * keep_next
        xc = _conv3_rows(xm, pprev, pnext, cw_ref[:, sl], cb_ref[:, sl], rows, tm)
        xc = xc * jax.nn.sigmoid(xc)
        z = jnp.dot(xb, w_ref[:, inner + t * MXU_DIM:inner + (t + 1) * MXU_DIM],
                    preferred_element_type=F32)
        xcb = xc.astype(BF16)
        q = jnp.dot(xcb, wq_ref[t], preferred_element_type=F32)
        k = jnp.dot(xcb, wk_ref[t], preferred_element_type=F32)
        v = jnp.dot(xm.astype(BF16), wv_ref[t], preferred_element_type=F32)
        qb, kb, vb = q.astype(BF16), k.astype(BF16), v.astype(BF16)
        gacc += (jnp.dot(qb, wg_ref[0, sl, :], preferred_element_type=F32)
                 + jnp.dot(kb, wg_ref[1, sl, :], preferred_element_type=F32)
                 + jnp.dot(vb, wg_ref[2, sl, :], preferred_element_type=F32))
        q_ref[:, sl] = qb
        k_ref[:, sl] = (k * k_scale).astype(BF16)
        v_ref[:, sl] = vb
        xc_ref[:, sl] = xc
        sz_ref[:, sl] = z * jax.nn.sigmoid(z)
    gate_ref[...] = gacc + bg_ref[...]


def _block_diag_tiles(w):
    nb, c, _ = w.shape
    per = MXU_DIM // c
    wt = w.reshape(nb // per, per, c, c)
    eye = jnp.eye(per, dtype=w.dtype)
    full = jnp.einsum('tpcd,pq->tpcqd', wt, eye)
    return full.reshape(nb // per, MXU_DIM, MXU_DIM)


def _ml_in(x, g, p, seq_len, tm=256):
    n, d = x.shape
    tm = min(tm, seq_len)
    inner = p['w_up'].shape[1] // 2
    ng = 4 * ML_HEADS
    dh = inner // ML_HEADS
    wq, wk, wv = (_block_diag_tiles(p[nm]).astype(BF16) for nm in ('w_q', 'w_k', 'w_v'))
    wg = jnp.transpose(p['w_gate'], (1, 2, 0, 3)).reshape(3, inner, ng).astype(BF16)
    bg = p['b_gate'].reshape(1, ng)
    prev, nxt = _halo_specs(tm, d, n)
    row = pl.BlockSpec((tm, d), lambda i: (i, 0))
    wide = pl.BlockSpec((tm, inner), lambda i: (i, 0))
    nt = inner // MXU_DIM
    return pl.pallas_call(
        functools.partial(_ml_in_kernel, seq_len=seq_len, tm=tm, inner=inner, k_scale=dh ** -0.5),
        out_shape=(jax.ShapeDtypeStruct((n, inner), BF16),) * 3
        + (jax.ShapeDtypeStruct((n, inner), F32),) * 2
        + (jax.ShapeDtypeStruct((n, ng), F32),),
        grid=(n // tm,),
        in_specs=[row, prev, nxt, _const_spec((1, d)), _const_spec((d, 2 * inner)),
                  _const_spec((3, inner)), _const_spec((1, inner)),
                  _const_spec((nt, MXU_DIM, MXU_DIM)), _const_spec((nt, MXU_DIM, MXU_DIM)),
                  _const_spec((nt, MXU_DIM, MXU_DIM)), _const_spec((3, inner, ng)),
                  _const_spec((1, ng))],
        out_specs=(wide,) * 5 + (pl.BlockSpec((tm, ng), lambda i: (i, 0)),),
        compiler_params=_params("parallel"),
    )(x, x, x, g.reshape(1, d), p['w_up'].astype(BF16), p['conv_w'],
      p['conv_b'].reshape(1, inner), wq, wk, wv, wg, bg)


def _log_sigmoid(x):
    return jnp.minimum(x, 0.0) - jnp.log1p(jnp.exp(-jnp.abs(x)))


def _mlstm_chunk_kernel(q_ref, k_ref, v_ref, gc_ref, gr_ref, h_ref, c_scr, n_scr, m_scr, *, lc):
    d = pl.program_id(2)

    @pl.when(pl.program_id(3) == 0)
    def _():
        c_scr[...] = jnp.zeros_like(c_scr)
        n_scr[...] = jnp.zeros_like(n_scr)
        m_scr[...] = jnp.zeros_like(m_scr)

    q = q_ref[...]
    k = k_ref[...]
    v = v_ref[...]
    gc = gc_ref[...]
    gr = gr_ref[...]
    i_col, g_col = gc[:, 0:1], gc[:, 1:2]
    i_row, g_row = gr[0:1, :], gr[1:2, :]

    jr = lax.broadcasted_iota(jnp.int32, (lc, lc), 0)
    sc = lax.broadcasted_iota(jnp.int32, (lc, lc), 1)
    seen = ((sc - jr) * (1 - 2 * d)) <= 0
    g_tot = jnp.where(d == 0, g_row[:, lc - 1:lc], g_row[:, 0:1])
    m_old = m_scr[0:1, 0:1]

    dmat = jnp.where(seen, g_col - g_row + i_row, -jnp.inf)
    inter = g_col + m_old
    m_q = jnp.maximum(inter, jnp.max(dmat, axis=1, keepdims=True))
    s_qk = lax.dot_general(q, k, (((1,), (1,)), ((), ())), preferred_element_type=F32)
    a = s_qk * jnp.exp(dmat - m_q)
    w_int = jnp.exp(inter - m_q)
    q_c = jnp.dot(q, c_scr[...].astype(BF16), preferred_element_type=F32)
    num = jnp.dot(a.astype(BF16), v, preferred_element_type=F32) + q_c * w_int
    q_n = jnp.sum(q.astype(F32) * n_scr[...], axis=1, keepdims=True)
    den = jnp.sum(a, axis=1, keepdims=True) + w_int * q_n
    den = jnp.maximum(jnp.abs(den), jnp.exp(-m_q))
    h_ref[...] = num / den

    a_row = g_tot - g_row + i_row
    m_new = jnp.maximum(g_tot + m_old, jnp.max(a_row, axis=1, keepdims=True))
    ws_col = jnp.exp(g_tot - g_col + i_col - m_new)
    dec = jnp.exp(g_tot + m_old - m_new)
    kw = k.astype(F32) * ws_col
    upd = lax.dot_general(kw.astype(BF16), v, (((0,), (0,)), ((), ())), preferred_element_type=F32)
    c_scr[...] = dec * c_scr[...] + upd
    n_scr[...] = dec * n_scr[...] + jnp.sum(kw, axis=0, keepdims=True)
    m_scr[...] = jnp.broadcast_to(m_new, m_scr.shape)


def _ml_gate_kernel(g_ref, o_ref, *, lc, nh):
    g = g_ref[...]
    jr = lax.broadcasted_iota(jnp.int32, (lc, lc), 0)
    sc = lax.broadcasted_iota(jnp.int32, (lc, lc), 1)
    tril = jnp.where(sc <= jr, 1.0, 0.0).astype(BF16)
    triu = jnp.where(sc >= jr, 1.0, 0.0).astype(BF16)
    col = lax.broadcasted_iota(jnp.int32, (1, 4 * nh), 1)
    lf = _log_sigmoid(g)
    x1 = lf.astype(BF16)
    r1 = lf - x1.astype(F32)
    x2 = r1.astype(BF16)
    x3 = (r1 - x2.astype(F32)).astype(BF16)
    cum = lambda m: (jnp.dot(m, x1, preferred_element_type=F32) + jnp.dot(m, x2, preferred_element_type=F32)
                     + jnp.dot(m, x3, preferred_element_type=F32))
    gsum = jnp.where(col < 2 * nh, cum(tril), cum(triu))
    o_ref[...] = jnp.where((col % (2 * nh)) >= nh, gsum, g)


def _mlstm_chunks(q, k, v, gates, batch, seq_len):
    n, inner = q.shape
    dh = inner // ML_HEADS
    lc = min(ML_CHUNK, seq_len)
    nc = seq_len // lc
    ng = gates.shape[1]
    gates = pl.pallas_call(
        functools.partial(_ml_gate_kernel, lc=lc, nh=ML_HEADS),
        out_shape=jax.ShapeDtypeStruct((n, ng), F32),
        grid=(n // lc,),
        in_specs=[pl.BlockSpec((lc, ng), lambda i: (i, 0))],
        out_specs=pl.BlockSpec((lc, ng), lambda i: (i, 0)),
        compiler_params=_params("parallel"),
    )(gates)
    g4 = gates.reshape(n, 2, 2, ML_HEADS)
    gcol = jnp.transpose(g4, (1, 3, 0, 2))
    grow = jnp.transpose(g4, (1, 3, 2, 0))

    def chunk(b, c, dd):
        return b * nc + c + dd * (nc - 1 - 2 * c)

    qkv_spec = pl.BlockSpec((lc, dh), lambda b, h, dd, c: (chunk(b, c, dd), h))
    return pl.pallas_call(
        functools.partial(_mlstm_chunk_kernel, lc=lc),
        out_shape=jax.ShapeDtypeStruct((2, n, inner), F32),
        grid=(batch, ML_HEADS, 2, nc),
        in_specs=[qkv_spec, qkv_spec, qkv_spec,
                  pl.BlockSpec((None, None, lc, 2), lambda b, h, dd, c: (dd, h, chunk(b, c, dd), 0)),
                  pl.BlockSpec((None, None, 2, lc), lambda b, h, dd, c: (dd, h, 0, chunk(b, c, dd)))],
        out_specs=pl.BlockSpec((None, lc, dh), lambda b, h, dd, c: (dd, chunk(b, c, dd), h)),
        scratch_shapes=[pltpu.VMEM((dh, dh), F32), pltpu.VMEM((1, dh), F32),
                        pltpu.VMEM((HALO_ROWS, HEAD_DIM), F32)],
        compiler_params=_params("parallel", "parallel", "parallel", "arbitrary"),
    )(q, k, v, gcol, grow)


def _ml_out_prologue(hf, hb, xc, sz, gain, skip):
    h = hf + hb
    dh = h.shape[1] // ML_HEADS
    parts = []
    for i in range(ML_HEADS):
        sl = slice(i * dh, (i + 1) * dh)
        parts.append(_rms(h[:, sl], gain[:, sl]))
    hn = jnp.concatenate(parts, axis=1)
    return (hn + skip * xc) * sz


def _mlstm_layer(x, batch, seq_len, g, p):
    q, k, v, xc, sz, gates = _ml_in(x, g, p, seq_len)
    inner = q.shape[1]
    hs = _mlstm_chunks(q, k, v, gates, batch, seq_len)
    return _mm_res(_ml_out_prologue, [hs[0], hs[1], xc, sz],
                   [p['norm_gain'].reshape(1, inner), p['skip'].reshape(1, inner)],
                   p['w_down'], None, x, tm=256)


def _rope_tables(L):
    inv = ROPE_THETA ** (-(2.0 * jnp.arange(ROPE_DIMS // 2, dtype=F32)) / ROPE_DIMS)
    ang = jnp.arange(L, dtype=F32)[:, None] * inv[None]
    pad = HEAD_DIM - ROPE_DIMS
    cos = jnp.concatenate([jnp.cos(ang), jnp.cos(ang), jnp.ones((L, pad), F32)], axis=-1)
    sin = jnp.concatenate([-jnp.sin(ang), jnp.sin(ang), jnp.zeros((L, pad), F32)], axis=-1)
    return cos, sin


def _band_kernel(q_ref, kp_ref, kc_ref, kn_ref, vp_ref, vc_ref, vn_ref, o_ref, lse_ref,
                 *, s_len, heads):
    i = pl.program_id(2)
    qb = BAND_BLOCK
    w = qb + 2 * BAND_HALF
    a = lax.broadcasted_iota(jnp.int32, (qb, w), 0)
    c = lax.broadcasted_iota(jnp.int32, (qb, w), 1)
    rel = c - BAND_HALF - a
    key_pos = i * qb - BAND_HALF + c
    valid = (jnp.abs(rel) <= BAND_HALF) & (key_pos >= 0) & (key_pos < s_len)
    scale = HEAD_DIM ** -0.5
    for h in range(heads):
        sl = slice(h * HEAD_DIM, (h + 1) * HEAD_DIM)
        kw = jnp.concatenate([kp_ref[qb - BAND_HALF:, sl], kc_ref[:, sl], kn_ref[:BAND_HALF, sl]], axis=0)
        vw = jnp.concatenate([vp_ref[qb - BAND_HALF:, sl], vc_ref[:, sl], vn_ref[:BAND_HALF, sl]], axis=0)
        s = lax.dot_general(q_ref[:, sl], kw, (((1,), (1,)), ((), ())),
                            preferred_element_type=F32) * scale
        s = jnp.where(valid, s, -jnp.inf)
        m = jnp.max(s, axis=1, keepdims=True)
        p = jnp.exp(s - m)
        l = jnp.sum(p, axis=1, keepdims=True)
        o = jnp.dot(p.astype(BF16), vw, preferred_element_type=F32)
        o_ref[:, sl] = o / l
        lse_ref[:, sl] = jnp.broadcast_to(m + jnp.log(l), (qb, HEAD_DIM))


def _band_attention(q, k, v, batch, seq_len, gi, dil):
    n = q.shape[0]
    s_len = seq_len // dil
    nb = s_len // BAND_BLOCK
    hw = DA_HEADS_PER_GROUP * HEAD_DIM
    groups = len(DA_GROUPS)
    view = lambda a: a.reshape(batch, s_len, dil * a.shape[1])
    qv, kv, vv = view(q), view(k), view(v)

    def spec(shift):
        return pl.BlockSpec(
            (None, BAND_BLOCK, hw),
            lambda b, r, i: (b, jnp.clip(i + shift, 0, nb - 1), r * groups + gi))

    out_spec = pl.BlockSpec((None, BAND_BLOCK, hw), lambda b, r, i: (b, i, r))
    o, lse = pl.pallas_call(
        functools.partial(_band_kernel, s_len=s_len, heads=DA_HEADS_PER_GROUP),
        out_shape=(jax.ShapeDtypeStruct((batch, s_len, dil * hw), F32),) * 2,
        grid=(batch, dil, nb),
        in_specs=[spec(0), spec(-1), spec(0), spec(1), spec(-1), spec(0), spec(1)],
        out_specs=(out_spec, out_spec),
        compiler_params=_params("parallel", "parallel", "parallel"),
    )(qv, kv, kv, kv, vv, vv, vv)
    return o.reshape(n, hw), lse.reshape(n, hw)


def _da_out_prologue(o0, o1, o2, l0, l1, l2):
    m = jnp.maximum(jnp.maximum(l0, l1), l2)
    e0, e1, e2 = jnp.exp(l0 - m), jnp.exp(l1 - m), jnp.exp(l2 - m)
    return (e0 * o0 + e1 * o1 + e2 * o2) / (e0 + e1 + e2)


def _dilated_layer(x, batch, seq_len, g, p):
    cos, sin = _rope_tables(seq_len)
    q, k, v = _qkv_rope(x, g, p['w_qkv'], p['q_gain'], p['k_gain'], cos, sin,
                        DA_HEADS, DA_HEADS, DA_HEADS, ROPE_DIMS // 2, seq_len, tm=256)
    outs, lses = [], []
    for gi, (_, dil) in enumerate(DA_GROUPS):
        o, lse = _band_attention(q, k, v, batch, seq_len, gi, dil)
        outs.append(o)
        lses.append(lse)
    return _mm_res(_da_out_prologue, outs + lses, [], p['w_o'], None, x)


def _router_kernel(x_ref, g_ref, wr_ref, xn_ref, aff_ref):
    xn = _rms(x_ref[...], g_ref[...])
    xn_ref[...] = xn.astype(BF16)
    logits = lax.dot_general(wr_ref[...], xn, (((1,), (1,)), ((), ())),
                             precision=HIGHEST, preferred_element_type=F32)
    m = jnp.max(logits, axis=0, keepdims=True)
    e = jnp.exp(logits - m)
    aff_ref[...] = e / jnp.sum(e, axis=0, keepdims=True)


def _router(x, g, w_router, tm=512):
    n, d = x.shape
    tm = min(tm, n)
    e = w_router.shape[1]
    return pl.pallas_call(
        _router_kernel,
        out_shape=(jax.ShapeDtypeStruct((n, d), BF16), jax.ShapeDtypeStruct((e, n), F32)),
        grid=(n // tm,),
        in_specs=[pl.BlockSpec((tm, d), lambda i: (i, 0)), _const_spec((1, d)), _const_spec((e, d))],
        out_specs=(pl.BlockSpec((tm, d), lambda i: (i, 0)), pl.BlockSpec((e, tm), lambda i: (0, i))),
        compiler_params=_params("parallel"),
    )(x, g.reshape(1, d), w_router.T)


def _expert_ffn_kernel(xe_ref, gate_ref, w1_ref, w3_ref, w2_ref, ye_ref):
    xe = xe_ref[...]
    h1 = jnp.dot(xe, w1_ref[...], preferred_element_type=F32)
    h3 = jnp.dot(xe, w3_ref[...], preferred_element_type=F32)
    hid = (h1 * jax.nn.sigmoid(h1) * h3).astype(BF16)
    ye_ref[...] = jnp.dot(hid, w2_ref[...], preferred_element_type=F32) * gate_ref[...]


def _expert_ffn(xe, gates, w1, w3, w2, tm=512):
    e, c, d = xe.shape
    f = w1.shape[2]
    tm = min(tm, c)
    return pl.pallas_call(
        _expert_ffn_kernel,
        out_shape=jax.ShapeDtypeStruct((e, c, d), F32),
        grid=(e, c // tm),
        in_specs=[pl.BlockSpec((None, tm, d), lambda ei, ci: (ei, ci, 0)),
                  pl.BlockSpec((None, tm, 1), lambda ei, ci: (ei, ci, 0)),
                  pl.BlockSpec((None, d, f), lambda ei, ci: (ei, 0, 0)),
                  pl.BlockSpec((None, d, f), lambda ei, ci: (ei, 0, 0)),
                  pl.BlockSpec((None, f, d), lambda ei, ci: (ei, 0, 0))],
        out_specs=pl.BlockSpec((None, tm, d), lambda ei, ci: (ei, ci, 0)),
        compiler_params=_params("parallel", "arbitrary"),
    )(xe, gates[..., None], w1, w3, w2)


def _moe_layer(x, group_sizes, g, w_router, w1, w3, w2):
    n, d = x.shape
    xn, aff_t = _router(x, g, w_router)
    xes, gts, idxs = [], [], []
    start = 0
    for ng in group_sizes:
        cap = EC_CAPACITY * ng // N_EXPERTS
        gates, idx = lax.top_k(aff_t[:, start:start + ng], cap)
        idx = idx + start
        xes.append(xn[idx])
        gts.append(gates)
        idxs.append(idx)
        start += ng
    ye = _expert_ffn(jnp.concatenate(xes, axis=1), jnp.concatenate(gts, axis=1),
                     w1.astype(BF16), w3.astype(BF16), w2.astype(BF16))
    idx_all = jnp.concatenate(idxs, axis=1)
    return _combine(x, ye.reshape(-1, d), idx_all.reshape(-1))


COMBINE_TOKENS = 256
COMBINE_ROWS = 256
ROW_ALIGN = 8


def _combine_kernel(tile_ref, start_ref, x_ref, tok_ref, ye_ref, o_ref):
    w = pl.program_id(0)
    first = jnp.logical_or(w == 0, tile_ref[w] != tile_ref[jnp.maximum(w - 1, 0)])

    @pl.when(first)
    def _():
        o_ref[...] = x_ref[...]

    tt = o_ref.shape[0]
    rows = lax.broadcasted_iota(jnp.int32, (tt, COMBINE_ROWS), 0)
    onehot = jnp.where(rows == tok_ref[...], 1.0, 0.0).astype(BF16)
    hi, lo = _split_bf16(ye_ref[...])
    o_ref[...] += (jnp.dot(onehot, hi, preferred_element_type=F32)
                   + jnp.dot(onehot, lo, preferred_element_type=F32))


def _combine(x, ye, tok):
    n, d = x.shape
    p = tok.shape[0]
    tt = min(COMBINE_TOKENS, n)
    tiles = n // tt
    order = jnp.argsort(tok)
    tok_sorted = tok[order]
    pad = 2 * COMBINE_ROWS
    ye_sorted = ye[jnp.concatenate([order, jnp.zeros((pad,), order.dtype)])]
    tok_padded = jnp.concatenate([tok_sorted, jnp.full((pad,), -1, tok.dtype)])

    bounds = jnp.searchsorted(tok_sorted, jnp.arange(tiles + 1, dtype=tok.dtype) * tt).astype(jnp.int32)
    lo, hi = bounds[:-1], bounds[1:]
    lo_al = (lo // ROW_ALIGN) * ROW_ALIGN
    n_items = jnp.maximum((hi - lo_al + COMBINE_ROWS - 1) // COMBINE_ROWS, 1)
    item_end = jnp.cumsum(n_items)
    item_start = item_end - n_items
    max_items = p // COMBINE_ROWS + 2 * tiles + 1
    w = jnp.arange(max_items, dtype=jnp.int32)
    tile = jnp.minimum(jnp.searchsorted(item_end, w, side='right'), tiles - 1).astype(jnp.int32)
    k = w - item_start[tile]
    live = k < n_items[tile]
    row_start = jnp.where(live, lo_al[tile] + COMBINE_ROWS * k, 0).astype(jnp.int32)
    rows = row_start[:, None] + jnp.arange(COMBINE_ROWS, dtype=jnp.int32)[None, :]
    ok = live[:, None] & (rows >= lo[tile][:, None]) & (rows < hi[tile][:, None])
    local = jnp.where(ok, tok_padded[rows] - tile[:, None] * tt, -1).astype(jnp.int32)

    grid_spec = pltpu.PrefetchScalarGridSpec(
        num_scalar_prefetch=2,
        grid=(max_items,),
        in_specs=[pl.BlockSpec((tt, d), lambda i, tile_r, start_r: (tile_r[i], 0)),
                  pl.BlockSpec((None, 1, COMBINE_ROWS), lambda i, tile_r, start_r: (i, 0, 0)),
                  pl.BlockSpec((pl.Element(COMBINE_ROWS), pl.Element(d)),
                               lambda i, tile_r, start_r: (start_r[i] * ROW_ALIGN, 0))],
        out_specs=pl.BlockSpec((tt, d), lambda i, tile_r, start_r: (tile_r[i], 0)),
    )
    return pl.pallas_call(
        _combine_kernel,
        out_shape=jax.ShapeDtypeStruct((n, d), F32),
        grid_spec=grid_spec,
        compiler_params=_params("arbitrary"),
    )(tile, row_start // ROW_ALIGN, x, local[:, None, :], ye_sorted)


def _trunk(x, batch, seq_len, group_sizes, p):
    depth = p['norm_gain'].shape[0]
    for i in range(depth):
        mixer, j = i % 4, i // 4
        g = p['norm_gain'][i, 0]
        if mixer == 0:
            x = _hyena_layer(x, batch, seq_len, g, {k[3:]: v[j] for k, v in p.items() if k.startswith('hy_')})
        elif mixer == 1:
            x = _gqa_layer(x, batch, seq_len, g, {k[3:]: v[j] for k, v in p.items() if k.startswith('ga_')})
        elif mixer == 2:
            x = _mlstm_layer(x, batch, seq_len, g, {k[3:]: v[j] for k, v in p.items() if k.startswith('ml_')})
        else:
            x = _dilated_layer(x, batch, seq_len, g, {k[3:]: v[j] for k, v in p.items() if k.startswith('da_')})
        x = _moe_layer(x, group_sizes, p['norm_gain'][i, 1], p['moe_w_router'][i],
                       p['moe_w1'][i], p['moe_w3'][i], p['moe_w2'][i])
    return x


def kernel(x_prompt, x_sample, norm_gain, hy_w_in, hy_b_in, hy_conv_w, hy_conv_b, hy_f_w1, hy_f_b1, hy_f_w2, hy_f_b2, hy_f_w3, hy_f_b3, hy_f_freq, hy_decay, hy_skip, hy_w_out, hy_b_out, ga_w_qkv, ga_q_gain, ga_k_gain, ga_w_o, ml_w_up, ml_conv_w, ml_conv_b, ml_w_q, ml_w_k, ml_w_v, ml_w_gate, ml_b_gate, ml_norm_gain, ml_skip, ml_w_down, da_w_qkv, da_q_gain, da_k_gain, da_w_o, moe_w_router, moe_w1, moe_w3, moe_w2):
    p = dict(
        norm_gain=norm_gain,
        hy_w_in=hy_w_in, hy_b_in=hy_b_in, hy_conv_w=hy_conv_w, hy_conv_b=hy_conv_b,
        hy_f_w1=hy_f_w1, hy_f_b1=hy_f_b1, hy_f_w2=hy_f_w2, hy_f_b2=hy_f_b2,
        hy_f_w3=hy_f_w3, hy_f_b3=hy_f_b3, hy_f_freq=hy_f_freq, hy_decay=hy_decay,
        hy_skip=hy_skip, hy_w_out=hy_w_out, hy_b_out=hy_b_out,
        ga_w_qkv=ga_w_qkv, ga_q_gain=ga_q_gain, ga_k_gain=ga_k_gain, ga_w_o=ga_w_o,
        ml_w_up=ml_w_up, ml_conv_w=ml_conv_w, ml_conv_b=ml_conv_b, ml_w_q=ml_w_q,
        ml_w_k=ml_w_k, ml_w_v=ml_w_v, ml_w_gate=ml_w_gate, ml_b_gate=ml_b_gate,
        ml_norm_gain=ml_norm_gain, ml_skip=ml_skip, ml_w_down=ml_w_down,
        da_w_qkv=da_w_qkv, da_q_gain=da_q_gain, da_k_gain=da_k_gain, da_w_o=da_w_o,
        moe_w_router=moe_w_router, moe_w1=moe_w1, moe_w3=moe_w3, moe_w2=moe_w2,
    )
    bp, seq_len, d = x_prompt.shape
    bs = x_sample.shape[0]
    assert x_sample.shape[1] == seq_len
    x = jnp.concatenate([x_prompt, x_sample], axis=0).reshape((bp + bs) * seq_len, d)
    y = _trunk(x, bp + bs, seq_len, (bp * seq_len, bs * seq_len), p)
    y = y.reshape(bp + bs, seq_len, d)
    return (y[:bp], y[bp:])
```

```python
import functools
import math

import jax
import jax.numpy as jnp
from jax import lax
from jax.experimental import pallas as pl
from jax.experimental.pallas import tpu as pltpu

F32 = jnp.float32
BF16 = jnp.bfloat16
HIGHEST = lax.Precision.HIGHEST

NORM_EPS = 1e-6
GRID_W = 64
HY_BANDS = 16
GA_HEADS = 8
GA_KV_HEADS = 2
GA_GROUP = GA_HEADS // GA_KV_HEADS
HEAD_DIM = 128
AXIAL_THETA = 10000.0
ML_HEADS = 4
ML_QKV_BLOCK = 4
DA_GROUPS = ((128, 1), (512, 4), (2048, 16))
DA_HEADS_PER_GROUP = 4
DA_HEADS = DA_HEADS_PER_GROUP * len(DA_GROUPS)
ROPE_THETA = 500000.0
ROPE_DIMS = HEAD_DIM // 4
N_EXPERTS = 16
EC_CAPACITY = 2

VMEM_LIMIT_BYTES = 52 * 1024 * 1024
HALO_ROWS = 8
MXU_DIM = 256
ML_CHUNK = 256
BAND_BLOCK = 128
BAND_HALF = 64


def _params(*sem):
    return pltpu.CompilerParams(dimension_semantics=sem, vmem_limit_bytes=VMEM_LIMIT_BYTES)


def _rms(x, g):
    ms = jnp.mean(x * x, axis=-1, keepdims=True)
    return x * lax.rsqrt(ms + NORM_EPS) * g


def _const_spec(shape):
    nd = len(shape)
    return pl.BlockSpec(shape, lambda *_: (0,) * nd)


def _conv3_rows(p, pprev, pnext, cw, cb, rows, tm):
    up = jnp.where(rows == 0, pprev, pltpu.roll(p, 1, 0))
    dn = jnp.where(rows == tm - 1, pnext, pltpu.roll(p, tm - 1, 0))
    return up * cw[0:1] + p * cw[1:2] + dn * cw[2:3] + cb


def _halo_specs(tm, d, n_rows):
    hb = tm // HALO_ROWS
    last = n_rows // HALO_ROWS - 1
    prev = pl.BlockSpec((HALO_ROWS, d), lambda i: (jnp.maximum(i * hb - 1, 0), 0))
    nxt = pl.BlockSpec((HALO_ROWS, d), lambda i: (jnp.minimum((i + 1) * hb, last), 0))
    return prev, nxt


def _edge_scales(i, tm, seq_len):
    t0 = i * tm
    keep_prev = jnp.where(t0 % seq_len == 0, 0.0, 1.0).astype(F32)
    keep_next = jnp.where((t0 + tm) % seq_len == 0, 0.0, 1.0).astype(F32)
    return keep_prev, keep_next


def _mm_res_kernel(*refs, prologue, n_row, n_const, tn):
    row_refs = refs[:n_row]
    const_refs = refs[n_row:n_row + n_const]
    w_ref, b_ref, res_ref, o_ref = refs[n_row + n_const:]
    lhs = prologue(*[r[...] for r in row_refs], *[c[...] for c in const_refs]).astype(BF16)
    for j in range(o_ref.shape[1] // tn):
        sl = slice(j * tn, (j + 1) * tn)
        o_ref[:, sl] = (res_ref[:, sl] + b_ref[:, sl]
                        + jnp.dot(lhs, w_ref[:, sl], preferred_element_type=F32))


def _mm_res(prologue, rows, consts, w, b, res, tm=512, tn=512):
    n, dout = res.shape
    tm = min(tm, n)
    k = w.shape[0]
    if b is None:
        b = jnp.zeros((1, dout), F32)
    in_specs = [pl.BlockSpec((tm, r.shape[1]), lambda i: (i, 0)) for r in rows]
    in_specs += [_const_spec(c.shape) for c in consts]
    in_specs += [_const_spec((k, dout)), _const_spec((1, dout)),
                 pl.BlockSpec((tm, dout), lambda i: (i, 0))]
    return pl.pallas_call(
        functools.partial(_mm_res_kernel, prologue=prologue, n_row=len(rows),
                          n_const=len(consts), tn=min(tn, dout)),
        out_shape=jax.ShapeDtypeStruct((n, dout), F32),
        grid=(n // tm,),
        in_specs=in_specs,
        out_specs=pl.BlockSpec((tm, dout), lambda i: (i, 0)),
        compiler_params=_params("parallel"),
    )(*rows, *consts, w.astype(BF16), b.reshape(1, dout).astype(F32), res)


def _hyena_in_kernel(x_ref, xp_ref, xn_ref, g_ref, w_ref, b_ref, cw_ref, cb_ref,
                     x0_ref, vx_ref, *, seq_len, tm, d, cols):
    keep_prev, keep_next = _edge_scales(pl.program_id(0), tm, seq_len)
    g = g_ref[...]
    xb = _rms(x_ref[...], g).astype(BF16)
    hb = _rms(jnp.concatenate([xp_ref[...], xn_ref[...]], axis=0), g).astype(BF16)
    rows = lax.broadcasted_iota(jnp.int32, (tm, 1), 0)

    def conv_part(c0):
        sl = slice(c0, c0 + cols)
        w = w_ref[:, sl]
        bias = b_ref[:, sl]
        p = jnp.dot(xb, w, preferred_element_type=F32) + bias
        ph = jnp.dot(hb, w, preferred_element_type=F32) + bias
        pprev = ph[HALO_ROWS - 1:HALO_ROWS, :] * keep_prev
        pnext = ph[HALO_ROWS:HALO_ROWS + 1, :] * keep_next
        return _conv3_rows(p, pprev, pnext, cw_ref[:, sl], cb_ref[:, sl], rows, tm)

    for j in range(d // cols):
        c = j * cols
        x0_ref[:, c:c + cols] = conv_part(c)
        vx_ref[:, c:c + cols] = conv_part(2 * d + c) * conv_part(d + c)


def _hyena_in(x, g, w_in, b_in, conv_w, conv_b, seq_len, tm=256, cols=512):
    n, d = x.shape
    tm = min(tm, seq_len)
    prev, nxt = _halo_specs(tm, d, n)
    row = pl.BlockSpec((tm, d), lambda i: (i, 0))
    return pl.pallas_call(
        functools.partial(_hyena_in_kernel, seq_len=seq_len, tm=tm, d=d, cols=cols),
        out_shape=(jax.ShapeDtypeStruct((n, d), F32), jax.ShapeDtypeStruct((n, d), F32)),
        grid=(n // tm,),
        in_specs=[row, prev, nxt, _const_spec((1, d)), _const_spec((d, 3 * d)),
                  _const_spec((1, 3 * d)), _const_spec((3, 3 * d)), _const_spec((1, 3 * d))],
        out_specs=(row, row),
        compiler_params=_params("parallel"),
    )(x, x, x, g.reshape(1, d), w_in.astype(BF16), b_in.reshape(1, 3 * d),
      conv_w, conv_b.reshape(1, 3 * d))


def _hyena_filter_taps(L, d, f_w1, f_b1, f_w2, f_b2, f_w3, f_b3, f_freq, decay):
    t = jnp.linspace(0.0, 1.0, L, dtype=F32)[:, None]
    w_ang = 2.0 * math.pi * jnp.arange(L, dtype=F32)[:, None] / L
    bands = jnp.linspace(1e-4, HY_BANDS - 1, HY_BANDS, dtype=F32)[None, :]
    z = jnp.concatenate([t, jnp.cos(bands * w_ang), -jnp.sin(bands * w_ang)], axis=-1)
    h = jnp.sin(f_freq[0] * (z @ f_w1 + f_b1))
    h = jnp.sin(f_freq[1] * (h @ f_w2 + f_b2))
    h = h @ f_w3 + f_b3
    h = h.reshape(L, 2, d) * jnp.exp(-t[:, :, None] * jnp.abs(decay)[None])
    k = jnp.concatenate([h[:, 0], jnp.zeros((1, d), F32), h[1:, 1][::-1]], axis=0)
    return k / jnp.sum(jnp.abs(k), axis=0, keepdims=True)


def _fft_dims(m):
    lg = m.bit_length() - 1
    p = 1 << ((lg + 1) // 2)
    return p, m // p


def _split_bf16(x):
    hi = x.astype(BF16)
    return hi, (x - hi.astype(F32)).astype(BF16)


def _mm_split(fh, fl, x, precise):
    if not precise:
        return jnp.dot(fh, x.astype(BF16), preferred_element_type=F32)
    xh, xl = _split_bf16(x)
    return (jnp.dot(fh, xh, preferred_element_type=F32) + jnp.dot(fl, xh, preferred_element_type=F32)
            + jnp.dot(fh, xl, preferred_element_type=F32))


def _cplx_as_real(cr, ci):
    top = jnp.concatenate([cr, -ci], axis=-1)
    bot = jnp.concatenate([ci, cr], axis=-1)
    return jnp.concatenate([top, bot], axis=-2)


def _unit_circle(idx, m):
    ang = (2.0 * math.pi / m) * idx.astype(F32)
    return jnp.cos(ang), jnp.sin(ang)


def _dft_consts(p, q):
    m = p * q
    k1 = jnp.arange(p, dtype=jnp.int32)
    n1 = jnp.arange(p // 2, dtype=jnp.int32)
    c, s = _unit_circle((k1[:, None] * n1[None, :]) % p, p)
    fa = _cplx_as_real(c, -s)
    c, s = _unit_circle((n1[:, None] * k1[None, :]) % p, p)
    fd = _cplx_as_real(c / m, s / m)
    k2 = jnp.arange(q, dtype=jnp.int32)
    n2 = jnp.arange(q, dtype=jnp.int32)
    idx = (n2[None, None, :] * (k2[None, :, None] * p + k1[:, None, None])) % m
    c, s = _unit_circle(idx, m)
    gb = _cplx_as_real(c, -s)
    ct, st = jnp.swapaxes(c, 1, 2), jnp.swapaxes(s, 1, 2)
    gc = _cplx_as_real(ct, st)
    return tuple(_split_bf16(a) for a in (fa, gb, gc, fd))


def _fft_a_kernel(x_ref, fh_ref, fl_ref, o_ref, *, precise):
    _, half, d = x_ref.shape
    y = _mm_split(fh_ref[...], fl_ref[...], x_ref[...].reshape(2 * half, d), precise)
    o_ref[...] = y.reshape(o_ref.shape)


def _fft_b_kernel(a_ref, gh_ref, gl_ref, o_ref):
    _, q, d = a_ref.shape
    x = _mm_split(gh_ref[...], gl_ref[...], a_ref[...].reshape(2 * q, d), True)
    o_ref[...] = x.reshape(o_ref.shape)


def _fft_bc_kernel(a_ref, gb_ref, gc_ref, k_ref, z_ref):
    _, q, d = a_ref.shape
    x = _mm_split(gb_ref[...], None, a_ref[...].reshape(2 * q, d), False)
    xr, xi = x[:q], x[q:]
    kr, ki = k_ref[0], k_ref[1]
    y = jnp.concatenate([xr * kr - xi * ki, xr * ki + xi * kr], axis=0)
    z = _mm_split(gc_ref[...], None, y, False)
    z_ref[...] = z.reshape(z_ref.shape)


def _fft_stage_a(x4, fa, p, q, d, precise):
    pairs = x4.shape[0]
    return pl.pallas_call(
        functools.partial(_fft_a_kernel, precise=precise),
        out_shape=jax.ShapeDtypeStruct((pairs, 2, p, q * d), F32),
        grid=(pairs, q),
        in_specs=[pl.BlockSpec((None, 2, p // 2, d), lambda b, j: (b, 0, 0, j)),
                  _const_spec((2 * p, p)), _const_spec((2 * p, p))],
        out_specs=pl.BlockSpec((None, 2, p, d), lambda b, j: (b, 0, 0, j)),
        compiler_params=_params("parallel", "parallel"),
    )(x4, *fa)


def _long_conv(vx, taps, batch, seq_len):
    n, d = vx.shape
    m = 2 * seq_len
    p, q = _fft_dims(m)
    assert batch % 2 == 0
    pairs = batch // 2
    fa, gb, gc, fd = _dft_consts(p, q)
    g_spec = pl.BlockSpec((None, 2 * q, 2 * q), lambda k1, b: (k1, 0, 0))
    slab = pl.BlockSpec((None, 2, q, d), lambda k1, b: (b, 0, k1, 0))

    zeros = jnp.zeros((seq_len, d), F32)
    kin = jnp.stack([taps[:seq_len], zeros, taps[seq_len:], zeros]).reshape(2, 2, p // 2, q * d)
    ka = _fft_stage_a(kin, fa, p, q, d, True).reshape(2, 2, p * q, d)
    kx = pl.pallas_call(
        _fft_b_kernel,
        out_shape=jax.ShapeDtypeStruct((2, 2, p * q, d), F32),
        grid=(p, 2),
        in_specs=[slab, g_spec, g_spec],
        out_specs=slab,
        compiler_params=_params("parallel", "parallel"),
    )(ka, *gb)
    sign = jnp.repeat(1.0 - 2.0 * (jnp.arange(p) % 2).astype(F32), q)[None, :, None]
    kspec = kx[0] + sign * kx[1]

    xa = _fft_stage_a(vx.reshape(pairs, 2, p // 2, q * d), fa, p, q, d, False).reshape(pairs, 2, p * q, d)
    z = pl.pallas_call(
        _fft_bc_kernel,
        out_shape=jax.ShapeDtypeStruct((pairs, 2, p * q, d), F32),
        grid=(p, pairs),
        in_specs=[slab, g_spec, g_spec, pl.BlockSpec((2, q, d), lambda k1, b: (0, k1, 0))],
        out_specs=slab,
        compiler_params=_params("parallel", "parallel"),
    )(xa, gb[0], gc[0], kspec)
    y = pl.pallas_call(
        functools.partial(_fft_a_kernel, precise=False),
        out_shape=jax.ShapeDtypeStruct((pairs, 2, p // 2, q * d), F32),
        grid=(pairs, q),
        in_specs=[pl.BlockSpec((None, 2, p, d), lambda b, j: (b, 0, 0, j)),
                  _const_spec((p, 2 * p)), _const_spec((p, 2 * p))],
        out_specs=pl.BlockSpec((None, 2, p // 2, d), lambda b, j: (b, 0, 0, j)),
        compiler_params=_params("parallel", "parallel"),
    )(z.reshape(pairs, 2, p, q * d), *fd)
    return y.reshape(n, d)


def _hyena_out_prologue(y, vx, x0, skip):
    return (y + vx * skip) * x0


def _hyena_layer(x, batch, seq_len, g, p):
    n, d = x.shape
    x0, vx = _hyena_in(x, g, p['w_in'], p['b_in'], p['conv_w'], p['conv_b'], seq_len)
    taps = _hyena_filter_taps(seq_len, d, p['f_w1'], p['f_b1'], p['f_w2'], p['f_b2'],
                              p['f_w3'], p['f_b3'], p['f_freq'], p['decay'])
    y = _long_conv(vx, taps, batch, seq_len)
    return _mm_res(_hyena_out_prologue, [y, vx, x0], [p['skip'].reshape(1, d)],
                   p['w_out'], p['b_out'], x)


def _head_norm_rope(xh, gain, cos, sin, half):
    y = _rms(xh, gain)
    lane = lax.broadcasted_iota(jnp.int32, (1, HEAD_DIM), 1)
    fwd = pltpu.roll(y, HEAD_DIM - half, 1)
    bwd = pltpu.roll(y, half, 1)
    partner = jnp.where((lane % (2 * half)) < half, fwd, bwd)
    return y * cos + partner * sin


def _qkv_rope_kernel(x_ref, g_ref, w_ref, qg_ref, kg_ref, cos_ref, sin_ref,
                     q_ref, k_ref, v_ref, *, nq, nk, nv, half):
    xb = _rms(x_ref[...], g_ref[...]).astype(BF16)
    cos = cos_ref[...]
    sin = sin_ref[...]
    per = MXU_DIM // HEAD_DIM
    for h0 in range(0, nq + nk + nv, per):
        pw = jnp.dot(xb, w_ref[:, h0 * HEAD_DIM:(h0 + per) * HEAD_DIM], preferred_element_type=F32)
        for h in range(h0, h0 + per):
            ph = pw[:, (h - h0) * HEAD_DIM:(h - h0 + 1) * HEAD_DIM]
            if h < nq:
                sl = slice(h * HEAD_DIM, (h + 1) * HEAD_DIM)
                q_ref[:, sl] = _head_norm_rope(ph, qg_ref[...], cos, sin, half).astype(BF16)
            elif h < nq + nk:
                sl = slice((h - nq) * HEAD_DIM, (h - nq + 1) * HEAD_DIM)
                k_ref[:, sl] = _head_norm_rope(ph, kg_ref[...], cos, sin, half).astype(BF16)
            else:
                sl = slice((h - nq - nk) * HEAD_DIM, (h - nq - nk + 1) * HEAD_DIM)
                v_ref[:, sl] = ph.astype(BF16)


def _qkv_rope(x, g, w_qkv, q_gain, k_gain, cos, sin, nq, nk, nv, half, seq_len, tm=512):
    n, d = x.shape
    tm = min(tm, seq_len)
    pos_blocks = seq_len // tm
    f = w_qkv.shape[1]
    row = pl.BlockSpec((tm, d), lambda i: (i, 0))
    tab = pl.BlockSpec((tm, HEAD_DIM), lambda i: (i % pos_blocks, 0))
    outs = tuple(jax.ShapeDtypeStruct((n, c * HEAD_DIM), BF16) for c in (nq, nk, nv))
    return pl.pallas_call(
        functools.partial(_qkv_rope_kernel, nq=nq, nk=nk, nv=nv, half=half),
        out_shape=outs,
        grid=(n // tm,),
        in_specs=[row, _const_spec((1, d)), _const_spec((d, f)),
                  _const_spec((1, HEAD_DIM)), _const_spec((1, HEAD_DIM)), tab, tab],
        out_specs=tuple(pl.BlockSpec((tm, c * HEAD_DIM), lambda i: (i, 0)) for c in (nq, nk, nv)),
        compiler_params=_params("parallel"),
    )(x, g.reshape(1, d), w_qkv.astype(BF16), q_gain.reshape(1, HEAD_DIM),
      k_gain.reshape(1, HEAD_DIM), cos, sin)


def _axial_tables(L):
    t = jnp.arange(L)
    r = (t // GRID_W).astype(F32)
    c = (t % GRID_W).astype(F32)
    nf = HEAD_DIM // 4
    inv = AXIAL_THETA ** (-(2.0 * jnp.arange(nf, dtype=F32)) / (2 * nf))
    ar, ac = r[:, None] * inv[None], c[:, None] * inv[None]
    cos = jnp.concatenate([jnp.cos(ar), jnp.cos(ar), jnp.cos(ac), jnp.cos(ac)], axis=-1)
    sin = jnp.concatenate([-jnp.sin(ar), jnp.sin(ar), -jnp.sin(ac), jnp.sin(ac)], axis=-1)
    return cos, sin


FLASH_SAFE_BOUND = 40.0
FLASH_BOUND_MARGIN = 1.001


def _flash_kernel(q_ref, k_ref, v_ref, o_ref, acc_scr, off_scr, lsum_scr, m_scr, l_scr, kn_scr,
                  *, tq, tk, tkf, seq_len, group):
    scale = HEAD_DIM ** -0.5
    c = scale * math.log2(math.e)

    @pl.when(pl.program_id(2) == 0)
    def _():
        def norm_step(j, mx):
            kt = k_ref[pl.ds(pl.multiple_of(j * tk, tk), tk), :].astype(F32)
            row = jnp.sum(kt * kt, axis=1, keepdims=True)
            return jnp.maximum(mx, jnp.max(row, axis=0, keepdims=True))

        kmax2 = lax.fori_loop(0, seq_len // tk, norm_step, jnp.zeros((1, 1), F32))
        kn_scr[...] = jnp.broadcast_to(kmax2, kn_scr.shape)

    kmax2 = kn_scr[0:1, 0:1]
    bmax = jnp.zeros((1, 1), F32)
    for h in range(group):
        qf = q_ref[:, h * HEAD_DIM:(h + 1) * HEAD_DIM].astype(F32)
        b = jnp.sqrt(jnp.sum(qf * qf, axis=1, keepdims=True) * kmax2) * FLASH_BOUND_MARGIN
        bmax = jnp.maximum(bmax, jnp.max(b, axis=0, keepdims=True))
        off_scr[h] = jnp.broadcast_to(b * c, (tq, HEAD_DIM))
    fast = (bmax * scale)[0, 0] <= FLASH_SAFE_BOUND

    @pl.when(fast)
    def _():
        acc_scr[...] = jnp.zeros_like(acc_scr)
        lsum_scr[...] = jnp.zeros_like(lsum_scr)
        n_lane_tiles = tkf // HEAD_DIM

        def body(j, carry):
            start = pl.multiple_of(j * tkf, tkf)
            kt = k_ref[pl.ds(start, tkf), :]
            vt = v_ref[pl.ds(start, tkf), :]
            for h in range(group):
                q = q_ref[:, h * HEAD_DIM:(h + 1) * HEAD_DIM]
                s = lax.dot_general(q, kt, (((1,), (1,)), ((), ())), preferred_element_type=F32)
                off = off_scr[h]
                p = jnp.exp2(s * c - jnp.concatenate([off] * n_lane_tiles, axis=1))
                part = p[:, 0:HEAD_DIM]
                for t in range(1, n_lane_tiles):
                    part = part + p[:, t * HEAD_DIM:(t + 1) * HEAD_DIM]
                lsum_scr[h] += part
                acc_scr[h] += jnp.dot(p.astype(BF16), vt, preferred_element_type=F32)
            return carry

        lax.fori_loop(0, seq_len // tkf, body, 0)
        for h in range(group):
            l = jnp.sum(lsum_scr[h], axis=1, keepdims=True)
            o_ref[:, h * HEAD_DIM:(h + 1) * HEAD_DIM] = (acc_scr[h] / l).astype(o_ref.dtype)

    @pl.when(jnp.logical_not(fast))
    def _():
        m_scr[...] = jnp.full(m_scr.shape, -jnp.inf, F32)
        l_scr[...] = jnp.zeros_like(l_scr)
        acc_scr[...] = jnp.zeros_like(acc_scr)

        def body(j, carry):
            start = pl.multiple_of(j * tk, tk)
            kt = k_ref[pl.ds(start, tk), :]
            vt = v_ref[pl.ds(start, tk), :]
            for h in range(group):
                q = q_ref[:, h * HEAD_DIM:(h + 1) * HEAD_DIM]
                s = lax.dot_general(q, kt, (((1,), (1,)), ((), ())), preferred_element_type=F32)
                m = m_scr[h]
                m_new = jnp.maximum(m, jnp.max(s, axis=1, keepdims=True))
                alpha = jnp.exp2((m - m_new) * c)
                p = jnp.exp2(s * c - m_new * c)
                l_scr[h] = alpha * l_scr[h] + jnp.sum(p, axis=1, keepdims=True)
                acc_scr[h] = alpha * acc_scr[h] + jnp.dot(p.astype(BF16), vt, preferred_element_type=F32)
                m_scr[h] = m_new
            return carry

        lax.fori_loop(0, seq_len // tk, body, 0)
        for h in range(group):
            o_ref[:, h * HEAD_DIM:(h + 1) * HEAD_DIM] = (acc_scr[h] / l_scr[h]).astype(o_ref.dtype)


def _flash_gqa(q, k, v, batch, seq_len, tq=256, tk=512, tkf=1024):
    tq = min(tq, seq_len)
    tk = min(tk, seq_len)
    tkf = min(tkf, seq_len)
    nq = seq_len // tq
    gw = GA_GROUP * HEAD_DIM
    wide = pltpu.VMEM((GA_GROUP, tq, HEAD_DIM), F32)
    thin = pltpu.VMEM((GA_GROUP, tq, 1), F32)
    return pl.pallas_call(
        functools.partial(_flash_kernel, tq=tq, tk=tk, tkf=tkf, seq_len=seq_len, group=GA_GROUP),
        out_shape=jax.ShapeDtypeStruct(q.shape, BF16),
        grid=(batch, GA_KV_HEADS, nq),
        in_specs=[pl.BlockSpec((tq, gw), lambda b, kv, i: (b * nq + i, kv)),
                  pl.BlockSpec((seq_len, HEAD_DIM), lambda b, kv, i: (b, kv)),
                  pl.BlockSpec((seq_len, HEAD_DIM), lambda b, kv, i: (b, kv))],
        out_specs=pl.BlockSpec((tq, gw), lambda b, kv, i: (b * nq + i, kv)),
        scratch_shapes=[wide, wide, wide, thin, thin, pltpu.VMEM((HALO_ROWS, HEAD_DIM), F32)],
        compiler_params=_params("parallel", "parallel", "arbitrary"),
    )(q, k, v)


def _identity_prologue(o):
    return o


def _gqa_layer(x, batch, seq_len, g, p):
    cos, sin = _axial_tables(seq_len)
    q, k, v = _qkv_rope(x, g, p['w_qkv'], p['q_gain'], p['k_gain'], cos, sin,
                        GA_HEADS, GA_KV_HEADS, GA_KV_HEADS, HEAD_DIM // 4, seq_len)
    o = _flash_gqa(q, k, v, batch, seq_len)
    return _mm_res(_identity_prologue, [o], [], p['w_o'], None, x)


def _ml_in_kernel(x_ref, xp_ref, xn_ref, g_ref, w_ref, cw_ref, cb_ref, wq_ref, wk_ref, wv_ref,
                  wg_ref, bg_ref, q_ref, k_ref, v_ref, xc_ref, sz_ref, gate_ref,
                  *, seq_len, tm, inner, k_scale):
    keep_prev, keep_next = _edge_scales(pl.program_id(0), tm, seq_len)
    g = g_ref[...]
    xb = _rms(x_ref[...], g).astype(BF16)
    hb = _rms(jnp.concatenate([xp_ref[...], xn_ref[...]], axis=0), g).astype(BF16)
    rows = lax.broadcasted_iota(jnp.int32, (tm, 1), 0)
    gacc = jnp.zeros(gate_ref.shape, F32)
    for t in range(inner // MXU_DIM):
        sl = slice(t * MXU_DIM, (t + 1) * MXU_DIM)
        w = w_ref[:, sl]
        xm = jnp.dot(xb, w, preferred_element_type=F32)
        xh = jnp.dot(hb, w, preferred_element_type=F32)
        pprev = xh[HALO_ROWS - 1:HALO_ROWS, :] * keep_prev
        pnext = xh[HALO_ROWS:HALO_ROWS + 1, :] * keep_next
        xc = _conv3_rows(xm, pprev, pnext, cw_ref[:, sl], cb_ref[:, sl], rows, tm)
        xc = xc * jax.nn.sigmoid(xc)
        z = jnp.dot(xb, w_ref[:, inner + t * MXU_DIM:inner + (t + 1) * MXU_DIM],
                    preferred_element_type=F32)
        xcb = xc.astype(BF16)
        q = jnp.dot(xcb, wq_ref[t], preferred_element_type=F32)
        k = jnp.dot(xcb, wk_ref[t], preferred_element_type=F32)
        v = jnp.dot(xm.astype(BF16), wv_ref[t], preferred_element_type=F32)
        qb, kb, vb = q.astype(BF16), k.astype(BF16), v.astype(BF16)
        gacc += (jnp.dot(qb, wg_ref[0, sl, :], preferred_element_type=F32)
                 + jnp.dot(kb, wg_ref[1, sl, :], preferred_element_type=F32)
                 + jnp.dot(vb, wg_ref[2, sl, :], preferred_element_type=F32))
        q_ref[:, sl] = qb
        k_ref[:, sl] = (k * k_scale).astype(BF16)
        v_ref[:, sl] = vb
        xc_ref[:, sl] = xc
        sz_ref[:, sl] = z * jax.nn.sigmoid(z)
    gate_ref[...] = gacc + bg_ref[...]


def _block_diag_tiles(w):
    nb, c, _ = w.shape
    per = MXU_DIM // c
    wt = w.reshape(nb // per, per, c, c)
    eye = jnp.eye(per, dtype=w.dtype)
    full = jnp.einsum('tpcd,pq->tpcqd', wt, eye)
    return full.reshape(nb // per, MXU_DIM, MXU_DIM)


def _ml_in(x, g, p, seq_len, tm=256):
    n, d = x.shape
    tm = min(tm, seq_len)
    inner = p['w_up'].shape[1] // 2
    ng = 4 * ML_HEADS
    dh = inner // ML_HEADS
    wq, wk, wv = (_block_diag_tiles(p[nm]).astype(BF16) for nm in ('w_q', 'w_k', 'w_v'))
    wg = jnp.transpose(p['w_gate'], (1, 2, 0, 3)).reshape(3, inner, ng).astype(BF16)
    bg = p['b_gate'].reshape(1, ng)
    prev, nxt = _halo_specs(tm, d, n)
    row = pl.BlockSpec((tm, d), lambda i: (i, 0))
    wide = pl.BlockSpec((tm, inner), lambda i: (i, 0))
    nt = inner // MXU_DIM
    return pl.pallas_call(
        functools.partial(_ml_in_kernel, seq_len=seq_len, tm=tm, inner=inner, k_scale=dh ** -0.5),
        out_shape=(jax.ShapeDtypeStruct((n, inner), BF16),) * 3
        + (jax.ShapeDtypeStruct((n, inner), F32),) * 2
        + (jax.ShapeDtypeStruct((n, ng), F32),),
        grid=(n // tm,),
        in_specs=[row, prev, nxt, _const_spec((1, d)), _const_spec((d, 2 * inner)),
                  _const_spec((3, inner)), _const_spec((1, inner)),
                  _const_spec((nt, MXU_DIM, MXU_DIM)), _const_spec((nt, MXU_DIM, MXU_DIM)),
                  _const_spec((nt, MXU_DIM, MXU_DIM)), _const_spec((3, inner, ng)),
                  _const_spec((1, ng))],
        out_specs=(wide,) * 5 + (pl.BlockSpec((tm, ng), lambda i: (i, 0)),),
        compiler_params=_params("parallel"),
    )(x, x, x, g.reshape(1, d), p['w_up'].astype(BF16), p['conv_w'],
      p['conv_b'].reshape(1, inner), wq, wk, wv, wg, bg)


def _log_sigmoid(x):
    return jnp.minimum(x, 0.0) - jnp.log1p(jnp.exp(-jnp.abs(x)))


def _mlstm_chunk_kernel(q_ref, k_ref, v_ref, gc_ref, gr_ref, h_ref, c_scr, n_scr, m_scr, *, lc):
    d = pl.program_id(2)

    @pl.when(pl.program_id(3) == 0)
    def _():
        c_scr[...] = jnp.zeros_like(c_scr)
        n_scr[...] = jnp.zeros_like(n_scr)
        m_scr[...] = jnp.zeros_like(m_scr)

    q = q_ref[...]
    k = k_ref[...]
    v = v_ref[...]
    gc = gc_ref[...]
    gr = gr_ref[...]
    i_col, g_col = gc[:, 0:1], gc[:, 1:2]
    i_row, g_row = gr[0:1, :], gr[1:2, :]

    jr = lax.broadcasted_iota(jnp.int32, (lc, lc), 0)
    sc = lax.broadcasted_iota(jnp.int32, (lc, lc), 1)
    seen = ((sc - jr) * (1 - 2 * d)) <= 0
    g_tot = jnp.where(d == 0, g_row[:, lc - 1:lc], g_row[:, 0:1])
    m_old = m_scr[0:1, 0:1]

    dmat = jnp.where(seen, g_col - g_row + i_row, -jnp.inf)
    inter = g_col + m_old
    m_q = jnp.maximum(inter, jnp.max(dmat, axis=1, keepdims=True))
    s_qk = lax.dot_general(q, k, (((1,), (1,)), ((), ())), preferred_element_type=F32)
    a = s_qk * jnp.exp(dmat - m_q)
    w_int = jnp.exp(inter - m_q)
    q_c = jnp.dot(q, c_scr[...].astype(BF16), preferred_element_type=F32)
    num = jnp.dot(a.astype(BF16), v, preferred_element_type=F32) + q_c * w_int
    q_n = jnp.sum(q.astype(F32) * n_scr[...], axis=1, keepdims=True)
    den = jnp.sum(a, axis=1, keepdims=True) + w_int * q_n
    den = jnp.maximum(jnp.abs(den), jnp.exp(-m_q))
    h_ref[...] = num / den

    a_row = g_tot - g_row + i_row
    m_new = jnp.maximum(g_tot + m_old, jnp.max(a_row, axis=1, keepdims=True))
    ws_col = jnp.exp(g_tot - g_col + i_col - m_new)
    dec = jnp.exp(g_tot + m_old - m_new)
    kw = k.astype(F32) * ws_col
    upd = lax.dot_general(kw.astype(BF16), v, (((0,), (0,)), ((), ())), preferred_element_type=F32)
    c_scr[...] = dec * c_scr[...] + upd
    n_scr[...] = dec * n_scr[...] + jnp.sum(kw, axis=0, keepdims=True)
    m_scr[...] = jnp.broadcast_to(m_new, m_scr.shape)


def _ml_gate_kernel(g_ref, o_ref, *, lc, nh):
    g = g_ref[...]
    jr = lax.broadcasted_iota(jnp.int32, (lc, lc), 0)
    sc = lax.broadcasted_iota(jnp.int32, (lc, lc), 1)
    tril = jnp.where(sc <= jr, 1.0, 0.0).astype(BF16)
    triu = jnp.where(sc >= jr, 1.0, 0.0).astype(BF16)
    col = lax.broadcasted_iota(jnp.int32, (1, 4 * nh), 1)
    lf = _log_sigmoid(g)
    x1 = lf.astype(BF16)
    r1 = lf - x1.astype(F32)
    x2 = r1.astype(BF16)
    x3 = (r1 - x2.astype(F32)).astype(BF16)
    cum = lambda m: (jnp.dot(m, x1, preferred_element_type=F32) + jnp.dot(m, x2, preferred_element_type=F32)
                     + jnp.dot(m, x3, preferred_element_type=F32))
    gsum = jnp.where(col < 2 * nh, cum(tril), cum(triu))
    o_ref[...] = jnp.where((col % (2 * nh)) >= nh, gsum, g)


def _mlstm_chunks(q, k, v, gates, batch, seq_len):
    n, inner = q.shape
    dh = inner // ML_HEADS
    lc = min(ML_CHUNK, seq_len)
    nc = seq_len // lc
    ng = gates.shape[1]
    gates = pl.pallas_call(
        functools.partial(_ml_gate_kernel, lc=lc, nh=ML_HEADS),
        out_shape=jax.ShapeDtypeStruct((n, ng), F32),
        grid=(n // lc,),
        in_specs=[pl.BlockSpec((lc, ng), lambda i: (i, 0))],
        out_specs=pl.BlockSpec((lc, ng), lambda i: (i, 0)),
        compiler_params=_params("parallel"),
    )(gates)
    g4 = gates.reshape(n, 2, 2, ML_HEADS)
    gcol = jnp.transpose(g4, (1, 3, 0, 2))
    grow = jnp.transpose(g4, (1, 3, 2, 0))

    def chunk(b, c, dd):
        return b * nc + c + dd * (nc - 1 - 2 * c)

    qkv_spec = pl.BlockSpec((lc, dh), lambda b, h, dd, c: (chunk(b, c, dd), h))
    return pl.pallas_call(
        functools.partial(_mlstm_chunk_kernel, lc=lc),
        out_shape=jax.ShapeDtypeStruct((2, n, inner), F32),
        grid=(batch, ML_HEADS, 2, nc),
        in_specs=[qkv_spec, qkv_spec, qkv_spec,
                  pl.BlockSpec((None, None, lc, 2), lambda b, h, dd, c: (dd, h, chunk(b, c, dd), 0)),
                  pl.BlockSpec((None, None, 2, lc), lambda b, h, dd, c: (dd, h, 0, chunk(b, c, dd)))],
        out_specs=pl.BlockSpec((None, lc, dh), lambda b, h, dd, c: (dd, chunk(b, c, dd), h)),
        scratch_shapes=[pltpu.VMEM((dh, dh), F32), pltpu.VMEM((1, dh), F32),
                        pltpu.VMEM((HALO_ROWS, HEAD_DIM), F32)],
        compiler_params=_params("parallel", "parallel", "parallel", "arbitrary"),
    )(q, k, v, gcol, grow)


def _ml_out_prologue(hf, hb, xc, sz, gain, skip):
    h = hf + hb
    dh = h.shape[1] // ML_HEADS
    parts = []
    for i in range(ML_HEADS):
        sl = slice(i * dh, (i + 1) * dh)
        parts.append(_rms(h[:, sl], gain[:, sl]))
    hn = jnp.concatenate(parts, axis=1)
    return (hn + skip * xc) * sz


def _mlstm_layer(x, batch, seq_len, g, p):
    q, k, v, xc, sz, gates = _ml_in(x, g, p, seq_len)
    inner = q.shape[1]
    hs = _mlstm_chunks(q, k, v, gates, batch, seq_len)
    return _mm_res(_ml_out_prologue, [hs[0], hs[1], xc, sz],
                   [p['norm_gain'].reshape(1, inner), p['skip'].reshape(1, inner)],
                   p['w_down'], None, x, tm=256)


def _rope_tables(L):
    inv = ROPE_THETA ** (-(2.0 * jnp.arange(ROPE_DIMS // 2, dtype=F32)) / ROPE_DIMS)
    ang = jnp.arange(L, dtype=F32)[:, None] * inv[None]
    pad = HEAD_DIM - ROPE_DIMS
    cos = jnp.concatenate([jnp.cos(ang), jnp.cos(ang), jnp.ones((L, pad), F32)], axis=-1)
    sin = jnp.concatenate([-jnp.sin(ang), jnp.sin(ang), jnp.zeros((L, pad), F32)], axis=-1)
    return cos, sin


def _band_kernel(q_ref, kp_ref, kc_ref, kn_ref, vp_ref, vc_ref, vn_ref, o_ref, lse_ref,
                 *, s_len, heads):
    i = pl.program_id(2)
    qb = BAND_BLOCK
    w = qb + 2 * BAND_HALF
    a = lax.broadcasted_iota(jnp.int32, (qb, w), 0)
    c = lax.broadcasted_iota(jnp.int32, (qb, w), 1)
    rel = c - BAND_HALF - a
    key_pos = i * qb - BAND_HALF + c
    valid = (jnp.abs(rel) <= BAND_HALF) & (key_pos >= 0) & (key_pos < s_len)
    scale = HEAD_DIM ** -0.5
    for h in range(heads):
        sl = slice(h * HEAD_DIM, (h + 1) * HEAD_DIM)
        kw = jnp.concatenate([kp_ref[qb - BAND_HALF:, sl], kc_ref[:, sl], kn_ref[:BAND_HALF, sl]], axis=0)
        vw = jnp.concatenate([vp_ref[qb - BAND_HALF:, sl], vc_ref[:, sl], vn_ref[:BAND_HALF, sl]], axis=0)
        s = lax.dot_general(q_ref[:, sl], kw, (((1,), (1,)), ((), ())),
                            preferred_element_type=F32) * scale
        s = jnp.where(valid, s, -jnp.inf)
        m = jnp.max(s, axis=1, keepdims=True)
        p = jnp.exp(s - m)
        l = jnp.sum(p, axis=1, keepdims=True)
        o = jnp.dot(p.astype(BF16), vw, preferred_element_type=F32)
        o_ref[:, sl] = o / l
        lse_ref[:, sl] = jnp.broadcast_to(m + jnp.log(l), (qb, HEAD_DIM))


def _band_attention(q, k, v, batch, seq_len, gi, dil):
    n = q.shape[0]
    s_len = seq_len // dil
    nb = s_len // BAND_BLOCK
    hw = DA_HEADS_PER_GROUP * HEAD_DIM
    groups = len(DA_GROUPS)
    view = lambda a: a.reshape(batch, s_len, dil * a.shape[1])
    qv, kv, vv = view(q), view(k), view(v)

    def spec(shift):
        return pl.BlockSpec(
            (None, BAND_BLOCK, hw),
            lambda b, r, i: (b, jnp.clip(i + shift, 0, nb - 1), r * groups + gi))

    out_spec = pl.BlockSpec((None, BAND_BLOCK, hw), lambda b, r, i: (b, i, r))
    o, lse = pl.pallas_call(
        functools.partial(_band_kernel, s_len=s_len, heads=DA_HEADS_PER_GROUP),
        out_shape=(jax.ShapeDtypeStruct((batch, s_len, dil * hw), F32),) * 2,
        grid=(batch, dil, nb),
        in_specs=[spec(0), spec(-1), spec(0), spec(1), spec(-1), spec(0), spec(1)],
        out_specs=(out_spec, out_spec),
        compiler_params=_params("parallel", "parallel", "parallel"),
    )(qv, kv, kv, kv, vv, vv, vv)
    return o.reshape(n, hw), lse.reshape(n, hw)


def _da_out_prologue(o0, o1, o2, l0, l1, l2):
    m = jnp.maximum(jnp.maximum(l0, l1), l2)
    e0, e1, e2 = jnp.exp(l0 - m), jnp.exp(l1 - m), jnp.exp(l2 - m)
    return (e0 * o0 + e1 * o1 + e2 * o2) / (e0 + e1 + e2)


def _dilated_layer(x, batch, seq_len, g, p):
    cos, sin = _rope_tables(seq_len)
    q, k, v = _qkv_rope(x, g, p['w_qkv'], p['q_gain'], p['k_gain'], cos, sin,
                        DA_HEADS, DA_HEADS, DA_HEADS, ROPE_DIMS // 2, seq_len, tm=256)
    outs, lses = [], []
    for gi, (_, dil) in enumerate(DA_GROUPS):
        o, lse = _band_attention(q, k, v, batch, seq_len, gi, dil)
        outs.append(o)
        lses.append(lse)
    return _mm_res(_da_out_prologue, outs + lses, [], p['w_o'], None, x)


def _router_kernel(x_ref, g_ref, wr_ref, xn_ref, aff_ref):
    xn = _rms(x_ref[...], g_ref[...])
    xn_ref[...] = xn.astype(BF16)
    logits = lax.dot_general(wr_ref[...], xn, (((1,), (1,)), ((), ())),
                             precision=HIGHEST, preferred_element_type=F32)
    m = jnp.max(logits, axis=0, keepdims=True)
    e = jnp.exp(logits - m)
    aff_ref[...] = e / jnp.sum(e, axis=0, keepdims=True)


def _router(x, g, w_router, tm=512):
    n, d = x.shape
    tm = min(tm, n)
    e = w_router.shape[1]
    return pl.pallas_call(
        _router_kernel,
        out_shape=(jax.ShapeDtypeStruct((n, d), BF16), jax.ShapeDtypeStruct((e, n), F32)),
        grid=(n // tm,),
        in_specs=[pl.BlockSpec((tm, d), lambda i: (i, 0)), _const_spec((1, d)), _const_spec((e, d))],
        out_specs=(pl.BlockSpec((tm, d), lambda i: (i, 0)), pl.BlockSpec((e, tm), lambda i: (0, i))),
        compiler_params=_params("parallel"),
    )(x, g.reshape(1, d), w_router.T)


def _expert_ffn_kernel(xe_ref, gate_ref, w1_ref, w3_ref, w2_ref, ye_ref):
    xe = xe_ref[...]
    h1 = jnp.dot(xe, w1_ref[...], preferred_element_type=F32)
    h3 = jnp.dot(xe, w3_ref[...], preferred_element_type=F32)
    hid = (h1 * jax.nn.sigmoid(h1) * h3).astype(BF16)
    ye_ref[...] = jnp.dot(hid, w2_ref[...], preferred_element_type=F32) * gate_ref[...]


def _expert_ffn(xe, gates, w1, w3, w2, tm=512):
    e, c, d = xe.shape
    f = w1.shape[2]
    tm = min(tm, c)
    return pl.pallas_call(
        _expert_ffn_kernel,
        out_shape=jax.ShapeDtypeStruct((e, c, d), F32),
        grid=(e, c // tm),
        in_specs=[pl.BlockSpec((None, tm, d), lambda ei, ci: (ei, ci, 0)),
                  pl.BlockSpec((None, tm, 1), lambda ei, ci: (ei, ci, 0)),
                  pl.BlockSpec((None, d, f), lambda ei, ci: (ei, 0, 0)),
                  pl.BlockSpec((None, d, f), lambda ei, ci: (ei, 0, 0)),
                  pl.BlockSpec((None, f, d), lambda ei, ci: (ei, 0, 0))],
        out_specs=pl.BlockSpec((None, tm, d), lambda ei, ci: (ei, ci, 0)),
        compiler_params=_params("parallel", "arbitrary"),
    )(xe, gates[..., None], w1, w3, w2)


def _moe_layer(x, group_sizes, g, w_router, w1, w3, w2, split_output):
    n, d = x.shape
    xn, aff_t = _router(x, g, w_router)
    xes, gts, idxs = [], [], []
    start = 0
    for ng in group_sizes:
        cap = EC_CAPACITY * ng // N_EXPERTS
        gates, idx = lax.top_k(aff_t[:, start:start + ng], cap)
        idx = idx + start
        xes.append(xn[idx])
        gts.append(gates)
        idxs.append(idx)
        start += ng
    ye = _expert_ffn(jnp.concatenate(xes, axis=1), jnp.concatenate(gts, axis=1),
                     w1.astype(BF16), w3.astype(BF16), w2.astype(BF16))
    idx_all = jnp.concatenate(idxs, axis=1)
    return _combine(x, ye.reshape(-1, d), idx_all.reshape(-1), group_sizes if split_output else (n,))


COMBINE_TOKENS = 256
COMBINE_ROWS = 256


def _combine_kernel(tile_ref, blk_ref, live_ref, x_ref, tok_ref, ye_ref, *o_refs, split_tiles):
    w = pl.program_id(0)
    tile = tile_ref[w]
    first = jnp.logical_or(w == 0, tile != tile_ref[jnp.maximum(w - 1, 0)])
    tt = x_ref.shape[0]
    rows = lax.broadcasted_iota(jnp.int32, (tt, COMBINE_ROWS), 0)
    onehot = jnp.where(rows == tok_ref[...] - tile * tt, 1.0, 0.0).astype(BF16)
    hi, lo = _split_bf16(ye_ref[...])
    add = jnp.dot(onehot, hi, preferred_element_type=F32) + jnp.dot(onehot, lo, preferred_element_type=F32)
    add = add * live_ref[w].astype(F32)
    lo_tile = 0
    for o_ref, n_tiles in zip(o_refs, split_tiles):
        mine = jnp.logical_and(tile >= lo_tile, tile < lo_tile + n_tiles)

        @pl.when(jnp.logical_and(mine, first))
        def _(o_ref=o_ref):
            o_ref[...] = x_ref[...] + add

        @pl.when(jnp.logical_and(mine, jnp.logical_not(first)))
        def _(o_ref=o_ref):
            o_ref[...] += add

        lo_tile += n_tiles


def _combine(x, ye, tok, splits):
    n, d = x.shape
    p = tok.shape[0]
    tt = min(COMBINE_TOKENS, min(splits))
    rb = COMBINE_ROWS
    assert p % rb == 0 and all(s % tt == 0 for s in splits) and sum(splits) == n
    tiles, nblk = n // tt, p // rb
    split_tiles = tuple(s // tt for s in splits)
    order = jnp.argsort(tok)
    tok_sorted = tok[order].astype(jnp.int32)
    ye_sorted = ye[order]

    edges = jnp.arange(tiles + 1, dtype=jnp.int32) * tt
    bounds = jnp.searchsorted(tok_sorted, edges, method='compare_all').astype(jnp.int32)
    lo, hi = bounds[:-1], bounds[1:]
    first_blk = jnp.minimum(lo // rb, nblk - 1)
    last_blk = jnp.where(hi > lo, (hi - 1) // rb, first_blk)
    n_items = last_blk - first_blk + 1
    item_end = jnp.cumsum(n_items)
    item_start = item_end - n_items
    max_items = nblk + 2 * tiles
    w = jnp.arange(max_items, dtype=jnp.int32)
    tile = jnp.minimum(jnp.searchsorted(item_end, w, side='right', method='compare_all'),
                       tiles - 1).astype(jnp.int32)
    k = w - item_start[tile]
    live = (k < n_items[tile]).astype(jnp.int32)
    blk = jnp.minimum(first_blk[tile] + k, nblk - 1).astype(jnp.int32)

    def out_spec(lo_tile, n_tiles):
        return pl.BlockSpec(
            (tt, d), lambda i, tile_r, blk_r, live_r: (jnp.clip(tile_r[i] - lo_tile, 0, n_tiles - 1), 0))

    starts = [sum(split_tiles[:j]) for j in range(len(splits))]
    grid_spec = pltpu.PrefetchScalarGridSpec(
        num_scalar_prefetch=3,
        grid=(max_items,),
        in_specs=[pl.BlockSpec((tt, d), lambda i, tile_r, blk_r, live_r: (tile_r[i], 0)),
                  pl.BlockSpec((None, 1, rb), lambda i, tile_r, blk_r, live_r: (blk_r[i], 0, 0)),
                  pl.BlockSpec((rb, d), lambda i, tile_r, blk_r, live_r: (blk_r[i], 0))],
        out_specs=tuple(out_spec(s, t) for s, t in zip(starts, split_tiles)),
    )
    return pl.pallas_call(
        functools.partial(_combine_kernel, split_tiles=split_tiles),
        out_shape=tuple(jax.ShapeDtypeStruct((s, d), F32) for s in splits),
        grid_spec=grid_spec,
        compiler_params=_params("arbitrary"),
    )(tile, blk, live, x, tok_sorted.reshape(nblk, 1, rb), ye_sorted)


def _trunk(x, batch, seq_len, group_sizes, p):
    depth = p['norm_gain'].shape[0]
    for i in range(depth):
        mixer, j = i % 4, i // 4
        g = p['norm_gain'][i, 0]
        if mixer == 0:
            x = _hyena_layer(x, batch, seq_len, g, {k[3:]: v[j] for k, v in p.items() if k.startswith('hy_')})
        elif mixer == 1:
            x = _gqa_layer(x, batch, seq_len, g, {k[3:]: v[j] for k, v in p.items() if k.startswith('ga_')})
        elif mixer == 2:
            x = _mlstm_layer(x, batch, seq_len, g, {k[3:]: v[j] for k, v in p.items() if k.startswith('ml_')})
        else:
            x = _dilated_layer(x, batch, seq_len, g, {k[3:]: v[j] for k, v in p.items() if k.startswith('da_')})
        last = i == depth - 1
        outs = _moe_layer(x, group_sizes, p['norm_gain'][i, 1], p['moe_w_router'][i],
                          p['moe_w1'][i], p['moe_w3'][i], p['moe_w2'][i], split_output=last)
        x = outs if last else outs[0]
    return x


def kernel(x_prompt, x_sample, norm_gain, hy_w_in, hy_b_in, hy_conv_w, hy_conv_b, hy_f_w1, hy_f_b1, hy_f_w2, hy_f_b2, hy_f_w3, hy_f_b3, hy_f_freq, hy_decay, hy_skip, hy_w_out, hy_b_out, ga_w_qkv, ga_q_gain, ga_k_gain, ga_w_o, ml_w_up, ml_conv_w, ml_conv_b, ml_w_q, ml_w_k, ml_w_v, ml_w_gate, ml_b_gate, ml_norm_gain, ml_skip, ml_w_down, da_w_qkv, da_q_gain, da_k_gain, da_w_o, moe_w_router, moe_w1, moe_w3, moe_w2):
    p = dict(
        norm_gain=norm_gain,
        hy_w_in=hy_w_in, hy_b_in=hy_b_in, hy_conv_w=hy_conv_w, hy_conv_b=hy_conv_b,
        hy_f_w1=hy_f_w1, hy_f_b1=hy_f_b1, hy_f_w2=hy_f_w2, hy_f_b2=hy_f_b2,
        hy_f_w3=hy_f_w3, hy_f_b3=hy_f_b3, hy_f_freq=hy_f_freq, hy_decay=hy_decay,
        hy_skip=hy_skip, hy_w_out=hy_w_out, hy_b_out=hy_b_out,
        ga_w_qkv=ga_w_qkv, ga_q_gain=ga_q_gain, ga_k_gain=ga_k_gain, ga_w_o=ga_w_o,
        ml_w_up=ml_w_up, ml_conv_w=ml_conv_w, ml_conv_b=ml_conv_b, ml_w_q=ml_w_q,
        ml_w_k=ml_w_k, ml_w_v=ml_w_v, ml_w_gate=ml_w_gate, ml_b_gate=ml_b_gate,
        ml_norm_gain=ml_norm_gain, ml_skip=ml_skip, ml_w_down=ml_w_down,
        da_w_qkv=da_w_qkv, da_q_gain=da_q_gain, da_k_gain=da_k_gain, da_w_o=da_w_o,
        moe_w_router=moe_w_router, moe_w1=moe_w1, moe_w3=moe_w3, moe_w2=moe_w2,
    )
    bp, seq_len, d = x_prompt.shape
    bs = x_sample.shape[0]
    assert x_sample.shape[1] == seq_len
    x = jnp.concatenate([x_prompt, x_sample], axis=0).reshape((bp + bs) * seq_len, d)
    y_prompt, y_sample = _trunk(x, bp + bs, seq_len, (bp * seq_len, bs * seq_len), p)
    return (y_prompt.reshape(bp, seq_len, d), y_sample.reshape(bs, seq_len, d))
```

```python
import functools
import math

import jax
import jax.numpy as jnp
from jax import lax
from jax.experimental import pallas as pl
from jax.experimental.pallas import tpu as pltpu

F32 = jnp.float32
BF16 = jnp.bfloat16
HIGHEST = lax.Precision.HIGHEST

NORM_EPS = 1e-6
GRID_W = 64
HY_BANDS = 16
GA_HEADS = 8
GA_KV_HEADS = 2
GA_GROUP = GA_HEADS // GA_KV_HEADS
HEAD_DIM = 128
AXIAL_THETA = 10000.0
ML_HEADS = 4
ML_QKV_BLOCK = 4
DA_GROUPS = ((128, 1), (512, 4), (2048, 16))
DA_HEADS_PER_GROUP = 4
DA_HEADS = DA_HEADS_PER_GROUP * len(DA_GROUPS)
ROPE_THETA = 500000.0
ROPE_DIMS = HEAD_DIM // 4
N_EXPERTS = 16
EC_CAPACITY = 2

VMEM_LIMIT_BYTES = 52 * 1024 * 1024
HALO_ROWS = 8
MXU_DIM = 256
ML_CHUNK = 256
BAND_BLOCK = 128
BAND_HALF = 64


def _params(*sem):
    return pltpu.CompilerParams(dimension_semantics=sem, vmem_limit_bytes=VMEM_LIMIT_BYTES)


def _rms(x, g):
    ms = jnp.mean(x * x, axis=-1, keepdims=True)
    return x * lax.rsqrt(ms + NORM_EPS) * g


def _const_spec(shape):
    nd = len(shape)
    return pl.BlockSpec(shape, lambda *_: (0,) * nd)


def _conv3_rows(p, pprev, pnext, cw, cb, rows, tm):
    up = jnp.where(rows == 0, pprev, pltpu.roll(p, 1, 0))
    dn = jnp.where(rows == tm - 1, pnext, pltpu.roll(p, tm - 1, 0))
    return up * cw[0:1] + p * cw[1:2] + dn * cw[2:3] + cb


def _halo_specs(tm, d, n_rows):
    hb = tm // HALO_ROWS
    last = n_rows // HALO_ROWS - 1
    prev = pl.BlockSpec((HALO_ROWS, d), lambda i: (jnp.maximum(i * hb - 1, 0), 0))
    nxt = pl.BlockSpec((HALO_ROWS, d), lambda i: (jnp.minimum((i + 1) * hb, last), 0))
    return prev, nxt


def _edge_scales(i, tm, seq_len):
    t0 = i * tm
    keep_prev = jnp.where(t0 % seq_len == 0, 0.0, 1.0).astype(F32)
    keep_next = jnp.where((t0 + tm) % seq_len == 0, 0.0, 1.0).astype(F32)
    return keep_prev, keep_next


def _mm_res_kernel(*refs, prologue, n_row, n_const, tn):
    row_refs = refs[:n_row]
    const_refs = refs[n_row:n_row + n_const]
    w_ref, b_ref, res_ref, o_ref = refs[n_row + n_const:]
    lhs = prologue(*[r[...] for r in row_refs], *[c[...] for c in const_refs]).astype(BF16)
    for j in range(o_ref.shape[1] // tn):
        sl = slice(j * tn, (j + 1) * tn)
        o_ref[:, sl] = (res_ref[:, sl] + b_ref[:, sl]
                        + jnp.dot(lhs, w_ref[:, sl], preferred_element_type=F32))


def _mm_res(prologue, rows, consts, w, b, res, tm=512, tn=512):
    n, dout = res.shape
    tm = min(tm, n)
    k = w.shape[0]
    if b is None:
        b = jnp.zeros((1, dout), F32)
    in_specs = [pl.BlockSpec((tm, r.shape[1]), lambda i: (i, 0)) if r.ndim == 2
                else pl.BlockSpec((r.shape[0], tm, r.shape[2]), lambda i: (0, i, 0)) for r in rows]
    in_specs += [_const_spec(c.shape) for c in consts]
    in_specs += [_const_spec((k, dout)), _const_spec((1, dout)),
                 pl.BlockSpec((tm, dout), lambda i: (i, 0))]
    return pl.pallas_call(
        functools.partial(_mm_res_kernel, prologue=prologue, n_row=len(rows),
                          n_const=len(consts), tn=min(tn, dout)),
        out_shape=jax.ShapeDtypeStruct((n, dout), F32),
        grid=(n // tm,),
        in_specs=in_specs,
        out_specs=pl.BlockSpec((tm, dout), lambda i: (i, 0)),
        compiler_params=_params("parallel"),
    )(*rows, *consts, w.astype(BF16), b.reshape(1, dout).astype(F32), res)


def _hyena_in_kernel(x_ref, xp_ref, xn_ref, g_ref, w_ref, b_ref, cw_ref, cb_ref,
                     x0_ref, vx_ref, *, seq_len, tm, d, cols):
    keep_prev, keep_next = _edge_scales(pl.program_id(0), tm, seq_len)
    g = g_ref[...]
    xb = _rms(x_ref[...], g).astype(BF16)
    hb = _rms(jnp.concatenate([xp_ref[...], xn_ref[...]], axis=0), g).astype(BF16)
    rows = lax.broadcasted_iota(jnp.int32, (tm, 1), 0)

    def conv_part(c0):
        sl = slice(c0, c0 + cols)
        w = w_ref[:, sl]
        bias = b_ref[:, sl]
        p = jnp.dot(xb, w, preferred_element_type=F32) + bias
        ph = jnp.dot(hb, w, preferred_element_type=F32) + bias
        pprev = ph[HALO_ROWS - 1:HALO_ROWS, :] * keep_prev
        pnext = ph[HALO_ROWS:HALO_ROWS + 1, :] * keep_next
        return _conv3_rows(p, pprev, pnext, cw_ref[:, sl], cb_ref[:, sl], rows, tm)

    for j in range(d // cols):
        c = j * cols
        x0_ref[:, c:c + cols] = conv_part(c)
        vx_ref[:, c:c + cols] = conv_part(2 * d + c) * conv_part(d + c)


def _hyena_in(x, g, w_in, b_in, conv_w, conv_b, seq_len, tm=256, cols=512):
    n, d = x.shape
    tm = min(tm, seq_len)
    prev, nxt = _halo_specs(tm, d, n)
    row = pl.BlockSpec((tm, d), lambda i: (i, 0))
    return pl.pallas_call(
        functools.partial(_hyena_in_kernel, seq_len=seq_len, tm=tm, d=d, cols=cols),
        out_shape=(jax.ShapeDtypeStruct((n, d), F32), jax.ShapeDtypeStruct((n, d), F32)),
        grid=(n // tm,),
        in_specs=[row, prev, nxt, _const_spec((1, d)), _const_spec((d, 3 * d)),
                  _const_spec((1, 3 * d)), _const_spec((3, 3 * d)), _const_spec((1, 3 * d))],
        out_specs=(row, row),
        compiler_params=_params("parallel"),
    )(x, x, x, g.reshape(1, d), w_in.astype(BF16), b_in.reshape(1, 3 * d),
      conv_w, conv_b.reshape(1, 3 * d))


def _hyena_filter_taps(L, d, f_w1, f_b1, f_w2, f_b2, f_w3, f_b3, f_freq, decay):
    t = jnp.linspace(0.0, 1.0, L, dtype=F32)[:, None]
    w_ang = 2.0 * math.pi * jnp.arange(L, dtype=F32)[:, None] / L
    bands = jnp.linspace(1e-4, HY_BANDS - 1, HY_BANDS, dtype=F32)[None, :]
    z = jnp.concatenate([t, jnp.cos(bands * w_ang), -jnp.sin(bands * w_ang)], axis=-1)
    h = jnp.sin(f_freq[0] * (z @ f_w1 + f_b1))
    h = jnp.sin(f_freq[1] * (h @ f_w2 + f_b2))
    h = h @ f_w3 + f_b3
    h = h.reshape(L, 2, d) * jnp.exp(-t[:, :, None] * jnp.abs(decay)[None])
    k = jnp.concatenate([h[:, 0], jnp.zeros((1, d), F32), h[1:, 1][::-1]], axis=0)
    return k / jnp.sum(jnp.abs(k), axis=0, keepdims=True)


def _fft_dims(m):
    lg = m.bit_length() - 1
    p = 1 << ((lg + 1) // 2)
    return p, m // p


def _split_bf16(x):
    hi = x.astype(BF16)
    return hi, (x - hi.astype(F32)).astype(BF16)


def _mm_split(fh, fl, x, precise):
    if not precise:
        return jnp.dot(fh, x.astype(BF16), preferred_element_type=F32)
    xh, xl = _split_bf16(x)
    return (jnp.dot(fh, xh, preferred_element_type=F32) + jnp.dot(fl, xh, preferred_element_type=F32)
            + jnp.dot(fh, xl, preferred_element_type=F32))


def _cplx_as_real(cr, ci):
    top = jnp.concatenate([cr, -ci], axis=-1)
    bot = jnp.concatenate([ci, cr], axis=-1)
    return jnp.concatenate([top, bot], axis=-2)


def _unit_circle(idx, m):
    ang = (2.0 * math.pi / m) * idx.astype(F32)
    return jnp.cos(ang), jnp.sin(ang)


def _dft_consts(p, q):
    m = p * q
    k1 = jnp.arange(p, dtype=jnp.int32)
    n1 = jnp.arange(p // 2, dtype=jnp.int32)
    c, s = _unit_circle((k1[:, None] * n1[None, :]) % p, p)
    fa = _cplx_as_real(c, -s)
    c, s = _unit_circle((n1[:, None] * k1[None, :]) % p, p)
    fd = _cplx_as_real(c / m, s / m)
    k2 = jnp.arange(q, dtype=jnp.int32)
    n2 = jnp.arange(q, dtype=jnp.int32)
    idx = (n2[None, None, :] * (k2[None, :, None] * p + k1[:, None, None])) % m
    c, s = _unit_circle(idx, m)
    gb = _cplx_as_real(c, -s)
    ct, st = jnp.swapaxes(c, 1, 2), jnp.swapaxes(s, 1, 2)
    gc = _cplx_as_real(ct, st)
    return tuple(_split_bf16(a) for a in (fa, gb, gc, fd))


def _fft_a_kernel(x_ref, fh_ref, fl_ref, o_ref, *, precise):
    _, half, d = x_ref.shape
    y = _mm_split(fh_ref[...], fl_ref[...], x_ref[...].reshape(2 * half, d), precise)
    o_ref[...] = y.reshape(o_ref.shape)


def _fft_b_kernel(a_ref, gh_ref, gl_ref, o_ref):
    _, q, d = a_ref.shape
    x = _mm_split(gh_ref[...], gl_ref[...], a_ref[...].reshape(2 * q, d), True)
    o_ref[...] = x.reshape(o_ref.shape)


def _fft_bc_kernel(a_ref, gb_ref, gc_ref, k_ref, z_ref):
    _, q, d = a_ref.shape
    x = _mm_split(gb_ref[...], None, a_ref[...].reshape(2 * q, d), False)
    xr, xi = x[:q], x[q:]
    kr, ki = k_ref[0], k_ref[1]
    y = jnp.concatenate([xr * kr - xi * ki, xr * ki + xi * kr], axis=0)
    z = _mm_split(gc_ref[...], None, y, False)
    z_ref[...] = z.reshape(z_ref.shape)


def _fft_stage_a(x4, fa, p, q, d, precise):
    pairs = x4.shape[0]
    return pl.pallas_call(
        functools.partial(_fft_a_kernel, precise=precise),
        out_shape=jax.ShapeDtypeStruct((pairs, 2, p, q * d), F32),
        grid=(pairs, q),
        in_specs=[pl.BlockSpec((None, 2, p // 2, d), lambda b, j: (b, 0, 0, j)),
                  _const_spec((2 * p, p)), _const_spec((2 * p, p))],
        out_specs=pl.BlockSpec((None, 2, p, d), lambda b, j: (b, 0, 0, j)),
        compiler_params=_params("parallel", "parallel"),
    )(x4, *fa)


def _long_conv(vx, taps, batch, seq_len):
    n, d = vx.shape
    m = 2 * seq_len
    p, q = _fft_dims(m)
    assert batch % 2 == 0
    pairs = batch // 2
    fa, gb, gc, fd = _dft_consts(p, q)
    g_spec = pl.BlockSpec((None, 2 * q, 2 * q), lambda k1, b: (k1, 0, 0))
    slab = pl.BlockSpec((None, 2, q, d), lambda k1, b: (b, 0, k1, 0))

    zeros = jnp.zeros((seq_len, d), F32)
    kin = jnp.stack([taps[:seq_len], zeros, taps[seq_len:], zeros]).reshape(2, 2, p // 2, q * d)
    ka = _fft_stage_a(kin, fa, p, q, d, True).reshape(2, 2, p * q, d)
    kx = pl.pallas_call(
        _fft_b_kernel,
        out_shape=jax.ShapeDtypeStruct((2, 2, p * q, d), F32),
        grid=(p, 2),
        in_specs=[slab, g_spec, g_spec],
        out_specs=slab,
        compiler_params=_params("parallel", "parallel"),
    )(ka, *gb)
    sign = jnp.repeat(1.0 - 2.0 * (jnp.arange(p) % 2).astype(F32), q)[None, :, None]
    kspec = kx[0] + sign * kx[1]

    xa = _fft_stage_a(vx.reshape(pairs, 2, p // 2, q * d), fa, p, q, d, False).reshape(pairs, 2, p * q, d)
    z = pl.pallas_call(
        _fft_bc_kernel,
        out_shape=jax.ShapeDtypeStruct((pairs, 2, p * q, d), F32),
        grid=(p, pairs),
        in_specs=[slab, g_spec, g_spec, pl.BlockSpec((2, q, d), lambda k1, b: (0, k1, 0))],
        out_specs=slab,
        compiler_params=_params("parallel", "parallel"),
    )(xa, gb[0], gc[0], kspec)
    y = pl.pallas_call(
        functools.partial(_fft_a_kernel, precise=False),
        out_shape=jax.ShapeDtypeStruct((pairs, 2, p // 2, q * d), F32),
        grid=(pairs, q),
        in_specs=[pl.BlockSpec((None, 2, p, d), lambda b, j: (b, 0, 0, j)),
                  _const_spec((p, 2 * p)), _const_spec((p, 2 * p))],
        out_specs=pl.BlockSpec((None, 2, p // 2, d), lambda b, j: (b, 0, 0, j)),
        compiler_params=_params("parallel", "parallel"),
    )(z.reshape(pairs, 2, p, q * d), *fd)
    return y.reshape(n, d)


def _hyena_out_prologue(y, vx, x0, skip):
    return (y + vx * skip) * x0


def _hyena_layer(x, batch, seq_len, g, p):
    n, d = x.shape
    x0, vx = _hyena_in(x, g, p['w_in'], p['b_in'], p['conv_w'], p['conv_b'], seq_len)
    taps = _hyena_filter_taps(seq_len, d, p['f_w1'], p['f_b1'], p['f_w2'], p['f_b2'],
                              p['f_w3'], p['f_b3'], p['f_freq'], p['decay'])
    y = _long_conv(vx, taps, batch, seq_len)
    return _mm_res(_hyena_out_prologue, [y, vx, x0], [p['skip'].reshape(1, d)],
                   p['w_out'], p['b_out'], x)


def _head_norm_rope(xh, gain, cos, sin, half):
    y = _rms(xh, gain)
    lane = lax.broadcasted_iota(jnp.int32, (1, HEAD_DIM), 1)
    fwd = pltpu.roll(y, HEAD_DIM - half, 1)
    bwd = pltpu.roll(y, half, 1)
    partner = jnp.where((lane % (2 * half)) < half, fwd, bwd)
    return y * cos + partner * sin


def _qkv_rope_kernel(x_ref, g_ref, w_ref, qg_ref, kg_ref, cos_ref, sin_ref,
                     q_ref, k_ref, v_ref, *, nq, nk, nv, half):
    xb = _rms(x_ref[...], g_ref[...]).astype(BF16)
    cos = cos_ref[...]
    sin = sin_ref[...]
    per = MXU_DIM // HEAD_DIM
    for h0 in range(0, nq + nk + nv, per):
        pw = jnp.dot(xb, w_ref[:, h0 * HEAD_DIM:(h0 + per) * HEAD_DIM], preferred_element_type=F32)
        for h in range(h0, h0 + per):
            ph = pw[:, (h - h0) * HEAD_DIM:(h - h0 + 1) * HEAD_DIM]
            if h < nq:
                q_ref[h] = _head_norm_rope(ph, qg_ref[...], cos, sin, half).astype(BF16)
            elif h < nq + nk:
                sl = slice((h - nq) * HEAD_DIM, (h - nq + 1) * HEAD_DIM)
                k_ref[:, sl] = _head_norm_rope(ph, kg_ref[...], cos, sin, half).astype(BF16)
            else:
                sl = slice((h - nq - nk) * HEAD_DIM, (h - nq - nk + 1) * HEAD_DIM)
                v_ref[:, sl] = ph.astype(BF16)


def _qkv_rope(x, g, w_qkv, q_gain, k_gain, cos, sin, nq, nk, nv, half, seq_len, tm=512):
    n, d = x.shape
    tm = min(tm, seq_len)
    pos_blocks = seq_len // tm
    f = w_qkv.shape[1]
    row = pl.BlockSpec((tm, d), lambda i: (i, 0))
    tab = pl.BlockSpec((tm, HEAD_DIM), lambda i: (i % pos_blocks, 0))
    outs = (jax.ShapeDtypeStruct((nq, n, HEAD_DIM), BF16),) + tuple(
        jax.ShapeDtypeStruct((n, c * HEAD_DIM), BF16) for c in (nk, nv))
    return pl.pallas_call(
        functools.partial(_qkv_rope_kernel, nq=nq, nk=nk, nv=nv, half=half),
        out_shape=outs,
        grid=(n // tm,),
        in_specs=[row, _const_spec((1, d)), _const_spec((d, f)),
                  _const_spec((1, HEAD_DIM)), _const_spec((1, HEAD_DIM)), tab, tab],
        out_specs=(pl.BlockSpec((nq, tm, HEAD_DIM), lambda i: (0, i, 0)),) + tuple(
            pl.BlockSpec((tm, c * HEAD_DIM), lambda i: (i, 0)) for c in (nk, nv)),
        compiler_params=_params("parallel"),
    )(x, g.reshape(1, d), w_qkv.astype(BF16), q_gain.reshape(1, HEAD_DIM),
      k_gain.reshape(1, HEAD_DIM), cos, sin)


def _axial_tables(L):
    t = jnp.arange(L)
    r = (t // GRID_W).astype(F32)
    c = (t % GRID_W).astype(F32)
    nf = HEAD_DIM // 4
    inv = AXIAL_THETA ** (-(2.0 * jnp.arange(nf, dtype=F32)) / (2 * nf))
    ar, ac = r[:, None] * inv[None], c[:, None] * inv[None]
    cos = jnp.concatenate([jnp.cos(ar), jnp.cos(ar), jnp.cos(ac), jnp.cos(ac)], axis=-1)
    sin = jnp.concatenate([-jnp.sin(ar), jnp.sin(ar), -jnp.sin(ac), jnp.sin(ac)], axis=-1)
    return cos, sin


FLASH_SAFE_BOUND = 40.0
FLASH_BOUND_MARGIN = 1.001


def _flash_kernel(q_ref, k_ref, v_ref, o_ref, acc_scr, off_scr, lsum_scr, m_scr, l_scr, kn_scr,
                  *, tq, tk, tkf, seq_len, group):
    scale = HEAD_DIM ** -0.5
    c = scale * math.log2(math.e)

    @pl.when(pl.program_id(2) == 0)
    def _():
        def norm_step(j, mx):
            kt = k_ref[pl.ds(pl.multiple_of(j * tk, tk), tk), :].astype(F32)
            row = jnp.sum(kt * kt, axis=1, keepdims=True)
            return jnp.maximum(mx, jnp.max(row, axis=0, keepdims=True))

        kmax2 = lax.fori_loop(0, seq_len // tk, norm_step, jnp.zeros((1, 1), F32))
        kn_scr[...] = jnp.broadcast_to(kmax2, kn_scr.shape)

    rows = group * tq
    q = q_ref[...].reshape(rows, HEAD_DIM)
    qf = q.astype(F32)
    b = jnp.sqrt(jnp.sum(qf * qf, axis=1, keepdims=True) * kn_scr[0:1, 0:1]) * FLASH_BOUND_MARGIN
    off_scr[...] = jnp.broadcast_to(b * c, (rows, HEAD_DIM))
    fast = (jnp.max(b, axis=0, keepdims=True) * scale)[0, 0] <= FLASH_SAFE_BOUND

    @pl.when(fast)
    def _():
        acc_scr[...] = jnp.zeros_like(acc_scr)
        lsum_scr[...] = jnp.zeros_like(lsum_scr)
        n_lane_tiles = tkf // HEAD_DIM

        def body(j, carry):
            start = pl.multiple_of(j * tkf, tkf)
            kt = k_ref[pl.ds(start, tkf), :]
            vt = v_ref[pl.ds(start, tkf), :]
            s = lax.dot_general(q, kt, (((1,), (1,)), ((), ())), preferred_element_type=F32)
            p = jnp.exp2(s * c - jnp.concatenate([off_scr[...]] * n_lane_tiles, axis=1))
            part = p[:, 0:HEAD_DIM]
            for t in range(1, n_lane_tiles):
                part = part + p[:, t * HEAD_DIM:(t + 1) * HEAD_DIM]
            lsum_scr[...] += part
            acc_scr[...] += jnp.dot(p.astype(BF16), vt, preferred_element_type=F32)
            return carry

        lax.fori_loop(0, seq_len // tkf, body, 0)
        l = jnp.sum(lsum_scr[...], axis=1, keepdims=True)
        o_ref[...] = (acc_scr[...] / l).reshape(o_ref.shape).astype(o_ref.dtype)

    @pl.when(jnp.logical_not(fast))
    def _():
        m_scr[...] = jnp.full(m_scr.shape, -jnp.inf, F32)
        l_scr[...] = jnp.zeros_like(l_scr)
        acc_scr[...] = jnp.zeros_like(acc_scr)

        def body(j, carry):
            start = pl.multiple_of(j * tk, tk)
            kt = k_ref[pl.ds(start, tk), :]
            vt = v_ref[pl.ds(start, tk), :]
            s = lax.dot_general(q, kt, (((1,), (1,)), ((), ())), preferred_element_type=F32)
            m = m_scr[...]
            m_new = jnp.maximum(m, jnp.max(s, axis=1, keepdims=True))
            alpha = jnp.exp2((m - m_new) * c)
            p = jnp.exp2(s * c - m_new * c)
            l_scr[...] = alpha * l_scr[...] + jnp.sum(p, axis=1, keepdims=True)
            acc_scr[...] = alpha * acc_scr[...] + jnp.dot(p.astype(BF16), vt, preferred_element_type=F32)
            m_scr[...] = m_new
            return carry

        lax.fori_loop(0, seq_len // tk, body, 0)
        o_ref[...] = (acc_scr[...] / l_scr[...]).reshape(o_ref.shape).astype(o_ref.dtype)


def _flash_gqa(q, k, v, batch, seq_len, tq=256, tk=512, tkf=512):
    tq = min(tq, seq_len)
    tk = min(tk, seq_len)
    tkf = min(tkf, seq_len)
    nq = seq_len // tq
    rows = GA_GROUP * tq
    wide = pltpu.VMEM((rows, HEAD_DIM), F32)
    thin = pltpu.VMEM((rows, 1), F32)
    q_spec = pl.BlockSpec((GA_GROUP, tq, HEAD_DIM), lambda b, kv, i: (kv, b * nq + i, 0))
    return pl.pallas_call(
        functools.partial(_flash_kernel, tq=tq, tk=tk, tkf=tkf, seq_len=seq_len, group=GA_GROUP),
        out_shape=jax.ShapeDtypeStruct(q.shape, BF16),
        grid=(batch, GA_KV_HEADS, nq),
        in_specs=[q_spec,
                  pl.BlockSpec((seq_len, HEAD_DIM), lambda b, kv, i: (b, kv)),
                  pl.BlockSpec((seq_len, HEAD_DIM), lambda b, kv, i: (b, kv))],
        out_specs=q_spec,
        scratch_shapes=[wide, wide, wide, thin, thin, pltpu.VMEM((HALO_ROWS, HEAD_DIM), F32)],
        compiler_params=_params("parallel", "parallel", "arbitrary"),
    )(q, k, v)


def _heads_to_lanes_prologue(o):
    return jnp.concatenate([o[h] for h in range(o.shape[0])], axis=1)


def _gqa_layer(x, batch, seq_len, g, p):
    cos, sin = _axial_tables(seq_len)
    q, k, v = _qkv_rope(x, g, p['w_qkv'], p['q_gain'], p['k_gain'], cos, sin,
                        GA_HEADS, GA_KV_HEADS, GA_KV_HEADS, HEAD_DIM // 4, seq_len)
    o = _flash_gqa(q, k, v, batch, seq_len)
    return _mm_res(_heads_to_lanes_prologue, [o], [], p['w_o'], None, x)


def _ml_in_kernel(x_ref, xp_ref, xn_ref, g_ref, w_ref, cw_ref, cb_ref, wq_ref, wk_ref, wv_ref,
                  wg_ref, bg_ref, q_ref, k_ref, v_ref, xc_ref, sz_ref, gate_ref,
                  *, seq_len, tm, inner, k_scale):
    keep_prev, keep_next = _edge_scales(pl.program_id(0), tm, seq_len)
    g = g_ref[...]
    xb = _rms(x_ref[...], g).astype(BF16)
    hb = _rms(jnp.concatenate([xp_ref[...], xn_ref[...]], axis=0), g).astype(BF16)
    rows = lax.broadcasted_iota(jnp.int32, (tm, 1), 0)
    gacc = jnp.zeros(gate_ref.shape, F32)
    for t in range(inner // MXU_DIM):
        sl = slice(t * MXU_DIM, (t + 1) * MXU_DIM)
        w = w_ref[:, sl]
        xm = jnp.dot(xb, w, preferred_element_type=F32)
        xh = jnp.dot(hb, w, preferred_element_type=F32)
        pprev = xh[HALO_ROWS - 1:HALO_ROWS, :] * keep_prev
        pnext = xh[HALO_ROWS:HALO_ROWS + 1, :] * keep_next
        xc = _conv3_rows(xm, pprev, pnext, cw_ref[:, sl], cb_ref[:, sl], rows, tm)
        xc = xc * jax.nn.sigmoid(xc)
        z = jnp.dot(xb, w_ref[:, inner + t * MXU_DIM:inner + (t + 1) * MXU_DIM],
                    preferred_element_type=F32)
        xcb = xc.astype(BF16)
        q = jnp.dot(xcb, wq_ref[t], preferred_element_type=F32)
        k = jnp.dot(xcb, wk_ref[t], preferred_element_type=F32)
        v = jnp.dot(xm.astype(BF16), wv_ref[t], preferred_element_type=F32)
        qb, kb, vb = q.astype(BF16), k.astype(BF16), v.astype(BF16)
        gacc += (jnp.dot(qb, wg_ref[0, sl, :], preferred_element_type=F32)
                 + jnp.dot(kb, wg_ref[1, sl, :], preferred_element_type=F32)
                 + jnp.dot(vb, wg_ref[2, sl, :], preferred_element_type=F32))
        q_ref[:, sl] = qb
        k_ref[:, sl] = (k * k_scale).astype(BF16)
        v_ref[:, sl] = vb
        xc_ref[:, sl] = xc
        sz_ref[:, sl] = z * jax.nn.sigmoid(z)
    gate_ref[...] = gacc + bg_ref[...]


def _block_diag_tiles(w):
    nb, c, _ = w.shape
    per = MXU_DIM // c
    wt = w.reshape(nb // per, per, c, c)
    eye = jnp.eye(per, dtype=w.dtype)
    full = jnp.einsum('tpcd,pq->tpcqd', wt, eye)
    return full.reshape(nb // per, MXU_DIM, MXU_DIM)


def _ml_in(x, g, p, seq_len, tm=256):
    n, d = x.shape
    tm = min(tm, seq_len)
    inner = p['w_up'].shape[1] // 2
    ng = 4 * ML_HEADS
    dh = inner // ML_HEADS
    wq, wk, wv = (_block_diag_tiles(p[nm]).astype(BF16) for nm in ('w_q', 'w_k', 'w_v'))
    wg = jnp.transpose(p['w_gate'], (1, 2, 0, 3)).reshape(3, inner, ng).astype(BF16)
    bg = p['b_gate'].reshape(1, ng)
    prev, nxt = _halo_specs(tm, d, n)
    row = pl.BlockSpec((tm, d), lambda i: (i, 0))
    wide = pl.BlockSpec((tm, inner), lambda i: (i, 0))
    nt = inner // MXU_DIM
    return pl.pallas_call(
        functools.partial(_ml_in_kernel, seq_len=seq_len, tm=tm, inner=inner, k_scale=dh ** -0.5),
        out_shape=(jax.ShapeDtypeStruct((n, inner), BF16),) * 3
        + (jax.ShapeDtypeStruct((n, inner), F32),) * 2
        + (jax.ShapeDtypeStruct((n, ng), F32),),
        grid=(n // tm,),
        in_specs=[row, prev, nxt, _const_spec((1, d)), _const_spec((d, 2 * inner)),
                  _const_spec((3, inner)), _const_spec((1, inner)),
                  _const_spec((nt, MXU_DIM, MXU_DIM)), _const_spec((nt, MXU_DIM, MXU_DIM)),
                  _const_spec((nt, MXU_DIM, MXU_DIM)), _const_spec((3, inner, ng)),
                  _const_spec((1, ng))],
        out_specs=(wide,) * 5 + (pl.BlockSpec((tm, ng), lambda i: (i, 0)),),
        compiler_params=_params("parallel"),
    )(x, x, x, g.reshape(1, d), p['w_up'].astype(BF16), p['conv_w'],
      p['conv_b'].reshape(1, inner), wq, wk, wv, wg, bg)


def _log_sigmoid(x):
    return jnp.minimum(x, 0.0) - jnp.log1p(jnp.exp(-jnp.abs(x)))


def _mlstm_chunk_kernel(q_ref, k_ref, v_ref, gc_ref, gr_ref, h_ref, c_scr, n_scr, m_scr, *, lc):
    d = pl.program_id(2)

    @pl.when(pl.program_id(3) == 0)
    def _():
        c_scr[...] = jnp.zeros_like(c_scr)
        n_scr[...] = jnp.zeros_like(n_scr)
        m_scr[...] = jnp.zeros_like(m_scr)

    q = q_ref[...]
    k = k_ref[...]
    v = v_ref[...]
    gc = gc_ref[...]
    gr = gr_ref[...]
    i_col, g_col = gc[:, 0:1], gc[:, 1:2]
    i_row, g_row = gr[0:1, :], gr[1:2, :]

    jr = lax.broadcasted_iota(jnp.int32, (lc, lc), 0)
    sc = lax.broadcasted_iota(jnp.int32, (lc, lc), 1)
    seen = ((sc - jr) * (1 - 2 * d)) <= 0
    g_tot = jnp.where(d == 0, g_row[:, lc - 1:lc], g_row[:, 0:1])
    m_old = m_scr[0:1, 0:1]

    dmat = jnp.where(seen, g_col - g_row + i_row, -jnp.inf)
    inter = g_col + m_old
    m_q = jnp.maximum(inter, jnp.max(dmat, axis=1, keepdims=True))
    s_qk = lax.dot_general(q, k, (((1,), (1,)), ((), ())), preferred_element_type=F32)
    a = s_qk * jnp.exp(dmat - m_q)
    w_int = jnp.exp(inter - m_q)
    q_c = jnp.dot(q, c_scr[...].astype(BF16), preferred_element_type=F32)
    num = jnp.dot(a.astype(BF16), v, preferred_element_type=F32) + q_c * w_int
    q_n = jnp.sum(q.astype(F32) * n_scr[...], axis=1, keepdims=True)
    den = jnp.sum(a, axis=1, keepdims=True) + w_int * q_n
    den = jnp.maximum(jnp.abs(den), jnp.exp(-m_q))
    h_ref[...] = num / den

    a_row = g_tot - g_row + i_row
    m_new = jnp.maximum(g_tot + m_old, jnp.max(a_row, axis=1, keepdims=True))
    ws_col = jnp.exp(g_tot - g_col + i_col - m_new)
    dec = jnp.exp(g_tot + m_old - m_new)
    kw = k.astype(F32) * ws_col
    upd = lax.dot_general(kw.astype(BF16), v, (((0,), (0,)), ((), ())), preferred_element_type=F32)
    c_scr[...] = dec * c_scr[...] + upd
    n_scr[...] = dec * n_scr[...] + jnp.sum(kw, axis=0, keepdims=True)
    m_scr[...] = jnp.broadcast_to(m_new, m_scr.shape)


def _ml_gate_kernel(g_ref, o_ref, *, lc, nh):
    g = g_ref[...]
    jr = lax.broadcasted_iota(jnp.int32, (lc, lc), 0)
    sc = lax.broadcasted_iota(jnp.int32, (lc, lc), 1)
    tril = jnp.where(sc <= jr, 1.0, 0.0).astype(BF16)
    triu = jnp.where(sc >= jr, 1.0, 0.0).astype(BF16)
    col = lax.broadcasted_iota(jnp.int32, (1, 4 * nh), 1)
    lf = _log_sigmoid(g)
    x1 = lf.astype(BF16)
    r1 = lf - x1.astype(F32)
    x2 = r1.astype(BF16)
    x3 = (r1 - x2.astype(F32)).astype(BF16)
    cum = lambda m: (jnp.dot(m, x1, preferred_element_type=F32) + jnp.dot(m, x2, preferred_element_type=F32)
                     + jnp.dot(m, x3, preferred_element_type=F32))
    gsum = jnp.where(col < 2 * nh, cum(tril), cum(triu))
    o_ref[...] = jnp.where((col % (2 * nh)) >= nh, gsum, g)


def _mlstm_chunks(q, k, v, gates, batch, seq_len):
    n, inner = q.shape
    dh = inner // ML_HEADS
    lc = min(ML_CHUNK, seq_len)
    nc = seq_len // lc
    ng = gates.shape[1]
    gates = pl.pallas_call(
        functools.partial(_ml_gate_kernel, lc=lc, nh=ML_HEADS),
        out_shape=jax.ShapeDtypeStruct((n, ng), F32),
        grid=(n // lc,),
        in_specs=[pl.BlockSpec((lc, ng), lambda i: (i, 0))],
        out_specs=pl.BlockSpec((lc, ng), lambda i: (i, 0)),
        compiler_params=_params("parallel"),
    )(gates)
    g4 = gates.reshape(n, 2, 2, ML_HEADS)
    gcol = jnp.transpose(g4, (1, 3, 0, 2))
    grow = jnp.transpose(g4, (1, 3, 2, 0))

    def chunk(b, c, dd):
        return b * nc + c + dd * (nc - 1 - 2 * c)

    qkv_spec = pl.BlockSpec((lc, dh), lambda b, h, dd, c: (chunk(b, c, dd), h))
    return pl.pallas_call(
        functools.partial(_mlstm_chunk_kernel, lc=lc),
        out_shape=jax.ShapeDtypeStruct((2, n, inner), F32),
        grid=(batch, ML_HEADS, 2, nc),
        in_specs=[qkv_spec, qkv_spec, qkv_spec,
                  pl.BlockSpec((None, None, lc, 2), lambda b, h, dd, c: (dd, h, chunk(b, c, dd), 0)),
                  pl.BlockSpec((None, None, 2, lc), lambda b, h, dd, c: (dd, h, 0, chunk(b, c, dd)))],
        out_specs=pl.BlockSpec((None, lc, dh), lambda b, h, dd, c: (dd, chunk(b, c, dd), h)),
        scratch_shapes=[pltpu.VMEM((dh, dh), F32), pltpu.VMEM((1, dh), F32),
                        pltpu.VMEM((HALO_ROWS, HEAD_DIM), F32)],
        compiler_params=_params("parallel", "parallel", "parallel", "arbitrary"),
    )(q, k, v, gcol, grow)


def _ml_out_prologue(hf, hb, xc, sz, gain, skip):
    h = hf + hb
    dh = h.shape[1] // ML_HEADS
    parts = []
    for i in range(ML_HEADS):
        sl = slice(i * dh, (i + 1) * dh)
        parts.append(_rms(h[:, sl], gain[:, sl]))
    hn = jnp.concatenate(parts, axis=1)
    return (hn + skip * xc) * sz


def _mlstm_layer(x, batch, seq_len, g, p):
    q, k, v, xc, sz, gates = _ml_in(x, g, p, seq_len)
    inner = q.shape[1]
    hs = _mlstm_chunks(q, k, v, gates, batch, seq_len)
    return _mm_res(_ml_out_prologue, [hs[0], hs[1], xc, sz],
                   [p['norm_gain'].reshape(1, inner), p['skip'].reshape(1, inner)],
                   p['w_down'], None, x, tm=256)


def _rope_tables(L):
    inv = ROPE_THETA ** (-(2.0 * jnp.arange(ROPE_DIMS // 2, dtype=F32)) / ROPE_DIMS)
    ang = jnp.arange(L, dtype=F32)[:, None] * inv[None]
    pad = HEAD_DIM - ROPE_DIMS
    cos = jnp.concatenate([jnp.cos(ang), jnp.cos(ang), jnp.ones((L, pad), F32)], axis=-1)
    sin = jnp.concatenate([-jnp.sin(ang), jnp.sin(ang), jnp.zeros((L, pad), F32)], axis=-1)
    return cos, sin


def _band_kernel(q_ref, kp_ref, kc_ref, kn_ref, vp_ref, vc_ref, vn_ref, o_ref, lse_ref,
                 *, s_len, heads):
    i = pl.program_id(2)
    qb = BAND_BLOCK
    w = qb + 2 * BAND_HALF
    a = lax.broadcasted_iota(jnp.int32, (qb, w), 0)
    c = lax.broadcasted_iota(jnp.int32, (qb, w), 1)
    rel = c - BAND_HALF - a
    key_pos = i * qb - BAND_HALF + c
    valid = (jnp.abs(rel) <= BAND_HALF) & (key_pos >= 0) & (key_pos < s_len)
    scale = HEAD_DIM ** -0.5
    for h in range(heads):
        sl = slice(h * HEAD_DIM, (h + 1) * HEAD_DIM)
        kw = jnp.concatenate([kp_ref[qb - BAND_HALF:, sl], kc_ref[:, sl], kn_ref[:BAND_HALF, sl]], axis=0)
        vw = jnp.concatenate([vp_ref[qb - BAND_HALF:, sl], vc_ref[:, sl], vn_ref[:BAND_HALF, sl]], axis=0)
        s = lax.dot_general(q_ref[:, sl], kw, (((1,), (1,)), ((), ())),
                            preferred_element_type=F32) * scale
        s = jnp.where(valid, s, -jnp.inf)
        m = jnp.max(s, axis=1, keepdims=True)
        p = jnp.exp(s - m)
        l = jnp.sum(p, axis=1, keepdims=True)
        o = jnp.dot(p.astype(BF16), vw, preferred_element_type=F32)
        o_ref[:, sl] = o / l
        lse_ref[:, sl] = jnp.broadcast_to(m + jnp.log(l), (qb, HEAD_DIM))


def _da_qkv_kernel(x_ref, g_ref, w_ref, qg_ref, kg_ref, cos_ref, sin_ref, *refs, dils, tm, half):
    n_perm = sum(1 for d in dils if d > 1)
    perm_refs, out_refs = refs[:n_perm], refs[n_perm:]
    xb = _rms(x_ref[...], g_ref[...]).astype(BF16)
    cos, sin = cos_ref[...], sin_ref[...]
    gw = DA_HEADS_PER_GROUP * HEAD_DIM
    heads = DA_HEADS_PER_GROUP * len(dils)
    for kind in range(3):
        gain = (qg_ref, kg_ref, None)[kind]
        pi = 0
        for gi, dil in enumerate(dils):
            c0 = (kind * heads + gi * DA_HEADS_PER_GROUP) * HEAD_DIM
            ph = jnp.dot(xb, w_ref[:, c0:c0 + gw], preferred_element_type=F32)
            if gain is not None:
                ph = jnp.concatenate(
                    [_head_norm_rope(ph[:, h * HEAD_DIM:(h + 1) * HEAD_DIM], gain[...], cos, sin, half)
                     for h in range(DA_HEADS_PER_GROUP)], axis=1)
            val = ph.astype(BF16)
            o_ref = out_refs[kind * len(dils) + gi]
            if dil == 1:
                o_ref[...] = val
            else:
                pv = jnp.dot(perm_refs[pi][...], val, preferred_element_type=F32).astype(BF16)
                pi += 1
                rows = tm // dil
                for r in range(dil):
                    o_ref[:, r * gw:(r + 1) * gw] = pv[r * rows:(r + 1) * rows, :]


def _da_qkv(x, g, w_qkv, q_gain, k_gain, cos, sin, seq_len, tm=256):
    n, d = x.shape
    tm = min(tm, seq_len)
    pos_blocks = seq_len // tm
    dils = tuple(dil for _, dil in DA_GROUPS)
    gw = DA_HEADS_PER_GROUP * HEAD_DIM
    f = w_qkv.shape[1]
    perms = []
    for dil in dils:
        if dil > 1:
            dst = jnp.arange(tm)
            src = (dst % (tm // dil)) * dil + dst // (tm // dil)
            perms.append((src[:, None] == jnp.arange(tm)[None, :]).astype(BF16))
    row = pl.BlockSpec((tm, d), lambda i: (i, 0))
    tab = pl.BlockSpec((tm, HEAD_DIM), lambda i: (i % pos_blocks, 0))
    out_shapes = tuple(jax.ShapeDtypeStruct((n // dil, dil * gw), BF16) for _ in range(3) for dil in dils)
    out_specs = tuple(pl.BlockSpec((tm // dil, dil * gw), lambda i: (i, 0)) for _ in range(3) for dil in dils)
    outs = pl.pallas_call(
        functools.partial(_da_qkv_kernel, dils=dils, tm=tm, half=ROPE_DIMS // 2),
        out_shape=out_shapes,
        grid=(n // tm,),
        in_specs=[row, _const_spec((1, d)), _const_spec((d, f)), _const_spec((1, HEAD_DIM)),
                  _const_spec((1, HEAD_DIM)), tab, tab] + [_const_spec((tm, tm))] * len(perms),
        out_specs=out_specs,
        compiler_params=_params("parallel"),
    )(x, g.reshape(1, d), w_qkv.astype(BF16), q_gain.reshape(1, HEAD_DIM), k_gain.reshape(1, HEAD_DIM),
      cos, sin, *perms)
    ng = len(dils)
    return [(outs[gi], outs[ng + gi], outs[2 * ng + gi]) for gi in range(ng)]


def _band_attention(q, k, v, batch, seq_len, dil):
    n = batch * seq_len
    s_len = seq_len // dil
    nb = s_len // BAND_BLOCK
    hw = DA_HEADS_PER_GROUP * HEAD_DIM
    view = lambda a: a.reshape(batch, s_len, dil * hw)
    qv, kv, vv = view(q), view(k), view(v)

    def spec(shift):
        return pl.BlockSpec(
            (None, BAND_BLOCK, hw),
            lambda b, r, i: (b, jnp.clip(i + shift, 0, nb - 1), r))

    out_spec = pl.BlockSpec((None, BAND_BLOCK, hw), lambda b, r, i: (b, i, r))
    o, lse = pl.pallas_call(
        functools.partial(_band_kernel, s_len=s_len, heads=DA_HEADS_PER_GROUP),
        out_shape=(jax.ShapeDtypeStruct((batch, s_len, dil * hw), F32),) * 2,
        grid=(batch, dil, nb),
        in_specs=[spec(0), spec(-1), spec(0), spec(1), spec(-1), spec(0), spec(1)],
        out_specs=(out_spec, out_spec),
        compiler_params=_params("parallel", "parallel", "parallel"),
    )(qv, kv, kv, kv, vv, vv, vv)
    return o.reshape(n, hw), lse.reshape(n, hw)


def _da_out_prologue(o0, o1, o2, l0, l1, l2):
    m = jnp.maximum(jnp.maximum(l0, l1), l2)
    e0, e1, e2 = jnp.exp(l0 - m), jnp.exp(l1 - m), jnp.exp(l2 - m)
    return (e0 * o0 + e1 * o1 + e2 * o2) / (e0 + e1 + e2)


def _dilated_layer(x, batch, seq_len, g, p):
    cos, sin = _rope_tables(seq_len)
    qkv = _da_qkv(x, g, p['w_qkv'], p['q_gain'], p['k_gain'], cos, sin, seq_len)
    outs, lses = [], []
    for (q, k, v), (_, dil) in zip(qkv, DA_GROUPS):
        o, lse = _band_attention(q, k, v, batch, seq_len, dil)
        outs.append(o)
        lses.append(lse)
    return _mm_res(_da_out_prologue, outs + lses, [], p['w_o'], None, x)


def _router_kernel(x_ref, g_ref, wr_ref, xn_ref, aff_ref):
    xn = _rms(x_ref[...], g_ref[...])
    xn_ref[...] = xn.astype(BF16)
    logits = lax.dot_general(wr_ref[...], xn, (((1,), (1,)), ((), ())),
                             precision=HIGHEST, preferred_element_type=F32)
    m = jnp.max(logits, axis=0, keepdims=True)
    e = jnp.exp(logits - m)
    aff_ref[...] = e / jnp.sum(e, axis=0, keepdims=True)


def _router(x, g, w_router, tm=512):
    n, d = x.shape
    tm = min(tm, n)
    e = w_router.shape[1]
    return pl.pallas_call(
        _router_kernel,
        out_shape=(jax.ShapeDtypeStruct((n, d), BF16), jax.ShapeDtypeStruct((e, n), F32)),
        grid=(n // tm,),
        in_specs=[pl.BlockSpec((tm, d), lambda i: (i, 0)), _const_spec((1, d)), _const_spec((e, d))],
        out_specs=(pl.BlockSpec((tm, d), lambda i: (i, 0)), pl.BlockSpec((e, tm), lambda i: (0, i))),
        compiler_params=_params("parallel"),
    )(x, g.reshape(1, d), w_router.T)


def _expert_ffn_kernel(xe_ref, gate_ref, w1_ref, w3_ref, w2_ref, ye_ref):
    xe = xe_ref[...]
    h1 = jnp.dot(xe, w1_ref[...], preferred_element_type=F32)
    h3 = jnp.dot(xe, w3_ref[...], preferred_element_type=F32)
    hid = (h1 * jax.nn.sigmoid(h1) * h3).astype(BF16)
    ye_ref[...] = (jnp.dot(hid, w2_ref[...], preferred_element_type=F32) * gate_ref[...]).astype(ye_ref.dtype)


def _expert_ffn(xe, gates, w1, w3, w2, tm=512):
    e, c, d = xe.shape
    f = w1.shape[2]
    tm = min(tm, c)
    return pl.pallas_call(
        _expert_ffn_kernel,
        out_shape=jax.ShapeDtypeStruct((e, c, d), BF16),
        grid=(e, c // tm),
        in_specs=[pl.BlockSpec((None, tm, d), lambda ei, ci: (ei, ci, 0)),
                  pl.BlockSpec((None, tm, 1), lambda ei, ci: (ei, ci, 0)),
                  pl.BlockSpec((None, d, f), lambda ei, ci: (ei, 0, 0)),
                  pl.BlockSpec((None, d, f), lambda ei, ci: (ei, 0, 0)),
                  pl.BlockSpec((None, f, d), lambda ei, ci: (ei, 0, 0))],
        out_specs=pl.BlockSpec((None, tm, d), lambda ei, ci: (ei, ci, 0)),
        compiler_params=_params("parallel", "arbitrary"),
    )(xe, gates[..., None], w1, w3, w2)


def _moe_layer(x, group_sizes, g, w_router, w1, w3, w2, split_output):
    n, d = x.shape
    xn, aff_t = _router(x, g, w_router)
    xes, gts, idxs = [], [], []
    start = 0
    for ng in group_sizes:
        cap = EC_CAPACITY * ng // N_EXPERTS
        gates, idx = lax.top_k(aff_t[:, start:start + ng], cap)
        idx = idx + start
        xes.append(xn[idx])
        gts.append(gates)
        idxs.append(idx)
        start += ng
    ye = _expert_ffn(jnp.concatenate(xes, axis=1), jnp.concatenate(gts, axis=1),
                     w1.astype(BF16), w3.astype(BF16), w2.astype(BF16))
    idx_all = jnp.concatenate(idxs, axis=1)
    return _combine(x, ye.reshape(-1, d), idx_all.reshape(-1), group_sizes if split_output else (n,))


COMBINE_TOKENS = 256
COMBINE_ROWS = 256


def _combine_kernel(tile_ref, blk_ref, live_ref, x_ref, tok_ref, ye_ref, *o_refs, split_tiles):
    w = pl.program_id(0)
    tile = tile_ref[w]
    first = jnp.logical_or(w == 0, tile != tile_ref[jnp.maximum(w - 1, 0)])
    tt = x_ref.shape[0]
    rows = lax.broadcasted_iota(jnp.int32, (tt, COMBINE_ROWS), 0)
    onehot = jnp.where(rows == tok_ref[...] - tile * tt, 1.0, 0.0).astype(BF16)
    add = jnp.dot(onehot, ye_ref[...], preferred_element_type=F32) * live_ref[w].astype(F32)
    lo_tile = 0
    for o_ref, n_tiles in zip(o_refs, split_tiles):
        mine = jnp.logical_and(tile >= lo_tile, tile < lo_tile + n_tiles)

        @pl.when(jnp.logical_and(mine, first))
        def _(o_ref=o_ref):
            o_ref[...] = x_ref[...] + add

        @pl.when(jnp.logical_and(mine, jnp.logical_not(first)))
        def _(o_ref=o_ref):
            o_ref[...] += add

        lo_tile += n_tiles


def _combine(x, ye, tok, splits):
    n, d = x.shape
    p = tok.shape[0]
    tt = min(COMBINE_TOKENS, min(splits))
    rb = COMBINE_ROWS
    assert p % rb == 0 and all(s % tt == 0 for s in splits) and sum(splits) == n
    tiles, nblk = n // tt, p // rb
    split_tiles = tuple(s // tt for s in splits)
    order = jnp.argsort(tok)
    tok_sorted = tok[order].astype(jnp.int32)
    ye_sorted = ye[order]

    edges = jnp.arange(tiles + 1, dtype=jnp.int32) * tt
    bounds = jnp.searchsorted(tok_sorted, edges, method='compare_all').astype(jnp.int32)
    lo, hi = bounds[:-1], bounds[1:]
    first_blk = jnp.minimum(lo // rb, nblk - 1)
    last_blk = jnp.where(hi > lo, (hi - 1) // rb, first_blk)
    n_items = last_blk - first_blk + 1
    item_end = jnp.cumsum(n_items)
    item_start = item_end - n_items
    max_items = nblk + 2 * tiles
    w = jnp.arange(max_items, dtype=jnp.int32)
    tile = jnp.minimum(jnp.searchsorted(item_end, w, side='right', method='compare_all'),
                       tiles - 1).astype(jnp.int32)
    k = w - item_start[tile]
    live = (k < n_items[tile]).astype(jnp.int32)
    blk = jnp.minimum(first_blk[tile] + k, nblk - 1).astype(jnp.int32)

    def out_spec(lo_tile, n_tiles):
        return pl.BlockSpec(
            (tt, d), lambda i, tile_r, blk_r, live_r: (jnp.clip(tile_r[i] - lo_tile, 0, n_tiles - 1), 0))

    starts = [sum(split_tiles[:j]) for j in range(len(splits))]
    grid_spec = pltpu.PrefetchScalarGridSpec(
        num_scalar_prefetch=3,
        grid=(max_items,),
        in_specs=[pl.BlockSpec((tt, d), lambda i, tile_r, blk_r, live_r: (tile_r[i], 0)),
                  pl.BlockSpec((None, 1, rb), lambda i, tile_r, blk_r, live_r: (blk_r[i], 0, 0)),
                  pl.BlockSpec((rb, d), lambda i, tile_r, blk_r, live_r: (blk_r[i], 0))],
        out_specs=tuple(out_spec(s, t) for s, t in zip(starts, split_tiles)),
    )
    return pl.pallas_call(
        functools.partial(_combine_kernel, split_tiles=split_tiles),
        out_shape=tuple(jax.ShapeDtypeStruct((s, d), F32) for s in splits),
        grid_spec=grid_spec,
        compiler_params=_params("arbitrary"),
    )(tile, blk, live, x, tok_sorted.reshape(nblk, 1, rb), ye_sorted)


def _trunk(x, batch, seq_len, group_sizes, p):
    depth = p['norm_gain'].shape[0]
    for i in range(depth):
        mixer, j = i % 4, i // 4
        g = p['norm_gain'][i, 0]
        if mixer == 0:
            x = _hyena_layer(x, batch, seq_len, g, {k[3:]: v[j] for k, v in p.items() if k.startswith('hy_')})
        elif mixer == 1:
            x = _gqa_layer(x, batch, seq_len, g, {k[3:]: v[j] for k, v in p.items() if k.startswith('ga_')})
        elif mixer == 2:
            x = _mlstm_layer(x, batch, seq_len, g, {k[3:]: v[j] for k, v in p.items() if k.startswith('ml_')})
        else:
            x = _dilated_layer(x, batch, seq_len, g, {k[3:]: v[j] for k, v in p.items() if k.startswith('da_')})
        last = i == depth - 1
        outs = _moe_layer(x, group_sizes, p['norm_gain'][i, 1], p['moe_w_router'][i],
                          p['moe_w1'][i], p['moe_w3'][i], p['moe_w2'][i], split_output=last)
        x = outs if last else outs[0]
    return x


def kernel(x_prompt, x_sample, norm_gain, hy_w_in, hy_b_in, hy_conv_w, hy_conv_b, hy_f_w1, hy_f_b1, hy_f_w2, hy_f_b2, hy_f_w3, hy_f_b3, hy_f_freq, hy_decay, hy_skip, hy_w_out, hy_b_out, ga_w_qkv, ga_q_gain, ga_k_gain, ga_w_o, ml_w_up, ml_conv_w, ml_conv_b, ml_w_q, ml_w_k, ml_w_v, ml_w_gate, ml_b_gate, ml_norm_gain, ml_skip, ml_w_down, da_w_qkv, da_q_gain, da_k_gain, da_w_o, moe_w_router, moe_w1, moe_w3, moe_w2):
    p = dict(
        norm_gain=norm_gain,
        hy_w_in=hy_w_in, hy_b_in=hy_b_in, hy_conv_w=hy_conv_w, hy_conv_b=hy_conv_b,
        hy_f_w1=hy_f_w1, hy_f_b1=hy_f_b1, hy_f_w2=hy_f_w2, hy_f_b2=hy_f_b2,
        hy_f_w3=hy_f_w3, hy_f_b3=hy_f_b3, hy_f_freq=hy_f_freq, hy_decay=hy_decay,
        hy_skip=hy_skip, hy_w_out=hy_w_out, hy_b_out=hy_b_out,
        ga_w_qkv=ga_w_qkv, ga_q_gain=ga_q_gain, ga_k_gain=ga_k_gain, ga_w_o=ga_w_o,
        ml_w_up=ml_w_up, ml_conv_w=ml_conv_w, ml_conv_b=ml_conv_b, ml_w_q=ml_w_q,
        ml_w_k=ml_w_k, ml_w_v=ml_w_v, ml_w_gate=ml_w_gate, ml_b_gate=ml_b_gate,
        ml_norm_gain=ml_norm_gain, ml_skip=ml_skip, ml_w_down=ml_w_down,
        da_w_qkv=da_w_qkv, da_q_gain=da_q_gain, da_k_gain=da_k_gain, da_w_o=da_w_o,
        moe_w_router=moe_w_router, moe_w1=moe_w1, moe_w3=moe_w3, moe_w2=moe_w2,
    )
    bp, seq_len, d = x_prompt.shape
    bs = x_sample.shape[0]
    assert x_sample.shape[1] == seq_len
    x = jnp.concatenate([x_prompt, x_sample], axis=0).reshape((bp + bs) * seq_len, d)
    y_prompt, y_sample = _trunk(x, bp + bs, seq_len, (bp * seq_len, bs * seq_len), p)
    return (y_prompt.reshape(bp, seq_len, d), y_sample.reshape(bs, seq_len, d))
```

```python
import functools
import math

import jax
import jax.numpy as jnp
from jax import lax
from jax.experimental import pallas as pl
from jax.experimental.pallas import tpu as pltpu

F32 = jnp.float32
BF16 = jnp.bfloat16
HIGHEST = lax.Precision.HIGHEST

NORM_EPS = 1e-6
GRID_W = 64
HY_BANDS = 16
GA_HEADS = 8
GA_KV_HEADS = 2
GA_GROUP = GA_HEADS // GA_KV_HEADS
HEAD_DIM = 128
AXIAL_THETA = 10000.0
ML_HEADS = 4
ML_QKV_BLOCK = 4
DA_GROUPS = ((128, 1), (512, 4), (2048, 16))
DA_HEADS_PER_GROUP = 4
DA_HEADS = DA_HEADS_PER_GROUP * len(DA_GROUPS)
ROPE_THETA = 500000.0
ROPE_DIMS = HEAD_DIM // 4
N_EXPERTS = 16
EC_CAPACITY = 2

VMEM_LIMIT_BYTES = 52 * 1024 * 1024
HALO_ROWS = 8
MXU_DIM = 256
ML_CHUNK = 256
BAND_BLOCK = 128
BAND_HALF = 64


def _params(*sem):
    return pltpu.CompilerParams(dimension_semantics=sem, vmem_limit_bytes=VMEM_LIMIT_BYTES)


def _rms(x, g):
    ms = jnp.mean(x * x, axis=-1, keepdims=True)
    return x * lax.rsqrt(ms + NORM_EPS) * g


def _const_spec(shape):
    nd = len(shape)
    return pl.BlockSpec(shape, lambda *_: (0,) * nd)


def _conv3_rows(p, pprev, pnext, cw, cb, rows, tm):
    up = jnp.where(rows == 0, pprev, pltpu.roll(p, 1, 0))
    dn = jnp.where(rows == tm - 1, pnext, pltpu.roll(p, tm - 1, 0))
    return up * cw[0:1] + p * cw[1:2] + dn * cw[2:3] + cb


def _halo_specs(tm, d, n_rows):
    hb = tm // HALO_ROWS
    last = n_rows // HALO_ROWS - 1
    prev = pl.BlockSpec((HALO_ROWS, d), lambda i: (jnp.maximum(i * hb - 1, 0), 0))
    nxt = pl.BlockSpec((HALO_ROWS, d), lambda i: (jnp.minimum((i + 1) * hb, last), 0))
    return prev, nxt


def _edge_scales(i, tm, seq_len):
    t0 = i * tm
    keep_prev = jnp.where(t0 % seq_len == 0, 0.0, 1.0).astype(F32)
    keep_next = jnp.where((t0 + tm) % seq_len == 0, 0.0, 1.0).astype(F32)
    return keep_prev, keep_next


def _mm_res_kernel(*refs, prologue, n_row, n_const, tn):
    row_refs = refs[:n_row]
    const_refs = refs[n_row:n_row + n_const]
    w_ref, b_ref, res_ref, o_ref = refs[n_row + n_const:]
    lhs = prologue(*[r[...] for r in row_refs], *[c[...] for c in const_refs]).astype(BF16)
    for j in range(o_ref.shape[1] // tn):
        sl = slice(j * tn, (j + 1) * tn)
        o_ref[:, sl] = (res_ref[:, sl] + b_ref[:, sl]
                        + jnp.dot(lhs, w_ref[:, sl], preferred_element_type=F32))


def _mm_res(prologue, rows, consts, w, b, res, tm=512, tn=512):
    n, dout = res.shape
    tm = min(tm, n)
    k = w.shape[0]
    if b is None:
        b = jnp.zeros((1, dout), F32)
    in_specs = [pl.BlockSpec((tm, r.shape[1]), lambda i: (i, 0)) if r.ndim == 2
                else pl.BlockSpec((r.shape[0], tm, r.shape[2]), lambda i: (0, i, 0)) for r in rows]
    in_specs += [_const_spec(c.shape) for c in consts]
    in_specs += [_const_spec((k, dout)), _const_spec((1, dout)),
                 pl.BlockSpec((tm, dout), lambda i: (i, 0))]
    return pl.pallas_call(
        functools.partial(_mm_res_kernel, prologue=prologue, n_row=len(rows),
                          n_const=len(consts), tn=min(tn, dout)),
        out_shape=jax.ShapeDtypeStruct((n, dout), F32),
        grid=(n // tm,),
        in_specs=in_specs,
        out_specs=pl.BlockSpec((tm, dout), lambda i: (i, 0)),
        compiler_params=_params("parallel"),
    )(*rows, *consts, w.astype(BF16), b.reshape(1, dout).astype(F32), res)


def _hyena_in_kernel(x_ref, xp_ref, xn_ref, g_ref, w_ref, b_ref, cw_ref, cb_ref,
                     x0_ref, vx_ref, *, seq_len, tm, d, cols):
    keep_prev, keep_next = _edge_scales(pl.program_id(0), tm, seq_len)
    g = g_ref[...]
    xb = _rms(x_ref[...], g).astype(BF16)
    hb = _rms(jnp.concatenate([xp_ref[...], xn_ref[...]], axis=0), g).astype(BF16)
    rows = lax.broadcasted_iota(jnp.int32, (tm, 1), 0)

    def conv_part(c0):
        sl = slice(c0, c0 + cols)
        w = w_ref[:, sl]
        bias = b_ref[:, sl]
        p = jnp.dot(xb, w, preferred_element_type=F32) + bias
        ph = jnp.dot(hb, w, preferred_element_type=F32) + bias
        pprev = ph[HALO_ROWS - 1:HALO_ROWS, :] * keep_prev
        pnext = ph[HALO_ROWS:HALO_ROWS + 1, :] * keep_next
        return _conv3_rows(p, pprev, pnext, cw_ref[:, sl], cb_ref[:, sl], rows, tm)

    for j in range(d // cols):
        c = j * cols
        x0_ref[:, c:c + cols] = conv_part(c)
        vx_ref[:, c:c + cols] = conv_part(2 * d + c) * conv_part(d + c)


def _hyena_in(x, g, w_in, b_in, conv_w, conv_b, seq_len, tm=256, cols=512):
    n, d = x.shape
    tm = min(tm, seq_len)
    prev, nxt = _halo_specs(tm, d, n)
    row = pl.BlockSpec((tm, d), lambda i: (i, 0))
    return pl.pallas_call(
        functools.partial(_hyena_in_kernel, seq_len=seq_len, tm=tm, d=d, cols=cols),
        out_shape=(jax.ShapeDtypeStruct((n, d), F32), jax.ShapeDtypeStruct((n, d), F32)),
        grid=(n // tm,),
        in_specs=[row, prev, nxt, _const_spec((1, d)), _const_spec((d, 3 * d)),
                  _const_spec((1, 3 * d)), _const_spec((3, 3 * d)), _const_spec((1, 3 * d))],
        out_specs=(row, row),
        compiler_params=_params("parallel"),
    )(x, x, x, g.reshape(1, d), w_in.astype(BF16), b_in.reshape(1, 3 * d),
      conv_w, conv_b.reshape(1, 3 * d))


def _hyena_filter_taps(L, d, f_w1, f_b1, f_w2, f_b2, f_w3, f_b3, f_freq, decay):
    t = jnp.linspace(0.0, 1.0, L, dtype=F32)[:, None]
    w_ang = 2.0 * math.pi * jnp.arange(L, dtype=F32)[:, None] / L
    bands = jnp.linspace(1e-4, HY_BANDS - 1, HY_BANDS, dtype=F32)[None, :]
    z = jnp.concatenate([t, jnp.cos(bands * w_ang), -jnp.sin(bands * w_ang)], axis=-1)
    h = jnp.sin(f_freq[0] * (z @ f_w1 + f_b1))
    h = jnp.sin(f_freq[1] * (h @ f_w2 + f_b2))
    h = h @ f_w3 + f_b3
    h = h.reshape(L, 2, d) * jnp.exp(-t[:, :, None] * jnp.abs(decay)[None])
    k = jnp.concatenate([h[:, 0], jnp.zeros((1, d), F32), h[1:, 1][::-1]], axis=0)
    return k / jnp.sum(jnp.abs(k), axis=0, keepdims=True)


def _fft_dims(m):
    lg = m.bit_length() - 1
    p = 1 << ((lg + 1) // 2)
    return p, m // p


def _split_bf16(x):
    hi = x.astype(BF16)
    return hi, (x - hi.astype(F32)).astype(BF16)


def _mm_split(fh, fl, x, precise):
    if not precise:
        return jnp.dot(fh, x.astype(BF16), preferred_element_type=F32)
    xh, xl = _split_bf16(x)
    return (jnp.dot(fh, xh, preferred_element_type=F32) + jnp.dot(fl, xh, preferred_element_type=F32)
            + jnp.dot(fh, xl, preferred_element_type=F32))


def _cplx_as_real(cr, ci):
    top = jnp.concatenate([cr, -ci], axis=-1)
    bot = jnp.concatenate([ci, cr], axis=-1)
    return jnp.concatenate([top, bot], axis=-2)


def _unit_circle(idx, m):
    ang = (2.0 * math.pi / m) * idx.astype(F32)
    return jnp.cos(ang), jnp.sin(ang)


def _dft_consts(p, q):
    m = p * q
    k1 = jnp.arange(p, dtype=jnp.int32)
    n1 = jnp.arange(p // 2, dtype=jnp.int32)
    c, s = _unit_circle((k1[:, None] * n1[None, :]) % p, p)
    fa = _cplx_as_real(c, -s)
    c, s = _unit_circle((n1[:, None] * k1[None, :]) % p, p)
    fd = _cplx_as_real(c / m, s / m)
    k2 = jnp.arange(q, dtype=jnp.int32)
    n2 = jnp.arange(q, dtype=jnp.int32)
    idx = (n2[None, None, :] * (k2[None, :, None] * p + k1[:, None, None])) % m
    c, s = _unit_circle(idx, m)
    gb = _cplx_as_real(c, -s)
    ct, st = jnp.swapaxes(c, 1, 2), jnp.swapaxes(s, 1, 2)
    gc = _cplx_as_real(ct, st)
    return tuple(_split_bf16(a) for a in (fa, gb, gc, fd))


def _fft_a_kernel(x_ref, fh_ref, fl_ref, o_ref, *, precise):
    _, half, d = x_ref.shape
    y = _mm_split(fh_ref[...], fl_ref[...], x_ref[...].reshape(2 * half, d), precise)
    o_ref[...] = y.reshape(o_ref.shape)


def _fft_b_kernel(a_ref, gh_ref, gl_ref, o_ref):
    _, q, d = a_ref.shape
    x = _mm_split(gh_ref[...], gl_ref[...], a_ref[...].reshape(2 * q, d), True)
    o_ref[...] = x.reshape(o_ref.shape)


def _fft_bc_kernel(a_ref, gb_ref, gc_ref, k_ref, z_ref):
    _, q, d = a_ref.shape
    x = _mm_split(gb_ref[...], None, a_ref[...].reshape(2 * q, d), False)
    xr, xi = x[:q], x[q:]
    kr, ki = k_ref[0], k_ref[1]
    y = jnp.concatenate([xr * kr - xi * ki, xr * ki + xi * kr], axis=0)
    z = _mm_split(gc_ref[...], None, y, False)
    z_ref[...] = z.reshape(z_ref.shape)


def _fft_stage_a(x4, fa, p, q, d, precise):
    pairs = x4.shape[0]
    return pl.pallas_call(
        functools.partial(_fft_a_kernel, precise=precise),
        out_shape=jax.ShapeDtypeStruct((pairs, 2, p, q * d), F32),
        grid=(pairs, q),
        in_specs=[pl.BlockSpec((None, 2, p // 2, d), lambda b, j: (b, 0, 0, j)),
                  _const_spec((2 * p, p)), _const_spec((2 * p, p))],
        out_specs=pl.BlockSpec((None, 2, p, d), lambda b, j: (b, 0, 0, j)),
        compiler_params=_params("parallel", "parallel"),
    )(x4, *fa)


def _long_conv(vx, taps, batch, seq_len):
    n, d = vx.shape
    m = 2 * seq_len
    p, q = _fft_dims(m)
    assert batch % 2 == 0
    pairs = batch // 2
    fa, gb, gc, fd = _dft_consts(p, q)
    g_spec = pl.BlockSpec((None, 2 * q, 2 * q), lambda k1, b: (k1, 0, 0))
    slab = pl.BlockSpec((None, 2, q, d), lambda k1, b: (b, 0, k1, 0))

    zeros = jnp.zeros((seq_len, d), F32)
    kin = jnp.stack([taps[:seq_len], zeros, taps[seq_len:], zeros]).reshape(2, 2, p // 2, q * d)
    ka = _fft_stage_a(kin, fa, p, q, d, True).reshape(2, 2, p * q, d)
    kx = pl.pallas_call(
        _fft_b_kernel,
        out_shape=jax.ShapeDtypeStruct((2, 2, p * q, d), F32),
        grid=(p, 2),
        in_specs=[slab, g_spec, g_spec],
        out_specs=slab,
        compiler_params=_params("parallel", "parallel"),
    )(ka, *gb)
    sign = jnp.repeat(1.0 - 2.0 * (jnp.arange(p) % 2).astype(F32), q)[None, :, None]
    kspec = kx[0] + sign * kx[1]

    xa = _fft_stage_a(vx.reshape(pairs, 2, p // 2, q * d), fa, p, q, d, False).reshape(pairs, 2, p * q, d)
    z = pl.pallas_call(
        _fft_bc_kernel,
        out_shape=jax.ShapeDtypeStruct((pairs, 2, p * q, d), F32),
        grid=(p, pairs),
        in_specs=[slab, g_spec, g_spec, pl.BlockSpec((2, q, d), lambda k1, b: (0, k1, 0))],
        out_specs=slab,
        compiler_params=_params("parallel", "parallel"),
    )(xa, gb[0], gc[0], kspec)
    y = pl.pallas_call(
        functools.partial(_fft_a_kernel, precise=False),
        out_shape=jax.ShapeDtypeStruct((pairs, 2, p // 2, q * d), F32),
        grid=(pairs, q),
        in_specs=[pl.BlockSpec((None, 2, p, d), lambda b, j: (b, 0, 0, j)),
                  _const_spec((p, 2 * p)), _const_spec((p, 2 * p))],
        out_specs=pl.BlockSpec((None, 2, p // 2, d), lambda b, j: (b, 0, 0, j)),
        compiler_params=_params("parallel", "parallel"),
    )(z.reshape(pairs, 2, p, q * d), *fd)
    return y.reshape(n, d)


def _hyena_out_prologue(y, vx, x0, skip):
    return (y + vx * skip) * x0


def _hyena_layer(x, batch, seq_len, g, p):
    n, d = x.shape
    x0, vx = _hyena_in(x, g, p['w_in'], p['b_in'], p['conv_w'], p['conv_b'], seq_len)
    taps = _hyena_filter_taps(seq_len, d, p['f_w1'], p['f_b1'], p['f_w2'], p['f_b2'],
                              p['f_w3'], p['f_b3'], p['f_freq'], p['decay'])
    y = _long_conv(vx, taps, batch, seq_len)
    return _mm_res(_hyena_out_prologue, [y, vx, x0], [p['skip'].reshape(1, d)],
                   p['w_out'], p['b_out'], x)


def _head_norm_rope(xh, gain, cos, sin, half):
    y = _rms(xh, gain)
    lane = lax.broadcasted_iota(jnp.int32, (1, HEAD_DIM), 1)
    fwd = pltpu.roll(y, HEAD_DIM - half, 1)
    bwd = pltpu.roll(y, half, 1)
    partner = jnp.where((lane % (2 * half)) < half, fwd, bwd)
    return y * cos + partner * sin


def _qkv_rope_kernel(x_ref, g_ref, w_ref, qg_ref, kg_ref, cos_ref, sin_ref,
                     q_ref, k_ref, v_ref, *, nq, nk, nv, half):
    xb = _rms(x_ref[...], g_ref[...]).astype(BF16)
    cos = cos_ref[...]
    sin = sin_ref[...]
    per = MXU_DIM // HEAD_DIM
    for h0 in range(0, nq + nk + nv, per):
        pw = jnp.dot(xb, w_ref[:, h0 * HEAD_DIM:(h0 + per) * HEAD_DIM], preferred_element_type=F32)
        for h in range(h0, h0 + per):
            ph = pw[:, (h - h0) * HEAD_DIM:(h - h0 + 1) * HEAD_DIM]
            if h < nq:
                sl = slice(h * HEAD_DIM, (h + 1) * HEAD_DIM)
                q_ref[:, sl] = _head_norm_rope(ph, qg_ref[...], cos, sin, half).astype(BF16)
            elif h < nq + nk:
                sl = slice((h - nq) * HEAD_DIM, (h - nq + 1) * HEAD_DIM)
                k_ref[:, sl] = _head_norm_rope(ph, kg_ref[...], cos, sin, half).astype(BF16)
            else:
                sl = slice((h - nq - nk) * HEAD_DIM, (h - nq - nk + 1) * HEAD_DIM)
                v_ref[:, sl] = ph.astype(BF16)


def _qkv_rope(x, g, w_qkv, q_gain, k_gain, cos, sin, nq, nk, nv, half, seq_len, tm=512):
    n, d = x.shape
    tm = min(tm, seq_len)
    pos_blocks = seq_len // tm
    f = w_qkv.shape[1]
    row = pl.BlockSpec((tm, d), lambda i: (i, 0))
    tab = pl.BlockSpec((tm, HEAD_DIM), lambda i: (i % pos_blocks, 0))
    outs = tuple(jax.ShapeDtypeStruct((n, c * HEAD_DIM), BF16) for c in (nq, nk, nv))
    return pl.pallas_call(
        functools.partial(_qkv_rope_kernel, nq=nq, nk=nk, nv=nv, half=half),
        out_shape=outs,
        grid=(n // tm,),
        in_specs=[row, _const_spec((1, d)), _const_spec((d, f)),
                  _const_spec((1, HEAD_DIM)), _const_spec((1, HEAD_DIM)), tab, tab],
        out_specs=tuple(pl.BlockSpec((tm, c * HEAD_DIM), lambda i: (i, 0)) for c in (nq, nk, nv)),
        compiler_params=_params("parallel"),
    )(x, g.reshape(1, d), w_qkv.astype(BF16), q_gain.reshape(1, HEAD_DIM),
      k_gain.reshape(1, HEAD_DIM), cos, sin)


def _axial_tables(L):
    t = jnp.arange(L)
    r = (t // GRID_W).astype(F32)
    c = (t % GRID_W).astype(F32)
    nf = HEAD_DIM // 4
    inv = AXIAL_THETA ** (-(2.0 * jnp.arange(nf, dtype=F32)) / (2 * nf))
    ar, ac = r[:, None] * inv[None], c[:, None] * inv[None]
    cos = jnp.concatenate([jnp.cos(ar), jnp.cos(ar), jnp.cos(ac), jnp.cos(ac)], axis=-1)
    sin = jnp.concatenate([-jnp.sin(ar), jnp.sin(ar), -jnp.sin(ac), jnp.sin(ac)], axis=-1)
    return cos, sin


FLASH_SAFE_BOUND = 40.0
FLASH_BOUND_MARGIN = 1.001


def _flash_kernel(q_ref, k_ref, v_ref, o_ref, acc_scr, off_scr, lsum_scr, m_scr, l_scr, kn_scr,
                  *, tq, tk, tkf, seq_len, group):
    scale = HEAD_DIM ** -0.5
    c = scale * math.log2(math.e)

    @pl.when(pl.program_id(2) == 0)
    def _():
        def norm_step(j, mx):
            kt = k_ref[pl.ds(pl.multiple_of(j * tk, tk), tk), :].astype(F32)
            row = jnp.sum(kt * kt, axis=1, keepdims=True)
            return jnp.maximum(mx, jnp.max(row, axis=0, keepdims=True))

        kmax2 = lax.fori_loop(0, seq_len // tk, norm_step, jnp.zeros((1, 1), F32))
        kn_scr[...] = jnp.broadcast_to(kmax2, kn_scr.shape)

    kmax2 = kn_scr[0:1, 0:1]
    bmax = jnp.zeros((1, 1), F32)
    for h in range(group):
        qf = q_ref[:, h * HEAD_DIM:(h + 1) * HEAD_DIM].astype(F32)
        b = jnp.sqrt(jnp.sum(qf * qf, axis=1, keepdims=True) * kmax2) * FLASH_BOUND_MARGIN
        bmax = jnp.maximum(bmax, jnp.max(b, axis=0, keepdims=True))
        off_scr[h] = jnp.broadcast_to(b * c, (tq, HEAD_DIM))
    fast = (bmax * scale)[0, 0] <= FLASH_SAFE_BOUND

    @pl.when(fast)
    def _():
        acc_scr[...] = jnp.zeros_like(acc_scr)
        lsum_scr[...] = jnp.zeros_like(lsum_scr)
        n_lane_tiles = tkf // HEAD_DIM

        def body(j, carry):
            start = pl.multiple_of(j * tkf, tkf)
            kt = k_ref[pl.ds(start, tkf), :]
            vt = v_ref[pl.ds(start, tkf), :]
            for h in range(group):
                q = q_ref[:, h * HEAD_DIM:(h + 1) * HEAD_DIM]
                s = lax.dot_general(q, kt, (((1,), (1,)), ((), ())), preferred_element_type=F32)
                off = off_scr[h]
                p = jnp.exp2(s * c - jnp.concatenate([off] * n_lane_tiles, axis=1))
                part = p[:, 0:HEAD_DIM]
                for t in range(1, n_lane_tiles):
                    part = part + p[:, t * HEAD_DIM:(t + 1) * HEAD_DIM]
                lsum_scr[h] += part
                acc_scr[h] += jnp.dot(p.astype(BF16), vt, preferred_element_type=F32)
            return carry

        lax.fori_loop(0, seq_len // tkf, body, 0)
        for h in range(group):
            l = jnp.sum(lsum_scr[h], axis=1, keepdims=True)
            o_ref[:, h * HEAD_DIM:(h + 1) * HEAD_DIM] = (acc_scr[h] / l).astype(o_ref.dtype)

    @pl.when(jnp.logical_not(fast))
    def _():
        m_scr[...] = jnp.full(m_scr.shape, -jnp.inf, F32)
        l_scr[...] = jnp.zeros_like(l_scr)
        acc_scr[...] = jnp.zeros_like(acc_scr)

        def body(j, carry):
            start = pl.multiple_of(j * tk, tk)
            kt = k_ref[pl.ds(start, tk), :]
            vt = v_ref[pl.ds(start, tk), :]
            for h in range(group):
                q = q_ref[:, h * HEAD_DIM:(h + 1) * HEAD_DIM]
                s = lax.dot_general(q, kt, (((1,), (1,)), ((), ())), preferred_element_type=F32)
                m = m_scr[h]
                m_new = jnp.maximum(m, jnp.max(s, axis=1, keepdims=True))
                alpha = jnp.exp2((m - m_new) * c)
                p = jnp.exp2(s * c - m_new * c)
                l_scr[h] = alpha * l_scr[h] + jnp.sum(p, axis=1, keepdims=True)
                acc_scr[h] = alpha * acc_scr[h] + jnp.dot(p.astype(BF16), vt, preferred_element_type=F32)
                m_scr[h] = m_new
            return carry

        lax.fori_loop(0, seq_len // tk, body, 0)
        for h in range(group):
            o_ref[:, h * HEAD_DIM:(h + 1) * HEAD_DIM] = (acc_scr[h] / l_scr[h]).astype(o_ref.dtype)


def _flash_gqa(q, k, v, batch, seq_len, tq=512, tk=512, tkf=1024):
    tq = min(tq, seq_len)
    tk = min(tk, seq_len)
    tkf = min(tkf, seq_len)
    nq = seq_len // tq
    gw = GA_GROUP * HEAD_DIM
    wide = pltpu.VMEM((GA_GROUP, tq, HEAD_DIM), F32)
    thin = pltpu.VMEM((GA_GROUP, tq, 1), F32)
    return pl.pallas_call(
        functools.partial(_flash_kernel, tq=tq, tk=tk, tkf=tkf, seq_len=seq_len, group=GA_GROUP),
        out_shape=jax.ShapeDtypeStruct(q.shape, BF16),
        grid=(batch, GA_KV_HEADS, nq),
        in_specs=[pl.BlockSpec((tq, gw), lambda b, kv, i: (b * nq + i, kv)),
                  pl.BlockSpec((seq_len, HEAD_DIM), lambda b, kv, i: (b, kv)),
                  pl.BlockSpec((seq_len, HEAD_DIM), lambda b, kv, i: (b, kv))],
        out_specs=pl.BlockSpec((tq, gw), lambda b, kv, i: (b * nq + i, kv)),
        scratch_shapes=[wide, wide, wide, thin, thin, pltpu.VMEM((HALO_ROWS, HEAD_DIM), F32)],
        compiler_params=_params("parallel", "parallel", "arbitrary"),
    )(q, k, v)


def _identity_prologue(o):
    return o


def _gqa_layer(x, batch, seq_len, g, p):
    cos, sin = _axial_tables(seq_len)
    q, k, v = _qkv_rope(x, g, p['w_qkv'], p['q_gain'], p['k_gain'], cos, sin,
                        GA_HEADS, GA_KV_HEADS, GA_KV_HEADS, HEAD_DIM // 4, seq_len)
    o = _flash_gqa(q, k, v, batch, seq_len)
    return _mm_res(_identity_prologue, [o], [], p['w_o'], None, x)


def _ml_in_kernel(x_ref, xp_ref, xn_ref, g_ref, w_ref, cw_ref, cb_ref, wq_ref, wk_ref, wv_ref,
                  wg_ref, bg_ref, q_ref, k_ref, v_ref, xc_ref, sz_ref, gate_ref,
                  *, seq_len, tm, inner, k_scale):
    keep_prev, keep_next = _edge_scales(pl.program_id(0), tm, seq_len)
    g = g_ref[...]
    xb = _rms(x_ref[...], g).astype(BF16)
    hb = _rms(jnp.concatenate([xp_ref[...], xn_ref[...]], axis=0), g).astype(BF16)
    rows = lax.broadcasted_iota(jnp.int32, (tm, 1), 0)
    gacc = jnp.zeros(gate_ref.shape, F32)
    for t in range(inner // MXU_DIM):
        sl = slice(t * MXU_DIM, (t + 1) * MXU_DIM)
        w = w_ref[:, sl]
        xm = jnp.dot(xb, w, preferred_element_type=F32)
        xh = jnp.dot(hb, w, preferred_element_type=F32)
        pprev = xh[HALO_ROWS - 1:HALO_ROWS, :] * keep_prev
        pnext = xh[HALO_ROWS:HALO_ROWS + 1, :] * keep_next
        xc = _conv3_rows(xm, pprev, pnext, cw_ref[:, sl], cb_ref[:, sl], rows, tm)
        xc = xc * jax.nn.sigmoid(xc)
        z = jnp.dot(xb, w_ref[:, inner + t * MXU_DIM:inner + (t + 1) * MXU_DIM],
                    preferred_element_type=F32)
        xcb = xc.astype(BF16)
        q = jnp.dot(xcb, wq_ref[t], preferred_element_type=F32)
        k = jnp.dot(xcb, wk_ref[t], preferred_element_type=F32)
        v = jnp.dot(xm.astype(BF16), wv_ref[t], preferred_element_type=F32)
        qb, kb, vb = q.astype(BF16), k.astype(BF16), v.astype(BF16)
        gacc += (jnp.dot(qb, wg_ref[0, sl, :], preferred_element_type=F32)
                 + jnp.dot(kb, wg_ref[1, sl, :], preferred_element_type=F32)
                 + jnp.dot(vb, wg_ref[2, sl, :], preferred_element_type=F32))
        q_ref[:, sl] = qb
        k_ref[:, sl] = (k * k_scale).astype(BF16)
        v_ref[:, sl] = vb
        xc_ref[:, sl] = xc
        sz_ref[:, sl] = z * jax.nn.sigmoid(z)
    gate_ref[...] = gacc + bg_ref[...]


def _block_diag_tiles(w):
    nb, c, _ = w.shape
    per = MXU_DIM // c
    wt = w.reshape(nb // per, per, c, c)
    eye = jnp.eye(per, dtype=w.dtype)
    full = jnp.einsum('tpcd,pq->tpcqd', wt, eye)
    return full.reshape(nb // per, MXU_DIM, MXU_DIM)


def _ml_in(x, g, p, seq_len, tm=256):
    n, d = x.shape
    tm = min(tm, seq_len)
    inner = p['w_up'].shape[1] // 2
    ng = 4 * ML_HEADS
    dh = inner // ML_HEADS
    wq, wk, wv = (_block_diag_tiles(p[nm]).astype(BF16) for nm in ('w_q', 'w_k', 'w_v'))
    wg = jnp.transpose(p['w_gate'], (1, 2, 0, 3)).reshape(3, inner, ng).astype(BF16)
    bg = p['b_gate'].reshape(1, ng)
    prev, nxt = _halo_specs(tm, d, n)
    row = pl.BlockSpec((tm, d), lambda i: (i, 0))
    wide = pl.BlockSpec((tm, inner), lambda i: (i, 0))
    nt = inner // MXU_DIM
    return pl.pallas_call(
        functools.partial(_ml_in_kernel, seq_len=seq_len, tm=tm, inner=inner, k_scale=dh ** -0.5),
        out_shape=(jax.ShapeDtypeStruct((n, inner), BF16),) * 3
        + (jax.ShapeDtypeStruct((n, inner), F32),) * 2
        + (jax.ShapeDtypeStruct((n, ng), F32),),
        grid=(n // tm,),
        in_specs=[row, prev, nxt, _const_spec((1, d)), _const_spec((d, 2 * inner)),
                  _const_spec((3, inner)), _const_spec((1, inner)),
                  _const_spec((nt, MXU_DIM, MXU_DIM)), _const_spec((nt, MXU_DIM, MXU_DIM)),
                  _const_spec((nt, MXU_DIM, MXU_DIM)), _const_spec((3, inner, ng)),
                  _const_spec((1, ng))],
        out_specs=(wide,) * 5 + (pl.BlockSpec((tm, ng), lambda i: (i, 0)),),
        compiler_params=_params("parallel"),
    )(x, x, x, g.reshape(1, d), p['w_up'].astype(BF16), p['conv_w'],
      p['conv_b'].reshape(1, inner), wq, wk, wv, wg, bg)


def _log_sigmoid(x):
    return jnp.minimum(x, 0.0) - jnp.log1p(jnp.exp(-jnp.abs(x)))


def _mlstm_chunk_kernel(q_ref, k_ref, v_ref, gc_ref, gr_ref, h_ref, c_scr, n_scr, m_scr, *, lc):
    d = pl.program_id(2)

    @pl.when(pl.program_id(3) == 0)
    def _():
        c_scr[...] = jnp.zeros_like(c_scr)
        n_scr[...] = jnp.zeros_like(n_scr)
        m_scr[...] = jnp.zeros_like(m_scr)

    q = q_ref[...]
    k = k_ref[...]
    v = v_ref[...]
    gc = gc_ref[...]
    gr = gr_ref[...]
    i_col, g_col = gc[:, 0:1], gc[:, 1:2]
    i_row, g_row = gr[0:1, :], gr[1:2, :]

    jr = lax.broadcasted_iota(jnp.int32, (lc, lc), 0)
    sc = lax.broadcasted_iota(jnp.int32, (lc, lc), 1)
    seen = ((sc - jr) * (1 - 2 * d)) <= 0
    g_tot = jnp.where(d == 0, g_row[:, lc - 1:lc], g_row[:, 0:1])
    m_old = m_scr[0:1, 0:1]

    dmat = jnp.where(seen, g_col - g_row + i_row, -jnp.inf)
    inter = g_col + m_old
    m_q = jnp.maximum(inter, jnp.max(dmat, axis=1, keepdims=True))
    s_qk = lax.dot_general(q, k, (((1,), (1,)), ((), ())), preferred_element_type=F32)
    a = s_qk * jnp.exp(dmat - m_q)
    w_int = jnp.exp(inter - m_q)
    q_c = jnp.dot(q, c_scr[...].astype(BF16), preferred_element_type=F32)
    num = jnp.dot(a.astype(BF16), v, preferred_element_type=F32) + q_c * w_int
    q_n = jnp.sum(q.astype(F32) * n_scr[...], axis=1, keepdims=True)
    den = jnp.sum(a, axis=1, keepdims=True) + w_int * q_n
    den = jnp.maximum(jnp.abs(den), jnp.exp(-m_q))
    h_ref[...] = num / den

    a_row = g_tot - g_row + i_row
    m_new = jnp.maximum(g_tot + m_old, jnp.max(a_row, axis=1, keepdims=True))
    ws_col = jnp.exp(g_tot - g_col + i_col - m_new)
    dec = jnp.exp(g_tot + m_old - m_new)
    kw = k.astype(F32) * ws_col
    upd = lax.dot_general(kw.astype(BF16), v, (((0,), (0,)), ((), ())), preferred_element_type=F32)
    c_scr[...] = dec * c_scr[...] + upd
    n_scr[...] = dec * n_scr[...] + jnp.sum(kw, axis=0, keepdims=True)
    m_scr[...] = jnp.broadcast_to(m_new, m_scr.shape)


def _ml_gate_kernel(g_ref, o_ref, *, lc, nh):
    g = g_ref[...]
    jr = lax.broadcasted_iota(jnp.int32, (lc, lc), 0)
    sc = lax.broadcasted_iota(jnp.int32, (lc, lc), 1)
    tril = jnp.where(sc <= jr, 1.0, 0.0).astype(BF16)
    triu = jnp.where(sc >= jr, 1.0, 0.0).astype(BF16)
    col = lax.broadcasted_iota(jnp.int32, (1, 4 * nh), 1)
    lf = _log_sigmoid(g)
    x1 = lf.astype(BF16)
    r1 = lf - x1.astype(F32)
    x2 = r1.astype(BF16)
    x3 = (r1 - x2.astype(F32)).astype(BF16)
    cum = lambda m: (jnp.dot(m, x1, preferred_element_type=F32) + jnp.dot(m, x2, preferred_element_type=F32)
                     + jnp.dot(m, x3, preferred_element_type=F32))
    gsum = jnp.where(col < 2 * nh, cum(tril), cum(triu))
    o_ref[...] = jnp.where((col % (2 * nh)) >= nh, gsum, g)


def _mlstm_chunks(q, k, v, gates, batch, seq_len):
    n, inner = q.shape
    dh = inner // ML_HEADS
    lc = min(ML_CHUNK, seq_len)
    nc = seq_len // lc
    ng = gates.shape[1]
    gates = pl.pallas_call(
        functools.partial(_ml_gate_kernel, lc=lc, nh=ML_HEADS),
        out_shape=jax.ShapeDtypeStruct((n, ng), F32),
        grid=(n // lc,),
        in_specs=[pl.BlockSpec((lc, ng), lambda i: (i, 0))],
        out_specs=pl.BlockSpec((lc, ng), lambda i: (i, 0)),
        compiler_params=_params("parallel"),
    )(gates)
    g4 = gates.reshape(n, 2, 2, ML_HEADS)
    gcol = jnp.transpose(g4, (1, 3, 0, 2))
    grow = jnp.transpose(g4, (1, 3, 2, 0))

    def chunk(b, c, dd):
        return b * nc + c + dd * (nc - 1 - 2 * c)

    qkv_spec = pl.BlockSpec((lc, dh), lambda b, h, dd, c: (chunk(b, c, dd), h))
    return pl.pallas_call(
        functools.partial(_mlstm_chunk_kernel, lc=lc),
        out_shape=jax.ShapeDtypeStruct((2, n, inner), F32),
        grid=(batch, ML_HEADS, 2, nc),
        in_specs=[qkv_spec, qkv_spec, qkv_spec,
                  pl.BlockSpec((None, None, lc, 2), lambda b, h, dd, c: (dd, h, chunk(b, c, dd), 0)),
                  pl.BlockSpec((None, None, 2, lc), lambda b, h, dd, c: (dd, h, 0, chunk(b, c, dd)))],
        out_specs=pl.BlockSpec((None, lc, dh), lambda b, h, dd, c: (dd, chunk(b, c, dd), h)),
        scratch_shapes=[pltpu.VMEM((dh, dh), F32), pltpu.VMEM((1, dh), F32),
                        pltpu.VMEM((HALO_ROWS, HEAD_DIM), F32)],
        compiler_params=_params("parallel", "parallel", "parallel", "arbitrary"),
    )(q, k, v, gcol, grow)


def _ml_out_prologue(hs, xc, sz, gain, skip):
    h = hs[0] + hs[1]
    dh = h.shape[1] // ML_HEADS
    parts = []
    for i in range(ML_HEADS):
        sl = slice(i * dh, (i + 1) * dh)
        parts.append(_rms(h[:, sl], gain[:, sl]))
    hn = jnp.concatenate(parts, axis=1)
    return (hn + skip * xc) * sz


def _mlstm_layer(x, batch, seq_len, g, p):
    q, k, v, xc, sz, gates = _ml_in(x, g, p, seq_len)
    inner = q.shape[1]
    hs = _mlstm_chunks(q, k, v, gates, batch, seq_len)
    return _mm_res(_ml_out_prologue, [hs, xc, sz],
                   [p['norm_gain'].reshape(1, inner), p['skip'].reshape(1, inner)],
                   p['w_down'], None, x, tm=256)


def _rope_tables(L):
    inv = ROPE_THETA ** (-(2.0 * jnp.arange(ROPE_DIMS // 2, dtype=F32)) / ROPE_DIMS)
    ang = jnp.arange(L, dtype=F32)[:, None] * inv[None]
    pad = HEAD_DIM - ROPE_DIMS
    cos = jnp.concatenate([jnp.cos(ang), jnp.cos(ang), jnp.ones((L, pad), F32)], axis=-1)
    sin = jnp.concatenate([-jnp.sin(ang), jnp.sin(ang), jnp.zeros((L, pad), F32)], axis=-1)
    return cos, sin


def _band_kernel(q_ref, kp_ref, kc_ref, kn_ref, vp_ref, vc_ref, vn_ref, o_ref, lse_ref,
                 *, s_len, heads):
    i = pl.program_id(2)
    qb = BAND_BLOCK
    w = qb + 2 * BAND_HALF
    a = lax.broadcasted_iota(jnp.int32, (qb, w), 0)
    c = lax.broadcasted_iota(jnp.int32, (qb, w), 1)
    rel = c - BAND_HALF - a
    key_pos = i * qb - BAND_HALF + c
    valid = (jnp.abs(rel) <= BAND_HALF) & (key_pos >= 0) & (key_pos < s_len)
    scale = HEAD_DIM ** -0.5
    for h in range(heads):
        sl = slice(h * HEAD_DIM, (h + 1) * HEAD_DIM)
        kw = jnp.concatenate([kp_ref[qb - BAND_HALF:, sl], kc_ref[:, sl], kn_ref[:BAND_HALF, sl]], axis=0)
        vw = jnp.concatenate([vp_ref[qb - BAND_HALF:, sl], vc_ref[:, sl], vn_ref[:BAND_HALF, sl]], axis=0)
        s = lax.dot_general(q_ref[:, sl], kw, (((1,), (1,)), ((), ())),
                            preferred_element_type=F32) * scale
        s = jnp.where(valid, s, -jnp.inf)
        m = jnp.max(s, axis=1, keepdims=True)
        p = jnp.exp(s - m)
        l = jnp.sum(p, axis=1, keepdims=True)
        o = jnp.dot(p.astype(BF16), vw, preferred_element_type=F32)
        o_ref[:, sl] = o / l
        lse_ref[:, sl] = jnp.broadcast_to(m + jnp.log(l), (qb, HEAD_DIM))


def _da_qkv_kernel(x_ref, g_ref, w_ref, qg_ref, kg_ref, cos_ref, sin_ref, *refs, dils, tm, half):
    n_perm = sum(1 for d in dils if d > 1)
    perm_refs, out_refs = refs[:n_perm], refs[n_perm:]
    xb = _rms(x_ref[...], g_ref[...]).astype(BF16)
    cos, sin = cos_ref[...], sin_ref[...]
    gw = DA_HEADS_PER_GROUP * HEAD_DIM
    heads = DA_HEADS_PER_GROUP * len(dils)
    for kind in range(3):
        gain = (qg_ref, kg_ref, None)[kind]
        pi = 0
        for gi, dil in enumerate(dils):
            c0 = (kind * heads + gi * DA_HEADS_PER_GROUP) * HEAD_DIM
            ph = jnp.dot(xb, w_ref[:, c0:c0 + gw], preferred_element_type=F32)
            if gain is not None:
                ph = jnp.concatenate(
                    [_head_norm_rope(ph[:, h * HEAD_DIM:(h + 1) * HEAD_DIM], gain[...], cos, sin, half)
                     for h in range(DA_HEADS_PER_GROUP)], axis=1)
            val = ph.astype(BF16)
            o_ref = out_refs[kind * len(dils) + gi]
            if dil == 1:
                o_ref[...] = val
            else:
                pv = jnp.dot(perm_refs[pi][...], val, preferred_element_type=F32).astype(BF16)
                pi += 1
                rows = tm // dil
                for r in range(dil):
                    o_ref[:, r * gw:(r + 1) * gw] = pv[r * rows:(r + 1) * rows, :]


def _da_qkv(x, g, w_qkv, q_gain, k_gain, cos, sin, seq_len, tm=256):
    n, d = x.shape
    tm = min(tm, seq_len)
    pos_blocks = seq_len // tm
    dils = tuple(dil for _, dil in DA_GROUPS)
    gw = DA_HEADS_PER_GROUP * HEAD_DIM
    f = w_qkv.shape[1]
    perms = []
    for dil in dils:
        if dil > 1:
            dst = jnp.arange(tm)
            src = (dst % (tm // dil)) * dil + dst // (tm // dil)
            perms.append((src[:, None] == jnp.arange(tm)[None, :]).astype(BF16))
    row = pl.BlockSpec((tm, d), lambda i: (i, 0))
    tab = pl.BlockSpec((tm, HEAD_DIM), lambda i: (i % pos_blocks, 0))
    out_shapes = tuple(jax.ShapeDtypeStruct((n // dil, dil * gw), BF16) for _ in range(3) for dil in dils)
    out_specs = tuple(pl.BlockSpec((tm // dil, dil * gw), lambda i: (i, 0)) for _ in range(3) for dil in dils)
    outs = pl.pallas_call(
        functools.partial(_da_qkv_kernel, dils=dils, tm=tm, half=ROPE_DIMS // 2),
        out_shape=out_shapes,
        grid=(n // tm,),
        in_specs=[row, _const_spec((1, d)), _const_spec((d, f)), _const_spec((1, HEAD_DIM)),
                  _const_spec((1, HEAD_DIM)), tab, tab] + [_const_spec((tm, tm))] * len(perms),
        out_specs=out_specs,
        compiler_params=_params("parallel"),
    )(x, g.reshape(1, d), w_qkv.astype(BF16), q_gain.reshape(1, HEAD_DIM), k_gain.reshape(1, HEAD_DIM),
      cos, sin, *perms)
    ng = len(dils)
    return [(outs[gi], outs[ng + gi], outs[2 * ng + gi]) for gi in range(ng)]


def _band_attention(q, k, v, batch, seq_len, dil):
    n = batch * seq_len
    s_len = seq_len // dil
    nb = s_len // BAND_BLOCK
    hw = DA_HEADS_PER_GROUP * HEAD_DIM
    view = lambda a: a.reshape(batch, s_len, dil * hw)
    qv, kv, vv = view(q), view(k), view(v)

    def spec(shift):
        return pl.BlockSpec(
            (None, BAND_BLOCK, hw),
            lambda b, r, i: (b, jnp.clip(i + shift, 0, nb - 1), r))

    out_spec = pl.BlockSpec((None, BAND_BLOCK, hw), lambda b, r, i: (b, i, r))
    o, lse = pl.pallas_call(
        functools.partial(_band_kernel, s_len=s_len, heads=DA_HEADS_PER_GROUP),
        out_shape=(jax.ShapeDtypeStruct((batch, s_len, dil * hw), F32),) * 2,
        grid=(batch, dil, nb),
        in_specs=[spec(0), spec(-1), spec(0), spec(1), spec(-1), spec(0), spec(1)],
        out_specs=(out_spec, out_spec),
        compiler_params=_params("parallel", "parallel", "parallel"),
    )(qv, kv, kv, kv, vv, vv, vv)
    return o.reshape(n, hw), lse.reshape(n, hw)


def _da_out_prologue(o0, o1, o2, l0, l1, l2):
    m = jnp.maximum(jnp.maximum(l0, l1), l2)
    e0, e1, e2 = jnp.exp(l0 - m), jnp.exp(l1 - m), jnp.exp(l2 - m)
    return (e0 * o0 + e1 * o1 + e2 * o2) / (e0 + e1 + e2)


def _dilated_layer(x, batch, seq_len, g, p):
    cos, sin = _rope_tables(seq_len)
    qkv = _da_qkv(x, g, p['w_qkv'], p['q_gain'], p['k_gain'], cos, sin, seq_len)
    outs, lses = [], []
    for (q, k, v), (_, dil) in zip(qkv, DA_GROUPS):
        o, lse = _band_attention(q, k, v, batch, seq_len, dil)
        outs.append(o)
        lses.append(lse)
    return _mm_res(_da_out_prologue, outs + lses, [], p['w_o'], None, x)


def _router_kernel(x_ref, g_ref, wr_ref, xn_ref, aff_ref):
    xn = _rms(x_ref[...], g_ref[...])
    xn_ref[...] = xn.astype(BF16)
    logits = lax.dot_general(wr_ref[...], xn, (((1,), (1,)), ((), ())),
                             precision=HIGHEST, preferred_element_type=F32)
    m = jnp.max(logits, axis=0, keepdims=True)
    e = jnp.exp(logits - m)
    aff_ref[...] = e / jnp.sum(e, axis=0, keepdims=True)


def _router(x, g, w_router, tm=512):
    n, d = x.shape
    tm = min(tm, n)
    e = w_router.shape[1]
    return pl.pallas_call(
        _router_kernel,
        out_shape=(jax.ShapeDtypeStruct((n, d), BF16), jax.ShapeDtypeStruct((e, n), F32)),
        grid=(n // tm,),
        in_specs=[pl.BlockSpec((tm, d), lambda i: (i, 0)), _const_spec((1, d)), _const_spec((e, d))],
        out_specs=(pl.BlockSpec((tm, d), lambda i: (i, 0)), pl.BlockSpec((e, tm), lambda i: (0, i))),
        compiler_params=_params("parallel"),
    )(x, g.reshape(1, d), w_router.T)


def _expert_ffn_kernel(xe_ref, gate_ref, w1_ref, w3_ref, w2_ref, ye_ref):
    xe = xe_ref[...]
    h1 = jnp.dot(xe, w1_ref[...], preferred_element_type=F32)
    h3 = jnp.dot(xe, w3_ref[...], preferred_element_type=F32)
    hid = (h1 * jax.nn.sigmoid(h1) * h3).astype(BF16)
    ye_ref[...] = (jnp.dot(hid, w2_ref[...], preferred_element_type=F32) * gate_ref[...]).astype(ye_ref.dtype)


def _expert_ffn(xe, gates, w1, w3, w2, tm=512):
    e, c, d = xe.shape
    f = w1.shape[2]
    tm = min(tm, c)
    return pl.pallas_call(
        _expert_ffn_kernel,
        out_shape=jax.ShapeDtypeStruct((e, c, d), BF16),
        grid=(e, c // tm),
        in_specs=[pl.BlockSpec((None, tm, d), lambda ei, ci: (ei, ci, 0)),
                  pl.BlockSpec((None, tm, 1), lambda ei, ci: (ei, ci, 0)),
                  pl.BlockSpec((None, d, f), lambda ei, ci: (ei, 0, 0)),
                  pl.BlockSpec((None, d, f), lambda ei, ci: (ei, 0, 0)),
                  pl.BlockSpec((None, f, d), lambda ei, ci: (ei, 0, 0))],
        out_specs=pl.BlockSpec((None, tm, d), lambda ei, ci: (ei, ci, 0)),
        compiler_params=_params("parallel", "arbitrary"),
    )(xe, gates[..., None], w1, w3, w2)


def _moe_layer(x, group_sizes, g, w_router, w1, w3, w2, split_output):
    n, d = x.shape
    xn, aff_t = _router(x, g, w_router)
    xes, gts, idxs = [], [], []
    start = 0
    for ng in group_sizes:
        cap = EC_CAPACITY * ng // N_EXPERTS
        gates, idx = lax.top_k(aff_t[:, start:start + ng], cap)
        idx = idx + start
        xes.append(xn[idx])
        gts.append(gates)
        idxs.append(idx)
        start += ng
    ye = _expert_ffn(jnp.concatenate(xes, axis=1), jnp.concatenate(gts, axis=1),
                     w1.astype(BF16), w3.astype(BF16), w2.astype(BF16))
    idx_all = jnp.concatenate(idxs, axis=1)
    return _combine(x, ye.reshape(-1, d), idx_all.reshape(-1), group_sizes if split_output else (n,))


COMBINE_TOKENS = 512
COMBINE_ROWS = 512


def _combine_kernel(tile_ref, blk_ref, live_ref, x_ref, tok_ref, ye_ref, *o_refs, split_tiles):
    w = pl.program_id(0)
    tile = tile_ref[w]
    first = jnp.logical_or(w == 0, tile != tile_ref[jnp.maximum(w - 1, 0)])
    tt = x_ref.shape[0]
    rows = lax.broadcasted_iota(jnp.int32, (tt, COMBINE_ROWS), 0)
    onehot = jnp.where(rows == tok_ref[...] - tile * tt, 1.0, 0.0).astype(BF16)
    add = jnp.dot(onehot, ye_ref[...], preferred_element_type=F32) * live_ref[w].astype(F32)
    lo_tile = 0
    for o_ref, n_tiles in zip(o_refs, split_tiles):
        mine = jnp.logical_and(tile >= lo_tile, tile < lo_tile + n_tiles)

        @pl.when(jnp.logical_and(mine, first))
        def _(o_ref=o_ref):
            o_ref[...] = x_ref[...] + add

        @pl.when(jnp.logical_and(mine, jnp.logical_not(first)))
        def _(o_ref=o_ref):
            o_ref[...] += add

        lo_tile += n_tiles


def _combine(x, ye, tok, splits):
    n, d = x.shape
    p = tok.shape[0]
    tt = min(COMBINE_TOKENS, min(splits))
    rb = COMBINE_ROWS
    assert p % rb == 0 and all(s % tt == 0 for s in splits) and sum(splits) == n
    tiles, nblk = n // tt, p // rb
    split_tiles = tuple(s // tt for s in splits)
    order = jnp.argsort(tok)
    tok_sorted = tok[order].astype(jnp.int32)
    ye_sorted = ye[order]

    edges = jnp.arange(tiles + 1, dtype=jnp.int32) * tt
    bounds = jnp.searchsorted(tok_sorted, edges, method='compare_all').astype(jnp.int32)
    lo, hi = bounds[:-1], bounds[1:]
    first_blk = jnp.minimum(lo // rb, nblk - 1)
    last_blk = jnp.where(hi > lo, (hi - 1) // rb, first_blk)
    n_items = last_blk - first_blk + 1
    item_end = jnp.cumsum(n_items)
    item_start = item_end - n_items
    max_items = nblk + 2 * tiles
    w = jnp.arange(max_items, dtype=jnp.int32)
    tile = jnp.minimum(jnp.searchsorted(item_end, w, side='right', method='compare_all'),
                       tiles - 1).astype(jnp.int32)
    k = w - item_start[tile]
    live = (k < n_items[tile]).astype(jnp.int32)
    blk = jnp.minimum(first_blk[tile] + k, nblk - 1).astype(jnp.int32)

    def out_spec(lo_tile, n_tiles):
        return pl.BlockSpec(
            (tt, d), lambda i, tile_r, blk_r, live_r: (jnp.clip(tile_r[i] - lo_tile, 0, n_tiles - 1), 0))

    starts = [sum(split_tiles[:j]) for j in range(len(splits))]
    grid_spec = pltpu.PrefetchScalarGridSpec(
        num_scalar_prefetch=3,
        grid=(max_items,),
        in_specs=[pl.BlockSpec((tt, d), lambda i, tile_r, blk_r, live_r: (tile_r[i], 0)),
                  pl.BlockSpec((None, 1, rb), lambda i, tile_r, blk_r, live_r: (blk_r[i], 0, 0)),
                  pl.BlockSpec((rb, d), lambda i, tile_r, blk_r, live_r: (blk_r[i], 0))],
        out_specs=tuple(out_spec(s, t) for s, t in zip(starts, split_tiles)),
    )
    return pl.pallas_call(
        functools.partial(_combine_kernel, split_tiles=split_tiles),
        out_shape=tuple(jax.ShapeDtypeStruct((s, d), F32) for s in splits),
        grid_spec=grid_spec,
        compiler_params=_params("arbitrary"),
    )(tile, blk, live, x, tok_sorted.reshape(nblk, 1, rb), ye_sorted)


def _trunk(x, batch, seq_len, group_sizes, p):
    depth = p['norm_gain'].shape[0]
    for i in range(depth):
        mixer, j = i % 4, i // 4
        g = p['norm_gain'][i, 0]
        if mixer == 0:
            x = _hyena_layer(x, batch, seq_len, g, {k[3:]: v[j] for k, v in p.items() if k.startswith('hy_')})
        elif mixer == 1:
            x = _gqa_layer(x, batch, seq_len, g, {k[3:]: v[j] for k, v in p.items() if k.startswith('ga_')})
        elif mixer == 2:
            x = _mlstm_layer(x, batch, seq_len, g, {k[3:]: v[j] for k, v in p.items() if k.startswith('ml_')})
        else:
            x = _dilated_layer(x, batch, seq_len, g, {k[3:]: v[j] for k, v in p.items() if k.startswith('da_')})
        last = i == depth - 1
        outs = _moe_layer(x, group_sizes, p['norm_gain'][i, 1], p['moe_w_router'][i],
                          p['moe_w1'][i], p['moe_w3'][i], p['moe_w2'][i], split_output=last)
        x = outs if last else outs[0]
    return x


def kernel(x_prompt, x_sample, norm_gain, hy_w_in, hy_b_in, hy_conv_w, hy_conv_b, hy_f_w1, hy_f_b1, hy_f_w2, hy_f_b2, hy_f_w3, hy_f_b3, hy_f_freq, hy_decay, hy_skip, hy_w_out, hy_b_out, ga_w_qkv, ga_q_gain, ga_k_gain, ga_w_o, ml_w_up, ml_conv_w, ml_conv_b, ml_w_q, ml_w_k, ml_w_v, ml_w_gate, ml_b_gate, ml_norm_gain, ml_skip, ml_w_down, da_w_qkv, da_q_gain, da_k_gain, da_w_o, moe_w_router, moe_w1, moe_w3, moe_w2):
    p = dict(
        norm_gain=norm_gain,
        hy_w_in=hy_w_in, hy_b_in=hy_b_in, hy_conv_w=hy_conv_w, hy_conv_b=hy_conv_b,
        hy_f_w1=hy_f_w1, hy_f_b1=hy_f_b1, hy_f_w2=hy_f_w2, hy_f_b2=hy_f_b2,
        hy_f_w3=hy_f_w3, hy_f_b3=hy_f_b3, hy_f_freq=hy_f_freq, hy_decay=hy_decay,
        hy_skip=hy_skip, hy_w_out=hy_w_out, hy_b_out=hy_b_out,
        ga_w_qkv=ga_w_qkv, ga_q_gain=ga_q_gain, ga_k_gain=ga_k_gain, ga_w_o=ga_w_o,
        ml_w_up=ml_w_up, ml_conv_w=ml_conv_w, ml_conv_b=ml_conv_b, ml_w_q=ml_w_q,
        ml_w_k=ml_w_k, ml_w_v=ml_w_v, ml_w_gate=ml_w_gate, ml_b_gate=ml_b_gate,
        ml_norm_gain=ml_norm_gain, ml_skip=ml_skip, ml_w_down=ml_w_down,
        da_w_qkv=da_w_qkv, da_q_gain=da_q_gain, da_k_gain=da_k_gain, da_w_o=da_w_o,
        moe_w_router=moe_w_router, moe_w1=moe_w1, moe_w3=moe_w3, moe_w2=moe_w2,
    )
    bp, seq_len, d = x_prompt.shape
    bs = x_sample.shape[0]
    assert x_sample.shape[1] == seq_len
    x = jnp.concatenate([x_prompt, x_sample], axis=0).reshape((bp + bs) * seq_len, d)
    y_prompt, y_sample = _trunk(x, bp + bs, seq_len, (bp * seq_len, bs * seq_len), p)
    return (y_prompt.reshape(bp, seq_len, d), y_sample.reshape(bs, seq_len, d))
```

```python
import functools
import math

import jax
import jax.numpy as jnp
from jax import lax
from jax.experimental import pallas as pl
from jax.experimental.pallas import tpu as pltpu

F32 = jnp.float32
BF16 = jnp.bfloat16
HIGHEST = lax.Precision.HIGHEST

NORM_EPS = 1e-6
GRID_W = 64
HY_BANDS = 16
GA_HEADS = 8
GA_KV_HEADS = 2
GA_GROUP = GA_HEADS // GA_KV_HEADS
HEAD_DIM = 128
AXIAL_THETA = 10000.0
ML_HEADS = 4
ML_QKV_BLOCK = 4
DA_GROUPS = ((128, 1), (512, 4), (2048, 16))
DA_HEADS_PER_GROUP = 4
DA_HEADS = DA_HEADS_PER_GROUP * len(DA_GROUPS)
ROPE_THETA = 500000.0
ROPE_DIMS = HEAD_DIM // 4
N_EXPERTS = 16
EC_CAPACITY = 2

VMEM_LIMIT_BYTES = 52 * 1024 * 1024
HALO_ROWS = 8
MXU_DIM = 256
ML_CHUNK = 256
BAND_BLOCK = 128
BAND_HALF = 64


def _params(*sem):
    return pltpu.CompilerParams(dimension_semantics=sem, vmem_limit_bytes=VMEM_LIMIT_BYTES)


def _rms(x, g):
    ms = jnp.mean(x * x, axis=-1, keepdims=True)
    return x * lax.rsqrt(ms + NORM_EPS) * g


def _const_spec(shape):
    nd = len(shape)
    return pl.BlockSpec(shape, lambda *_: (0,) * nd)


def _conv3_rows(p, pprev, pnext, cw, cb, rows, tm):
    up = jnp.where(rows == 0, pprev, pltpu.roll(p, 1, 0))
    dn = jnp.where(rows == tm - 1, pnext, pltpu.roll(p, tm - 1, 0))
    return up * cw[0:1] + p * cw[1:2] + dn * cw[2:3] + cb


def _halo_specs(tm, d, n_rows):
    hb = tm // HALO_ROWS
    last = n_rows // HALO_ROWS - 1
    prev = pl.BlockSpec((HALO_ROWS, d), lambda i: (jnp.maximum(i * hb - 1, 0), 0))
    nxt = pl.BlockSpec((HALO_ROWS, d), lambda i: (jnp.minimum((i + 1) * hb, last), 0))
    return prev, nxt


def _edge_scales(i, tm, seq_len):
    t0 = i * tm
    keep_prev = jnp.where(t0 % seq_len == 0, 0.0, 1.0).astype(F32)
    keep_next = jnp.where((t0 + tm) % seq_len == 0, 0.0, 1.0).astype(F32)
    return keep_prev, keep_next


def _mm_res_kernel(*refs, prologue, n_row, n_const, tn):
    row_refs = refs[:n_row]
    const_refs = refs[n_row:n_row + n_const]
    w_ref, b_ref, res_ref, o_ref = refs[n_row + n_const:]
    lhs = prologue(*[r[...] for r in row_refs], *[c[...] for c in const_refs]).astype(BF16)
    for j in range(o_ref.shape[1] // tn):
        sl = slice(j * tn, (j + 1) * tn)
        o_ref[:, sl] = (res_ref[:, sl] + b_ref[:, sl]
                        + jnp.dot(lhs, w_ref[:, sl], preferred_element_type=F32))


def _mm_res(prologue, rows, consts, w, b, res, tm=512, tn=512):
    n, dout = res.shape
    tm = min(tm, n)
    k = w.shape[0]
    if b is None:
        b = jnp.zeros((1, dout), F32)
    in_specs = [pl.BlockSpec((tm, r.shape[1]), lambda i: (i, 0)) if r.ndim == 2
                else pl.BlockSpec((r.shape[0], tm, r.shape[2]), lambda i: (0, i, 0)) for r in rows]
    in_specs += [_const_spec(c.shape) for c in consts]
    in_specs += [_const_spec((k, dout)), _const_spec((1, dout)),
                 pl.BlockSpec((tm, dout), lambda i: (i, 0))]
    return pl.pallas_call(
        functools.partial(_mm_res_kernel, prologue=prologue, n_row=len(rows),
                          n_const=len(consts), tn=min(tn, dout)),
        out_shape=jax.ShapeDtypeStruct((n, dout), F32),
        grid=(n // tm,),
        in_specs=in_specs,
        out_specs=pl.BlockSpec((tm, dout), lambda i: (i, 0)),
        compiler_params=_params("parallel"),
    )(*rows, *consts, w.astype(BF16), b.reshape(1, dout).astype(F32), res)


def _hyena_in_kernel(x_ref, xp_ref, xn_ref, g_ref, w_ref, b_ref, cw_ref, cb_ref,
                     x0_ref, vx_ref, *, seq_len, tm, d, cols):
    keep_prev, keep_next = _edge_scales(pl.program_id(0), tm, seq_len)
    g = g_ref[...]
    xb = _rms(x_ref[...], g).astype(BF16)
    hb = _rms(jnp.concatenate([xp_ref[...], xn_ref[...]], axis=0), g).astype(BF16)
    rows = lax.broadcasted_iota(jnp.int32, (tm, 1), 0)

    def conv_part(c0):
        sl = slice(c0, c0 + cols)
        w = w_ref[:, sl]
        bias = b_ref[:, sl]
        p = jnp.dot(xb, w, preferred_element_type=F32) + bias
        ph = jnp.dot(hb, w, preferred_element_type=F32) + bias
        pprev = ph[HALO_ROWS - 1:HALO_ROWS, :] * keep_prev
        pnext = ph[HALO_ROWS:HALO_ROWS + 1, :] * keep_next
        return _conv3_rows(p, pprev, pnext, cw_ref[:, sl], cb_ref[:, sl], rows, tm)

    for j in range(d // cols):
        c = j * cols
        x0_ref[:, c:c + cols] = conv_part(c)
        vx_ref[:, c:c + cols] = conv_part(2 * d + c) * conv_part(d + c)


def _hyena_in(x, g, w_in, b_in, conv_w, conv_b, seq_len, tm=256, cols=512):
    n, d = x.shape
    tm = min(tm, seq_len)
    prev, nxt = _halo_specs(tm, d, n)
    row = pl.BlockSpec((tm, d), lambda i: (i, 0))
    return pl.pallas_call(
        functools.partial(_hyena_in_kernel, seq_len=seq_len, tm=tm, d=d, cols=cols),
        out_shape=(jax.ShapeDtypeStruct((n, d), F32), jax.ShapeDtypeStruct((n, d), F32)),
        grid=(n // tm,),
        in_specs=[row, prev, nxt, _const_spec((1, d)), _const_spec((d, 3 * d)),
                  _const_spec((1, 3 * d)), _const_spec((3, 3 * d)), _const_spec((1, 3 * d))],
        out_specs=(row, row),
        compiler_params=_params("parallel"),
    )(x, x, x, g.reshape(1, d), w_in.astype(BF16), b_in.reshape(1, 3 * d),
      conv_w, conv_b.reshape(1, 3 * d))


def _hyena_filter_taps(L, d, f_w1, f_b1, f_w2, f_b2, f_w3, f_b3, f_freq, decay):
    t = jnp.linspace(0.0, 1.0, L, dtype=F32)[:, None]
    w_ang = 2.0 * math.pi * jnp.arange(L, dtype=F32)[:, None] / L
    bands = jnp.linspace(1e-4, HY_BANDS - 1, HY_BANDS, dtype=F32)[None, :]
    z = jnp.concatenate([t, jnp.cos(bands * w_ang), -jnp.sin(bands * w_ang)], axis=-1)
    h = jnp.sin(f_freq[0] * (z @ f_w1 + f_b1))
    h = jnp.sin(f_freq[1] * (h @ f_w2 + f_b2))
    h = h @ f_w3 + f_b3
    h = h.reshape(L, 2, d) * jnp.exp(-t[:, :, None] * jnp.abs(decay)[None])
    k = jnp.concatenate([h[:, 0], jnp.zeros((1, d), F32), h[1:, 1][::-1]], axis=0)
    return k / jnp.sum(jnp.abs(k), axis=0, keepdims=True)


def _fft_dims(m):
    lg = m.bit_length() - 1
    p = 1 << ((lg + 1) // 2)
    return p, m // p


def _split_bf16(x):
    hi = x.astype(BF16)
    return hi, (x - hi.astype(F32)).astype(BF16)


def _mm_split(fh, fl, x, precise):
    if not precise:
        return jnp.dot(fh, x.astype(BF16), preferred_element_type=F32)
    xh, xl = _split_bf16(x)
    return (jnp.dot(fh, xh, preferred_element_type=F32) + jnp.dot(fl, xh, preferred_element_type=F32)
            + jnp.dot(fh, xl, preferred_element_type=F32))


def _cplx_as_real(cr, ci):
    top = jnp.concatenate([cr, -ci], axis=-1)
    bot = jnp.concatenate([ci, cr], axis=-1)
    return jnp.concatenate([top, bot], axis=-2)


def _unit_circle(idx, m):
    ang = (2.0 * math.pi / m) * idx.astype(F32)
    return jnp.cos(ang), jnp.sin(ang)


def _dft_consts(p, q):
    m = p * q
    k1 = jnp.arange(p, dtype=jnp.int32)
    n1 = jnp.arange(p // 2, dtype=jnp.int32)
    c, s = _unit_circle((k1[:, None] * n1[None, :]) % p, p)
    fa = _cplx_as_real(c, -s)
    c, s = _unit_circle((n1[:, None] * k1[None, :]) % p, p)
    fd = _cplx_as_real(c / m, s / m)
    k2 = jnp.arange(q, dtype=jnp.int32)
    n2 = jnp.arange(q, dtype=jnp.int32)
    idx = (n2[None, None, :] * (k2[None, :, None] * p + k1[:, None, None])) % m
    c, s = _unit_circle(idx, m)
    gb = _cplx_as_real(c, -s)
    ct, st = jnp.swapaxes(c, 1, 2), jnp.swapaxes(s, 1, 2)
    gc = _cplx_as_real(ct, st)
    return tuple(_split_bf16(a) for a in (fa, gb, gc, fd))


def _fft_a_kernel(x_ref, fh_ref, fl_ref, o_ref, *, precise):
    _, rows_in, group, d = x_ref.shape
    rows_out = o_ref.shape[1]
    for j in range(group):
        x = x_ref[:, :, j, :].reshape(2 * rows_in, d)
        y = _mm_split(fh_ref[...], fl_ref[...], x, precise)
        o_ref[:, :, j, :] = y.reshape(2, rows_out, d)


def _fft_b_kernel(a_ref, gh_ref, gl_ref, o_ref):
    _, q, d = a_ref.shape
    x = _mm_split(gh_ref[...], gl_ref[...], a_ref[...].reshape(2 * q, d), True)
    o_ref[...] = x.reshape(o_ref.shape)


def _fft_bc_kernel(a_ref, gb_ref, gc_ref, k_ref, z_ref):
    _, q, d = a_ref.shape
    x = _mm_split(gb_ref[...], None, a_ref[...].reshape(2 * q, d), False)
    xr, xi = x[:q], x[q:]
    kr, ki = k_ref[0], k_ref[1]
    y = jnp.concatenate([xr * kr - xi * ki, xr * ki + xi * kr], axis=0)
    z = _mm_split(gc_ref[...], None, y, False)
    z_ref[...] = z.reshape(z_ref.shape)


def _fft_rows(x5, f, rows_out, precise):
    pairs, _, rows_in, q, d = x5.shape
    blk = lambda r: pl.BlockSpec((None, 2, r, HALO_ROWS, d), lambda b, j: (b, 0, 0, j, 0))
    return pl.pallas_call(
        functools.partial(_fft_a_kernel, precise=precise),
        out_shape=jax.ShapeDtypeStruct((pairs, 2, rows_out, q, d), F32),
        grid=(pairs, q // HALO_ROWS),
        in_specs=[blk(rows_in), _const_spec(f[0].shape), _const_spec(f[1].shape)],
        out_specs=blk(rows_out),
        compiler_params=_params("parallel", "parallel"),
    )(x5, *f)


def _long_conv(vx, taps, batch, seq_len):
    n, d = vx.shape
    m = 2 * seq_len
    p, q = _fft_dims(m)
    assert batch % 2 == 0
    pairs = batch // 2
    fa, gb, gc, fd = _dft_consts(p, q)
    g_spec = pl.BlockSpec((None, 2 * q, 2 * q), lambda k1, b: (k1, 0, 0))
    slab = pl.BlockSpec((None, 2, q, d), lambda k1, b: (b, 0, k1, 0))

    zeros = jnp.zeros((seq_len, d), F32)
    kin = jnp.stack([taps[:seq_len], zeros, taps[seq_len:], zeros]).reshape(2, 2, p // 2, q, d)
    ka = _fft_rows(kin, fa, p, True).reshape(2, 2, p * q, d)
    kx = pl.pallas_call(
        _fft_b_kernel,
        out_shape=jax.ShapeDtypeStruct((2, 2, p * q, d), F32),
        grid=(p, 2),
        in_specs=[slab, g_spec, g_spec],
        out_specs=slab,
        compiler_params=_params("parallel", "parallel"),
    )(ka, *gb)
    sign = jnp.repeat(1.0 - 2.0 * (jnp.arange(p) % 2).astype(F32), q)[None, :, None]
    kspec = kx[0] + sign * kx[1]

    xa = _fft_rows(vx.reshape(pairs, 2, p // 2, q, d), fa, p, False).reshape(pairs, 2, p * q, d)
    z = pl.pallas_call(
        _fft_bc_kernel,
        out_shape=jax.ShapeDtypeStruct((pairs, 2, p * q, d), F32),
        grid=(p, pairs),
        in_specs=[slab, g_spec, g_spec, pl.BlockSpec((2, q, d), lambda k1, b: (0, k1, 0))],
        out_specs=slab,
        compiler_params=_params("parallel", "parallel"),
    )(xa, gb[0], gc[0], kspec)
    y = _fft_rows(z.reshape(pairs, 2, p, q, d), fd, p // 2, False)
    return y.reshape(n, d)


def _hyena_out_prologue(y, vx, x0, skip):
    return (y + vx * skip) * x0


def _hyena_layer(x, batch, seq_len, g, p):
    n, d = x.shape
    x0, vx = _hyena_in(x, g, p['w_in'], p['b_in'], p['conv_w'], p['conv_b'], seq_len)
    taps = _hyena_filter_taps(seq_len, d, p['f_w1'], p['f_b1'], p['f_w2'], p['f_b2'],
                              p['f_w3'], p['f_b3'], p['f_freq'], p['decay'])
    y = _long_conv(vx, taps, batch, seq_len)
    return _mm_res(_hyena_out_prologue, [y, vx, x0], [p['skip'].reshape(1, d)],
                   p['w_out'], p['b_out'], x)


def _head_norm_rope(xh, gain, cos, sin, half):
    y = _rms(xh, gain)
    lane = lax.broadcasted_iota(jnp.int32, (1, HEAD_DIM), 1)
    fwd = pltpu.roll(y, HEAD_DIM - half, 1)
    bwd = pltpu.roll(y, half, 1)
    partner = jnp.where((lane % (2 * half)) < half, fwd, bwd)
    return y * cos + partner * sin


def _qkv_rope_kernel(x_ref, g_ref, w_ref, qg_ref, kg_ref, cos_ref, sin_ref,
                     q_ref, k_ref, v_ref, *, nq, nk, nv, half):
    xb = _rms(x_ref[...], g_ref[...]).astype(BF16)
    cos = cos_ref[...]
    sin = sin_ref[...]
    per = MXU_DIM // HEAD_DIM
    for h0 in range(0, nq + nk + nv, per):
        pw = jnp.dot(xb, w_ref[:, h0 * HEAD_DIM:(h0 + per) * HEAD_DIM], preferred_element_type=F32)
        for h in range(h0, h0 + per):
            ph = pw[:, (h - h0) * HEAD_DIM:(h - h0 + 1) * HEAD_DIM]
            if h < nq:
                sl = slice(h * HEAD_DIM, (h + 1) * HEAD_DIM)
                q_ref[:, sl] = _head_norm_rope(ph, qg_ref[...], cos, sin, half).astype(BF16)
            elif h < nq + nk:
                sl = slice((h - nq) * HEAD_DIM, (h - nq + 1) * HEAD_DIM)
                k_ref[:, sl] = _head_norm_rope(ph, kg_ref[...], cos, sin, half).astype(BF16)
            else:
                sl = slice((h - nq - nk) * HEAD_DIM, (h - nq - nk + 1) * HEAD_DIM)
                v_ref[:, sl] = ph.astype(BF16)


def _qkv_rope(x, g, w_qkv, q_gain, k_gain, cos, sin, nq, nk, nv, half, seq_len, tm=512):
    n, d = x.shape
    tm = min(tm, seq_len)
    pos_blocks = seq_len // tm
    f = w_qkv.shape[1]
    row = pl.BlockSpec((tm, d), lambda i: (i, 0))
    tab = pl.BlockSpec((tm, HEAD_DIM), lambda i: (i % pos_blocks, 0))
    outs = tuple(jax.ShapeDtypeStruct((n, c * HEAD_DIM), BF16) for c in (nq, nk, nv))
    return pl.pallas_call(
        functools.partial(_qkv_rope_kernel, nq=nq, nk=nk, nv=nv, half=half),
        out_shape=outs,
        grid=(n // tm,),
        in_specs=[row, _const_spec((1, d)), _const_spec((d, f)),
                  _const_spec((1, HEAD_DIM)), _const_spec((1, HEAD_DIM)), tab, tab],
        out_specs=tuple(pl.BlockSpec((tm, c * HEAD_DIM), lambda i: (i, 0)) for c in (nq, nk, nv)),
        compiler_params=_params("parallel"),
    )(x, g.reshape(1, d), w_qkv.astype(BF16), q_gain.reshape(1, HEAD_DIM),
      k_gain.reshape(1, HEAD_DIM), cos, sin)


def _axial_tables(L):
    t = jnp.arange(L)
    r = (t // GRID_W).astype(F32)
    c = (t % GRID_W).astype(F32)
    nf = HEAD_DIM // 4
    inv = AXIAL_THETA ** (-(2.0 * jnp.arange(nf, dtype=F32)) / (2 * nf))
    ar, ac = r[:, None] * inv[None], c[:, None] * inv[None]
    cos = jnp.concatenate([jnp.cos(ar), jnp.cos(ar), jnp.cos(ac), jnp.cos(ac)], axis=-1)
    sin = jnp.concatenate([-jnp.sin(ar), jnp.sin(ar), -jnp.sin(ac), jnp.sin(ac)], axis=-1)
    return cos, sin


FLASH_SAFE_BOUND = 40.0
FLASH_BOUND_MARGIN = 1.001


def _flash_kernel(q_ref, k_ref, v_ref, o_ref, acc_scr, off_scr, lsum_scr, m_scr, l_scr, kn_scr,
                  *, tq, tk, tkf, seq_len, group):
    scale = HEAD_DIM ** -0.5
    c = scale * math.log2(math.e)

    @pl.when(pl.program_id(2) == 0)
    def _():
        def norm_step(j, mx):
            kt = k_ref[pl.ds(pl.multiple_of(j * tk, tk), tk), :].astype(F32)
            row = jnp.sum(kt * kt, axis=1, keepdims=True)
            return jnp.maximum(mx, jnp.max(row, axis=0, keepdims=True))

        kmax2 = lax.fori_loop(0, seq_len // tk, norm_step, jnp.zeros((1, 1), F32))
        kn_scr[...] = jnp.broadcast_to(kmax2, kn_scr.shape)

    kmax2 = kn_scr[0:1, 0:1]
    bmax = jnp.zeros((1, 1), F32)
    for h in range(group):
        qf = q_ref[:, h * HEAD_DIM:(h + 1) * HEAD_DIM].astype(F32)
        b = jnp.sqrt(jnp.sum(qf * qf, axis=1, keepdims=True) * kmax2) * FLASH_BOUND_MARGIN
        bmax = jnp.maximum(bmax, jnp.max(b, axis=0, keepdims=True))
        off_scr[h] = jnp.broadcast_to(b * c, (tq, HEAD_DIM))
    fast = (bmax * scale)[0, 0] <= FLASH_SAFE_BOUND

    @pl.when(fast)
    def _():
        acc_scr[...] = jnp.zeros_like(acc_scr)
        lsum_scr[...] = jnp.zeros_like(lsum_scr)
        n_lane_tiles = tkf // HEAD_DIM

        def body(j, carry):
            start = pl.multiple_of(j * tkf, tkf)
            kt = k_ref[pl.ds(start, tkf), :]
            vt = v_ref[pl.ds(start, tkf), :]
            for h in range(group):
                q = q_ref[:, h * HEAD_DIM:(h + 1) * HEAD_DIM]
                s = lax.dot_general(q, kt, (((1,), (1,)), ((), ())), preferred_element_type=F32)
                off = off_scr[h]
                p = jnp.exp2(s * c - jnp.concatenate([off] * n_lane_tiles, axis=1))
                part = p[:, 0:HEAD_DIM]
                for t in range(1, n_lane_tiles):
                    part = part + p[:, t * HEAD_DIM:(t + 1) * HEAD_DIM]
                lsum_scr[h] += part
                acc_scr[h] += jnp.dot(p.astype(BF16), vt, preferred_element_type=F32)
            return carry

        lax.fori_loop(0, seq_len // tkf, body, 0)
        for h in range(group):
            l = jnp.sum(lsum_scr[h], axis=1, keepdims=True)
            o_ref[:, h * HEAD_DIM:(h + 1) * HEAD_DIM] = (acc_scr[h] / l).astype(o_ref.dtype)

    @pl.when(jnp.logical_not(fast))
    def _():
        m_scr[...] = jnp.full(m_scr.shape, -jnp.inf, F32)
        l_scr[...] = jnp.zeros_like(l_scr)
        acc_scr[...] = jnp.zeros_like(acc_scr)

        def body(j, carry):
            start = pl.multiple_of(j * tk, tk)
            kt = k_ref[pl.ds(start, tk), :]
            vt = v_ref[pl.ds(start, tk), :]
            for h in range(group):
                q = q_ref[:, h * HEAD_DIM:(h + 1) * HEAD_DIM]
                s = lax.dot_general(q, kt, (((1,), (1,)), ((), ())), preferred_element_type=F32)
                m = m_scr[h]
                m_new = jnp.maximum(m, jnp.max(s, axis=1, keepdims=True))
                alpha = jnp.exp2((m - m_new) * c)
                p = jnp.exp2(s * c - m_new * c)
                l_scr[h] = alpha * l_scr[h] + jnp.sum(p, axis=1, keepdims=True)
                acc_scr[h] = alpha * acc_scr[h] + jnp.dot(p.astype(BF16), vt, preferred_element_type=F32)
                m_scr[h] = m_new
            return carry

        lax.fori_loop(0, seq_len // tk, body, 0)
        for h in range(group):
            o_ref[:, h * HEAD_DIM:(h + 1) * HEAD_DIM] = (acc_scr[h] / l_scr[h]).astype(o_ref.dtype)


def _flash_gqa(q, k, v, batch, seq_len, tq=512, tk=512, tkf=1024):
    tq = min(tq, seq_len)
    tk = min(tk, seq_len)
    tkf = min(tkf, seq_len)
    nq = seq_len // tq
    gw = GA_GROUP * HEAD_DIM
    wide = pltpu.VMEM((GA_GROUP, tq, HEAD_DIM), F32)
    thin = pltpu.VMEM((GA_GROUP, tq, 1), F32)
    return pl.pallas_call(
        functools.partial(_flash_kernel, tq=tq, tk=tk, tkf=tkf, seq_len=seq_len, group=GA_GROUP),
        out_shape=jax.ShapeDtypeStruct(q.shape, BF16),
        grid=(batch, GA_KV_HEADS, nq),
        in_specs=[pl.BlockSpec((tq, gw), lambda b, kv, i: (b * nq + i, kv)),
                  pl.BlockSpec((seq_len, HEAD_DIM), lambda b, kv, i: (b, kv)),
                  pl.BlockSpec((seq_len, HEAD_DIM), lambda b, kv, i: (b, kv))],
        out_specs=pl.BlockSpec((tq, gw), lambda b, kv, i: (b * nq + i, kv)),
        scratch_shapes=[wide, wide, wide, thin, thin, pltpu.VMEM((HALO_ROWS, HEAD_DIM), F32)],
        compiler_params=_params("parallel", "parallel", "arbitrary"),
    )(q, k, v)


def _identity_prologue(o):
    return o


def _gqa_layer(x, batch, seq_len, g, p):
    cos, sin = _axial_tables(seq_len)
    q, k, v = _qkv_rope(x, g, p['w_qkv'], p['q_gain'], p['k_gain'], cos, sin,
                        GA_HEADS, GA_KV_HEADS, GA_KV_HEADS, HEAD_DIM // 4, seq_len)
    o = _flash_gqa(q, k, v, batch, seq_len)
    return _mm_res(_identity_prologue, [o], [], p['w_o'], None, x)


def _ml_in_kernel(x_ref, xp_ref, xn_ref, g_ref, w_ref, cw_ref, cb_ref, wq_ref, wk_ref, wv_ref,
                  wg_ref, bg_ref, q_ref, k_ref, v_ref, xc_ref, sz_ref, gate_ref,
                  *, seq_len, tm, inner, k_scale):
    keep_prev, keep_next = _edge_scales(pl.program_id(0), tm, seq_len)
    g = g_ref[...]
    xb = _rms(x_ref[...], g).astype(BF16)
    hb = _rms(jnp.concatenate([xp_ref[...], xn_ref[...]], axis=0), g).astype(BF16)
    rows = lax.broadcasted_iota(jnp.int32, (tm, 1), 0)
    gacc = jnp.zeros(gate_ref.shape, F32)
    for t in range(inner // MXU_DIM):
        sl = slice(t * MXU_DIM, (t + 1) * MXU_DIM)
        w = w_ref[:, sl]
        xm = jnp.dot(xb, w, preferred_element_type=F32)
        xh = jnp.dot(hb, w, preferred_element_type=F32)
        pprev = xh[HALO_ROWS - 1:HALO_ROWS, :] * keep_prev
        pnext = xh[HALO_ROWS:HALO_ROWS + 1, :] * keep_next
        xc = _conv3_rows(xm, pprev, pnext, cw_ref[:, sl], cb_ref[:, sl], rows, tm)
        xc = xc * jax.nn.sigmoid(xc)
        z = jnp.dot(xb, w_ref[:, inner + t * MXU_DIM:inner + (t + 1) * MXU_DIM],
                    preferred_element_type=F32)
        xcb = xc.astype(BF16)
        q = jnp.dot(xcb, wq_ref[t], preferred_element_type=F32)
        k = jnp.dot(xcb, wk_ref[t], preferred_element_type=F32)
        v = jnp.dot(xm.astype(BF16), wv_ref[t], preferred_element_type=F32)
        qb, kb, vb = q.astype(BF16), k.astype(BF16), v.astype(BF16)
        gacc += (jnp.dot(qb, wg_ref[0, sl, :], preferred_element_type=F32)
                 + jnp.dot(kb, wg_ref[1, sl, :], preferred_element_type=F32)
                 + jnp.dot(vb, wg_ref[2, sl, :], preferred_element_type=F32))
        q_ref[:, sl] = qb
        k_ref[:, sl] = (k * k_scale).astype(BF16)
        v_ref[:, sl] = vb
        xc_ref[:, sl] = xc
        sz_ref[:, sl] = z * jax.nn.sigmoid(z)
    gate_ref[...] = gacc + bg_ref[...]


def _block_diag_tiles(w):
    nb, c, _ = w.shape
    per = MXU_DIM // c
    wt = w.reshape(nb // per, per, c, c)
    eye = jnp.eye(per, dtype=w.dtype)
    full = jnp.einsum('tpcd,pq->tpcqd', wt, eye)
    return full.reshape(nb // per, MXU_DIM, MXU_DIM)


def _ml_in(x, g, p, seq_len, tm=256):
    n, d = x.shape
    tm = min(tm, seq_len)
    inner = p['w_up'].shape[1] // 2
    ng = 4 * ML_HEADS
    dh = inner // ML_HEADS
    wq, wk, wv = (_block_diag_tiles(p[nm]).astype(BF16) for nm in ('w_q', 'w_k', 'w_v'))
    wg = jnp.transpose(p['w_gate'], (1, 2, 0, 3)).reshape(3, inner, ng).astype(BF16)
    bg = p['b_gate'].reshape(1, ng)
    prev, nxt = _halo_specs(tm, d, n)
    row = pl.BlockSpec((tm, d), lambda i: (i, 0))
    wide = pl.BlockSpec((tm, inner), lambda i: (i, 0))
    nt = inner // MXU_DIM
    return pl.pallas_call(
        functools.partial(_ml_in_kernel, seq_len=seq_len, tm=tm, inner=inner, k_scale=dh ** -0.5),
        out_shape=(jax.ShapeDtypeStruct((n, inner), BF16),) * 3
        + (jax.ShapeDtypeStruct((n, inner), F32),) * 2
        + (jax.ShapeDtypeStruct((n, ng), F32),),
        grid=(n // tm,),
        in_specs=[row, prev, nxt, _const_spec((1, d)), _const_spec((d, 2 * inner)),
                  _const_spec((3, inner)), _const_spec((1, inner)),
                  _const_spec((nt, MXU_DIM, MXU_DIM)), _const_spec((nt, MXU_DIM, MXU_DIM)),
                  _const_spec((nt, MXU_DIM, MXU_DIM)), _const_spec((3, inner, ng)),
                  _const_spec((1, ng))],
        out_specs=(wide,) * 5 + (pl.BlockSpec((tm, ng), lambda i: (i, 0)),),
        compiler_params=_params("parallel"),
    )(x, x, x, g.reshape(1, d), p['w_up'].astype(BF16), p['conv_w'],
      p['conv_b'].reshape(1, inner), wq, wk, wv, wg, bg)


def _log_sigmoid(x):
    return jnp.minimum(x, 0.0) - jnp.log1p(jnp.exp(-jnp.abs(x)))


def _mlstm_chunk_kernel(q_ref, k_ref, v_ref, gc_ref, gr_ref, h_ref, c_scr, n_scr, m_scr, *, lc):
    d = pl.program_id(2)

    @pl.when(pl.program_id(3) == 0)
    def _():
        c_scr[...] = jnp.zeros_like(c_scr)
        n_scr[...] = jnp.zeros_like(n_scr)
        m_scr[...] = jnp.zeros_like(m_scr)

    q = q_ref[...]
    k = k_ref[...]
    v = v_ref[...]
    gc = gc_ref[...]
    gr = gr_ref[...]
    i_col, g_col = gc[:, 0:1], gc[:, 1:2]
    i_row, g_row = gr[0:1, :], gr[1:2, :]

    jr = lax.broadcasted_iota(jnp.int32, (lc, lc), 0)
    sc = lax.broadcasted_iota(jnp.int32, (lc, lc), 1)
    seen = ((sc - jr) * (1 - 2 * d)) <= 0
    g_tot = jnp.where(d == 0, g_row[:, lc - 1:lc], g_row[:, 0:1])
    m_old = m_scr[0:1, 0:1]

    dmat = jnp.where(seen, g_col - g_row + i_row, -jnp.inf)
    inter = g_col + m_old
    m_q = jnp.maximum(inter, jnp.max(dmat, axis=1, keepdims=True))
    s_qk = lax.dot_general(q, k, (((1,), (1,)), ((), ())), preferred_element_type=F32)
    a = s_qk * jnp.exp(dmat - m_q)
    w_int = jnp.exp(inter - m_q)
    q_c = jnp.dot(q, c_scr[...].astype(BF16), preferred_element_type=F32)
    num = jnp.dot(a.astype(BF16), v, preferred_element_type=F32) + q_c * w_int
    q_n = jnp.sum(q.astype(F32) * n_scr[...], axis=1, keepdims=True)
    den = jnp.sum(a, axis=1, keepdims=True) + w_int * q_n
    den = jnp.maximum(jnp.abs(den), jnp.exp(-m_q))
    h_ref[...] = num / den

    a_row = g_tot - g_row + i_row
    m_new = jnp.maximum(g_tot + m_old, jnp.max(a_row, axis=1, keepdims=True))
    ws_col = jnp.exp(g_tot - g_col + i_col - m_new)
    dec = jnp.exp(g_tot + m_old - m_new)
    kw = k.astype(F32) * ws_col
    upd = lax.dot_general(kw.astype(BF16), v, (((0,), (0,)), ((), ())), preferred_element_type=F32)
    c_scr[...] = dec * c_scr[...] + upd
    n_scr[...] = dec * n_scr[...] + jnp.sum(kw, axis=0, keepdims=True)
    m_scr[...] = jnp.broadcast_to(m_new, m_scr.shape)


def _ml_gate_kernel(g_ref, o_ref, *, lc, nh):
    g = g_ref[...]
    jr = lax.broadcasted_iota(jnp.int32, (lc, lc), 0)
    sc = lax.broadcasted_iota(jnp.int32, (lc, lc), 1)
    tril = jnp.where(sc <= jr, 1.0, 0.0).astype(BF16)
    triu = jnp.where(sc >= jr, 1.0, 0.0).astype(BF16)
    col = lax.broadcasted_iota(jnp.int32, (1, 4 * nh), 1)
    lf = _log_sigmoid(g)
    x1 = lf.astype(BF16)
    r1 = lf - x1.astype(F32)
    x2 = r1.astype(BF16)
    x3 = (r1 - x2.astype(F32)).astype(BF16)
    cum = lambda m: (jnp.dot(m, x1, preferred_element_type=F32) + jnp.dot(m, x2, preferred_element_type=F32)
                     + jnp.dot(m, x3, preferred_element_type=F32))
    gsum = jnp.where(col < 2 * nh, cum(tril), cum(triu))
    o_ref[...] = jnp.where((col % (2 * nh)) >= nh, gsum, g)


def _mlstm_chunks(q, k, v, gates, batch, seq_len):
    n, inner = q.shape
    dh = inner // ML_HEADS
    lc = min(ML_CHUNK, seq_len)
    nc = seq_len // lc
    ng = gates.shape[1]
    gates = pl.pallas_call(
        functools.partial(_ml_gate_kernel, lc=lc, nh=ML_HEADS),
        out_shape=jax.ShapeDtypeStruct((n, ng), F32),
        grid=(n // lc,),
        in_specs=[pl.BlockSpec((lc, ng), lambda i: (i, 0))],
        out_specs=pl.BlockSpec((lc, ng), lambda i: (i, 0)),
        compiler_params=_params("parallel"),
    )(gates)
    g4 = gates.reshape(n, 2, 2, ML_HEADS)
    gcol = jnp.transpose(g4, (1, 3, 0, 2))
    grow = jnp.transpose(g4, (1, 3, 2, 0))

    def chunk(b, c, dd):
        return b * nc + c + dd * (nc - 1 - 2 * c)

    qkv_spec = pl.BlockSpec((lc, dh), lambda b, h, dd, c: (chunk(b, c, dd), h))
    return pl.pallas_call(
        functools.partial(_mlstm_chunk_kernel, lc=lc),
        out_shape=jax.ShapeDtypeStruct((2, n, inner), F32),
        grid=(batch, ML_HEADS, 2, nc),
        in_specs=[qkv_spec, qkv_spec, qkv_spec,
                  pl.BlockSpec((None, None, lc, 2), lambda b, h, dd, c: (dd, h, chunk(b, c, dd), 0)),
                  pl.BlockSpec((None, None, 2, lc), lambda b, h, dd, c: (dd, h, 0, chunk(b, c, dd)))],
        out_specs=pl.BlockSpec((None, lc, dh), lambda b, h, dd, c: (dd, chunk(b, c, dd), h)),
        scratch_shapes=[pltpu.VMEM((dh, dh), F32), pltpu.VMEM((1, dh), F32),
                        pltpu.VMEM((HALO_ROWS, HEAD_DIM), F32)],
        compiler_params=_params("parallel", "parallel", "parallel", "arbitrary"),
    )(q, k, v, gcol, grow)


def _ml_out_prologue(hs, xc, sz, gain, skip):
    h = hs[0] + hs[1]
    dh = h.shape[1] // ML_HEADS
    parts = []
    for i in range(ML_HEADS):
        sl = slice(i * dh, (i + 1) * dh)
        parts.append(_rms(h[:, sl], gain[:, sl]))
    hn = jnp.concatenate(parts, axis=1)
    return (hn + skip * xc) * sz


def _mlstm_layer(x, batch, seq_len, g, p):
    q, k, v, xc, sz, gates = _ml_in(x, g, p, seq_len)
    inner = q.shape[1]
    hs = _mlstm_chunks(q, k, v, gates, batch, seq_len)
    return _mm_res(_ml_out_prologue, [hs, xc, sz],
                   [p['norm_gain'].reshape(1, inner), p['skip'].reshape(1, inner)],
                   p['w_down'], None, x, tm=256)


def _rope_tables(L):
    inv = ROPE_THETA ** (-(2.0 * jnp.arange(ROPE_DIMS // 2, dtype=F32)) / ROPE_DIMS)
    ang = jnp.arange(L, dtype=F32)[:, None] * inv[None]
    pad = HEAD_DIM - ROPE_DIMS
    cos = jnp.concatenate([jnp.cos(ang), jnp.cos(ang), jnp.ones((L, pad), F32)], axis=-1)
    sin = jnp.concatenate([-jnp.sin(ang), jnp.sin(ang), jnp.zeros((L, pad), F32)], axis=-1)
    return cos, sin


def _band_kernel(q_ref, kp_ref, kc_ref, kn_ref, vp_ref, vc_ref, vn_ref, o_ref, lse_ref,
                 *, s_len, heads):
    i = pl.program_id(2)
    qb = BAND_BLOCK
    w = qb + 2 * BAND_HALF
    a = lax.broadcasted_iota(jnp.int32, (qb, w), 0)
    c = lax.broadcasted_iota(jnp.int32, (qb, w), 1)
    rel = c - BAND_HALF - a
    key_pos = i * qb - BAND_HALF + c
    valid = (jnp.abs(rel) <= BAND_HALF) & (key_pos >= 0) & (key_pos < s_len)
    scale = HEAD_DIM ** -0.5
    for h in range(heads):
        sl = slice(h * HEAD_DIM, (h + 1) * HEAD_DIM)
        kw = jnp.concatenate([kp_ref[qb - BAND_HALF:, sl], kc_ref[:, sl], kn_ref[:BAND_HALF, sl]], axis=0)
        vw = jnp.concatenate([vp_ref[qb - BAND_HALF:, sl], vc_ref[:, sl], vn_ref[:BAND_HALF, sl]], axis=0)
        s = lax.dot_general(q_ref[:, sl], kw, (((1,), (1,)), ((), ())),
                            preferred_element_type=F32) * scale
        s = jnp.where(valid, s, -jnp.inf)
        m = jnp.max(s, axis=1, keepdims=True)
        p = jnp.exp(s - m)
        l = jnp.sum(p, axis=1, keepdims=True)
        o = jnp.dot(p.astype(BF16), vw, preferred_element_type=F32)
        o_ref[:, sl] = o / l
        lse_ref[:, sl] = jnp.broadcast_to(m + jnp.log(l), (qb, HEAD_DIM))


def _da_qkv_kernel(x_ref, g_ref, w_ref, qg_ref, kg_ref, cos_ref, sin_ref, *refs, dils, tm, half):
    n_perm = sum(1 for d in dils if d > 1)
    perm_refs, out_refs = refs[:n_perm], refs[n_perm:]
    xb = _rms(x_ref[...], g_ref[...]).astype(BF16)
    cos, sin = cos_ref[...], sin_ref[...]
    gw = DA_HEADS_PER_GROUP * HEAD_DIM
    heads = DA_HEADS_PER_GROUP * len(dils)
    for kind in range(3):
        gain = (qg_ref, kg_ref, None)[kind]
        pi = 0
        for gi, dil in enumerate(dils):
            c0 = (kind * heads + gi * DA_HEADS_PER_GROUP) * HEAD_DIM
            ph = jnp.dot(xb, w_ref[:, c0:c0 + gw], preferred_element_type=F32)
            if gain is not None:
                ph = jnp.concatenate(
                    [_head_norm_rope(ph[:, h * HEAD_DIM:(h + 1) * HEAD_DIM], gain[...], cos, sin, half)
                     for h in range(DA_HEADS_PER_GROUP)], axis=1)
            val = ph.astype(BF16)
            o_ref = out_refs[kind * len(dils) + gi]
            if dil == 1:
                o_ref[...] = val
            else:
                pv = jnp.dot(perm_refs[pi][...], val, preferred_element_type=F32).astype(BF16)
                pi += 1
                rows = tm // dil
                for r in range(dil):
                    o_ref[:, r * gw:(r + 1) * gw] = pv[r * rows:(r + 1) * rows, :]


def _da_qkv(x, g, w_qkv, q_gain, k_gain, cos, sin, seq_len, tm=256):
    n, d = x.shape
    tm = min(tm, seq_len)
    pos_blocks = seq_len // tm
    dils = tuple(dil for _, dil in DA_GROUPS)
    gw = DA_HEADS_PER_GROUP * HEAD_DIM
    f = w_qkv.shape[1]
    perms = []
    for dil in dils:
        if dil > 1:
            dst = jnp.arange(tm)
            src = (dst % (tm // dil)) * dil + dst // (tm // dil)
            perms.append((src[:, None] == jnp.arange(tm)[None, :]).astype(BF16))
    row = pl.BlockSpec((tm, d), lambda i: (i, 0))
    tab = pl.BlockSpec((tm, HEAD_DIM), lambda i: (i % pos_blocks, 0))
    out_shapes = tuple(jax.ShapeDtypeStruct((n // dil, dil * gw), BF16) for _ in range(3) for dil in dils)
    out_specs = tuple(pl.BlockSpec((tm // dil, dil * gw), lambda i: (i, 0)) for _ in range(3) for dil in dils)
    outs = pl.pallas_call(
        functools.partial(_da_qkv_kernel, dils=dils, tm=tm, half=ROPE_DIMS // 2),
        out_shape=out_shapes,
        grid=(n // tm,),
        in_specs=[row, _const_spec((1, d)), _const_spec((d, f)), _const_spec((1, HEAD_DIM)),
                  _const_spec((1, HEAD_DIM)), tab, tab] + [_const_spec((tm, tm))] * len(perms),
        out_specs=out_specs,
        compiler_params=_params("parallel"),
    )(x, g.reshape(1, d), w_qkv.astype(BF16), q_gain.reshape(1, HEAD_DIM), k_gain.reshape(1, HEAD_DIM),
      cos, sin, *perms)
    ng = len(dils)
    return [(outs[gi], outs[ng + gi], outs[2 * ng + gi]) for gi in range(ng)]


def _band_attention(q, k, v, batch, seq_len, dil):
    n = batch * seq_len
    s_len = seq_len // dil
    nb = s_len // BAND_BLOCK
    hw = DA_HEADS_PER_GROUP * HEAD_DIM
    view = lambda a: a.reshape(batch, s_len, dil * hw)
    qv, kv, vv = view(q), view(k), view(v)

    def spec(shift):
        return pl.BlockSpec(
            (None, BAND_BLOCK, hw),
            lambda b, r, i: (b, jnp.clip(i + shift, 0, nb - 1), r))

    out_spec = pl.BlockSpec((None, BAND_BLOCK, hw), lambda b, r, i: (b, i, r))
    o, lse = pl.pallas_call(
        functools.partial(_band_kernel, s_len=s_len, heads=DA_HEADS_PER_GROUP),
        out_shape=(jax.ShapeDtypeStruct((batch, s_len, dil * hw), F32),) * 2,
        grid=(batch, dil, nb),
        in_specs=[spec(0), spec(-1), spec(0), spec(1), spec(-1), spec(0), spec(1)],
        out_specs=(out_spec, out_spec),
        compiler_params=_params("parallel", "parallel", "parallel"),
    )(qv, kv, kv, kv, vv, vv, vv)
    return o.reshape(n, hw), lse.reshape(n, hw)


def _da_out_prologue(o0, o1, o2, l0, l1, l2):
    m = jnp.maximum(jnp.maximum(l0, l1), l2)
    e0, e1, e2 = jnp.exp(l0 - m), jnp.exp(l1 - m), jnp.exp(l2 - m)
    return (e0 * o0 + e1 * o1 + e2 * o2) / (e0 + e1 + e2)


def _dilated_layer(x, batch, seq_len, g, p):
    cos, sin = _rope_tables(seq_len)
    qkv = _da_qkv(x, g, p['w_qkv'], p['q_gain'], p['k_gain'], cos, sin, seq_len)
    outs, lses = [], []
    for (q, k, v), (_, dil) in zip(qkv, DA_GROUPS):
        o, lse = _band_attention(q, k, v, batch, seq_len, dil)
        outs.append(o)
        lses.append(lse)
    return _mm_res(_da_out_prologue, outs + lses, [], p['w_o'], None, x)


def _router_kernel(x_ref, g_ref, wr_ref, xn_ref, aff_ref):
    xn = _rms(x_ref[...], g_ref[...])
    xn_ref[...] = xn.astype(BF16)
    logits = lax.dot_general(wr_ref[...], xn, (((1,), (1,)), ((), ())),
                             precision=HIGHEST, preferred_element_type=F32)
    m = jnp.max(logits, axis=0, keepdims=True)
    e = jnp.exp(logits - m)
    aff_ref[...] = e / jnp.sum(e, axis=0, keepdims=True)


def _router(x, g, w_router, tm=512):
    n, d = x.shape
    tm = min(tm, n)
    e = w_router.shape[1]
    return pl.pallas_call(
        _router_kernel,
        out_shape=(jax.ShapeDtypeStruct((n, d), BF16), jax.ShapeDtypeStruct((e, n), F32)),
        grid=(n // tm,),
        in_specs=[pl.BlockSpec((tm, d), lambda i: (i, 0)), _const_spec((1, d)), _const_spec((e, d))],
        out_specs=(pl.BlockSpec((tm, d), lambda i: (i, 0)), pl.BlockSpec((e, tm), lambda i: (0, i))),
        compiler_params=_params("parallel"),
    )(x, g.reshape(1, d), w_router.T)


def _expert_ffn_kernel(xe_ref, gate_ref, w1_ref, w3_ref, w2_ref, ye_ref):
    xe = xe_ref[...]
    h1 = jnp.dot(xe, w1_ref[...], preferred_element_type=F32)
    h3 = jnp.dot(xe, w3_ref[...], preferred_element_type=F32)
    hid = (h1 * jax.nn.sigmoid(h1) * h3).astype(BF16)
    ye_ref[...] = (jnp.dot(hid, w2_ref[...], preferred_element_type=F32) * gate_ref[...]).astype(ye_ref.dtype)


def _expert_ffn(xe, gates, w1, w3, w2, tm=512):
    e, c, d = xe.shape
    f = w1.shape[2]
    tm = min(tm, c)
    return pl.pallas_call(
        _expert_ffn_kernel,
        out_shape=jax.ShapeDtypeStruct((e, c, d), BF16),
        grid=(e, c // tm),
        in_specs=[pl.BlockSpec((None, tm, d), lambda ei, ci: (ei, ci, 0)),
                  pl.BlockSpec((None, tm, 1), lambda ei, ci: (ei, ci, 0)),
                  pl.BlockSpec((None, d, f), lambda ei, ci: (ei, 0, 0)),
                  pl.BlockSpec((None, d, f), lambda ei, ci: (ei, 0, 0)),
                  pl.BlockSpec((None, f, d), lambda ei, ci: (ei, 0, 0))],
        out_specs=pl.BlockSpec((None, tm, d), lambda ei, ci: (ei, ci, 0)),
        compiler_params=_params("parallel", "arbitrary"),
    )(xe, gates[..., None], w1, w3, w2)


def _moe_layer(x, group_sizes, g, w_router, w1, w3, w2, split_output):
    n, d = x.shape
    xn, aff_t = _router(x, g, w_router)
    xes, gts, idxs = [], [], []
    start = 0
    for ng in group_sizes:
        cap = EC_CAPACITY * ng // N_EXPERTS
        gates, idx = lax.top_k(aff_t[:, start:start + ng], cap)
        idx = idx + start
        xes.append(xn[idx])
        gts.append(gates)
        idxs.append(idx)
        start += ng
    ye = _expert_ffn(jnp.concatenate(xes, axis=1), jnp.concatenate(gts, axis=1),
                     w1.astype(BF16), w3.astype(BF16), w2.astype(BF16))
    idx_all = jnp.concatenate(idxs, axis=1)
    return _combine(x, ye.reshape(-1, d), idx_all.reshape(-1), group_sizes if split_output else (n,))


COMBINE_TOKENS = 512
COMBINE_ROWS = 512


def _combine_kernel(tile_ref, blk_ref, live_ref, x_ref, tok_ref, ye_ref, *o_refs, split_tiles):
    w = pl.program_id(0)
    tile = tile_ref[w]
    first = jnp.logical_or(w == 0, tile != tile_ref[jnp.maximum(w - 1, 0)])
    tt = x_ref.shape[0]
    rows = lax.broadcasted_iota(jnp.int32, (tt, COMBINE_ROWS), 0)
    onehot = jnp.where(rows == tok_ref[...] - tile * tt, 1.0, 0.0).astype(BF16)
    add = jnp.dot(onehot, ye_ref[...], preferred_element_type=F32) * live_ref[w].astype(F32)
    lo_tile = 0
    for o_ref, n_tiles in zip(o_refs, split_tiles):
        mine = jnp.logical_and(tile >= lo_tile, tile < lo_tile + n_tiles)

        @pl.when(jnp.logical_and(mine, first))
        def _(o_ref=o_ref):
            o_ref[...] = x_ref[...] + add

        @pl.when(jnp.logical_and(mine, jnp.logical_not(first)))
        def _(o_ref=o_ref):
            o_ref[...] += add

        lo_tile += n_tiles


def _combine(x, ye, tok, splits):
    n, d = x.shape
    p = tok.shape[0]
    tt = min(COMBINE_TOKENS, min(splits))
    rb = COMBINE_ROWS
    assert p % rb == 0 and all(s % tt == 0 for s in splits) and sum(splits) == n
    tiles, nblk = n // tt, p // rb
    split_tiles = tuple(s // tt for s in splits)
    order = jnp.argsort(tok)
    tok_sorted = tok[order].astype(jnp.int32)
    ye_sorted = ye[order]

    edges = jnp.arange(tiles + 1, dtype=jnp.int32) * tt
    bounds = jnp.searchsorted(tok_sorted, edges, method='compare_all').astype(jnp.int32)
    lo, hi = bounds[:-1], bounds[1:]
    first_blk = jnp.minimum(lo // rb, nblk - 1)
    last_blk = jnp.where(hi > lo, (hi - 1) // rb, first_blk)
    n_items = last_blk - first_blk + 1
    item_end = jnp.cumsum(n_items)
    item_start = item_end - n_items
    max_items = nblk + 2 * tiles
    w = jnp.arange(max_items, dtype=jnp.int32)
    tile = jnp.minimum(jnp.searchsorted(item_end, w, side='right', method='compare_all'),
                       tiles - 1).astype(jnp.int32)
    k = w - item_start[tile]
    live = (k < n_items[tile]).astype(jnp.int32)
    blk = jnp.minimum(first_blk[tile] + k, nblk - 1).astype(jnp.int32)

    def out_spec(lo_tile, n_tiles):
        return pl.BlockSpec(
            (tt, d), lambda i, tile_r, blk_r, live_r: (jnp.clip(tile_r[i] - lo_tile, 0, n_tiles - 1), 0))

    starts = [sum(split_tiles[:j]) for j in range(len(splits))]
    grid_spec = pltpu.PrefetchScalarGridSpec(
        num_scalar_prefetch=3,
        grid=(max_items,),
        in_specs=[pl.BlockSpec((tt, d), lambda i, tile_r, blk_r, live_r: (tile_r[i], 0)),
                  pl.BlockSpec((None, 1, rb), lambda i, tile_r, blk_r, live_r: (blk_r[i], 0, 0)),
                  pl.BlockSpec((rb, d), lambda i, tile_r, blk_r, live_r: (blk_r[i], 0))],
        out_specs=tuple(out_spec(s, t) for s, t in zip(starts, split_tiles)),
    )
    return pl.pallas_call(
        functools.partial(_combine_kernel, split_tiles=split_tiles),
        out_shape=tuple(jax.ShapeDtypeStruct((s, d), F32) for s in splits),
        grid_spec=grid_spec,
        compiler_params=_params("arbitrary"),
    )(tile, blk, live, x, tok_sorted.reshape(nblk, 1, rb), ye_sorted)


def _trunk(x, batch, seq_len, group_sizes, p):
    depth = p['norm_gain'].shape[0]
    for i in range(depth):
        mixer, j = i % 4, i // 4
        g = p['norm_gain'][i, 0]
        if mixer == 0:
            x = _hyena_layer(x, batch, seq_len, g, {k[3:]: v[j] for k, v in p.items() if k.startswith('hy_')})
        elif mixer == 1:
            x = _gqa_layer(x, batch, seq_len, g, {k[3:]: v[j] for k, v in p.items() if k.startswith('ga_')})
        elif mixer == 2:
            x = _mlstm_layer(x, batch, seq_len, g, {k[3:]: v[j] for k, v in p.items() if k.startswith('ml_')})
        else:
            x = _dilated_layer(x, batch, seq_len, g, {k[3:]: v[j] for k, v in p.items() if k.startswith('da_')})
        last = i == depth - 1
        outs = _moe_layer(x, group_sizes, p['norm_gain'][i, 1], p['moe_w_router'][i],
                          p['moe_w1'][i], p['moe_w3'][i], p['moe_w2'][i], split_output=last)
        x = outs if last else outs[0]
    return x


def kernel(x_prompt, x_sample, norm_gain, hy_w_in, hy_b_in, hy_conv_w, hy_conv_b, hy_f_w1, hy_f_b1, hy_f_w2, hy_f_b2, hy_f_w3, hy_f_b3, hy_f_freq, hy_decay, hy_skip, hy_w_out, hy_b_out, ga_w_qkv, ga_q_gain, ga_k_gain, ga_w_o, ml_w_up, ml_conv_w, ml_conv_b, ml_w_q, ml_w_k, ml_w_v, ml_w_gate, ml_b_gate, ml_norm_gain, ml_skip, ml_w_down, da_w_qkv, da_q_gain, da_k_gain, da_w_o, moe_w_router, moe_w1, moe_w3, moe_w2):
    p = dict(
        norm_gain=norm_gain,
        hy_w_in=hy_w_in, hy_b_in=hy_b_in, hy_conv_w=hy_conv_w, hy_conv_b=hy_conv_b,
        hy_f_w1=hy_f_w1, hy_f_b1=hy_f_b1, hy_f_w2=hy_f_w2, hy_f_b2=hy_f_b2,
        hy_f_w3=hy_f_w3, hy_f_b3=hy_f_b3, hy_f_freq=hy_f_freq, hy_decay=hy_decay,
        hy_skip=hy_skip, hy_w_out=hy_w_out, hy_b_out=hy_b_out,
        ga_w_qkv=ga_w_qkv, ga_q_gain=ga_q_gain, ga_k_gain=ga_k_gain, ga_w_o=ga_w_o,
        ml_w_up=ml_w_up, ml_conv_w=ml_conv_w, ml_conv_b=ml_conv_b, ml_w_q=ml_w_q,
        ml_w_k=ml_w_k, ml_w_v=ml_w_v, ml_w_gate=ml_w_gate, ml_b_gate=ml_b_gate,
        ml_norm_gain=ml_norm_gain, ml_skip=ml_skip, ml_w_down=ml_w_down,
        da_w_qkv=da_w_qkv, da_q_gain=da_q_gain, da_k_gain=da_k_gain, da_w_o=da_w_o,
        moe_w_router=moe_w_router, moe_w1=moe_w1, moe_w3=moe_w3, moe_w2=moe_w2,
    )
    bp, seq_len, d = x_prompt.shape
    bs = x_sample.shape[0]
    assert x_sample.shape[1] == seq_len
    x = jnp.concatenate([x_prompt, x_sample], axis=0).reshape((bp + bs) * seq_len, d)
    y_prompt, y_sample = _trunk(x, bp + bs, seq_len, (bp * seq_len, bs * seq_len), p)
    return (y_prompt.reshape(bp, seq_len, d), y_sample.reshape(bs, seq_len, d))
```

```python
import functools
import math

import jax
import jax.numpy as jnp
from jax import lax
from jax.experimental import pallas as pl
from jax.experimental.pallas import tpu as pltpu

F32 = jnp.float32
BF16 = jnp.bfloat16
HIGHEST = lax.Precision.HIGHEST

NORM_EPS = 1e-6
GRID_W = 64
HY_BANDS = 16
GA_HEADS = 8
GA_KV_HEADS = 2
GA_GROUP = GA_HEADS // GA_KV_HEADS
HEAD_DIM = 128
AXIAL_THETA = 10000.0
ML_HEADS = 4
ML_QKV_BLOCK = 4
DA_GROUPS = ((128, 1), (512, 4), (2048, 16))
DA_HEADS_PER_GROUP = 4
DA_HEADS = DA_HEADS_PER_GROUP * len(DA_GROUPS)
ROPE_THETA = 500000.0
ROPE_DIMS = HEAD_DIM // 4
N_EXPERTS = 16
EC_CAPACITY = 2

VMEM_LIMIT_BYTES = 52 * 1024 * 1024
HALO_ROWS = 8
MXU_DIM = 256
ML_CHUNK = 256
BAND_BLOCK = 128
BAND_HALF = 64
BAND_SUBBLOCKS = 4


def _params(*sem):
    return pltpu.CompilerParams(dimension_semantics=sem, vmem_limit_bytes=VMEM_LIMIT_BYTES)


def _rms(x, g):
    ms = jnp.mean(x * x, axis=-1, keepdims=True)
    return x * lax.rsqrt(ms + NORM_EPS) * g


def _const_spec(shape):
    nd = len(shape)
    return pl.BlockSpec(shape, lambda *_: (0,) * nd)


def _conv3_rows(p, pprev, pnext, cw, cb, rows, tm):
    up = jnp.where(rows == 0, pprev, pltpu.roll(p, 1, 0))
    dn = jnp.where(rows == tm - 1, pnext, pltpu.roll(p, tm - 1, 0))
    return up * cw[0:1] + p * cw[1:2] + dn * cw[2:3] + cb


def _halo_specs(tm, d, n_rows):
    hb = tm // HALO_ROWS
    last = n_rows // HALO_ROWS - 1
    prev = pl.BlockSpec((HALO_ROWS, d), lambda i: (jnp.maximum(i * hb - 1, 0), 0))
    nxt = pl.BlockSpec((HALO_ROWS, d), lambda i: (jnp.minimum((i + 1) * hb, last), 0))
    return prev, nxt


def _edge_scales(i, tm, seq_len):
    t0 = i * tm
    keep_prev = jnp.where(t0 % seq_len == 0, 0.0, 1.0).astype(F32)
    keep_next = jnp.where((t0 + tm) % seq_len == 0, 0.0, 1.0).astype(F32)
    return keep_prev, keep_next


def _mm_res_kernel(*refs, prologue, n_row, n_const, tn):
    row_refs = refs[:n_row]
    const_refs = refs[n_row:n_row + n_const]
    w_ref, b_ref, res_ref, o_ref = refs[n_row + n_const:]
    lhs = prologue(*[r[...] for r in row_refs], *[c[...] for c in const_refs]).astype(BF16)
    for j in range(o_ref.shape[1] // tn):
        sl = slice(j * tn, (j + 1) * tn)
        o_ref[:, sl] = (res_ref[:, sl] + b_ref[:, sl]
                        + jnp.dot(lhs, w_ref[:, sl], preferred_element_type=F32))


def _mm_res(prologue, rows, consts, w, b, res, tm=512, tn=512):
    n, dout = res.shape
    tm = min(tm, n)
    k = w.shape[0]
    if b is None:
        b = jnp.zeros((1, dout), F32)
    in_specs = [pl.BlockSpec((tm, r.shape[1]), lambda i: (i, 0)) if r.ndim == 2
                else pl.BlockSpec((r.shape[0], tm, r.shape[2]), lambda i: (0, i, 0)) for r in rows]
    in_specs += [_const_spec(c.shape) for c in consts]
    in_specs += [_const_spec((k, dout)), _const_spec((1, dout)),
                 pl.BlockSpec((tm, dout), lambda i: (i, 0))]
    return pl.pallas_call(
        functools.partial(_mm_res_kernel, prologue=prologue, n_row=len(rows),
                          n_const=len(consts), tn=min(tn, dout)),
        out_shape=jax.ShapeDtypeStruct((n, dout), F32),
        grid=(n // tm,),
        in_specs=in_specs,
        out_specs=pl.BlockSpec((tm, dout), lambda i: (i, 0)),
        compiler_params=_params("parallel"),
    )(*rows, *consts, w.astype(BF16), b.reshape(1, dout).astype(F32), res)


def _hyena_in_kernel(x_ref, xp_ref, xn_ref, g_ref, w_ref, b_ref, cw_ref, cb_ref,
                     x0_ref, vx_ref, *, seq_len, tm, d, cols):
    keep_prev, keep_next = _edge_scales(pl.program_id(0), tm, seq_len)
    g = g_ref[...]
    xb = _rms(x_ref[...], g).astype(BF16)
    hb = _rms(jnp.concatenate([xp_ref[...], xn_ref[...]], axis=0), g).astype(BF16)
    rows = lax.broadcasted_iota(jnp.int32, (tm, 1), 0)

    def conv_part(c0):
        sl = slice(c0, c0 + cols)
        w = w_ref[:, sl]
        bias = b_ref[:, sl]
        p = jnp.dot(xb, w, preferred_element_type=F32) + bias
        ph = jnp.dot(hb, w, preferred_element_type=F32) + bias
        pprev = ph[HALO_ROWS - 1:HALO_ROWS, :] * keep_prev
        pnext = ph[HALO_ROWS:HALO_ROWS + 1, :] * keep_next
        return _conv3_rows(p, pprev, pnext, cw_ref[:, sl], cb_ref[:, sl], rows, tm)

    for j in range(d // cols):
        c = j * cols
        x0_ref[:, c:c + cols] = conv_part(c)
        vx_ref[:, c:c + cols] = conv_part(2 * d + c) * conv_part(d + c)


def _hyena_in(x, g, w_in, b_in, conv_w, conv_b, seq_len, tm=256, cols=512):
    n, d = x.shape
    tm = min(tm, seq_len)
    prev, nxt = _halo_specs(tm, d, n)
    row = pl.BlockSpec((tm, d), lambda i: (i, 0))
    return pl.pallas_call(
        functools.partial(_hyena_in_kernel, seq_len=seq_len, tm=tm, d=d, cols=cols),
        out_shape=(jax.ShapeDtypeStruct((n, d), F32), jax.ShapeDtypeStruct((n, d), F32)),
        grid=(n // tm,),
        in_specs=[row, prev, nxt, _const_spec((1, d)), _const_spec((d, 3 * d)),
                  _const_spec((1, 3 * d)), _const_spec((3, 3 * d)), _const_spec((1, 3 * d))],
        out_specs=(row, row),
        compiler_params=_params("parallel"),
    )(x, x, x, g.reshape(1, d), w_in.astype(BF16), b_in.reshape(1, 3 * d),
      conv_w, conv_b.reshape(1, 3 * d))


def _hyena_filter_taps(L, d, f_w1, f_b1, f_w2, f_b2, f_w3, f_b3, f_freq, decay):
    t = jnp.linspace(0.0, 1.0, L, dtype=F32)[:, None]
    w_ang = 2.0 * math.pi * jnp.arange(L, dtype=F32)[:, None] / L
    bands = jnp.linspace(1e-4, HY_BANDS - 1, HY_BANDS, dtype=F32)[None, :]
    z = jnp.concatenate([t, jnp.cos(bands * w_ang), -jnp.sin(bands * w_ang)], axis=-1)
    h = jnp.sin(f_freq[0] * (z @ f_w1 + f_b1))
    h = jnp.sin(f_freq[1] * (h @ f_w2 + f_b2))
    h = h @ f_w3 + f_b3
    h = h.reshape(L, 2, d) * jnp.exp(-t[:, :, None] * jnp.abs(decay)[None])
    k = jnp.concatenate([h[:, 0], jnp.zeros((1, d), F32), h[1:, 1][::-1]], axis=0)
    return k / jnp.sum(jnp.abs(k), axis=0, keepdims=True)


def _fft_dims(m):
    lg = m.bit_length() - 1
    p = 1 << ((lg + 1) // 2)
    return p, m // p


def _split_bf16(x):
    hi = x.astype(BF16)
    return hi, (x - hi.astype(F32)).astype(BF16)


def _mm_split(fh, fl, x, precise):
    if not precise:
        return jnp.dot(fh, x.astype(BF16), preferred_element_type=F32)
    xh, xl = _split_bf16(x)
    return (jnp.dot(fh, xh, preferred_element_type=F32) + jnp.dot(fl, xh, preferred_element_type=F32)
            + jnp.dot(fh, xl, preferred_element_type=F32))


def _cplx_as_real(cr, ci):
    top = jnp.concatenate([cr, -ci], axis=-1)
    bot = jnp.concatenate([ci, cr], axis=-1)
    return jnp.concatenate([top, bot], axis=-2)


def _unit_circle(idx, m):
    ang = (2.0 * math.pi / m) * idx.astype(F32)
    return jnp.cos(ang), jnp.sin(ang)


def _dft_consts(p, q):
    m = p * q
    k1 = jnp.arange(p, dtype=jnp.int32)
    n1 = jnp.arange(p // 2, dtype=jnp.int32)
    c, s = _unit_circle((k1[:, None] * n1[None, :]) % p, p)
    fa = _cplx_as_real(c, -s)
    c, s = _unit_circle((n1[:, None] * k1[None, :]) % p, p)
    fd = _cplx_as_real(c / m, s / m)
    k2 = jnp.arange(q, dtype=jnp.int32)
    n2 = jnp.arange(q, dtype=jnp.int32)
    idx = (n2[None, None, :] * (k2[None, :, None] * p + k1[:, None, None])) % m
    c, s = _unit_circle(idx, m)
    gb = _cplx_as_real(c, -s)
    ct, st = jnp.swapaxes(c, 1, 2), jnp.swapaxes(s, 1, 2)
    gc = _cplx_as_real(ct, st)
    return tuple(_split_bf16(a) for a in (fa, gb, gc, fd))


def _fft_a_kernel(x_ref, fh_ref, fl_ref, o_ref, *, precise):
    _, rows_in, group, d = x_ref.shape
    rows_out = o_ref.shape[1]
    for j in range(group):
        x = x_ref[:, :, j, :].reshape(2 * rows_in, d)
        y = _mm_split(fh_ref[...], fl_ref[...], x, precise)
        o_ref[:, :, j, :] = y.reshape(2, rows_out, d)


def _fft_b_kernel(a_ref, gh_ref, gl_ref, o_ref):
    _, q, d = a_ref.shape
    x = _mm_split(gh_ref[...], gl_ref[...], a_ref[...].reshape(2 * q, d), True)
    o_ref[...] = x.reshape(o_ref.shape)


def _fft_bc_kernel(a_ref, gb_ref, gc_ref, k_ref, z_ref):
    _, q, d = a_ref.shape
    x = _mm_split(gb_ref[...], None, a_ref[...].reshape(2 * q, d), False)
    xr, xi = x[:q], x[q:]
    kr, ki = k_ref[0], k_ref[1]
    y = jnp.concatenate([xr * kr - xi * ki, xr * ki + xi * kr], axis=0)
    z = _mm_split(gc_ref[...], None, y, False)
    z_ref[...] = z.reshape(z_ref.shape)


def _fft_rows(x5, f, rows_out, precise):
    pairs, _, rows_in, q, d = x5.shape
    blk = lambda r: pl.BlockSpec((None, 2, r, HALO_ROWS, d), lambda b, j: (b, 0, 0, j, 0))
    return pl.pallas_call(
        functools.partial(_fft_a_kernel, precise=precise),
        out_shape=jax.ShapeDtypeStruct((pairs, 2, rows_out, q, d), F32),
        grid=(pairs, q // HALO_ROWS),
        in_specs=[blk(rows_in), _const_spec(f[0].shape), _const_spec(f[1].shape)],
        out_specs=blk(rows_out),
        compiler_params=_params("parallel", "parallel"),
    )(x5, *f)


def _long_conv(vx, taps, batch, seq_len):
    n, d = vx.shape
    m = 2 * seq_len
    p, q = _fft_dims(m)
    assert batch % 2 == 0
    pairs = batch // 2
    fa, gb, gc, fd = _dft_consts(p, q)
    g_spec = pl.BlockSpec((None, 2 * q, 2 * q), lambda k1, b: (k1, 0, 0))
    slab = pl.BlockSpec((None, 2, q, d), lambda k1, b: (b, 0, k1, 0))

    zeros = jnp.zeros((seq_len, d), F32)
    kin = jnp.stack([taps[:seq_len], zeros, taps[seq_len:], zeros]).reshape(2, 2, p // 2, q, d)
    ka = _fft_rows(kin, fa, p, True).reshape(2, 2, p * q, d)
    kx = pl.pallas_call(
        _fft_b_kernel,
        out_shape=jax.ShapeDtypeStruct((2, 2, p * q, d), F32),
        grid=(p, 2),
        in_specs=[slab, g_spec, g_spec],
        out_specs=slab,
        compiler_params=_params("parallel", "parallel"),
    )(ka, *gb)
    sign = jnp.repeat(1.0 - 2.0 * (jnp.arange(p) % 2).astype(F32), q)[None, :, None]
    kspec = kx[0] + sign * kx[1]

    xa = _fft_rows(vx.reshape(pairs, 2, p // 2, q, d), fa, p, False).reshape(pairs, 2, p * q, d)
    z = pl.pallas_call(
        _fft_bc_kernel,
        out_shape=jax.ShapeDtypeStruct((pairs, 2, p * q, d), F32),
        grid=(p, pairs),
        in_specs=[slab, g_spec, g_spec, pl.BlockSpec((2, q, d), lambda k1, b: (0, k1, 0))],
        out_specs=slab,
        compiler_params=_params("parallel", "parallel"),
    )(xa, gb[0], gc[0], kspec)
    y = _fft_rows(z.reshape(pairs, 2, p, q, d), fd, p // 2, False)
    return y.reshape(n, d)


def _hyena_out_prologue(y, vx, x0, skip):
    return (y + vx * skip) * x0


def _hyena_layer(x, batch, seq_len, g, p):
    n, d = x.shape
    x0, vx = _hyena_in(x, g, p['w_in'], p['b_in'], p['conv_w'], p['conv_b'], seq_len)
    taps = _hyena_filter_taps(seq_len, d, p['f_w1'], p['f_b1'], p['f_w2'], p['f_b2'],
                              p['f_w3'], p['f_b3'], p['f_freq'], p['decay'])
    y = _long_conv(vx, taps, batch, seq_len)
    return _mm_res(_hyena_out_prologue, [y, vx, x0], [p['skip'].reshape(1, d)],
                   p['w_out'], p['b_out'], x)


def _head_norm_rope(xh, gain, cos, sin, half):
    y = _rms(xh, gain)
    lane = lax.broadcasted_iota(jnp.int32, (1, HEAD_DIM), 1)
    fwd = pltpu.roll(y, HEAD_DIM - half, 1)
    bwd = pltpu.roll(y, half, 1)
    partner = jnp.where((lane % (2 * half)) < half, fwd, bwd)
    return y * cos + partner * sin


def _qkv_rope_kernel(x_ref, g_ref, w_ref, qg_ref, kg_ref, cos_ref, sin_ref,
                     q_ref, k_ref, v_ref, *, nq, nk, nv, half):
    xb = _rms(x_ref[...], g_ref[...]).astype(BF16)
    cos = cos_ref[...]
    sin = sin_ref[...]
    per = MXU_DIM // HEAD_DIM
    for h0 in range(0, nq + nk + nv, per):
        pw = jnp.dot(xb, w_ref[:, h0 * HEAD_DIM:(h0 + per) * HEAD_DIM], preferred_element_type=F32)
        for h in range(h0, h0 + per):
            ph = pw[:, (h - h0) * HEAD_DIM:(h - h0 + 1) * HEAD_DIM]
            if h < nq:
                sl = slice(h * HEAD_DIM, (h + 1) * HEAD_DIM)
                q_ref[:, sl] = _head_norm_rope(ph, qg_ref[...], cos, sin, half).astype(BF16)
            elif h < nq + nk:
                sl = slice((h - nq) * HEAD_DIM, (h - nq + 1) * HEAD_DIM)
                k_ref[:, sl] = _head_norm_rope(ph, kg_ref[...], cos, sin, half).astype(BF16)
            else:
                sl = slice((h - nq - nk) * HEAD_DIM, (h - nq - nk + 1) * HEAD_DIM)
                v_ref[:, sl] = ph.astype(BF16)


def _qkv_rope(x, g, w_qkv, q_gain, k_gain, cos, sin, nq, nk, nv, half, seq_len, tm=512):
    n, d = x.shape
    tm = min(tm, seq_len)
    pos_blocks = seq_len // tm
    f = w_qkv.shape[1]
    row = pl.BlockSpec((tm, d), lambda i: (i, 0))
    tab = pl.BlockSpec((tm, HEAD_DIM), lambda i: (i % pos_blocks, 0))
    outs = tuple(jax.ShapeDtypeStruct((n, c * HEAD_DIM), BF16) for c in (nq, nk, nv))
    return pl.pallas_call(
        functools.partial(_qkv_rope_kernel, nq=nq, nk=nk, nv=nv, half=half),
        out_shape=outs,
        grid=(n // tm,),
        in_specs=[row, _const_spec((1, d)), _const_spec((d, f)),
                  _const_spec((1, HEAD_DIM)), _const_spec((1, HEAD_DIM)), tab, tab],
        out_specs=tuple(pl.BlockSpec((tm, c * HEAD_DIM), lambda i: (i, 0)) for c in (nq, nk, nv)),
        compiler_params=_params("parallel"),
    )(x, g.reshape(1, d), w_qkv.astype(BF16), q_gain.reshape(1, HEAD_DIM),
      k_gain.reshape(1, HEAD_DIM), cos, sin)


def _axial_tables(L):
    t = jnp.arange(L)
    r = (t // GRID_W).astype(F32)
    c = (t % GRID_W).astype(F32)
    nf = HEAD_DIM // 4
    inv = AXIAL_THETA ** (-(2.0 * jnp.arange(nf, dtype=F32)) / (2 * nf))
    ar, ac = r[:, None] * inv[None], c[:, None] * inv[None]
    cos = jnp.concatenate([jnp.cos(ar), jnp.cos(ar), jnp.cos(ac), jnp.cos(ac)], axis=-1)
    sin = jnp.concatenate([-jnp.sin(ar), jnp.sin(ar), -jnp.sin(ac), jnp.sin(ac)], axis=-1)
    return cos, sin


FLASH_SAFE_BOUND = 40.0
FLASH_BOUND_MARGIN = 1.001


def _flash_kernel(q_ref, k_ref, v_ref, o_ref, acc_scr, off_scr, lsum_scr, m_scr, l_scr, kn_scr,
                  *, tq, tk, tkf, seq_len, group):
    scale = HEAD_DIM ** -0.5
    c = scale * math.log2(math.e)

    @pl.when(pl.program_id(2) == 0)
    def _():
        def norm_step(j, mx):
            kt = k_ref[pl.ds(pl.multiple_of(j * tk, tk), tk), :].astype(F32)
            row = jnp.sum(kt * kt, axis=1, keepdims=True)
            return jnp.maximum(mx, jnp.max(row, axis=0, keepdims=True))

        kmax2 = lax.fori_loop(0, seq_len // tk, norm_step, jnp.zeros((1, 1), F32))
        kn_scr[...] = jnp.broadcast_to(kmax2, kn_scr.shape)

    kmax2 = kn_scr[0:1, 0:1]
    bmax = jnp.zeros((1, 1), F32)
    for h in range(group):
        qf = q_ref[:, h * HEAD_DIM:(h + 1) * HEAD_DIM].astype(F32)
        b = jnp.sqrt(jnp.sum(qf * qf, axis=1, keepdims=True) * kmax2) * FLASH_BOUND_MARGIN
        bmax = jnp.maximum(bmax, jnp.max(b, axis=0, keepdims=True))
        off_scr[h] = jnp.broadcast_to(b * c, (tq, HEAD_DIM))
    fast = (bmax * scale)[0, 0] <= FLASH_SAFE_BOUND

    @pl.when(fast)
    def _():
        acc_scr[...] = jnp.zeros_like(acc_scr)
        lsum_scr[...] = jnp.zeros_like(lsum_scr)
        n_lane_tiles = tkf // HEAD_DIM

        def body(j, carry):
            start = pl.multiple_of(j * tkf, tkf)
            kt = k_ref[pl.ds(start, tkf), :]
            vt = v_ref[pl.ds(start, tkf), :]
            for h in range(group):
                q = q_ref[:, h * HEAD_DIM:(h + 1) * HEAD_DIM]
                s = lax.dot_general(q, kt, (((1,), (1,)), ((), ())), preferred_element_type=F32)
                off = off_scr[h]
                p = jnp.exp2(s * c - jnp.concatenate([off] * n_lane_tiles, axis=1))
                part = p[:, 0:HEAD_DIM]
                for t in range(1, n_lane_tiles):
                    part = part + p[:, t * HEAD_DIM:(t + 1) * HEAD_DIM]
                lsum_scr[h] += part
                acc_scr[h] += jnp.dot(p.astype(BF16), vt, preferred_element_type=F32)
            return carry

        lax.fori_loop(0, seq_len // tkf, body, 0)
        for h in range(group):
            l = jnp.sum(lsum_scr[h], axis=1, keepdims=True)
            o_ref[:, h * HEAD_DIM:(h + 1) * HEAD_DIM] = (acc_scr[h] / l).astype(o_ref.dtype)

    @pl.when(jnp.logical_not(fast))
    def _():
        m_scr[...] = jnp.full(m_scr.shape, -jnp.inf, F32)
        l_scr[...] = jnp.zeros_like(l_scr)
        acc_scr[...] = jnp.zeros_like(acc_scr)

        def body(j, carry):
            start = pl.multiple_of(j * tk, tk)
            kt = k_ref[pl.ds(start, tk), :]
            vt = v_ref[pl.ds(start, tk), :]
            for h in range(group):
                q = q_ref[:, h * HEAD_DIM:(h + 1) * HEAD_DIM]
                s = lax.dot_general(q, kt, (((1,), (1,)), ((), ())), preferred_element_type=F32)
                m = m_scr[h]
                m_new = jnp.maximum(m, jnp.max(s, axis=1, keepdims=True))
                alpha = jnp.exp2((m - m_new) * c)
                p = jnp.exp2(s * c - m_new * c)
                l_scr[h] = alpha * l_scr[h] + jnp.sum(p, axis=1, keepdims=True)
                acc_scr[h] = alpha * acc_scr[h] + jnp.dot(p.astype(BF16), vt, preferred_element_type=F32)
                m_scr[h] = m_new
            return carry

        lax.fori_loop(0, seq_len // tk, body, 0)
        for h in range(group):
            o_ref[:, h * HEAD_DIM:(h + 1) * HEAD_DIM] = (acc_scr[h] / l_scr[h]).astype(o_ref.dtype)


def _flash_gqa(q, k, v, batch, seq_len, tq=1024, tk=512, tkf=1024):
    tq = min(tq, seq_len)
    tk = min(tk, seq_len)
    tkf = min(tkf, seq_len)
    nq = seq_len // tq
    gw = GA_GROUP * HEAD_DIM
    wide = pltpu.VMEM((GA_GROUP, tq, HEAD_DIM), F32)
    thin = pltpu.VMEM((GA_GROUP, tq, 1), F32)
    return pl.pallas_call(
        functools.partial(_flash_kernel, tq=tq, tk=tk, tkf=tkf, seq_len=seq_len, group=GA_GROUP),
        out_shape=jax.ShapeDtypeStruct(q.shape, BF16),
        grid=(batch, GA_KV_HEADS, nq),
        in_specs=[pl.BlockSpec((tq, gw), lambda b, kv, i: (b * nq + i, kv)),
                  pl.BlockSpec((seq_len, HEAD_DIM), lambda b, kv, i: (b, kv)),
                  pl.BlockSpec((seq_len, HEAD_DIM), lambda b, kv, i: (b, kv))],
        out_specs=pl.BlockSpec((tq, gw), lambda b, kv, i: (b * nq + i, kv)),
        scratch_shapes=[wide, wide, wide, thin, thin, pltpu.VMEM((HALO_ROWS, HEAD_DIM), F32)],
        compiler_params=_params("parallel", "parallel", "arbitrary"),
    )(q, k, v)


def _identity_prologue(o):
    return o


def _gqa_layer(x, batch, seq_len, g, p):
    cos, sin = _axial_tables(seq_len)
    q, k, v = _qkv_rope(x, g, p['w_qkv'], p['q_gain'], p['k_gain'], cos, sin,
                        GA_HEADS, GA_KV_HEADS, GA_KV_HEADS, HEAD_DIM // 4, seq_len)
    o = _flash_gqa(q, k, v, batch, seq_len)
    return _mm_res(_identity_prologue, [o], [], p['w_o'], None, x)


def _ml_in_kernel(x_ref, xp_ref, xn_ref, g_ref, w_ref, cw_ref, cb_ref, wq_ref, wk_ref, wv_ref,
                  wg_ref, bg_ref, q_ref, k_ref, v_ref, xc_ref, sz_ref, gate_ref,
                  *, seq_len, tm, inner, k_scale):
    keep_prev, keep_next = _edge_scales(pl.program_id(0), tm, seq_len)
    g = g_ref[...]
    xb = _rms(x_ref[...], g).astype(BF16)
    hb = _rms(jnp.concatenate([xp_ref[...], xn_ref[...]], axis=0), g).astype(BF16)
    rows = lax.broadcasted_iota(jnp.int32, (tm, 1), 0)
    gacc = jnp.zeros(gate_ref.shape, F32)
    for t in range(inner // MXU_DIM):
        sl = slice(t * MXU_DIM, (t + 1) * MXU_DIM)
        w = w_ref[:, sl]
        xm = jnp.dot(xb, w, preferred_element_type=F32)
        xh = jnp.dot(hb, w, preferred_element_type=F32)
        pprev = xh[HALO_ROWS - 1:HALO_ROWS, :] * keep_prev
        pnext = xh[HALO_ROWS:HALO_ROWS + 1, :] * keep_next
        xc = _conv3_rows(xm, pprev, pnext, cw_ref[:, sl], cb_ref[:, sl], rows, tm)
        xc = xc * jax.nn.sigmoid(xc)
        z = jnp.dot(xb, w_ref[:, inner + t * MXU_DIM:inner + (t + 1) * MXU_DIM],
                    preferred_element_type=F32)
        xcb = xc.astype(BF16)
        q = jnp.dot(xcb, wq_ref[t], preferred_element_type=F32)
        k = jnp.dot(xcb, wk_ref[t], preferred_element_type=F32)
        v = jnp.dot(xm.astype(BF16), wv_ref[t], preferred_element_type=F32)
        qb, kb, vb = q.astype(BF16), k.astype(BF16), v.astype(BF16)
        gacc += (jnp.dot(qb, wg_ref[0, sl, :], preferred_element_type=F32)
                 + jnp.dot(kb, wg_ref[1, sl, :], preferred_element_type=F32)
                 + jnp.dot(vb, wg_ref[2, sl, :], preferred_element_type=F32))
        q_ref[:, sl] = qb
        k_ref[:, sl] = (k * k_scale).astype(BF16)
        v_ref[:, sl] = vb
        xc_ref[:, sl] = xc
        sz_ref[:, sl] = z * jax.nn.sigmoid(z)
    gate_ref[...] = gacc + bg_ref[...]


def _block_diag_tiles(w):
    nb, c, _ = w.shape
    per = MXU_DIM // c
    wt = w.reshape(nb // per, per, c, c)
    eye = jnp.eye(per, dtype=w.dtype)
    full = jnp.einsum('tpcd,pq->tpcqd', wt, eye)
    return full.reshape(nb // per, MXU_DIM, MXU_DIM)


def _ml_in(x, g, p, seq_len, tm=256):
    n, d = x.shape
    tm = min(tm, seq_len)
    inner = p['w_up'].shape[1] // 2
    ng = 4 * ML_HEADS
    dh = inner // ML_HEADS
    wq, wk, wv = (_block_diag_tiles(p[nm]).astype(BF16) for nm in ('w_q', 'w_k', 'w_v'))
    wg = jnp.transpose(p['w_gate'], (1, 2, 0, 3)).reshape(3, inner, ng).astype(BF16)
    bg = p['b_gate'].reshape(1, ng)
    prev, nxt = _halo_specs(tm, d, n)
    row = pl.BlockSpec((tm, d), lambda i: (i, 0))
    wide = pl.BlockSpec((tm, inner), lambda i: (i, 0))
    nt = inner // MXU_DIM
    return pl.pallas_call(
        functools.partial(_ml_in_kernel, seq_len=seq_len, tm=tm, inner=inner, k_scale=dh ** -0.5),
        out_shape=(jax.ShapeDtypeStruct((n, inner), BF16),) * 3
        + (jax.ShapeDtypeStruct((n, inner), F32),) * 2
        + (jax.ShapeDtypeStruct((n, ng), F32),),
        grid=(n // tm,),
        in_specs=[row, prev, nxt, _const_spec((1, d)), _const_spec((d, 2 * inner)),
                  _const_spec((3, inner)), _const_spec((1, inner)),
                  _const_spec((nt, MXU_DIM, MXU_DIM)), _const_spec((nt, MXU_DIM, MXU_DIM)),
                  _const_spec((nt, MXU_DIM, MXU_DIM)), _const_spec((3, inner, ng)),
                  _const_spec((1, ng))],
        out_specs=(wide,) * 5 + (pl.BlockSpec((tm, ng), lambda i: (i, 0)),),
        compiler_params=_params("parallel"),
    )(x, x, x, g.reshape(1, d), p['w_up'].astype(BF16), p['conv_w'],
      p['conv_b'].reshape(1, inner), wq, wk, wv, wg, bg)


def _log_sigmoid(x):
    return jnp.minimum(x, 0.0) - jnp.log1p(jnp.exp(-jnp.abs(x)))


def _mlstm_chunk_kernel(qf_ref, kf_ref, vf_ref, gcf_ref, grf_ref, qb_ref, kb_ref, vb_ref, gcb_ref, grb_ref,
                        hf_ref, hb_ref, c_scr, n_scr, m_scr, *, lc):
    @pl.when(pl.program_id(2) == 0)
    def _():
        c_scr[...] = jnp.zeros_like(c_scr)
        n_scr[...] = jnp.zeros_like(n_scr)
        m_scr[...] = jnp.zeros_like(m_scr)

    _mlstm_step(qf_ref, kf_ref, vf_ref, gcf_ref, grf_ref, hf_ref, c_scr.at[0], n_scr.at[0], m_scr.at[0], lc, False)
    _mlstm_step(qb_ref, kb_ref, vb_ref, gcb_ref, grb_ref, hb_ref, c_scr.at[1], n_scr.at[1], m_scr.at[1], lc, True)


def _mlstm_step(q_ref, k_ref, v_ref, gc_ref, gr_ref, h_ref, c_scr, n_scr, m_scr, lc, backward):
    q = q_ref[...]
    k = k_ref[...]
    v = v_ref[...]
    gc = gc_ref[...]
    gr = gr_ref[...]
    i_col, g_col = gc[:, 0:1], gc[:, 1:2]
    i_row, g_row = gr[0:1, :], gr[1:2, :]

    jr = lax.broadcasted_iota(jnp.int32, (lc, lc), 0)
    sc = lax.broadcasted_iota(jnp.int32, (lc, lc), 1)
    seen = (sc >= jr) if backward else (sc <= jr)
    g_tot = g_row[:, 0:1] if backward else g_row[:, lc - 1:lc]
    m_old = m_scr[0:1, 0:1]

    dmat = jnp.where(seen, g_col - g_row + i_row, -jnp.inf)
    inter = g_col + m_old
    m_q = jnp.maximum(inter, jnp.max(dmat, axis=1, keepdims=True))
    s_qk = lax.dot_general(q, k, (((1,), (1,)), ((), ())), preferred_element_type=F32)
    a = s_qk * jnp.exp(dmat - m_q)
    w_int = jnp.exp(inter - m_q)
    q_c = jnp.dot(q, c_scr[...].astype(BF16), preferred_element_type=F32)
    num = jnp.dot(a.astype(BF16), v, preferred_element_type=F32) + q_c * w_int
    q_n = jnp.sum(q.astype(F32) * n_scr[...], axis=1, keepdims=True)
    den = jnp.sum(a, axis=1, keepdims=True) + w_int * q_n
    den = jnp.maximum(jnp.abs(den), jnp.exp(-m_q))
    h_ref[...] = num / den

    a_row = g_tot - g_row + i_row
    m_new = jnp.maximum(g_tot + m_old, jnp.max(a_row, axis=1, keepdims=True))
    ws_col = jnp.exp(g_tot - g_col + i_col - m_new)
    dec = jnp.exp(g_tot + m_old - m_new)
    kw = k.astype(F32) * ws_col
    upd = lax.dot_general(kw.astype(BF16), v, (((0,), (0,)), ((), ())), preferred_element_type=F32)
    c_scr[...] = dec * c_scr[...] + upd
    n_scr[...] = dec * n_scr[...] + jnp.sum(kw, axis=0, keepdims=True)
    m_scr[...] = jnp.broadcast_to(m_new, m_scr.shape)


def _ml_gate_kernel(g_ref, o_ref, *, lc, nh):
    g = g_ref[...]
    jr = lax.broadcasted_iota(jnp.int32, (lc, lc), 0)
    sc = lax.broadcasted_iota(jnp.int32, (lc, lc), 1)
    tril = jnp.where(sc <= jr, 1.0, 0.0).astype(BF16)
    triu = jnp.where(sc >= jr, 1.0, 0.0).astype(BF16)
    col = lax.broadcasted_iota(jnp.int32, (1, 4 * nh), 1)
    lf = _log_sigmoid(g)
    x1 = lf.astype(BF16)
    r1 = lf - x1.astype(F32)
    x2 = r1.astype(BF16)
    x3 = (r1 - x2.astype(F32)).astype(BF16)
    cum = lambda m: (jnp.dot(m, x1, preferred_element_type=F32) + jnp.dot(m, x2, preferred_element_type=F32)
                     + jnp.dot(m, x3, preferred_element_type=F32))
    gsum = jnp.where(col < 2 * nh, cum(tril), cum(triu))
    o_ref[...] = jnp.where((col % (2 * nh)) >= nh, gsum, g)


def _mlstm_chunks(q, k, v, gates, batch, seq_len):
    n, inner = q.shape
    dh = inner // ML_HEADS
    lc = min(ML_CHUNK, seq_len)
    nc = seq_len // lc
    ng = gates.shape[1]
    gates = pl.pallas_call(
        functools.partial(_ml_gate_kernel, lc=lc, nh=ML_HEADS),
        out_shape=jax.ShapeDtypeStruct((n, ng), F32),
        grid=(n // lc,),
        in_specs=[pl.BlockSpec((lc, ng), lambda i: (i, 0))],
        out_specs=pl.BlockSpec((lc, ng), lambda i: (i, 0)),
        compiler_params=_params("parallel"),
    )(gates)
    g4 = gates.reshape(n, 2, 2, ML_HEADS)
    gcol = jnp.transpose(g4, (1, 3, 0, 2))
    grow = jnp.transpose(g4, (1, 3, 2, 0))

    def specs(dd):
        chunk = (lambda b, c: b * nc + c) if dd == 0 else (lambda b, c: b * nc + nc - 1 - c)
        qkv = pl.BlockSpec((lc, dh), lambda b, h, c: (chunk(b, c), h))
        gc = pl.BlockSpec((None, None, lc, 2), lambda b, h, c: (dd, h, chunk(b, c), 0))
        gr = pl.BlockSpec((None, None, 2, lc), lambda b, h, c: (dd, h, 0, chunk(b, c)))
        return [qkv, qkv, qkv, gc, gr], qkv

    in_f, out_f = specs(0)
    in_b, out_b = specs(1)
    return pl.pallas_call(
        functools.partial(_mlstm_chunk_kernel, lc=lc),
        out_shape=(jax.ShapeDtypeStruct((n, inner), F32),) * 2,
        grid=(batch, ML_HEADS, nc),
        in_specs=in_f + in_b,
        out_specs=(out_f, out_b),
        scratch_shapes=[pltpu.VMEM((2, dh, dh), F32), pltpu.VMEM((2, 1, dh), F32),
                        pltpu.VMEM((2, HALO_ROWS, HEAD_DIM), F32)],
        compiler_params=_params("parallel", "parallel", "arbitrary"),
    )(q, k, v, gcol, grow, q, k, v, gcol, grow)


def _ml_out_prologue(hf, hb, xc, sz, gain, skip):
    h = hf + hb
    dh = h.shape[1] // ML_HEADS
    parts = []
    for i in range(ML_HEADS):
        sl = slice(i * dh, (i + 1) * dh)
        parts.append(_rms(h[:, sl], gain[:, sl]))
    hn = jnp.concatenate(parts, axis=1)
    return (hn + skip * xc) * sz


def _mlstm_layer(x, batch, seq_len, g, p):
    q, k, v, xc, sz, gates = _ml_in(x, g, p, seq_len)
    inner = q.shape[1]
    hf, hb = _mlstm_chunks(q, k, v, gates, batch, seq_len)
    return _mm_res(_ml_out_prologue, [hf, hb, xc, sz],
                   [p['norm_gain'].reshape(1, inner), p['skip'].reshape(1, inner)],
                   p['w_down'], None, x, tm=256)


def _rope_tables(L):
    inv = ROPE_THETA ** (-(2.0 * jnp.arange(ROPE_DIMS // 2, dtype=F32)) / ROPE_DIMS)
    ang = jnp.arange(L, dtype=F32)[:, None] * inv[None]
    pad = HEAD_DIM - ROPE_DIMS
    cos = jnp.concatenate([jnp.cos(ang), jnp.cos(ang), jnp.ones((L, pad), F32)], axis=-1)
    sin = jnp.concatenate([-jnp.sin(ang), jnp.sin(ang), jnp.zeros((L, pad), F32)], axis=-1)
    return cos, sin


def _band_kernel(q_ref, kp_ref, kc_ref, kn_ref, vp_ref, vc_ref, vn_ref, o_ref, lse_ref,
                 *, s_len, heads, subs):
    i = pl.program_id(2)
    qb = BAND_BLOCK
    w = qb + 2 * BAND_HALF
    a = lax.broadcasted_iota(jnp.int32, (qb, w), 0)
    c = lax.broadcasted_iota(jnp.int32, (qb, w), 1)
    in_band = jnp.abs(c - BAND_HALF - a) <= BAND_HALF
    scale = HEAD_DIM ** -0.5
    for h in range(heads):
        sl = slice(h * HEAD_DIM, (h + 1) * HEAD_DIM)
        kw = jnp.concatenate([kp_ref[qb - BAND_HALF:, sl], kc_ref[:, sl], kn_ref[:BAND_HALF, sl]], axis=0)
        vw = jnp.concatenate([vp_ref[qb - BAND_HALF:, sl], vc_ref[:, sl], vn_ref[:BAND_HALF, sl]], axis=0)
        for u in range(subs):
            rows = slice(u * qb, (u + 1) * qb)
            key_pos = (i * subs + u) * qb - BAND_HALF + c
            valid = in_band & (key_pos >= 0) & (key_pos < s_len)
            s = lax.dot_general(q_ref[rows, sl], kw[u * qb:u * qb + w], (((1,), (1,)), ((), ())),
                                preferred_element_type=F32) * scale
            s = jnp.where(valid, s, -jnp.inf)
            m = jnp.max(s, axis=1, keepdims=True)
            p = jnp.exp(s - m)
            l = jnp.sum(p, axis=1, keepdims=True)
            o = jnp.dot(p.astype(BF16), vw[u * qb:u * qb + w], preferred_element_type=F32)
            o_ref[rows, sl] = o / l
            lse_ref[rows, sl] = jnp.broadcast_to(m + jnp.log(l), (qb, HEAD_DIM))


def _da_qkv_kernel(x_ref, g_ref, w_ref, qg_ref, kg_ref, cos_ref, sin_ref, *refs, dils, tm, half):
    n_perm = sum(1 for d in dils if d > 1)
    perm_refs, out_refs = refs[:n_perm], refs[n_perm:]
    xb = _rms(x_ref[...], g_ref[...]).astype(BF16)
    cos, sin = cos_ref[...], sin_ref[...]
    gw = DA_HEADS_PER_GROUP * HEAD_DIM
    heads = DA_HEADS_PER_GROUP * len(dils)
    for kind in range(3):
        gain = (qg_ref, kg_ref, None)[kind]
        pi = 0
        for gi, dil in enumerate(dils):
            c0 = (kind * heads + gi * DA_HEADS_PER_GROUP) * HEAD_DIM
            ph = jnp.dot(xb, w_ref[:, c0:c0 + gw], preferred_element_type=F32)
            if gain is not None:
                ph = jnp.concatenate(
                    [_head_norm_rope(ph[:, h * HEAD_DIM:(h + 1) * HEAD_DIM], gain[...], cos, sin, half)
                     for h in range(DA_HEADS_PER_GROUP)], axis=1)
            val = ph.astype(BF16)
            o_ref = out_refs[kind * len(dils) + gi]
            if dil == 1:
                o_ref[...] = val
            else:
                pv = jnp.dot(perm_refs[pi][...], val, preferred_element_type=F32).astype(BF16)
                pi += 1
                rows = tm // dil
                for r in range(dil):
                    o_ref[:, r * gw:(r + 1) * gw] = pv[r * rows:(r + 1) * rows, :]


def _da_qkv(x, g, w_qkv, q_gain, k_gain, cos, sin, seq_len, tm=256):
    n, d = x.shape
    tm = min(tm, seq_len)
    pos_blocks = seq_len // tm
    dils = tuple(dil for _, dil in DA_GROUPS)
    gw = DA_HEADS_PER_GROUP * HEAD_DIM
    f = w_qkv.shape[1]
    perms = []
    for dil in dils:
        if dil > 1:
            dst = jnp.arange(tm)
            src = (dst % (tm // dil)) * dil + dst // (tm // dil)
            perms.append((src[:, None] == jnp.arange(tm)[None, :]).astype(BF16))
    row = pl.BlockSpec((tm, d), lambda i: (i, 0))
    tab = pl.BlockSpec((tm, HEAD_DIM), lambda i: (i % pos_blocks, 0))
    out_shapes = tuple(jax.ShapeDtypeStruct((n // dil, dil * gw), BF16) for _ in range(3) for dil in dils)
    out_specs = tuple(pl.BlockSpec((tm // dil, dil * gw), lambda i: (i, 0)) for _ in range(3) for dil in dils)
    outs = pl.pallas_call(
        functools.partial(_da_qkv_kernel, dils=dils, tm=tm, half=ROPE_DIMS // 2),
        out_shape=out_shapes,
        grid=(n // tm,),
        in_specs=[row, _const_spec((1, d)), _const_spec((d, f)), _const_spec((1, HEAD_DIM)),
                  _const_spec((1, HEAD_DIM)), tab, tab] + [_const_spec((tm, tm))] * len(perms),
        out_specs=out_specs,
        compiler_params=_params("parallel"),
    )(x, g.reshape(1, d), w_qkv.astype(BF16), q_gain.reshape(1, HEAD_DIM), k_gain.reshape(1, HEAD_DIM),
      cos, sin, *perms)
    ng = len(dils)
    return [(outs[gi], outs[ng + gi], outs[2 * ng + gi]) for gi in range(ng)]


def _band_attention(q, k, v, batch, seq_len, dil):
    n = batch * seq_len
    s_len = seq_len // dil
    nb = s_len // BAND_BLOCK
    subs = min(BAND_SUBBLOCKS, nb)
    hw = DA_HEADS_PER_GROUP * HEAD_DIM
    view = lambda a: a.reshape(batch, s_len, dil * hw)
    qv, kv, vv = view(q), view(k), view(v)

    main = pl.BlockSpec((None, subs * BAND_BLOCK, hw), lambda b, r, i: (b, i, r))
    prev = pl.BlockSpec((None, BAND_BLOCK, hw), lambda b, r, i: (b, jnp.maximum(i * subs - 1, 0), r))
    nxt = pl.BlockSpec((None, BAND_BLOCK, hw), lambda b, r, i: (b, jnp.minimum((i + 1) * subs, nb - 1), r))
    o, lse = pl.pallas_call(
        functools.partial(_band_kernel, s_len=s_len, heads=DA_HEADS_PER_GROUP, subs=subs),
        out_shape=(jax.ShapeDtypeStruct((batch, s_len, dil * hw), F32),) * 2,
        grid=(batch, dil, nb // subs),
        in_specs=[main, prev, main, nxt, prev, main, nxt],
        out_specs=(main, main),
        compiler_params=_params("parallel", "parallel", "parallel"),
    )(qv, kv, kv, kv, vv, vv, vv)
    return o.reshape(n, hw), lse.reshape(n, hw)


def _da_out_prologue(o0, o1, o2, l0, l1, l2):
    m = jnp.maximum(jnp.maximum(l0, l1), l2)
    e0, e1, e2 = jnp.exp(l0 - m), jnp.exp(l1 - m), jnp.exp(l2 - m)
    return (e0 * o0 + e1 * o1 + e2 * o2) / (e0 + e1 + e2)


def _dilated_layer(x, batch, seq_len, g, p):
    cos, sin = _rope_tables(seq_len)
    qkv = _da_qkv(x, g, p['w_qkv'], p['q_gain'], p['k_gain'], cos, sin, seq_len)
    outs, lses = [], []
    for (q, k, v), (_, dil) in zip(qkv, DA_GROUPS):
        o, lse = _band_attention(q, k, v, batch, seq_len, dil)
        outs.append(o)
        lses.append(lse)
    return _mm_res(_da_out_prologue, outs + lses, [], p['w_o'], None, x)


def _router_kernel(x_ref, g_ref, wr_ref, xn_ref, aff_ref):
    xn = _rms(x_ref[...], g_ref[...])
    xn_ref[...] = xn.astype(BF16)
    logits = lax.dot_general(wr_ref[...], xn, (((1,), (1,)), ((), ())),
                             precision=HIGHEST, preferred_element_type=F32)
    m = jnp.max(logits, axis=0, keepdims=True)
    e = jnp.exp(logits - m)
    aff_ref[...] = e / jnp.sum(e, axis=0, keepdims=True)


def _router(x, g, w_router, tm=512):
    n, d = x.shape
    tm = min(tm, n)
    e = w_router.shape[1]
    return pl.pallas_call(
        _router_kernel,
        out_shape=(jax.ShapeDtypeStruct((n, d), BF16), jax.ShapeDtypeStruct((e, n), F32)),
        grid=(n // tm,),
        in_specs=[pl.BlockSpec((tm, d), lambda i: (i, 0)), _const_spec((1, d)), _const_spec((e, d))],
        out_specs=(pl.BlockSpec((tm, d), lambda i: (i, 0)), pl.BlockSpec((e, tm), lambda i: (0, i))),
        compiler_params=_params("parallel"),
    )(x, g.reshape(1, d), w_router.T)


def _expert_ffn_kernel(xe_ref, gate_ref, w1_ref, w3_ref, w2_ref, ye_ref):
    xe = xe_ref[...]
    h1 = jnp.dot(xe, w1_ref[...], preferred_element_type=F32)
    h3 = jnp.dot(xe, w3_ref[...], preferred_element_type=F32)
    hid = (h1 * jax.nn.sigmoid(h1) * h3).astype(BF16)
    ye_ref[...] = (jnp.dot(hid, w2_ref[...], preferred_element_type=F32) * gate_ref[...]).astype(ye_ref.dtype)


def _expert_ffn(xe, gates, w1, w3, w2, tm=512):
    e, c, d = xe.shape
    f = w1.shape[2]
    tm = min(tm, c)
    return pl.pallas_call(
        _expert_ffn_kernel,
        out_shape=jax.ShapeDtypeStruct((e, c, d), BF16),
        grid=(e, c // tm),
        in_specs=[pl.BlockSpec((None, tm, d), lambda ei, ci: (ei, ci, 0)),
                  pl.BlockSpec((None, tm, 1), lambda ei, ci: (ei, ci, 0)),
                  pl.BlockSpec((None, d, f), lambda ei, ci: (ei, 0, 0)),
                  pl.BlockSpec((None, d, f), lambda ei, ci: (ei, 0, 0)),
                  pl.BlockSpec((None, f, d), lambda ei, ci: (ei, 0, 0))],
        out_specs=pl.BlockSpec((None, tm, d), lambda ei, ci: (ei, ci, 0)),
        compiler_params=_params("parallel", "arbitrary"),
    )(xe, gates[..., None], w1, w3, w2)


def _moe_layer(x, group_sizes, g, w_router, w1, w3, w2, split_output):
    n, d = x.shape
    xn, aff_t = _router(x, g, w_router)
    xes, gts, idxs = [], [], []
    start = 0
    for ng in group_sizes:
        cap = EC_CAPACITY * ng // N_EXPERTS
        gates, idx = lax.top_k(aff_t[:, start:start + ng], cap)
        idx = idx + start
        xes.append(xn[idx])
        gts.append(gates)
        idxs.append(idx)
        start += ng
    ye = _expert_ffn(jnp.concatenate(xes, axis=1), jnp.concatenate(gts, axis=1),
                     w1.astype(BF16), w3.astype(BF16), w2.astype(BF16))
    idx_all = jnp.concatenate(idxs, axis=1)
    return _combine(x, ye.reshape(-1, d), idx_all.reshape(-1), group_sizes if split_output else (n,))


COMBINE_TOKENS = 512
COMBINE_ROWS = 512


def _combine_kernel(tile_ref, blk_ref, live_ref, x_ref, tok_ref, ye_ref, *o_refs, split_tiles):
    w = pl.program_id(0)
    tile = tile_ref[w]
    first = jnp.logical_or(w == 0, tile != tile_ref[jnp.maximum(w - 1, 0)])
    tt = x_ref.shape[0]
    rows = lax.broadcasted_iota(jnp.int32, (tt, COMBINE_ROWS), 0)
    onehot = jnp.where(rows == tok_ref[...] - tile * tt, 1.0, 0.0).astype(BF16)
    add = jnp.dot(onehot, ye_ref[...], preferred_element_type=F32) * live_ref[w].astype(F32)
    lo_tile = 0
    for o_ref, n_tiles in zip(o_refs, split_tiles):
        mine = jnp.logical_and(tile >= lo_tile, tile < lo_tile + n_tiles)

        @pl.when(jnp.logical_and(mine, first))
        def _(o_ref=o_ref):
            o_ref[...] = x_ref[...] + add

        @pl.when(jnp.logical_and(mine, jnp.logical_not(first)))
        def _(o_ref=o_ref):
            o_ref[...] += add

        lo_tile += n_tiles


def _combine(x, ye, tok, splits):
    n, d = x.shape
    p = tok.shape[0]
    tt = min(COMBINE_TOKENS, min(splits))
    rb = COMBINE_ROWS
    assert p % rb == 0 and all(s % tt == 0 for s in splits) and sum(splits) == n
    tiles, nblk = n // tt, p // rb
    split_tiles = tuple(s // tt for s in splits)
    order = jnp.argsort(tok)
    tok_sorted = tok[order].astype(jnp.int32)
    ye_sorted = ye[order]

    edges = jnp.arange(tiles + 1, dtype=jnp.int32) * tt
    bounds = jnp.searchsorted(tok_sorted, edges, method='compare_all').astype(jnp.int32)
    lo, hi = bounds[:-1], bounds[1:]
    first_blk = jnp.minimum(lo // rb, nblk - 1)
    last_blk = jnp.where(hi > lo, (hi - 1) // rb, first_blk)
    n_items = last_blk - first_blk + 1
    item_end = jnp.cumsum(n_items)
    item_start = item_end - n_items
    max_items = nblk + 2 * tiles
    w = jnp.arange(max_items, dtype=jnp.int32)
    tile = jnp.minimum(jnp.searchsorted(item_end, w, side='right', method='compare_all'),
                       tiles - 1).astype(jnp.int32)
    k = w - item_start[tile]
    live = (k < n_items[tile]).astype(jnp.int32)
    blk = jnp.minimum(first_blk[tile] + k, nblk - 1).astype(jnp.int32)

    def out_spec(lo_tile, n_tiles):
        return pl.BlockSpec(
            (tt, d), lambda i, tile_r, blk_r, live_r: (jnp.clip(tile_r[i] - lo_tile, 0, n_tiles - 1), 0))

    starts = [sum(split_tiles[:j]) for j in range(len(splits))]
    grid_spec = pltpu.PrefetchScalarGridSpec(
        num_scalar_prefetch=3,
        grid=(max_items,),
        in_specs=[pl.BlockSpec((tt, d), lambda i, tile_r, blk_r, live_r: (tile_r[i], 0)),
                  pl.BlockSpec((None, 1, rb), lambda i, tile_r, blk_r, live_r: (blk_r[i], 0, 0)),
                  pl.BlockSpec((rb, d), lambda i, tile_r, blk_r, live_r: (blk_r[i], 0))],
        out_specs=tuple(out_spec(s, t) for s, t in zip(starts, split_tiles)),
    )
    return pl.pallas_call(
        functools.partial(_combine_kernel, split_tiles=split_tiles),
        out_shape=tuple(jax.ShapeDtypeStruct((s, d), F32) for s in splits),
        grid_spec=grid_spec,
        compiler_params=_params("arbitrary"),
    )(tile, blk, live, x, tok_sorted.reshape(nblk, 1, rb), ye_sorted)


def _trunk(x, batch, seq_len, group_sizes, p):
    depth = p['norm_gain'].shape[0]
    for i in range(depth):
        mixer, j = i % 4, i // 4
        g = p['norm_gain'][i, 0]
        if mixer == 0:
            x = _hyena_layer(x, batch, seq_len, g, {k[3:]: v[j] for k, v in p.items() if k.startswith('hy_')})
        elif mixer == 1:
            x = _gqa_layer(x, batch, seq_len, g, {k[3:]: v[j] for k, v in p.items() if k.startswith('ga_')})
        elif mixer == 2:
            x = _mlstm_layer(x, batch, seq_len, g, {k[3:]: v[j] for k, v in p.items() if k.startswith('ml_')})
        else:
            x = _dilated_layer(x, batch, seq_len, g, {k[3:]: v[j] for k, v in p.items() if k.startswith('da_')})
        last = i == depth - 1
        outs = _moe_layer(x, group_sizes, p['norm_gain'][i, 1], p['moe_w_router'][i],
                          p['moe_w1'][i], p['moe_w3'][i], p['moe_w2'][i], split_output=last)
        x = outs if last else outs[0]
    return x


def kernel(x_prompt, x_sample, norm_gain, hy_w_in, hy_b_in, hy_conv_w, hy_conv_b, hy_f_w1, hy_f_b1, hy_f_w2, hy_f_b2, hy_f_w3, hy_f_b3, hy_f_freq, hy_decay, hy_skip, hy_w_out, hy_b_out, ga_w_qkv, ga_q_gain, ga_k_gain, ga_w_o, ml_w_up, ml_conv_w, ml_conv_b, ml_w_q, ml_w_k, ml_w_v, ml_w_gate, ml_b_gate, ml_norm_gain, ml_skip, ml_w_down, da_w_qkv, da_q_gain, da_k_gain, da_w_o, moe_w_router, moe_w1, moe_w3, moe_w2):
    p = dict(
        norm_gain=norm_gain,
        hy_w_in=hy_w_in, hy_b_in=hy_b_in, hy_conv_w=hy_conv_w, hy_conv_b=hy_conv_b,
        hy_f_w1=hy_f_w1, hy_f_b1=hy_f_b1, hy_f_w2=hy_f_w2, hy_f_b2=hy_f_b2,
        hy_f_w3=hy_f_w3, hy_f_b3=hy_f_b3, hy_f_freq=hy_f_freq, hy_decay=hy_decay,
        hy_skip=hy_skip, hy_w_out=hy_w_out, hy_b_out=hy_b_out,
        ga_w_qkv=ga_w_qkv, ga_q_gain=ga_q_gain, ga_k_gain=ga_k_gain, ga_w_o=ga_w_o,
        ml_w_up=ml_w_up, ml_conv_w=ml_conv_w, ml_conv_b=ml_conv_b, ml_w_q=ml_w_q,
        ml_w_k=ml_w_k, ml_w_v=ml_w_v, ml_w_gate=ml_w_gate, ml_b_gate=ml_b_gate,
        ml_norm_gain=ml_norm_gain, ml_skip=ml_skip, ml_w_down=ml_w_down,
        da_w_qkv=da_w_qkv, da_q_gain=da_q_gain, da_k_gain=da_k_gain, da_w_o=da_w_o,
        moe_w_router=moe_w_router, moe_w1=moe_w1, moe_w3=moe_w3, moe_w2=moe_w2,
    )
    bp, seq_len, d = x_prompt.shape
    bs = x_sample.shape[0]
    assert x_sample.shape[1] == seq_len
    x = jnp.concatenate([x_prompt, x_sample], axis=0).reshape((bp + bs) * seq_len, d)
    y_prompt, y_sample = _trunk(x, bp + bs, seq_len, (bp * seq_len, bs * seq_len), p)
    return (y_prompt.reshape(bp, seq_len, d), y_sample.reshape(bs, seq_len, d))
```

```python
import functools
import math

import jax
import jax.numpy as jnp
from jax import lax
from jax.experimental import pallas as pl
from jax.experimental.pallas import tpu as pltpu

F32 = jnp.float32
BF16 = jnp.bfloat16
HIGHEST = lax.Precision.HIGHEST

NORM_EPS = 1e-6
GRID_W = 64
HY_BANDS = 16
GA_HEADS = 8
GA_KV_HEADS = 2
GA_GROUP = GA_HEADS // GA_KV_HEADS
HEAD_DIM = 128
AXIAL_THETA = 10000.0
ML_HEADS = 4
ML_QKV_BLOCK = 4
DA_GROUPS = ((128, 1), (512, 4), (2048, 16))
DA_HEADS_PER_GROUP = 4
DA_HEADS = DA_HEADS_PER_GROUP * len(DA_GROUPS)
ROPE_THETA = 500000.0
ROPE_DIMS = HEAD_DIM // 4
N_EXPERTS = 16
EC_CAPACITY = 2

VMEM_LIMIT_BYTES = 52 * 1024 * 1024
HALO_ROWS = 8
MXU_DIM = 256
ML_CHUNK = 256
BAND_BLOCK = 128
BAND_HALF = 64
BAND_SUBBLOCKS = 4


def _params(*sem):
    return pltpu.CompilerParams(dimension_semantics=sem, vmem_limit_bytes=VMEM_LIMIT_BYTES)


def _rms(x, g):
    ms = jnp.mean(x * x, axis=-1, keepdims=True)
    return x * lax.rsqrt(ms + NORM_EPS) * g


def _const_spec(shape):
    nd = len(shape)
    return pl.BlockSpec(shape, lambda *_: (0,) * nd)


def _conv3_rows(p, pprev, pnext, cw, cb, rows, tm):
    up = jnp.where(rows == 0, pprev, pltpu.roll(p, 1, 0))
    dn = jnp.where(rows == tm - 1, pnext, pltpu.roll(p, tm - 1, 0))
    return up * cw[0:1] + p * cw[1:2] + dn * cw[2:3] + cb


def _halo_specs(tm, d, n_rows):
    hb = tm // HALO_ROWS
    last = n_rows // HALO_ROWS - 1
    prev = pl.BlockSpec((HALO_ROWS, d), lambda i: (jnp.maximum(i * hb - 1, 0), 0))
    nxt = pl.BlockSpec((HALO_ROWS, d), lambda i: (jnp.minimum((i + 1) * hb, last), 0))
    return prev, nxt


def _edge_scales(i, tm, seq_len):
    t0 = i * tm
    keep_prev = jnp.where(t0 % seq_len == 0, 0.0, 1.0).astype(F32)
    keep_next = jnp.where((t0 + tm) % seq_len == 0, 0.0, 1.0).astype(F32)
    return keep_prev, keep_next


def _mm_res_kernel(*refs, prologue, n_row, n_const, tn):
    row_refs = refs[:n_row]
    const_refs = refs[n_row:n_row + n_const]
    w_ref, b_ref, res_ref, o_ref = refs[n_row + n_const:]
    lhs = prologue(*[r[...] for r in row_refs], *[c[...] for c in const_refs]).astype(BF16)
    for j in range(o_ref.shape[1] // tn):
        sl = slice(j * tn, (j + 1) * tn)
        o_ref[:, sl] = (res_ref[:, sl] + b_ref[:, sl]
                        + jnp.dot(lhs, w_ref[:, sl], preferred_element_type=F32))


def _mm_res(prologue, rows, consts, w, b, res, tm=512, tn=512):
    n, dout = res.shape
    tm = min(tm, n)
    k = w.shape[0]
    if b is None:
        b = jnp.zeros((1, dout), F32)
    in_specs = [pl.BlockSpec((tm, r.shape[1]), lambda i: (i, 0)) if r.ndim == 2
                else pl.BlockSpec((r.shape[0], tm, r.shape[2]), lambda i: (0, i, 0)) for r in rows]
    in_specs += [_const_spec(c.shape) for c in consts]
    in_specs += [_const_spec((k, dout)), _const_spec((1, dout)),
                 pl.BlockSpec((tm, dout), lambda i: (i, 0))]
    return pl.pallas_call(
        functools.partial(_mm_res_kernel, prologue=prologue, n_row=len(rows),
                          n_const=len(consts), tn=min(tn, dout)),
        out_shape=jax.ShapeDtypeStruct((n, dout), F32),
        grid=(n // tm,),
        in_specs=in_specs,
        out_specs=pl.BlockSpec((tm, dout), lambda i: (i, 0)),
        compiler_params=_params("parallel"),
    )(*rows, *consts, w.astype(BF16), b.reshape(1, dout).astype(F32), res)


def _hyena_in_kernel(x_ref, xp_ref, xn_ref, g_ref, w_ref, b_ref, cw_ref, cb_ref,
                     x0_ref, vx_ref, *, seq_len, tm, d, cols):
    keep_prev, keep_next = _edge_scales(pl.program_id(0), tm, seq_len)
    g = g_ref[...]
    xb = _rms(x_ref[...], g).astype(BF16)
    hb = _rms(jnp.concatenate([xp_ref[...], xn_ref[...]], axis=0), g).astype(BF16)
    rows = lax.broadcasted_iota(jnp.int32, (tm, 1), 0)

    def conv_part(c0):
        sl = slice(c0, c0 + cols)
        w = w_ref[:, sl]
        bias = b_ref[:, sl]
        p = jnp.dot(xb, w, preferred_element_type=F32) + bias
        ph = jnp.dot(hb, w, preferred_element_type=F32) + bias
        pprev = ph[HALO_ROWS - 1:HALO_ROWS, :] * keep_prev
        pnext = ph[HALO_ROWS:HALO_ROWS + 1, :] * keep_next
        return _conv3_rows(p, pprev, pnext, cw_ref[:, sl], cb_ref[:, sl], rows, tm)

    for j in range(d // cols):
        c = j * cols
        x0_ref[:, c:c + cols] = conv_part(c)
        vx_ref[:, c:c + cols] = conv_part(2 * d + c) * conv_part(d + c)


def _hyena_in(x, g, w_in, b_in, conv_w, conv_b, seq_len, tm=256, cols=512):
    n, d = x.shape
    tm = min(tm, seq_len)
    prev, nxt = _halo_specs(tm, d, n)
    row = pl.BlockSpec((tm, d), lambda i: (i, 0))
    return pl.pallas_call(
        functools.partial(_hyena_in_kernel, seq_len=seq_len, tm=tm, d=d, cols=cols),
        out_shape=(jax.ShapeDtypeStruct((n, d), F32), jax.ShapeDtypeStruct((n, d), F32)),
        grid=(n // tm,),
        in_specs=[row, prev, nxt, _const_spec((1, d)), _const_spec((d, 3 * d)),
                  _const_spec((1, 3 * d)), _const_spec((3, 3 * d)), _const_spec((1, 3 * d))],
        out_specs=(row, row),
        compiler_params=_params("parallel"),
    )(x, x, x, g.reshape(1, d), w_in.astype(BF16), b_in.reshape(1, 3 * d),
      conv_w, conv_b.reshape(1, 3 * d))


def _hyena_filter_taps(L, d, f_w1, f_b1, f_w2, f_b2, f_w3, f_b3, f_freq, decay):
    t = jnp.linspace(0.0, 1.0, L, dtype=F32)[:, None]
    w_ang = 2.0 * math.pi * jnp.arange(L, dtype=F32)[:, None] / L
    bands = jnp.linspace(1e-4, HY_BANDS - 1, HY_BANDS, dtype=F32)[None, :]
    z = jnp.concatenate([t, jnp.cos(bands * w_ang), -jnp.sin(bands * w_ang)], axis=-1)

    def branch(zz, tt, col0):
        h = jnp.sin(f_freq[0] * (zz @ f_w1 + f_b1))
        h = jnp.sin(f_freq[1] * (h @ f_w2 + f_b2))
        h = h @ f_w3[:, col0:col0 + d] + f_b3[col0:col0 + d]
        return h * jnp.exp(-tt * jnp.abs(decay[col0 // d])[None])

    h_fwd = branch(z, t, 0)
    h_bwd_rev = branch(z[::-1], t[::-1], d)
    k = jnp.concatenate([h_fwd, jnp.zeros((1, d), F32), h_bwd_rev[:-1]], axis=0)
    return k / jnp.sum(jnp.abs(k), axis=0, keepdims=True)


def _fft_dims(m):
    lg = m.bit_length() - 1
    p = 1 << ((lg + 1) // 2)
    return p, m // p


def _split_bf16(x):
    hi = x.astype(BF16)
    return hi, (x - hi.astype(F32)).astype(BF16)


def _mm_split(fh, fl, x, precise):
    if not precise:
        return jnp.dot(fh, x.astype(BF16), preferred_element_type=F32)
    xh, xl = _split_bf16(x)
    return (jnp.dot(fh, xh, preferred_element_type=F32) + jnp.dot(fl, xh, preferred_element_type=F32)
            + jnp.dot(fh, xl, preferred_element_type=F32))


def _cplx_as_real(cr, ci):
    top = jnp.concatenate([cr, -ci], axis=-1)
    bot = jnp.concatenate([ci, cr], axis=-1)
    return jnp.concatenate([top, bot], axis=-2)


def _unit_circle(idx, m):
    ang = (2.0 * math.pi / m) * idx.astype(F32)
    return jnp.cos(ang), jnp.sin(ang)


def _dft_consts(p, q):
    m = p * q
    k1 = jnp.arange(p, dtype=jnp.int32)
    n1 = jnp.arange(p // 2, dtype=jnp.int32)
    c, s = _unit_circle((k1[:, None] * n1[None, :]) % p, p)
    fa = _cplx_as_real(c, -s)
    c, s = _unit_circle((n1[:, None] * k1[None, :]) % p, p)
    fd = _cplx_as_real(c / m, s / m)
    k2 = jnp.arange(q, dtype=jnp.int32)
    n2 = jnp.arange(q, dtype=jnp.int32)
    idx = (n2[None, None, :] * (k2[None, :, None] * p + k1[:, None, None])) % m
    c, s = _unit_circle(idx, m)
    gb = _cplx_as_real(c, -s)
    ct, st = jnp.swapaxes(c, 1, 2), jnp.swapaxes(s, 1, 2)
    gc = _cplx_as_real(ct, st)
    return tuple(_split_bf16(a) for a in (fa, gb, gc, fd))


def _fft_a_kernel(x_ref, fh_ref, fl_ref, o_ref, *, precise):
    _, rows_in, group, d = x_ref.shape
    rows_out = o_ref.shape[1]
    for j in range(group):
        x = x_ref[:, :, j, :].reshape(2 * rows_in, d)
        y = _mm_split(fh_ref[...], fl_ref[...], x, precise)
        o_ref[:, :, j, :] = y.reshape(2, rows_out, d)


def _fft_b_kernel(a_ref, gh_ref, gl_ref, o_ref):
    _, q, d = a_ref.shape
    x = _mm_split(gh_ref[...], gl_ref[...], a_ref[...].reshape(2 * q, d), True)
    o_ref[...] = x.reshape(o_ref.shape)


def _fft_bc_kernel(a_ref, gb_ref, gc_ref, k_ref, z_ref):
    _, q, d = a_ref.shape
    x = _mm_split(gb_ref[...], None, a_ref[...].reshape(2 * q, d), False)
    xr, xi = x[:q], x[q:]
    kr, ki = k_ref[0], k_ref[1]
    y = jnp.concatenate([xr * kr - xi * ki, xr * ki + xi * kr], axis=0)
    z = _mm_split(gc_ref[...], None, y, False)
    z_ref[...] = z.reshape(z_ref.shape)


def _fft_rows(x5, f, rows_out, precise):
    pairs, _, rows_in, q, d = x5.shape
    blk = lambda r: pl.BlockSpec((None, 2, r, HALO_ROWS, d), lambda b, j: (b, 0, 0, j, 0))
    return pl.pallas_call(
        functools.partial(_fft_a_kernel, precise=precise),
        out_shape=jax.ShapeDtypeStruct((pairs, 2, rows_out, q, d), F32),
        grid=(pairs, q // HALO_ROWS),
        in_specs=[blk(rows_in), _const_spec(f[0].shape), _const_spec(f[1].shape)],
        out_specs=blk(rows_out),
        compiler_params=_params("parallel", "parallel"),
    )(x5, *f)


def _long_conv(vx, taps, batch, seq_len):
    n, d = vx.shape
    m = 2 * seq_len
    p, q = _fft_dims(m)
    assert batch % 2 == 0
    pairs = batch // 2
    fa, gb, gc, fd = _dft_consts(p, q)
    g_spec = pl.BlockSpec((None, 2 * q, 2 * q), lambda k1, b: (k1, 0, 0))
    slab = pl.BlockSpec((None, 2, q, d), lambda k1, b: (b, 0, k1, 0))

    zeros = jnp.zeros((seq_len, d), F32)
    kin = jnp.stack([taps[:seq_len], zeros, taps[seq_len:], zeros]).reshape(2, 2, p // 2, q, d)
    ka = _fft_rows(kin, fa, p, True).reshape(2, 2, p * q, d)
    kx = pl.pallas_call(
        _fft_b_kernel,
        out_shape=jax.ShapeDtypeStruct((2, 2, p * q, d), F32),
        grid=(p, 2),
        in_specs=[slab, g_spec, g_spec],
        out_specs=slab,
        compiler_params=_params("parallel", "parallel"),
    )(ka, *gb)
    sign = jnp.repeat(1.0 - 2.0 * (jnp.arange(p) % 2).astype(F32), q)[None, :, None]
    kspec = kx[0] + sign * kx[1]

    xa = _fft_rows(vx.reshape(pairs, 2, p // 2, q, d), fa, p, False).reshape(pairs, 2, p * q, d)
    z = pl.pallas_call(
        _fft_bc_kernel,
        out_shape=jax.ShapeDtypeStruct((pairs, 2, p * q, d), F32),
        grid=(p, pairs),
        in_specs=[slab, g_spec, g_spec, pl.BlockSpec((2, q, d), lambda k1, b: (0, k1, 0))],
        out_specs=slab,
        compiler_params=_params("parallel", "parallel"),
    )(xa, gb[0], gc[0], kspec)
    y = _fft_rows(z.reshape(pairs, 2, p, q, d), fd, p // 2, False)
    return y.reshape(n, d)


def _hyena_out_prologue(y, vx, x0, skip):
    return (y + vx * skip) * x0


def _hyena_layer(x, batch, seq_len, g, p):
    n, d = x.shape
    x0, vx = _hyena_in(x, g, p['w_in'], p['b_in'], p['conv_w'], p['conv_b'], seq_len)
    taps = _hyena_filter_taps(seq_len, d, p['f_w1'], p['f_b1'], p['f_w2'], p['f_b2'],
                              p['f_w3'], p['f_b3'], p['f_freq'], p['decay'])
    y = _long_conv(vx, taps, batch, seq_len)
    return _mm_res(_hyena_out_prologue, [y, vx, x0], [p['skip'].reshape(1, d)],
                   p['w_out'], p['b_out'], x)


def _head_norm_rope(xh, gain, cos, sin, half):
    y = _rms(xh, gain)
    lane = lax.broadcasted_iota(jnp.int32, (1, HEAD_DIM), 1)
    fwd = pltpu.roll(y, HEAD_DIM - half, 1)
    bwd = pltpu.roll(y, half, 1)
    partner = jnp.where((lane % (2 * half)) < half, fwd, bwd)
    return y * cos + partner * sin


def _qkv_rope_kernel(x_ref, g_ref, w_ref, qg_ref, kg_ref, cos_ref, sin_ref,
                     q_ref, k_ref, v_ref, *, nq, nk, nv, half):
    xb = _rms(x_ref[...], g_ref[...]).astype(BF16)
    cos = cos_ref[...]
    sin = sin_ref[...]
    per = MXU_DIM // HEAD_DIM
    for h0 in range(0, nq + nk + nv, per):
        pw = jnp.dot(xb, w_ref[:, h0 * HEAD_DIM:(h0 + per) * HEAD_DIM], preferred_element_type=F32)
        for h in range(h0, h0 + per):
            ph = pw[:, (h - h0) * HEAD_DIM:(h - h0 + 1) * HEAD_DIM]
            if h < nq:
                sl = slice(h * HEAD_DIM, (h + 1) * HEAD_DIM)
                q_ref[:, sl] = _head_norm_rope(ph, qg_ref[...], cos, sin, half).astype(BF16)
            elif h < nq + nk:
                sl = slice((h - nq) * HEAD_DIM, (h - nq + 1) * HEAD_DIM)
                k_ref[:, sl] = _head_norm_rope(ph, kg_ref[...], cos, sin, half).astype(BF16)
            else:
                sl = slice((h - nq - nk) * HEAD_DIM, (h - nq - nk + 1) * HEAD_DIM)
                v_ref[:, sl] = ph.astype(BF16)


def _qkv_rope(x, g, w_qkv, q_gain, k_gain, cos, sin, nq, nk, nv, half, seq_len, tm=512):
    n, d = x.shape
    tm = min(tm, seq_len)
    pos_blocks = seq_len // tm
    f = w_qkv.shape[1]
    row = pl.BlockSpec((tm, d), lambda i: (i, 0))
    tab = pl.BlockSpec((tm, HEAD_DIM), lambda i: (i % pos_blocks, 0))
    outs = tuple(jax.ShapeDtypeStruct((n, c * HEAD_DIM), BF16) for c in (nq, nk, nv))
    return pl.pallas_call(
        functools.partial(_qkv_rope_kernel, nq=nq, nk=nk, nv=nv, half=half),
        out_shape=outs,
        grid=(n // tm,),
        in_specs=[row, _const_spec((1, d)), _const_spec((d, f)),
                  _const_spec((1, HEAD_DIM)), _const_spec((1, HEAD_DIM)), tab, tab],
        out_specs=tuple(pl.BlockSpec((tm, c * HEAD_DIM), lambda i: (i, 0)) for c in (nq, nk, nv)),
        compiler_params=_params("parallel"),
    )(x, g.reshape(1, d), w_qkv.astype(BF16), q_gain.reshape(1, HEAD_DIM),
      k_gain.reshape(1, HEAD_DIM), cos, sin)


def _axial_tables(L):
    t = jnp.arange(L)
    r = (t // GRID_W).astype(F32)
    c = (t % GRID_W).astype(F32)
    nf = HEAD_DIM // 4
    inv = AXIAL_THETA ** (-(2.0 * jnp.arange(nf, dtype=F32)) / (2 * nf))
    ar, ac = r[:, None] * inv[None], c[:, None] * inv[None]
    cos = jnp.concatenate([jnp.cos(ar), jnp.cos(ar), jnp.cos(ac), jnp.cos(ac)], axis=-1)
    sin = jnp.concatenate([-jnp.sin(ar), jnp.sin(ar), -jnp.sin(ac), jnp.sin(ac)], axis=-1)
    return cos, sin


FLASH_SAFE_BOUND = 40.0
FLASH_BOUND_MARGIN = 1.001


def _flash_kernel(q_ref, k_ref, v_ref, o_ref, acc_scr, off_scr, lsum_scr, m_scr, l_scr, kn_scr,
                  *, tq, tk, tkf, seq_len, group):
    scale = HEAD_DIM ** -0.5
    c = scale * math.log2(math.e)

    @pl.when(pl.program_id(2) == 0)
    def _():
        def norm_step(j, mx):
            kt = k_ref[pl.ds(pl.multiple_of(j * tk, tk), tk), :].astype(F32)
            row = jnp.sum(kt * kt, axis=1, keepdims=True)
            return jnp.maximum(mx, jnp.max(row, axis=0, keepdims=True))

        kmax2 = lax.fori_loop(0, seq_len // tk, norm_step, jnp.zeros((1, 1), F32))
        kn_scr[...] = jnp.broadcast_to(kmax2, kn_scr.shape)

    kmax2 = kn_scr[0:1, 0:1]
    bmax = jnp.zeros((1, 1), F32)
    for h in range(group):
        qf = q_ref[:, h * HEAD_DIM:(h + 1) * HEAD_DIM].astype(F32)
        b = jnp.sqrt(jnp.sum(qf * qf, axis=1, keepdims=True) * kmax2) * FLASH_BOUND_MARGIN
        bmax = jnp.maximum(bmax, jnp.max(b, axis=0, keepdims=True))
        off_scr[h] = jnp.broadcast_to(b * c, (tq, HEAD_DIM))
    fast = (bmax * scale)[0, 0] <= FLASH_SAFE_BOUND

    @pl.when(fast)
    def _():
        acc_scr[...] = jnp.zeros_like(acc_scr)
        lsum_scr[...] = jnp.zeros_like(lsum_scr)
        n_lane_tiles = tkf // HEAD_DIM

        def body(j, carry):
            start = pl.multiple_of(j * tkf, tkf)
            kt = k_ref[pl.ds(start, tkf), :]
            vt = v_ref[pl.ds(start, tkf), :]
            for h in range(group):
                q = q_ref[:, h * HEAD_DIM:(h + 1) * HEAD_DIM]
                s = lax.dot_general(q, kt, (((1,), (1,)), ((), ())), preferred_element_type=F32)
                off = off_scr[h]
                p = jnp.exp2(s * c - jnp.concatenate([off] * n_lane_tiles, axis=1))
                part = p[:, 0:HEAD_DIM]
                for t in range(1, n_lane_tiles):
                    part = part + p[:, t * HEAD_DIM:(t + 1) * HEAD_DIM]
                lsum_scr[h] += part
                acc_scr[h] += jnp.dot(p.astype(BF16), vt, preferred_element_type=F32)
            return carry

        lax.fori_loop(0, seq_len // tkf, body, 0)
        for h in range(group):
            l = jnp.sum(lsum_scr[h], axis=1, keepdims=True)
            o_ref[:, h * HEAD_DIM:(h + 1) * HEAD_DIM] = (acc_scr[h] / l).astype(o_ref.dtype)

    @pl.when(jnp.logical_not(fast))
    def _():
        m_scr[...] = jnp.full(m_scr.shape, -jnp.inf, F32)
        l_scr[...] = jnp.zeros_like(l_scr)
        acc_scr[...] = jnp.zeros_like(acc_scr)

        def body(j, carry):
            start = pl.multiple_of(j * tk, tk)
            kt = k_ref[pl.ds(start, tk), :]
            vt = v_ref[pl.ds(start, tk), :]
            for h in range(group):
                q = q_ref[:, h * HEAD_DIM:(h + 1) * HEAD_DIM]
                s = lax.dot_general(q, kt, (((1,), (1,)), ((), ())), preferred_element_type=F32)
                m = m_scr[h]
                m_new = jnp.maximum(m, jnp.max(s, axis=1, keepdims=True))
                alpha = jnp.exp2((m - m_new) * c)
                p = jnp.exp2(s * c - m_new * c)
                l_scr[h] = alpha * l_scr[h] + jnp.sum(p, axis=1, keepdims=True)
                acc_scr[h] = alpha * acc_scr[h] + jnp.dot(p.astype(BF16), vt, preferred_element_type=F32)
                m_scr[h] = m_new
            return carry

        lax.fori_loop(0, seq_len // tk, body, 0)
        for h in range(group):
            o_ref[:, h * HEAD_DIM:(h + 1) * HEAD_DIM] = (acc_scr[h] / l_scr[h]).astype(o_ref.dtype)


def _flash_gqa(q, k, v, batch, seq_len, tq=1024, tk=512, tkf=1024):
    tq = min(tq, seq_len)
    tk = min(tk, seq_len)
    tkf = min(tkf, seq_len)
    nq = seq_len // tq
    gw = GA_GROUP * HEAD_DIM
    wide = pltpu.VMEM((GA_GROUP, tq, HEAD_DIM), F32)
    thin = pltpu.VMEM((GA_GROUP, tq, 1), F32)
    return pl.pallas_call(
        functools.partial(_flash_kernel, tq=tq, tk=tk, tkf=tkf, seq_len=seq_len, group=GA_GROUP),
        out_shape=jax.ShapeDtypeStruct(q.shape, BF16),
        grid=(batch, GA_KV_HEADS, nq),
        in_specs=[pl.BlockSpec((tq, gw), lambda b, kv, i: (b * nq + i, kv)),
                  pl.BlockSpec((seq_len, HEAD_DIM), lambda b, kv, i: (b, kv)),
                  pl.BlockSpec((seq_len, HEAD_DIM), lambda b, kv, i: (b, kv))],
        out_specs=pl.BlockSpec((tq, gw), lambda b, kv, i: (b * nq + i, kv)),
        scratch_shapes=[wide, wide, wide, thin, thin, pltpu.VMEM((HALO_ROWS, HEAD_DIM), F32)],
        compiler_params=_params("parallel", "parallel", "arbitrary"),
    )(q, k, v)


def _identity_prologue(o):
    return o


def _gqa_layer(x, batch, seq_len, g, p):
    cos, sin = _axial_tables(seq_len)
    q, k, v = _qkv_rope(x, g, p['w_qkv'], p['q_gain'], p['k_gain'], cos, sin,
                        GA_HEADS, GA_KV_HEADS, GA_KV_HEADS, HEAD_DIM // 4, seq_len)
    o = _flash_gqa(q, k, v, batch, seq_len)
    return _mm_res(_identity_prologue, [o], [], p['w_o'], None, x)


def _ml_in_kernel(x_ref, xp_ref, xn_ref, g_ref, w_ref, cw_ref, cb_ref, wq_ref, wk_ref, wv_ref,
                  wg_ref, bg_ref, q_ref, k_ref, v_ref, xc_ref, sz_ref, gate_ref,
                  *, seq_len, tm, inner, k_scale):
    keep_prev, keep_next = _edge_scales(pl.program_id(0), tm, seq_len)
    g = g_ref[...]
    xb = _rms(x_ref[...], g).astype(BF16)
    hb = _rms(jnp.concatenate([xp_ref[...], xn_ref[...]], axis=0), g).astype(BF16)
    rows = lax.broadcasted_iota(jnp.int32, (tm, 1), 0)
    gacc = jnp.zeros(gate_ref.shape, F32)
    for t in range(inner // MXU_DIM):
        sl = slice(t * MXU_DIM, (t + 1) * MXU_DIM)
        w = w_ref[:, sl]
        xm = jnp.dot(xb, w, preferred_element_type=F32)
        xh = jnp.dot(hb, w, preferred_element_type=F32)
        pprev = xh[HALO_ROWS - 1:HALO_ROWS, :] * keep_prev
        pnext = xh[HALO_ROWS:HALO_ROWS + 1, :] * keep_next
        xc = _conv3_rows(xm, pprev, pnext, cw_ref[:, sl], cb_ref[:, sl], rows, tm)
        xc = xc * jax.nn.sigmoid(xc)
        z = jnp.dot(xb, w_ref[:, inner + t * MXU_DIM:inner + (t + 1) * MXU_DIM],
                    preferred_element_type=F32)
        xcb = xc.astype(BF16)
        q = jnp.dot(xcb, wq_ref[t], preferred_element_type=F32)
        k = jnp.dot(xcb, wk_ref[t], preferred_element_type=F32)
        v = jnp.dot(xm.astype(BF16), wv_ref[t], preferred_element_type=F32)
        qb, kb, vb = q.astype(BF16), k.astype(BF16), v.astype(BF16)
        gacc += (jnp.dot(qb, wg_ref[0, sl, :], preferred_element_type=F32)
                 + jnp.dot(kb, wg_ref[1, sl, :], preferred_element_type=F32)
                 + jnp.dot(vb, wg_ref[2, sl, :], preferred_element_type=F32))
        q_ref[:, sl] = qb
        k_ref[:, sl] = (k * k_scale).astype(BF16)
        v_ref[:, sl] = vb
        xc_ref[:, sl] = xc
        sz_ref[:, sl] = z * jax.nn.sigmoid(z)
    gate_ref[...] = gacc + bg_ref[...]


def _block_diag_tiles(w):
    nb, c, _ = w.shape
    per = MXU_DIM // c
    wt = w.reshape(nb // per, per, c, c)
    eye = jnp.eye(per, dtype=w.dtype)
    full = jnp.einsum('tpcd,pq->tpcqd', wt, eye)
    return full.reshape(nb // per, MXU_DIM, MXU_DIM)


def _ml_in(x, g, p, seq_len, tm=256):
    n, d = x.shape
    tm = min(tm, seq_len)
    inner = p['w_up'].shape[1] // 2
    ng = 4 * ML_HEADS
    dh = inner // ML_HEADS
    wq, wk, wv = (_block_diag_tiles(p[nm]).astype(BF16) for nm in ('w_q', 'w_k', 'w_v'))
    wg = jnp.transpose(p['w_gate'], (1, 2, 0, 3)).reshape(3, inner, ng).astype(BF16)
    bg = p['b_gate'].reshape(1, ng)
    prev, nxt = _halo_specs(tm, d, n)
    row = pl.BlockSpec((tm, d), lambda i: (i, 0))
    wide = pl.BlockSpec((tm, inner), lambda i: (i, 0))
    nt = inner // MXU_DIM
    return pl.pallas_call(
        functools.partial(_ml_in_kernel, seq_len=seq_len, tm=tm, inner=inner, k_scale=dh ** -0.5),
        out_shape=(jax.ShapeDtypeStruct((n, inner), BF16),) * 3
        + (jax.ShapeDtypeStruct((n, inner), F32),) * 2
        + (jax.ShapeDtypeStruct((n, ng), F32),),
        grid=(n // tm,),
        in_specs=[row, prev, nxt, _const_spec((1, d)), _const_spec((d, 2 * inner)),
                  _const_spec((3, inner)), _const_spec((1, inner)),
                  _const_spec((nt, MXU_DIM, MXU_DIM)), _const_spec((nt, MXU_DIM, MXU_DIM)),
                  _const_spec((nt, MXU_DIM, MXU_DIM)), _const_spec((3, inner, ng)),
                  _const_spec((1, ng))],
        out_specs=(wide,) * 5 + (pl.BlockSpec((tm, ng), lambda i: (i, 0)),),
        compiler_params=_params("parallel"),
    )(x, x, x, g.reshape(1, d), p['w_up'].astype(BF16), p['conv_w'],
      p['conv_b'].reshape(1, inner), wq, wk, wv, wg, bg)


def _log_sigmoid(x):
    return jnp.minimum(x, 0.0) - jnp.log1p(jnp.exp(-jnp.abs(x)))


def _mlstm_chunk_kernel(qf_ref, kf_ref, vf_ref, gcf_ref, grf_ref, qb_ref, kb_ref, vb_ref, gcb_ref, grb_ref,
                        hf_ref, hb_ref, c_scr, n_scr, m_scr, *, lc):
    @pl.when(pl.program_id(2) == 0)
    def _():
        c_scr[...] = jnp.zeros_like(c_scr)
        n_scr[...] = jnp.zeros_like(n_scr)
        m_scr[...] = jnp.zeros_like(m_scr)

    _mlstm_step(qf_ref, kf_ref, vf_ref, gcf_ref, grf_ref, hf_ref, c_scr.at[0], n_scr.at[0], m_scr.at[0], lc, False)
    _mlstm_step(qb_ref, kb_ref, vb_ref, gcb_ref, grb_ref, hb_ref, c_scr.at[1], n_scr.at[1], m_scr.at[1], lc, True)


def _mlstm_step(q_ref, k_ref, v_ref, gc_ref, gr_ref, h_ref, c_scr, n_scr, m_scr, lc, backward):
    q = q_ref[...]
    k = k_ref[...]
    v = v_ref[...]
    gc = gc_ref[...]
    gr = gr_ref[...]
    i_col, g_col = gc[:, 0:1], gc[:, 1:2]
    i_row, g_row = gr[0:1, :], gr[1:2, :]

    jr = lax.broadcasted_iota(jnp.int32, (lc, lc), 0)
    sc = lax.broadcasted_iota(jnp.int32, (lc, lc), 1)
    seen = (sc >= jr) if backward else (sc <= jr)
    g_tot = g_row[:, 0:1] if backward else g_row[:, lc - 1:lc]
    m_old = m_scr[0:1, 0:1]

    dmat = jnp.where(seen, g_col - g_row + i_row, -jnp.inf)
    inter = g_col + m_old
    m_q = jnp.maximum(inter, jnp.max(dmat, axis=1, keepdims=True))
    s_qk = lax.dot_general(q, k, (((1,), (1,)), ((), ())), preferred_element_type=F32)
    a = s_qk * jnp.exp(dmat - m_q)
    w_int = jnp.exp(inter - m_q)
    q_c = jnp.dot(q, c_scr[...].astype(BF16), preferred_element_type=F32)
    num = jnp.dot(a.astype(BF16), v, preferred_element_type=F32) + q_c * w_int
    q_n = jnp.sum(q.astype(F32) * n_scr[...], axis=1, keepdims=True)
    den = jnp.sum(a, axis=1, keepdims=True) + w_int * q_n
    den = jnp.maximum(jnp.abs(den), jnp.exp(-m_q))
    h_ref[...] = num / den

    a_row = g_tot - g_row + i_row
    m_new = jnp.maximum(g_tot + m_old, jnp.max(a_row, axis=1, keepdims=True))
    ws_col = jnp.exp(g_tot - g_col + i_col - m_new)
    dec = jnp.exp(g_tot + m_old - m_new)
    kw = k.astype(F32) * ws_col
    upd = lax.dot_general(kw.astype(BF16), v, (((0,), (0,)), ((), ())), preferred_element_type=F32)
    c_scr[...] = dec * c_scr[...] + upd
    n_scr[...] = dec * n_scr[...] + jnp.sum(kw, axis=0, keepdims=True)
    m_scr[...] = jnp.broadcast_to(m_new, m_scr.shape)


def _ml_gate_kernel(g_ref, o_ref, *, lc, nh):
    g = g_ref[...]
    jr = lax.broadcasted_iota(jnp.int32, (lc, lc), 0)
    sc = lax.broadcasted_iota(jnp.int32, (lc, lc), 1)
    tril = jnp.where(sc <= jr, 1.0, 0.0).astype(BF16)
    triu = jnp.where(sc >= jr, 1.0, 0.0).astype(BF16)
    col = lax.broadcasted_iota(jnp.int32, (1, 4 * nh), 1)
    lf = _log_sigmoid(g)
    x1 = lf.astype(BF16)
    r1 = lf - x1.astype(F32)
    x2 = r1.astype(BF16)
    x3 = (r1 - x2.astype(F32)).astype(BF16)
    cum = lambda m: (jnp.dot(m, x1, preferred_element_type=F32) + jnp.dot(m, x2, preferred_element_type=F32)
                     + jnp.dot(m, x3, preferred_element_type=F32))
    gsum = jnp.where(col < 2 * nh, cum(tril), cum(triu))
    o_ref[...] = jnp.where((col % (2 * nh)) >= nh, gsum, g)


def _mlstm_chunks(q, k, v, gates, batch, seq_len):
    n, inner = q.shape
    dh = inner // ML_HEADS
    lc = min(ML_CHUNK, seq_len)
    nc = seq_len // lc
    ng = gates.shape[1]
    gates = pl.pallas_call(
        functools.partial(_ml_gate_kernel, lc=lc, nh=ML_HEADS),
        out_shape=jax.ShapeDtypeStruct((n, ng), F32),
        grid=(n // lc,),
        in_specs=[pl.BlockSpec((lc, ng), lambda i: (i, 0))],
        out_specs=pl.BlockSpec((lc, ng), lambda i: (i, 0)),
        compiler_params=_params("parallel"),
    )(gates)
    g4 = gates.reshape(n, 2, 2, ML_HEADS)
    gcol = jnp.transpose(g4, (1, 3, 0, 2))
    grow = jnp.transpose(g4, (1, 3, 2, 0))

    def specs(dd):
        chunk = (lambda b, c: b * nc + c) if dd == 0 else (lambda b, c: b * nc + nc - 1 - c)
        qkv = pl.BlockSpec((lc, dh), lambda b, h, c: (chunk(b, c), h))
        gc = pl.BlockSpec((None, None, lc, 2), lambda b, h, c: (dd, h, chunk(b, c), 0))
        gr = pl.BlockSpec((None, None, 2, lc), lambda b, h, c: (dd, h, 0, chunk(b, c)))
        return [qkv, qkv, qkv, gc, gr], qkv

    in_f, out_f = specs(0)
    in_b, out_b = specs(1)
    return pl.pallas_call(
        functools.partial(_mlstm_chunk_kernel, lc=lc),
        out_shape=(jax.ShapeDtypeStruct((n, inner), F32),) * 2,
        grid=(batch, ML_HEADS, nc),
        in_specs=in_f + in_b,
        out_specs=(out_f, out_b),
        scratch_shapes=[pltpu.VMEM((2, dh, dh), F32), pltpu.VMEM((2, 1, dh), F32),
                        pltpu.VMEM((2, HALO_ROWS, HEAD_DIM), F32)],
        compiler_params=_params("parallel", "parallel", "arbitrary"),
    )(q, k, v, gcol, grow, q, k, v, gcol, grow)


def _ml_out_prologue(hf, hb, xc, sz, gain, skip):
    h = hf + hb
    dh = h.shape[1] // ML_HEADS
    parts = []
    for i in range(ML_HEADS):
        sl = slice(i * dh, (i + 1) * dh)
        parts.append(_rms(h[:, sl], gain[:, sl]))
    hn = jnp.concatenate(parts, axis=1)
    return (hn + skip * xc) * sz


def _mlstm_layer(x, batch, seq_len, g, p):
    q, k, v, xc, sz, gates = _ml_in(x, g, p, seq_len)
    inner = q.shape[1]
    hf, hb = _mlstm_chunks(q, k, v, gates, batch, seq_len)
    return _mm_res(_ml_out_prologue, [hf, hb, xc, sz],
                   [p['norm_gain'].reshape(1, inner), p['skip'].reshape(1, inner)],
                   p['w_down'], None, x, tm=256)


def _rope_tables(L):
    inv = ROPE_THETA ** (-(2.0 * jnp.arange(ROPE_DIMS // 2, dtype=F32)) / ROPE_DIMS)
    ang = jnp.arange(L, dtype=F32)[:, None] * inv[None]
    pad = HEAD_DIM - ROPE_DIMS
    cos = jnp.concatenate([jnp.cos(ang), jnp.cos(ang), jnp.ones((L, pad), F32)], axis=-1)
    sin = jnp.concatenate([-jnp.sin(ang), jnp.sin(ang), jnp.zeros((L, pad), F32)], axis=-1)
    return cos, sin


def _band_kernel(q_ref, kp_ref, kc_ref, kn_ref, vp_ref, vc_ref, vn_ref, o_ref, lse_ref,
                 *, s_len, heads, subs):
    i = pl.program_id(2)
    qb = BAND_BLOCK
    w = qb + 2 * BAND_HALF
    a = lax.broadcasted_iota(jnp.int32, (qb, w), 0)
    c = lax.broadcasted_iota(jnp.int32, (qb, w), 1)
    in_band = jnp.abs(c - BAND_HALF - a) <= BAND_HALF
    scale = HEAD_DIM ** -0.5
    for h in range(heads):
        sl = slice(h * HEAD_DIM, (h + 1) * HEAD_DIM)
        kw = jnp.concatenate([kp_ref[qb - BAND_HALF:, sl], kc_ref[:, sl], kn_ref[:BAND_HALF, sl]], axis=0)
        vw = jnp.concatenate([vp_ref[qb - BAND_HALF:, sl], vc_ref[:, sl], vn_ref[:BAND_HALF, sl]], axis=0)
        for u in range(subs):
            rows = slice(u * qb, (u + 1) * qb)
            key_pos = (i * subs + u) * qb - BAND_HALF + c
            valid = in_band & (key_pos >= 0) & (key_pos < s_len)
            s = lax.dot_general(q_ref[rows, sl], kw[u * qb:u * qb + w], (((1,), (1,)), ((), ())),
                                preferred_element_type=F32) * scale
            s = jnp.where(valid, s, -jnp.inf)
            m = jnp.max(s, axis=1, keepdims=True)
            p = jnp.exp(s - m)
            l = jnp.sum(p, axis=1, keepdims=True)
            o = jnp.dot(p.astype(BF16), vw[u * qb:u * qb + w], preferred_element_type=F32)
            o_ref[rows, sl] = o / l
            lse_ref[rows, sl] = jnp.broadcast_to(m + jnp.log(l), (qb, HEAD_DIM))


def _da_qkv_kernel(x_ref, g_ref, w_ref, qg_ref, kg_ref, cos_ref, sin_ref, *refs, dils, tm, half):
    n_perm = sum(1 for d in dils if d > 1)
    perm_refs, out_refs = refs[:n_perm], refs[n_perm:]
    xb = _rms(x_ref[...], g_ref[...]).astype(BF16)
    cos, sin = cos_ref[...], sin_ref[...]
    gw = DA_HEADS_PER_GROUP * HEAD_DIM
    heads = DA_HEADS_PER_GROUP * len(dils)
    for kind in range(3):
        gain = (qg_ref, kg_ref, None)[kind]
        pi = 0
        for gi, dil in enumerate(dils):
            c0 = (kind * heads + gi * DA_HEADS_PER_GROUP) * HEAD_DIM
            ph = jnp.dot(xb, w_ref[:, c0:c0 + gw], preferred_element_type=F32)
            if gain is not None:
                ph = jnp.concatenate(
                    [_head_norm_rope(ph[:, h * HEAD_DIM:(h + 1) * HEAD_DIM], gain[...], cos, sin, half)
                     for h in range(DA_HEADS_PER_GROUP)], axis=1)
            val = ph.astype(BF16)
            o_ref = out_refs[kind * len(dils) + gi]
            if dil == 1:
                o_ref[...] = val
            else:
                pv = jnp.dot(perm_refs[pi][...], val, preferred_element_type=F32).astype(BF16)
                pi += 1
                rows = tm // dil
                for r in range(dil):
                    o_ref[:, r * gw:(r + 1) * gw] = pv[r * rows:(r + 1) * rows, :]


def _da_qkv(x, g, w_qkv, q_gain, k_gain, cos, sin, seq_len, tm=256):
    n, d = x.shape
    tm = min(tm, seq_len)
    pos_blocks = seq_len // tm
    dils = tuple(dil for _, dil in DA_GROUPS)
    gw = DA_HEADS_PER_GROUP * HEAD_DIM
    f = w_qkv.shape[1]
    perms = []
    for dil in dils:
        if dil > 1:
            dst = jnp.arange(tm)
            src = (dst % (tm // dil)) * dil + dst // (tm // dil)
            perms.append((src[:, None] == jnp.arange(tm)[None, :]).astype(BF16))
    row = pl.BlockSpec((tm, d), lambda i: (i, 0))
    tab = pl.BlockSpec((tm, HEAD_DIM), lambda i: (i % pos_blocks, 0))
    out_shapes = tuple(jax.ShapeDtypeStruct((n // dil, dil * gw), BF16) for _ in range(3) for dil in dils)
    out_specs = tuple(pl.BlockSpec((tm // dil, dil * gw), lambda i: (i, 0)) for _ in range(3) for dil in dils)
    outs = pl.pallas_call(
        functools.partial(_da_qkv_kernel, dils=dils, tm=tm, half=ROPE_DIMS // 2),
        out_shape=out_shapes,
        grid=(n // tm,),
        in_specs=[row, _const_spec((1, d)), _const_spec((d, f)), _const_spec((1, HEAD_DIM)),
                  _const_spec((1, HEAD_DIM)), tab, tab] + [_const_spec((tm, tm))] * len(perms),
        out_specs=out_specs,
        compiler_params=_params("parallel"),
    )(x, g.reshape(1, d), w_qkv.astype(BF16), q_gain.reshape(1, HEAD_DIM), k_gain.reshape(1, HEAD_DIM),
      cos, sin, *perms)
    ng = len(dils)
    return [(outs[gi], outs[ng + gi], outs[2 * ng + gi]) for gi in range(ng)]


def _band_attention(q, k, v, batch, seq_len, dil):
    n = batch * seq_len
    s_len = seq_len // dil
    nb = s_len // BAND_BLOCK
    subs = min(BAND_SUBBLOCKS, nb)
    hw = DA_HEADS_PER_GROUP * HEAD_DIM
    view = lambda a: a.reshape(batch, s_len, dil * hw)
    qv, kv, vv = view(q), view(k), view(v)

    main = pl.BlockSpec((None, subs * BAND_BLOCK, hw), lambda b, r, i: (b, i, r))
    prev = pl.BlockSpec((None, BAND_BLOCK, hw), lambda b, r, i: (b, jnp.maximum(i * subs - 1, 0), r))
    nxt = pl.BlockSpec((None, BAND_BLOCK, hw), lambda b, r, i: (b, jnp.minimum((i + 1) * subs, nb - 1), r))
    o, lse = pl.pallas_call(
        functools.partial(_band_kernel, s_len=s_len, heads=DA_HEADS_PER_GROUP, subs=subs),
        out_shape=(jax.ShapeDtypeStruct((batch, s_len, dil * hw), F32),) * 2,
        grid=(batch, dil, nb // subs),
        in_specs=[main, prev, main, nxt, prev, main, nxt],
        out_specs=(main, main),
        compiler_params=_params("parallel", "parallel", "parallel"),
    )(qv, kv, kv, kv, vv, vv, vv)
    return o.reshape(n, hw), lse.reshape(n, hw)


def _da_out_prologue(o0, o1, o2, l0, l1, l2):
    m = jnp.maximum(jnp.maximum(l0, l1), l2)
    e0, e1, e2 = jnp.exp(l0 - m), jnp.exp(l1 - m), jnp.exp(l2 - m)
    return (e0 * o0 + e1 * o1 + e2 * o2) / (e0 + e1 + e2)


def _dilated_layer(x, batch, seq_len, g, p):
    cos, sin = _rope_tables(seq_len)
    qkv = _da_qkv(x, g, p['w_qkv'], p['q_gain'], p['k_gain'], cos, sin, seq_len)
    outs, lses = [], []
    for (q, k, v), (_, dil) in zip(qkv, DA_GROUPS):
        o, lse = _band_attention(q, k, v, batch, seq_len, dil)
        outs.append(o)
        lses.append(lse)
    return _mm_res(_da_out_prologue, outs + lses, [], p['w_o'], None, x)


def _router_kernel(x_ref, g_ref, wr_ref, xn_ref, aff_ref):
    xn = _rms(x_ref[...], g_ref[...])
    xn_ref[...] = xn.astype(BF16)
    logits = lax.dot_general(wr_ref[...], xn, (((1,), (1,)), ((), ())),
                             precision=HIGHEST, preferred_element_type=F32)
    m = jnp.max(logits, axis=0, keepdims=True)
    e = jnp.exp(logits - m)
    aff_ref[...] = e / jnp.sum(e, axis=0, keepdims=True)


def _router(x, g, w_router, tm=512):
    n, d = x.shape
    tm = min(tm, n)
    e = w_router.shape[1]
    return pl.pallas_call(
        _router_kernel,
        out_shape=(jax.ShapeDtypeStruct((n, d), BF16), jax.ShapeDtypeStruct((e, n), F32)),
        grid=(n // tm,),
        in_specs=[pl.BlockSpec((tm, d), lambda i: (i, 0)), _const_spec((1, d)), _const_spec((e, d))],
        out_specs=(pl.BlockSpec((tm, d), lambda i: (i, 0)), pl.BlockSpec((e, tm), lambda i: (0, i))),
        compiler_params=_params("parallel"),
    )(x, g.reshape(1, d), w_router.T)


def _expert_ffn_kernel(xe_ref, gate_ref, w1_ref, w3_ref, w2_ref, ye_ref, w1b, w3b, w2b):
    @pl.when(pl.program_id(1) == 0)
    def _():
        w1b[...] = w1_ref[...].astype(BF16)
        w3b[...] = w3_ref[...].astype(BF16)
        w2b[...] = w2_ref[...].astype(BF16)

    xe = xe_ref[...]
    h1 = jnp.dot(xe, w1b[...], preferred_element_type=F32)
    h3 = jnp.dot(xe, w3b[...], preferred_element_type=F32)
    hid = (h1 * jax.nn.sigmoid(h1) * h3).astype(BF16)
    ye_ref[...] = (jnp.dot(hid, w2b[...], preferred_element_type=F32) * gate_ref[...]).astype(ye_ref.dtype)


def _expert_ffn(xe, gates, w1, w3, w2, tm=512):
    e, c, d = xe.shape
    f = w1.shape[2]
    tm = min(tm, c)
    return pl.pallas_call(
        _expert_ffn_kernel,
        out_shape=jax.ShapeDtypeStruct((e, c, d), BF16),
        grid=(e, c // tm),
        in_specs=[pl.BlockSpec((None, tm, d), lambda ei, ci: (ei, ci, 0)),
                  pl.BlockSpec((None, tm, 1), lambda ei, ci: (ei, ci, 0)),
                  pl.BlockSpec((None, d, f), lambda ei, ci: (ei, 0, 0)),
                  pl.BlockSpec((None, d, f), lambda ei, ci: (ei, 0, 0)),
                  pl.BlockSpec((None, f, d), lambda ei, ci: (ei, 0, 0))],
        out_specs=pl.BlockSpec((None, tm, d), lambda ei, ci: (ei, ci, 0)),
        scratch_shapes=[pltpu.VMEM((d, f), BF16), pltpu.VMEM((d, f), BF16), pltpu.VMEM((f, d), BF16)],
        compiler_params=_params("parallel", "arbitrary"),
    )(xe, gates[..., None], w1, w3, w2)


def _moe_layer(x, group_sizes, g, w_router, w1, w3, w2, split_output):
    n, d = x.shape
    xn, aff_t = _router(x, g, w_router)
    gts, idxs = [], []
    start = 0
    for ng in group_sizes:
        cap = EC_CAPACITY * ng // N_EXPERTS
        gates, idx = lax.top_k(aff_t[:, start:start + ng], cap)
        gts.append(gates)
        idxs.append(idx + start)
        start += ng
    idx_all = jnp.concatenate(idxs, axis=1)
    ye = _expert_ffn(xn[idx_all], jnp.concatenate(gts, axis=1), w1, w3, w2)
    return _combine(x, ye.reshape(-1, d), idx_all.reshape(-1), group_sizes if split_output else (n,))


COMBINE_TOKENS = 512
COMBINE_ROWS = 512


def _combine_kernel(tile_ref, blk_ref, live_ref, x_ref, tok_ref, ye_ref, *o_refs, split_tiles):
    w = pl.program_id(0)
    tile = tile_ref[w]
    first = jnp.logical_or(w == 0, tile != tile_ref[jnp.maximum(w - 1, 0)])
    tt = x_ref.shape[0]
    rows = lax.broadcasted_iota(jnp.int32, (tt, COMBINE_ROWS), 0)
    onehot = jnp.where(rows == tok_ref[...] - tile * tt, 1.0, 0.0).astype(BF16)
    add = jnp.dot(onehot, ye_ref[...], preferred_element_type=F32) * live_ref[w].astype(F32)
    lo_tile = 0
    for o_ref, n_tiles in zip(o_refs, split_tiles):
        mine = jnp.logical_and(tile >= lo_tile, tile < lo_tile + n_tiles)

        @pl.when(jnp.logical_and(mine, first))
        def _(o_ref=o_ref):
            o_ref[...] = x_ref[...] + add

        @pl.when(jnp.logical_and(mine, jnp.logical_not(first)))
        def _(o_ref=o_ref):
            o_ref[...] += add

        lo_tile += n_tiles


def _combine(x, ye, tok, splits):
    n, d = x.shape
    p = tok.shape[0]
    tt = min(COMBINE_TOKENS, min(splits))
    rb = COMBINE_ROWS
    assert p % rb == 0 and all(s % tt == 0 for s in splits) and sum(splits) == n
    tiles, nblk = n // tt, p // rb
    split_tiles = tuple(s // tt for s in splits)
    order = jnp.argsort(tok)
    tok_sorted = tok[order].astype(jnp.int32)
    ye_sorted = ye[order]

    edges = jnp.arange(tiles + 1, dtype=jnp.int32) * tt
    bounds = jnp.searchsorted(tok_sorted, edges, method='compare_all').astype(jnp.int32)
    lo, hi = bounds[:-1], bounds[1:]
    first_blk = jnp.minimum(lo // rb, nblk - 1)
    last_blk = jnp.where(hi > lo, (hi - 1) // rb, first_blk)
    n_items = last_blk - first_blk + 1
    item_end = jnp.cumsum(n_items)
    item_start = item_end - n_items
    max_items = nblk + 2 * tiles
    w = jnp.arange(max_items, dtype=jnp.int32)
    tile = jnp.minimum(jnp.searchsorted(item_end, w, side='right', method='compare_all'),
                       tiles - 1).astype(jnp.int32)
    k = w - item_start[tile]
    live = (k < n_items[tile]).astype(jnp.int32)
    blk = jnp.minimum(first_blk[tile] + k, nblk - 1).astype(jnp.int32)

    def out_spec(lo_tile, n_tiles):
        return pl.BlockSpec(
            (tt, d), lambda i, tile_r, blk_r, live_r: (jnp.clip(tile_r[i] - lo_tile, 0, n_tiles - 1), 0))

    starts = [sum(split_tiles[:j]) for j in range(len(splits))]
    grid_spec = pltpu.PrefetchScalarGridSpec(
        num_scalar_prefetch=3,
        grid=(max_items,),
        in_specs=[pl.BlockSpec((tt, d), lambda i, tile_r, blk_r, live_r: (tile_r[i], 0)),
                  pl.BlockSpec((None, 1, rb), lambda i, tile_r, blk_r, live_r: (blk_r[i], 0, 0)),
                  pl.BlockSpec((rb, d), lambda i, tile_r, blk_r, live_r: (blk_r[i], 0))],
        out_specs=tuple(out_spec(s, t) for s, t in zip(starts, split_tiles)),
    )
    return pl.pallas_call(
        functools.partial(_combine_kernel, split_tiles=split_tiles),
        out_shape=tuple(jax.ShapeDtypeStruct((s, d), F32) for s in splits),
        grid_spec=grid_spec,
        compiler_params=_params("arbitrary"),
    )(tile, blk, live, x, tok_sorted.reshape(nblk, 1, rb), ye_sorted)


def _trunk(x, batch, seq_len, group_sizes, p):
    depth = p['norm_gain'].shape[0]
    for i in range(depth):
        mixer, j = i % 4, i // 4
        g = p['norm_gain'][i, 0]
        if mixer == 0:
            x = _hyena_layer(x, batch, seq_len, g, {k[3:]: v[j] for k, v in p.items() if k.startswith('hy_')})
        elif mixer == 1:
            x = _gqa_layer(x, batch, seq_len, g, {k[3:]: v[j] for k, v in p.items() if k.startswith('ga_')})
        elif mixer == 2:
            x = _mlstm_layer(x, batch, seq_len, g, {k[3:]: v[j] for k, v in p.items() if k.startswith('ml_')})
        else:
            x = _dilated_layer(x, batch, seq_len, g, {k[3:]: v[j] for k, v in p.items() if k.startswith('da_')})
        last = i == depth - 1
        outs = _moe_layer(x, group_sizes, p['norm_gain'][i, 1], p['moe_w_router'][i],
                          p['moe_w1'][i], p['moe_w3'][i], p['moe_w2'][i], split_output=last)
        x = outs if last else outs[0]
    return x


def kernel(x_prompt, x_sample, norm_gain, hy_w_in, hy_b_in, hy_conv_w, hy_conv_b, hy_f_w1, hy_f_b1, hy_f_w2, hy_f_b2, hy_f_w3, hy_f_b3, hy_f_freq, hy_decay, hy_skip, hy_w_out, hy_b_out, ga_w_qkv, ga_q_gain, ga_k_gain, ga_w_o, ml_w_up, ml_conv_w, ml_conv_b, ml_w_q, ml_w_k, ml_w_v, ml_w_gate, ml_b_gate, ml_norm_gain, ml_skip, ml_w_down, da_w_qkv, da_q_gain, da_k_gain, da_w_o, moe_w_router, moe_w1, moe_w3, moe_w2):
    p = dict(
        norm_gain=norm_gain,
        hy_w_in=hy_w_in, hy_b_in=hy_b_in, hy_conv_w=hy_conv_w, hy_conv_b=hy_conv_b,
        hy_f_w1=hy_f_w1, hy_f_b1=hy_f_b1, hy_f_w2=hy_f_w2, hy_f_b2=hy_f_b2,
        hy_f_w3=hy_f_w3, hy_f_b3=hy_f_b3, hy_f_freq=hy_f_freq, hy_decay=hy_decay,
        hy_skip=hy_skip, hy_w_out=hy_w_out, hy_b_out=hy_b_out,
        ga_w_qkv=ga_w_qkv, ga_q_gain=ga_q_gain, ga_k_gain=ga_k_gain, ga_w_o=ga_w_o,
        ml_w_up=ml_w_up, ml_conv_w=ml_conv_w, ml_conv_b=ml_conv_b, ml_w_q=ml_w_q,
        ml_w_k=ml_w_k, ml_w_v=ml_w_v, ml_w_gate=ml_w_gate, ml_b_gate=ml_b_gate,
        ml_norm_gain=ml_norm_gain, ml_skip=ml_skip, ml_w_down=ml_w_down,
        da_w_qkv=da_w_qkv, da_q_gain=da_q_gain, da_k_gain=da_k_gain, da_w_o=da_w_o,
        moe_w_router=moe_w_router, moe_w1=moe_w1, moe_w3=moe_w3, moe_w2=moe_w2,
    )
    bp, seq_len, d = x_prompt.shape
    bs = x_sample.shape[0]
    assert x_sample.shape[1] == seq_len
    x = jnp.concatenate([x_prompt, x_sample], axis=0).reshape((bp + bs) * seq_len, d)
    y_prompt, y_sample = _trunk(x, bp + bs, seq_len, (bp * seq_len, bs * seq_len), p)
    return (y_prompt.reshape(bp, seq_len, d), y_sample.reshape(bs, seq_len, d))
```

```python
import functools
import math

import jax
import jax.numpy as jnp
from jax import lax
from jax.experimental import pallas as pl
from jax.experimental.pallas import tpu as pltpu

F32 = jnp.float32
BF16 = jnp.bfloat16
HIGHEST = lax.Precision.HIGHEST

NORM_EPS = 1e-6
GRID_W = 64
HY_BANDS = 16
GA_HEADS = 8
GA_KV_HEADS = 2
GA_GROUP = GA_HEADS // GA_KV_HEADS
HEAD_DIM = 128
AXIAL_THETA = 10000.0
ML_HEADS = 4
ML_QKV_BLOCK = 4
DA_GROUPS = ((128, 1), (512, 4), (2048, 16))
DA_HEADS_PER_GROUP = 4
DA_HEADS = DA_HEADS_PER_GROUP * len(DA_GROUPS)
ROPE_THETA = 500000.0
ROPE_DIMS = HEAD_DIM // 4
N_EXPERTS = 16
EC_CAPACITY = 2

VMEM_LIMIT_BYTES = 52 * 1024 * 1024
HALO_ROWS = 8
MXU_DIM = 256
ML_CHUNK = 256
BAND_BLOCK = 128
BAND_HALF = 64
BAND_SUBBLOCKS = 4


def _params(*sem):
    return pltpu.CompilerParams(dimension_semantics=sem, vmem_limit_bytes=VMEM_LIMIT_BYTES)


def _rms(x, g):
    ms = jnp.mean(x * x, axis=-1, keepdims=True)
    return x * lax.rsqrt(ms + NORM_EPS) * g


def _const_spec(shape):
    nd = len(shape)
    return pl.BlockSpec(shape, lambda *_: (0,) * nd)


def _conv3_rows(p, pprev, pnext, cw, cb, rows, tm):
    up = jnp.where(rows == 0, pprev, pltpu.roll(p, 1, 0))
    dn = jnp.where(rows == tm - 1, pnext, pltpu.roll(p, tm - 1, 0))
    return up * cw[0:1] + p * cw[1:2] + dn * cw[2:3] + cb


def _halo_specs(tm, d, n_rows):
    hb = tm // HALO_ROWS
    last = n_rows // HALO_ROWS - 1
    prev = pl.BlockSpec((HALO_ROWS, d), lambda i: (jnp.maximum(i * hb - 1, 0), 0))
    nxt = pl.BlockSpec((HALO_ROWS, d), lambda i: (jnp.minimum((i + 1) * hb, last), 0))
    return prev, nxt


def _edge_scales(i, tm, seq_len):
    t0 = i * tm
    keep_prev = jnp.where(t0 % seq_len == 0, 0.0, 1.0).astype(F32)
    keep_next = jnp.where((t0 + tm) % seq_len == 0, 0.0, 1.0).astype(F32)
    return keep_prev, keep_next


def _route(x, g_ref, wr_ref, xn_ref, aff_ref):
    xn = _rms(x, g_ref[...])
    xn_ref[...] = xn.astype(BF16)
    logits = lax.dot_general(wr_ref[...], xn, (((1,), (1,)), ((), ())),
                             precision=HIGHEST, preferred_element_type=F32)
    m = jnp.max(logits, axis=0, keepdims=True)
    e = jnp.exp(logits - m)
    aff_ref[...] = e / jnp.sum(e, axis=0, keepdims=True)


def _mm_res_kernel(*refs, prologue, n_row, n_const, tn):
    row_refs = refs[:n_row]
    const_refs = refs[n_row:n_row + n_const]
    w_ref, b_ref, res_ref, g2_ref, wr_ref, o_ref, xn_ref, aff_ref = refs[n_row + n_const:]
    lhs = prologue(*[r[...] for r in row_refs], *[c[...] for c in const_refs]).astype(BF16)
    for j in range(o_ref.shape[1] // tn):
        sl = slice(j * tn, (j + 1) * tn)
        o_ref[:, sl] = (res_ref[:, sl] + b_ref[:, sl]
                        + jnp.dot(lhs, w_ref[:, sl], preferred_element_type=F32))
    _route(o_ref[...], g2_ref, wr_ref, xn_ref, aff_ref)


def _mm_res(prologue, rows, consts, w, b, res, route, tm=512, tn=512):
    n, dout = res.shape
    tm = min(tm, n)
    k = w.shape[0]
    g2, w_router = route
    ne = w_router.shape[1]
    if b is None:
        b = jnp.zeros((1, dout), F32)
    in_specs = [pl.BlockSpec((tm, r.shape[1]), lambda i: (i, 0)) if r.ndim == 2
                else pl.BlockSpec((r.shape[0], tm, r.shape[2]), lambda i: (0, i, 0)) for r in rows]
    in_specs += [_const_spec(c.shape) for c in consts]
    in_specs += [_const_spec((k, dout)), _const_spec((1, dout)),
                 pl.BlockSpec((tm, dout), lambda i: (i, 0)),
                 _const_spec((1, dout)), _const_spec((ne, dout))]
    tile = pl.BlockSpec((tm, dout), lambda i: (i, 0))
    return pl.pallas_call(
        functools.partial(_mm_res_kernel, prologue=prologue, n_row=len(rows),
                          n_const=len(consts), tn=min(tn, dout)),
        out_shape=(jax.ShapeDtypeStruct((n, dout), F32), jax.ShapeDtypeStruct((n, dout), BF16),
                   jax.ShapeDtypeStruct((ne, n), F32)),
        grid=(n // tm,),
        in_specs=in_specs,
        out_specs=(tile, tile, pl.BlockSpec((ne, tm), lambda i: (0, i))),
        compiler_params=_params("parallel"),
    )(*rows, *consts, w.astype(BF16), b.reshape(1, dout).astype(F32), res,
      g2.reshape(1, dout), w_router.T)


def _hyena_in_kernel(x_ref, xp_ref, xn_ref, g_ref, w_ref, b_ref, cw_ref, cb_ref,
                     x0_ref, vx_ref, *, seq_len, tm, d, cols):
    keep_prev, keep_next = _edge_scales(pl.program_id(0), tm, seq_len)
    g = g_ref[...]
    xb = _rms(x_ref[...], g).astype(BF16)
    hb = _rms(jnp.concatenate([xp_ref[...], xn_ref[...]], axis=0), g).astype(BF16)
    rows = lax.broadcasted_iota(jnp.int32, (tm, 1), 0)

    def conv_part(c0):
        sl = slice(c0, c0 + cols)
        w = w_ref[:, sl]
        bias = b_ref[:, sl]
        p = jnp.dot(xb, w, preferred_element_type=F32) + bias
        ph = jnp.dot(hb, w, preferred_element_type=F32) + bias
        pprev = ph[HALO_ROWS - 1:HALO_ROWS, :] * keep_prev
        pnext = ph[HALO_ROWS:HALO_ROWS + 1, :] * keep_next
        return _conv3_rows(p, pprev, pnext, cw_ref[:, sl], cb_ref[:, sl], rows, tm)

    for j in range(d // cols):
        c = j * cols
        x0_ref[:, c:c + cols] = conv_part(c)
        vx_ref[:, c:c + cols] = conv_part(2 * d + c) * conv_part(d + c)


def _hyena_in(x, g, w_in, b_in, conv_w, conv_b, seq_len, tm=256, cols=512):
    n, d = x.shape
    tm = min(tm, seq_len)
    prev, nxt = _halo_specs(tm, d, n)
    row = pl.BlockSpec((tm, d), lambda i: (i, 0))
    return pl.pallas_call(
        functools.partial(_hyena_in_kernel, seq_len=seq_len, tm=tm, d=d, cols=cols),
        out_shape=(jax.ShapeDtypeStruct((n, d), F32), jax.ShapeDtypeStruct((n, d), F32)),
        grid=(n // tm,),
        in_specs=[row, prev, nxt, _const_spec((1, d)), _const_spec((d, 3 * d)),
                  _const_spec((1, 3 * d)), _const_spec((3, 3 * d)), _const_spec((1, 3 * d))],
        out_specs=(row, row),
        compiler_params=_params("parallel"),
    )(x, x, x, g.reshape(1, d), w_in.astype(BF16), b_in.reshape(1, 3 * d),
      conv_w, conv_b.reshape(1, 3 * d))


def _hyena_filter_taps(L, d, f_w1, f_b1, f_w2, f_b2, f_w3, f_b3, f_freq, decay):
    t = jnp.linspace(0.0, 1.0, L, dtype=F32)[:, None]
    w_ang = 2.0 * math.pi * jnp.arange(L, dtype=F32)[:, None] / L
    bands = jnp.linspace(1e-4, HY_BANDS - 1, HY_BANDS, dtype=F32)[None, :]
    z = jnp.concatenate([t, jnp.cos(bands * w_ang), -jnp.sin(bands * w_ang)], axis=-1)

    def branch(zz, tt, col0):
        h = jnp.sin(f_freq[0] * (zz @ f_w1 + f_b1))
        h = jnp.sin(f_freq[1] * (h @ f_w2 + f_b2))
        h = h @ f_w3[:, col0:col0 + d] + f_b3[col0:col0 + d]
        return h * jnp.exp(-tt * jnp.abs(decay[col0 // d])[None])

    h_fwd = branch(z, t, 0)
    h_bwd_rev = branch(z[::-1], t[::-1], d)
    k = jnp.concatenate([h_fwd, jnp.zeros((1, d), F32), h_bwd_rev[:-1]], axis=0)
    return k / jnp.sum(jnp.abs(k), axis=0, keepdims=True)


def _fft_dims(m):
    lg = m.bit_length() - 1
    p = 1 << ((lg + 1) // 2)
    return p, m // p


def _split_bf16(x):
    hi = x.astype(BF16)
    return hi, (x - hi.astype(F32)).astype(BF16)


def _mm_split(fh, fl, x, precise):
    if not precise:
        return jnp.dot(fh, x.astype(BF16), preferred_element_type=F32)
    xh, xl = _split_bf16(x)
    return (jnp.dot(fh, xh, preferred_element_type=F32) + jnp.dot(fl, xh, preferred_element_type=F32)
            + jnp.dot(fh, xl, preferred_element_type=F32))


def _cplx_as_real(cr, ci):
    top = jnp.concatenate([cr, -ci], axis=-1)
    bot = jnp.concatenate([ci, cr], axis=-1)
    return jnp.concatenate([top, bot], axis=-2)


def _unit_circle(idx, m):
    ang = (2.0 * math.pi / m) * idx.astype(F32)
    return jnp.cos(ang), jnp.sin(ang)


def _dft_consts(p, q):
    m = p * q
    k1 = jnp.arange(p, dtype=jnp.int32)
    n1 = jnp.arange(p // 2, dtype=jnp.int32)
    c, s = _unit_circle((k1[:, None] * n1[None, :]) % p, p)
    fa = _cplx_as_real(c, -s)
    c, s = _unit_circle((n1[:, None] * k1[None, :]) % p, p)
    fd = _cplx_as_real(c / m, s / m)
    k2 = jnp.arange(q, dtype=jnp.int32)
    n2 = jnp.arange(q, dtype=jnp.int32)
    idx = (n2[None, None, :] * (k2[None, :, None] * p + k1[:, None, None])) % m
    c, s = _unit_circle(idx, m)
    gb = _cplx_as_real(c, -s)
    ct, st = jnp.swapaxes(c, 1, 2), jnp.swapaxes(s, 1, 2)
    gc = _cplx_as_real(ct, st)
    return tuple(_split_bf16(a) for a in (fa, gb, gc, fd))


def _fft_a_kernel(x_ref, fh_ref, fl_ref, o_ref, *, precise):
    _, rows_in, group, d = x_ref.shape
    rows_out = o_ref.shape[1]
    for j in range(group):
        x = x_ref[:, :, j, :].reshape(2 * rows_in, d)
        y = _mm_split(fh_ref[...], fl_ref[...], x, precise)
        o_ref[:, :, j, :] = y.reshape(2, rows_out, d)


def _fft_b_kernel(a_ref, gh_ref, gl_ref, o_ref):
    _, q, d = a_ref.shape
    x = _mm_split(gh_ref[...], gl_ref[...], a_ref[...].reshape(2 * q, d), True)
    o_ref[...] = x.reshape(o_ref.shape)


def _fft_bc_kernel(a_ref, gb_ref, gc_ref, k_ref, z_ref):
    _, q, d = a_ref.shape
    x = _mm_split(gb_ref[...], None, a_ref[...].reshape(2 * q, d), False)
    xr, xi = x[:q], x[q:]
    kr, ki = k_ref[0], k_ref[1]
    y = jnp.concatenate([xr * kr - xi * ki, xr * ki + xi * kr], axis=0)
    z = _mm_split(gc_ref[...], None, y, False)
    z_ref[...] = z.reshape(z_ref.shape)


def _fft_rows(x5, f, rows_out, precise):
    pairs, _, rows_in, q, d = x5.shape
    blk = lambda r: pl.BlockSpec((None, 2, r, HALO_ROWS, d), lambda b, j: (b, 0, 0, j, 0))
    return pl.pallas_call(
        functools.partial(_fft_a_kernel, precise=precise),
        out_shape=jax.ShapeDtypeStruct((pairs, 2, rows_out, q, d), F32),
        grid=(pairs, q // HALO_ROWS),
        in_specs=[blk(rows_in), _const_spec(f[0].shape), _const_spec(f[1].shape)],
        out_specs=blk(rows_out),
        compiler_params=_params("parallel", "parallel"),
    )(x5, *f)


def _long_conv(vx, taps, batch, seq_len):
    n, d = vx.shape
    m = 2 * seq_len
    p, q = _fft_dims(m)
    assert batch % 2 == 0
    pairs = batch // 2
    fa, gb, gc, fd = _dft_consts(p, q)
    g_spec = pl.BlockSpec((None, 2 * q, 2 * q), lambda k1, b: (k1, 0, 0))
    slab = pl.BlockSpec((None, 2, q, d), lambda k1, b: (b, 0, k1, 0))

    zeros = jnp.zeros((seq_len, d), F32)
    kin = jnp.stack([taps[:seq_len], zeros, taps[seq_len:], zeros]).reshape(2, 2, p // 2, q, d)
    ka = _fft_rows(kin, fa, p, True).reshape(2, 2, p * q, d)
    kx = pl.pallas_call(
        _fft_b_kernel,
        out_shape=jax.ShapeDtypeStruct((2, 2, p * q, d), F32),
        grid=(p, 2),
        in_specs=[slab, g_spec, g_spec],
        out_specs=slab,
        compiler_params=_params("parallel", "parallel"),
    )(ka, *gb)
    sign = jnp.repeat(1.0 - 2.0 * (jnp.arange(p) % 2).astype(F32), q)[None, :, None]
    kspec = kx[0] + sign * kx[1]

    xa = _fft_rows(vx.reshape(pairs, 2, p // 2, q, d), fa, p, False).reshape(pairs, 2, p * q, d)
    z = pl.pallas_call(
        _fft_bc_kernel,
        out_shape=jax.ShapeDtypeStruct((pairs, 2, p * q, d), F32),
        grid=(p, pairs),
        in_specs=[slab, g_spec, g_spec, pl.BlockSpec((2, q, d), lambda k1, b: (0, k1, 0))],
        out_specs=slab,
        compiler_params=_params("parallel", "parallel"),
    )(xa, gb[0], gc[0], kspec)
    y = _fft_rows(z.reshape(pairs, 2, p, q, d), fd, p // 2, False)
    return y.reshape(n, d)


def _hyena_out_prologue(y, vx, x0, skip):
    return (y + vx * skip) * x0


def _hyena_layer(x, batch, seq_len, g, p, route):
    n, d = x.shape
    x0, vx = _hyena_in(x, g, p['w_in'], p['b_in'], p['conv_w'], p['conv_b'], seq_len)
    taps = _hyena_filter_taps(seq_len, d, p['f_w1'], p['f_b1'], p['f_w2'], p['f_b2'],
                              p['f_w3'], p['f_b3'], p['f_freq'], p['decay'])
    y = _long_conv(vx, taps, batch, seq_len)
    return _mm_res(_hyena_out_prologue, [y, vx, x0], [p['skip'].reshape(1, d)],
                   p['w_out'], p['b_out'], x, route)


def _head_norm_rope(xh, gain, cos, sin, half):
    y = _rms(xh, gain)
    lane = lax.broadcasted_iota(jnp.int32, (1, HEAD_DIM), 1)
    fwd = pltpu.roll(y, HEAD_DIM - half, 1)
    bwd = pltpu.roll(y, half, 1)
    partner = jnp.where((lane % (2 * half)) < half, fwd, bwd)
    return y * cos + partner * sin


def _qkv_rope_kernel(x_ref, g_ref, w_ref, qg_ref, kg_ref, cos_ref, sin_ref,
                     q_ref, k_ref, v_ref, *, nq, nk, nv, half):
    xb = _rms(x_ref[...], g_ref[...]).astype(BF16)
    cos = cos_ref[...]
    sin = sin_ref[...]
    per = MXU_DIM // HEAD_DIM
    for h0 in range(0, nq + nk + nv, per):
        pw = jnp.dot(xb, w_ref[:, h0 * HEAD_DIM:(h0 + per) * HEAD_DIM], preferred_element_type=F32)
        for h in range(h0, h0 + per):
            ph = pw[:, (h - h0) * HEAD_DIM:(h - h0 + 1) * HEAD_DIM]
            if h < nq:
                sl = slice(h * HEAD_DIM, (h + 1) * HEAD_DIM)
                q_ref[:, sl] = _head_norm_rope(ph, qg_ref[...], cos, sin, half).astype(BF16)
            elif h < nq + nk:
                sl = slice((h - nq) * HEAD_DIM, (h - nq + 1) * HEAD_DIM)
                k_ref[:, sl] = _head_norm_rope(ph, kg_ref[...], cos, sin, half).astype(BF16)
            else:
                sl = slice((h - nq - nk) * HEAD_DIM, (h - nq - nk + 1) * HEAD_DIM)
                v_ref[:, sl] = ph.astype(BF16)


def _qkv_rope(x, g, w_qkv, q_gain, k_gain, cos, sin, nq, nk, nv, half, seq_len, tm=512):
    n, d = x.shape
    tm = min(tm, seq_len)
    pos_blocks = seq_len // tm
    f = w_qkv.shape[1]
    row = pl.BlockSpec((tm, d), lambda i: (i, 0))
    tab = pl.BlockSpec((tm, HEAD_DIM), lambda i: (i % pos_blocks, 0))
    outs = tuple(jax.ShapeDtypeStruct((n, c * HEAD_DIM), BF16) for c in (nq, nk, nv))
    return pl.pallas_call(
        functools.partial(_qkv_rope_kernel, nq=nq, nk=nk, nv=nv, half=half),
        out_shape=outs,
        grid=(n // tm,),
        in_specs=[row, _const_spec((1, d)), _const_spec((d, f)),
                  _const_spec((1, HEAD_DIM)), _const_spec((1, HEAD_DIM)), tab, tab],
        out_specs=tuple(pl.BlockSpec((tm, c * HEAD_DIM), lambda i: (i, 0)) for c in (nq, nk, nv)),
        compiler_params=_params("parallel"),
    )(x, g.reshape(1, d), w_qkv.astype(BF16), q_gain.reshape(1, HEAD_DIM),
      k_gain.reshape(1, HEAD_DIM), cos, sin)


def _axial_tables(L):
    t = jnp.arange(L)
    r = (t // GRID_W).astype(F32)
    c = (t % GRID_W).astype(F32)
    nf = HEAD_DIM // 4
    inv = AXIAL_THETA ** (-(2.0 * jnp.arange(nf, dtype=F32)) / (2 * nf))
    ar, ac = r[:, None] * inv[None], c[:, None] * inv[None]
    cos = jnp.concatenate([jnp.cos(ar), jnp.cos(ar), jnp.cos(ac), jnp.cos(ac)], axis=-1)
    sin = jnp.concatenate([-jnp.sin(ar), jnp.sin(ar), -jnp.sin(ac), jnp.sin(ac)], axis=-1)
    return cos, sin


FLASH_SAFE_BOUND = 40.0
FLASH_BOUND_MARGIN = 1.001


def _flash_kernel(q_ref, k_ref, v_ref, o_ref, acc_scr, off_scr, lsum_scr, m_scr, l_scr, kn_scr,
                  *, tq, tk, tkf, seq_len, group):
    scale = HEAD_DIM ** -0.5
    c = scale * math.log2(math.e)

    @pl.when(pl.program_id(2) == 0)
    def _():
        def norm_step(j, mx):
            kt = k_ref[pl.ds(pl.multiple_of(j * tk, tk), tk), :].astype(F32)
            row = jnp.sum(kt * kt, axis=1, keepdims=True)
            return jnp.maximum(mx, jnp.max(row, axis=0, keepdims=True))

        kmax2 = lax.fori_loop(0, seq_len // tk, norm_step, jnp.zeros((1, 1), F32))
        kn_scr[...] = jnp.broadcast_to(kmax2, kn_scr.shape)

    kmax2 = kn_scr[0:1, 0:1]
    bmax = jnp.zeros((1, 1), F32)
    for h in range(group):
        qf = q_ref[:, h * HEAD_DIM:(h + 1) * HEAD_DIM].astype(F32)
        b = jnp.sqrt(jnp.sum(qf * qf, axis=1, keepdims=True) * kmax2) * FLASH_BOUND_MARGIN
        bmax = jnp.maximum(bmax, jnp.max(b, axis=0, keepdims=True))
        off_scr[h] = jnp.broadcast_to(b * c, (tq, HEAD_DIM))
    fast = (bmax * scale)[0, 0] <= FLASH_SAFE_BOUND

    @pl.when(fast)
    def _():
        acc_scr[...] = jnp.zeros_like(acc_scr)
        lsum_scr[...] = jnp.zeros_like(lsum_scr)
        n_lane_tiles = tkf // HEAD_DIM

        def body(j, carry):
            start = pl.multiple_of(j * tkf, tkf)
            kt = k_ref[pl.ds(start, tkf), :]
            vt = v_ref[pl.ds(start, tkf), :]
            for h in range(group):
                q = q_ref[:, h * HEAD_DIM:(h + 1) * HEAD_DIM]
                s = lax.dot_general(q, kt, (((1,), (1,)), ((), ())), preferred_element_type=F32)
                off = off_scr[h]
                p = jnp.exp2(s * c - jnp.concatenate([off] * n_lane_tiles, axis=1))
                part = p[:, 0:HEAD_DIM]
                for t in range(1, n_lane_tiles):
                    part = part + p[:, t * HEAD_DIM:(t + 1) * HEAD_DIM]
                lsum_scr[h] += part
                acc_scr[h] += jnp.dot(p.astype(BF16), vt, preferred_element_type=F32)
            return carry

        lax.fori_loop(0, seq_len // tkf, body, 0)
        for h in range(group):
            l = jnp.sum(lsum_scr[h], axis=1, keepdims=True)
            o_ref[:, h * HEAD_DIM:(h + 1) * HEAD_DIM] = (acc_scr[h] / l).astype(o_ref.dtype)

    @pl.when(jnp.logical_not(fast))
    def _():
        m_scr[...] = jnp.full(m_scr.shape, -jnp.inf, F32)
        l_scr[...] = jnp.zeros_like(l_scr)
        acc_scr[...] = jnp.zeros_like(acc_scr)

        def body(j, carry):
            start = pl.multiple_of(j * tk, tk)
            kt = k_ref[pl.ds(start, tk), :]
            vt = v_ref[pl.ds(start, tk), :]
            for h in range(group):
                q = q_ref[:, h * HEAD_DIM:(h + 1) * HEAD_DIM]
                s = lax.dot_general(q, kt, (((1,), (1,)), ((), ())), preferred_element_type=F32)
                m = m_scr[h]
                m_new = jnp.maximum(m, jnp.max(s, axis=1, keepdims=True))
                alpha = jnp.exp2((m - m_new) * c)
                p = jnp.exp2(s * c - m_new * c)
                l_scr[h] = alpha * l_scr[h] + jnp.sum(p, axis=1, keepdims=True)
                acc_scr[h] = alpha * acc_scr[h] + jnp.dot(p.astype(BF16), vt, preferred_element_type=F32)
                m_scr[h] = m_new
            return carry

        lax.fori_loop(0, seq_len // tk, body, 0)
        for h in range(group):
            o_ref[:, h * HEAD_DIM:(h + 1) * HEAD_DIM] = (acc_scr[h] / l_scr[h]).astype(o_ref.dtype)


def _flash_gqa(q, k, v, batch, seq_len, tq=1024, tk=512, tkf=1024):
    tq = min(tq, seq_len)
    tk = min(tk, seq_len)
    tkf = min(tkf, seq_len)
    nq = seq_len // tq
    gw = GA_GROUP * HEAD_DIM
    wide = pltpu.VMEM((GA_GROUP, tq, HEAD_DIM), F32)
    thin = pltpu.VMEM((GA_GROUP, tq, 1), F32)
    return pl.pallas_call(
        functools.partial(_flash_kernel, tq=tq, tk=tk, tkf=tkf, seq_len=seq_len, group=GA_GROUP),
        out_shape=jax.ShapeDtypeStruct(q.shape, BF16),
        grid=(batch, GA_KV_HEADS, nq),
        in_specs=[pl.BlockSpec((tq, gw), lambda b, kv, i: (b * nq + i, kv)),
                  pl.BlockSpec((seq_len, HEAD_DIM), lambda b, kv, i: (b, kv)),
                  pl.BlockSpec((seq_len, HEAD_DIM), lambda b, kv, i: (b, kv))],
        out_specs=pl.BlockSpec((tq, gw), lambda b, kv, i: (b * nq + i, kv)),
        scratch_shapes=[wide, wide, wide, thin, thin, pltpu.VMEM((HALO_ROWS, HEAD_DIM), F32)],
        compiler_params=_params("parallel", "parallel", "arbitrary"),
    )(q, k, v)


def _identity_prologue(o):
    return o


def _gqa_layer(x, batch, seq_len, g, p, route):
    cos, sin = _axial_tables(seq_len)
    q, k, v = _qkv_rope(x, g, p['w_qkv'], p['q_gain'], p['k_gain'], cos, sin,
                        GA_HEADS, GA_KV_HEADS, GA_KV_HEADS, HEAD_DIM // 4, seq_len)
    o = _flash_gqa(q, k, v, batch, seq_len)
    return _mm_res(_identity_prologue, [o], [], p['w_o'], None, x, route)


def _ml_in_kernel(x_ref, xp_ref, xn_ref, g_ref, w_ref, cw_ref, cb_ref, wq_ref, wk_ref, wv_ref,
                  wg_ref, bg_ref, q_ref, k_ref, v_ref, xc_ref, sz_ref, gate_ref,
                  *, seq_len, tm, inner, k_scale):
    keep_prev, keep_next = _edge_scales(pl.program_id(0), tm, seq_len)
    g = g_ref[...]
    xb = _rms(x_ref[...], g).astype(BF16)
    hb = _rms(jnp.concatenate([xp_ref[...], xn_ref[...]], axis=0), g).astype(BF16)
    rows = lax.broadcasted_iota(jnp.int32, (tm, 1), 0)
    gacc = jnp.zeros(gate_ref.shape, F32)
    for t in range(inner // MXU_DIM):
        sl = slice(t * MXU_DIM, (t + 1) * MXU_DIM)
        w = w_ref[:, sl]
        xm = jnp.dot(xb, w, preferred_element_type=F32)
        xh = jnp.dot(hb, w, preferred_element_type=F32)
        pprev = xh[HALO_ROWS - 1:HALO_ROWS, :] * keep_prev
        pnext = xh[HALO_ROWS:HALO_ROWS + 1, :] * keep_next
        xc = _conv3_rows(xm, pprev, pnext, cw_ref[:, sl], cb_ref[:, sl], rows, tm)
        xc = xc * jax.nn.sigmoid(xc)
        z = jnp.dot(xb, w_ref[:, inner + t * MXU_DIM:inner + (t + 1) * MXU_DIM],
                    preferred_element_type=F32)
        xcb = xc.astype(BF16)
        q = jnp.dot(xcb, wq_ref[t], preferred_element_type=F32)
        k = jnp.dot(xcb, wk_ref[t], preferred_element_type=F32)
        v = jnp.dot(xm.astype(BF16), wv_ref[t], preferred_element_type=F32)
        qb, kb, vb = q.astype(BF16), k.astype(BF16), v.astype(BF16)
        gacc += (jnp.dot(qb, wg_ref[0, sl, :], preferred_element_type=F32)
                 + jnp.dot(kb, wg_ref[1, sl, :], preferred_element_type=F32)
                 + jnp.dot(vb, wg_ref[2, sl, :], preferred_element_type=F32))
        q_ref[:, sl] = qb
        k_ref[:, sl] = (k * k_scale).astype(BF16)
        v_ref[:, sl] = vb
        xc_ref[:, sl] = xc
        sz_ref[:, sl] = z * jax.nn.sigmoid(z)
    gate_ref[...] = gacc + bg_ref[...]


def _block_diag_tiles(w):
    nb, c, _ = w.shape
    per = MXU_DIM // c
    wt = w.reshape(nb // per, per, c, c)
    eye = jnp.eye(per, dtype=w.dtype)
    full = jnp.einsum('tpcd,pq->tpcqd', wt, eye)
    return full.reshape(nb // per, MXU_DIM, MXU_DIM)


def _ml_in(x, g, p, seq_len, tm=256):
    n, d = x.shape
    tm = min(tm, seq_len)
    inner = p['w_up'].shape[1] // 2
    ng = 4 * ML_HEADS
    dh = inner // ML_HEADS
    wq, wk, wv = (_block_diag_tiles(p[nm]).astype(BF16) for nm in ('w_q', 'w_k', 'w_v'))
    wg = jnp.transpose(p['w_gate'], (1, 2, 0, 3)).reshape(3, inner, ng).astype(BF16)
    bg = p['b_gate'].reshape(1, ng)
    prev, nxt = _halo_specs(tm, d, n)
    row = pl.BlockSpec((tm, d), lambda i: (i, 0))
    wide = pl.BlockSpec((tm, inner), lambda i: (i, 0))
    nt = inner // MXU_DIM
    return pl.pallas_call(
        functools.partial(_ml_in_kernel, seq_len=seq_len, tm=tm, inner=inner, k_scale=dh ** -0.5),
        out_shape=(jax.ShapeDtypeStruct((n, inner), BF16),) * 3
        + (jax.ShapeDtypeStruct((n, inner), F32),) * 2
        + (jax.ShapeDtypeStruct((n, ng), F32),),
        grid=(n // tm,),
        in_specs=[row, prev, nxt, _const_spec((1, d)), _const_spec((d, 2 * inner)),
                  _const_spec((3, inner)), _const_spec((1, inner)),
                  _const_spec((nt, MXU_DIM, MXU_DIM)), _const_spec((nt, MXU_DIM, MXU_DIM)),
                  _const_spec((nt, MXU_DIM, MXU_DIM)), _const_spec((3, inner, ng)),
                  _const_spec((1, ng))],
        out_specs=(wide,) * 5 + (pl.BlockSpec((tm, ng), lambda i: (i, 0)),),
        compiler_params=_params("parallel"),
    )(x, x, x, g.reshape(1, d), p['w_up'].astype(BF16), p['conv_w'],
      p['conv_b'].reshape(1, inner), wq, wk, wv, wg, bg)


def _log_sigmoid(x):
    return jnp.minimum(x, 0.0) - jnp.log1p(jnp.exp(-jnp.abs(x)))


def _mlstm_chunk_kernel(qf_ref, kf_ref, vf_ref, gcf_ref, grf_ref, qb_ref, kb_ref, vb_ref, gcb_ref, grb_ref,
                        hf_ref, hb_ref, c_scr, n_scr, m_scr, *, lc):
    @pl.when(pl.program_id(2) == 0)
    def _():
        c_scr[...] = jnp.zeros_like(c_scr)
        n_scr[...] = jnp.zeros_like(n_scr)
        m_scr[...] = jnp.zeros_like(m_scr)

    _mlstm_step(qf_ref, kf_ref, vf_ref, gcf_ref, grf_ref, hf_ref, c_scr.at[0], n_scr.at[0], m_scr.at[0], lc, False)
    _mlstm_step(qb_ref, kb_ref, vb_ref, gcb_ref, grb_ref, hb_ref, c_scr.at[1], n_scr.at[1], m_scr.at[1], lc, True)


def _mlstm_step(q_ref, k_ref, v_ref, gc_ref, gr_ref, h_ref, c_scr, n_scr, m_scr, lc, backward):
    q = q_ref[...]
    k = k_ref[...]
    v = v_ref[...]
    gc = gc_ref[...]
    gr = gr_ref[...]
    i_col, g_col = gc[:, 0:1], gc[:, 1:2]
    i_row, g_row = gr[0:1, :], gr[1:2, :]

    jr = lax.broadcasted_iota(jnp.int32, (lc, lc), 0)
    sc = lax.broadcasted_iota(jnp.int32, (lc, lc), 1)
    seen = (sc >= jr) if backward else (sc <= jr)
    g_tot = g_row[:, 0:1] if backward else g_row[:, lc - 1:lc]
    m_old = m_scr[0:1, 0:1]

    dmat = jnp.where(seen, g_col - g_row + i_row, -jnp.inf)
    inter = g_col + m_old
    m_q = jnp.maximum(inter, jnp.max(dmat, axis=1, keepdims=True))
    s_qk = lax.dot_general(q, k, (((1,), (1,)), ((), ())), preferred_element_type=F32)
    a = s_qk * jnp.exp(dmat - m_q)
    w_int = jnp.exp(inter - m_q)
    q_c = jnp.dot(q, c_scr[...].astype(BF16), preferred_element_type=F32)
    num = jnp.dot(a.astype(BF16), v, preferred_element_type=F32) + q_c * w_int
    q_n = jnp.sum(q.astype(F32) * n_scr[...], axis=1, keepdims=True)
    den = jnp.sum(a, axis=1, keepdims=True) + w_int * q_n
    den = jnp.maximum(jnp.abs(den), jnp.exp(-m_q))
    h_ref[...] = num / den

    a_row = g_tot - g_row + i_row
    m_new = jnp.maximum(g_tot + m_old, jnp.max(a_row, axis=1, keepdims=True))
    ws_col = jnp.exp(g_tot - g_col + i_col - m_new)
    dec = jnp.exp(g_tot + m_old - m_new)
    kw = k.astype(F32) * ws_col
    upd = lax.dot_general(kw.astype(BF16), v, (((0,), (0,)), ((), ())), preferred_element_type=F32)
    c_scr[...] = dec * c_scr[...] + upd
    n_scr[...] = dec * n_scr[...] + jnp.sum(kw, axis=0, keepdims=True)
    m_scr[...] = jnp.broadcast_to(m_new, m_scr.shape)


def _ml_gate_kernel(g_ref, o_ref, *, lc, nh):
    g = g_ref[...]
    jr = lax.broadcasted_iota(jnp.int32, (lc, lc), 0)
    sc = lax.broadcasted_iota(jnp.int32, (lc, lc), 1)
    tril = jnp.where(sc <= jr, 1.0, 0.0).astype(BF16)
    triu = jnp.where(sc >= jr, 1.0, 0.0).astype(BF16)
    col = lax.broadcasted_iota(jnp.int32, (1, 4 * nh), 1)
    lf = _log_sigmoid(g)
    x1 = lf.astype(BF16)
    r1 = lf - x1.astype(F32)
    x2 = r1.astype(BF16)
    x3 = (r1 - x2.astype(F32)).astype(BF16)
    cum = lambda m: (jnp.dot(m, x1, preferred_element_type=F32) + jnp.dot(m, x2, preferred_element_type=F32)
                     + jnp.dot(m, x3, preferred_element_type=F32))
    gsum = jnp.where(col < 2 * nh, cum(tril), cum(triu))
    o_ref[...] = jnp.where((col % (2 * nh)) >= nh, gsum, g)


def _mlstm_chunks(q, k, v, gates, batch, seq_len):
    n, inner = q.shape
    dh = inner // ML_HEADS
    lc = min(ML_CHUNK, seq_len)
    nc = seq_len // lc
    ng = gates.shape[1]
    gates = pl.pallas_call(
        functools.partial(_ml_gate_kernel, lc=lc, nh=ML_HEADS),
        out_shape=jax.ShapeDtypeStruct((n, ng), F32),
        grid=(n // lc,),
        in_specs=[pl.BlockSpec((lc, ng), lambda i: (i, 0))],
        out_specs=pl.BlockSpec((lc, ng), lambda i: (i, 0)),
        compiler_params=_params("parallel"),
    )(gates)
    g4 = gates.reshape(n, 2, 2, ML_HEADS)
    gcol = jnp.transpose(g4, (1, 3, 0, 2))
    grow = jnp.transpose(g4, (1, 3, 2, 0))

    def specs(dd):
        chunk = (lambda b, c: b * nc + c) if dd == 0 else (lambda b, c: b * nc + nc - 1 - c)
        qkv = pl.BlockSpec((lc, dh), lambda b, h, c: (chunk(b, c), h))
        gc = pl.BlockSpec((None, None, lc, 2), lambda b, h, c: (dd, h, chunk(b, c), 0))
        gr = pl.BlockSpec((None, None, 2, lc), lambda b, h, c: (dd, h, 0, chunk(b, c)))
        return [qkv, qkv, qkv, gc, gr], qkv

    in_f, out_f = specs(0)
    in_b, out_b = specs(1)
    return pl.pallas_call(
        functools.partial(_mlstm_chunk_kernel, lc=lc),
        out_shape=(jax.ShapeDtypeStruct((n, inner), F32),) * 2,
        grid=(batch, ML_HEADS, nc),
        in_specs=in_f + in_b,
        out_specs=(out_f, out_b),
        scratch_shapes=[pltpu.VMEM((2, dh, dh), F32), pltpu.VMEM((2, 1, dh), F32),
                        pltpu.VMEM((2, HALO_ROWS, HEAD_DIM), F32)],
        compiler_params=_params("parallel", "parallel", "arbitrary"),
    )(q, k, v, gcol, grow, q, k, v, gcol, grow)


def _ml_out_prologue(hf, hb, xc, sz, gain, skip):
    h = hf + hb
    dh = h.shape[1] // ML_HEADS
    parts = []
    for i in range(ML_HEADS):
        sl = slice(i * dh, (i + 1) * dh)
        parts.append(_rms(h[:, sl], gain[:, sl]))
    hn = jnp.concatenate(parts, axis=1)
    return (hn + skip * xc) * sz


def _mlstm_layer(x, batch, seq_len, g, p, route):
    q, k, v, xc, sz, gates = _ml_in(x, g, p, seq_len)
    inner = q.shape[1]
    hf, hb = _mlstm_chunks(q, k, v, gates, batch, seq_len)
    return _mm_res(_ml_out_prologue, [hf, hb, xc, sz],
                   [p['norm_gain'].reshape(1, inner), p['skip'].reshape(1, inner)],
                   p['w_down'], None, x, route, tm=256)


def _rope_tables(L):
    inv = ROPE_THETA ** (-(2.0 * jnp.arange(ROPE_DIMS // 2, dtype=F32)) / ROPE_DIMS)
    ang = jnp.arange(L, dtype=F32)[:, None] * inv[None]
    pad = HEAD_DIM - ROPE_DIMS
    cos = jnp.concatenate([jnp.cos(ang), jnp.cos(ang), jnp.ones((L, pad), F32)], axis=-1)
    sin = jnp.concatenate([-jnp.sin(ang), jnp.sin(ang), jnp.zeros((L, pad), F32)], axis=-1)
    return cos, sin


def _band_kernel(q_ref, kp_ref, kc_ref, kn_ref, vp_ref, vc_ref, vn_ref, o_ref, lse_ref,
                 *, s_len, heads, subs):
    i = pl.program_id(2)
    qb = BAND_BLOCK
    w = qb + 2 * BAND_HALF
    a = lax.broadcasted_iota(jnp.int32, (qb, w), 0)
    c = lax.broadcasted_iota(jnp.int32, (qb, w), 1)
    in_band = jnp.abs(c - BAND_HALF - a) <= BAND_HALF
    scale = HEAD_DIM ** -0.5
    for h in range(heads):
        sl = slice(h * HEAD_DIM, (h + 1) * HEAD_DIM)
        kw = jnp.concatenate([kp_ref[qb - BAND_HALF:, sl], kc_ref[:, sl], kn_ref[:BAND_HALF, sl]], axis=0)
        vw = jnp.concatenate([vp_ref[qb - BAND_HALF:, sl], vc_ref[:, sl], vn_ref[:BAND_HALF, sl]], axis=0)
        for u in range(subs):
            rows = slice(u * qb, (u + 1) * qb)
            key_pos = (i * subs + u) * qb - BAND_HALF + c
            valid = in_band & (key_pos >= 0) & (key_pos < s_len)
            s = lax.dot_general(q_ref[rows, sl], kw[u * qb:u * qb + w], (((1,), (1,)), ((), ())),
                                preferred_element_type=F32) * scale
            s = jnp.where(valid, s, -jnp.inf)
            m = jnp.max(s, axis=1, keepdims=True)
            p = jnp.exp(s - m)
            l = jnp.sum(p, axis=1, keepdims=True)
            o = jnp.dot(p.astype(BF16), vw[u * qb:u * qb + w], preferred_element_type=F32)
            o_ref[rows, sl] = o / l
            lse_ref[rows, sl] = jnp.broadcast_to(m + jnp.log(l), (qb, HEAD_DIM))


def _da_qkv_kernel(x_ref, g_ref, w_ref, qg_ref, kg_ref, cos_ref, sin_ref, *refs, dils, tm, half):
    n_perm = sum(1 for d in dils if d > 1)
    perm_refs, out_refs = refs[:n_perm], refs[n_perm:]
    xb = _rms(x_ref[...], g_ref[...]).astype(BF16)
    cos, sin = cos_ref[...], sin_ref[...]
    gw = DA_HEADS_PER_GROUP * HEAD_DIM
    heads = DA_HEADS_PER_GROUP * len(dils)
    for kind in range(3):
        gain = (qg_ref, kg_ref, None)[kind]
        pi = 0
        for gi, dil in enumerate(dils):
            c0 = (kind * heads + gi * DA_HEADS_PER_GROUP) * HEAD_DIM
            ph = jnp.dot(xb, w_ref[:, c0:c0 + gw], preferred_element_type=F32)
            if gain is not None:
                ph = jnp.concatenate(
                    [_head_norm_rope(ph[:, h * HEAD_DIM:(h + 1) * HEAD_DIM], gain[...], cos, sin, half)
                     for h in range(DA_HEADS_PER_GROUP)], axis=1)
            val = ph.astype(BF16)
            o_ref = out_refs[kind * len(dils) + gi]
            if dil == 1:
                o_ref[...] = val
            else:
                pv = jnp.dot(perm_refs[pi][...], val, preferred_element_type=F32).astype(BF16)
                pi += 1
                rows = tm // dil
                for r in range(dil):
                    o_ref[:, r * gw:(r + 1) * gw] = pv[r * rows:(r + 1) * rows, :]


def _da_qkv(x, g, w_qkv, q_gain, k_gain, cos, sin, seq_len, tm=256):
    n, d = x.shape
    tm = min(tm, seq_len)
    pos_blocks = seq_len // tm
    dils = tuple(dil for _, dil in DA_GROUPS)
    gw = DA_HEADS_PER_GROUP * HEAD_DIM
    f = w_qkv.shape[1]
    perms = []
    for dil in dils:
        if dil > 1:
            dst = jnp.arange(tm)
            src = (dst % (tm // dil)) * dil + dst // (tm // dil)
            perms.append((src[:, None] == jnp.arange(tm)[None, :]).astype(BF16))
    row = pl.BlockSpec((tm, d), lambda i: (i, 0))
    tab = pl.BlockSpec((tm, HEAD_DIM), lambda i: (i % pos_blocks, 0))
    out_shapes = tuple(jax.ShapeDtypeStruct((n // dil, dil * gw), BF16) for _ in range(3) for dil in dils)
    out_specs = tuple(pl.BlockSpec((tm // dil, dil * gw), lambda i: (i, 0)) for _ in range(3) for dil in dils)
    outs = pl.pallas_call(
        functools.partial(_da_qkv_kernel, dils=dils, tm=tm, half=ROPE_DIMS // 2),
        out_shape=out_shapes,
        grid=(n // tm,),
        in_specs=[row, _const_spec((1, d)), _const_spec((d, f)), _const_spec((1, HEAD_DIM)),
                  _const_spec((1, HEAD_DIM)), tab, tab] + [_const_spec((tm, tm))] * len(perms),
        out_specs=out_specs,
        compiler_params=_params("parallel"),
    )(x, g.reshape(1, d), w_qkv.astype(BF16), q_gain.reshape(1, HEAD_DIM), k_gain.reshape(1, HEAD_DIM),
      cos, sin, *perms)
    ng = len(dils)
    return [(outs[gi], outs[ng + gi], outs[2 * ng + gi]) for gi in range(ng)]


def _band_attention(q, k, v, batch, seq_len, dil):
    n = batch * seq_len
    s_len = seq_len // dil
    nb = s_len // BAND_BLOCK
    subs = min(BAND_SUBBLOCKS, nb)
    hw = DA_HEADS_PER_GROUP * HEAD_DIM
    view = lambda a: a.reshape(batch, s_len, dil * hw)
    qv, kv, vv = view(q), view(k), view(v)

    main = pl.BlockSpec((None, subs * BAND_BLOCK, hw), lambda b, r, i: (b, i, r))
    prev = pl.BlockSpec((None, BAND_BLOCK, hw), lambda b, r, i: (b, jnp.maximum(i * subs - 1, 0), r))
    nxt = pl.BlockSpec((None, BAND_BLOCK, hw), lambda b, r, i: (b, jnp.minimum((i + 1) * subs, nb - 1), r))
    o, lse = pl.pallas_call(
        functools.partial(_band_kernel, s_len=s_len, heads=DA_HEADS_PER_GROUP, subs=subs),
        out_shape=(jax.ShapeDtypeStruct((batch, s_len, dil * hw), F32),) * 2,
        grid=(batch, dil, nb // subs),
        in_specs=[main, prev, main, nxt, prev, main, nxt],
        out_specs=(main, main),
        compiler_params=_params("parallel", "parallel", "parallel"),
    )(qv, kv, kv, kv, vv, vv, vv)
    return o.reshape(n, hw), lse.reshape(n, hw)


def _da_out_prologue(o0, o1, o2, l0, l1, l2):
    m = jnp.maximum(jnp.maximum(l0, l1), l2)
    e0, e1, e2 = jnp.exp(l0 - m), jnp.exp(l1 - m), jnp.exp(l2 - m)
    return (e0 * o0 + e1 * o1 + e2 * o2) / (e0 + e1 + e2)


def _dilated_layer(x, batch, seq_len, g, p, route):
    cos, sin = _rope_tables(seq_len)
    qkv = _da_qkv(x, g, p['w_qkv'], p['q_gain'], p['k_gain'], cos, sin, seq_len)
    outs, lses = [], []
    for (q, k, v), (_, dil) in zip(qkv, DA_GROUPS):
        o, lse = _band_attention(q, k, v, batch, seq_len, dil)
        outs.append(o)
        lses.append(lse)
    return _mm_res(_da_out_prologue, outs + lses, [], p['w_o'], None, x, route)


def _expert_ffn_kernel(xe_ref, gate_ref, w1_ref, w3_ref, w2_ref, ye_ref, w1b, w3b, w2b):
    @pl.when(pl.program_id(1) == 0)
    def _():
        w1b[...] = w1_ref[...].astype(BF16)
        w3b[...] = w3_ref[...].astype(BF16)
        w2b[...] = w2_ref[...].astype(BF16)

    xe = xe_ref[...]
    h1 = jnp.dot(xe, w1b[...], preferred_element_type=F32)
    h3 = jnp.dot(xe, w3b[...], preferred_element_type=F32)
    hid = (h1 * jax.nn.sigmoid(h1) * h3).astype(BF16)
    ye_ref[...] = (jnp.dot(hid, w2b[...], preferred_element_type=F32) * gate_ref[...]).astype(ye_ref.dtype)


def _expert_ffn(xe, gates, w1, w3, w2, tm=512):
    e, c, d = xe.shape
    f = w1.shape[2]
    tm = min(tm, c)
    return pl.pallas_call(
        _expert_ffn_kernel,
        out_shape=jax.ShapeDtypeStruct((e, c, d), BF16),
        grid=(e, c // tm),
        in_specs=[pl.BlockSpec((None, tm, d), lambda ei, ci: (ei, ci, 0)),
                  pl.BlockSpec((None, tm, 1), lambda ei, ci: (ei, ci, 0)),
                  pl.BlockSpec((None, d, f), lambda ei, ci: (ei, 0, 0)),
                  pl.BlockSpec((None, d, f), lambda ei, ci: (ei, 0, 0)),
                  pl.BlockSpec((None, f, d), lambda ei, ci: (ei, 0, 0))],
        out_specs=pl.BlockSpec((None, tm, d), lambda ei, ci: (ei, ci, 0)),
        scratch_shapes=[pltpu.VMEM((d, f), BF16), pltpu.VMEM((d, f), BF16), pltpu.VMEM((f, d), BF16)],
        compiler_params=_params("parallel", "arbitrary"),
    )(xe, gates[..., None], w1, w3, w2)


def _moe_layer(x, xn, aff_t, group_sizes, w1, w3, w2, split_output):
    n, d = x.shape
    gts, idxs = [], []
    start = 0
    for ng in group_sizes:
        cap = EC_CAPACITY * ng // N_EXPERTS
        gates, idx = lax.top_k(aff_t[:, start:start + ng], cap)
        gts.append(gates)
        idxs.append(idx + start)
        start += ng
    idx_all = jnp.concatenate(idxs, axis=1)
    ye = _expert_ffn(xn[idx_all], jnp.concatenate(gts, axis=1), w1, w3, w2)
    return _combine(x, ye.reshape(-1, d), idx_all.reshape(-1), group_sizes if split_output else (n,))


COMBINE_TOKENS = 512
COMBINE_ROWS = 512


def _combine_kernel(tile_ref, blk_ref, live_ref, x_ref, tok_ref, ye_ref, *o_refs, split_tiles):
    w = pl.program_id(0)
    tile = tile_ref[w]
    first = jnp.logical_or(w == 0, tile != tile_ref[jnp.maximum(w - 1, 0)])
    tt = x_ref.shape[0]
    rows = lax.broadcasted_iota(jnp.int32, (tt, COMBINE_ROWS), 0)
    onehot = jnp.where(rows == tok_ref[...] - tile * tt, 1.0, 0.0).astype(BF16)
    add = jnp.dot(onehot, ye_ref[...], preferred_element_type=F32) * live_ref[w].astype(F32)
    lo_tile = 0
    for o_ref, n_tiles in zip(o_refs, split_tiles):
        mine = jnp.logical_and(tile >= lo_tile, tile < lo_tile + n_tiles)

        @pl.when(jnp.logical_and(mine, first))
        def _(o_ref=o_ref):
            o_ref[...] = x_ref[...] + add

        @pl.when(jnp.logical_and(mine, jnp.logical_not(first)))
        def _(o_ref=o_ref):
            o_ref[...] += add

        lo_tile += n_tiles


def _combine(x, ye, tok, splits):
    n, d = x.shape
    p = tok.shape[0]
    tt = min(COMBINE_TOKENS, min(splits))
    rb = COMBINE_ROWS
    assert p % rb == 0 and all(s % tt == 0 for s in splits) and sum(splits) == n
    tiles, nblk = n // tt, p // rb
    split_tiles = tuple(s // tt for s in splits)
    order = jnp.argsort(tok)
    tok_sorted = tok[order].astype(jnp.int32)
    ye_sorted = ye[order]

    edges = jnp.arange(tiles + 1, dtype=jnp.int32) * tt
    bounds = jnp.searchsorted(tok_sorted, edges, method='compare_all').astype(jnp.int32)
    lo, hi = bounds[:-1], bounds[1:]
    first_blk = jnp.minimum(lo // rb, nblk - 1)
    last_blk = jnp.where(hi > lo, (hi - 1) // rb, first_blk)
    n_items = last_blk - first_blk + 1
    item_end = jnp.cumsum(n_items)
    item_start = item_end - n_items
    max_items = nblk + 2 * tiles
    w = jnp.arange(max_items, dtype=jnp.int32)
    tile = jnp.minimum(jnp.searchsorted(item_end, w, side='right', method='compare_all'),
                       tiles - 1).astype(jnp.int32)
    k = w - item_start[tile]
    live = (k < n_items[tile]).astype(jnp.int32)
    blk = jnp.minimum(first_blk[tile] + k, nblk - 1).astype(jnp.int32)

    def out_spec(lo_tile, n_tiles):
        return pl.BlockSpec(
            (tt, d), lambda i, tile_r, blk_r, live_r: (jnp.clip(tile_r[i] - lo_tile, 0, n_tiles - 1), 0))

    starts = [sum(split_tiles[:j]) for j in range(len(splits))]
    grid_spec = pltpu.PrefetchScalarGridSpec(
        num_scalar_prefetch=3,
        grid=(max_items,),
        in_specs=[pl.BlockSpec((tt, d), lambda i, tile_r, blk_r, live_r: (tile_r[i], 0)),
                  pl.BlockSpec((None, 1, rb), lambda i, tile_r, blk_r, live_r: (blk_r[i], 0, 0)),
                  pl.BlockSpec((rb, d), lambda i, tile_r, blk_r, live_r: (blk_r[i], 0))],
        out_specs=tuple(out_spec(s, t) for s, t in zip(starts, split_tiles)),
    )
    return pl.pallas_call(
        functools.partial(_combine_kernel, split_tiles=split_tiles),
        out_shape=tuple(jax.ShapeDtypeStruct((s, d), F32) for s in splits),
        grid_spec=grid_spec,
        compiler_params=_params("arbitrary"),
    )(tile, blk, live, x, tok_sorted.reshape(nblk, 1, rb), ye_sorted)


def _trunk(x, batch, seq_len, group_sizes, p):
    depth = p['norm_gain'].shape[0]
    layers = ((_hyena_layer, 'hy_'), (_gqa_layer, 'ga_'), (_mlstm_layer, 'ml_'), (_dilated_layer, 'da_'))
    for i in range(depth):
        layer, prefix = layers[i % 4]
        lp = {k[len(prefix):]: v[i // 4] for k, v in p.items() if k.startswith(prefix)}
        route = (p['norm_gain'][i, 1], p['moe_w_router'][i])
        x, xn, aff_t = layer(x, batch, seq_len, p['norm_gain'][i, 0], lp, route)
        last = i == depth - 1
        outs = _moe_layer(x, xn, aff_t, group_sizes, p['moe_w1'][i], p['moe_w3'][i], p['moe_w2'][i],
                          split_output=last)
        x = outs if last else outs[0]
    return x


def kernel(x_prompt, x_sample, norm_gain, hy_w_in, hy_b_in, hy_conv_w, hy_conv_b, hy_f_w1, hy_f_b1, hy_f_w2, hy_f_b2, hy_f_w3, hy_f_b3, hy_f_freq, hy_decay, hy_skip, hy_w_out, hy_b_out, ga_w_qkv, ga_q_gain, ga_k_gain, ga_w_o, ml_w_up, ml_conv_w, ml_conv_b, ml_w_q, ml_w_k, ml_w_v, ml_w_gate, ml_b_gate, ml_norm_gain, ml_skip, ml_w_down, da_w_qkv, da_q_gain, da_k_gain, da_w_o, moe_w_router, moe_w1, moe_w3, moe_w2):
    p = dict(
        norm_gain=norm_gain,
        hy_w_in=hy_w_in, hy_b_in=hy_b_in, hy_conv_w=hy_conv_w, hy_conv_b=hy_conv_b,
        hy_f_w1=hy_f_w1, hy_f_b1=hy_f_b1, hy_f_w2=hy_f_w2, hy_f_b2=hy_f_b2,
        hy_f_w3=hy_f_w3, hy_f_b3=hy_f_b3, hy_f_freq=hy_f_freq, hy_decay=hy_decay,
        hy_skip=hy_skip, hy_w_out=hy_w_out, hy_b_out=hy_b_out,
        ga_w_qkv=ga_w_qkv, ga_q_gain=ga_q_gain, ga_k_gain=ga_k_gain, ga_w_o=ga_w_o,
        ml_w_up=ml_w_up, ml_conv_w=ml_conv_w, ml_conv_b=ml_conv_b, ml_w_q=ml_w_q,
        ml_w_k=ml_w_k, ml_w_v=ml_w_v, ml_w_gate=ml_w_gate, ml_b_gate=ml_b_gate,
        ml_norm_gain=ml_norm_gain, ml_skip=ml_skip, ml_w_down=ml_w_down,
        da_w_qkv=da_w_qkv, da_q_gain=da_q_gain, da_k_gain=da_k_gain, da_w_o=da_w_o,
        moe_w_router=moe_w_router, moe_w1=moe_w1, moe_w3=moe_w3, moe_w2=moe_w2,
    )
    bp, seq_len, d = x_prompt.shape
    bs = x_sample.shape[0]
    assert x_sample.shape[1] == seq_len
    x = jnp.concatenate([x_prompt, x_sample], axis=0).reshape((bp + bs) * seq_len, d)
    y_prompt, y_sample = _trunk(x, bp + bs, seq_len, (bp * seq_len, bs * seq_len), p)
    return (y_prompt.reshape(bp, seq_len, d), y_sample.reshape(bs, seq_len, d))
```

```python
import functools
import math

import jax
import jax.numpy as jnp
from jax import lax
from jax.experimental import pallas as pl
from jax.experimental.pallas import tpu as pltpu

F32 = jnp.float32
BF16 = jnp.bfloat16
HIGHEST = lax.Precision.HIGHEST

NORM_EPS = 1e-6
GRID_W = 64
HY_BANDS = 16
GA_HEADS = 8
GA_KV_HEADS = 2
GA_GROUP = GA_HEADS // GA_KV_HEADS
HEAD_DIM = 128
AXIAL_THETA = 10000.0
ML_HEADS = 4
ML_QKV_BLOCK = 4
DA_GROUPS = ((128, 1), (512, 4), (2048, 16))
DA_HEADS_PER_GROUP = 4
DA_HEADS = DA_HEADS_PER_GROUP * len(DA_GROUPS)
ROPE_THETA = 500000.0
ROPE_DIMS = HEAD_DIM // 4
N_EXPERTS = 16
EC_CAPACITY = 2

VMEM_LIMIT_BYTES = 52 * 1024 * 1024
HALO_ROWS = 8
MXU_DIM = 256
ML_CHUNK = 256
BAND_BLOCK = 128
BAND_HALF = 64
BAND_SUBBLOCKS = 4


def _params(*sem):
    return pltpu.CompilerParams(dimension_semantics=sem, vmem_limit_bytes=VMEM_LIMIT_BYTES)


def _rms(x, g):
    ms = jnp.mean(x * x, axis=-1, keepdims=True)
    return x * lax.rsqrt(ms + NORM_EPS) * g


def _const_spec(shape):
    nd = len(shape)
    return pl.BlockSpec(shape, lambda *_: (0,) * nd)


def _conv3_rows(p, pprev, pnext, cw, cb, rows, tm):
    up = jnp.where(rows == 0, pprev, pltpu.roll(p, 1, 0))
    dn = jnp.where(rows == tm - 1, pnext, pltpu.roll(p, tm - 1, 0))
    return up * cw[0:1] + p * cw[1:2] + dn * cw[2:3] + cb


def _halo_specs(tm, d, n_rows):
    hb = tm // HALO_ROWS
    last = n_rows // HALO_ROWS - 1
    prev = pl.BlockSpec((HALO_ROWS, d), lambda i: (jnp.maximum(i * hb - 1, 0), 0))
    nxt = pl.BlockSpec((HALO_ROWS, d), lambda i: (jnp.minimum((i + 1) * hb, last), 0))
    return prev, nxt


def _edge_scales(i, tm, seq_len):
    t0 = i * tm
    keep_prev = jnp.where(t0 % seq_len == 0, 0.0, 1.0).astype(F32)
    keep_next = jnp.where((t0 + tm) % seq_len == 0, 0.0, 1.0).astype(F32)
    return keep_prev, keep_next


def _route(x, g_ref, wr_ref, xn_ref, aff_ref):
    xn = _rms(x, g_ref[...])
    xn_ref[...] = xn.astype(BF16)
    logits = lax.dot_general(wr_ref[...], xn, (((1,), (1,)), ((), ())),
                             precision=HIGHEST, preferred_element_type=F32)
    m = jnp.max(logits, axis=0, keepdims=True)
    e = jnp.exp(logits - m)
    aff_ref[...] = e / jnp.sum(e, axis=0, keepdims=True)


def _mm_res_kernel(*refs, prologue, n_row, n_const, tn):
    row_refs = refs[:n_row]
    const_refs = refs[n_row:n_row + n_const]
    w_ref, b_ref, res_ref, g2_ref, wr_ref, o_ref, xn_ref, aff_ref = refs[n_row + n_const:]
    lhs = prologue(*[r[...] for r in row_refs], *[c[...] for c in const_refs]).astype(BF16)
    for j in range(o_ref.shape[1] // tn):
        sl = slice(j * tn, (j + 1) * tn)
        o_ref[:, sl] = (res_ref[:, sl] + b_ref[:, sl]
                        + jnp.dot(lhs, w_ref[:, sl], preferred_element_type=F32))
    _route(o_ref[...], g2_ref, wr_ref, xn_ref, aff_ref)


def _mm_res(prologue, rows, consts, w, b, res, route, tm=512, tn=512):
    n, dout = res.shape
    tm = min(tm, n)
    k = w.shape[0]
    g2, w_router = route
    ne = w_router.shape[1]
    if b is None:
        b = jnp.zeros((1, dout), F32)
    in_specs = [pl.BlockSpec((tm, r.shape[1]), lambda i: (i, 0)) if r.ndim == 2
                else pl.BlockSpec((r.shape[0], tm, r.shape[2]), lambda i: (0, i, 0)) for r in rows]
    in_specs += [_const_spec(c.shape) for c in consts]
    in_specs += [_const_spec((k, dout)), _const_spec((1, dout)),
                 pl.BlockSpec((tm, dout), lambda i: (i, 0)),
                 _const_spec((1, dout)), _const_spec((ne, dout))]
    tile = pl.BlockSpec((tm, dout), lambda i: (i, 0))
    return pl.pallas_call(
        functools.partial(_mm_res_kernel, prologue=prologue, n_row=len(rows),
                          n_const=len(consts), tn=min(tn, dout)),
        out_shape=(jax.ShapeDtypeStruct((n, dout), F32), jax.ShapeDtypeStruct((n, dout), BF16),
                   jax.ShapeDtypeStruct((ne, n), F32)),
        grid=(n // tm,),
        in_specs=in_specs,
        out_specs=(tile, tile, pl.BlockSpec((ne, tm), lambda i: (0, i))),
        compiler_params=_params("parallel"),
    )(*rows, *consts, w.astype(BF16), b.reshape(1, dout).astype(F32), res,
      g2.reshape(1, dout), w_router.T)


def _hyena_in_kernel(x_ref, xp_ref, xn_ref, g_ref, w_ref, b_ref, cw_ref, cb_ref,
                     x0_ref, vx_ref, *, seq_len, tm, d, cols):
    keep_prev, keep_next = _edge_scales(pl.program_id(0), tm, seq_len)
    g = g_ref[...]
    xb = _rms(jnp.concatenate([x_ref[...], xp_ref[...], xn_ref[...]], axis=0), g).astype(BF16)
    rows = lax.broadcasted_iota(jnp.int32, (tm, 1), 0)

    def conv_part(c0):
        sl = slice(c0, c0 + cols)
        pa = jnp.dot(xb, w_ref[:, sl], preferred_element_type=F32) + b_ref[:, sl]
        p = pa[:tm]
        pprev = pa[tm + HALO_ROWS - 1:tm + HALO_ROWS, :] * keep_prev
        pnext = pa[tm + HALO_ROWS:tm + HALO_ROWS + 1, :] * keep_next
        return _conv3_rows(p, pprev, pnext, cw_ref[:, sl], cb_ref[:, sl], rows, tm)

    for j in range(d // cols):
        c = j * cols
        x0_ref[:, c:c + cols] = conv_part(c)
        vx_ref[:, c:c + cols] = conv_part(2 * d + c) * conv_part(d + c)


def _hyena_in(x, g, w_in, b_in, conv_w, conv_b, seq_len, tm=512, cols=512):
    n, d = x.shape
    tm = min(tm, seq_len)
    prev, nxt = _halo_specs(tm, d, n)
    row = pl.BlockSpec((tm, d), lambda i: (i, 0))
    return pl.pallas_call(
        functools.partial(_hyena_in_kernel, seq_len=seq_len, tm=tm, d=d, cols=cols),
        out_shape=(jax.ShapeDtypeStruct((n, d), F32), jax.ShapeDtypeStruct((n, d), F32)),
        grid=(n // tm,),
        in_specs=[row, prev, nxt, _const_spec((1, d)), _const_spec((d, 3 * d)),
                  _const_spec((1, 3 * d)), _const_spec((3, 3 * d)), _const_spec((1, 3 * d))],
        out_specs=(row, row),
        compiler_params=_params("parallel"),
    )(x, x, x, g.reshape(1, d), w_in.astype(BF16), b_in.reshape(1, 3 * d),
      conv_w, conv_b.reshape(1, 3 * d))


def _hyena_filter_taps(L, d, f_w1, f_b1, f_w2, f_b2, f_w3, f_b3, f_freq, decay):
    t = jnp.linspace(0.0, 1.0, L, dtype=F32)[:, None]
    w_ang = 2.0 * math.pi * jnp.arange(L, dtype=F32)[:, None] / L
    bands = jnp.linspace(1e-4, HY_BANDS - 1, HY_BANDS, dtype=F32)[None, :]
    z = jnp.concatenate([t, jnp.cos(bands * w_ang), -jnp.sin(bands * w_ang)], axis=-1)

    def branch(zz, tt, col0):
        h = jnp.sin(f_freq[0] * (zz @ f_w1 + f_b1))
        h = jnp.sin(f_freq[1] * (h @ f_w2 + f_b2))
        h = h @ f_w3[:, col0:col0 + d] + f_b3[col0:col0 + d]
        return h * jnp.exp(-tt * jnp.abs(decay[col0 // d])[None])

    h_fwd = branch(z, t, 0)
    h_bwd_rev = branch(z[::-1], t[::-1], d)
    k = jnp.concatenate([h_fwd, jnp.zeros((1, d), F32), h_bwd_rev[:-1]], axis=0)
    return k / jnp.sum(jnp.abs(k), axis=0, keepdims=True)


def _fft_dims(m):
    lg = m.bit_length() - 1
    p = 1 << ((lg + 1) // 2)
    return p, m // p


def _split_bf16(x):
    hi = x.astype(BF16)
    return hi, (x - hi.astype(F32)).astype(BF16)


def _mm_split(fh, fl, x, precise):
    if not precise:
        return jnp.dot(fh, x.astype(BF16), preferred_element_type=F32)
    xh, xl = _split_bf16(x)
    return (jnp.dot(fh, xh, preferred_element_type=F32) + jnp.dot(fl, xh, preferred_element_type=F32)
            + jnp.dot(fh, xl, preferred_element_type=F32))


def _cplx_as_real(cr, ci):
    top = jnp.concatenate([cr, -ci], axis=-1)
    bot = jnp.concatenate([ci, cr], axis=-1)
    return jnp.concatenate([top, bot], axis=-2)


def _unit_circle(idx, m):
    ang = (2.0 * math.pi / m) * idx.astype(F32)
    return jnp.cos(ang), jnp.sin(ang)


def _dft_consts(p, q):
    m = p * q
    k1 = jnp.arange(p, dtype=jnp.int32)
    n1 = jnp.arange(p // 2, dtype=jnp.int32)
    c, s = _unit_circle((k1[:, None] * n1[None, :]) % p, p)
    fa = _cplx_as_real(c, -s)
    c, s = _unit_circle((n1[:, None] * k1[None, :]) % p, p)
    fd = _cplx_as_real(c / m, s / m)
    k2 = jnp.arange(q, dtype=jnp.int32)
    n2 = jnp.arange(q, dtype=jnp.int32)
    idx = (n2[None, None, :] * (k2[None, :, None] * p + k1[:, None, None])) % m
    c, s = _unit_circle(idx, m)
    gb = _cplx_as_real(c, -s)
    ct, st = jnp.swapaxes(c, 1, 2), jnp.swapaxes(s, 1, 2)
    gc = _cplx_as_real(ct, st)
    return tuple(_split_bf16(a) for a in (fa, gb, gc, fd))


def _fft_a_kernel(x_ref, fh_ref, fl_ref, o_ref, *, precise):
    _, rows_in, group, d = x_ref.shape
    rows_out = o_ref.shape[1]
    for j in range(group):
        x = x_ref[:, :, j, :].reshape(2 * rows_in, d)
        y = _mm_split(fh_ref[...], fl_ref[...], x, precise)
        o_ref[:, :, j, :] = y.reshape(2, rows_out, d)


def _fft_b_kernel(a_ref, gh_ref, gl_ref, o_ref):
    _, q, d = a_ref.shape
    x = _mm_split(gh_ref[...], gl_ref[...], a_ref[...].reshape(2 * q, d), True)
    o_ref[...] = x.reshape(o_ref.shape)


def _fft_bc_kernel(a_ref, gb_ref, gc_ref, k_ref, z_ref):
    _, q, d = a_ref.shape
    x = _mm_split(gb_ref[...], None, a_ref[...].reshape(2 * q, d), False)
    xr, xi = x[:q], x[q:]
    kr, ki = k_ref[0], k_ref[1]
    y = jnp.concatenate([xr * kr - xi * ki, xr * ki + xi * kr], axis=0)
    z = _mm_split(gc_ref[...], None, y, False)
    z_ref[...] = z.reshape(z_ref.shape)


def _fft_rows(x5, f, rows_out, precise):
    pairs, _, rows_in, q, d = x5.shape
    blk = lambda r: pl.BlockSpec((None, 2, r, HALO_ROWS, d), lambda b, j: (b, 0, 0, j, 0))
    return pl.pallas_call(
        functools.partial(_fft_a_kernel, precise=precise),
        out_shape=jax.ShapeDtypeStruct((pairs, 2, rows_out, q, d), F32),
        grid=(pairs, q // HALO_ROWS),
        in_specs=[blk(rows_in), _const_spec(f[0].shape), _const_spec(f[1].shape)],
        out_specs=blk(rows_out),
        compiler_params=_params("parallel", "parallel"),
    )(x5, *f)


def _long_conv(vx, taps, batch, seq_len):
    n, d = vx.shape
    m = 2 * seq_len
    p, q = _fft_dims(m)
    assert batch % 2 == 0
    pairs = batch // 2
    fa, gb, gc, fd = _dft_consts(p, q)
    g_spec = pl.BlockSpec((None, 2 * q, 2 * q), lambda k1, b: (k1, 0, 0))
    slab = pl.BlockSpec((None, 2, q, d), lambda k1, b: (b, 0, k1, 0))

    zeros = jnp.zeros((seq_len, d), F32)
    kin = jnp.stack([taps[:seq_len], zeros, taps[seq_len:], zeros]).reshape(2, 2, p // 2, q, d)
    ka = _fft_rows(kin, fa, p, True).reshape(2, 2, p * q, d)
    kx = pl.pallas_call(
        _fft_b_kernel,
        out_shape=jax.ShapeDtypeStruct((2, 2, p * q, d), F32),
        grid=(p, 2),
        in_specs=[slab, g_spec, g_spec],
        out_specs=slab,
        compiler_params=_params("parallel", "parallel"),
    )(ka, *gb)
    sign = jnp.repeat(1.0 - 2.0 * (jnp.arange(p) % 2).astype(F32), q)[None, :, None]
    kspec = kx[0] + sign * kx[1]

    xa = _fft_rows(vx.reshape(pairs, 2, p // 2, q, d), fa, p, False).reshape(pairs, 2, p * q, d)
    z = pl.pallas_call(
        _fft_bc_kernel,
        out_shape=jax.ShapeDtypeStruct((pairs, 2, p * q, d), F32),
        grid=(p, pairs),
        in_specs=[slab, g_spec, g_spec, pl.BlockSpec((2, q, d), lambda k1, b: (0, k1, 0))],
        out_specs=slab,
        compiler_params=_params("parallel", "parallel"),
    )(xa, gb[0], gc[0], kspec)
    y = _fft_rows(z.reshape(pairs, 2, p, q, d), fd, p // 2, False)
    return y.reshape(n, d)


def _hyena_out_prologue(y, vx, x0, skip):
    return (y + vx * skip) * x0


def _hyena_layer(x, batch, seq_len, g, p, route):
    n, d = x.shape
    x0, vx = _hyena_in(x, g, p['w_in'], p['b_in'], p['conv_w'], p['conv_b'], seq_len)
    taps = _hyena_filter_taps(seq_len, d, p['f_w1'], p['f_b1'], p['f_w2'], p['f_b2'],
                              p['f_w3'], p['f_b3'], p['f_freq'], p['decay'])
    y = _long_conv(vx, taps, batch, seq_len)
    return _mm_res(_hyena_out_prologue, [y, vx, x0], [p['skip'].reshape(1, d)],
                   p['w_out'], p['b_out'], x, route)


def _head_norm_rope(xh, gain, cos, sin, half):
    y = _rms(xh, gain)
    lane = lax.broadcasted_iota(jnp.int32, (1, HEAD_DIM), 1)
    fwd = pltpu.roll(y, HEAD_DIM - half, 1)
    bwd = pltpu.roll(y, half, 1)
    partner = jnp.where((lane % (2 * half)) < half, fwd, bwd)
    return y * cos + partner * sin


def _qkv_rope_kernel(x_ref, g_ref, w_ref, qg_ref, kg_ref, cos_ref, sin_ref,
                     q_ref, k_ref, v_ref, *, nq, nk, nv, half):
    xb = _rms(x_ref[...], g_ref[...]).astype(BF16)
    cos = cos_ref[...]
    sin = sin_ref[...]
    per = MXU_DIM // HEAD_DIM
    for h0 in range(0, nq + nk + nv, per):
        pw = jnp.dot(xb, w_ref[:, h0 * HEAD_DIM:(h0 + per) * HEAD_DIM], preferred_element_type=F32)
        for h in range(h0, h0 + per):
            ph = pw[:, (h - h0) * HEAD_DIM:(h - h0 + 1) * HEAD_DIM]
            if h < nq:
                sl = slice(h * HEAD_DIM, (h + 1) * HEAD_DIM)
                q_ref[:, sl] = _head_norm_rope(ph, qg_ref[...], cos, sin, half).astype(BF16)
            elif h < nq + nk:
                sl = slice((h - nq) * HEAD_DIM, (h - nq + 1) * HEAD_DIM)
                k_ref[:, sl] = _head_norm_rope(ph, kg_ref[...], cos, sin, half).astype(BF16)
            else:
                sl = slice((h - nq - nk) * HEAD_DIM, (h - nq - nk + 1) * HEAD_DIM)
                v_ref[:, sl] = ph.astype(BF16)


def _qkv_rope(x, g, w_qkv, q_gain, k_gain, cos, sin, nq, nk, nv, half, seq_len, tm=512):
    n, d = x.shape
    tm = min(tm, seq_len)
    pos_blocks = seq_len // tm
    f = w_qkv.shape[1]
    row = pl.BlockSpec((tm, d), lambda i: (i, 0))
    tab = pl.BlockSpec((tm, HEAD_DIM), lambda i: (i % pos_blocks, 0))
    outs = tuple(jax.ShapeDtypeStruct((n, c * HEAD_DIM), BF16) for c in (nq, nk, nv))
    return pl.pallas_call(
        functools.partial(_qkv_rope_kernel, nq=nq, nk=nk, nv=nv, half=half),
        out_shape=outs,
        grid=(n // tm,),
        in_specs=[row, _const_spec((1, d)), _const_spec((d, f)),
                  _const_spec((1, HEAD_DIM)), _const_spec((1, HEAD_DIM)), tab, tab],
        out_specs=tuple(pl.BlockSpec((tm, c * HEAD_DIM), lambda i: (i, 0)) for c in (nq, nk, nv)),
        compiler_params=_params("parallel"),
    )(x, g.reshape(1, d), w_qkv.astype(BF16), q_gain.reshape(1, HEAD_DIM),
      k_gain.reshape(1, HEAD_DIM), cos, sin)


def _axial_tables(L):
    t = jnp.arange(L)
    r = (t // GRID_W).astype(F32)
    c = (t % GRID_W).astype(F32)
    nf = HEAD_DIM // 4
    inv = AXIAL_THETA ** (-(2.0 * jnp.arange(nf, dtype=F32)) / (2 * nf))
    ar, ac = r[:, None] * inv[None], c[:, None] * inv[None]
    cos = jnp.concatenate([jnp.cos(ar), jnp.cos(ar), jnp.cos(ac), jnp.cos(ac)], axis=-1)
    sin = jnp.concatenate([-jnp.sin(ar), jnp.sin(ar), -jnp.sin(ac), jnp.sin(ac)], axis=-1)
    return cos, sin


FLASH_SAFE_BOUND = 40.0
FLASH_BOUND_MARGIN = 1.001


def _flash_kernel(q_ref, k_ref, v_ref, o_ref, acc_scr, off_scr, lsum_scr, m_scr, l_scr, kn_scr,
                  *, tq, tk, tkf, seq_len, group):
    scale = HEAD_DIM ** -0.5
    c = scale * math.log2(math.e)

    @pl.when(pl.program_id(2) == 0)
    def _():
        def norm_step(j, mx):
            kt = k_ref[pl.ds(pl.multiple_of(j * tk, tk), tk), :].astype(F32)
            row = jnp.sum(kt * kt, axis=1, keepdims=True)
            return jnp.maximum(mx, jnp.max(row, axis=0, keepdims=True))

        kmax2 = lax.fori_loop(0, seq_len // tk, norm_step, jnp.zeros((1, 1), F32))
        kn_scr[...] = jnp.broadcast_to(kmax2, kn_scr.shape)

    kmax2 = kn_scr[0:1, 0:1]
    bmax = jnp.zeros((1, 1), F32)
    for h in range(group):
        qf = q_ref[:, h * HEAD_DIM:(h + 1) * HEAD_DIM].astype(F32)
        b = jnp.sqrt(jnp.sum(qf * qf, axis=1, keepdims=True) * kmax2) * FLASH_BOUND_MARGIN
        bmax = jnp.maximum(bmax, jnp.max(b, axis=0, keepdims=True))
        off_scr[h] = jnp.broadcast_to(b * c, (tq, HEAD_DIM))
    fast = (bmax * scale)[0, 0] <= FLASH_SAFE_BOUND

    @pl.when(fast)
    def _():
        acc_scr[...] = jnp.zeros_like(acc_scr)
        lsum_scr[...] = jnp.zeros_like(lsum_scr)
        n_lane_tiles = tkf // HEAD_DIM

        def body(j, carry):
            start = pl.multiple_of(j * tkf, tkf)
            kt = k_ref[pl.ds(start, tkf), :]
            vt = v_ref[pl.ds(start, tkf), :]
            for h in range(group):
                q = q_ref[:, h * HEAD_DIM:(h + 1) * HEAD_DIM]
                s = lax.dot_general(q, kt, (((1,), (1,)), ((), ())), preferred_element_type=F32)
                off = off_scr[h]
                p = jnp.exp2(s * c - jnp.concatenate([off] * n_lane_tiles, axis=1))
                part = p[:, 0:HEAD_DIM]
                for t in range(1, n_lane_tiles):
                    part = part + p[:, t * HEAD_DIM:(t + 1) * HEAD_DIM]
                lsum_scr[h] += part
                acc_scr[h] += jnp.dot(p.astype(BF16), vt, preferred_element_type=F32)
            return carry

        lax.fori_loop(0, seq_len // tkf, body, 0)
        for h in range(group):
            l = jnp.sum(lsum_scr[h], axis=1, keepdims=True)
            o_ref[:, h * HEAD_DIM:(h + 1) * HEAD_DIM] = (acc_scr[h] / l).astype(o_ref.dtype)

    @pl.when(jnp.logical_not(fast))
    def _():
        m_scr[...] = jnp.full(m_scr.shape, -jnp.inf, F32)
        l_scr[...] = jnp.zeros_like(l_scr)
        acc_scr[...] = jnp.zeros_like(acc_scr)

        def body(j, carry):
            start = pl.multiple_of(j * tk, tk)
            kt = k_ref[pl.ds(start, tk), :]
            vt = v_ref[pl.ds(start, tk), :]
            for h in range(group):
                q = q_ref[:, h * HEAD_DIM:(h + 1) * HEAD_DIM]
                s = lax.dot_general(q, kt, (((1,), (1,)), ((), ())), preferred_element_type=F32)
                m = m_scr[h]
                m_new = jnp.maximum(m, jnp.max(s, axis=1, keepdims=True))
                alpha = jnp.exp2((m - m_new) * c)
                p = jnp.exp2(s * c - m_new * c)
                l_scr[h] = alpha * l_scr[h] + jnp.sum(p, axis=1, keepdims=True)
                acc_scr[h] = alpha * acc_scr[h] + jnp.dot(p.astype(BF16), vt, preferred_element_type=F32)
                m_scr[h] = m_new
            return carry

        lax.fori_loop(0, seq_len // tk, body, 0)
        for h in range(group):
            o_ref[:, h * HEAD_DIM:(h + 1) * HEAD_DIM] = (acc_scr[h] / l_scr[h]).astype(o_ref.dtype)


def _flash_gqa(q, k, v, batch, seq_len, tq=1024, tk=512, tkf=1024):
    tq = min(tq, seq_len)
    tk = min(tk, seq_len)
    tkf = min(tkf, seq_len)
    nq = seq_len // tq
    gw = GA_GROUP * HEAD_DIM
    wide = pltpu.VMEM((GA_GROUP, tq, HEAD_DIM), F32)
    thin = pltpu.VMEM((GA_GROUP, tq, 1), F32)
    return pl.pallas_call(
        functools.partial(_flash_kernel, tq=tq, tk=tk, tkf=tkf, seq_len=seq_len, group=GA_GROUP),
        out_shape=jax.ShapeDtypeStruct(q.shape, BF16),
        grid=(batch, GA_KV_HEADS, nq),
        in_specs=[pl.BlockSpec((tq, gw), lambda b, kv, i: (b * nq + i, kv)),
                  pl.BlockSpec((seq_len, HEAD_DIM), lambda b, kv, i: (b, kv)),
                  pl.BlockSpec((seq_len, HEAD_DIM), lambda b, kv, i: (b, kv))],
        out_specs=pl.BlockSpec((tq, gw), lambda b, kv, i: (b * nq + i, kv)),
        scratch_shapes=[wide, wide, wide, thin, thin, pltpu.VMEM((HALO_ROWS, HEAD_DIM), F32)],
        compiler_params=_params("parallel", "parallel", "arbitrary"),
    )(q, k, v)


def _identity_prologue(o):
    return o


def _gqa_layer(x, batch, seq_len, g, p, route):
    cos, sin = _axial_tables(seq_len)
    q, k, v = _qkv_rope(x, g, p['w_qkv'], p['q_gain'], p['k_gain'], cos, sin,
                        GA_HEADS, GA_KV_HEADS, GA_KV_HEADS, HEAD_DIM // 4, seq_len)
    o = _flash_gqa(q, k, v, batch, seq_len)
    return _mm_res(_identity_prologue, [o], [], p['w_o'], None, x, route)


def _ml_in_kernel(x_ref, xp_ref, xn_ref, g_ref, w_ref, cw_ref, cb_ref, wq_ref, wk_ref, wv_ref,
                  wg_ref, bg_ref, q_ref, k_ref, v_ref, xc_ref, sz_ref, gate_ref,
                  *, seq_len, tm, inner, k_scale):
    keep_prev, keep_next = _edge_scales(pl.program_id(0), tm, seq_len)
    g = g_ref[...]
    xa = _rms(jnp.concatenate([x_ref[...], xp_ref[...], xn_ref[...]], axis=0), g).astype(BF16)
    xb = xa[:tm]
    rows = lax.broadcasted_iota(jnp.int32, (tm, 1), 0)
    gacc = jnp.zeros(gate_ref.shape, F32)
    for t in range(inner // MXU_DIM):
        sl = slice(t * MXU_DIM, (t + 1) * MXU_DIM)
        xma = jnp.dot(xa, w_ref[:, sl], preferred_element_type=F32)
        xm = xma[:tm]
        pprev = xma[tm + HALO_ROWS - 1:tm + HALO_ROWS, :] * keep_prev
        pnext = xma[tm + HALO_ROWS:tm + HALO_ROWS + 1, :] * keep_next
        xc = _conv3_rows(xm, pprev, pnext, cw_ref[:, sl], cb_ref[:, sl], rows, tm)
        xc = xc * jax.nn.sigmoid(xc)
        z = jnp.dot(xb, w_ref[:, inner + t * MXU_DIM:inner + (t + 1) * MXU_DIM],
                    preferred_element_type=F32)
        xcb = xc.astype(BF16)
        q = jnp.dot(xcb, wq_ref[t], preferred_element_type=F32)
        k = jnp.dot(xcb, wk_ref[t], preferred_element_type=F32)
        v = jnp.dot(xm.astype(BF16), wv_ref[t], preferred_element_type=F32)
        qb, kb, vb = q.astype(BF16), k.astype(BF16), v.astype(BF16)
        gacc += (jnp.dot(qb, wg_ref[0, sl, :], preferred_element_type=F32)
                 + jnp.dot(kb, wg_ref[1, sl, :], preferred_element_type=F32)
                 + jnp.dot(vb, wg_ref[2, sl, :], preferred_element_type=F32))
        q_ref[:, sl] = qb
        k_ref[:, sl] = (k * k_scale).astype(BF16)
        v_ref[:, sl] = vb
        xc_ref[:, sl] = xc
        sz_ref[:, sl] = z * jax.nn.sigmoid(z)
    gate_ref[...] = gacc + bg_ref[...]


def _block_diag_tiles(w):
    nb, c, _ = w.shape
    per = MXU_DIM // c
    wt = w.reshape(nb // per, per, c, c)
    eye = jnp.eye(per, dtype=w.dtype)
    full = jnp.einsum('tpcd,pq->tpcqd', wt, eye)
    return full.reshape(nb // per, MXU_DIM, MXU_DIM)


def _ml_in(x, g, p, seq_len, tm=256):
    n, d = x.shape
    tm = min(tm, seq_len)
    inner = p['w_up'].shape[1] // 2
    ng = 4 * ML_HEADS
    dh = inner // ML_HEADS
    wq, wk, wv = (_block_diag_tiles(p[nm]).astype(BF16) for nm in ('w_q', 'w_k', 'w_v'))
    wg = jnp.transpose(p['w_gate'], (1, 2, 0, 3)).reshape(3, inner, ng).astype(BF16)
    bg = p['b_gate'].reshape(1, ng)
    prev, nxt = _halo_specs(tm, d, n)
    row = pl.BlockSpec((tm, d), lambda i: (i, 0))
    wide = pl.BlockSpec((tm, inner), lambda i: (i, 0))
    nt = inner // MXU_DIM
    return pl.pallas_call(
        functools.partial(_ml_in_kernel, seq_len=seq_len, tm=tm, inner=inner, k_scale=dh ** -0.5),
        out_shape=(jax.ShapeDtypeStruct((n, inner), BF16),) * 3
        + (jax.ShapeDtypeStruct((n, inner), F32),) * 2
        + (jax.ShapeDtypeStruct((n, ng), F32),),
        grid=(n // tm,),
        in_specs=[row, prev, nxt, _const_spec((1, d)), _const_spec((d, 2 * inner)),
                  _const_spec((3, inner)), _const_spec((1, inner)),
                  _const_spec((nt, MXU_DIM, MXU_DIM)), _const_spec((nt, MXU_DIM, MXU_DIM)),
                  _const_spec((nt, MXU_DIM, MXU_DIM)), _const_spec((3, inner, ng)),
                  _const_spec((1, ng))],
        out_specs=(wide,) * 5 + (pl.BlockSpec((tm, ng), lambda i: (i, 0)),),
        compiler_params=_params("parallel"),
    )(x, x, x, g.reshape(1, d), p['w_up'].astype(BF16), p['conv_w'],
      p['conv_b'].reshape(1, inner), wq, wk, wv, wg, bg)


def _log_sigmoid(x):
    return jnp.minimum(x, 0.0) - jnp.log1p(jnp.exp(-jnp.abs(x)))


def _mlstm_chunk_kernel(qf_ref, kf_ref, vf_ref, gcf_ref, grf_ref, qb_ref, kb_ref, vb_ref, gcb_ref, grb_ref,
                        hf_ref, hb_ref, c_scr, n_scr, m_scr, *, lc):
    @pl.when(pl.program_id(2) == 0)
    def _():
        c_scr[...] = jnp.zeros_like(c_scr)
        n_scr[...] = jnp.zeros_like(n_scr)
        m_scr[...] = jnp.zeros_like(m_scr)

    _mlstm_step(qf_ref, kf_ref, vf_ref, gcf_ref, grf_ref, hf_ref, c_scr.at[0], n_scr.at[0], m_scr.at[0], lc, False)
    _mlstm_step(qb_ref, kb_ref, vb_ref, gcb_ref, grb_ref, hb_ref, c_scr.at[1], n_scr.at[1], m_scr.at[1], lc, True)


def _mlstm_step(q_ref, k_ref, v_ref, gc_ref, gr_ref, h_ref, c_scr, n_scr, m_scr, lc, backward):
    q = q_ref[...]
    k = k_ref[...]
    v = v_ref[...]
    gc = gc_ref[...]
    gr = gr_ref[...]
    i_col, g_col = gc[:, 0:1], gc[:, 1:2]
    i_row, g_row = gr[0:1, :], gr[1:2, :]

    jr = lax.broadcasted_iota(jnp.int32, (lc, lc), 0)
    sc = lax.broadcasted_iota(jnp.int32, (lc, lc), 1)
    seen = (sc >= jr) if backward else (sc <= jr)
    g_tot = g_row[:, 0:1] if backward else g_row[:, lc - 1:lc]
    m_old = m_scr[0:1, 0:1]

    dmat = jnp.where(seen, g_col - g_row + i_row, -jnp.inf)
    inter = g_col + m_old
    m_q = jnp.maximum(inter, jnp.max(dmat, axis=1, keepdims=True))
    s_qk = lax.dot_general(q, k, (((1,), (1,)), ((), ())), preferred_element_type=F32)
    a = s_qk * jnp.exp(dmat - m_q)
    w_int = jnp.exp(inter - m_q)
    q_c = jnp.dot(q, c_scr[...].astype(BF16), preferred_element_type=F32)
    num = jnp.dot(a.astype(BF16), v, preferred_element_type=F32) + q_c * w_int
    q_n = jnp.sum(q.astype(F32) * n_scr[...], axis=1, keepdims=True)
    den = jnp.sum(a, axis=1, keepdims=True) + w_int * q_n
    den = jnp.maximum(jnp.abs(den), jnp.exp(-m_q))
    h_ref[...] = num / den

    a_row = g_tot - g_row + i_row
    m_new = jnp.maximum(g_tot + m_old, jnp.max(a_row, axis=1, keepdims=True))
    ws_col = jnp.exp(g_tot - g_col + i_col - m_new)
    dec = jnp.exp(g_tot + m_old - m_new)
    kw = k.astype(F32) * ws_col
    upd = lax.dot_general(kw.astype(BF16), v, (((0,), (0,)), ((), ())), preferred_element_type=F32)
    c_scr[...] = dec * c_scr[...] + upd
    n_scr[...] = dec * n_scr[...] + jnp.sum(kw, axis=0, keepdims=True)
    m_scr[...] = jnp.broadcast_to(m_new, m_scr.shape)


def _ml_gate_kernel(g_ref, o_ref, *, lc, nh):
    g = g_ref[...]
    jr = lax.broadcasted_iota(jnp.int32, (lc, lc), 0)
    sc = lax.broadcasted_iota(jnp.int32, (lc, lc), 1)
    tril = jnp.where(sc <= jr, 1.0, 0.0).astype(BF16)
    triu = jnp.where(sc >= jr, 1.0, 0.0).astype(BF16)
    col = lax.broadcasted_iota(jnp.int32, (1, 4 * nh), 1)
    lf = _log_sigmoid(g)
    x1 = lf.astype(BF16)
    r1 = lf - x1.astype(F32)
    x2 = r1.astype(BF16)
    x3 = (r1 - x2.astype(F32)).astype(BF16)
    cum = lambda m: (jnp.dot(m, x1, preferred_element_type=F32) + jnp.dot(m, x2, preferred_element_type=F32)
                     + jnp.dot(m, x3, preferred_element_type=F32))
    gsum = jnp.where(col < 2 * nh, cum(tril), cum(triu))
    o_ref[...] = jnp.where((col % (2 * nh)) >= nh, gsum, g)


def _mlstm_chunks(q, k, v, gates, batch, seq_len):
    n, inner = q.shape
    dh = inner // ML_HEADS
    lc = min(ML_CHUNK, seq_len)
    nc = seq_len // lc
    ng = gates.shape[1]
    gates = pl.pallas_call(
        functools.partial(_ml_gate_kernel, lc=lc, nh=ML_HEADS),
        out_shape=jax.ShapeDtypeStruct((n, ng), F32),
        grid=(n // lc,),
        in_specs=[pl.BlockSpec((lc, ng), lambda i: (i, 0))],
        out_specs=pl.BlockSpec((lc, ng), lambda i: (i, 0)),
        compiler_params=_params("parallel"),
    )(gates)
    g4 = gates.reshape(n, 2, 2, ML_HEADS)
    gcol = jnp.transpose(g4, (1, 3, 0, 2))
    grow = jnp.transpose(g4, (1, 3, 2, 0))

    def specs(dd):
        chunk = (lambda b, c: b * nc + c) if dd == 0 else (lambda b, c: b * nc + nc - 1 - c)
        qkv = pl.BlockSpec((lc, dh), lambda b, h, c: (chunk(b, c), h))
        gc = pl.BlockSpec((None, None, lc, 2), lambda b, h, c: (dd, h, chunk(b, c), 0))
        gr = pl.BlockSpec((None, None, 2, lc), lambda b, h, c: (dd, h, 0, chunk(b, c)))
        return [qkv, qkv, qkv, gc, gr], qkv

    in_f, out_f = specs(0)
    in_b, out_b = specs(1)
    return pl.pallas_call(
        functools.partial(_mlstm_chunk_kernel, lc=lc),
        out_shape=(jax.ShapeDtypeStruct((n, inner), F32),) * 2,
        grid=(batch, ML_HEADS, nc),
        in_specs=in_f + in_b,
        out_specs=(out_f, out_b),
        scratch_shapes=[pltpu.VMEM((2, dh, dh), F32), pltpu.VMEM((2, 1, dh), F32),
                        pltpu.VMEM((2, HALO_ROWS, HEAD_DIM), F32)],
        compiler_params=_params("parallel", "parallel", "arbitrary"),
    )(q, k, v, gcol, grow, q, k, v, gcol, grow)


def _ml_out_prologue(hf, hb, xc, sz, gain, skip):
    h = hf + hb
    dh = h.shape[1] // ML_HEADS
    parts = []
    for i in range(ML_HEADS):
        sl = slice(i * dh, (i + 1) * dh)
        parts.append(_rms(h[:, sl], gain[:, sl]))
    hn = jnp.concatenate(parts, axis=1)
    return (hn + skip * xc) * sz


def _mlstm_layer(x, batch, seq_len, g, p, route):
    q, k, v, xc, sz, gates = _ml_in(x, g, p, seq_len)
    inner = q.shape[1]
    hf, hb = _mlstm_chunks(q, k, v, gates, batch, seq_len)
    return _mm_res(_ml_out_prologue, [hf, hb, xc, sz],
                   [p['norm_gain'].reshape(1, inner), p['skip'].reshape(1, inner)],
                   p['w_down'], None, x, route, tm=256)


def _rope_tables(L):
    inv = ROPE_THETA ** (-(2.0 * jnp.arange(ROPE_DIMS // 2, dtype=F32)) / ROPE_DIMS)
    ang = jnp.arange(L, dtype=F32)[:, None] * inv[None]
    pad = HEAD_DIM - ROPE_DIMS
    cos = jnp.concatenate([jnp.cos(ang), jnp.cos(ang), jnp.ones((L, pad), F32)], axis=-1)
    sin = jnp.concatenate([-jnp.sin(ang), jnp.sin(ang), jnp.zeros((L, pad), F32)], axis=-1)
    return cos, sin


def _band_kernel(q_ref, kp_ref, kc_ref, kn_ref, vp_ref, vc_ref, vn_ref, o_ref, lse_ref,
                 *, s_len, heads, subs):
    i = pl.program_id(2)
    qb = BAND_BLOCK
    w = qb + 2 * BAND_HALF
    a = lax.broadcasted_iota(jnp.int32, (qb, w), 0)
    c = lax.broadcasted_iota(jnp.int32, (qb, w), 1)
    in_band = jnp.abs(c - BAND_HALF - a) <= BAND_HALF
    scale = HEAD_DIM ** -0.5
    for h in range(heads):
        sl = slice(h * HEAD_DIM, (h + 1) * HEAD_DIM)
        kw = jnp.concatenate([kp_ref[qb - BAND_HALF:, sl], kc_ref[:, sl], kn_ref[:BAND_HALF, sl]], axis=0)
        vw = jnp.concatenate([vp_ref[qb - BAND_HALF:, sl], vc_ref[:, sl], vn_ref[:BAND_HALF, sl]], axis=0)
        for u in range(subs):
            rows = slice(u * qb, (u + 1) * qb)
            key_pos = (i * subs + u) * qb - BAND_HALF + c
            valid = in_band & (key_pos >= 0) & (key_pos < s_len)
            s = lax.dot_general(q_ref[rows, sl], kw[u * qb:u * qb + w], (((1,), (1,)), ((), ())),
                                preferred_element_type=F32) * scale
            s = jnp.where(valid, s, -jnp.inf)
            m = jnp.max(s, axis=1, keepdims=True)
            p = jnp.exp(s - m)
            l = jnp.sum(p, axis=1, keepdims=True)
            o = jnp.dot(p.astype(BF16), vw[u * qb:u * qb + w], preferred_element_type=F32)
            o_ref[rows, sl] = o / l
            lse_ref[rows, sl] = jnp.broadcast_to(m + jnp.log(l), (qb, HEAD_DIM))


def _da_qkv_kernel(x_ref, g_ref, w_ref, qg_ref, kg_ref, cos_ref, sin_ref, *refs, dils, tm, half):
    n_perm = sum(1 for d in dils if d > 1)
    perm_refs, out_refs = refs[:n_perm], refs[n_perm:]
    xb = _rms(x_ref[...], g_ref[...]).astype(BF16)
    cos, sin = cos_ref[...], sin_ref[...]
    gw = DA_HEADS_PER_GROUP * HEAD_DIM
    heads = DA_HEADS_PER_GROUP * len(dils)
    for kind in range(3):
        gain = (qg_ref, kg_ref, None)[kind]
        pi = 0
        for gi, dil in enumerate(dils):
            c0 = (kind * heads + gi * DA_HEADS_PER_GROUP) * HEAD_DIM
            ph = jnp.dot(xb, w_ref[:, c0:c0 + gw], preferred_element_type=F32)
            if gain is not None:
                ph = jnp.concatenate(
                    [_head_norm_rope(ph[:, h * HEAD_DIM:(h + 1) * HEAD_DIM], gain[...], cos, sin, half)
                     for h in range(DA_HEADS_PER_GROUP)], axis=1)
            val = ph.astype(BF16)
            o_ref = out_refs[kind * len(dils) + gi]
            if dil == 1:
                o_ref[...] = val
            else:
                pv = jnp.dot(perm_refs[pi][...], val, preferred_element_type=F32).astype(BF16)
                pi += 1
                rows = tm // dil
                for r in range(dil):
                    o_ref[:, r * gw:(r + 1) * gw] = pv[r * rows:(r + 1) * rows, :]


def _da_qkv(x, g, w_qkv, q_gain, k_gain, cos, sin, seq_len, tm=256):
    n, d = x.shape
    tm = min(tm, seq_len)
    pos_blocks = seq_len // tm
    dils = tuple(dil for _, dil in DA_GROUPS)
    gw = DA_HEADS_PER_GROUP * HEAD_DIM
    f = w_qkv.shape[1]
    perms = []
    for dil in dils:
        if dil > 1:
            dst = jnp.arange(tm)
            src = (dst % (tm // dil)) * dil + dst // (tm // dil)
            perms.append((src[:, None] == jnp.arange(tm)[None, :]).astype(BF16))
    row = pl.BlockSpec((tm, d), lambda i: (i, 0))
    tab = pl.BlockSpec((tm, HEAD_DIM), lambda i: (i % pos_blocks, 0))
    out_shapes = tuple(jax.ShapeDtypeStruct((n // dil, dil * gw), BF16) for _ in range(3) for dil in dils)
    out_specs = tuple(pl.BlockSpec((tm // dil, dil * gw), lambda i: (i, 0)) for _ in range(3) for dil in dils)
    outs = pl.pallas_call(
        functools.partial(_da_qkv_kernel, dils=dils, tm=tm, half=ROPE_DIMS // 2),
        out_shape=out_shapes,
        grid=(n // tm,),
        in_specs=[row, _const_spec((1, d)), _const_spec((d, f)), _const_spec((1, HEAD_DIM)),
                  _const_spec((1, HEAD_DIM)), tab, tab] + [_const_spec((tm, tm))] * len(perms),
        out_specs=out_specs,
        compiler_params=_params("parallel"),
    )(x, g.reshape(1, d), w_qkv.astype(BF16), q_gain.reshape(1, HEAD_DIM), k_gain.reshape(1, HEAD_DIM),
      cos, sin, *perms)
    ng = len(dils)
    return [(outs[gi], outs[ng + gi], outs[2 * ng + gi]) for gi in range(ng)]


def _band_attention(q, k, v, batch, seq_len, dil):
    n = batch * seq_len
    s_len = seq_len // dil
    nb = s_len // BAND_BLOCK
    subs = min(BAND_SUBBLOCKS, nb)
    hw = DA_HEADS_PER_GROUP * HEAD_DIM
    view = lambda a: a.reshape(batch, s_len, dil * hw)
    qv, kv, vv = view(q), view(k), view(v)

    main = pl.BlockSpec((None, subs * BAND_BLOCK, hw), lambda b, r, i: (b, i, r))
    prev = pl.BlockSpec((None, BAND_BLOCK, hw), lambda b, r, i: (b, jnp.maximum(i * subs - 1, 0), r))
    nxt = pl.BlockSpec((None, BAND_BLOCK, hw), lambda b, r, i: (b, jnp.minimum((i + 1) * subs, nb - 1), r))
    o, lse = pl.pallas_call(
        functools.partial(_band_kernel, s_len=s_len, heads=DA_HEADS_PER_GROUP, subs=subs),
        out_shape=(jax.ShapeDtypeStruct((batch, s_len, dil * hw), F32),) * 2,
        grid=(batch, dil, nb // subs),
        in_specs=[main, prev, main, nxt, prev, main, nxt],
        out_specs=(main, main),
        compiler_params=_params("parallel", "parallel", "parallel"),
    )(qv, kv, kv, kv, vv, vv, vv)
    return o.reshape(n, hw), lse.reshape(n, hw)


def _da_out_prologue(o0, o1, o2, l0, l1, l2):
    m = jnp.maximum(jnp.maximum(l0, l1), l2)
    e0, e1, e2 = jnp.exp(l0 - m), jnp.exp(l1 - m), jnp.exp(l2 - m)
    return (e0 * o0 + e1 * o1 + e2 * o2) / (e0 + e1 + e2)


def _dilated_layer(x, batch, seq_len, g, p, route):
    cos, sin = _rope_tables(seq_len)
    qkv = _da_qkv(x, g, p['w_qkv'], p['q_gain'], p['k_gain'], cos, sin, seq_len)
    outs, lses = [], []
    for (q, k, v), (_, dil) in zip(qkv, DA_GROUPS):
        o, lse = _band_attention(q, k, v, batch, seq_len, dil)
        outs.append(o)
        lses.append(lse)
    return _mm_res(_da_out_prologue, outs + lses, [], p['w_o'], None, x, route)


def _expert_ffn_kernel(xe_ref, gate_ref, w1_ref, w3_ref, w2_ref, ye_ref, w1b, w3b, w2b):
    @pl.when(pl.program_id(1) == 0)
    def _():
        w1b[...] = w1_ref[...].astype(BF16)
        w3b[...] = w3_ref[...].astype(BF16)
        w2b[...] = w2_ref[...].astype(BF16)

    xe = xe_ref[...]
    h1 = jnp.dot(xe, w1b[...], preferred_element_type=F32)
    h3 = jnp.dot(xe, w3b[...], preferred_element_type=F32)
    hid = (h1 * jax.nn.sigmoid(h1) * h3).astype(BF16)
    ye_ref[...] = (jnp.dot(hid, w2b[...], preferred_element_type=F32) * gate_ref[...]).astype(ye_ref.dtype)


def _expert_ffn(xe, gates, w1, w3, w2, tm=512):
    e, c, d = xe.shape
    f = w1.shape[2]
    tm = min(tm, c)
    return pl.pallas_call(
        _expert_ffn_kernel,
        out_shape=jax.ShapeDtypeStruct((e, c, d), BF16),
        grid=(e, c // tm),
        in_specs=[pl.BlockSpec((None, tm, d), lambda ei, ci: (ei, ci, 0)),
                  pl.BlockSpec((None, tm, 1), lambda ei, ci: (ei, ci, 0)),
                  pl.BlockSpec((None, d, f), lambda ei, ci: (ei, 0, 0)),
                  pl.BlockSpec((None, d, f), lambda ei, ci: (ei, 0, 0)),
                  pl.BlockSpec((None, f, d), lambda ei, ci: (ei, 0, 0))],
        out_specs=pl.BlockSpec((None, tm, d), lambda ei, ci: (ei, ci, 0)),
        scratch_shapes=[pltpu.VMEM((d, f), BF16), pltpu.VMEM((d, f), BF16), pltpu.VMEM((f, d), BF16)],
        compiler_params=_params("parallel", "arbitrary"),
    )(xe, gates[..., None], w1, w3, w2)


def _moe_layer(x, xn, aff_t, group_sizes, w1, w3, w2, split_output):
    n, d = x.shape
    gts, idxs = [], []
    start = 0
    for ng in group_sizes:
        cap = EC_CAPACITY * ng // N_EXPERTS
        gates, idx = lax.top_k(aff_t[:, start:start + ng], cap)
        gts.append(gates)
        idxs.append(idx + start)
        start += ng
    idx_all = jnp.concatenate(idxs, axis=1)
    ye = _expert_ffn(xn[idx_all], jnp.concatenate(gts, axis=1), w1, w3, w2)
    return _combine(x, ye.reshape(-1, d), idx_all.reshape(-1), group_sizes if split_output else (n,))


COMBINE_TOKENS = 512
COMBINE_ROWS = 512


def _combine_kernel(tile_ref, blk_ref, live_ref, x_ref, tok_ref, ye_ref, *o_refs, split_tiles):
    w = pl.program_id(0)
    tile = tile_ref[w]
    first = jnp.logical_or(w == 0, tile != tile_ref[jnp.maximum(w - 1, 0)])
    tt = x_ref.shape[0]
    rows = lax.broadcasted_iota(jnp.int32, (tt, COMBINE_ROWS), 0)
    onehot = jnp.where(rows == tok_ref[...] - tile * tt, 1.0, 0.0).astype(BF16)
    add = jnp.dot(onehot, ye_ref[...], preferred_element_type=F32) * live_ref[w].astype(F32)
    lo_tile = 0
    for o_ref, n_tiles in zip(o_refs, split_tiles):
        mine = jnp.logical_and(tile >= lo_tile, tile < lo_tile + n_tiles)

        @pl.when(jnp.logical_and(mine, first))
        def _(o_ref=o_ref):
            o_ref[...] = x_ref[...] + add

        @pl.when(jnp.logical_and(mine, jnp.logical_not(first)))
        def _(o_ref=o_ref):
            o_ref[...] += add

        lo_tile += n_tiles


def _combine(x, ye, tok, splits):
    n, d = x.shape
    p = tok.shape[0]
    tt = min(COMBINE_TOKENS, min(splits))
    rb = COMBINE_ROWS
    assert p % rb == 0 and all(s % tt == 0 for s in splits) and sum(splits) == n
    tiles, nblk = n // tt, p // rb
    split_tiles = tuple(s // tt for s in splits)
    order = jnp.argsort(tok)
    tok_sorted = tok[order].astype(jnp.int32)
    ye_sorted = ye[order]

    edges = jnp.arange(tiles + 1, dtype=jnp.int32) * tt
    bounds = jnp.searchsorted(tok_sorted, edges, method='compare_all').astype(jnp.int32)
    lo, hi = bounds[:-1], bounds[1:]
    first_blk = jnp.minimum(lo // rb, nblk - 1)
    last_blk = jnp.where(hi > lo, (hi - 1) // rb, first_blk)
    n_items = last_blk - first_blk + 1
    item_end = jnp.cumsum(n_items)
    item_start = item_end - n_items
    max_items = nblk + 2 * tiles
    w = jnp.arange(max_items, dtype=jnp.int32)
    tile = jnp.minimum(jnp.searchsorted(item_end, w, side='right', method='compare_all'),
                       tiles - 1).astype(jnp.int32)
    k = w - item_start[tile]
    live = (k < n_items[tile]).astype(jnp.int32)
    blk = jnp.minimum(first_blk[tile] + k, nblk - 1).astype(jnp.int32)

    def out_spec(lo_tile, n_tiles):
        return pl.BlockSpec(
            (tt, d), lambda i, tile_r, blk_r, live_r: (jnp.clip(tile_r[i] - lo_tile, 0, n_tiles - 1), 0))

    starts = [sum(split_tiles[:j]) for j in range(len(splits))]
    grid_spec = pltpu.PrefetchScalarGridSpec(
        num_scalar_prefetch=3,
        grid=(max_items,),
        in_specs=[pl.BlockSpec((tt, d), lambda i, tile_r, blk_r, live_r: (tile_r[i], 0)),
                  pl.BlockSpec((None, 1, rb), lambda i, tile_r, blk_r, live_r: (blk_r[i], 0, 0)),
                  pl.BlockSpec((rb, d), lambda i, tile_r, blk_r, live_r: (blk_r[i], 0))],
        out_specs=tuple(out_spec(s, t) for s, t in zip(starts, split_tiles)),
    )
    return pl.pallas_call(
        functools.partial(_combine_kernel, split_tiles=split_tiles),
        out_shape=tuple(jax.ShapeDtypeStruct((s, d), F32) for s in splits),
        grid_spec=grid_spec,
        compiler_params=_params("arbitrary"),
    )(tile, blk, live, x, tok_sorted.reshape(nblk, 1, rb), ye_sorted)


def _trunk(x, batch, seq_len, group_sizes, p):
    depth = p['norm_gain'].shape[0]
    layers = ((_hyena_layer, 'hy_'), (_gqa_layer, 'ga_'), (_mlstm_layer, 'ml_'), (_dilated_layer, 'da_'))
    for i in range(depth):
        layer, prefix = layers[i % 4]
        lp = {k[len(prefix):]: v[i // 4] for k, v in p.items() if k.startswith(prefix)}
        route = (p['norm_gain'][i, 1], p['moe_w_router'][i])
        x, xn, aff_t = layer(x, batch, seq_len, p['norm_gain'][i, 0], lp, route)
        last = i == depth - 1
        outs = _moe_layer(x, xn, aff_t, group_sizes, p['moe_w1'][i], p['moe_w3'][i], p['moe_w2'][i],
                          split_output=last)
        x = outs if last else outs[0]
    return x


def kernel(x_prompt, x_sample, norm_gain, hy_w_in, hy_b_in, hy_conv_w, hy_conv_b, hy_f_w1, hy_f_b1, hy_f_w2, hy_f_b2, hy_f_w3, hy_f_b3, hy_f_freq, hy_decay, hy_skip, hy_w_out, hy_b_out, ga_w_qkv, ga_q_gain, ga_k_gain, ga_w_o, ml_w_up, ml_conv_w, ml_conv_b, ml_w_q, ml_w_k, ml_w_v, ml_w_gate, ml_b_gate, ml_norm_gain, ml_skip, ml_w_down, da_w_qkv, da_q_gain, da_k_gain, da_w_o, moe_w_router, moe_w1, moe_w3, moe_w2):
    p = dict(
        norm_gain=norm_gain,
        hy_w_in=hy_w_in, hy_b_in=hy_b_in, hy_conv_w=hy_conv_w, hy_conv_b=hy_conv_b,
        hy_f_w1=hy_f_w1, hy_f_b1=hy_f_b1, hy_f_w2=hy_f_w2, hy_f_b2=hy_f_b2,
        hy_f_w3=hy_f_w3, hy_f_b3=hy_f_b3, hy_f_freq=hy_f_freq, hy_decay=hy_decay,
        hy_skip=hy_skip, hy_w_out=hy_w_out, hy_b_out=hy_b_out,
        ga_w_qkv=ga_w_qkv, ga_q_gain=ga_q_gain, ga_k_gain=ga_k_gain, ga_w_o=ga_w_o,
        ml_w_up=ml_w_up, ml_conv_w=ml_conv_w, ml_conv_b=ml_conv_b, ml_w_q=ml_w_q,
        ml_w_k=ml_w_k, ml_w_v=ml_w_v, ml_w_gate=ml_w_gate, ml_b_gate=ml_b_gate,
        ml_norm_gain=ml_norm_gain, ml_skip=ml_skip, ml_w_down=ml_w_down,
        da_w_qkv=da_w_qkv, da_q_gain=da_q_gain, da_k_gain=da_k_gain, da_w_o=da_w_o,
        moe_w_router=moe_w_router, moe_w1=moe_w1, moe_w3=moe_w3, moe_w2=moe_w2,
    )
    bp, seq_len, d = x_prompt.shape
    bs = x_sample.shape[0]
    assert x_sample.shape[1] == seq_len
    x = jnp.concatenate([x_prompt, x_sample], axis=0).reshape((bp + bs) * seq_len, d)
    y_prompt, y_sample = _trunk(x, bp + bs, seq_len, (bp * seq_len, bs * seq_len), p)
    return (y_prompt.reshape(bp, seq_len, d), y_sample.reshape(bs, seq_len, d))
```

```python
import functools
import math

import jax
import jax.numpy as jnp
from jax import lax
from jax.experimental import pallas as pl
from jax.experimental.pallas import tpu as pltpu

F32 = jnp.float32
BF16 = jnp.bfloat16
HIGHEST = lax.Precision.HIGHEST

NORM_EPS = 1e-6
GRID_W = 64
HY_BANDS = 16
GA_HEADS = 8
GA_KV_HEADS = 2
GA_GROUP = GA_HEADS // GA_KV_HEADS
HEAD_DIM = 128
AXIAL_THETA = 10000.0
ML_HEADS = 4
ML_QKV_BLOCK = 4
DA_GROUPS = ((128, 1), (512, 4), (2048, 16))
DA_HEADS_PER_GROUP = 4
DA_HEADS = DA_HEADS_PER_GROUP * len(DA_GROUPS)
ROPE_THETA = 500000.0
ROPE_DIMS = HEAD_DIM // 4
N_EXPERTS = 16
EC_CAPACITY = 2

VMEM_LIMIT_BYTES = 52 * 1024 * 1024
HALO_ROWS = 8
MXU_DIM = 256
ML_CHUNK = 256
BAND_BLOCK = 128
BAND_HALF = 64
BAND_SUBBLOCKS = 4


def _params(*sem):
    return pltpu.CompilerParams(dimension_semantics=sem, vmem_limit_bytes=VMEM_LIMIT_BYTES)


def _rms(x, g):
    ms = jnp.mean(x * x, axis=-1, keepdims=True)
    return x * lax.rsqrt(ms + NORM_EPS) * g


def _const_spec(shape, single=False):
    nd = len(shape)
    if single:
        return pl.BlockSpec(shape, lambda *_: (0,) * nd, pipeline_mode=pl.Buffered(1))
    return pl.BlockSpec(shape, lambda *_: (0,) * nd)


def _conv3_rows(p, pprev, pnext, cw, cb, rows, tm):
    up = jnp.where(rows == 0, pprev, pltpu.roll(p, 1, 0))
    dn = jnp.where(rows == tm - 1, pnext, pltpu.roll(p, tm - 1, 0))
    return up * cw[0:1] + p * cw[1:2] + dn * cw[2:3] + cb


def _halo_specs(tm, d, n_rows):
    hb = tm // HALO_ROWS
    last = n_rows // HALO_ROWS - 1
    prev = pl.BlockSpec((HALO_ROWS, d), lambda i: (jnp.maximum(i * hb - 1, 0), 0))
    nxt = pl.BlockSpec((HALO_ROWS, d), lambda i: (jnp.minimum((i + 1) * hb, last), 0))
    return prev, nxt


def _edge_scales(i, tm, seq_len):
    t0 = i * tm
    keep_prev = jnp.where(t0 % seq_len == 0, 0.0, 1.0).astype(F32)
    keep_next = jnp.where((t0 + tm) % seq_len == 0, 0.0, 1.0).astype(F32)
    return keep_prev, keep_next


def _route(x, g_ref, wr_ref, xn_ref, aff_ref):
    xn = _rms(x, g_ref[...])
    xn_ref[...] = xn.astype(BF16)
    logits = lax.dot_general(wr_ref[...], xn, (((1,), (1,)), ((), ())),
                             precision=HIGHEST, preferred_element_type=F32)
    m = jnp.max(logits, axis=0, keepdims=True)
    e = jnp.exp(logits - m)
    aff_ref[...] = e / jnp.sum(e, axis=0, keepdims=True)


def _mm_res_kernel(*refs, prologue, n_row, n_const, tn):
    row_refs = refs[:n_row]
    const_refs = refs[n_row:n_row + n_const]
    w_ref, b_ref, res_ref, g2_ref, wr_ref, o_ref, xn_ref, aff_ref = refs[n_row + n_const:]
    lhs = prologue(*[r[...] for r in row_refs], *[c[...] for c in const_refs]).astype(BF16)
    for j in range(o_ref.shape[1] // tn):
        sl = slice(j * tn, (j + 1) * tn)
        o_ref[:, sl] = (res_ref[:, sl] + b_ref[:, sl]
                        + jnp.dot(lhs, w_ref[:, sl], preferred_element_type=F32))
    _route(o_ref[...], g2_ref, wr_ref, xn_ref, aff_ref)


def _mm_res(prologue, rows, consts, w, b, res, route, tm=512, tn=512):
    n, dout = res.shape
    tm = min(tm, n)
    k = w.shape[0]
    g2, w_router = route
    ne = w_router.shape[1]
    if b is None:
        b = jnp.zeros((1, dout), F32)
    in_specs = [pl.BlockSpec((tm, r.shape[1]), lambda i: (i, 0)) if r.ndim == 2
                else pl.BlockSpec((r.shape[0], tm, r.shape[2]), lambda i: (0, i, 0)) for r in rows]
    in_specs += [_const_spec(c.shape) for c in consts]
    in_specs += [_const_spec((k, dout)), _const_spec((1, dout)),
                 pl.BlockSpec((tm, dout), lambda i: (i, 0)),
                 _const_spec((1, dout)), _const_spec((ne, dout))]
    tile = pl.BlockSpec((tm, dout), lambda i: (i, 0))
    return pl.pallas_call(
        functools.partial(_mm_res_kernel, prologue=prologue, n_row=len(rows),
                          n_const=len(consts), tn=min(tn, dout)),
        out_shape=(jax.ShapeDtypeStruct((n, dout), F32), jax.ShapeDtypeStruct((n, dout), BF16),
                   jax.ShapeDtypeStruct((ne, n), F32)),
        grid=(n // tm,),
        in_specs=in_specs,
        out_specs=(tile, tile, pl.BlockSpec((ne, tm), lambda i: (0, i))),
        compiler_params=_params("parallel"),
    )(*rows, *consts, w.astype(BF16), b.reshape(1, dout).astype(F32), res,
      g2.reshape(1, dout), w_router.T)


def _hyena_in_kernel(x_ref, xp_ref, xn_ref, g_ref, w_ref, b_ref, cw_ref, cb_ref,
                     x0_ref, vx_ref, *, seq_len, tm, d, cols):
    keep_prev, keep_next = _edge_scales(pl.program_id(0), tm, seq_len)
    g = g_ref[...]
    xb = _rms(jnp.concatenate([x_ref[...], xp_ref[...], xn_ref[...]], axis=0), g).astype(BF16)
    rows = lax.broadcasted_iota(jnp.int32, (tm, 1), 0)

    def conv_part(c0):
        sl = slice(c0, c0 + cols)
        pa = jnp.dot(xb, w_ref[:, sl], preferred_element_type=F32) + b_ref[:, sl]
        p = pa[:tm]
        pprev = pa[tm + HALO_ROWS - 1:tm + HALO_ROWS, :] * keep_prev
        pnext = pa[tm + HALO_ROWS:tm + HALO_ROWS + 1, :] * keep_next
        return _conv3_rows(p, pprev, pnext, cw_ref[:, sl], cb_ref[:, sl], rows, tm)

    for j in range(d // cols):
        c = j * cols
        x0_ref[:, c:c + cols] = conv_part(c)
        vx_ref[:, c:c + cols] = conv_part(2 * d + c) * conv_part(d + c)


def _hyena_in(x, g, w_in, b_in, conv_w, conv_b, seq_len, tm=512, cols=512):
    n, d = x.shape
    tm = min(tm, seq_len)
    prev, nxt = _halo_specs(tm, d, n)
    row = pl.BlockSpec((tm, d), lambda i: (i, 0))
    return pl.pallas_call(
        functools.partial(_hyena_in_kernel, seq_len=seq_len, tm=tm, d=d, cols=cols),
        out_shape=(jax.ShapeDtypeStruct((n, d), F32), jax.ShapeDtypeStruct((n, d), F32)),
        grid=(n // tm,),
        in_specs=[row, prev, nxt, _const_spec((1, d)), _const_spec((d, 3 * d)),
                  _const_spec((1, 3 * d)), _const_spec((3, 3 * d)), _const_spec((1, 3 * d))],
        out_specs=(row, row),
        compiler_params=_params("parallel"),
    )(x, x, x, g.reshape(1, d), w_in.astype(BF16), b_in.reshape(1, 3 * d),
      conv_w, conv_b.reshape(1, 3 * d))


def _hyena_filter_taps(L, d, f_w1, f_b1, f_w2, f_b2, f_w3, f_b3, f_freq, decay):
    t = jnp.linspace(0.0, 1.0, L, dtype=F32)[:, None]
    w_ang = 2.0 * math.pi * jnp.arange(L, dtype=F32)[:, None] / L
    bands = jnp.linspace(1e-4, HY_BANDS - 1, HY_BANDS, dtype=F32)[None, :]
    z = jnp.concatenate([t, jnp.cos(bands * w_ang), -jnp.sin(bands * w_ang)], axis=-1)

    def branch(zz, tt, col0):
        h = jnp.sin(f_freq[0] * (zz @ f_w1 + f_b1))
        h = jnp.sin(f_freq[1] * (h @ f_w2 + f_b2))
        h = h @ f_w3[:, col0:col0 + d] + f_b3[col0:col0 + d]
        return h * jnp.exp(-tt * jnp.abs(decay[col0 // d])[None])

    h_fwd = branch(z, t, 0)
    h_bwd_rev = branch(z[::-1], t[::-1], d)
    k = jnp.concatenate([h_fwd, jnp.zeros((1, d), F32), h_bwd_rev[:-1]], axis=0)
    return k / jnp.sum(jnp.abs(k), axis=0, keepdims=True)


def _fft_dims(m):
    lg = m.bit_length() - 1
    p = 1 << ((lg + 1) // 2)
    return p, m // p


def _split_bf16(x):
    hi = x.astype(BF16)
    return hi, (x - hi.astype(F32)).astype(BF16)


def _mm_split(fh, fl, x, precise):
    if not precise:
        return jnp.dot(fh, x.astype(BF16), preferred_element_type=F32)
    xh, xl = _split_bf16(x)
    return (jnp.dot(fh, xh, preferred_element_type=F32) + jnp.dot(fl, xh, preferred_element_type=F32)
            + jnp.dot(fh, xl, preferred_element_type=F32))


def _cplx_as_real(cr, ci):
    top = jnp.concatenate([cr, -ci], axis=-1)
    bot = jnp.concatenate([ci, cr], axis=-1)
    return jnp.concatenate([top, bot], axis=-2)


def _unit_circle(idx, m):
    ang = (2.0 * math.pi / m) * idx.astype(F32)
    return jnp.cos(ang), jnp.sin(ang)


def _dft_consts(p, q):
    m = p * q
    k1 = jnp.arange(p, dtype=jnp.int32)
    n1 = jnp.arange(p // 2, dtype=jnp.int32)
    c, s = _unit_circle((k1[:, None] * n1[None, :]) % p, p)
    fa = _cplx_as_real(c, -s)
    c, s = _unit_circle((n1[:, None] * k1[None, :]) % p, p)
    fd = _cplx_as_real(c / m, s / m)
    k2 = jnp.arange(q, dtype=jnp.int32)
    n2 = jnp.arange(q, dtype=jnp.int32)
    idx = (n2[None, None, :] * (k2[None, :, None] * p + k1[:, None, None])) % m
    c, s = _unit_circle(idx, m)
    gb = _cplx_as_real(c, -s)
    ct, st = jnp.swapaxes(c, 1, 2), jnp.swapaxes(s, 1, 2)
    gc = _cplx_as_real(ct, st)
    return tuple(_split_bf16(a) for a in (fa, gb, gc, fd))


def _fft_a_kernel(x_ref, fh_ref, fl_ref, o_ref, *, precise):
    _, rows_in, group, d = x_ref.shape
    rows_out = o_ref.shape[1]
    for j in range(group):
        x = x_ref[:, :, j, :].reshape(2 * rows_in, d)
        y = _mm_split(fh_ref[...], fl_ref[...], x, precise)
        o_ref[:, :, j, :] = y.reshape(2, rows_out, d)


def _fft_b_kernel(a_ref, gh_ref, gl_ref, o_ref):
    _, q, d = a_ref.shape
    x = _mm_split(gh_ref[...], gl_ref[...], a_ref[...].reshape(2 * q, d), True)
    o_ref[...] = x.reshape(o_ref.shape)


def _fft_bc_kernel(a_ref, gb_ref, gc_ref, k_ref, z_ref):
    _, q, d = a_ref.shape
    x = _mm_split(gb_ref[...], None, a_ref[...].reshape(2 * q, d), False)
    xr, xi = x[:q], x[q:]
    kr, ki = k_ref[0], k_ref[1]
    y = jnp.concatenate([xr * kr - xi * ki, xr * ki + xi * kr], axis=0)
    z = _mm_split(gc_ref[...], None, y, False)
    z_ref[...] = z.reshape(z_ref.shape)


def _fft_rows(x5, f, rows_out, precise):
    pairs, _, rows_in, q, d = x5.shape
    blk = lambda r: pl.BlockSpec((None, 2, r, HALO_ROWS, d), lambda b, j: (b, 0, 0, j, 0))
    return pl.pallas_call(
        functools.partial(_fft_a_kernel, precise=precise),
        out_shape=jax.ShapeDtypeStruct((pairs, 2, rows_out, q, d), F32),
        grid=(pairs, q // HALO_ROWS),
        in_specs=[blk(rows_in), _const_spec(f[0].shape), _const_spec(f[1].shape)],
        out_specs=blk(rows_out),
        compiler_params=_params("parallel", "parallel"),
    )(x5, *f)


def _long_conv(vx, taps, batch, seq_len):
    n, d = vx.shape
    m = 2 * seq_len
    p, q = _fft_dims(m)
    assert batch % 2 == 0
    pairs = batch // 2
    fa, gb, gc, fd = _dft_consts(p, q)
    g_spec = pl.BlockSpec((None, 2 * q, 2 * q), lambda k1, b: (k1, 0, 0))
    slab = pl.BlockSpec((None, 2, q, d), lambda k1, b: (b, 0, k1, 0))

    zeros = jnp.zeros((seq_len, d), F32)
    kin = jnp.stack([taps[:seq_len], zeros, taps[seq_len:], zeros]).reshape(2, 2, p // 2, q, d)
    ka = _fft_rows(kin, fa, p, True).reshape(2, 2, p * q, d)
    kx = pl.pallas_call(
        _fft_b_kernel,
        out_shape=jax.ShapeDtypeStruct((2, 2, p * q, d), F32),
        grid=(p, 2),
        in_specs=[slab, g_spec, g_spec],
        out_specs=slab,
        compiler_params=_params("parallel", "parallel"),
    )(ka, *gb)
    sign = jnp.repeat(1.0 - 2.0 * (jnp.arange(p) % 2).astype(F32), q)[None, :, None]
    kspec = kx[0] + sign * kx[1]

    xa = _fft_rows(vx.reshape(pairs, 2, p // 2, q, d), fa, p, False).reshape(pairs, 2, p * q, d)
    z = pl.pallas_call(
        _fft_bc_kernel,
        out_shape=jax.ShapeDtypeStruct((pairs, 2, p * q, d), F32),
        grid=(p, pairs),
        in_specs=[slab, g_spec, g_spec, pl.BlockSpec((2, q, d), lambda k1, b: (0, k1, 0))],
        out_specs=slab,
        compiler_params=_params("parallel", "parallel"),
    )(xa, gb[0], gc[0], kspec)
    y = _fft_rows(z.reshape(pairs, 2, p, q, d), fd, p // 2, False)
    return y.reshape(n, d)


def _hyena_out_prologue(y, vx, x0, skip):
    return (y + vx * skip) * x0


def _hyena_layer(x, batch, seq_len, g, p, route):
    n, d = x.shape
    x0, vx = _hyena_in(x, g, p['w_in'], p['b_in'], p['conv_w'], p['conv_b'], seq_len)
    taps = _hyena_filter_taps(seq_len, d, p['f_w1'], p['f_b1'], p['f_w2'], p['f_b2'],
                              p['f_w3'], p['f_b3'], p['f_freq'], p['decay'])
    y = _long_conv(vx, taps, batch, seq_len)
    return _mm_res(_hyena_out_prologue, [y, vx, x0], [p['skip'].reshape(1, d)],
                   p['w_out'], p['b_out'], x, route)


def _head_norm_rope(xh, gain, cos, sin, half):
    y = _rms(xh, gain)
    lane = lax.broadcasted_iota(jnp.int32, (1, HEAD_DIM), 1)
    fwd = pltpu.roll(y, HEAD_DIM - half, 1)
    bwd = pltpu.roll(y, half, 1)
    partner = jnp.where((lane % (2 * half)) < half, fwd, bwd)
    return y * cos + partner * sin


def _qkv_rope_kernel(x_ref, g_ref, w_ref, qg_ref, kg_ref, cos_ref, sin_ref,
                     q_ref, k_ref, v_ref, *, nq, nk, nv, half):
    xb = _rms(x_ref[...], g_ref[...]).astype(BF16)
    cos = cos_ref[...]
    sin = sin_ref[...]
    per = MXU_DIM // HEAD_DIM
    for h0 in range(0, nq + nk + nv, per):
        pw = jnp.dot(xb, w_ref[:, h0 * HEAD_DIM:(h0 + per) * HEAD_DIM], preferred_element_type=F32)
        for h in range(h0, h0 + per):
            ph = pw[:, (h - h0) * HEAD_DIM:(h - h0 + 1) * HEAD_DIM]
            if h < nq:
                sl = slice(h * HEAD_DIM, (h + 1) * HEAD_DIM)
                q_ref[:, sl] = _head_norm_rope(ph, qg_ref[...], cos, sin, half).astype(BF16)
            elif h < nq + nk:
                sl = slice((h - nq) * HEAD_DIM, (h - nq + 1) * HEAD_DIM)
                k_ref[:, sl] = _head_norm_rope(ph, kg_ref[...], cos, sin, half).astype(BF16)
            else:
                sl = slice((h - nq - nk) * HEAD_DIM, (h - nq - nk + 1) * HEAD_DIM)
                v_ref[:, sl] = ph.astype(BF16)


def _qkv_rope(x, g, w_qkv, q_gain, k_gain, cos, sin, nq, nk, nv, half, seq_len, tm=512):
    n, d = x.shape
    tm = min(tm, seq_len)
    pos_blocks = seq_len // tm
    f = w_qkv.shape[1]
    row = pl.BlockSpec((tm, d), lambda i: (i, 0))
    tab = pl.BlockSpec((tm, HEAD_DIM), lambda i: (i % pos_blocks, 0))
    outs = tuple(jax.ShapeDtypeStruct((n, c * HEAD_DIM), BF16) for c in (nq, nk, nv))
    return pl.pallas_call(
        functools.partial(_qkv_rope_kernel, nq=nq, nk=nk, nv=nv, half=half),
        out_shape=outs,
        grid=(n // tm,),
        in_specs=[row, _const_spec((1, d)), _const_spec((d, f)),
                  _const_spec((1, HEAD_DIM)), _const_spec((1, HEAD_DIM)), tab, tab],
        out_specs=tuple(pl.BlockSpec((tm, c * HEAD_DIM), lambda i: (i, 0)) for c in (nq, nk, nv)),
        compiler_params=_params("parallel"),
    )(x, g.reshape(1, d), w_qkv.astype(BF16), q_gain.reshape(1, HEAD_DIM),
      k_gain.reshape(1, HEAD_DIM), cos, sin)


def _axial_tables(L):
    t = jnp.arange(L)
    r = (t // GRID_W).astype(F32)
    c = (t % GRID_W).astype(F32)
    nf = HEAD_DIM // 4
    inv = AXIAL_THETA ** (-(2.0 * jnp.arange(nf, dtype=F32)) / (2 * nf))
    ar, ac = r[:, None] * inv[None], c[:, None] * inv[None]
    cos = jnp.concatenate([jnp.cos(ar), jnp.cos(ar), jnp.cos(ac), jnp.cos(ac)], axis=-1)
    sin = jnp.concatenate([-jnp.sin(ar), jnp.sin(ar), -jnp.sin(ac), jnp.sin(ac)], axis=-1)
    return cos, sin


FLASH_SAFE_BOUND = 40.0
FLASH_BOUND_MARGIN = 1.001


def _flash_kernel(q_ref, k_ref, v_ref, o_ref, acc_scr, off_scr, lsum_scr, m_scr, l_scr, kn_scr,
                  *, tq, tk, tkf, seq_len, group):
    scale = HEAD_DIM ** -0.5
    c = scale * math.log2(math.e)

    @pl.when(pl.program_id(2) == 0)
    def _():
        def norm_step(j, mx):
            kt = k_ref[pl.ds(pl.multiple_of(j * tk, tk), tk), :].astype(F32)
            row = jnp.sum(kt * kt, axis=1, keepdims=True)
            return jnp.maximum(mx, jnp.max(row, axis=0, keepdims=True))

        kmax2 = lax.fori_loop(0, seq_len // tk, norm_step, jnp.zeros((1, 1), F32))
        kn_scr[...] = jnp.broadcast_to(kmax2, kn_scr.shape)

    kmax2 = kn_scr[0:1, 0:1]
    bmax = jnp.zeros((1, 1), F32)
    for h in range(group):
        qf = q_ref[:, h * HEAD_DIM:(h + 1) * HEAD_DIM].astype(F32)
        b = jnp.sqrt(jnp.sum(qf * qf, axis=1, keepdims=True) * kmax2) * FLASH_BOUND_MARGIN
        bmax = jnp.maximum(bmax, jnp.max(b, axis=0, keepdims=True))
        off_scr[h] = jnp.broadcast_to(b * c, (tq, HEAD_DIM))
    fast = (bmax * scale)[0, 0] <= FLASH_SAFE_BOUND

    @pl.when(fast)
    def _():
        acc_scr[...] = jnp.zeros_like(acc_scr)
        lsum_scr[...] = jnp.zeros_like(lsum_scr)
        n_lane_tiles = tkf // HEAD_DIM

        def body(j, carry):
            start = pl.multiple_of(j * tkf, tkf)
            kt = k_ref[pl.ds(start, tkf), :]
            vt = v_ref[pl.ds(start, tkf), :]
            for h in range(group):
                q = q_ref[:, h * HEAD_DIM:(h + 1) * HEAD_DIM]
                s = lax.dot_general(q, kt, (((1,), (1,)), ((), ())), preferred_element_type=F32)
                off = off_scr[h]
                p = jnp.exp2(s * c - jnp.concatenate([off] * n_lane_tiles, axis=1))
                part = p[:, 0:HEAD_DIM]
                for t in range(1, n_lane_tiles):
                    part = part + p[:, t * HEAD_DIM:(t + 1) * HEAD_DIM]
                lsum_scr[h] += part
                acc_scr[h] += jnp.dot(p.astype(BF16), vt, preferred_element_type=F32)
            return carry

        lax.fori_loop(0, seq_len // tkf, body, 0)
        for h in range(group):
            l = jnp.sum(lsum_scr[h], axis=1, keepdims=True)
            o_ref[:, h * HEAD_DIM:(h + 1) * HEAD_DIM] = (acc_scr[h] / l).astype(o_ref.dtype)

    @pl.when(jnp.logical_not(fast))
    def _():
        m_scr[...] = jnp.full(m_scr.shape, -jnp.inf, F32)
        l_scr[...] = jnp.zeros_like(l_scr)
        acc_scr[...] = jnp.zeros_like(acc_scr)

        def body(j, carry):
            start = pl.multiple_of(j * tk, tk)
            kt = k_ref[pl.ds(start, tk), :]
            vt = v_ref[pl.ds(start, tk), :]
            for h in range(group):
                q = q_ref[:, h * HEAD_DIM:(h + 1) * HEAD_DIM]
                s = lax.dot_general(q, kt, (((1,), (1,)), ((), ())), preferred_element_type=F32)
                m = m_scr[h]
                m_new = jnp.maximum(m, jnp.max(s, axis=1, keepdims=True))
                alpha = jnp.exp2((m - m_new) * c)
                p = jnp.exp2(s * c - m_new * c)
                l_scr[h] = alpha * l_scr[h] + jnp.sum(p, axis=1, keepdims=True)
                acc_scr[h] = alpha * acc_scr[h] + jnp.dot(p.astype(BF16), vt, preferred_element_type=F32)
                m_scr[h] = m_new
            return carry

        lax.fori_loop(0, seq_len // tk, body, 0)
        for h in range(group):
            o_ref[:, h * HEAD_DIM:(h + 1) * HEAD_DIM] = (acc_scr[h] / l_scr[h]).astype(o_ref.dtype)


def _flash_gqa(q, k, v, batch, seq_len, tq=1024, tk=512, tkf=1024):
    tq = min(tq, seq_len)
    tk = min(tk, seq_len)
    tkf = min(tkf, seq_len)
    nq = seq_len // tq
    gw = GA_GROUP * HEAD_DIM
    wide = pltpu.VMEM((GA_GROUP, tq, HEAD_DIM), F32)
    thin = pltpu.VMEM((GA_GROUP, tq, 1), F32)
    return pl.pallas_call(
        functools.partial(_flash_kernel, tq=tq, tk=tk, tkf=tkf, seq_len=seq_len, group=GA_GROUP),
        out_shape=jax.ShapeDtypeStruct(q.shape, BF16),
        grid=(batch, GA_KV_HEADS, nq),
        in_specs=[pl.BlockSpec((tq, gw), lambda b, kv, i: (b * nq + i, kv)),
                  pl.BlockSpec((seq_len, HEAD_DIM), lambda b, kv, i: (b, kv)),
                  pl.BlockSpec((seq_len, HEAD_DIM), lambda b, kv, i: (b, kv))],
        out_specs=pl.BlockSpec((tq, gw), lambda b, kv, i: (b * nq + i, kv)),
        scratch_shapes=[wide, wide, wide, thin, thin, pltpu.VMEM((HALO_ROWS, HEAD_DIM), F32)],
        compiler_params=_params("parallel", "parallel", "arbitrary"),
    )(q, k, v)


def _identity_prologue(o):
    return o


def _gqa_layer(x, batch, seq_len, g, p, route):
    cos, sin = _axial_tables(seq_len)
    q, k, v = _qkv_rope(x, g, p['w_qkv'], p['q_gain'], p['k_gain'], cos, sin,
                        GA_HEADS, GA_KV_HEADS, GA_KV_HEADS, HEAD_DIM // 4, seq_len)
    o = _flash_gqa(q, k, v, batch, seq_len)
    return _mm_res(_identity_prologue, [o], [], p['w_o'], None, x, route)


def _ml_in_kernel(x_ref, xp_ref, xn_ref, g_ref, w_ref, cw_ref, cb_ref, wq_ref, wk_ref, wv_ref,
                  wg_ref, bg_ref, q_ref, k_ref, v_ref, xc_ref, sz_ref, gate_ref,
                  *, seq_len, tm, inner, k_scale):
    keep_prev, keep_next = _edge_scales(pl.program_id(0), tm, seq_len)
    g = g_ref[...]
    xa = _rms(jnp.concatenate([x_ref[...], xp_ref[...], xn_ref[...]], axis=0), g).astype(BF16)
    xb = xa[:tm]
    rows = lax.broadcasted_iota(jnp.int32, (tm, 1), 0)
    gacc = jnp.zeros(gate_ref.shape, F32)
    for t in range(inner // MXU_DIM):
        sl = slice(t * MXU_DIM, (t + 1) * MXU_DIM)
        xma = jnp.dot(xa, w_ref[:, sl], preferred_element_type=F32)
        xm = xma[:tm]
        pprev = xma[tm + HALO_ROWS - 1:tm + HALO_ROWS, :] * keep_prev
        pnext = xma[tm + HALO_ROWS:tm + HALO_ROWS + 1, :] * keep_next
        xc = _conv3_rows(xm, pprev, pnext, cw_ref[:, sl], cb_ref[:, sl], rows, tm)
        xc = xc * jax.nn.sigmoid(xc)
        z = jnp.dot(xb, w_ref[:, inner + t * MXU_DIM:inner + (t + 1) * MXU_DIM],
                    preferred_element_type=F32)
        xcb = xc.astype(BF16)
        q = jnp.dot(xcb, wq_ref[t], preferred_element_type=F32)
        k = jnp.dot(xcb, wk_ref[t], preferred_element_type=F32)
        v = jnp.dot(xm.astype(BF16), wv_ref[t], preferred_element_type=F32)
        qb, kb, vb = q.astype(BF16), k.astype(BF16), v.astype(BF16)
        gacc += (jnp.dot(qb, wg_ref[0, sl, :], preferred_element_type=F32)
                 + jnp.dot(kb, wg_ref[1, sl, :], preferred_element_type=F32)
                 + jnp.dot(vb, wg_ref[2, sl, :], preferred_element_type=F32))
        q_ref[:, sl] = qb
        k_ref[:, sl] = (k * k_scale).astype(BF16)
        v_ref[:, sl] = vb
        xc_ref[:, sl] = xc
        sz_ref[:, sl] = z * jax.nn.sigmoid(z)
    gate_ref[...] = gacc + bg_ref[...]


def _block_diag_tiles(w):
    nb, c, _ = w.shape
    per = MXU_DIM // c
    wt = w.reshape(nb // per, per, c, c)
    eye = jnp.eye(per, dtype=w.dtype)
    full = jnp.einsum('tpcd,pq->tpcqd', wt, eye)
    return full.reshape(nb // per, MXU_DIM, MXU_DIM)


def _ml_in(x, g, p, seq_len, tm=512):
    n, d = x.shape
    tm = min(tm, seq_len)
    inner = p['w_up'].shape[1] // 2
    ng = 4 * ML_HEADS
    dh = inner // ML_HEADS
    wq, wk, wv = (_block_diag_tiles(p[nm]).astype(BF16) for nm in ('w_q', 'w_k', 'w_v'))
    wg = jnp.transpose(p['w_gate'], (1, 2, 0, 3)).reshape(3, inner, ng).astype(BF16)
    bg = p['b_gate'].reshape(1, ng)
    prev, nxt = _halo_specs(tm, d, n)
    row = pl.BlockSpec((tm, d), lambda i: (i, 0))
    wide = pl.BlockSpec((tm, inner), lambda i: (i, 0))
    nt = inner // MXU_DIM
    return pl.pallas_call(
        functools.partial(_ml_in_kernel, seq_len=seq_len, tm=tm, inner=inner, k_scale=dh ** -0.5),
        out_shape=(jax.ShapeDtypeStruct((n, inner), BF16),) * 3
        + (jax.ShapeDtypeStruct((n, inner), F32),) * 2
        + (jax.ShapeDtypeStruct((n, ng), F32),),
        grid=(n // tm,),
        in_specs=[row, prev, nxt, _const_spec((1, d)), _const_spec((d, 2 * inner), single=True),
                  _const_spec((3, inner)), _const_spec((1, inner)),
                  _const_spec((nt, MXU_DIM, MXU_DIM), single=True),
                  _const_spec((nt, MXU_DIM, MXU_DIM), single=True),
                  _const_spec((nt, MXU_DIM, MXU_DIM), single=True),
                  _const_spec((3, inner, ng), single=True), _const_spec((1, ng))],
        out_specs=(wide,) * 5 + (pl.BlockSpec((tm, ng), lambda i: (i, 0)),),
        compiler_params=_params("parallel"),
    )(x, x, x, g.reshape(1, d), p['w_up'].astype(BF16), p['conv_w'],
      p['conv_b'].reshape(1, inner), wq, wk, wv, wg, bg)


def _log_sigmoid(x):
    return jnp.minimum(x, 0.0) - jnp.log1p(jnp.exp(-jnp.abs(x)))


def _mlstm_chunk_kernel(qf_ref, kf_ref, vf_ref, gcf_ref, grf_ref, qb_ref, kb_ref, vb_ref, gcb_ref, grb_ref,
                        hf_ref, hb_ref, c_scr, n_scr, m_scr, *, lc):
    @pl.when(pl.program_id(2) == 0)
    def _():
        c_scr[...] = jnp.zeros_like(c_scr)
        n_scr[...] = jnp.zeros_like(n_scr)
        m_scr[...] = jnp.zeros_like(m_scr)

    _mlstm_step(qf_ref, kf_ref, vf_ref, gcf_ref, grf_ref, hf_ref, c_scr.at[0], n_scr.at[0], m_scr.at[0], lc, False)
    _mlstm_step(qb_ref, kb_ref, vb_ref, gcb_ref, grb_ref, hb_ref, c_scr.at[1], n_scr.at[1], m_scr.at[1], lc, True)


def _mlstm_step(q_ref, k_ref, v_ref, gc_ref, gr_ref, h_ref, c_scr, n_scr, m_scr, lc, backward):
    q = q_ref[...]
    k = k_ref[...]
    v = v_ref[...]
    gc = gc_ref[...]
    gr = gr_ref[...]
    i_col, g_col = gc[:, 0:1], gc[:, 1:2]
    i_row, g_row = gr[0:1, :], gr[1:2, :]

    jr = lax.broadcasted_iota(jnp.int32, (lc, lc), 0)
    sc = lax.broadcasted_iota(jnp.int32, (lc, lc), 1)
    seen = (sc >= jr) if backward else (sc <= jr)
    g_tot = g_row[:, 0:1] if backward else g_row[:, lc - 1:lc]
    m_old = m_scr[0:1, 0:1]

    dmat = jnp.where(seen, g_col - g_row + i_row, -jnp.inf)
    inter = g_col + m_old
    m_q = jnp.maximum(inter, jnp.max(dmat, axis=1, keepdims=True))
    s_qk = lax.dot_general(q, k, (((1,), (1,)), ((), ())), preferred_element_type=F32)
    a = s_qk * jnp.exp(dmat - m_q)
    w_int = jnp.exp(inter - m_q)
    q_c = jnp.dot(q, c_scr[...].astype(BF16), preferred_element_type=F32)
    num = jnp.dot(a.astype(BF16), v, preferred_element_type=F32) + q_c * w_int
    q_n = jnp.sum(q.astype(F32) * n_scr[...], axis=1, keepdims=True)
    den = jnp.sum(a, axis=1, keepdims=True) + w_int * q_n
    den = jnp.maximum(jnp.abs(den), jnp.exp(-m_q))
    h_ref[...] = num / den

    a_row = g_tot - g_row + i_row
    m_new = jnp.maximum(g_tot + m_old, jnp.max(a_row, axis=1, keepdims=True))
    ws_col = jnp.exp(g_tot - g_col + i_col - m_new)
    dec = jnp.exp(g_tot + m_old - m_new)
    kw = k.astype(F32) * ws_col
    upd = lax.dot_general(kw.astype(BF16), v, (((0,), (0,)), ((), ())), preferred_element_type=F32)
    c_scr[...] = dec * c_scr[...] + upd
    n_scr[...] = dec * n_scr[...] + jnp.sum(kw, axis=0, keepdims=True)
    m_scr[...] = jnp.broadcast_to(m_new, m_scr.shape)


def _ml_gate_kernel(g_ref, o_ref, *, lc, nh):
    g = g_ref[...]
    jr = lax.broadcasted_iota(jnp.int32, (lc, lc), 0)
    sc = lax.broadcasted_iota(jnp.int32, (lc, lc), 1)
    tril = jnp.where(sc <= jr, 1.0, 0.0).astype(BF16)
    triu = jnp.where(sc >= jr, 1.0, 0.0).astype(BF16)
    col = lax.broadcasted_iota(jnp.int32, (1, 4 * nh), 1)
    lf = _log_sigmoid(g)
    x1 = lf.astype(BF16)
    r1 = lf - x1.astype(F32)
    x2 = r1.astype(BF16)
    x3 = (r1 - x2.astype(F32)).astype(BF16)
    cum = lambda m: (jnp.dot(m, x1, preferred_element_type=F32) + jnp.dot(m, x2, preferred_element_type=F32)
                     + jnp.dot(m, x3, preferred_element_type=F32))
    gsum = jnp.where(col < 2 * nh, cum(tril), cum(triu))
    o_ref[...] = jnp.where((col % (2 * nh)) >= nh, gsum, g)


def _mlstm_chunks(q, k, v, gates, batch, seq_len):
    n, inner = q.shape
    dh = inner // ML_HEADS
    lc = min(ML_CHUNK, seq_len)
    nc = seq_len // lc
    ng = gates.shape[1]
    gates = pl.pallas_call(
        functools.partial(_ml_gate_kernel, lc=lc, nh=ML_HEADS),
        out_shape=jax.ShapeDtypeStruct((n, ng), F32),
        grid=(n // lc,),
        in_specs=[pl.BlockSpec((lc, ng), lambda i: (i, 0))],
        out_specs=pl.BlockSpec((lc, ng), lambda i: (i, 0)),
        compiler_params=_params("parallel"),
    )(gates)
    g4 = gates.reshape(n, 2, 2, ML_HEADS)
    gcol = jnp.transpose(g4, (1, 3, 0, 2))
    grow = jnp.transpose(g4, (1, 3, 2, 0))

    def specs(dd):
        chunk = (lambda b, c: b * nc + c) if dd == 0 else (lambda b, c: b * nc + nc - 1 - c)
        qkv = pl.BlockSpec((lc, dh), lambda b, h, c: (chunk(b, c), h))
        gc = pl.BlockSpec((None, None, lc, 2), lambda b, h, c: (dd, h, chunk(b, c), 0))
        gr = pl.BlockSpec((None, None, 2, lc), lambda b, h, c: (dd, h, 0, chunk(b, c)))
        return [qkv, qkv, qkv, gc, gr], qkv

    in_f, out_f = specs(0)
    in_b, out_b = specs(1)
    return pl.pallas_call(
        functools.partial(_mlstm_chunk_kernel, lc=lc),
        out_shape=(jax.ShapeDtypeStruct((n, inner), F32),) * 2,
        grid=(batch, ML_HEADS, nc),
        in_specs=in_f + in_b,
        out_specs=(out_f, out_b),
        scratch_shapes=[pltpu.VMEM((2, dh, dh), F32), pltpu.VMEM((2, 1, dh), F32),
                        pltpu.VMEM((2, HALO_ROWS, HEAD_DIM), F32)],
        compiler_params=_params("parallel", "parallel", "arbitrary"),
    )(q, k, v, gcol, grow, q, k, v, gcol, grow)


def _ml_out_prologue(hf, hb, xc, sz, gain, skip):
    h = hf + hb
    dh = h.shape[1] // ML_HEADS
    parts = []
    for i in range(ML_HEADS):
        sl = slice(i * dh, (i + 1) * dh)
        parts.append(_rms(h[:, sl], gain[:, sl]))
    hn = jnp.concatenate(parts, axis=1)
    return (hn + skip * xc) * sz


def _mlstm_layer(x, batch, seq_len, g, p, route):
    q, k, v, xc, sz, gates = _ml_in(x, g, p, seq_len)
    inner = q.shape[1]
    hf, hb = _mlstm_chunks(q, k, v, gates, batch, seq_len)
    return _mm_res(_ml_out_prologue, [hf, hb, xc, sz],
                   [p['norm_gain'].reshape(1, inner), p['skip'].reshape(1, inner)],
                   p['w_down'], None, x, route, tm=256)


def _rope_tables(L):
    inv = ROPE_THETA ** (-(2.0 * jnp.arange(ROPE_DIMS // 2, dtype=F32)) / ROPE_DIMS)
    ang = jnp.arange(L, dtype=F32)[:, None] * inv[None]
    pad = HEAD_DIM - ROPE_DIMS
    cos = jnp.concatenate([jnp.cos(ang), jnp.cos(ang), jnp.ones((L, pad), F32)], axis=-1)
    sin = jnp.concatenate([-jnp.sin(ang), jnp.sin(ang), jnp.zeros((L, pad), F32)], axis=-1)
    return cos, sin


def _band_kernel(q_ref, kp_ref, kc_ref, kn_ref, vp_ref, vc_ref, vn_ref, o_ref, lse_ref,
                 *, s_len, heads, subs):
    i = pl.program_id(2)
    qb = BAND_BLOCK
    w = qb + 2 * BAND_HALF
    a = lax.broadcasted_iota(jnp.int32, (qb, w), 0)
    c = lax.broadcasted_iota(jnp.int32, (qb, w), 1)
    in_band = jnp.abs(c - BAND_HALF - a) <= BAND_HALF
    scale = HEAD_DIM ** -0.5
    for h in range(heads):
        sl = slice(h * HEAD_DIM, (h + 1) * HEAD_DIM)
        kw = jnp.concatenate([kp_ref[qb - BAND_HALF:, sl], kc_ref[:, sl], kn_ref[:BAND_HALF, sl]], axis=0)
        vw = jnp.concatenate([vp_ref[qb - BAND_HALF:, sl], vc_ref[:, sl], vn_ref[:BAND_HALF, sl]], axis=0)
        for u in range(subs):
            rows = slice(u * qb, (u + 1) * qb)
            key_pos = (i * subs + u) * qb - BAND_HALF + c
            valid = in_band & (key_pos >= 0) & (key_pos < s_len)
            s = lax.dot_general(q_ref[rows, sl], kw[u * qb:u * qb + w], (((1,), (1,)), ((), ())),
                                preferred_element_type=F32) * scale
            s = jnp.where(valid, s, -jnp.inf)
            m = jnp.max(s, axis=1, keepdims=True)
            p = jnp.exp(s - m)
            l = jnp.sum(p, axis=1, keepdims=True)
            o = jnp.dot(p.astype(BF16), vw[u * qb:u * qb + w], preferred_element_type=F32)
            o_ref[rows, sl] = o / l
            lse_ref[rows, sl] = jnp.broadcast_to(m + jnp.log(l), (qb, HEAD_DIM))


def _da_qkv_kernel(x_ref, g_ref, w_ref, qg_ref, kg_ref, cos_ref, sin_ref, *refs, dils, tm, half):
    n_perm = sum(1 for d in dils if d > 1)
    perm_refs, out_refs = refs[:n_perm], refs[n_perm:]
    xb = _rms(x_ref[...], g_ref[...]).astype(BF16)
    cos, sin = cos_ref[...], sin_ref[...]
    gw = DA_HEADS_PER_GROUP * HEAD_DIM
    heads = DA_HEADS_PER_GROUP * len(dils)
    for kind in range(3):
        gain = (qg_ref, kg_ref, None)[kind]
        pi = 0
        for gi, dil in enumerate(dils):
            c0 = (kind * heads + gi * DA_HEADS_PER_GROUP) * HEAD_DIM
            ph = jnp.dot(xb, w_ref[:, c0:c0 + gw], preferred_element_type=F32)
            if gain is not None:
                ph = jnp.concatenate(
                    [_head_norm_rope(ph[:, h * HEAD_DIM:(h + 1) * HEAD_DIM], gain[...], cos, sin, half)
                     for h in range(DA_HEADS_PER_GROUP)], axis=1)
            val = ph.astype(BF16)
            o_ref = out_refs[kind * len(dils) + gi]
            if dil == 1:
                o_ref[...] = val
            else:
                pv = jnp.dot(perm_refs[pi][...], val, preferred_element_type=F32).astype(BF16)
                pi += 1
                rows = tm // dil
                for r in range(dil):
                    o_ref[:, r * gw:(r + 1) * gw] = pv[r * rows:(r + 1) * rows, :]


def _da_qkv(x, g, w_qkv, q_gain, k_gain, cos, sin, seq_len, tm=512):
    n, d = x.shape
    tm = min(tm, seq_len)
    pos_blocks = seq_len // tm
    dils = tuple(dil for _, dil in DA_GROUPS)
    gw = DA_HEADS_PER_GROUP * HEAD_DIM
    f = w_qkv.shape[1]
    perms = []
    for dil in dils:
        if dil > 1:
            dst = jnp.arange(tm)
            src = (dst % (tm // dil)) * dil + dst // (tm // dil)
            perms.append((src[:, None] == jnp.arange(tm)[None, :]).astype(BF16))
    row = pl.BlockSpec((tm, d), lambda i: (i, 0))
    tab = pl.BlockSpec((tm, HEAD_DIM), lambda i: (i % pos_blocks, 0))
    out_shapes = tuple(jax.ShapeDtypeStruct((n // dil, dil * gw), BF16) for _ in range(3) for dil in dils)
    out_specs = tuple(pl.BlockSpec((tm // dil, dil * gw), lambda i: (i, 0)) for _ in range(3) for dil in dils)
    outs = pl.pallas_call(
        functools.partial(_da_qkv_kernel, dils=dils, tm=tm, half=ROPE_DIMS // 2),
        out_shape=out_shapes,
        grid=(n // tm,),
        in_specs=[row, _const_spec((1, d)), _const_spec((d, f), single=True), _const_spec((1, HEAD_DIM)),
                  _const_spec((1, HEAD_DIM)), tab, tab] + [_const_spec((tm, tm))] * len(perms),
        out_specs=out_specs,
        compiler_params=_params("parallel"),
    )(x, g.reshape(1, d), w_qkv.astype(BF16), q_gain.reshape(1, HEAD_DIM), k_gain.reshape(1, HEAD_DIM),
      cos, sin, *perms)
    ng = len(dils)
    return [(outs[gi], outs[ng + gi], outs[2 * ng + gi]) for gi in range(ng)]


def _band_attention(q, k, v, batch, seq_len, dil):
    n = batch * seq_len
    s_len = seq_len // dil
    nb = s_len // BAND_BLOCK
    subs = min(BAND_SUBBLOCKS, nb)
    hw = DA_HEADS_PER_GROUP * HEAD_DIM
    view = lambda a: a.reshape(batch, s_len, dil * hw)
    qv, kv, vv = view(q), view(k), view(v)

    main = pl.BlockSpec((None, subs * BAND_BLOCK, hw), lambda b, r, i: (b, i, r))
    prev = pl.BlockSpec((None, BAND_BLOCK, hw), lambda b, r, i: (b, jnp.maximum(i * subs - 1, 0), r))
    nxt = pl.BlockSpec((None, BAND_BLOCK, hw), lambda b, r, i: (b, jnp.minimum((i + 1) * subs, nb - 1), r))
    o, lse = pl.pallas_call(
        functools.partial(_band_kernel, s_len=s_len, heads=DA_HEADS_PER_GROUP, subs=subs),
        out_shape=(jax.ShapeDtypeStruct((batch, s_len, dil * hw), F32),) * 2,
        grid=(batch, dil, nb // subs),
        in_specs=[main, prev, main, nxt, prev, main, nxt],
        out_specs=(main, main),
        compiler_params=_params("parallel", "parallel", "parallel"),
    )(qv, kv, kv, kv, vv, vv, vv)
    return o.reshape(n, hw), lse.reshape(n, hw)


def _da_out_prologue(o0, o1, o2, l0, l1, l2):
    m = jnp.maximum(jnp.maximum(l0, l1), l2)
    e0, e1, e2 = jnp.exp(l0 - m), jnp.exp(l1 - m), jnp.exp(l2 - m)
    return (e0 * o0 + e1 * o1 + e2 * o2) / (e0 + e1 + e2)


def _dilated_layer(x, batch, seq_len, g, p, route):
    cos, sin = _rope_tables(seq_len)
    qkv = _da_qkv(x, g, p['w_qkv'], p['q_gain'], p['k_gain'], cos, sin, seq_len)
    outs, lses = [], []
    for (q, k, v), (_, dil) in zip(qkv, DA_GROUPS):
        o, lse = _band_attention(q, k, v, batch, seq_len, dil)
        outs.append(o)
        lses.append(lse)
    return _mm_res(_da_out_prologue, outs + lses, [], p['w_o'], None, x, route)


def _expert_ffn_kernel(xe_ref, gate_ref, w1_ref, w3_ref, w2_ref, ye_ref, w1b, w3b, w2b):
    @pl.when(pl.program_id(1) == 0)
    def _():
        w1b[...] = w1_ref[...].astype(BF16)
        w3b[...] = w3_ref[...].astype(BF16)
        w2b[...] = w2_ref[...].astype(BF16)

    xe = xe_ref[...]
    h1 = jnp.dot(xe, w1b[...], preferred_element_type=F32)
    h3 = jnp.dot(xe, w3b[...], preferred_element_type=F32)
    hid = (h1 * jax.nn.sigmoid(h1) * h3).astype(BF16)
    ye_ref[...] = (jnp.dot(hid, w2b[...], preferred_element_type=F32) * gate_ref[...]).astype(ye_ref.dtype)


def _expert_ffn(xe, gates, w1, w3, w2, tm=512):
    e, c, d = xe.shape
    f = w1.shape[2]
    tm = min(tm, c)
    return pl.pallas_call(
        _expert_ffn_kernel,
        out_shape=jax.ShapeDtypeStruct((e, c, d), BF16),
        grid=(e, c // tm),
        in_specs=[pl.BlockSpec((None, tm, d), lambda ei, ci: (ei, ci, 0)),
                  pl.BlockSpec((None, tm, 1), lambda ei, ci: (ei, ci, 0)),
                  pl.BlockSpec((None, d, f), lambda ei, ci: (ei, 0, 0)),
                  pl.BlockSpec((None, d, f), lambda ei, ci: (ei, 0, 0)),
                  pl.BlockSpec((None, f, d), lambda ei, ci: (ei, 0, 0))],
        out_specs=pl.BlockSpec((None, tm, d), lambda ei, ci: (ei, ci, 0)),
        scratch_shapes=[pltpu.VMEM((d, f), BF16), pltpu.VMEM((d, f), BF16), pltpu.VMEM((f, d), BF16)],
        compiler_params=_params("parallel", "arbitrary"),
    )(xe, gates[..., None], w1, w3, w2)


def _moe_layer(x, xn, aff_t, group_sizes, w1, w3, w2, split_output):
    n, d = x.shape
    gts, idxs = [], []
    start = 0
    for ng in group_sizes:
        cap = EC_CAPACITY * ng // N_EXPERTS
        gates, idx = lax.top_k(aff_t[:, start:start + ng], cap)
        gts.append(gates)
        idxs.append(idx + start)
        start += ng
    idx_all = jnp.concatenate(idxs, axis=1)
    ye = _expert_ffn(xn[idx_all], jnp.concatenate(gts, axis=1), w1, w3, w2)
    return _combine(x, ye.reshape(-1, d), idx_all.reshape(-1), group_sizes if split_output else (n,))


COMBINE_TOKENS = 512
COMBINE_ROWS = 512


def _combine_kernel(tile_ref, blk_ref, live_ref, x_ref, tok_ref, ye_ref, *o_refs, split_tiles):
    w = pl.program_id(0)
    tile = tile_ref[w]
    first = jnp.logical_or(w == 0, tile != tile_ref[jnp.maximum(w - 1, 0)])
    tt = x_ref.shape[0]
    rows = lax.broadcasted_iota(jnp.int32, (tt, COMBINE_ROWS), 0)
    onehot = jnp.where(rows == tok_ref[...] - tile * tt, 1.0, 0.0).astype(BF16)
    add = jnp.dot(onehot, ye_ref[...], preferred_element_type=F32) * live_ref[w].astype(F32)
    lo_tile = 0
    for o_ref, n_tiles in zip(o_refs, split_tiles):
        mine = jnp.logical_and(tile >= lo_tile, tile < lo_tile + n_tiles)

        @pl.when(jnp.logical_and(mine, first))
        def _(o_ref=o_ref):
            o_ref[...] = x_ref[...] + add

        @pl.when(jnp.logical_and(mine, jnp.logical_not(first)))
        def _(o_ref=o_ref):
            o_ref[...] += add

        lo_tile += n_tiles


def _combine(x, ye, tok, splits):
    n, d = x.shape
    p = tok.shape[0]
    tt = min(COMBINE_TOKENS, min(splits))
    rb = COMBINE_ROWS
    assert p % rb == 0 and all(s % tt == 0 for s in splits) and sum(splits) == n
    tiles, nblk = n // tt, p // rb
    split_tiles = tuple(s // tt for s in splits)
    order = jnp.argsort(tok)
    tok_sorted = tok[order].astype(jnp.int32)
    ye_sorted = ye[order]

    edges = jnp.arange(tiles + 1, dtype=jnp.int32) * tt
    bounds = jnp.searchsorted(tok_sorted, edges, method='compare_all').astype(jnp.int32)
    lo, hi = bounds[:-1], bounds[1:]
    first_blk = jnp.minimum(lo // rb, nblk - 1)
    last_blk = jnp.where(hi > lo, (hi - 1) // rb, first_blk)
    n_items = last_blk - first_blk + 1
    item_end = jnp.cumsum(n_items)
    item_start = item_end - n_items
    max_items = nblk + 2 * tiles
    w = jnp.arange(max_items, dtype=jnp.int32)
    tile = jnp.minimum(jnp.searchsorted(item_end, w, side='right', method='compare_all'),
                       tiles - 1).astype(jnp.int32)
    k = w - item_start[tile]
    live = (k < n_items[tile]).astype(jnp.int32)
    blk = jnp.minimum(first_blk[tile] + k, nblk - 1).astype(jnp.int32)

    def out_spec(lo_tile, n_tiles):
        return pl.BlockSpec(
            (tt, d), lambda i, tile_r, blk_r, live_r: (jnp.clip(tile_r[i] - lo_tile, 0, n_tiles - 1), 0))

    starts = [sum(split_tiles[:j]) for j in range(len(splits))]
    grid_spec = pltpu.PrefetchScalarGridSpec(
        num_scalar_prefetch=3,
        grid=(max_items,),
        in_specs=[pl.BlockSpec((tt, d), lambda i, tile_r, blk_r, live_r: (tile_r[i], 0)),
                  pl.BlockSpec((None, 1, rb), lambda i, tile_r, blk_r, live_r: (blk_r[i], 0, 0)),
                  pl.BlockSpec((rb, d), lambda i, tile_r, blk_r, live_r: (blk_r[i], 0))],
        out_specs=tuple(out_spec(s, t) for s, t in zip(starts, split_tiles)),
    )
    return pl.pallas_call(
        functools.partial(_combine_kernel, split_tiles=split_tiles),
        out_shape=tuple(jax.ShapeDtypeStruct((s, d), F32) for s in splits),
        grid_spec=grid_spec,
        compiler_params=_params("arbitrary"),
    )(tile, blk, live, x, tok_sorted.reshape(nblk, 1, rb), ye_sorted)


def _trunk(x, batch, seq_len, group_sizes, p):
    depth = p['norm_gain'].shape[0]
    layers = ((_hyena_layer, 'hy_'), (_gqa_layer, 'ga_'), (_mlstm_layer, 'ml_'), (_dilated_layer, 'da_'))
    for i in range(depth):
        layer, prefix = layers[i % 4]
        lp = {k[len(prefix):]: v[i // 4] for k, v in p.items() if k.startswith(prefix)}
        route = (p['norm_gain'][i, 1], p['moe_w_router'][i])
        x, xn, aff_t = layer(x, batch, seq_len, p['norm_gain'][i, 0], lp, route)
        last = i == depth - 1
        outs = _moe_layer(x, xn, aff_t, group_sizes, p['moe_w1'][i], p['moe_w3'][i], p['moe_w2'][i],
                          split_output=last)
        x = outs if last else outs[0]
    return x


def kernel(x_prompt, x_sample, norm_gain, hy_w_in, hy_b_in, hy_conv_w, hy_conv_b, hy_f_w1, hy_f_b1, hy_f_w2, hy_f_b2, hy_f_w3, hy_f_b3, hy_f_freq, hy_decay, hy_skip, hy_w_out, hy_b_out, ga_w_qkv, ga_q_gain, ga_k_gain, ga_w_o, ml_w_up, ml_conv_w, ml_conv_b, ml_w_q, ml_w_k, ml_w_v, ml_w_gate, ml_b_gate, ml_norm_gain, ml_skip, ml_w_down, da_w_qkv, da_q_gain, da_k_gain, da_w_o, moe_w_router, moe_w1, moe_w3, moe_w2):
    p = dict(
        norm_gain=norm_gain,
        hy_w_in=hy_w_in, hy_b_in=hy_b_in, hy_conv_w=hy_conv_w, hy_conv_b=hy_conv_b,
        hy_f_w1=hy_f_w1, hy_f_b1=hy_f_b1, hy_f_w2=hy_f_w2, hy_f_b2=hy_f_b2,
        hy_f_w3=hy_f_w3, hy_f_b3=hy_f_b3, hy_f_freq=hy_f_freq, hy_decay=hy_decay,
        hy_skip=hy_skip, hy_w_out=hy_w_out, hy_b_out=hy_b_out,
        ga_w_qkv=ga_w_qkv, ga_q_gain=ga_q_gain, ga_k_gain=ga_k_gain, ga_w_o=ga_w_o,
        ml_w_up=ml_w_up, ml_conv_w=ml_conv_w, ml_conv_b=ml_conv_b, ml_w_q=ml_w_q,
        ml_w_k=ml_w_k, ml_w_v=ml_w_v, ml_w_gate=ml_w_gate, ml_b_gate=ml_b_gate,
        ml_norm_gain=ml_norm_gain, ml_skip=ml_skip, ml_w_down=ml_w_down,
        da_w_qkv=da_w_qkv, da_q_gain=da_q_gain, da_k_gain=da_k_gain, da_w_o=da_w_o,
        moe_w_router=moe_w_router, moe_w1=moe_w1, moe_w3=moe_w3, moe_w2=moe_w2,
    )
    bp, seq_len, d = x_prompt.shape
    bs = x_sample.shape[0]
    assert x_sample.shape[1] == seq_len
    x = jnp.concatenate([x_prompt, x_sample], axis=0).reshape((bp + bs) * seq_len, d)
    y_prompt, y_sample = _trunk(x, bp + bs, seq_len, (bp * seq_len, bs * seq_len), p)
    return (y_prompt.reshape(bp, seq_len, d), y_sample.reshape(bs, seq_len, d))
```

```python
import functools
import math

import jax
import jax.numpy as jnp
from jax import lax
from jax.experimental import pallas as pl
from jax.experimental.pallas import tpu as pltpu

F32 = jnp.float32
BF16 = jnp.bfloat16
HIGHEST = lax.Precision.HIGHEST

NORM_EPS = 1e-6
GRID_W = 64
HY_BANDS = 16
GA_HEADS = 8
GA_KV_HEADS = 2
GA_GROUP = GA_HEADS // GA_KV_HEADS
HEAD_DIM = 128
AXIAL_THETA = 10000.0
ML_HEADS = 4
ML_QKV_BLOCK = 4
DA_GROUPS = ((128, 1), (512, 4), (2048, 16))
DA_HEADS_PER_GROUP = 4
DA_HEADS = DA_HEADS_PER_GROUP * len(DA_GROUPS)
ROPE_THETA = 500000.0
ROPE_DIMS = HEAD_DIM // 4
N_EXPERTS = 16
EC_CAPACITY = 2

VMEM_LIMIT_BYTES = 52 * 1024 * 1024
HALO_ROWS = 8
MXU_DIM = 256
ML_CHUNK = 256
BAND_BLOCK = 128
BAND_HALF = 64
BAND_SUBBLOCKS = 4
DA_OUT_ROWS = 256


def _params(*sem):
    return pltpu.CompilerParams(dimension_semantics=sem, vmem_limit_bytes=VMEM_LIMIT_BYTES)


def _rms(x, g):
    ms = jnp.mean(x * x, axis=-1, keepdims=True)
    return x * lax.rsqrt(ms + NORM_EPS) * g


def _const_spec(shape, single=False):
    nd = len(shape)
    if single:
        return pl.BlockSpec(shape, lambda *_: (0,) * nd, pipeline_mode=pl.Buffered(1))
    return pl.BlockSpec(shape, lambda *_: (0,) * nd)


def _conv3_rows(p, pprev, pnext, cw, cb, rows, tm):
    up = jnp.where(rows == 0, pprev, pltpu.roll(p, 1, 0))
    dn = jnp.where(rows == tm - 1, pnext, pltpu.roll(p, tm - 1, 0))
    return up * cw[0:1] + p * cw[1:2] + dn * cw[2:3] + cb


def _halo_specs(tm, d, n_rows):
    hb = tm // HALO_ROWS
    last = n_rows // HALO_ROWS - 1
    prev = pl.BlockSpec((HALO_ROWS, d), lambda i: (jnp.maximum(i * hb - 1, 0), 0))
    nxt = pl.BlockSpec((HALO_ROWS, d), lambda i: (jnp.minimum((i + 1) * hb, last), 0))
    return prev, nxt


def _edge_scales(i, tm, seq_len):
    t0 = i * tm
    keep_prev = jnp.where(t0 % seq_len == 0, 0.0, 1.0).astype(F32)
    keep_next = jnp.where((t0 + tm) % seq_len == 0, 0.0, 1.0).astype(F32)
    return keep_prev, keep_next


def _route(x, g_ref, wr_ref, xn_ref, aff_ref):
    xn = _rms(x, g_ref[...])
    xn_ref[...] = xn.astype(BF16)
    logits = lax.dot_general(wr_ref[...], xn, (((1,), (1,)), ((), ())),
                             precision=HIGHEST, preferred_element_type=F32)
    m = jnp.max(logits, axis=0, keepdims=True)
    e = jnp.exp(logits - m)
    aff_ref[...] = e / jnp.sum(e, axis=0, keepdims=True)


def _mm_res_kernel(*refs, prologue, n_row, n_const, tn):
    row_refs = refs[:n_row]
    const_refs = refs[n_row:n_row + n_const]
    w_ref, b_ref, res_ref, g2_ref, wr_ref, o_ref, xn_ref, aff_ref = refs[n_row + n_const:]
    lhs = prologue(*[r[...] for r in row_refs], *[c[...] for c in const_refs]).astype(BF16)
    for j in range(o_ref.shape[1] // tn):
        sl = slice(j * tn, (j + 1) * tn)
        o_ref[:, sl] = (res_ref[:, sl] + b_ref[:, sl]
                        + jnp.dot(lhs, w_ref[:, sl], preferred_element_type=F32))
    _route(o_ref[...], g2_ref, wr_ref, xn_ref, aff_ref)


def _mm_res(prologue, rows, consts, w, b, res, route, tm=512, tn=512):
    n, dout = res.shape
    tm = min(tm, n)
    k = w.shape[0]
    g2, w_router = route
    ne = w_router.shape[1]
    if b is None:
        b = jnp.zeros((1, dout), F32)
    in_specs = [pl.BlockSpec((tm * r.shape[0] // n, r.shape[1]), lambda i: (i, 0)) if r.ndim == 2
                else pl.BlockSpec((r.shape[0], tm, r.shape[2]), lambda i: (0, i, 0)) for r in rows]
    in_specs += [_const_spec(c.shape) for c in consts]
    in_specs += [_const_spec((k, dout)), _const_spec((1, dout)),
                 pl.BlockSpec((tm, dout), lambda i: (i, 0)),
                 _const_spec((1, dout)), _const_spec((ne, dout))]
    tile = pl.BlockSpec((tm, dout), lambda i: (i, 0))
    return pl.pallas_call(
        functools.partial(_mm_res_kernel, prologue=prologue, n_row=len(rows),
                          n_const=len(consts), tn=min(tn, dout)),
        out_shape=(jax.ShapeDtypeStruct((n, dout), F32), jax.ShapeDtypeStruct((n, dout), BF16),
                   jax.ShapeDtypeStruct((ne, n), F32)),
        grid=(n // tm,),
        in_specs=in_specs,
        out_specs=(tile, tile, pl.BlockSpec((ne, tm), lambda i: (0, i))),
        compiler_params=_params("parallel"),
    )(*rows, *consts, w.astype(BF16), b.reshape(1, dout).astype(F32), res,
      g2.reshape(1, dout), w_router.T)


def _hyena_in_kernel(x_ref, xp_ref, xn_ref, g_ref, w_ref, b_ref, cw_ref, cb_ref,
                     x0_ref, vx_ref, *, seq_len, tm, d, cols):
    keep_prev, keep_next = _edge_scales(pl.program_id(0), tm, seq_len)
    g = g_ref[...]
    xb = _rms(jnp.concatenate([x_ref[...], xp_ref[...], xn_ref[...]], axis=0), g).astype(BF16)
    rows = lax.broadcasted_iota(jnp.int32, (tm, 1), 0)

    def conv_part(c0):
        sl = slice(c0, c0 + cols)
        pa = jnp.dot(xb, w_ref[:, sl], preferred_element_type=F32) + b_ref[:, sl]
        p = pa[:tm]
        pprev = pa[tm + HALO_ROWS - 1:tm + HALO_ROWS, :] * keep_prev
        pnext = pa[tm + HALO_ROWS:tm + HALO_ROWS + 1, :] * keep_next
        return _conv3_rows(p, pprev, pnext, cw_ref[:, sl], cb_ref[:, sl], rows, tm)

    for j in range(d // cols):
        c = j * cols
        x0_ref[:, c:c + cols] = conv_part(c)
        vx_ref[:, c:c + cols] = conv_part(2 * d + c) * conv_part(d + c)


def _hyena_in(x, g, w_in, b_in, conv_w, conv_b, seq_len, tm=512, cols=512):
    n, d = x.shape
    tm = min(tm, seq_len)
    prev, nxt = _halo_specs(tm, d, n)
    row = pl.BlockSpec((tm, d), lambda i: (i, 0))
    return pl.pallas_call(
        functools.partial(_hyena_in_kernel, seq_len=seq_len, tm=tm, d=d, cols=cols),
        out_shape=(jax.ShapeDtypeStruct((n, d), F32), jax.ShapeDtypeStruct((n, d), F32)),
        grid=(n // tm,),
        in_specs=[row, prev, nxt, _const_spec((1, d)), _const_spec((d, 3 * d)),
                  _const_spec((1, 3 * d)), _const_spec((3, 3 * d)), _const_spec((1, 3 * d))],
        out_specs=(row, row),
        compiler_params=_params("parallel"),
    )(x, x, x, g.reshape(1, d), w_in.astype(BF16), b_in.reshape(1, 3 * d),
      conv_w, conv_b.reshape(1, 3 * d))


def _hyena_filter_taps(L, d, f_w1, f_b1, f_w2, f_b2, f_w3, f_b3, f_freq, decay):
    t = jnp.linspace(0.0, 1.0, L, dtype=F32)[:, None]
    w_ang = 2.0 * math.pi * jnp.arange(L, dtype=F32)[:, None] / L
    bands = jnp.linspace(1e-4, HY_BANDS - 1, HY_BANDS, dtype=F32)[None, :]
    z = jnp.concatenate([t, jnp.cos(bands * w_ang), -jnp.sin(bands * w_ang)], axis=-1)

    def branch(zz, tt, col0):
        h = jnp.sin(f_freq[0] * (zz @ f_w1 + f_b1))
        h = jnp.sin(f_freq[1] * (h @ f_w2 + f_b2))
        h = h @ f_w3[:, col0:col0 + d] + f_b3[col0:col0 + d]
        return h * jnp.exp(-tt * jnp.abs(decay[col0 // d])[None])

    h_fwd = branch(z, t, 0)
    h_bwd_rev = branch(z[::-1], t[::-1], d)
    k = jnp.concatenate([h_fwd, jnp.zeros((1, d), F32), h_bwd_rev[:-1]], axis=0)
    return k / jnp.sum(jnp.abs(k), axis=0, keepdims=True)


def _fft_dims(m):
    lg = m.bit_length() - 1
    p = 1 << ((lg + 1) // 2)
    return p, m // p


def _split_bf16(x):
    hi = x.astype(BF16)
    return hi, (x - hi.astype(F32)).astype(BF16)


def _mm_split(fh, fl, x, precise):
    if not precise:
        return jnp.dot(fh, x.astype(BF16), preferred_element_type=F32)
    xh, xl = _split_bf16(x)
    return (jnp.dot(fh, xh, preferred_element_type=F32) + jnp.dot(fl, xh, preferred_element_type=F32)
            + jnp.dot(fh, xl, preferred_element_type=F32))


def _cplx_as_real(cr, ci):
    top = jnp.concatenate([cr, -ci], axis=-1)
    bot = jnp.concatenate([ci, cr], axis=-1)
    return jnp.concatenate([top, bot], axis=-2)


def _unit_circle(idx, m):
    ang = (2.0 * math.pi / m) * idx.astype(F32)
    return jnp.cos(ang), jnp.sin(ang)


def _dft_consts(p, q):
    m = p * q
    k1 = jnp.arange(p, dtype=jnp.int32)
    n1 = jnp.arange(p // 2, dtype=jnp.int32)
    c, s = _unit_circle((k1[:, None] * n1[None, :]) % p, p)
    fa = _cplx_as_real(c, -s)
    c, s = _unit_circle((n1[:, None] * k1[None, :]) % p, p)
    fd = _cplx_as_real(c / m, s / m)
    k2 = jnp.arange(q, dtype=jnp.int32)
    n2 = jnp.arange(q, dtype=jnp.int32)
    idx = (n2[None, None, :] * (k2[None, :, None] * p + k1[:, None, None])) % m
    c, s = _unit_circle(idx, m)
    gb = _cplx_as_real(c, -s)
    ct, st = jnp.swapaxes(c, 1, 2), jnp.swapaxes(s, 1, 2)
    gc = _cplx_as_real(ct, st)
    return tuple(_split_bf16(a) for a in (fa, gb, gc, fd))


def _fft_a_kernel(x_ref, fh_ref, fl_ref, o_ref, *, precise):
    _, rows_in, group, d = x_ref.shape
    rows_out = o_ref.shape[1]
    for j in range(group):
        x = x_ref[:, :, j, :].reshape(2 * rows_in, d)
        y = _mm_split(fh_ref[...], fl_ref[...], x, precise)
        o_ref[:, :, j, :] = y.reshape(2, rows_out, d)


def _fft_b_kernel(a_ref, gh_ref, gl_ref, o_ref):
    _, q, d = a_ref.shape
    x = _mm_split(gh_ref[...], gl_ref[...], a_ref[...].reshape(2 * q, d), True)
    o_ref[...] = x.reshape(o_ref.shape)


def _fft_bc_kernel(a_ref, gb_ref, gc_ref, k_ref, z_ref):
    _, q, d = a_ref.shape
    x = _mm_split(gb_ref[...], None, a_ref[...].reshape(2 * q, d), False)
    xr, xi = x[:q], x[q:]
    kr, ki = k_ref[0], k_ref[1]
    y = jnp.concatenate([xr * kr - xi * ki, xr * ki + xi * kr], axis=0)
    z = _mm_split(gc_ref[...], None, y, False)
    z_ref[...] = z.reshape(z_ref.shape)


def _fft_rows(x5, f, rows_out, precise):
    pairs, _, rows_in, q, d = x5.shape
    blk = lambda r: pl.BlockSpec((None, 2, r, HALO_ROWS, d), lambda b, j: (b, 0, 0, j, 0))
    return pl.pallas_call(
        functools.partial(_fft_a_kernel, precise=precise),
        out_shape=jax.ShapeDtypeStruct((pairs, 2, rows_out, q, d), F32),
        grid=(pairs, q // HALO_ROWS),
        in_specs=[blk(rows_in), _const_spec(f[0].shape), _const_spec(f[1].shape)],
        out_specs=blk(rows_out),
        compiler_params=_params("parallel", "parallel"),
    )(x5, *f)


def _long_conv(vx, taps, batch, seq_len):
    n, d = vx.shape
    m = 2 * seq_len
    p, q = _fft_dims(m)
    assert batch % 2 == 0
    pairs = batch // 2
    fa, gb, gc, fd = _dft_consts(p, q)
    g_spec = pl.BlockSpec((None, 2 * q, 2 * q), lambda k1, b: (k1, 0, 0))
    slab = pl.BlockSpec((None, 2, q, d), lambda k1, b: (b, 0, k1, 0))

    zeros = jnp.zeros((seq_len, d), F32)
    kin = jnp.stack([taps[:seq_len], zeros, taps[seq_len:], zeros]).reshape(2, 2, p // 2, q, d)
    ka = _fft_rows(kin, fa, p, True).reshape(2, 2, p * q, d)
    kx = pl.pallas_call(
        _fft_b_kernel,
        out_shape=jax.ShapeDtypeStruct((2, 2, p * q, d), F32),
        grid=(p, 2),
        in_specs=[slab, g_spec, g_spec],
        out_specs=slab,
        compiler_params=_params("parallel", "parallel"),
    )(ka, *gb)
    sign = jnp.repeat(1.0 - 2.0 * (jnp.arange(p) % 2).astype(F32), q)[None, :, None]
    kspec = kx[0] + sign * kx[1]

    xa = _fft_rows(vx.reshape(pairs, 2, p // 2, q, d), fa, p, False).reshape(pairs, 2, p * q, d)
    z = pl.pallas_call(
        _fft_bc_kernel,
        out_shape=jax.ShapeDtypeStruct((pairs, 2, p * q, d), F32),
        grid=(p, pairs),
        in_specs=[slab, g_spec, g_spec, pl.BlockSpec((2, q, d), lambda k1, b: (0, k1, 0))],
        out_specs=slab,
        compiler_params=_params("parallel", "parallel"),
    )(xa, gb[0], gc[0], kspec)
    y = _fft_rows(z.reshape(pairs, 2, p, q, d), fd, p // 2, False)
    return y.reshape(n, d)


def _hyena_out_prologue(y, vx, x0, skip):
    return (y + vx * skip) * x0


def _hyena_layer(x, batch, seq_len, g, p, route):
    n, d = x.shape
    x0, vx = _hyena_in(x, g, p['w_in'], p['b_in'], p['conv_w'], p['conv_b'], seq_len)
    taps = _hyena_filter_taps(seq_len, d, p['f_w1'], p['f_b1'], p['f_w2'], p['f_b2'],
                              p['f_w3'], p['f_b3'], p['f_freq'], p['decay'])
    y = _long_conv(vx, taps, batch, seq_len)
    return _mm_res(_hyena_out_prologue, [y, vx, x0], [p['skip'].reshape(1, d)],
                   p['w_out'], p['b_out'], x, route)


def _head_norm_rope(xh, gain, cos, sin, half):
    y = _rms(xh, gain)
    lane = lax.broadcasted_iota(jnp.int32, (1, HEAD_DIM), 1)
    fwd = pltpu.roll(y, HEAD_DIM - half, 1)
    bwd = pltpu.roll(y, half, 1)
    partner = jnp.where((lane % (2 * half)) < half, fwd, bwd)
    return y * cos + partner * sin


def _qkv_rope_kernel(x_ref, g_ref, w_ref, qg_ref, kg_ref, cos_ref, sin_ref,
                     q_ref, k_ref, v_ref, *, nq, nk, nv, half):
    xb = _rms(x_ref[...], g_ref[...]).astype(BF16)
    cos = cos_ref[...]
    sin = sin_ref[...]
    per = MXU_DIM // HEAD_DIM
    for h0 in range(0, nq + nk + nv, per):
        pw = jnp.dot(xb, w_ref[:, h0 * HEAD_DIM:(h0 + per) * HEAD_DIM], preferred_element_type=F32)
        for h in range(h0, h0 + per):
            ph = pw[:, (h - h0) * HEAD_DIM:(h - h0 + 1) * HEAD_DIM]
            if h < nq:
                sl = slice(h * HEAD_DIM, (h + 1) * HEAD_DIM)
                q_ref[:, sl] = _head_norm_rope(ph, qg_ref[...], cos, sin, half).astype(BF16)
            elif h < nq + nk:
                sl = slice((h - nq) * HEAD_DIM, (h - nq + 1) * HEAD_DIM)
                k_ref[:, sl] = _head_norm_rope(ph, kg_ref[...], cos, sin, half).astype(BF16)
            else:
                sl = slice((h - nq - nk) * HEAD_DIM, (h - nq - nk + 1) * HEAD_DIM)
                v_ref[:, sl] = ph.astype(BF16)


def _qkv_rope(x, g, w_qkv, q_gain, k_gain, cos, sin, nq, nk, nv, half, seq_len, tm=512):
    n, d = x.shape
    tm = min(tm, seq_len)
    pos_blocks = seq_len // tm
    f = w_qkv.shape[1]
    row = pl.BlockSpec((tm, d), lambda i: (i, 0))
    tab = pl.BlockSpec((tm, HEAD_DIM), lambda i: (i % pos_blocks, 0))
    outs = tuple(jax.ShapeDtypeStruct((n, c * HEAD_DIM), BF16) for c in (nq, nk, nv))
    return pl.pallas_call(
        functools.partial(_qkv_rope_kernel, nq=nq, nk=nk, nv=nv, half=half),
        out_shape=outs,
        grid=(n // tm,),
        in_specs=[row, _const_spec((1, d)), _const_spec((d, f)),
                  _const_spec((1, HEAD_DIM)), _const_spec((1, HEAD_DIM)), tab, tab],
        out_specs=tuple(pl.BlockSpec((tm, c * HEAD_DIM), lambda i: (i, 0)) for c in (nq, nk, nv)),
        compiler_params=_params("parallel"),
    )(x, g.reshape(1, d), w_qkv.astype(BF16), q_gain.reshape(1, HEAD_DIM),
      k_gain.reshape(1, HEAD_DIM), cos, sin)


def _axial_tables(L):
    t = jnp.arange(L)
    r = (t // GRID_W).astype(F32)
    c = (t % GRID_W).astype(F32)
    nf = HEAD_DIM // 4
    inv = AXIAL_THETA ** (-(2.0 * jnp.arange(nf, dtype=F32)) / (2 * nf))
    ar, ac = r[:, None] * inv[None], c[:, None] * inv[None]
    cos = jnp.concatenate([jnp.cos(ar), jnp.cos(ar), jnp.cos(ac), jnp.cos(ac)], axis=-1)
    sin = jnp.concatenate([-jnp.sin(ar), jnp.sin(ar), -jnp.sin(ac), jnp.sin(ac)], axis=-1)
    return cos, sin


FLASH_SAFE_BOUND = 40.0
FLASH_BOUND_MARGIN = 1.001


def _flash_kernel(q_ref, k_ref, v_ref, o_ref, acc_scr, off_scr, lsum_scr, m_scr, l_scr, kn_scr,
                  *, tq, tk, tkf, seq_len, group):
    scale = HEAD_DIM ** -0.5
    c = scale * math.log2(math.e)

    @pl.when(pl.program_id(2) == 0)
    def _():
        def norm_step(j, mx):
            kt = k_ref[pl.ds(pl.multiple_of(j * tk, tk), tk), :].astype(F32)
            row = jnp.sum(kt * kt, axis=1, keepdims=True)
            return jnp.maximum(mx, jnp.max(row, axis=0, keepdims=True))

        kmax2 = lax.fori_loop(0, seq_len // tk, norm_step, jnp.zeros((1, 1), F32))
        kn_scr[...] = jnp.broadcast_to(kmax2, kn_scr.shape)

    kmax2 = kn_scr[0:1, 0:1]
    bmax = jnp.zeros((1, 1), F32)
    for h in range(group):
        qf = q_ref[:, h * HEAD_DIM:(h + 1) * HEAD_DIM].astype(F32)
        b = jnp.sqrt(jnp.sum(qf * qf, axis=1, keepdims=True) * kmax2) * FLASH_BOUND_MARGIN
        bmax = jnp.maximum(bmax, jnp.max(b, axis=0, keepdims=True))
        off_scr[h] = jnp.broadcast_to(b * c, (tq, HEAD_DIM))
    fast = (bmax * scale)[0, 0] <= FLASH_SAFE_BOUND

    @pl.when(fast)
    def _():
        acc_scr[...] = jnp.zeros_like(acc_scr)
        lsum_scr[...] = jnp.zeros_like(lsum_scr)
        n_lane_tiles = tkf // HEAD_DIM

        def body(j, carry):
            start = pl.multiple_of(j * tkf, tkf)
            kt = k_ref[pl.ds(start, tkf), :]
            vt = v_ref[pl.ds(start, tkf), :]
            for h in range(group):
                q = q_ref[:, h * HEAD_DIM:(h + 1) * HEAD_DIM]
                s = lax.dot_general(q, kt, (((1,), (1,)), ((), ())), preferred_element_type=F32)
                off = off_scr[h]
                p = jnp.exp2(s * c - jnp.concatenate([off] * n_lane_tiles, axis=1))
                part = p[:, 0:HEAD_DIM]
                for t in range(1, n_lane_tiles):
                    part = part + p[:, t * HEAD_DIM:(t + 1) * HEAD_DIM]
                lsum_scr[h] += part
                acc_scr[h] += jnp.dot(p.astype(BF16), vt, preferred_element_type=F32)
            return carry

        lax.fori_loop(0, seq_len // tkf, body, 0)
        for h in range(group):
            l = jnp.sum(lsum_scr[h], axis=1, keepdims=True)
            o_ref[:, h * HEAD_DIM:(h + 1) * HEAD_DIM] = (acc_scr[h] / l).astype(o_ref.dtype)

    @pl.when(jnp.logical_not(fast))
    def _():
        m_scr[...] = jnp.full(m_scr.shape, -jnp.inf, F32)
        l_scr[...] = jnp.zeros_like(l_scr)
        acc_scr[...] = jnp.zeros_like(acc_scr)

        def body(j, carry):
            start = pl.multiple_of(j * tk, tk)
            kt = k_ref[pl.ds(start, tk), :]
            vt = v_ref[pl.ds(start, tk), :]
            for h in range(group):
                q = q_ref[:, h * HEAD_DIM:(h + 1) * HEAD_DIM]
                s = lax.dot_general(q, kt, (((1,), (1,)), ((), ())), preferred_element_type=F32)
                m = m_scr[h]
                m_new = jnp.maximum(m, jnp.max(s, axis=1, keepdims=True))
                alpha = jnp.exp2((m - m_new) * c)
                p = jnp.exp2(s * c - m_new * c)
                l_scr[h] = alpha * l_scr[h] + jnp.sum(p, axis=1, keepdims=True)
                acc_scr[h] = alpha * acc_scr[h] + jnp.dot(p.astype(BF16), vt, preferred_element_type=F32)
                m_scr[h] = m_new
            return carry

        lax.fori_loop(0, seq_len // tk, body, 0)
        for h in range(group):
            o_ref[:, h * HEAD_DIM:(h + 1) * HEAD_DIM] = (acc_scr[h] / l_scr[h]).astype(o_ref.dtype)


def _flash_gqa(q, k, v, batch, seq_len, tq=1024, tk=512, tkf=1024):
    tq = min(tq, seq_len)
    tk = min(tk, seq_len)
    tkf = min(tkf, seq_len)
    nq = seq_len // tq
    gw = GA_GROUP * HEAD_DIM
    wide = pltpu.VMEM((GA_GROUP, tq, HEAD_DIM), F32)
    thin = pltpu.VMEM((GA_GROUP, tq, 1), F32)
    return pl.pallas_call(
        functools.partial(_flash_kernel, tq=tq, tk=tk, tkf=tkf, seq_len=seq_len, group=GA_GROUP),
        out_shape=jax.ShapeDtypeStruct(q.shape, BF16),
        grid=(batch, GA_KV_HEADS, nq),
        in_specs=[pl.BlockSpec((tq, gw), lambda b, kv, i: (b * nq + i, kv)),
                  pl.BlockSpec((seq_len, HEAD_DIM), lambda b, kv, i: (b, kv)),
                  pl.BlockSpec((seq_len, HEAD_DIM), lambda b, kv, i: (b, kv))],
        out_specs=pl.BlockSpec((tq, gw), lambda b, kv, i: (b * nq + i, kv)),
        scratch_shapes=[wide, wide, wide, thin, thin, pltpu.VMEM((HALO_ROWS, HEAD_DIM), F32)],
        compiler_params=_params("parallel", "parallel", "arbitrary"),
    )(q, k, v)


def _identity_prologue(o):
    return o


def _gqa_layer(x, batch, seq_len, g, p, route):
    cos, sin = _axial_tables(seq_len)
    q, k, v = _qkv_rope(x, g, p['w_qkv'], p['q_gain'], p['k_gain'], cos, sin,
                        GA_HEADS, GA_KV_HEADS, GA_KV_HEADS, HEAD_DIM // 4, seq_len)
    o = _flash_gqa(q, k, v, batch, seq_len)
    return _mm_res(_identity_prologue, [o], [], p['w_o'], None, x, route)


def _ml_in_kernel(x_ref, xp_ref, xn_ref, g_ref, w_ref, cw_ref, cb_ref, wq_ref, wk_ref, wv_ref,
                  wg_ref, bg_ref, q_ref, k_ref, v_ref, xc_ref, sz_ref, gate_ref,
                  *, seq_len, tm, inner, k_scale):
    keep_prev, keep_next = _edge_scales(pl.program_id(0), tm, seq_len)
    g = g_ref[...]
    xa = _rms(jnp.concatenate([x_ref[...], xp_ref[...], xn_ref[...]], axis=0), g).astype(BF16)
    xb = xa[:tm]
    rows = lax.broadcasted_iota(jnp.int32, (tm, 1), 0)
    gacc = jnp.zeros(gate_ref.shape, F32)
    for t in range(inner // MXU_DIM):
        sl = slice(t * MXU_DIM, (t + 1) * MXU_DIM)
        xma = jnp.dot(xa, w_ref[:, sl], preferred_element_type=F32)
        xm = xma[:tm]
        pprev = xma[tm + HALO_ROWS - 1:tm + HALO_ROWS, :] * keep_prev
        pnext = xma[tm + HALO_ROWS:tm + HALO_ROWS + 1, :] * keep_next
        xc = _conv3_rows(xm, pprev, pnext, cw_ref[:, sl], cb_ref[:, sl], rows, tm)
        xc = xc * jax.nn.sigmoid(xc)
        z = jnp.dot(xb, w_ref[:, inner + t * MXU_DIM:inner + (t + 1) * MXU_DIM],
                    preferred_element_type=F32)
        xcb = xc.astype(BF16)
        q = jnp.dot(xcb, wq_ref[t], preferred_element_type=F32)
        k = jnp.dot(xcb, wk_ref[t], preferred_element_type=F32)
        v = jnp.dot(xm.astype(BF16), wv_ref[t], preferred_element_type=F32)
        qb, kb, vb = q.astype(BF16), k.astype(BF16), v.astype(BF16)
        gacc += (jnp.dot(qb, wg_ref[0, sl, :], preferred_element_type=F32)
                 + jnp.dot(kb, wg_ref[1, sl, :], preferred_element_type=F32)
                 + jnp.dot(vb, wg_ref[2, sl, :], preferred_element_type=F32))
        q_ref[:, sl] = qb
        k_ref[:, sl] = (k * k_scale).astype(BF16)
        v_ref[:, sl] = vb
        xc_ref[:, sl] = xc
        sz_ref[:, sl] = z * jax.nn.sigmoid(z)
    gate_ref[...] = gacc + bg_ref[...]


def _block_diag_tiles(w):
    nb, c, _ = w.shape
    per = MXU_DIM // c
    wt = w.reshape(nb // per, per, c, c)
    eye = jnp.eye(per, dtype=w.dtype)
    full = jnp.einsum('tpcd,pq->tpcqd', wt, eye)
    return full.reshape(nb // per, MXU_DIM, MXU_DIM)


def _ml_in(x, g, p, seq_len, tm=512):
    n, d = x.shape
    tm = min(tm, seq_len)
    inner = p['w_up'].shape[1] // 2
    ng = 4 * ML_HEADS
    dh = inner // ML_HEADS
    wq, wk, wv = (_block_diag_tiles(p[nm]).astype(BF16) for nm in ('w_q', 'w_k', 'w_v'))
    wg = jnp.transpose(p['w_gate'], (1, 2, 0, 3)).reshape(3, inner, ng).astype(BF16)
    bg = p['b_gate'].reshape(1, ng)
    prev, nxt = _halo_specs(tm, d, n)
    row = pl.BlockSpec((tm, d), lambda i: (i, 0))
    wide = pl.BlockSpec((tm, inner), lambda i: (i, 0))
    nt = inner // MXU_DIM
    return pl.pallas_call(
        functools.partial(_ml_in_kernel, seq_len=seq_len, tm=tm, inner=inner, k_scale=dh ** -0.5),
        out_shape=(jax.ShapeDtypeStruct((n, inner), BF16),) * 3
        + (jax.ShapeDtypeStruct((n, inner), F32),) * 2
        + (jax.ShapeDtypeStruct((n, ng), F32),),
        grid=(n // tm,),
        in_specs=[row, prev, nxt, _const_spec((1, d)), _const_spec((d, 2 * inner), single=True),
                  _const_spec((3, inner)), _const_spec((1, inner)),
                  _const_spec((nt, MXU_DIM, MXU_DIM), single=True),
                  _const_spec((nt, MXU_DIM, MXU_DIM), single=True),
                  _const_spec((nt, MXU_DIM, MXU_DIM), single=True),
                  _const_spec((3, inner, ng), single=True), _const_spec((1, ng))],
        out_specs=(wide,) * 5 + (pl.BlockSpec((tm, ng), lambda i: (i, 0)),),
        compiler_params=_params("parallel"),
    )(x, x, x, g.reshape(1, d), p['w_up'].astype(BF16), p['conv_w'],
      p['conv_b'].reshape(1, inner), wq, wk, wv, wg, bg)


def _log_sigmoid(x):
    return jnp.minimum(x, 0.0) - jnp.log1p(jnp.exp(-jnp.abs(x)))


def _mlstm_chunk_kernel(qf_ref, kf_ref, vf_ref, gcf_ref, grf_ref, qb_ref, kb_ref, vb_ref, gcb_ref, grb_ref,
                        hf_ref, hb_ref, c_scr, n_scr, m_scr, *, lc):
    @pl.when(pl.program_id(2) == 0)
    def _():
        c_scr[...] = jnp.zeros_like(c_scr)
        n_scr[...] = jnp.zeros_like(n_scr)
        m_scr[...] = jnp.zeros_like(m_scr)

    _mlstm_step(qf_ref, kf_ref, vf_ref, gcf_ref, grf_ref, hf_ref, c_scr.at[0], n_scr.at[0], m_scr.at[0], lc, False)
    _mlstm_step(qb_ref, kb_ref, vb_ref, gcb_ref, grb_ref, hb_ref, c_scr.at[1], n_scr.at[1], m_scr.at[1], lc, True)


def _mlstm_step(q_ref, k_ref, v_ref, gc_ref, gr_ref, h_ref, c_scr, n_scr, m_scr, lc, backward):
    q = q_ref[...]
    k = k_ref[...]
    v = v_ref[...]
    gc = gc_ref[...]
    gr = gr_ref[...]
    i_col, g_col = gc[:, 0:1], gc[:, 1:2]
    i_row, g_row = gr[0:1, :], gr[1:2, :]

    jr = lax.broadcasted_iota(jnp.int32, (lc, lc), 0)
    sc = lax.broadcasted_iota(jnp.int32, (lc, lc), 1)
    seen = (sc >= jr) if backward else (sc <= jr)
    g_tot = g_row[:, 0:1] if backward else g_row[:, lc - 1:lc]
    m_old = m_scr[0:1, 0:1]

    dmat = jnp.where(seen, g_col - g_row + i_row, -jnp.inf)
    inter = g_col + m_old
    m_q = jnp.maximum(inter, jnp.max(dmat, axis=1, keepdims=True))
    s_qk = lax.dot_general(q, k, (((1,), (1,)), ((), ())), preferred_element_type=F32)
    a = s_qk * jnp.exp(dmat - m_q)
    w_int = jnp.exp(inter - m_q)
    q_c = jnp.dot(q, c_scr[...].astype(BF16), preferred_element_type=F32)
    num = jnp.dot(a.astype(BF16), v, preferred_element_type=F32) + q_c * w_int
    q_n = jnp.sum(q.astype(F32) * n_scr[...], axis=1, keepdims=True)
    den = jnp.sum(a, axis=1, keepdims=True) + w_int * q_n
    den = jnp.maximum(jnp.abs(den), jnp.exp(-m_q))
    h_ref[...] = num / den

    a_row = g_tot - g_row + i_row
    m_new = jnp.maximum(g_tot + m_old, jnp.max(a_row, axis=1, keepdims=True))
    ws_col = jnp.exp(g_tot - g_col + i_col - m_new)
    dec = jnp.exp(g_tot + m_old - m_new)
    kw = k.astype(F32) * ws_col
    upd = lax.dot_general(kw.astype(BF16), v, (((0,), (0,)), ((), ())), preferred_element_type=F32)
    c_scr[...] = dec * c_scr[...] + upd
    n_scr[...] = dec * n_scr[...] + jnp.sum(kw, axis=0, keepdims=True)
    m_scr[...] = jnp.broadcast_to(m_new, m_scr.shape)


def _ml_gate_kernel(g_ref, o_ref, *, lc, nh):
    g = g_ref[...]
    jr = lax.broadcasted_iota(jnp.int32, (lc, lc), 0)
    sc = lax.broadcasted_iota(jnp.int32, (lc, lc), 1)
    tril = jnp.where(sc <= jr, 1.0, 0.0).astype(BF16)
    triu = jnp.where(sc >= jr, 1.0, 0.0).astype(BF16)
    col = lax.broadcasted_iota(jnp.int32, (1, 4 * nh), 1)
    lf = _log_sigmoid(g)
    x1 = lf.astype(BF16)
    r1 = lf - x1.astype(F32)
    x2 = r1.astype(BF16)
    x3 = (r1 - x2.astype(F32)).astype(BF16)
    cum = lambda m: (jnp.dot(m, x1, preferred_element_type=F32) + jnp.dot(m, x2, preferred_element_type=F32)
                     + jnp.dot(m, x3, preferred_element_type=F32))
    gsum = jnp.where(col < 2 * nh, cum(tril), cum(triu))
    o_ref[...] = jnp.where((col % (2 * nh)) >= nh, gsum, g)


def _mlstm_chunks(q, k, v, gates, batch, seq_len):
    n, inner = q.shape
    dh = inner // ML_HEADS
    lc = min(ML_CHUNK, seq_len)
    nc = seq_len // lc
    ng = gates.shape[1]
    gates = pl.pallas_call(
        functools.partial(_ml_gate_kernel, lc=lc, nh=ML_HEADS),
        out_shape=jax.ShapeDtypeStruct((n, ng), F32),
        grid=(n // lc,),
        in_specs=[pl.BlockSpec((lc, ng), lambda i: (i, 0))],
        out_specs=pl.BlockSpec((lc, ng), lambda i: (i, 0)),
        compiler_params=_params("parallel"),
    )(gates)
    g4 = gates.reshape(n, 2, 2, ML_HEADS)
    gcol = jnp.transpose(g4, (1, 3, 0, 2))
    grow = jnp.transpose(g4, (1, 3, 2, 0))

    def specs(dd):
        chunk = (lambda b, c: b * nc + c) if dd == 0 else (lambda b, c: b * nc + nc - 1 - c)
        qkv = pl.BlockSpec((lc, dh), lambda b, h, c: (chunk(b, c), h))
        gc = pl.BlockSpec((None, None, lc, 2), lambda b, h, c: (dd, h, chunk(b, c), 0))
        gr = pl.BlockSpec((None, None, 2, lc), lambda b, h, c: (dd, h, 0, chunk(b, c)))
        return [qkv, qkv, qkv, gc, gr], qkv

    in_f, out_f = specs(0)
    in_b, out_b = specs(1)
    return pl.pallas_call(
        functools.partial(_mlstm_chunk_kernel, lc=lc),
        out_shape=(jax.ShapeDtypeStruct((n, inner), F32),) * 2,
        grid=(batch, ML_HEADS, nc),
        in_specs=in_f + in_b,
        out_specs=(out_f, out_b),
        scratch_shapes=[pltpu.VMEM((2, dh, dh), F32), pltpu.VMEM((2, 1, dh), F32),
                        pltpu.VMEM((2, HALO_ROWS, HEAD_DIM), F32)],
        compiler_params=_params("parallel", "parallel", "arbitrary"),
    )(q, k, v, gcol, grow, q, k, v, gcol, grow)


def _ml_out_prologue(hf, hb, xc, sz, gain, skip):
    h = hf + hb
    dh = h.shape[1] // ML_HEADS
    parts = []
    for i in range(ML_HEADS):
        sl = slice(i * dh, (i + 1) * dh)
        parts.append(_rms(h[:, sl], gain[:, sl]))
    hn = jnp.concatenate(parts, axis=1)
    return (hn + skip * xc) * sz


def _mlstm_layer(x, batch, seq_len, g, p, route):
    q, k, v, xc, sz, gates = _ml_in(x, g, p, seq_len)
    inner = q.shape[1]
    hf, hb = _mlstm_chunks(q, k, v, gates, batch, seq_len)
    return _mm_res(_ml_out_prologue, [hf, hb, xc, sz],
                   [p['norm_gain'].reshape(1, inner), p['skip'].reshape(1, inner)],
                   p['w_down'], None, x, route, tm=256)


def _rope_tables(L):
    inv = ROPE_THETA ** (-(2.0 * jnp.arange(ROPE_DIMS // 2, dtype=F32)) / ROPE_DIMS)
    ang = jnp.arange(L, dtype=F32)[:, None] * inv[None]
    pad = HEAD_DIM - ROPE_DIMS
    cos = jnp.concatenate([jnp.cos(ang), jnp.cos(ang), jnp.ones((L, pad), F32)], axis=-1)
    sin = jnp.concatenate([-jnp.sin(ang), jnp.sin(ang), jnp.zeros((L, pad), F32)], axis=-1)
    return cos, sin


def _band_kernel(q_ref, kp_ref, kc_ref, kn_ref, vp_ref, vc_ref, vn_ref, o_ref, lse_ref,
                 *, s_len, heads, subs):
    i = pl.program_id(2)
    qb = BAND_BLOCK
    w = qb + 2 * BAND_HALF
    a = lax.broadcasted_iota(jnp.int32, (qb, w), 0)
    c = lax.broadcasted_iota(jnp.int32, (qb, w), 1)
    in_band = jnp.abs(c - BAND_HALF - a) <= BAND_HALF
    scale = HEAD_DIM ** -0.5
    for h in range(heads):
        sl = slice(h * HEAD_DIM, (h + 1) * HEAD_DIM)
        kw = jnp.concatenate([kp_ref[qb - BAND_HALF:, sl], kc_ref[:, sl], kn_ref[:BAND_HALF, sl]], axis=0)
        vw = jnp.concatenate([vp_ref[qb - BAND_HALF:, sl], vc_ref[:, sl], vn_ref[:BAND_HALF, sl]], axis=0)
        for u in range(subs):
            rows = slice(u * qb, (u + 1) * qb)
            key_pos = (i * subs + u) * qb - BAND_HALF + c
            valid = in_band & (key_pos >= 0) & (key_pos < s_len)
            s = lax.dot_general(q_ref[rows, sl], kw[u * qb:u * qb + w], (((1,), (1,)), ((), ())),
                                preferred_element_type=F32) * scale
            s = jnp.where(valid, s, -jnp.inf)
            m = jnp.max(s, axis=1, keepdims=True)
            p = jnp.exp(s - m)
            l = jnp.sum(p, axis=1, keepdims=True)
            o = jnp.dot(p.astype(BF16), vw[u * qb:u * qb + w], preferred_element_type=F32)
            o_ref[rows, sl] = o / l
            lse_ref[rows, sl] = jnp.broadcast_to(m + jnp.log(l), (qb, HEAD_DIM))


def _da_qkv_kernel(x_ref, g_ref, w_ref, qg_ref, kg_ref, cos_ref, sin_ref, *refs, dils, tm, half):
    n_perm = sum(1 for d in dils if d > 1)
    perm_refs, out_refs = refs[:n_perm], refs[n_perm:]
    xb = _rms(x_ref[...], g_ref[...]).astype(BF16)
    cos, sin = cos_ref[...], sin_ref[...]
    gw = DA_HEADS_PER_GROUP * HEAD_DIM
    heads = DA_HEADS_PER_GROUP * len(dils)
    for kind in range(3):
        gain = (qg_ref, kg_ref, None)[kind]
        pi = 0
        for gi, dil in enumerate(dils):
            c0 = (kind * heads + gi * DA_HEADS_PER_GROUP) * HEAD_DIM
            ph = jnp.dot(xb, w_ref[:, c0:c0 + gw], preferred_element_type=F32)
            if gain is not None:
                ph = jnp.concatenate(
                    [_head_norm_rope(ph[:, h * HEAD_DIM:(h + 1) * HEAD_DIM], gain[...], cos, sin, half)
                     for h in range(DA_HEADS_PER_GROUP)], axis=1)
            val = ph.astype(BF16)
            o_ref = out_refs[kind * len(dils) + gi]
            if dil == 1:
                o_ref[...] = val
            else:
                pv = jnp.dot(perm_refs[pi][...], val, preferred_element_type=F32).astype(BF16)
                pi += 1
                rows = tm // dil
                for r in range(dil):
                    o_ref[:, r * gw:(r + 1) * gw] = pv[r * rows:(r + 1) * rows, :]


def _da_qkv(x, g, w_qkv, q_gain, k_gain, cos, sin, seq_len, tm=256):
    n, d = x.shape
    tm = min(tm, seq_len)
    pos_blocks = seq_len // tm
    dils = tuple(dil for _, dil in DA_GROUPS)
    gw = DA_HEADS_PER_GROUP * HEAD_DIM
    f = w_qkv.shape[1]
    perms = []
    for dil in dils:
        if dil > 1:
            perms.append(_dilation_perm(tm, dil))
    row = pl.BlockSpec((tm, d), lambda i: (i, 0))
    tab = pl.BlockSpec((tm, HEAD_DIM), lambda i: (i % pos_blocks, 0))
    out_shapes = tuple(jax.ShapeDtypeStruct((n // dil, dil * gw), BF16) for _ in range(3) for dil in dils)
    out_specs = tuple(pl.BlockSpec((tm // dil, dil * gw), lambda i: (i, 0)) for _ in range(3) for dil in dils)
    outs = pl.pallas_call(
        functools.partial(_da_qkv_kernel, dils=dils, tm=tm, half=ROPE_DIMS // 2),
        out_shape=out_shapes,
        grid=(n // tm,),
        in_specs=[row, _const_spec((1, d)), _const_spec((d, f), single=True), _const_spec((1, HEAD_DIM)),
                  _const_spec((1, HEAD_DIM)), tab, tab] + [_const_spec((tm, tm))] * len(perms),
        out_specs=out_specs,
        compiler_params=_params("parallel"),
    )(x, g.reshape(1, d), w_qkv.astype(BF16), q_gain.reshape(1, HEAD_DIM), k_gain.reshape(1, HEAD_DIM),
      cos, sin, *perms)
    ng = len(dils)
    return [(outs[gi], outs[ng + gi], outs[2 * ng + gi]) for gi in range(ng)]


def _band_attention(q, k, v, batch, seq_len, dil):
    n = batch * seq_len
    s_len = seq_len // dil
    nb = s_len // BAND_BLOCK
    subs = min(BAND_SUBBLOCKS, nb)
    hw = DA_HEADS_PER_GROUP * HEAD_DIM
    view = lambda a: a.reshape(batch, s_len, dil * hw)
    qv, kv, vv = view(q), view(k), view(v)

    main = pl.BlockSpec((None, subs * BAND_BLOCK, hw), lambda b, r, i: (b, i, r))
    prev = pl.BlockSpec((None, BAND_BLOCK, hw), lambda b, r, i: (b, jnp.maximum(i * subs - 1, 0), r))
    nxt = pl.BlockSpec((None, BAND_BLOCK, hw), lambda b, r, i: (b, jnp.minimum((i + 1) * subs, nb - 1), r))
    o, lse = pl.pallas_call(
        functools.partial(_band_kernel, s_len=s_len, heads=DA_HEADS_PER_GROUP, subs=subs),
        out_shape=(jax.ShapeDtypeStruct((batch, s_len, dil * hw), F32),) * 2,
        grid=(batch, dil, nb // subs),
        in_specs=[main, prev, main, nxt, prev, main, nxt],
        out_specs=(main, main),
        compiler_params=_params("parallel", "parallel", "parallel"),
    )(qv, kv, kv, kv, vv, vv, vv)
    return o.reshape(n // dil, dil * hw), lse.reshape(n // dil, dil * hw)


def _undilate(blk, perm_t):
    hw = DA_HEADS_PER_GROUP * HEAD_DIM
    dil = blk.shape[1] // hw
    stacked = jnp.concatenate([blk[:, r * hw:(r + 1) * hw] for r in range(dil)], axis=0)
    hi, lo = _split_bf16(stacked)
    return jnp.dot(perm_t, hi, preferred_element_type=F32) + jnp.dot(perm_t, lo, preferred_element_type=F32)


def _da_out_prologue(o0, o1, o2, l0, l1, l2, pt1, pt2):
    o1, l1 = _undilate(o1, pt1), _undilate(l1, pt1)
    o2, l2 = _undilate(o2, pt2), _undilate(l2, pt2)
    m = jnp.maximum(jnp.maximum(l0, l1), l2)
    e0, e1, e2 = jnp.exp(l0 - m), jnp.exp(l1 - m), jnp.exp(l2 - m)
    return (e0 * o0 + e1 * o1 + e2 * o2) / (e0 + e1 + e2)


def _dilation_perm(tm, dil):
    dst = jnp.arange(tm)
    src = (dst % (tm // dil)) * dil + dst // (tm // dil)
    return (src[:, None] == jnp.arange(tm)[None, :]).astype(BF16)


def _dilated_layer(x, batch, seq_len, g, p, route):
    cos, sin = _rope_tables(seq_len)
    qkv = _da_qkv(x, g, p['w_qkv'], p['q_gain'], p['k_gain'], cos, sin, seq_len)
    outs, lses = [], []
    for (q, k, v), (_, dil) in zip(qkv, DA_GROUPS):
        o, lse = _band_attention(q, k, v, batch, seq_len, dil)
        outs.append(o)
        lses.append(lse)
    tm = min(DA_OUT_ROWS, seq_len)
    perms_t = [_dilation_perm(tm, dil).T for _, dil in DA_GROUPS[1:]]
    return _mm_res(_da_out_prologue, outs + lses, perms_t, p['w_o'], None, x, route, tm=tm)


def _expert_ffn_kernel(xe_ref, gate_ref, w1_ref, w3_ref, w2_ref, ye_ref, w1b, w3b, w2b):
    @pl.when(pl.program_id(1) == 0)
    def _():
        w1b[...] = w1_ref[...].astype(BF16)
        w3b[...] = w3_ref[...].astype(BF16)
        w2b[...] = w2_ref[...].astype(BF16)

    xe = xe_ref[...]
    h1 = jnp.dot(xe, w1b[...], preferred_element_type=F32)
    h3 = jnp.dot(xe, w3b[...], preferred_element_type=F32)
    hid = (h1 * jax.nn.sigmoid(h1) * h3).astype(BF16)
    ye_ref[...] = (jnp.dot(hid, w2b[...], preferred_element_type=F32) * gate_ref[...]).astype(ye_ref.dtype)


def _expert_ffn(xe, gates, w1, w3, w2, tm=512):
    e, c, d = xe.shape
    f = w1.shape[2]
    tm = min(tm, c)
    return pl.pallas_call(
        _expert_ffn_kernel,
        out_shape=jax.ShapeDtypeStruct((e, c, d), BF16),
        grid=(e, c // tm),
        in_specs=[pl.BlockSpec((None, tm, d), lambda ei, ci: (ei, ci, 0)),
                  pl.BlockSpec((None, tm, 1), lambda ei, ci: (ei, ci, 0)),
                  pl.BlockSpec((None, d, f), lambda ei, ci: (ei, 0, 0)),
                  pl.BlockSpec((None, d, f), lambda ei, ci: (ei, 0, 0)),
                  pl.BlockSpec((None, f, d), lambda ei, ci: (ei, 0, 0))],
        out_specs=pl.BlockSpec((None, tm, d), lambda ei, ci: (ei, ci, 0)),
        scratch_shapes=[pltpu.VMEM((d, f), BF16), pltpu.VMEM((d, f), BF16), pltpu.VMEM((f, d), BF16)],
        compiler_params=_params("parallel", "arbitrary"),
    )(xe, gates[..., None], w1, w3, w2)


def _moe_layer(x, xn, aff_t, group_sizes, w1, w3, w2, split_output):
    n, d = x.shape
    gts, idxs = [], []
    start = 0
    for ng in group_sizes:
        cap = EC_CAPACITY * ng // N_EXPERTS
        gates, idx = lax.top_k(aff_t[:, start:start + ng], cap)
        gts.append(gates)
        idxs.append(idx + start)
        start += ng
    idx_all = jnp.concatenate(idxs, axis=1)
    ye = _expert_ffn(xn[idx_all], jnp.concatenate(gts, axis=1), w1, w3, w2)
    return _combine(x, ye.reshape(-1, d), idx_all.reshape(-1), group_sizes if split_output else (n,))


COMBINE_TOKENS = 512
COMBINE_ROWS = 512


def _combine_kernel(tile_ref, blk_ref, live_ref, x_ref, tok_ref, ye_ref, *o_refs, split_tiles):
    w = pl.program_id(0)
    tile = tile_ref[w]
    first = jnp.logical_or(w == 0, tile != tile_ref[jnp.maximum(w - 1, 0)])
    tt = x_ref.shape[0]
    rows = lax.broadcasted_iota(jnp.int32, (tt, COMBINE_ROWS), 0)
    onehot = jnp.where(rows == tok_ref[...] - tile * tt, 1.0, 0.0).astype(BF16)
    add = jnp.dot(onehot, ye_ref[...], preferred_element_type=F32) * live_ref[w].astype(F32)
    lo_tile = 0
    for o_ref, n_tiles in zip(o_refs, split_tiles):
        mine = jnp.logical_and(tile >= lo_tile, tile < lo_tile + n_tiles)

        @pl.when(jnp.logical_and(mine, first))
        def _(o_ref=o_ref):
            o_ref[...] = x_ref[...] + add

        @pl.when(jnp.logical_and(mine, jnp.logical_not(first)))
        def _(o_ref=o_ref):
            o_ref[...] += add

        lo_tile += n_tiles


def _combine(x, ye, tok, splits):
    n, d = x.shape
    p = tok.shape[0]
    tt = min(COMBINE_TOKENS, min(splits))
    rb = COMBINE_ROWS
    assert p % rb == 0 and all(s % tt == 0 for s in splits) and sum(splits) == n
    tiles, nblk = n // tt, p // rb
    split_tiles = tuple(s // tt for s in splits)
    order = jnp.argsort(tok)
    tok_sorted = tok[order].astype(jnp.int32)
    ye_sorted = ye[order]

    edges = jnp.arange(tiles + 1, dtype=jnp.int32) * tt
    bounds = jnp.searchsorted(tok_sorted, edges, method='compare_all').astype(jnp.int32)
    lo, hi = bounds[:-1], bounds[1:]
    first_blk = jnp.minimum(lo // rb, nblk - 1)
    last_blk = jnp.where(hi > lo, (hi - 1) // rb, first_blk)
    n_items = last_blk - first_blk + 1
    item_end = jnp.cumsum(n_items)
    item_start = item_end - n_items
    max_items = nblk + 2 * tiles
    w = jnp.arange(max_items, dtype=jnp.int32)
    tile = jnp.minimum(jnp.searchsorted(item_end, w, side='right', method='compare_all'),
                       tiles - 1).astype(jnp.int32)
    k = w - item_start[tile]
    live = (k < n_items[tile]).astype(jnp.int32)
    blk = jnp.minimum(first_blk[tile] + k, nblk - 1).astype(jnp.int32)

    def out_spec(lo_tile, n_tiles):
        return pl.BlockSpec(
            (tt, d), lambda i, tile_r, blk_r, live_r: (jnp.clip(tile_r[i] - lo_tile, 0, n_tiles - 1), 0))

    starts = [sum(split_tiles[:j]) for j in range(len(splits))]
    grid_spec = pltpu.PrefetchScalarGridSpec(
        num_scalar_prefetch=3,
        grid=(max_items,),
        in_specs=[pl.BlockSpec((tt, d), lambda i, tile_r, blk_r, live_r: (tile_r[i], 0)),
                  pl.BlockSpec((None, 1, rb), lambda i, tile_r, blk_r, live_r: (blk_r[i], 0, 0)),
                  pl.BlockSpec((rb, d), lambda i, tile_r, blk_r, live_r: (blk_r[i], 0))],
        out_specs=tuple(out_spec(s, t) for s, t in zip(starts, split_tiles)),
    )
    return pl.pallas_call(
        functools.partial(_combine_kernel, split_tiles=split_tiles),
        out_shape=tuple(jax.ShapeDtypeStruct((s, d), F32) for s in splits),
        grid_spec=grid_spec,
        compiler_params=_params("arbitrary"),
    )(tile, blk, live, x, tok_sorted.reshape(nblk, 1, rb), ye_sorted)


def _trunk(x, batch, seq_len, group_sizes, p):
    depth = p['norm_gain'].shape[0]
    layers = ((_hyena_layer, 'hy_'), (_gqa_layer, 'ga_'), (_mlstm_layer, 'ml_'), (_dilated_layer, 'da_'))
    for i in range(depth):
        layer, prefix = layers[i % 4]
        lp = {k[len(prefix):]: v[i // 4] for k, v in p.items() if k.startswith(prefix)}
        route = (p['norm_gain'][i, 1], p['moe_w_router'][i])
        x, xn, aff_t = layer(x, batch, seq_len, p['norm_gain'][i, 0], lp, route)
        last = i == depth - 1
        outs = _moe_layer(x, xn, aff_t, group_sizes, p['moe_w1'][i], p['moe_w3'][i], p['moe_w2'][i],
                          split_output=last)
        x = outs if last else outs[0]
    return x


def kernel(x_prompt, x_sample, norm_gain, hy_w_in, hy_b_in, hy_conv_w, hy_conv_b, hy_f_w1, hy_f_b1, hy_f_w2, hy_f_b2, hy_f_w3, hy_f_b3, hy_f_freq, hy_decay, hy_skip, hy_w_out, hy_b_out, ga_w_qkv, ga_q_gain, ga_k_gain, ga_w_o, ml_w_up, ml_conv_w, ml_conv_b, ml_w_q, ml_w_k, ml_w_v, ml_w_gate, ml_b_gate, ml_norm_gain, ml_skip, ml_w_down, da_w_qkv, da_q_gain, da_k_gain, da_w_o, moe_w_router, moe_w1, moe_w3, moe_w2):
    p = dict(
        norm_gain=norm_gain,
        hy_w_in=hy_w_in, hy_b_in=hy_b_in, hy_conv_w=hy_conv_w, hy_conv_b=hy_conv_b,
        hy_f_w1=hy_f_w1, hy_f_b1=hy_f_b1, hy_f_w2=hy_f_w2, hy_f_b2=hy_f_b2,
        hy_f_w3=hy_f_w3, hy_f_b3=hy_f_b3, hy_f_freq=hy_f_freq, hy_decay=hy_decay,
        hy_skip=hy_skip, hy_w_out=hy_w_out, hy_b_out=hy_b_out,
        ga_w_qkv=ga_w_qkv, ga_q_gain=ga_q_gain, ga_k_gain=ga_k_gain, ga_w_o=ga_w_o,
        ml_w_up=ml_w_up, ml_conv_w=ml_conv_w, ml_conv_b=ml_conv_b, ml_w_q=ml_w_q,
        ml_w_k=ml_w_k, ml_w_v=ml_w_v, ml_w_gate=ml_w_gate, ml_b_gate=ml_b_gate,
        ml_norm_gain=ml_norm_gain, ml_skip=ml_skip, ml_w_down=ml_w_down,
        da_w_qkv=da_w_qkv, da_q_gain=da_q_gain, da_k_gain=da_k_gain, da_w_o=da_w_o,
        moe_w_router=moe_w_router, moe_w1=moe_w1, moe_w3=moe_w3, moe_w2=moe_w2,
    )
    bp, seq_len, d = x_prompt.shape
    bs = x_sample.shape[0]
    assert x_sample.shape[1] == seq_len
    x = jnp.concatenate([x_prompt, x_sample], axis=0).reshape((bp + bs) * seq_len, d)
    y_prompt, y_sample = _trunk(x, bp + bs, seq_len, (bp * seq_len, bs * seq_len), p)
    return (y_prompt.reshape(bp, seq_len, d), y_sample.reshape(bs, seq_len, d))
```

```python
import functools
import math

import jax
import jax.numpy as jnp
from jax import lax
from jax.experimental import pallas as pl
from jax.experimental.pallas import tpu as pltpu

F32 = jnp.float32
BF16 = jnp.bfloat16
HIGHEST = lax.Precision.HIGHEST

NORM_EPS = 1e-6
GRID_W = 64
HY_BANDS = 16
GA_HEADS = 8
GA_KV_HEADS = 2
GA_GROUP = GA_HEADS // GA_KV_HEADS
HEAD_DIM = 128
AXIAL_THETA = 10000.0
ML_HEADS = 4
ML_QKV_BLOCK = 4
DA_GROUPS = ((128, 1), (512, 4), (2048, 16))
DA_HEADS_PER_GROUP = 4
DA_HEADS = DA_HEADS_PER_GROUP * len(DA_GROUPS)
ROPE_THETA = 500000.0
ROPE_DIMS = HEAD_DIM // 4
N_EXPERTS = 16
EC_CAPACITY = 2

VMEM_LIMIT_BYTES = 52 * 1024 * 1024
HALO_ROWS = 8
MXU_DIM = 256
ML_CHUNK = 256
BAND_BLOCK = 128
BAND_HALF = 64
BAND_SUBBLOCKS = 4
DA_OUT_ROWS = 256


def _params(*sem):
    return pltpu.CompilerParams(dimension_semantics=sem, vmem_limit_bytes=VMEM_LIMIT_BYTES)


def _rms(x, g):
    ms = jnp.mean(x * x, axis=-1, keepdims=True)
    return x * lax.rsqrt(ms + NORM_EPS) * g


def _const_spec(shape, single=False):
    nd = len(shape)
    if single:
        return pl.BlockSpec(shape, lambda *_: (0,) * nd, pipeline_mode=pl.Buffered(1))
    return pl.BlockSpec(shape, lambda *_: (0,) * nd)


def _conv3_rows(p, pprev, pnext, cw, cb, rows, tm):
    up = jnp.where(rows == 0, pprev, pltpu.roll(p, 1, 0))
    dn = jnp.where(rows == tm - 1, pnext, pltpu.roll(p, tm - 1, 0))
    return up * cw[0:1] + p * cw[1:2] + dn * cw[2:3] + cb


def _halo_specs(tm, d, n_rows):
    hb = tm // HALO_ROWS
    last = n_rows // HALO_ROWS - 1
    prev = pl.BlockSpec((HALO_ROWS, d), lambda i: (jnp.maximum(i * hb - 1, 0), 0))
    nxt = pl.BlockSpec((HALO_ROWS, d), lambda i: (jnp.minimum((i + 1) * hb, last), 0))
    return prev, nxt


def _edge_scales(i, tm, seq_len):
    t0 = i * tm
    keep_prev = jnp.where(t0 % seq_len == 0, 0.0, 1.0).astype(F32)
    keep_next = jnp.where((t0 + tm) % seq_len == 0, 0.0, 1.0).astype(F32)
    return keep_prev, keep_next


def _route(x, g_ref, wr_ref, xn_ref, aff_ref):
    xn = _rms(x, g_ref[...])
    xn_ref[...] = xn.astype(BF16)
    logits = lax.dot_general(wr_ref[...], xn, (((1,), (1,)), ((), ())),
                             precision=HIGHEST, preferred_element_type=F32)
    m = jnp.max(logits, axis=0, keepdims=True)
    e = jnp.exp(logits - m)
    aff_ref[...] = e / jnp.sum(e, axis=0, keepdims=True)


def _mm_res_kernel(*refs, prologue, n_row, n_const, tn):
    row_refs = refs[:n_row]
    const_refs = refs[n_row:n_row + n_const]
    w_ref, b_ref, res_ref, g2_ref, wr_ref, o_ref, xn_ref, aff_ref = refs[n_row + n_const:]
    lhs = prologue(*[r[...] for r in row_refs], *[c[...] for c in const_refs]).astype(BF16)
    for j in range(o_ref.shape[1] // tn):
        sl = slice(j * tn, (j + 1) * tn)
        o_ref[:, sl] = (res_ref[:, sl] + b_ref[:, sl]
                        + jnp.dot(lhs, w_ref[:, sl], preferred_element_type=F32))
    _route(o_ref[...], g2_ref, wr_ref, xn_ref, aff_ref)


def _mm_res(prologue, rows, consts, w, b, res, route, tm=512, tn=512):
    n, dout = res.shape
    tm = min(tm, n)
    k = w.shape[0]
    g2, w_router = route
    ne = w_router.shape[1]
    if b is None:
        b = jnp.zeros((1, dout), F32)
    in_specs = [pl.BlockSpec((tm * r.shape[0] // n, r.shape[1]), lambda i: (i, 0)) if r.ndim == 2
                else pl.BlockSpec((r.shape[0], tm, r.shape[2]), lambda i: (0, i, 0)) for r in rows]
    in_specs += [_const_spec(c.shape) for c in consts]
    in_specs += [_const_spec((k, dout)), _const_spec((1, dout)),
                 pl.BlockSpec((tm, dout), lambda i: (i, 0)),
                 _const_spec((1, dout)), _const_spec((ne, dout))]
    tile = pl.BlockSpec((tm, dout), lambda i: (i, 0))
    return pl.pallas_call(
        functools.partial(_mm_res_kernel, prologue=prologue, n_row=len(rows),
                          n_const=len(consts), tn=min(tn, dout)),
        out_shape=(jax.ShapeDtypeStruct((n, dout), F32), jax.ShapeDtypeStruct((n, dout), BF16),
                   jax.ShapeDtypeStruct((ne, n), F32)),
        grid=(n // tm,),
        in_specs=in_specs,
        out_specs=(tile, tile, pl.BlockSpec((ne, tm), lambda i: (0, i))),
        compiler_params=_params("parallel"),
    )(*rows, *consts, w.astype(BF16), b.reshape(1, dout).astype(F32), res,
      g2.reshape(1, dout), w_router.T)


def _hyena_in_kernel(x_ref, xp_ref, xn_ref, g_ref, w_ref, b_ref, cw_ref, cb_ref,
                     x0_ref, vx_ref, *, seq_len, tm, d, cols):
    keep_prev, keep_next = _edge_scales(pl.program_id(0), tm, seq_len)
    g = g_ref[...]
    xb = _rms(jnp.concatenate([x_ref[...], xp_ref[...], xn_ref[...]], axis=0), g).astype(BF16)
    rows = lax.broadcasted_iota(jnp.int32, (tm, 1), 0)

    def conv_part(c0):
        sl = slice(c0, c0 + cols)
        pa = jnp.dot(xb, w_ref[:, sl], preferred_element_type=F32) + b_ref[:, sl]
        p = pa[:tm]
        pprev = pa[tm + HALO_ROWS - 1:tm + HALO_ROWS, :] * keep_prev
        pnext = pa[tm + HALO_ROWS:tm + HALO_ROWS + 1, :] * keep_next
        return _conv3_rows(p, pprev, pnext, cw_ref[:, sl], cb_ref[:, sl], rows, tm)

    for j in range(d // cols):
        c = j * cols
        x0_ref[:, c:c + cols] = conv_part(c)
        vx_ref[:, c:c + cols] = conv_part(2 * d + c) * conv_part(d + c)


def _hyena_in(x, g, w_in, b_in, conv_w, conv_b, seq_len, tm=512, cols=512):
    n, d = x.shape
    tm = min(tm, seq_len)
    prev, nxt = _halo_specs(tm, d, n)
    row = pl.BlockSpec((tm, d), lambda i: (i, 0))
    return pl.pallas_call(
        functools.partial(_hyena_in_kernel, seq_len=seq_len, tm=tm, d=d, cols=cols),
        out_shape=(jax.ShapeDtypeStruct((n, d), F32), jax.ShapeDtypeStruct((n, d), F32)),
        grid=(n // tm,),
        in_specs=[row, prev, nxt, _const_spec((1, d)), _const_spec((d, 3 * d)),
                  _const_spec((1, 3 * d)), _const_spec((3, 3 * d)), _const_spec((1, 3 * d))],
        out_specs=(row, row),
        compiler_params=_params("parallel"),
    )(x, x, x, g.reshape(1, d), w_in.astype(BF16), b_in.reshape(1, 3 * d),
      conv_w, conv_b.reshape(1, 3 * d))


def _hyena_filter_taps(L, d, f_w1, f_b1, f_w2, f_b2, f_w3, f_b3, f_freq, decay):
    t = jnp.linspace(0.0, 1.0, L, dtype=F32)[:, None]
    w_ang = 2.0 * math.pi * jnp.arange(L, dtype=F32)[:, None] / L
    bands = jnp.linspace(1e-4, HY_BANDS - 1, HY_BANDS, dtype=F32)[None, :]
    z = jnp.concatenate([t, jnp.cos(bands * w_ang), -jnp.sin(bands * w_ang)], axis=-1)

    def branch(zz, tt, col0):
        h = jnp.sin(f_freq[0] * (zz @ f_w1 + f_b1))
        h = jnp.sin(f_freq[1] * (h @ f_w2 + f_b2))
        h = h @ f_w3[:, col0:col0 + d] + f_b3[col0:col0 + d]
        return h * jnp.exp(-tt * jnp.abs(decay[col0 // d])[None])

    h_fwd = branch(z, t, 0)
    h_bwd_rev = branch(z[::-1], t[::-1], d)
    k = jnp.concatenate([h_fwd, jnp.zeros((1, d), F32), h_bwd_rev[:-1]], axis=0)
    return k / jnp.sum(jnp.abs(k), axis=0, keepdims=True)


def _fft_dims(m):
    lg = m.bit_length() - 1
    p = 1 << ((lg + 1) // 2)
    return p, m // p


def _split_bf16(x):
    hi = x.astype(BF16)
    return hi, (x - hi.astype(F32)).astype(BF16)


def _mm_split(fh, fl, x, precise):
    if not precise:
        return jnp.dot(fh, x.astype(BF16), preferred_element_type=F32)
    xh, xl = _split_bf16(x)
    return (jnp.dot(fh, xh, preferred_element_type=F32) + jnp.dot(fl, xh, preferred_element_type=F32)
            + jnp.dot(fh, xl, preferred_element_type=F32))


def _cplx_as_real(cr, ci):
    top = jnp.concatenate([cr, -ci], axis=-1)
    bot = jnp.concatenate([ci, cr], axis=-1)
    return jnp.concatenate([top, bot], axis=-2)


def _unit_circle(idx, m):
    ang = (2.0 * math.pi / m) * idx.astype(F32)
    return jnp.cos(ang), jnp.sin(ang)


def _dft_consts(p, q):
    m = p * q
    k1 = jnp.arange(p, dtype=jnp.int32)
    n1 = jnp.arange(p // 2, dtype=jnp.int32)
    c, s = _unit_circle((k1[:, None] * n1[None, :]) % p, p)
    fa = _cplx_as_real(c, -s)
    c, s = _unit_circle((n1[:, None] * k1[None, :]) % p, p)
    fd = _cplx_as_real(c / m, s / m)
    k2 = jnp.arange(q, dtype=jnp.int32)
    n2 = jnp.arange(q, dtype=jnp.int32)
    idx = (n2[None, None, :] * (k2[None, :, None] * p + k1[:, None, None])) % m
    c, s = _unit_circle(idx, m)
    gb = _cplx_as_real(c, -s)
    ct, st = jnp.swapaxes(c, 1, 2), jnp.swapaxes(s, 1, 2)
    gc = _cplx_as_real(ct, st)
    return tuple(_split_bf16(a) for a in (fa, gb, gc, fd))


def _fft_a_kernel(x_ref, fh_ref, fl_ref, o_ref, *, precise):
    _, rows_in, group, d = x_ref.shape
    rows_out = o_ref.shape[1]
    for j in range(group):
        x = x_ref[:, :, j, :].reshape(2 * rows_in, d)
        y = _mm_split(fh_ref[...], fl_ref[...], x, precise)
        o_ref[:, :, j, :] = y.reshape(2, rows_out, d)


def _fft_b_kernel(a_ref, gh_ref, gl_ref, o_ref):
    _, q, d = a_ref.shape
    x = _mm_split(gh_ref[...], gl_ref[...], a_ref[...].reshape(2 * q, d), True)
    o_ref[...] = x.reshape(o_ref.shape)


def _fft_bc_kernel(a_ref, gb_ref, gc_ref, k_ref, z_ref):
    _, q, d = a_ref.shape
    x = _mm_split(gb_ref[...], None, a_ref[...].reshape(2 * q, d), False)
    xr, xi = x[:q], x[q:]
    kr, ki = k_ref[0], k_ref[1]
    y = jnp.concatenate([xr * kr - xi * ki, xr * ki + xi * kr], axis=0)
    z = _mm_split(gc_ref[...], None, y, False)
    z_ref[...] = z.reshape(z_ref.shape)


def _fft_rows(x5, f, rows_out, precise):
    pairs, _, rows_in, q, d = x5.shape
    blk = lambda r: pl.BlockSpec((None, 2, r, HALO_ROWS, d), lambda b, j: (b, 0, 0, j, 0))
    return pl.pallas_call(
        functools.partial(_fft_a_kernel, precise=precise),
        out_shape=jax.ShapeDtypeStruct((pairs, 2, rows_out, q, d), F32),
        grid=(pairs, q // HALO_ROWS),
        in_specs=[blk(rows_in), _const_spec(f[0].shape), _const_spec(f[1].shape)],
        out_specs=blk(rows_out),
        compiler_params=_params("parallel", "parallel"),
    )(x5, *f)


def _long_conv(vx, taps, batch, seq_len):
    n, d = vx.shape
    m = 2 * seq_len
    p, q = _fft_dims(m)
    assert batch % 2 == 0
    pairs = batch // 2
    fa, gb, gc, fd = _dft_consts(p, q)
    g_spec = pl.BlockSpec((None, 2 * q, 2 * q), lambda k1, b: (k1, 0, 0))
    slab = pl.BlockSpec((None, 2, q, d), lambda k1, b: (b, 0, k1, 0))

    zeros = jnp.zeros((seq_len, d), F32)
    kin = jnp.stack([taps[:seq_len], zeros, taps[seq_len:], zeros]).reshape(2, 2, p // 2, q, d)
    ka = _fft_rows(kin, fa, p, True).reshape(2, 2, p * q, d)
    kx = pl.pallas_call(
        _fft_b_kernel,
        out_shape=jax.ShapeDtypeStruct((2, 2, p * q, d), F32),
        grid=(p, 2),
        in_specs=[slab, g_spec, g_spec],
        out_specs=slab,
        compiler_params=_params("parallel", "parallel"),
    )(ka, *gb)
    sign = jnp.repeat(1.0 - 2.0 * (jnp.arange(p) % 2).astype(F32), q)[None, :, None]
    kspec = kx[0] + sign * kx[1]

    xa = _fft_rows(vx.reshape(pairs, 2, p // 2, q, d), fa, p, False).reshape(pairs, 2, p * q, d)
    z = pl.pallas_call(
        _fft_bc_kernel,
        out_shape=jax.ShapeDtypeStruct((pairs, 2, p * q, d), F32),
        grid=(p, pairs),
        in_specs=[slab, g_spec, g_spec, pl.BlockSpec((2, q, d), lambda k1, b: (0, k1, 0))],
        out_specs=slab,
        compiler_params=_params("parallel", "parallel"),
    )(xa, gb[0], gc[0], kspec)
    y = _fft_rows(z.reshape(pairs, 2, p, q, d), fd, p // 2, False)
    return y.reshape(n, d)


def _hyena_out_prologue(y, vx, x0, skip):
    return (y + vx * skip) * x0


def _hyena_layer(x, batch, seq_len, g, p, route):
    n, d = x.shape
    x0, vx = _hyena_in(x, g, p['w_in'], p['b_in'], p['conv_w'], p['conv_b'], seq_len)
    taps = _hyena_filter_taps(seq_len, d, p['f_w1'], p['f_b1'], p['f_w2'], p['f_b2'],
                              p['f_w3'], p['f_b3'], p['f_freq'], p['decay'])
    y = _long_conv(vx, taps, batch, seq_len)
    return _mm_res(_hyena_out_prologue, [y, vx, x0], [p['skip'].reshape(1, d)],
                   p['w_out'], p['b_out'], x, route)


def _head_norm_rope(xh, gain, cos, sin, half):
    y = _rms(xh, gain)
    lane = lax.broadcasted_iota(jnp.int32, (1, HEAD_DIM), 1)
    fwd = pltpu.roll(y, HEAD_DIM - half, 1)
    bwd = pltpu.roll(y, half, 1)
    partner = jnp.where((lane % (2 * half)) < half, fwd, bwd)
    return y * cos + partner * sin


def _qkv_rope_kernel(x_ref, g_ref, w_ref, qg_ref, kg_ref, cos_ref, sin_ref,
                     q_ref, k_ref, v_ref, *, nq, nk, nv, half):
    xb = _rms(x_ref[...], g_ref[...]).astype(BF16)
    cos = cos_ref[...]
    sin = sin_ref[...]
    per = MXU_DIM // HEAD_DIM
    for h0 in range(0, nq + nk + nv, per):
        pw = jnp.dot(xb, w_ref[:, h0 * HEAD_DIM:(h0 + per) * HEAD_DIM], preferred_element_type=F32)
        for h in range(h0, h0 + per):
            ph = pw[:, (h - h0) * HEAD_DIM:(h - h0 + 1) * HEAD_DIM]
            if h < nq:
                sl = slice(h * HEAD_DIM, (h + 1) * HEAD_DIM)
                q_ref[:, sl] = _head_norm_rope(ph, qg_ref[...], cos, sin, half).astype(BF16)
            elif h < nq + nk:
                sl = slice((h - nq) * HEAD_DIM, (h - nq + 1) * HEAD_DIM)
                k_ref[:, sl] = _head_norm_rope(ph, kg_ref[...], cos, sin, half).astype(BF16)
            else:
                sl = slice((h - nq - nk) * HEAD_DIM, (h - nq - nk + 1) * HEAD_DIM)
                v_ref[:, sl] = ph.astype(BF16)


def _qkv_rope(x, g, w_qkv, q_gain, k_gain, cos, sin, nq, nk, nv, half, seq_len, tm=512):
    n, d = x.shape
    tm = min(tm, seq_len)
    pos_blocks = seq_len // tm
    f = w_qkv.shape[1]
    row = pl.BlockSpec((tm, d), lambda i: (i, 0))
    tab = pl.BlockSpec((tm, HEAD_DIM), lambda i: (i % pos_blocks, 0))
    outs = tuple(jax.ShapeDtypeStruct((n, c * HEAD_DIM), BF16) for c in (nq, nk, nv))
    return pl.pallas_call(
        functools.partial(_qkv_rope_kernel, nq=nq, nk=nk, nv=nv, half=half),
        out_shape=outs,
        grid=(n // tm,),
        in_specs=[row, _const_spec((1, d)), _const_spec((d, f)),
                  _const_spec((1, HEAD_DIM)), _const_spec((1, HEAD_DIM)), tab, tab],
        out_specs=tuple(pl.BlockSpec((tm, c * HEAD_DIM), lambda i: (i, 0)) for c in (nq, nk, nv)),
        compiler_params=_params("parallel"),
    )(x, g.reshape(1, d), w_qkv.astype(BF16), q_gain.reshape(1, HEAD_DIM),
      k_gain.reshape(1, HEAD_DIM), cos, sin)


def _axial_tables(L):
    t = jnp.arange(L)
    r = (t // GRID_W).astype(F32)
    c = (t % GRID_W).astype(F32)
    nf = HEAD_DIM // 4
    inv = AXIAL_THETA ** (-(2.0 * jnp.arange(nf, dtype=F32)) / (2 * nf))
    ar, ac = r[:, None] * inv[None], c[:, None] * inv[None]
    cos = jnp.concatenate([jnp.cos(ar), jnp.cos(ar), jnp.cos(ac), jnp.cos(ac)], axis=-1)
    sin = jnp.concatenate([-jnp.sin(ar), jnp.sin(ar), -jnp.sin(ac), jnp.sin(ac)], axis=-1)
    return cos, sin


FLASH_SAFE_BOUND = 40.0
FLASH_BOUND_MARGIN = 1.001


def _flash_kernel(q_ref, k_ref, v_ref, o_ref, acc_scr, off_scr, lsum_scr, m_scr, l_scr, kn_scr,
                  *, tq, tk, tkf, seq_len, group):
    scale = HEAD_DIM ** -0.5
    c = scale * math.log2(math.e)

    @pl.when(pl.program_id(2) == 0)
    def _():
        def norm_step(j, mx):
            kt = k_ref[pl.ds(pl.multiple_of(j * tk, tk), tk), :].astype(F32)
            row = jnp.sum(kt * kt, axis=1, keepdims=True)
            return jnp.maximum(mx, jnp.max(row, axis=0, keepdims=True))

        kmax2 = lax.fori_loop(0, seq_len // tk, norm_step, jnp.zeros((1, 1), F32))
        kn_scr[...] = jnp.broadcast_to(kmax2, kn_scr.shape)

    kmax2 = kn_scr[0:1, 0:1]
    bmax = jnp.zeros((1, 1), F32)
    for h in range(group):
        qf = q_ref[:, h * HEAD_DIM:(h + 1) * HEAD_DIM].astype(F32)
        b = jnp.sqrt(jnp.sum(qf * qf, axis=1, keepdims=True) * kmax2) * FLASH_BOUND_MARGIN
        bmax = jnp.maximum(bmax, jnp.max(b, axis=0, keepdims=True))
        off_scr[h] = jnp.broadcast_to(b * c, (tq, HEAD_DIM))
    fast = (bmax * scale)[0, 0] <= FLASH_SAFE_BOUND

    @pl.when(fast)
    def _():
        acc_scr[...] = jnp.zeros_like(acc_scr)
        lsum_scr[...] = jnp.zeros_like(lsum_scr)
        n_lane_tiles = tkf // HEAD_DIM

        def body(j, carry):
            start = pl.multiple_of(j * tkf, tkf)
            kt = k_ref[pl.ds(start, tkf), :]
            vt = v_ref[pl.ds(start, tkf), :]
            for h in range(group):
                q = q_ref[:, h * HEAD_DIM:(h + 1) * HEAD_DIM]
                s = lax.dot_general(q, kt, (((1,), (1,)), ((), ())), preferred_element_type=F32)
                off = off_scr[h]
                p = jnp.exp2(s * c - jnp.concatenate([off] * n_lane_tiles, axis=1))
                part = p[:, 0:HEAD_DIM]
                for t in range(1, n_lane_tiles):
                    part = part + p[:, t * HEAD_DIM:(t + 1) * HEAD_DIM]
                lsum_scr[h] += part
                acc_scr[h] += jnp.dot(p.astype(BF16), vt, preferred_element_type=F32)
            return carry

        lax.fori_loop(0, seq_len // tkf, body, 0)
        for h in range(group):
            l = jnp.sum(lsum_scr[h], axis=1, keepdims=True)
            o_ref[:, h * HEAD_DIM:(h + 1) * HEAD_DIM] = (acc_scr[h] / l).astype(o_ref.dtype)

    @pl.when(jnp.logical_not(fast))
    def _():
        m_scr[...] = jnp.full(m_scr.shape, -jnp.inf, F32)
        l_scr[...] = jnp.zeros_like(l_scr)
        acc_scr[...] = jnp.zeros_like(acc_scr)

        def body(j, carry):
            start = pl.multiple_of(j * tk, tk)
            kt = k_ref[pl.ds(start, tk), :]
            vt = v_ref[pl.ds(start, tk), :]
            for h in range(group):
                q = q_ref[:, h * HEAD_DIM:(h + 1) * HEAD_DIM]
                s = lax.dot_general(q, kt, (((1,), (1,)), ((), ())), preferred_element_type=F32)
                m = m_scr[h]
                m_new = jnp.maximum(m, jnp.max(s, axis=1, keepdims=True))
                alpha = jnp.exp2((m - m_new) * c)
                p = jnp.exp2(s * c - m_new * c)
                l_scr[h] = alpha * l_scr[h] + jnp.sum(p, axis=1, keepdims=True)
                acc_scr[h] = alpha * acc_scr[h] + jnp.dot(p.astype(BF16), vt, preferred_element_type=F32)
                m_scr[h] = m_new
            return carry

        lax.fori_loop(0, seq_len // tk, body, 0)
        for h in range(group):
            o_ref[:, h * HEAD_DIM:(h + 1) * HEAD_DIM] = (acc_scr[h] / l_scr[h]).astype(o_ref.dtype)


def _flash_gqa(q, k, v, batch, seq_len, tq=1024, tk=512, tkf=1024):
    tq = min(tq, seq_len)
    tk = min(tk, seq_len)
    tkf = min(tkf, seq_len)
    nq = seq_len // tq
    gw = GA_GROUP * HEAD_DIM
    wide = pltpu.VMEM((GA_GROUP, tq, HEAD_DIM), F32)
    thin = pltpu.VMEM((GA_GROUP, tq, 1), F32)
    return pl.pallas_call(
        functools.partial(_flash_kernel, tq=tq, tk=tk, tkf=tkf, seq_len=seq_len, group=GA_GROUP),
        out_shape=jax.ShapeDtypeStruct(q.shape, BF16),
        grid=(batch, GA_KV_HEADS, nq),
        in_specs=[pl.BlockSpec((tq, gw), lambda b, kv, i: (b * nq + i, kv)),
                  pl.BlockSpec((seq_len, HEAD_DIM), lambda b, kv, i: (b, kv)),
                  pl.BlockSpec((seq_len, HEAD_DIM), lambda b, kv, i: (b, kv))],
        out_specs=pl.BlockSpec((tq, gw), lambda b, kv, i: (b * nq + i, kv)),
        scratch_shapes=[wide, wide, wide, thin, thin, pltpu.VMEM((HALO_ROWS, HEAD_DIM), F32)],
        compiler_params=_params("parallel", "parallel", "arbitrary"),
    )(q, k, v)


def _identity_prologue(o):
    return o


def _gqa_layer(x, batch, seq_len, g, p, route):
    cos, sin = _axial_tables(seq_len)
    q, k, v = _qkv_rope(x, g, p['w_qkv'], p['q_gain'], p['k_gain'], cos, sin,
                        GA_HEADS, GA_KV_HEADS, GA_KV_HEADS, HEAD_DIM // 4, seq_len)
    o = _flash_gqa(q, k, v, batch, seq_len)
    return _mm_res(_identity_prologue, [o], [], p['w_o'], None, x, route)


def _ml_in_kernel(x_ref, xp_ref, xn_ref, g_ref, w_ref, cw_ref, cb_ref, wq_ref, wk_ref, wv_ref,
                  wg_ref, bg_ref, q_ref, k_ref, v_ref, xc_ref, sz_ref, gate_ref,
                  *, seq_len, tm, inner, k_scale):
    keep_prev, keep_next = _edge_scales(pl.program_id(0), tm, seq_len)
    g = g_ref[...]
    xa = _rms(jnp.concatenate([x_ref[...], xp_ref[...], xn_ref[...]], axis=0), g).astype(BF16)
    xb = xa[:tm]
    rows = lax.broadcasted_iota(jnp.int32, (tm, 1), 0)
    gacc = jnp.zeros(gate_ref.shape, F32)
    for t in range(inner // MXU_DIM):
        sl = slice(t * MXU_DIM, (t + 1) * MXU_DIM)
        xma = jnp.dot(xa, w_ref[:, sl], preferred_element_type=F32)
        xm = xma[:tm]
        pprev = xma[tm + HALO_ROWS - 1:tm + HALO_ROWS, :] * keep_prev
        pnext = xma[tm + HALO_ROWS:tm + HALO_ROWS + 1, :] * keep_next
        xc = _conv3_rows(xm, pprev, pnext, cw_ref[:, sl], cb_ref[:, sl], rows, tm)
        xc = xc * jax.nn.sigmoid(xc)
        z = jnp.dot(xb, w_ref[:, inner + t * MXU_DIM:inner + (t + 1) * MXU_DIM],
                    preferred_element_type=F32)
        xcb = xc.astype(BF16)
        q = jnp.dot(xcb, wq_ref[t], preferred_element_type=F32)
        k = jnp.dot(xcb, wk_ref[t], preferred_element_type=F32)
        v = jnp.dot(xm.astype(BF16), wv_ref[t], preferred_element_type=F32)
        qb, kb, vb = q.astype(BF16), k.astype(BF16), v.astype(BF16)
        gacc += (jnp.dot(qb, wg_ref[0, sl, :], preferred_element_type=F32)
                 + jnp.dot(kb, wg_ref[1, sl, :], preferred_element_type=F32)
                 + jnp.dot(vb, wg_ref[2, sl, :], preferred_element_type=F32))
        q_ref[:, sl] = qb
        k_ref[:, sl] = (k * k_scale).astype(BF16)
        v_ref[:, sl] = vb
        xc_ref[:, sl] = xc
        sz_ref[:, sl] = z * jax.nn.sigmoid(z)
    gate_ref[...] = gacc + bg_ref[...]


def _block_diag_tiles(w):
    nb, c, _ = w.shape
    per = MXU_DIM // c
    wt = w.reshape(nb // per, per, c, c)
    eye = jnp.eye(per, dtype=w.dtype)
    full = jnp.einsum('tpcd,pq->tpcqd', wt, eye)
    return full.reshape(nb // per, MXU_DIM, MXU_DIM)


def _ml_in(x, g, p, seq_len, tm=512):
    n, d = x.shape
    tm = min(tm, seq_len)
    inner = p['w_up'].shape[1] // 2
    ng = 4 * ML_HEADS
    dh = inner // ML_HEADS
    wq, wk, wv = (_block_diag_tiles(p[nm]).astype(BF16) for nm in ('w_q', 'w_k', 'w_v'))
    wg = jnp.transpose(p['w_gate'], (1, 2, 0, 3)).reshape(3, inner, ng).astype(BF16)
    bg = p['b_gate'].reshape(1, ng)
    prev, nxt = _halo_specs(tm, d, n)
    row = pl.BlockSpec((tm, d), lambda i: (i, 0))
    wide = pl.BlockSpec((tm, inner), lambda i: (i, 0))
    nt = inner // MXU_DIM
    return pl.pallas_call(
        functools.partial(_ml_in_kernel, seq_len=seq_len, tm=tm, inner=inner, k_scale=dh ** -0.5),
        out_shape=(jax.ShapeDtypeStruct((n, inner), BF16),) * 3
        + (jax.ShapeDtypeStruct((n, inner), F32),) * 2
        + (jax.ShapeDtypeStruct((n, ng), F32),),
        grid=(n // tm,),
        in_specs=[row, prev, nxt, _const_spec((1, d)), _const_spec((d, 2 * inner), single=True),
                  _const_spec((3, inner)), _const_spec((1, inner)),
                  _const_spec((nt, MXU_DIM, MXU_DIM), single=True),
                  _const_spec((nt, MXU_DIM, MXU_DIM), single=True),
                  _const_spec((nt, MXU_DIM, MXU_DIM), single=True),
                  _const_spec((3, inner, ng), single=True), _const_spec((1, ng))],
        out_specs=(wide,) * 5 + (pl.BlockSpec((tm, ng), lambda i: (i, 0)),),
        compiler_params=_params("parallel"),
    )(x, x, x, g.reshape(1, d), p['w_up'].astype(BF16), p['conv_w'],
      p['conv_b'].reshape(1, inner), wq, wk, wv, wg, bg)


def _log_sigmoid(x):
    return jnp.minimum(x, 0.0) - jnp.log1p(jnp.exp(-jnp.abs(x)))


def _mlstm_chunk_kernel(qf_ref, kf_ref, vf_ref, gcf_ref, grf_ref, qb_ref, kb_ref, vb_ref, gcb_ref, grb_ref,
                        hf_ref, hb_ref, c_scr, n_scr, m_scr, *, lc):
    @pl.when(pl.program_id(2) == 0)
    def _():
        c_scr[...] = jnp.zeros_like(c_scr)
        n_scr[...] = jnp.zeros_like(n_scr)
        m_scr[...] = jnp.zeros_like(m_scr)

    _mlstm_step(qf_ref, kf_ref, vf_ref, gcf_ref, grf_ref, hf_ref, c_scr.at[0], n_scr.at[0], m_scr.at[0], lc, False)
    _mlstm_step(qb_ref, kb_ref, vb_ref, gcb_ref, grb_ref, hb_ref, c_scr.at[1], n_scr.at[1], m_scr.at[1], lc, True)


def _mlstm_step(q_ref, k_ref, v_ref, gc_ref, gr_ref, h_ref, c_scr, n_scr, m_scr, lc, backward):
    q = q_ref[...]
    k = k_ref[...]
    v = v_ref[...]
    gc = gc_ref[...]
    gr = gr_ref[...]
    i_col, g_col = gc[:, 0:1], gc[:, 1:2]
    i_row, g_row = gr[0:1, :], gr[1:2, :]

    jr = lax.broadcasted_iota(jnp.int32, (lc, lc), 0)
    sc = lax.broadcasted_iota(jnp.int32, (lc, lc), 1)
    seen = (sc >= jr) if backward else (sc <= jr)
    g_tot = g_row[:, 0:1] if backward else g_row[:, lc - 1:lc]
    m_old = m_scr[0:1, 0:1]

    dmat = jnp.where(seen, g_col - g_row + i_row, -jnp.inf)
    inter = g_col + m_old
    m_q = jnp.maximum(inter, jnp.max(dmat, axis=1, keepdims=True))
    s_qk = lax.dot_general(q, k, (((1,), (1,)), ((), ())), preferred_element_type=F32)
    a = s_qk * jnp.exp(dmat - m_q)
    w_int = jnp.exp(inter - m_q)
    q_c = jnp.dot(q, c_scr[...].astype(BF16), preferred_element_type=F32)
    num = jnp.dot(a.astype(BF16), v, preferred_element_type=F32) + q_c * w_int
    q_n = jnp.sum(q.astype(F32) * n_scr[...], axis=1, keepdims=True)
    den = jnp.sum(a, axis=1, keepdims=True) + w_int * q_n
    den = jnp.maximum(jnp.abs(den), jnp.exp(-m_q))
    h_ref[...] = num / den

    a_row = g_tot - g_row + i_row
    m_new = jnp.maximum(g_tot + m_old, jnp.max(a_row, axis=1, keepdims=True))
    ws_col = jnp.exp(g_tot - g_col + i_col - m_new)
    dec = jnp.exp(g_tot + m_old - m_new)
    kw = k.astype(F32) * ws_col
    upd = lax.dot_general(kw.astype(BF16), v, (((0,), (0,)), ((), ())), preferred_element_type=F32)
    c_scr[...] = dec * c_scr[...] + upd
    n_scr[...] = dec * n_scr[...] + jnp.sum(kw, axis=0, keepdims=True)
    m_scr[...] = jnp.broadcast_to(m_new, m_scr.shape)


def _ml_gate_kernel(g_ref, o_ref, *, lc, nh):
    g = g_ref[...]
    jr = lax.broadcasted_iota(jnp.int32, (lc, lc), 0)
    sc = lax.broadcasted_iota(jnp.int32, (lc, lc), 1)
    tril = jnp.where(sc <= jr, 1.0, 0.0).astype(BF16)
    triu = jnp.where(sc >= jr, 1.0, 0.0).astype(BF16)
    col = lax.broadcasted_iota(jnp.int32, (1, 4 * nh), 1)
    lf = _log_sigmoid(g)
    x1 = lf.astype(BF16)
    r1 = lf - x1.astype(F32)
    x2 = r1.astype(BF16)
    x3 = (r1 - x2.astype(F32)).astype(BF16)
    cum = lambda m: (jnp.dot(m, x1, preferred_element_type=F32) + jnp.dot(m, x2, preferred_element_type=F32)
                     + jnp.dot(m, x3, preferred_element_type=F32))
    gsum = jnp.where(col < 2 * nh, cum(tril), cum(triu))
    o_ref[...] = jnp.where((col % (2 * nh)) >= nh, gsum, g)


def _mlstm_chunks(q, k, v, gates, batch, seq_len):
    n, inner = q.shape
    dh = inner // ML_HEADS
    lc = min(ML_CHUNK, seq_len)
    nc = seq_len // lc
    ng = gates.shape[1]
    gates = pl.pallas_call(
        functools.partial(_ml_gate_kernel, lc=lc, nh=ML_HEADS),
        out_shape=jax.ShapeDtypeStruct((n, ng), F32),
        grid=(n // lc,),
        in_specs=[pl.BlockSpec((lc, ng), lambda i: (i, 0))],
        out_specs=pl.BlockSpec((lc, ng), lambda i: (i, 0)),
        compiler_params=_params("parallel"),
    )(gates)
    g4 = gates.reshape(n, 2, 2, ML_HEADS)
    gcol = jnp.transpose(g4, (1, 3, 0, 2))
    grow = jnp.transpose(g4, (1, 3, 2, 0))

    def specs(dd):
        chunk = (lambda b, c: b * nc + c) if dd == 0 else (lambda b, c: b * nc + nc - 1 - c)
        qkv = pl.BlockSpec((lc, dh), lambda b, h, c: (chunk(b, c), h))
        gc = pl.BlockSpec((None, None, lc, 2), lambda b, h, c: (dd, h, chunk(b, c), 0))
        gr = pl.BlockSpec((None, None, 2, lc), lambda b, h, c: (dd, h, 0, chunk(b, c)))
        return [qkv, qkv, qkv, gc, gr], qkv

    in_f, out_f = specs(0)
    in_b, out_b = specs(1)
    return pl.pallas_call(
        functools.partial(_mlstm_chunk_kernel, lc=lc),
        out_shape=(jax.ShapeDtypeStruct((n, inner), F32),) * 2,
        grid=(batch, ML_HEADS, nc),
        in_specs=in_f + in_b,
        out_specs=(out_f, out_b),
        scratch_shapes=[pltpu.VMEM((2, dh, dh), F32), pltpu.VMEM((2, 1, dh), F32),
                        pltpu.VMEM((2, HALO_ROWS, HEAD_DIM), F32)],
        compiler_params=_params("parallel", "parallel", "arbitrary"),
    )(q, k, v, gcol, grow, q, k, v, gcol, grow)


def _ml_out_prologue(hf, hb, xc, sz, gain, skip):
    h = hf + hb
    dh = h.shape[1] // ML_HEADS
    parts = []
    for i in range(ML_HEADS):
        sl = slice(i * dh, (i + 1) * dh)
        parts.append(_rms(h[:, sl], gain[:, sl]))
    hn = jnp.concatenate(parts, axis=1)
    return (hn + skip * xc) * sz


def _mlstm_layer(x, batch, seq_len, g, p, route):
    q, k, v, xc, sz, gates = _ml_in(x, g, p, seq_len)
    inner = q.shape[1]
    hf, hb = _mlstm_chunks(q, k, v, gates, batch, seq_len)
    return _mm_res(_ml_out_prologue, [hf, hb, xc, sz],
                   [p['norm_gain'].reshape(1, inner), p['skip'].reshape(1, inner)],
                   p['w_down'], None, x, route, tm=256)


def _rope_tables(L):
    inv = ROPE_THETA ** (-(2.0 * jnp.arange(ROPE_DIMS // 2, dtype=F32)) / ROPE_DIMS)
    ang = jnp.arange(L, dtype=F32)[:, None] * inv[None]
    pad = HEAD_DIM - ROPE_DIMS
    cos = jnp.concatenate([jnp.cos(ang), jnp.cos(ang), jnp.ones((L, pad), F32)], axis=-1)
    sin = jnp.concatenate([-jnp.sin(ang), jnp.sin(ang), jnp.zeros((L, pad), F32)], axis=-1)
    return cos, sin


def _band_kernel(q_ref, kp_ref, kc_ref, kn_ref, vp_ref, vc_ref, vn_ref, o_ref, lse_ref,
                 *, s_len, heads, subs):
    i = pl.program_id(2)
    qb = BAND_BLOCK
    w = qb + 2 * BAND_HALF
    a = lax.broadcasted_iota(jnp.int32, (qb, w), 0)
    c = lax.broadcasted_iota(jnp.int32, (qb, w), 1)
    in_band = jnp.abs(c - BAND_HALF - a) <= BAND_HALF
    scale = HEAD_DIM ** -0.5
    for h in range(heads):
        sl = slice(h * HEAD_DIM, (h + 1) * HEAD_DIM)
        kw = jnp.concatenate([kp_ref[qb - BAND_HALF:, sl], kc_ref[:, sl], kn_ref[:BAND_HALF, sl]], axis=0)
        vw = jnp.concatenate([vp_ref[qb - BAND_HALF:, sl], vc_ref[:, sl], vn_ref[:BAND_HALF, sl]], axis=0)
        for u in range(subs):
            rows = slice(u * qb, (u + 1) * qb)
            key_pos = (i * subs + u) * qb - BAND_HALF + c
            valid = in_band & (key_pos >= 0) & (key_pos < s_len)
            s = lax.dot_general(q_ref[rows, sl], kw[u * qb:u * qb + w], (((1,), (1,)), ((), ())),
                                preferred_element_type=F32) * scale
            s = jnp.where(valid, s, -jnp.inf)
            m = jnp.max(s, axis=1, keepdims=True)
            p = jnp.exp(s - m)
            l = jnp.sum(p, axis=1, keepdims=True)
            o = jnp.dot(p.astype(BF16), vw[u * qb:u * qb + w], preferred_element_type=F32)
            o_ref[rows, sl] = o / l
            lse_ref[rows, sl] = jnp.broadcast_to(m + jnp.log(l), (qb, HEAD_DIM))


def _da_qkv_kernel(x_ref, g_ref, w_ref, qg_ref, kg_ref, cos_ref, sin_ref, *refs, dils, tm, half):
    n_perm = sum(1 for d in dils if d > 1)
    perm_refs, out_refs = refs[:n_perm], refs[n_perm:]
    xb = _rms(x_ref[...], g_ref[...]).astype(BF16)
    cos, sin = cos_ref[...], sin_ref[...]
    gw = DA_HEADS_PER_GROUP * HEAD_DIM
    heads = DA_HEADS_PER_GROUP * len(dils)
    for kind in range(3):
        gain = (qg_ref, kg_ref, None)[kind]
        pi = 0
        for gi, dil in enumerate(dils):
            c0 = (kind * heads + gi * DA_HEADS_PER_GROUP) * HEAD_DIM
            ph = jnp.dot(xb, w_ref[:, c0:c0 + gw], preferred_element_type=F32)
            if gain is not None:
                ph = jnp.concatenate(
                    [_head_norm_rope(ph[:, h * HEAD_DIM:(h + 1) * HEAD_DIM], gain[...], cos, sin, half)
                     for h in range(DA_HEADS_PER_GROUP)], axis=1)
            val = ph.astype(BF16)
            o_ref = out_refs[kind * len(dils) + gi]
            if dil == 1:
                o_ref[...] = val
            else:
                pv = jnp.dot(perm_refs[pi][...], val, preferred_element_type=F32).astype(BF16)
                pi += 1
                rows = tm // dil
                for r in range(dil):
                    o_ref[:, r * gw:(r + 1) * gw] = pv[r * rows:(r + 1) * rows, :]


def _da_qkv(x, g, w_qkv, q_gain, k_gain, cos, sin, seq_len, tm=256):
    n, d = x.shape
    tm = min(tm, seq_len)
    pos_blocks = seq_len // tm
    dils = tuple(dil for _, dil in DA_GROUPS)
    gw = DA_HEADS_PER_GROUP * HEAD_DIM
    f = w_qkv.shape[1]
    perms = []
    for dil in dils:
        if dil > 1:
            perms.append(_dilation_perm(tm, dil))
    row = pl.BlockSpec((tm, d), lambda i: (i, 0))
    tab = pl.BlockSpec((tm, HEAD_DIM), lambda i: (i % pos_blocks, 0))
    out_shapes = tuple(jax.ShapeDtypeStruct((n // dil, dil * gw), BF16) for _ in range(3) for dil in dils)
    out_specs = tuple(pl.BlockSpec((tm // dil, dil * gw), lambda i: (i, 0)) for _ in range(3) for dil in dils)
    outs = pl.pallas_call(
        functools.partial(_da_qkv_kernel, dils=dils, tm=tm, half=ROPE_DIMS // 2),
        out_shape=out_shapes,
        grid=(n // tm,),
        in_specs=[row, _const_spec((1, d)), _const_spec((d, f), single=True), _const_spec((1, HEAD_DIM)),
                  _const_spec((1, HEAD_DIM)), tab, tab] + [_const_spec((tm, tm))] * len(perms),
        out_specs=out_specs,
        compiler_params=_params("parallel"),
    )(x, g.reshape(1, d), w_qkv.astype(BF16), q_gain.reshape(1, HEAD_DIM), k_gain.reshape(1, HEAD_DIM),
      cos, sin, *perms)
    ng = len(dils)
    return [(outs[gi], outs[ng + gi], outs[2 * ng + gi]) for gi in range(ng)]


def _band_attention(q, k, v, batch, seq_len, dil):
    n = batch * seq_len
    s_len = seq_len // dil
    nb = s_len // BAND_BLOCK
    subs = min(BAND_SUBBLOCKS, nb)
    hw = DA_HEADS_PER_GROUP * HEAD_DIM
    view = lambda a: a.reshape(batch, s_len, dil * hw)
    qv, kv, vv = view(q), view(k), view(v)

    main = pl.BlockSpec((None, subs * BAND_BLOCK, hw), lambda b, r, i: (b, i, r))
    prev = pl.BlockSpec((None, BAND_BLOCK, hw), lambda b, r, i: (b, jnp.maximum(i * subs - 1, 0), r))
    nxt = pl.BlockSpec((None, BAND_BLOCK, hw), lambda b, r, i: (b, jnp.minimum((i + 1) * subs, nb - 1), r))
    o, lse = pl.pallas_call(
        functools.partial(_band_kernel, s_len=s_len, heads=DA_HEADS_PER_GROUP, subs=subs),
        out_shape=(jax.ShapeDtypeStruct((batch, s_len, dil * hw), F32),) * 2,
        grid=(batch, dil, nb // subs),
        in_specs=[main, prev, main, nxt, prev, main, nxt],
        out_specs=(main, main),
        compiler_params=_params("parallel", "parallel", "parallel"),
    )(qv, kv, kv, kv, vv, vv, vv)
    return o.reshape(n // dil, dil * hw), lse.reshape(n // dil, dil * hw)


def _undilate(blk, perm_t):
    hw = DA_HEADS_PER_GROUP * HEAD_DIM
    dil = blk.shape[1] // hw
    stacked = jnp.concatenate([blk[:, r * hw:(r + 1) * hw] for r in range(dil)], axis=0)
    hi, lo = _split_bf16(stacked)
    return jnp.dot(perm_t, hi, preferred_element_type=F32) + jnp.dot(perm_t, lo, preferred_element_type=F32)


def _da_out_prologue(o0, o1, o2, l0, l1, l2, pt1, pt2):
    o1, l1 = _undilate(o1, pt1), _undilate(l1, pt1)
    o2, l2 = _undilate(o2, pt2), _undilate(l2, pt2)
    m = jnp.maximum(jnp.maximum(l0, l1), l2)
    e0, e1, e2 = jnp.exp(l0 - m), jnp.exp(l1 - m), jnp.exp(l2 - m)
    return (e0 * o0 + e1 * o1 + e2 * o2) / (e0 + e1 + e2)


def _dilation_perm(tm, dil):
    dst = jnp.arange(tm)
    src = (dst % (tm // dil)) * dil + dst // (tm // dil)
    return (src[:, None] == jnp.arange(tm)[None, :]).astype(BF16)


def _dilated_layer(x, batch, seq_len, g, p, route):
    cos, sin = _rope_tables(seq_len)
    qkv = _da_qkv(x, g, p['w_qkv'], p['q_gain'], p['k_gain'], cos, sin, seq_len)
    outs, lses = [], []
    for (q, k, v), (_, dil) in zip(qkv, DA_GROUPS):
        o, lse = _band_attention(q, k, v, batch, seq_len, dil)
        outs.append(o)
        lses.append(lse)
    tm = min(DA_OUT_ROWS, seq_len)
    perms_t = [_dilation_perm(tm, dil).T for _, dil in DA_GROUPS[1:]]
    return _mm_res(_da_out_prologue, outs + lses, perms_t, p['w_o'], None, x, route, tm=tm)


def _expert_ffn_kernel(xe_ref, gate_ref, w1_ref, w3_ref, w2_ref, ye_ref, w1b, w3b, w2b):
    @pl.when(pl.program_id(1) == 0)
    def _():
        w1b[...] = w1_ref[...].astype(BF16)
        w3b[...] = w3_ref[...].astype(BF16)
        w2b[...] = w2_ref[...].astype(BF16)

    xe = xe_ref[...]
    h1 = jnp.dot(xe, w1b[...], preferred_element_type=F32)
    h3 = jnp.dot(xe, w3b[...], preferred_element_type=F32)
    hid = (h1 * jax.nn.sigmoid(h1) * h3).astype(BF16)
    ye_ref[...] = (jnp.dot(hid, w2b[...], preferred_element_type=F32) * gate_ref[...]).astype(ye_ref.dtype)


def _expert_ffn(xe, gates, w1, w3, w2, layer, tm=512):
    e, c, d = xe.shape
    f = w1.shape[3]
    tm = min(tm, c)
    return pl.pallas_call(
        _expert_ffn_kernel,
        out_shape=jax.ShapeDtypeStruct((e, c, d), BF16),
        grid=(e, c // tm),
        in_specs=[pl.BlockSpec((None, tm, d), lambda ei, ci: (ei, ci, 0)),
                  pl.BlockSpec((None, tm, 1), lambda ei, ci: (ei, ci, 0)),
                  pl.BlockSpec((None, None, d, f), lambda ei, ci: (layer, ei, 0, 0)),
                  pl.BlockSpec((None, None, d, f), lambda ei, ci: (layer, ei, 0, 0)),
                  pl.BlockSpec((None, None, f, d), lambda ei, ci: (layer, ei, 0, 0))],
        out_specs=pl.BlockSpec((None, tm, d), lambda ei, ci: (ei, ci, 0)),
        scratch_shapes=[pltpu.VMEM((d, f), BF16), pltpu.VMEM((d, f), BF16), pltpu.VMEM((f, d), BF16)],
        compiler_params=_params("parallel", "arbitrary"),
    )(xe, gates[..., None], w1, w3, w2)


def _moe_layer(x, xn, aff_t, group_sizes, w1, w3, w2, layer, split_output):
    n, d = x.shape
    gts, idxs = [], []
    start = 0
    for ng in group_sizes:
        cap = EC_CAPACITY * ng // N_EXPERTS
        gates, idx = lax.top_k(aff_t[:, start:start + ng], cap)
        gts.append(gates)
        idxs.append(idx + start)
        start += ng
    idx_all = jnp.concatenate(idxs, axis=1)
    ye = _expert_ffn(xn[idx_all], jnp.concatenate(gts, axis=1), w1, w3, w2, layer)
    return _combine(x, ye.reshape(-1, d), idx_all.reshape(-1), group_sizes if split_output else (n,))


COMBINE_TOKENS = 512
COMBINE_ROWS = 512


def _combine_kernel(tile_ref, blk_ref, live_ref, x_ref, tok_ref, ye_ref, *o_refs, split_tiles):
    w = pl.program_id(0)
    tile = tile_ref[w]
    first = jnp.logical_or(w == 0, tile != tile_ref[jnp.maximum(w - 1, 0)])
    tt = x_ref.shape[0]
    rows = lax.broadcasted_iota(jnp.int32, (tt, COMBINE_ROWS), 0)
    onehot = jnp.where(rows == tok_ref[...] - tile * tt, 1.0, 0.0).astype(BF16)
    add = jnp.dot(onehot, ye_ref[...], preferred_element_type=F32) * live_ref[w].astype(F32)
    lo_tile = 0
    for o_ref, n_tiles in zip(o_refs, split_tiles):
        mine = jnp.logical_and(tile >= lo_tile, tile < lo_tile + n_tiles)

        @pl.when(jnp.logical_and(mine, first))
        def _(o_ref=o_ref):
            o_ref[...] = x_ref[...] + add

        @pl.when(jnp.logical_and(mine, jnp.logical_not(first)))
        def _(o_ref=o_ref):
            o_ref[...] += add

        lo_tile += n_tiles


def _combine(x, ye, tok, splits):
    n, d = x.shape
    p = tok.shape[0]
    tt = min(COMBINE_TOKENS, min(splits))
    rb = COMBINE_ROWS
    assert p % rb == 0 and all(s % tt == 0 for s in splits) and sum(splits) == n
    tiles, nblk = n // tt, p // rb
    split_tiles = tuple(s // tt for s in splits)
    order = jnp.argsort(tok)
    tok_sorted = tok[order].astype(jnp.int32)
    ye_sorted = ye[order]

    edges = jnp.arange(tiles + 1, dtype=jnp.int32) * tt
    bounds = jnp.searchsorted(tok_sorted, edges, method='compare_all').astype(jnp.int32)
    lo, hi = bounds[:-1], bounds[1:]
    first_blk = jnp.minimum(lo // rb, nblk - 1)
    last_blk = jnp.where(hi > lo, (hi - 1) // rb, first_blk)
    n_items = last_blk - first_blk + 1
    item_end = jnp.cumsum(n_items)
    item_start = item_end - n_items
    max_items = nblk + 2 * tiles
    w = jnp.arange(max_items, dtype=jnp.int32)
    tile = jnp.minimum(jnp.searchsorted(item_end, w, side='right', method='compare_all'),
                       tiles - 1).astype(jnp.int32)
    k = w - item_start[tile]
    live = (k < n_items[tile]).astype(jnp.int32)
    blk = jnp.minimum(first_blk[tile] + k, nblk - 1).astype(jnp.int32)

    def out_spec(lo_tile, n_tiles):
        return pl.BlockSpec(
            (tt, d), lambda i, tile_r, blk_r, live_r: (jnp.clip(tile_r[i] - lo_tile, 0, n_tiles - 1), 0))

    starts = [sum(split_tiles[:j]) for j in range(len(splits))]
    grid_spec = pltpu.PrefetchScalarGridSpec(
        num_scalar_prefetch=3,
        grid=(max_items,),
        in_specs=[pl.BlockSpec((tt, d), lambda i, tile_r, blk_r, live_r: (tile_r[i], 0)),
                  pl.BlockSpec((None, 1, rb), lambda i, tile_r, blk_r, live_r: (blk_r[i], 0, 0)),
                  pl.BlockSpec((rb, d), lambda i, tile_r, blk_r, live_r: (blk_r[i], 0))],
        out_specs=tuple(out_spec(s, t) for s, t in zip(starts, split_tiles)),
    )
    return pl.pallas_call(
        functools.partial(_combine_kernel, split_tiles=split_tiles),
        out_shape=tuple(jax.ShapeDtypeStruct((s, d), F32) for s in splits),
        grid_spec=grid_spec,
        compiler_params=_params("arbitrary"),
    )(tile, blk, live, x, tok_sorted.reshape(nblk, 1, rb), ye_sorted)


def _trunk(x, batch, seq_len, group_sizes, p):
    depth = p['norm_gain'].shape[0]
    layers = ((_hyena_layer, 'hy_'), (_gqa_layer, 'ga_'), (_mlstm_layer, 'ml_'), (_dilated_layer, 'da_'))
    for i in range(depth):
        layer, prefix = layers[i % 4]
        lp = {k[len(prefix):]: v[i // 4] for k, v in p.items() if k.startswith(prefix)}
        route = (p['norm_gain'][i, 1], p['moe_w_router'][i])
        x, xn, aff_t = layer(x, batch, seq_len, p['norm_gain'][i, 0], lp, route)
        last = i == depth - 1
        outs = _moe_layer(x, xn, aff_t, group_sizes, p['moe_w1'], p['moe_w3'], p['moe_w2'], i,
                          split_output=last)
        x = outs if last else outs[0]
    return x


def kernel(x_prompt, x_sample, norm_gain, hy_w_in, hy_b_in, hy_conv_w, hy_conv_b, hy_f_w1, hy_f_b1, hy_f_w2, hy_f_b2, hy_f_w3, hy_f_b3, hy_f_freq, hy_decay, hy_skip, hy_w_out, hy_b_out, ga_w_qkv, ga_q_gain, ga_k_gain, ga_w_o, ml_w_up, ml_conv_w, ml_conv_b, ml_w_q, ml_w_k, ml_w_v, ml_w_gate, ml_b_gate, ml_norm_gain, ml_skip, ml_w_down, da_w_qkv, da_q_gain, da_k_gain, da_w_o, moe_w_router, moe_w1, moe_w3, moe_w2):
    p = dict(
        norm_gain=norm_gain,
        hy_w_in=hy_w_in, hy_b_in=hy_b_in, hy_conv_w=hy_conv_w, hy_conv_b=hy_conv_b,
        hy_f_w1=hy_f_w1, hy_f_b1=hy_f_b1, hy_f_w2=hy_f_w2, hy_f_b2=hy_f_b2,
        hy_f_w3=hy_f_w3, hy_f_b3=hy_f_b3, hy_f_freq=hy_f_freq, hy_decay=hy_decay,
        hy_skip=hy_skip, hy_w_out=hy_w_out, hy_b_out=hy_b_out,
        ga_w_qkv=ga_w_qkv, ga_q_gain=ga_q_gain, ga_k_gain=ga_k_gain, ga_w_o=ga_w_o,
        ml_w_up=ml_w_up, ml_conv_w=ml_conv_w, ml_conv_b=ml_conv_b, ml_w_q=ml_w_q,
        ml_w_k=ml_w_k, ml_w_v=ml_w_v, ml_w_gate=ml_w_gate, ml_b_gate=ml_b_gate,
        ml_norm_gain=ml_norm_gain, ml_skip=ml_skip, ml_w_down=ml_w_down,
        da_w_qkv=da_w_qkv, da_q_gain=da_q_gain, da_k_gain=da_k_gain, da_w_o=da_w_o,
        moe_w_router=moe_w_router, moe_w1=moe_w1, moe_w3=moe_w3, moe_w2=moe_w2,
    )
    bp, seq_len, d = x_prompt.shape
    bs = x_sample.shape[0]
    assert x_sample.shape[1] == seq_len
    x = jnp.concatenate([x_prompt, x_sample], axis=0).reshape((bp + bs) * seq_len, d)
    y_prompt, y_sample = _trunk(x, bp + bs, seq_len, (bp * seq_len, bs * seq_len), p)
    return (y_prompt.reshape(bp, seq_len, d), y_sample.reshape(bs, seq_len, d))
```

```python
import functools
import math

import jax
import jax.numpy as jnp
from jax import lax
from jax.experimental import pallas as pl
from jax.experimental.pallas import tpu as pltpu

F32 = jnp.float32
BF16 = jnp.bfloat16
HIGHEST = lax.Precision.HIGHEST

NORM_EPS = 1e-6
GRID_W = 64
HY_BANDS = 16
GA_HEADS = 8
GA_KV_HEADS = 2
GA_GROUP = GA_HEADS // GA_KV_HEADS
HEAD_DIM = 128
AXIAL_THETA = 10000.0
ML_HEADS = 4
ML_QKV_BLOCK = 4
DA_GROUPS = ((128, 1), (512, 4), (2048, 16))
DA_HEADS_PER_GROUP = 4
DA_HEADS = DA_HEADS_PER_GROUP * len(DA_GROUPS)
ROPE_THETA = 500000.0
ROPE_DIMS = HEAD_DIM // 4
N_EXPERTS = 16
EC_CAPACITY = 2

VMEM_LIMIT_BYTES = 52 * 1024 * 1024
HALO_ROWS = 8
MXU_DIM = 256
ML_CHUNK = 256
BAND_BLOCK = 128
BAND_HALF = 64
BAND_SUBBLOCKS = 4
DA_OUT_ROWS = 256


def _params(*sem):
    return pltpu.CompilerParams(dimension_semantics=sem, vmem_limit_bytes=VMEM_LIMIT_BYTES)


def _rms(x, g):
    ms = jnp.mean(x * x, axis=-1, keepdims=True)
    return x * lax.rsqrt(ms + NORM_EPS) * g


def _const_spec(shape, single=False):
    nd = len(shape)
    if single:
        return pl.BlockSpec(shape, lambda *_: (0,) * nd, pipeline_mode=pl.Buffered(1))
    return pl.BlockSpec(shape, lambda *_: (0,) * nd)


def _conv3_rows(p, pprev, pnext, cw, cb, rows, tm):
    up = jnp.where(rows == 0, pprev, pltpu.roll(p, 1, 0))
    dn = jnp.where(rows == tm - 1, pnext, pltpu.roll(p, tm - 1, 0))
    return up * cw[0:1] + p * cw[1:2] + dn * cw[2:3] + cb


def _halo_specs(tm, d, n_rows):
    hb = tm // HALO_ROWS
    last = n_rows // HALO_ROWS - 1
    prev = pl.BlockSpec((HALO_ROWS, d), lambda i: (jnp.maximum(i * hb - 1, 0), 0))
    nxt = pl.BlockSpec((HALO_ROWS, d), lambda i: (jnp.minimum((i + 1) * hb, last), 0))
    return prev, nxt


def _edge_scales(i, tm, seq_len):
    t0 = i * tm
    keep_prev = jnp.where(t0 % seq_len == 0, 0.0, 1.0).astype(F32)
    keep_next = jnp.where((t0 + tm) % seq_len == 0, 0.0, 1.0).astype(F32)
    return keep_prev, keep_next


def _route(x, g_ref, wr_ref, xn_ref, aff_ref):
    xn = _rms(x, g_ref[...])
    xn_ref[...] = xn.astype(BF16)
    logits = lax.dot_general(wr_ref[...], xn, (((1,), (1,)), ((), ())),
                             precision=HIGHEST, preferred_element_type=F32)
    m = jnp.max(logits, axis=0, keepdims=True)
    e = jnp.exp(logits - m)
    aff_ref[...] = e / jnp.sum(e, axis=0, keepdims=True)


def _mm_res_kernel(*refs, prologue, n_row, n_const, tn):
    row_refs = refs[:n_row]
    const_refs = refs[n_row:n_row + n_const]
    w_ref, b_ref, res_ref, g2_ref, wr_ref, o_ref, xn_ref, aff_ref = refs[n_row + n_const:]
    lhs = prologue(*[r[...] for r in row_refs], *[c[...] for c in const_refs]).astype(BF16)
    for j in range(o_ref.shape[1] // tn):
        sl = slice(j * tn, (j + 1) * tn)
        o_ref[:, sl] = (res_ref[:, sl] + b_ref[:, sl]
                        + jnp.dot(lhs, w_ref[:, sl], preferred_element_type=F32))
    _route(o_ref[...], g2_ref, wr_ref, xn_ref, aff_ref)


def _mm_res(prologue, rows, consts, w, b, res, route, tm=512, tn=512):
    n, dout = res.shape
    tm = min(tm, n)
    k = w.shape[0]
    g2, w_router = route
    ne = w_router.shape[1]
    if b is None:
        b = jnp.zeros((1, dout), F32)
    in_specs = [pl.BlockSpec((tm * r.shape[0] // n, r.shape[1]), lambda i: (i, 0)) if r.ndim == 2
                else pl.BlockSpec((r.shape[0], tm, r.shape[2]), lambda i: (0, i, 0)) for r in rows]
    in_specs += [_const_spec(c.shape) for c in consts]
    in_specs += [_const_spec((k, dout)), _const_spec((1, dout)),
                 pl.BlockSpec((tm, dout), lambda i: (i, 0)),
                 _const_spec((1, dout)), _const_spec((ne, dout))]
    tile = pl.BlockSpec((tm, dout), lambda i: (i, 0))
    return pl.pallas_call(
        functools.partial(_mm_res_kernel, prologue=prologue, n_row=len(rows),
                          n_const=len(consts), tn=min(tn, dout)),
        out_shape=(jax.ShapeDtypeStruct((n, dout), F32), jax.ShapeDtypeStruct((n, dout), BF16),
                   jax.ShapeDtypeStruct((ne, n), F32)),
        grid=(n // tm,),
        in_specs=in_specs,
        out_specs=(tile, tile, pl.BlockSpec((ne, tm), lambda i: (0, i))),
        compiler_params=_params("parallel"),
    )(*rows, *consts, w.astype(BF16), b.reshape(1, dout).astype(F32), res,
      g2.reshape(1, dout), w_router.T)


def _hyena_in_kernel(x_ref, xp_ref, xn_ref, g_ref, w_ref, b_ref, cw_ref, cb_ref,
                     x0_ref, vx_ref, *, seq_len, tm, d, cols):
    keep_prev, keep_next = _edge_scales(pl.program_id(0), tm, seq_len)
    g = g_ref[...]
    xb = _rms(jnp.concatenate([x_ref[...], xp_ref[...], xn_ref[...]], axis=0), g).astype(BF16)
    rows = lax.broadcasted_iota(jnp.int32, (tm, 1), 0)

    def conv_part(c0):
        sl = slice(c0, c0 + cols)
        pa = jnp.dot(xb, w_ref[:, sl], preferred_element_type=F32) + b_ref[:, sl]
        p = pa[:tm]
        pprev = pa[tm + HALO_ROWS - 1:tm + HALO_ROWS, :] * keep_prev
        pnext = pa[tm + HALO_ROWS:tm + HALO_ROWS + 1, :] * keep_next
        return _conv3_rows(p, pprev, pnext, cw_ref[:, sl], cb_ref[:, sl], rows, tm)

    for j in range(d // cols):
        c = j * cols
        x0_ref[:, c:c + cols] = conv_part(c)
        vx_ref[:, c:c + cols] = conv_part(2 * d + c) * conv_part(d + c)


def _hyena_in(x, g, w_in, b_in, conv_w, conv_b, seq_len, tm=512, cols=512):
    n, d = x.shape
    tm = min(tm, seq_len)
    prev, nxt = _halo_specs(tm, d, n)
    row = pl.BlockSpec((tm, d), lambda i: (i, 0))
    return pl.pallas_call(
        functools.partial(_hyena_in_kernel, seq_len=seq_len, tm=tm, d=d, cols=cols),
        out_shape=(jax.ShapeDtypeStruct((n, d), F32), jax.ShapeDtypeStruct((n, d), F32)),
        grid=(n // tm,),
        in_specs=[row, prev, nxt, _const_spec((1, d)), _const_spec((d, 3 * d)),
                  _const_spec((1, 3 * d)), _const_spec((3, 3 * d)), _const_spec((1, 3 * d))],
        out_specs=(row, row),
        compiler_params=_params("parallel"),
    )(x, x, x, g.reshape(1, d), w_in.astype(BF16), b_in.reshape(1, 3 * d),
      conv_w, conv_b.reshape(1, 3 * d))


def _hyena_filter_taps(L, d, f_w1, f_b1, f_w2, f_b2, f_w3, f_b3, f_freq, decay):
    t = jnp.linspace(0.0, 1.0, L, dtype=F32)[:, None]
    w_ang = 2.0 * math.pi * jnp.arange(L, dtype=F32)[:, None] / L
    bands = jnp.linspace(1e-4, HY_BANDS - 1, HY_BANDS, dtype=F32)[None, :]
    z = jnp.concatenate([t, jnp.cos(bands * w_ang), -jnp.sin(bands * w_ang)], axis=-1)

    def branch(zz, tt, col0):
        h = jnp.sin(f_freq[0] * (zz @ f_w1 + f_b1))
        h = jnp.sin(f_freq[1] * (h @ f_w2 + f_b2))
        h = h @ f_w3[:, col0:col0 + d] + f_b3[col0:col0 + d]
        return h * jnp.exp(-tt * jnp.abs(decay[col0 // d])[None])

    h_fwd = branch(z, t, 0)
    h_bwd_rev = branch(z[::-1], t[::-1], d)
    k = jnp.concatenate([h_fwd, jnp.zeros((1, d), F32), h_bwd_rev[:-1]], axis=0)
    return k / jnp.sum(jnp.abs(k), axis=0, keepdims=True)


def _fft_dims(m):
    lg = m.bit_length() - 1
    p = 1 << ((lg + 1) // 2)
    return p, m // p


def _split_bf16(x):
    hi = x.astype(BF16)
    return hi, (x - hi.astype(F32)).astype(BF16)


def _mm_split(fh, fl, x, precise):
    if not precise:
        return jnp.dot(fh, x.astype(BF16), preferred_element_type=F32)
    xh, xl = _split_bf16(x)
    return (jnp.dot(fh, xh, preferred_element_type=F32) + jnp.dot(fl, xh, preferred_element_type=F32)
            + jnp.dot(fh, xl, preferred_element_type=F32))


def _cplx_as_real(cr, ci):
    top = jnp.concatenate([cr, -ci], axis=-1)
    bot = jnp.concatenate([ci, cr], axis=-1)
    return jnp.concatenate([top, bot], axis=-2)


def _unit_circle(idx, m):
    ang = (2.0 * math.pi / m) * idx.astype(F32)
    return jnp.cos(ang), jnp.sin(ang)


def _dft_consts(p, q):
    m = p * q
    k1 = jnp.arange(p, dtype=jnp.int32)
    n1 = jnp.arange(p // 2, dtype=jnp.int32)
    c, s = _unit_circle((k1[:, None] * n1[None, :]) % p, p)
    fa = _cplx_as_real(c, -s)
    c, s = _unit_circle((n1[:, None] * k1[None, :]) % p, p)
    fd = _cplx_as_real(c / m, s / m)
    k2 = jnp.arange(q, dtype=jnp.int32)
    n2 = jnp.arange(q, dtype=jnp.int32)
    idx = (n2[None, None, :] * (k2[None, :, None] * p + k1[:, None, None])) % m
    c, s = _unit_circle(idx, m)
    gb = _cplx_as_real(c, -s)
    ct, st = jnp.swapaxes(c, 1, 2), jnp.swapaxes(s, 1, 2)
    gc = _cplx_as_real(ct, st)
    return tuple(_split_bf16(a) for a in (fa, gb, gc, fd))


def _fft_a_kernel(x_ref, fh_ref, fl_ref, o_ref, *, precise):
    _, rows_in, group, d = x_ref.shape
    rows_out = o_ref.shape[1]
    for j in range(group):
        x = x_ref[:, :, j, :].reshape(2 * rows_in, d)
        y = _mm_split(fh_ref[...], fl_ref[...], x, precise)
        o_ref[:, :, j, :] = y.reshape(2, rows_out, d)


def _fft_b_kernel(a_ref, gh_ref, gl_ref, o_ref):
    _, q, d = a_ref.shape
    x = _mm_split(gh_ref[...], gl_ref[...], a_ref[...].reshape(2 * q, d), True)
    o_ref[...] = x.reshape(o_ref.shape)


def _fft_bc_kernel(a_ref, gb_ref, gc_ref, k_ref, z_ref):
    _, q, d = a_ref.shape
    x = _mm_split(gb_ref[...], None, a_ref[...].reshape(2 * q, d), False)
    xr, xi = x[:q], x[q:]
    kr, ki = k_ref[0], k_ref[1]
    y = jnp.concatenate([xr * kr - xi * ki, xr * ki + xi * kr], axis=0)
    z = _mm_split(gc_ref[...], None, y, False)
    z_ref[...] = z.reshape(z_ref.shape)


def _fft_rows(x5, f, rows_out, precise):
    pairs, _, rows_in, q, d = x5.shape
    blk = lambda r: pl.BlockSpec((None, 2, r, HALO_ROWS, d), lambda b, j: (b, 0, 0, j, 0))
    return pl.pallas_call(
        functools.partial(_fft_a_kernel, precise=precise),
        out_shape=jax.ShapeDtypeStruct((pairs, 2, rows_out, q, d), F32),
        grid=(pairs, q // HALO_ROWS),
        in_specs=[blk(rows_in), _const_spec(f[0].shape), _const_spec(f[1].shape)],
        out_specs=blk(rows_out),
        compiler_params=_params("parallel", "parallel"),
    )(x5, *f)


def _long_conv(vx, taps, batch, seq_len):
    n, d = vx.shape
    m = 2 * seq_len
    p, q = _fft_dims(m)
    assert batch % 2 == 0
    pairs = batch // 2
    fa, gb, gc, fd = _dft_consts(p, q)
    g_spec = pl.BlockSpec((None, 2 * q, 2 * q), lambda k1, b: (k1, 0, 0))
    slab = pl.BlockSpec((None, 2, q, d), lambda k1, b: (b, 0, k1, 0))

    zeros = jnp.zeros((seq_len, d), F32)
    kin = jnp.stack([taps[:seq_len], zeros, taps[seq_len:], zeros]).reshape(2, 2, p // 2, q, d)
    ka = _fft_rows(kin, fa, p, True).reshape(2, 2, p * q, d)
    kx = pl.pallas_call(
        _fft_b_kernel,
        out_shape=jax.ShapeDtypeStruct((2, 2, p * q, d), F32),
        grid=(p, 2),
        in_specs=[slab, g_spec, g_spec],
        out_specs=slab,
        compiler_params=_params("parallel", "parallel"),
    )(ka, *gb)
    sign = jnp.repeat(1.0 - 2.0 * (jnp.arange(p) % 2).astype(F32), q)[None, :, None]
    kspec = kx[0] + sign * kx[1]

    xa = _fft_rows(vx.reshape(pairs, 2, p // 2, q, d), fa, p, False).reshape(pairs, 2, p * q, d)
    z = pl.pallas_call(
        _fft_bc_kernel,
        out_shape=jax.ShapeDtypeStruct((pairs, 2, p * q, d), F32),
        grid=(p, pairs),
        in_specs=[slab, g_spec, g_spec, pl.BlockSpec((2, q, d), lambda k1, b: (0, k1, 0))],
        out_specs=slab,
        compiler_params=_params("parallel", "parallel"),
    )(xa, gb[0], gc[0], kspec)
    y = _fft_rows(z.reshape(pairs, 2, p, q, d), fd, p // 2, False)
    return y.reshape(n, d)


def _hyena_out_prologue(y, vx, x0, skip):
    return (y + vx * skip) * x0


def _hyena_layer(x, batch, seq_len, g, p, route):
    n, d = x.shape
    x0, vx = _hyena_in(x, g, p['w_in'], p['b_in'], p['conv_w'], p['conv_b'], seq_len)
    taps = _hyena_filter_taps(seq_len, d, p['f_w1'], p['f_b1'], p['f_w2'], p['f_b2'],
                              p['f_w3'], p['f_b3'], p['f_freq'], p['decay'])
    y = _long_conv(vx, taps, batch, seq_len)
    return _mm_res(_hyena_out_prologue, [y, vx, x0], [p['skip'].reshape(1, d)],
                   p['w_out'], p['b_out'], x, route)


def _head_norm_rope(xh, gain, cos, sin, half):
    y = _rms(xh, gain)
    lane = lax.broadcasted_iota(jnp.int32, (1, HEAD_DIM), 1)
    fwd = pltpu.roll(y, HEAD_DIM - half, 1)
    bwd = pltpu.roll(y, half, 1)
    partner = jnp.where((lane % (2 * half)) < half, fwd, bwd)
    return y * cos + partner * sin


def _qkv_rope_kernel(x_ref, g_ref, w_ref, qg_ref, kg_ref, cos_ref, sin_ref,
                     q_ref, k_ref, v_ref, *, nq, nk, nv, half):
    xb = _rms(x_ref[...], g_ref[...]).astype(BF16)
    cos = cos_ref[...]
    sin = sin_ref[...]
    per = MXU_DIM // HEAD_DIM
    for h0 in range(0, nq + nk + nv, per):
        pw = jnp.dot(xb, w_ref[:, h0 * HEAD_DIM:(h0 + per) * HEAD_DIM], preferred_element_type=F32)
        for h in range(h0, h0 + per):
            ph = pw[:, (h - h0) * HEAD_DIM:(h - h0 + 1) * HEAD_DIM]
            if h < nq:
                sl = slice(h * HEAD_DIM, (h + 1) * HEAD_DIM)
                q_ref[:, sl] = _head_norm_rope(ph, qg_ref[...], cos, sin, half).astype(BF16)
            elif h < nq + nk:
                sl = slice((h - nq) * HEAD_DIM, (h - nq + 1) * HEAD_DIM)
                k_ref[:, sl] = _head_norm_rope(ph, kg_ref[...], cos, sin, half).astype(BF16)
            else:
                sl = slice((h - nq - nk) * HEAD_DIM, (h - nq - nk + 1) * HEAD_DIM)
                v_ref[:, sl] = ph.astype(BF16)


def _qkv_rope(x, g, w_qkv, q_gain, k_gain, cos, sin, nq, nk, nv, half, seq_len, tm=512):
    n, d = x.shape
    tm = min(tm, seq_len)
    pos_blocks = seq_len // tm
    f = w_qkv.shape[1]
    row = pl.BlockSpec((tm, d), lambda i: (i, 0))
    tab = pl.BlockSpec((tm, HEAD_DIM), lambda i: (i % pos_blocks, 0))
    outs = tuple(jax.ShapeDtypeStruct((n, c * HEAD_DIM), BF16) for c in (nq, nk, nv))
    return pl.pallas_call(
        functools.partial(_qkv_rope_kernel, nq=nq, nk=nk, nv=nv, half=half),
        out_shape=outs,
        grid=(n // tm,),
        in_specs=[row, _const_spec((1, d)), _const_spec((d, f)),
                  _const_spec((1, HEAD_DIM)), _const_spec((1, HEAD_DIM)), tab, tab],
        out_specs=tuple(pl.BlockSpec((tm, c * HEAD_DIM), lambda i: (i, 0)) for c in (nq, nk, nv)),
        compiler_params=_params("parallel"),
    )(x, g.reshape(1, d), w_qkv.astype(BF16), q_gain.reshape(1, HEAD_DIM),
      k_gain.reshape(1, HEAD_DIM), cos, sin)


def _axial_tables(L):
    t = jnp.arange(L)
    r = (t // GRID_W).astype(F32)
    c = (t % GRID_W).astype(F32)
    nf = HEAD_DIM // 4
    inv = AXIAL_THETA ** (-(2.0 * jnp.arange(nf, dtype=F32)) / (2 * nf))
    ar, ac = r[:, None] * inv[None], c[:, None] * inv[None]
    cos = jnp.concatenate([jnp.cos(ar), jnp.cos(ar), jnp.cos(ac), jnp.cos(ac)], axis=-1)
    sin = jnp.concatenate([-jnp.sin(ar), jnp.sin(ar), -jnp.sin(ac), jnp.sin(ac)], axis=-1)
    return cos, sin


FLASH_SAFE_BOUND = 40.0
FLASH_BOUND_MARGIN = 1.001


def _flash_kernel(q_ref, k_ref, v_ref, o_ref, acc_scr, off_scr, lsum_scr, m_scr, l_scr, kn_scr,
                  *, tq, tk, tkf, seq_len, group):
    scale = HEAD_DIM ** -0.5
    c = scale * math.log2(math.e)

    @pl.when(pl.program_id(2) == 0)
    def _():
        def norm_step(j, mx):
            kt = k_ref[pl.ds(pl.multiple_of(j * tk, tk), tk), :].astype(F32)
            row = jnp.sum(kt * kt, axis=1, keepdims=True)
            return jnp.maximum(mx, jnp.max(row, axis=0, keepdims=True))

        kmax2 = lax.fori_loop(0, seq_len // tk, norm_step, jnp.zeros((1, 1), F32))
        kn_scr[...] = jnp.broadcast_to(kmax2, kn_scr.shape)

    kmax2 = kn_scr[0:1, 0:1]
    bmax = jnp.zeros((1, 1), F32)
    for h in range(group):
        qf = q_ref[:, h * HEAD_DIM:(h + 1) * HEAD_DIM].astype(F32)
        b = jnp.sqrt(jnp.sum(qf * qf, axis=1, keepdims=True) * kmax2) * FLASH_BOUND_MARGIN
        bmax = jnp.maximum(bmax, jnp.max(b, axis=0, keepdims=True))
        off_scr[h] = jnp.broadcast_to(b * c, (tq, HEAD_DIM))
    fast = (bmax * scale)[0, 0] <= FLASH_SAFE_BOUND

    @pl.when(fast)
    def _():
        acc_scr[...] = jnp.zeros_like(acc_scr)
        lsum_scr[...] = jnp.zeros_like(lsum_scr)
        n_lane_tiles = tkf // HEAD_DIM

        def body(j, carry):
            start = pl.multiple_of(j * tkf, tkf)
            kt = k_ref[pl.ds(start, tkf), :]
            vt = v_ref[pl.ds(start, tkf), :]
            for h in range(group):
                q = q_ref[:, h * HEAD_DIM:(h + 1) * HEAD_DIM]
                s = lax.dot_general(q, kt, (((1,), (1,)), ((), ())), preferred_element_type=F32)
                off = off_scr[h]
                p = jnp.exp2(s * c - jnp.concatenate([off] * n_lane_tiles, axis=1))
                part = p[:, 0:HEAD_DIM]
                for t in range(1, n_lane_tiles):
                    part = part + p[:, t * HEAD_DIM:(t + 1) * HEAD_DIM]
                lsum_scr[h] += part
                acc_scr[h] += jnp.dot(p.astype(BF16), vt, preferred_element_type=F32)
            return carry

        lax.fori_loop(0, seq_len // tkf, body, 0)
        for h in range(group):
            l = jnp.sum(lsum_scr[h], axis=1, keepdims=True)
            o_ref[:, h * HEAD_DIM:(h + 1) * HEAD_DIM] = (acc_scr[h] / l).astype(o_ref.dtype)

    @pl.when(jnp.logical_not(fast))
    def _():
        m_scr[...] = jnp.full(m_scr.shape, -jnp.inf, F32)
        l_scr[...] = jnp.zeros_like(l_scr)
        acc_scr[...] = jnp.zeros_like(acc_scr)

        def body(j, carry):
            start = pl.multiple_of(j * tk, tk)
            kt = k_ref[pl.ds(start, tk), :]
            vt = v_ref[pl.ds(start, tk), :]
            for h in range(group):
                q = q_ref[:, h * HEAD_DIM:(h + 1) * HEAD_DIM]
                s = lax.dot_general(q, kt, (((1,), (1,)), ((), ())), preferred_element_type=F32)
                m = m_scr[h]
                m_new = jnp.maximum(m, jnp.max(s, axis=1, keepdims=True))
                alpha = jnp.exp2((m - m_new) * c)
                p = jnp.exp2(s * c - m_new * c)
                l_scr[h] = alpha * l_scr[h] + jnp.sum(p, axis=1, keepdims=True)
                acc_scr[h] = alpha * acc_scr[h] + jnp.dot(p.astype(BF16), vt, preferred_element_type=F32)
                m_scr[h] = m_new
            return carry

        lax.fori_loop(0, seq_len // tk, body, 0)
        for h in range(group):
            o_ref[:, h * HEAD_DIM:(h + 1) * HEAD_DIM] = (acc_scr[h] / l_scr[h]).astype(o_ref.dtype)


def _flash_gqa(q, k, v, batch, seq_len, tq=1024, tk=512, tkf=1024):
    tq = min(tq, seq_len)
    tk = min(tk, seq_len)
    tkf = min(tkf, seq_len)
    nq = seq_len // tq
    gw = GA_GROUP * HEAD_DIM
    wide = pltpu.VMEM((GA_GROUP, tq, HEAD_DIM), F32)
    thin = pltpu.VMEM((GA_GROUP, tq, 1), F32)
    return pl.pallas_call(
        functools.partial(_flash_kernel, tq=tq, tk=tk, tkf=tkf, seq_len=seq_len, group=GA_GROUP),
        out_shape=jax.ShapeDtypeStruct(q.shape, BF16),
        grid=(batch, GA_KV_HEADS, nq),
        in_specs=[pl.BlockSpec((tq, gw), lambda b, kv, i: (b * nq + i, kv)),
                  pl.BlockSpec((seq_len, HEAD_DIM), lambda b, kv, i: (b, kv)),
                  pl.BlockSpec((seq_len, HEAD_DIM), lambda b, kv, i: (b, kv))],
        out_specs=pl.BlockSpec((tq, gw), lambda b, kv, i: (b * nq + i, kv)),
        scratch_shapes=[wide, wide, wide, thin, thin, pltpu.VMEM((HALO_ROWS, HEAD_DIM), F32)],
        compiler_params=_params("parallel", "parallel", "arbitrary"),
    )(q, k, v)


def _identity_prologue(o):
    return o


def _gqa_layer(x, batch, seq_len, g, p, route):
    cos, sin = _axial_tables(seq_len)
    q, k, v = _qkv_rope(x, g, p['w_qkv'], p['q_gain'], p['k_gain'], cos, sin,
                        GA_HEADS, GA_KV_HEADS, GA_KV_HEADS, HEAD_DIM // 4, seq_len)
    o = _flash_gqa(q, k, v, batch, seq_len)
    return _mm_res(_identity_prologue, [o], [], p['w_o'], None, x, route)


def _ml_in_kernel(x_ref, xp_ref, xn_ref, g_ref, w_ref, cw_ref, cb_ref, wq_ref, wk_ref, wv_ref,
                  wg_ref, bg_ref, q_ref, k_ref, v_ref, xc_ref, sz_ref, gate_ref,
                  *, seq_len, tm, inner, k_scale):
    keep_prev, keep_next = _edge_scales(pl.program_id(0), tm, seq_len)
    g = g_ref[...]
    xa = _rms(jnp.concatenate([x_ref[...], xp_ref[...], xn_ref[...]], axis=0), g).astype(BF16)
    xb = xa[:tm]
    rows = lax.broadcasted_iota(jnp.int32, (tm, 1), 0)
    gacc = jnp.zeros(gate_ref.shape, F32)
    for t in range(inner // MXU_DIM):
        sl = slice(t * MXU_DIM, (t + 1) * MXU_DIM)
        xma = jnp.dot(xa, w_ref[:, sl], preferred_element_type=F32)
        xm = xma[:tm]
        pprev = xma[tm + HALO_ROWS - 1:tm + HALO_ROWS, :] * keep_prev
        pnext = xma[tm + HALO_ROWS:tm + HALO_ROWS + 1, :] * keep_next
        xc = _conv3_rows(xm, pprev, pnext, cw_ref[:, sl], cb_ref[:, sl], rows, tm)
        xc = xc * jax.nn.sigmoid(xc)
        z = jnp.dot(xb, w_ref[:, inner + t * MXU_DIM:inner + (t + 1) * MXU_DIM],
                    preferred_element_type=F32)
        xcb = xc.astype(BF16)
        q = jnp.dot(xcb, wq_ref[t], preferred_element_type=F32)
        k = jnp.dot(xcb, wk_ref[t], preferred_element_type=F32)
        v = jnp.dot(xm.astype(BF16), wv_ref[t], preferred_element_type=F32)
        qb, kb, vb = q.astype(BF16), k.astype(BF16), v.astype(BF16)
        gacc += (jnp.dot(qb, wg_ref[0, sl, :], preferred_element_type=F32)
                 + jnp.dot(kb, wg_ref[1, sl, :], preferred_element_type=F32)
                 + jnp.dot(vb, wg_ref[2, sl, :], preferred_element_type=F32))
        q_ref[:, sl] = qb
        k_ref[:, sl] = (k * k_scale).astype(BF16)
        v_ref[:, sl] = vb
        xc_ref[:, sl] = xc
        sz_ref[:, sl] = z * jax.nn.sigmoid(z)
    gate_ref[...] = gacc + bg_ref[...]


def _block_diag_tiles(w):
    nb, c, _ = w.shape
    per = MXU_DIM // c
    wt = w.reshape(nb // per, per, c, c)
    eye = jnp.eye(per, dtype=w.dtype)
    full = jnp.einsum('tpcd,pq->tpcqd', wt, eye)
    return full.reshape(nb // per, MXU_DIM, MXU_DIM)


def _ml_in(x, g, p, seq_len, tm=512):
    n, d = x.shape
    tm = min(tm, seq_len)
    inner = p['w_up'].shape[1] // 2
    ng = 4 * ML_HEADS
    dh = inner // ML_HEADS
    wq, wk, wv = (_block_diag_tiles(p[nm]).astype(BF16) for nm in ('w_q', 'w_k', 'w_v'))
    wg = jnp.transpose(p['w_gate'], (1, 2, 0, 3)).reshape(3, inner, ng).astype(BF16)
    bg = p['b_gate'].reshape(1, ng)
    prev, nxt = _halo_specs(tm, d, n)
    row = pl.BlockSpec((tm, d), lambda i: (i, 0))
    wide = pl.BlockSpec((tm, inner), lambda i: (i, 0))
    nt = inner // MXU_DIM
    return pl.pallas_call(
        functools.partial(_ml_in_kernel, seq_len=seq_len, tm=tm, inner=inner, k_scale=dh ** -0.5),
        out_shape=(jax.ShapeDtypeStruct((n, inner), BF16),) * 3
        + (jax.ShapeDtypeStruct((n, inner), F32),) * 2
        + (jax.ShapeDtypeStruct((n, ng), F32),),
        grid=(n // tm,),
        in_specs=[row, prev, nxt, _const_spec((1, d)), _const_spec((d, 2 * inner), single=True),
                  _const_spec((3, inner)), _const_spec((1, inner)),
                  _const_spec((nt, MXU_DIM, MXU_DIM), single=True),
                  _const_spec((nt, MXU_DIM, MXU_DIM), single=True),
                  _const_spec((nt, MXU_DIM, MXU_DIM), single=True),
                  _const_spec((3, inner, ng), single=True), _const_spec((1, ng))],
        out_specs=(wide,) * 5 + (pl.BlockSpec((tm, ng), lambda i: (i, 0)),),
        compiler_params=_params("parallel"),
    )(x, x, x, g.reshape(1, d), p['w_up'].astype(BF16), p['conv_w'],
      p['conv_b'].reshape(1, inner), wq, wk, wv, wg, bg)


def _log_sigmoid(x):
    return jnp.minimum(x, 0.0) - jnp.log1p(jnp.exp(-jnp.abs(x)))


def _mlstm_chunk_kernel(qf_ref, kf_ref, vf_ref, gcf_ref, grf_ref, qb_ref, kb_ref, vb_ref, gcb_ref, grb_ref,
                        hf_ref, hb_ref, c_scr, n_scr, m_scr, *, lc):
    @pl.when(pl.program_id(2) == 0)
    def _():
        c_scr[...] = jnp.zeros_like(c_scr)
        n_scr[...] = jnp.zeros_like(n_scr)
        m_scr[...] = jnp.zeros_like(m_scr)

    _mlstm_step(qf_ref, kf_ref, vf_ref, gcf_ref, grf_ref, hf_ref, c_scr.at[0], n_scr.at[0], m_scr.at[0], lc, False)
    _mlstm_step(qb_ref, kb_ref, vb_ref, gcb_ref, grb_ref, hb_ref, c_scr.at[1], n_scr.at[1], m_scr.at[1], lc, True)


def _mlstm_step(q_ref, k_ref, v_ref, gc_ref, gr_ref, h_ref, c_scr, n_scr, m_scr, lc, backward):
    q = q_ref[...]
    k = k_ref[...]
    v = v_ref[...]
    gc = gc_ref[...]
    gr = gr_ref[...]
    i_col, g_col = gc[:, 0:1], gc[:, 1:2]
    i_row, g_row = gr[0:1, :], gr[1:2, :]

    jr = lax.broadcasted_iota(jnp.int32, (lc, lc), 0)
    sc = lax.broadcasted_iota(jnp.int32, (lc, lc), 1)
    seen = (sc >= jr) if backward else (sc <= jr)
    g_tot = g_row[:, 0:1] if backward else g_row[:, lc - 1:lc]
    m_old = m_scr[0:1, 0:1]

    dmat = jnp.where(seen, g_col - g_row + i_row, -jnp.inf)
    inter = g_col + m_old
    m_q = jnp.maximum(inter, jnp.max(dmat, axis=1, keepdims=True))
    s_qk = lax.dot_general(q, k, (((1,), (1,)), ((), ())), preferred_element_type=F32)
    a = s_qk * jnp.exp(dmat - m_q)
    w_int = jnp.exp(inter - m_q)
    q_c = jnp.dot(q, c_scr[...].astype(BF16), preferred_element_type=F32)
    num = jnp.dot(a.astype(BF16), v, preferred_element_type=F32) + q_c * w_int
    q_n = jnp.sum(q.astype(F32) * n_scr[...], axis=1, keepdims=True)
    den = jnp.sum(a, axis=1, keepdims=True) + w_int * q_n
    den = jnp.maximum(jnp.abs(den), jnp.exp(-m_q))
    h_ref[...] = num / den

    a_row = g_tot - g_row + i_row
    m_new = jnp.maximum(g_tot + m_old, jnp.max(a_row, axis=1, keepdims=True))
    ws_col = jnp.exp(g_tot - g_col + i_col - m_new)
    dec = jnp.exp(g_tot + m_old - m_new)
    kw = k.astype(F32) * ws_col
    upd = lax.dot_general(kw.astype(BF16), v, (((0,), (0,)), ((), ())), preferred_element_type=F32)
    c_scr[...] = dec * c_scr[...] + upd
    n_scr[...] = dec * n_scr[...] + jnp.sum(kw, axis=0, keepdims=True)
    m_scr[...] = jnp.broadcast_to(m_new, m_scr.shape)


def _ml_gate_kernel(g_ref, o_ref, *, lc, nh):
    g = g_ref[...]
    jr = lax.broadcasted_iota(jnp.int32, (lc, lc), 0)
    sc = lax.broadcasted_iota(jnp.int32, (lc, lc), 1)
    tril = jnp.where(sc <= jr, 1.0, 0.0).astype(BF16)
    triu = jnp.where(sc >= jr, 1.0, 0.0).astype(BF16)
    col = lax.broadcasted_iota(jnp.int32, (1, 4 * nh), 1)
    lf = _log_sigmoid(g)
    x1 = lf.astype(BF16)
    r1 = lf - x1.astype(F32)
    x2 = r1.astype(BF16)
    x3 = (r1 - x2.astype(F32)).astype(BF16)
    cum = lambda m: (jnp.dot(m, x1, preferred_element_type=F32) + jnp.dot(m, x2, preferred_element_type=F32)
                     + jnp.dot(m, x3, preferred_element_type=F32))
    gsum = jnp.where(col < 2 * nh, cum(tril), cum(triu))
    o_ref[...] = jnp.where((col % (2 * nh)) >= nh, gsum, g)


def _mlstm_chunks(q, k, v, gates, batch, seq_len):
    n, inner = q.shape
    dh = inner // ML_HEADS
    lc = min(ML_CHUNK, seq_len)
    nc = seq_len // lc
    ng = gates.shape[1]
    gates = pl.pallas_call(
        functools.partial(_ml_gate_kernel, lc=lc, nh=ML_HEADS),
        out_shape=jax.ShapeDtypeStruct((n, ng), F32),
        grid=(n // lc,),
        in_specs=[pl.BlockSpec((lc, ng), lambda i: (i, 0))],
        out_specs=pl.BlockSpec((lc, ng), lambda i: (i, 0)),
        compiler_params=_params("parallel"),
    )(gates)
    g4 = gates.reshape(n, 2, 2, ML_HEADS)
    gcol = jnp.transpose(g4, (1, 3, 0, 2))
    grow = jnp.transpose(g4, (1, 3, 2, 0))

    def specs(dd):
        chunk = (lambda b, c: b * nc + c) if dd == 0 else (lambda b, c: b * nc + nc - 1 - c)
        qkv = pl.BlockSpec((lc, dh), lambda b, h, c: (chunk(b, c), h))
        gc = pl.BlockSpec((None, None, lc, 2), lambda b, h, c: (dd, h, chunk(b, c), 0))
        gr = pl.BlockSpec((None, None, 2, lc), lambda b, h, c: (dd, h, 0, chunk(b, c)))
        return [qkv, qkv, qkv, gc, gr], qkv

    in_f, out_f = specs(0)
    in_b, out_b = specs(1)
    return pl.pallas_call(
        functools.partial(_mlstm_chunk_kernel, lc=lc),
        out_shape=(jax.ShapeDtypeStruct((n, inner), F32),) * 2,
        grid=(batch, ML_HEADS, nc),
        in_specs=in_f + in_b,
        out_specs=(out_f, out_b),
        scratch_shapes=[pltpu.VMEM((2, dh, dh), F32), pltpu.VMEM((2, 1, dh), F32),
                        pltpu.VMEM((2, HALO_ROWS, HEAD_DIM), F32)],
        compiler_params=_params("parallel", "parallel", "arbitrary"),
    )(q, k, v, gcol, grow, q, k, v, gcol, grow)


def _ml_out_prologue(hf, hb, xc, sz, gain, skip):
    h = hf + hb
    dh = h.shape[1] // ML_HEADS
    parts = []
    for i in range(ML_HEADS):
        sl = slice(i * dh, (i + 1) * dh)
        parts.append(_rms(h[:, sl], gain[:, sl]))
    hn = jnp.concatenate(parts, axis=1)
    return (hn + skip * xc) * sz


def _mlstm_layer(x, batch, seq_len, g, p, route):
    q, k, v, xc, sz, gates = _ml_in(x, g, p, seq_len)
    inner = q.shape[1]
    hf, hb = _mlstm_chunks(q, k, v, gates, batch, seq_len)
    return _mm_res(_ml_out_prologue, [hf, hb, xc, sz],
                   [p['norm_gain'].reshape(1, inner), p['skip'].reshape(1, inner)],
                   p['w_down'], None, x, route, tm=256)


def _rope_tables(L):
    inv = ROPE_THETA ** (-(2.0 * jnp.arange(ROPE_DIMS // 2, dtype=F32)) / ROPE_DIMS)
    ang = jnp.arange(L, dtype=F32)[:, None] * inv[None]
    pad = HEAD_DIM - ROPE_DIMS
    cos = jnp.concatenate([jnp.cos(ang), jnp.cos(ang), jnp.ones((L, pad), F32)], axis=-1)
    sin = jnp.concatenate([-jnp.sin(ang), jnp.sin(ang), jnp.zeros((L, pad), F32)], axis=-1)
    return cos, sin


def _band_kernel(q_ref, kp_ref, kc_ref, kn_ref, vp_ref, vc_ref, vn_ref, o_ref, lse_ref,
                 *, s_len, heads, subs):
    i = pl.program_id(2)
    qb = BAND_BLOCK
    w = qb + 2 * BAND_HALF
    a = lax.broadcasted_iota(jnp.int32, (qb, w), 0)
    c = lax.broadcasted_iota(jnp.int32, (qb, w), 1)
    in_band = jnp.abs(c - BAND_HALF - a) <= BAND_HALF
    scale = HEAD_DIM ** -0.5
    for h in range(heads):
        sl = slice(h * HEAD_DIM, (h + 1) * HEAD_DIM)
        kw = jnp.concatenate([kp_ref[qb - BAND_HALF:, sl], kc_ref[:, sl], kn_ref[:BAND_HALF, sl]], axis=0)
        vw = jnp.concatenate([vp_ref[qb - BAND_HALF:, sl], vc_ref[:, sl], vn_ref[:BAND_HALF, sl]], axis=0)
        for u in range(subs):
            rows = slice(u * qb, (u + 1) * qb)
            key_pos = (i * subs + u) * qb - BAND_HALF + c
            valid = in_band & (key_pos >= 0) & (key_pos < s_len)
            s = lax.dot_general(q_ref[rows, sl], kw[u * qb:u * qb + w], (((1,), (1,)), ((), ())),
                                preferred_element_type=F32) * scale
            s = jnp.where(valid, s, -jnp.inf)
            m = jnp.max(s, axis=1, keepdims=True)
            p = jnp.exp(s - m)
            l = jnp.sum(p, axis=1, keepdims=True)
            o = jnp.dot(p.astype(BF16), vw[u * qb:u * qb + w], preferred_element_type=F32)
            o_ref[rows, sl] = o / l
            lse_ref[rows, sl] = jnp.broadcast_to(m + jnp.log(l), (qb, HEAD_DIM))


def _da_qkv_kernel(x_ref, g_ref, w_ref, qg_ref, kg_ref, cos_ref, sin_ref, *refs, dils, tm, half):
    n_perm = sum(1 for d in dils if d > 1)
    perm_refs, out_refs = refs[:n_perm], refs[n_perm:]
    xb = _rms(x_ref[...], g_ref[...]).astype(BF16)
    cos, sin = cos_ref[...], sin_ref[...]
    gw = DA_HEADS_PER_GROUP * HEAD_DIM
    heads = DA_HEADS_PER_GROUP * len(dils)
    for kind in range(3):
        gain = (qg_ref, kg_ref, None)[kind]
        pi = 0
        for gi, dil in enumerate(dils):
            c0 = (kind * heads + gi * DA_HEADS_PER_GROUP) * HEAD_DIM
            ph = jnp.dot(xb, w_ref[:, c0:c0 + gw], preferred_element_type=F32)
            if gain is not None:
                ph = jnp.concatenate(
                    [_head_norm_rope(ph[:, h * HEAD_DIM:(h + 1) * HEAD_DIM], gain[...], cos, sin, half)
                     for h in range(DA_HEADS_PER_GROUP)], axis=1)
            val = ph.astype(BF16)
            o_ref = out_refs[kind * len(dils) + gi]
            if dil == 1:
                o_ref[...] = val
            else:
                pv = jnp.dot(perm_refs[pi][...], val, preferred_element_type=F32).astype(BF16)
                pi += 1
                rows = tm // dil
                for r in range(dil):
                    o_ref[:, r * gw:(r + 1) * gw] = pv[r * rows:(r + 1) * rows, :]


def _da_qkv(x, g, w_qkv, q_gain, k_gain, cos, sin, seq_len, tm=256):
    n, d = x.shape
    tm = min(tm, seq_len)
    pos_blocks = seq_len // tm
    dils = tuple(dil for _, dil in DA_GROUPS)
    gw = DA_HEADS_PER_GROUP * HEAD_DIM
    f = w_qkv.shape[1]
    perms = []
    for dil in dils:
        if dil > 1:
            perms.append(_dilation_perm(tm, dil))
    row = pl.BlockSpec((tm, d), lambda i: (i, 0))
    tab = pl.BlockSpec((tm, HEAD_DIM), lambda i: (i % pos_blocks, 0))
    out_shapes = tuple(jax.ShapeDtypeStruct((n // dil, dil * gw), BF16) for _ in range(3) for dil in dils)
    out_specs = tuple(pl.BlockSpec((tm // dil, dil * gw), lambda i: (i, 0)) for _ in range(3) for dil in dils)
    outs = pl.pallas_call(
        functools.partial(_da_qkv_kernel, dils=dils, tm=tm, half=ROPE_DIMS // 2),
        out_shape=out_shapes,
        grid=(n // tm,),
        in_specs=[row, _const_spec((1, d)), _const_spec((d, f), single=True), _const_spec((1, HEAD_DIM)),
                  _const_spec((1, HEAD_DIM)), tab, tab] + [_const_spec((tm, tm))] * len(perms),
        out_specs=out_specs,
        compiler_params=_params("parallel"),
    )(x, g.reshape(1, d), w_qkv.astype(BF16), q_gain.reshape(1, HEAD_DIM), k_gain.reshape(1, HEAD_DIM),
      cos, sin, *perms)
    ng = len(dils)
    return [(outs[gi], outs[ng + gi], outs[2 * ng + gi]) for gi in range(ng)]


def _band_attention(q, k, v, batch, seq_len, dil):
    n = batch * seq_len
    s_len = seq_len // dil
    nb = s_len // BAND_BLOCK
    subs = min(BAND_SUBBLOCKS, nb)
    hw = DA_HEADS_PER_GROUP * HEAD_DIM
    view = lambda a: a.reshape(batch, s_len, dil * hw)
    qv, kv, vv = view(q), view(k), view(v)

    main = pl.BlockSpec((None, subs * BAND_BLOCK, hw), lambda b, r, i: (b, i, r))
    prev = pl.BlockSpec((None, BAND_BLOCK, hw), lambda b, r, i: (b, jnp.maximum(i * subs - 1, 0), r))
    nxt = pl.BlockSpec((None, BAND_BLOCK, hw), lambda b, r, i: (b, jnp.minimum((i + 1) * subs, nb - 1), r))
    o, lse = pl.pallas_call(
        functools.partial(_band_kernel, s_len=s_len, heads=DA_HEADS_PER_GROUP, subs=subs),
        out_shape=(jax.ShapeDtypeStruct((batch, s_len, dil * hw), F32),) * 2,
        grid=(batch, dil, nb // subs),
        in_specs=[main, prev, main, nxt, prev, main, nxt],
        out_specs=(main, main),
        compiler_params=_params("parallel", "parallel", "parallel"),
    )(qv, kv, kv, kv, vv, vv, vv)
    return o.reshape(n // dil, dil * hw), lse.reshape(n // dil, dil * hw)


def _undilate(blk, perm_t):
    hw = DA_HEADS_PER_GROUP * HEAD_DIM
    dil = blk.shape[1] // hw
    stacked = jnp.concatenate([blk[:, r * hw:(r + 1) * hw] for r in range(dil)], axis=0)
    hi, lo = _split_bf16(stacked)
    return jnp.dot(perm_t, hi, preferred_element_type=F32) + jnp.dot(perm_t, lo, preferred_element_type=F32)


def _da_out_prologue(o0, o1, o2, l0, l1, l2, pt1, pt2):
    o1, l1 = _undilate(o1, pt1), _undilate(l1, pt1)
    o2, l2 = _undilate(o2, pt2), _undilate(l2, pt2)
    m = jnp.maximum(jnp.maximum(l0, l1), l2)
    e0, e1, e2 = jnp.exp(l0 - m), jnp.exp(l1 - m), jnp.exp(l2 - m)
    return (e0 * o0 + e1 * o1 + e2 * o2) / (e0 + e1 + e2)


def _dilation_perm(tm, dil):
    dst = jnp.arange(tm)
    src = (dst % (tm // dil)) * dil + dst // (tm // dil)
    return (src[:, None] == jnp.arange(tm)[None, :]).astype(BF16)


def _dilated_layer(x, batch, seq_len, g, p, route):
    cos, sin = _rope_tables(seq_len)
    qkv = _da_qkv(x, g, p['w_qkv'], p['q_gain'], p['k_gain'], cos, sin, seq_len)
    outs, lses = [], []
    for (q, k, v), (_, dil) in zip(qkv, DA_GROUPS):
        o, lse = _band_attention(q, k, v, batch, seq_len, dil)
        outs.append(o)
        lses.append(lse)
    tm = min(DA_OUT_ROWS, seq_len)
    perms_t = [_dilation_perm(tm, dil).T for _, dil in DA_GROUPS[1:]]
    return _mm_res(_da_out_prologue, outs + lses, perms_t, p['w_o'], None, x, route, tm=tm)


def _expert_ffn_kernel(xe_ref, gate_ref, w1_ref, w3_ref, w2_ref, ye_ref, w1b, w3b, w2b):
    @pl.when(pl.program_id(1) == 0)
    def _():
        w1b[...] = w1_ref[...].astype(BF16)
        w3b[...] = w3_ref[...].astype(BF16)
        w2b[...] = w2_ref[...].astype(BF16)

    xe = xe_ref[...]
    h1 = jnp.dot(xe, w1b[...], preferred_element_type=F32)
    h3 = jnp.dot(xe, w3b[...], preferred_element_type=F32)
    hid = (h1 * jax.nn.sigmoid(h1) * h3).astype(BF16)
    tm = xe.shape[0]
    diag = (lax.broadcasted_iota(jnp.int32, (tm, tm), 0) == lax.broadcasted_iota(jnp.int32, (tm, tm), 1))
    gate = jnp.sum(jnp.where(diag, gate_ref[...], 0.0), axis=1, keepdims=True)
    ye_ref[...] = (jnp.dot(hid, w2b[...], preferred_element_type=F32) * gate).astype(ye_ref.dtype)


def _expert_ffn(xe, gates, w1, w3, w2, layer, tm=512):
    e, c, d = xe.shape
    f = w1.shape[3]
    tm = min(tm, c)
    return pl.pallas_call(
        _expert_ffn_kernel,
        out_shape=jax.ShapeDtypeStruct((e, c, d), BF16),
        grid=(e, c // tm),
        in_specs=[pl.BlockSpec((None, tm, d), lambda ei, ci: (ei, ci, 0)),
                  pl.BlockSpec((None, 1, tm), lambda ei, ci: (ei, 0, ci)),
                  pl.BlockSpec((None, None, d, f), lambda ei, ci: (layer, ei, 0, 0)),
                  pl.BlockSpec((None, None, d, f), lambda ei, ci: (layer, ei, 0, 0)),
                  pl.BlockSpec((None, None, f, d), lambda ei, ci: (layer, ei, 0, 0))],
        out_specs=pl.BlockSpec((None, tm, d), lambda ei, ci: (ei, ci, 0)),
        scratch_shapes=[pltpu.VMEM((d, f), BF16), pltpu.VMEM((d, f), BF16), pltpu.VMEM((f, d), BF16)],
        compiler_params=_params("parallel", "arbitrary"),
    )(xe, gates[:, None, :], w1, w3, w2)


def _moe_layer(x, xn, aff_t, group_sizes, w1, w3, w2, layer, split_output):
    n, d = x.shape
    gts, idxs = [], []
    start = 0
    for ng in group_sizes:
        cap = EC_CAPACITY * ng // N_EXPERTS
        gates, idx = lax.top_k(aff_t[:, start:start + ng], cap)
        gts.append(gates)
        idxs.append(idx + start)
        start += ng
    idx_all = jnp.concatenate(idxs, axis=1)
    ye = _expert_ffn(xn[idx_all], jnp.concatenate(gts, axis=1), w1, w3, w2, layer)
    return _combine(x, ye.reshape(-1, d), idx_all.reshape(-1), group_sizes if split_output else (n,))


COMBINE_TOKENS = 512
COMBINE_ROWS = 512


def _combine_kernel(tile_ref, blk_ref, live_ref, x_ref, tok_ref, ye_ref, *o_refs, split_tiles):
    w = pl.program_id(0)
    tile = tile_ref[w]
    first = jnp.logical_or(w == 0, tile != tile_ref[jnp.maximum(w - 1, 0)])
    tt = x_ref.shape[0]
    rows = lax.broadcasted_iota(jnp.int32, (tt, COMBINE_ROWS), 0)
    onehot = jnp.where(rows == tok_ref[...] - tile * tt, 1.0, 0.0).astype(BF16)
    add = jnp.dot(onehot, ye_ref[...], preferred_element_type=F32) * live_ref[w].astype(F32)
    lo_tile = 0
    for o_ref, n_tiles in zip(o_refs, split_tiles):
        mine = jnp.logical_and(tile >= lo_tile, tile < lo_tile + n_tiles)

        @pl.when(jnp.logical_and(mine, first))
        def _(o_ref=o_ref):
            o_ref[...] = x_ref[...] + add

        @pl.when(jnp.logical_and(mine, jnp.logical_not(first)))
        def _(o_ref=o_ref):
            o_ref[...] += add

        lo_tile += n_tiles


def _combine(x, ye, tok, splits):
    n, d = x.shape
    p = tok.shape[0]
    tt = min(COMBINE_TOKENS, min(splits))
    rb = COMBINE_ROWS
    assert p % rb == 0 and all(s % tt == 0 for s in splits) and sum(splits) == n
    tiles, nblk = n // tt, p // rb
    split_tiles = tuple(s // tt for s in splits)
    order = jnp.argsort(tok)
    tok_sorted = tok[order].astype(jnp.int32)
    ye_sorted = ye[order]

    edges = jnp.arange(tiles + 1, dtype=jnp.int32) * tt
    bounds = jnp.searchsorted(tok_sorted, edges, method='compare_all').astype(jnp.int32)
    lo, hi = bounds[:-1], bounds[1:]
    first_blk = jnp.minimum(lo // rb, nblk - 1)
    last_blk = jnp.where(hi > lo, (hi - 1) // rb, first_blk)
    n_items = last_blk - first_blk + 1
    item_end = jnp.cumsum(n_items)
    item_start = item_end - n_items
    max_items = nblk + 2 * tiles
    w = jnp.arange(max_items, dtype=jnp.int32)
    tile = jnp.minimum(jnp.searchsorted(item_end, w, side='right', method='compare_all'),
                       tiles - 1).astype(jnp.int32)
    k = w - item_start[tile]
    live = (k < n_items[tile]).astype(jnp.int32)
    blk = jnp.minimum(first_blk[tile] + k, nblk - 1).astype(jnp.int32)

    def out_spec(lo_tile, n_tiles):
        return pl.BlockSpec(
            (tt, d), lambda i, tile_r, blk_r, live_r: (jnp.clip(tile_r[i] - lo_tile, 0, n_tiles - 1), 0))

    starts = [sum(split_tiles[:j]) for j in range(len(splits))]
    grid_spec = pltpu.PrefetchScalarGridSpec(
        num_scalar_prefetch=3,
        grid=(max_items,),
        in_specs=[pl.BlockSpec((tt, d), lambda i, tile_r, blk_r, live_r: (tile_r[i], 0)),
                  pl.BlockSpec((None, 1, rb), lambda i, tile_r, blk_r, live_r: (blk_r[i], 0, 0)),
                  pl.BlockSpec((rb, d), lambda i, tile_r, blk_r, live_r: (blk_r[i], 0))],
        out_specs=tuple(out_spec(s, t) for s, t in zip(starts, split_tiles)),
    )
    return pl.pallas_call(
        functools.partial(_combine_kernel, split_tiles=split_tiles),
        out_shape=tuple(jax.ShapeDtypeStruct((s, d), F32) for s in splits),
        grid_spec=grid_spec,
        compiler_params=_params("arbitrary"),
    )(tile, blk, live, x, tok_sorted.reshape(nblk, 1, rb), ye_sorted)


def _trunk(x, batch, seq_len, group_sizes, p):
    depth = p['norm_gain'].shape[0]
    layers = ((_hyena_layer, 'hy_'), (_gqa_layer, 'ga_'), (_mlstm_layer, 'ml_'), (_dilated_layer, 'da_'))
    for i in range(depth):
        layer, prefix = layers[i % 4]
        lp = {k[len(prefix):]: v[i // 4] for k, v in p.items() if k.startswith(prefix)}
        route = (p['norm_gain'][i, 1], p['moe_w_router'][i])
        x, xn, aff_t = layer(x, batch, seq_len, p['norm_gain'][i, 0], lp, route)
        last = i == depth - 1
        outs = _moe_layer(x, xn, aff_t, group_sizes, p['moe_w1'], p['moe_w3'], p['moe_w2'], i,
                          split_output=last)
        x = outs if last else outs[0]
    return x


def kernel(x_prompt, x_sample, norm_gain, hy_w_in, hy_b_in, hy_conv_w, hy_conv_b, hy_f_w1, hy_f_b1, hy_f_w2, hy_f_b2, hy_f_w3, hy_f_b3, hy_f_freq, hy_decay, hy_skip, hy_w_out, hy_b_out, ga_w_qkv, ga_q_gain, ga_k_gain, ga_w_o, ml_w_up, ml_conv_w, ml_conv_b, ml_w_q, ml_w_k, ml_w_v, ml_w_gate, ml_b_gate, ml_norm_gain, ml_skip, ml_w_down, da_w_qkv, da_q_gain, da_k_gain, da_w_o, moe_w_router, moe_w1, moe_w3, moe_w2):
    p = dict(
        norm_gain=norm_gain,
        hy_w_in=hy_w_in, hy_b_in=hy_b_in, hy_conv_w=hy_conv_w, hy_conv_b=hy_conv_b,
        hy_f_w1=hy_f_w1, hy_f_b1=hy_f_b1, hy_f_w2=hy_f_w2, hy_f_b2=hy_f_b2,
        hy_f_w3=hy_f_w3, hy_f_b3=hy_f_b3, hy_f_freq=hy_f_freq, hy_decay=hy_decay,
        hy_skip=hy_skip, hy_w_out=hy_w_out, hy_b_out=hy_b_out,
        ga_w_qkv=ga_w_qkv, ga_q_gain=ga_q_gain, ga_k_gain=ga_k_gain, ga_w_o=ga_w_o,
        ml_w_up=ml_w_up, ml_conv_w=ml_conv_w, ml_conv_b=ml_conv_b, ml_w_q=ml_w_q,
        ml_w_k=ml_w_k, ml_w_v=ml_w_v, ml_w_gate=ml_w_gate, ml_b_gate=ml_b_gate,
        ml_norm_gain=ml_norm_gain, ml_skip=ml_skip, ml_w_down=ml_w_down,
        da_w_qkv=da_w_qkv, da_q_gain=da_q_gain, da_k_gain=da_k_gain, da_w_o=da_w_o,
        moe_w_router=moe_w_router, moe_w1=moe_w1, moe_w3=moe_w3, moe_w2=moe_w2,
    )
    bp, seq_len, d = x_prompt.shape
    bs = x_sample.shape[0]
    assert x_sample.shape[1] == seq_len
    x = jnp.concatenate([x_prompt, x_sample], axis=0).reshape((bp + bs) * seq_len, d)
    y_prompt, y_sample = _trunk(x, bp + bs, seq_len, (bp * seq_len, bs * seq_len), p)
    return (y_prompt.reshape(bp, seq_len, d), y_sample.reshape(bs, seq_len, d))
```

```python
import functools
import math

import jax
import jax.numpy as jnp
from jax import lax
from jax.experimental import pallas as pl
from jax.experimental.pallas import tpu as pltpu

F32 = jnp.float32
BF16 = jnp.bfloat16
HIGHEST = lax.Precision.HIGHEST

NORM_EPS = 1e-6
GRID_W = 64
HY_BANDS = 16
GA_HEADS = 8
GA_KV_HEADS = 2
GA_GROUP = GA_HEADS // GA_KV_HEADS
HEAD_DIM = 128
AXIAL_THETA = 10000.0
ML_HEADS = 4
ML_QKV_BLOCK = 4
DA_GROUPS = ((128, 1), (512, 4), (2048, 16))
DA_HEADS_PER_GROUP = 4
DA_HEADS = DA_HEADS_PER_GROUP * len(DA_GROUPS)
ROPE_THETA = 500000.0
ROPE_DIMS = HEAD_DIM // 4
N_EXPERTS = 16
EC_CAPACITY = 2

VMEM_LIMIT_BYTES = 52 * 1024 * 1024
HALO_ROWS = 8
MXU_DIM = 256
ML_CHUNK = 256
BAND_BLOCK = 128
BAND_HALF = 64
BAND_SUBBLOCKS = 4
DA_OUT_ROWS = 256


def _params(*sem):
    return pltpu.CompilerParams(dimension_semantics=sem, vmem_limit_bytes=VMEM_LIMIT_BYTES)


def _rms(x, g):
    ms = jnp.mean(x * x, axis=-1, keepdims=True)
    return x * lax.rsqrt(ms + NORM_EPS) * g


def _const_spec(shape, single=False):
    nd = len(shape)
    if single:
        return pl.BlockSpec(shape, lambda *_: (0,) * nd, pipeline_mode=pl.Buffered(1))
    return pl.BlockSpec(shape, lambda *_: (0,) * nd)


def _conv3_rows(p, pprev, pnext, cw, cb, rows, tm):
    up = jnp.where(rows == 0, pprev, pltpu.roll(p, 1, 0))
    dn = jnp.where(rows == tm - 1, pnext, pltpu.roll(p, tm - 1, 0))
    return up * cw[0:1] + p * cw[1:2] + dn * cw[2:3] + cb


def _halo_specs(tm, d, n_rows):
    hb = tm // HALO_ROWS
    last = n_rows // HALO_ROWS - 1
    prev = pl.BlockSpec((HALO_ROWS, d), lambda i: (jnp.maximum(i * hb - 1, 0), 0))
    nxt = pl.BlockSpec((HALO_ROWS, d), lambda i: (jnp.minimum((i + 1) * hb, last), 0))
    return prev, nxt


def _edge_scales(i, tm, seq_len):
    t0 = i * tm
    keep_prev = jnp.where(t0 % seq_len == 0, 0.0, 1.0).astype(F32)
    keep_next = jnp.where((t0 + tm) % seq_len == 0, 0.0, 1.0).astype(F32)
    return keep_prev, keep_next


def _route(x, g_ref, wr_ref, xn_ref, aff_ref):
    xn = _rms(x, g_ref[...])
    xn_ref[...] = xn.astype(BF16)
    logits = lax.dot_general(wr_ref[...], xn, (((1,), (1,)), ((), ())),
                             precision=HIGHEST, preferred_element_type=F32)
    m = jnp.max(logits, axis=0, keepdims=True)
    e = jnp.exp(logits - m)
    aff_ref[...] = e / jnp.sum(e, axis=0, keepdims=True)


def _mm_res_kernel(*refs, prologue, n_row, n_const, tn):
    row_refs = refs[:n_row]
    const_refs = refs[n_row:n_row + n_const]
    w_ref, b_ref, res_ref, g2_ref, wr_ref, o_ref, xn_ref, aff_ref = refs[n_row + n_const:]
    lhs = prologue(*[r[...] for r in row_refs], *[c[...] for c in const_refs]).astype(BF16)
    for j in range(o_ref.shape[1] // tn):
        sl = slice(j * tn, (j + 1) * tn)
        o_ref[:, sl] = (res_ref[:, sl] + b_ref[:, sl]
                        + jnp.dot(lhs, w_ref[:, sl], preferred_element_type=F32))
    _route(o_ref[...], g2_ref, wr_ref, xn_ref, aff_ref)


def _mm_res(prologue, rows, consts, w, b, res, route, tm=512, tn=512):
    n, dout = res.shape
    tm = min(tm, n)
    k = w.shape[0]
    g2, w_router = route
    ne = w_router.shape[1]
    if b is None:
        b = jnp.zeros((1, dout), F32)
    in_specs = [pl.BlockSpec((tm * r.shape[0] // n, r.shape[1]), lambda i: (i, 0)) if r.ndim == 2
                else pl.BlockSpec((r.shape[0], tm, r.shape[2]), lambda i: (0, i, 0)) for r in rows]
    in_specs += [_const_spec(c.shape) for c in consts]
    in_specs += [_const_spec((k, dout)), _const_spec((1, dout)),
                 pl.BlockSpec((tm, dout), lambda i: (i, 0)),
                 _const_spec((1, dout)), _const_spec((ne, dout))]
    tile = pl.BlockSpec((tm, dout), lambda i: (i, 0))
    return pl.pallas_call(
        functools.partial(_mm_res_kernel, prologue=prologue, n_row=len(rows),
                          n_const=len(consts), tn=min(tn, dout)),
        out_shape=(jax.ShapeDtypeStruct((n, dout), F32), jax.ShapeDtypeStruct((n, dout), BF16),
                   jax.ShapeDtypeStruct((ne, n), F32)),
        grid=(n // tm,),
        in_specs=in_specs,
        out_specs=(tile, tile, pl.BlockSpec((ne, tm), lambda i: (0, i))),
        compiler_params=_params("parallel"),
    )(*rows, *consts, w.astype(BF16), b.reshape(1, dout).astype(F32), res,
      g2.reshape(1, dout), w_router.T)


def _hyena_in_kernel(x_ref, xp_ref, xn_ref, g_ref, w_ref, b_ref, cw_ref, cb_ref,
                     x0_ref, vx_ref, *, seq_len, tm, d, cols):
    keep_prev, keep_next = _edge_scales(pl.program_id(0), tm, seq_len)
    g = g_ref[...]
    xb = _rms(jnp.concatenate([x_ref[...], xp_ref[...], xn_ref[...]], axis=0), g).astype(BF16)
    rows = lax.broadcasted_iota(jnp.int32, (tm, 1), 0)

    def conv_part(c0):
        sl = slice(c0, c0 + cols)
        pa = jnp.dot(xb, w_ref[:, sl], preferred_element_type=F32) + b_ref[:, sl]
        p = pa[:tm]
        pprev = pa[tm + HALO_ROWS - 1:tm + HALO_ROWS, :] * keep_prev
        pnext = pa[tm + HALO_ROWS:tm + HALO_ROWS + 1, :] * keep_next
        return _conv3_rows(p, pprev, pnext, cw_ref[:, sl], cb_ref[:, sl], rows, tm)

    for j in range(d // cols):
        c = j * cols
        x0_ref[:, c:c + cols] = conv_part(c)
        vx_ref[:, c:c + cols] = conv_part(2 * d + c) * conv_part(d + c)


def _hyena_in(x, g, w_in, b_in, conv_w, conv_b, seq_len, tm=512, cols=512):
    n, d = x.shape
    tm = min(tm, seq_len)
    prev, nxt = _halo_specs(tm, d, n)
    row = pl.BlockSpec((tm, d), lambda i: (i, 0))
    return pl.pallas_call(
        functools.partial(_hyena_in_kernel, seq_len=seq_len, tm=tm, d=d, cols=cols),
        out_shape=(jax.ShapeDtypeStruct((n, d), F32), jax.ShapeDtypeStruct((n, d), F32)),
        grid=(n // tm,),
        in_specs=[row, prev, nxt, _const_spec((1, d)), _const_spec((d, 3 * d)),
                  _const_spec((1, 3 * d)), _const_spec((3, 3 * d)), _const_spec((1, 3 * d))],
        out_specs=(row, row),
        compiler_params=_params("parallel"),
    )(x, x, x, g.reshape(1, d), w_in.astype(BF16), b_in.reshape(1, 3 * d),
      conv_w, conv_b.reshape(1, 3 * d))


def _hyena_filter_taps(L, d, f_w1, f_b1, f_w2, f_b2, f_w3, f_b3, f_freq, decay):
    t = jnp.linspace(0.0, 1.0, L, dtype=F32)[:, None]
    w_ang = 2.0 * math.pi * jnp.arange(L, dtype=F32)[:, None] / L
    bands = jnp.linspace(1e-4, HY_BANDS - 1, HY_BANDS, dtype=F32)[None, :]
    z = jnp.concatenate([t, jnp.cos(bands * w_ang), -jnp.sin(bands * w_ang)], axis=-1)

    def branch(zz, tt, col0):
        h = jnp.sin(f_freq[0] * (zz @ f_w1 + f_b1))
        h = jnp.sin(f_freq[1] * (h @ f_w2 + f_b2))
        h = h @ f_w3[:, col0:col0 + d] + f_b3[col0:col0 + d]
        return h * jnp.exp(-tt * jnp.abs(decay[col0 // d])[None])

    h_fwd = branch(z, t, 0)
    h_bwd_rev = branch(z[::-1], t[::-1], d)
    k = jnp.concatenate([h_fwd, jnp.zeros((1, d), F32), h_bwd_rev[:-1]], axis=0)
    return k / jnp.sum(jnp.abs(k), axis=0, keepdims=True)


def _fft_dims(m):
    lg = m.bit_length() - 1
    p = 1 << ((lg + 1) // 2)
    return p, m // p


def _split_bf16(x):
    hi = x.astype(BF16)
    return hi, (x - hi.astype(F32)).astype(BF16)


def _mm_split(fh, fl, x, precise):
    if not precise:
        return jnp.dot(fh, x.astype(BF16), preferred_element_type=F32)
    xh, xl = _split_bf16(x)
    return (jnp.dot(fh, xh, preferred_element_type=F32) + jnp.dot(fl, xh, preferred_element_type=F32)
            + jnp.dot(fh, xl, preferred_element_type=F32))


def _cplx_as_real(cr, ci):
    top = jnp.concatenate([cr, -ci], axis=-1)
    bot = jnp.concatenate([ci, cr], axis=-1)
    return jnp.concatenate([top, bot], axis=-2)


def _unit_circle(idx, m):
    ang = (2.0 * math.pi / m) * idx.astype(F32)
    return jnp.cos(ang), jnp.sin(ang)


def _dft_consts(p, q):
    m = p * q
    k1 = jnp.arange(p, dtype=jnp.int32)
    n1 = jnp.arange(p // 2, dtype=jnp.int32)
    c, s = _unit_circle((k1[:, None] * n1[None, :]) % p, p)
    fa = _cplx_as_real(c, -s)
    c, s = _unit_circle((n1[:, None] * k1[None, :]) % p, p)
    fd = _cplx_as_real(c / m, s / m)
    k2 = jnp.arange(q, dtype=jnp.int32)
    n2 = jnp.arange(q, dtype=jnp.int32)
    idx = (n2[None, None, :] * (k2[None, :, None] * p + k1[:, None, None])) % m
    c, s = _unit_circle(idx, m)
    gb = _cplx_as_real(c, -s)
    ct, st = jnp.swapaxes(c, 1, 2), jnp.swapaxes(s, 1, 2)
    gc = _cplx_as_real(ct, st)
    return tuple(_split_bf16(a) for a in (fa, gb, gc, fd))


def _fft_a_kernel(x_ref, fh_ref, fl_ref, o_ref, *, precise):
    _, rows_in, group, d = x_ref.shape
    rows_out = o_ref.shape[1]
    for j in range(group):
        x = x_ref[:, :, j, :].reshape(2 * rows_in, d)
        y = _mm_split(fh_ref[...], fl_ref[...], x, precise)
        o_ref[:, :, j, :] = y.reshape(2, rows_out, d)


def _fft_b_kernel(a_ref, gh_ref, gl_ref, o_ref):
    _, q, d = a_ref.shape
    x = _mm_split(gh_ref[...], gl_ref[...], a_ref[...].reshape(2 * q, d), True)
    o_ref[...] = x.reshape(o_ref.shape)


def _fft_bc_kernel(a_ref, gb_ref, gc_ref, k_ref, z_ref):
    _, q, d = a_ref.shape
    x = _mm_split(gb_ref[...], None, a_ref[...].reshape(2 * q, d), False)
    xr, xi = x[:q], x[q:]
    kr, ki = k_ref[0], k_ref[1]
    y = jnp.concatenate([xr * kr - xi * ki, xr * ki + xi * kr], axis=0)
    z = _mm_split(gc_ref[...], None, y, False)
    z_ref[...] = z.reshape(z_ref.shape)


def _fft_rows(x5, f, rows_out, precise):
    pairs, _, rows_in, q, d = x5.shape
    blk = lambda r: pl.BlockSpec((None, 2, r, HALO_ROWS, d), lambda b, j: (b, 0, 0, j, 0))
    return pl.pallas_call(
        functools.partial(_fft_a_kernel, precise=precise),
        out_shape=jax.ShapeDtypeStruct((pairs, 2, rows_out, q, d), F32),
        grid=(pairs, q // HALO_ROWS),
        in_specs=[blk(rows_in), _const_spec(f[0].shape), _const_spec(f[1].shape)],
        out_specs=blk(rows_out),
        compiler_params=_params("parallel", "parallel"),
    )(x5, *f)


def _long_conv(vx, taps, batch, seq_len):
    n, d = vx.shape
    m = 2 * seq_len
    p, q = _fft_dims(m)
    assert batch % 2 == 0
    pairs = batch // 2
    fa, gb, gc, fd = _dft_consts(p, q)
    g_spec = pl.BlockSpec((None, 2 * q, 2 * q), lambda k1, b: (k1, 0, 0))
    slab = pl.BlockSpec((None, 2, q, d), lambda k1, b: (b, 0, k1, 0))

    zeros = jnp.zeros((seq_len, d), F32)
    kin = jnp.stack([taps[:seq_len], zeros, taps[seq_len:], zeros]).reshape(2, 2, p // 2, q, d)
    ka = _fft_rows(kin, fa, p, True).reshape(2, 2, p * q, d)
    kx = pl.pallas_call(
        _fft_b_kernel,
        out_shape=jax.ShapeDtypeStruct((2, 2, p * q, d), F32),
        grid=(p, 2),
        in_specs=[slab, g_spec, g_spec],
        out_specs=slab,
        compiler_params=_params("parallel", "parallel"),
    )(ka, *gb)
    sign = jnp.repeat(1.0 - 2.0 * (jnp.arange(p) % 2).astype(F32), q)[None, :, None]
    kspec = kx[0] + sign * kx[1]

    xa = _fft_rows(vx.reshape(pairs, 2, p // 2, q, d), fa, p, False).reshape(pairs, 2, p * q, d)
    z = pl.pallas_call(
        _fft_bc_kernel,
        out_shape=jax.ShapeDtypeStruct((pairs, 2, p * q, d), F32),
        grid=(p, pairs),
        in_specs=[slab, g_spec, g_spec, pl.BlockSpec((2, q, d), lambda k1, b: (0, k1, 0))],
        out_specs=slab,
        compiler_params=_params("parallel", "parallel"),
    )(xa, gb[0], gc[0], kspec)
    y = _fft_rows(z.reshape(pairs, 2, p, q, d), fd, p // 2, False)
    return y.reshape(n, d)


def _hyena_out_prologue(y, vx, x0, skip):
    return (y + vx * skip) * x0


def _hyena_layer(x, batch, seq_len, g, p, route):
    n, d = x.shape
    x0, vx = _hyena_in(x, g, p['w_in'], p['b_in'], p['conv_w'], p['conv_b'], seq_len)
    taps = _hyena_filter_taps(seq_len, d, p['f_w1'], p['f_b1'], p['f_w2'], p['f_b2'],
                              p['f_w3'], p['f_b3'], p['f_freq'], p['decay'])
    y = _long_conv(vx, taps, batch, seq_len)
    return _mm_res(_hyena_out_prologue, [y, vx, x0], [p['skip'].reshape(1, d)],
                   p['w_out'], p['b_out'], x, route)


def _head_norm_rope(xh, gain, cos, sin, half):
    y = _rms(xh, gain)
    lane = lax.broadcasted_iota(jnp.int32, (1, HEAD_DIM), 1)
    fwd = pltpu.roll(y, HEAD_DIM - half, 1)
    bwd = pltpu.roll(y, half, 1)
    partner = jnp.where((lane % (2 * half)) < half, fwd, bwd)
    return y * cos + partner * sin


def _qkv_rope_kernel(x_ref, g_ref, w_ref, qg_ref, kg_ref, cos_ref, sin_ref,
                     q_ref, k_ref, v_ref, *, nq, nk, nv, half):
    xb = _rms(x_ref[...], g_ref[...]).astype(BF16)
    cos = cos_ref[...]
    sin = sin_ref[...]
    per = MXU_DIM // HEAD_DIM
    for h0 in range(0, nq + nk + nv, per):
        pw = jnp.dot(xb, w_ref[:, h0 * HEAD_DIM:(h0 + per) * HEAD_DIM], preferred_element_type=F32)
        for h in range(h0, h0 + per):
            ph = pw[:, (h - h0) * HEAD_DIM:(h - h0 + 1) * HEAD_DIM]
            if h < nq:
                sl = slice(h * HEAD_DIM, (h + 1) * HEAD_DIM)
                q_ref[:, sl] = _head_norm_rope(ph, qg_ref[...], cos, sin, half).astype(BF16)
            elif h < nq + nk:
                sl = slice((h - nq) * HEAD_DIM, (h - nq + 1) * HEAD_DIM)
                k_ref[:, sl] = _head_norm_rope(ph, kg_ref[...], cos, sin, half).astype(BF16)
            else:
                sl = slice((h - nq - nk) * HEAD_DIM, (h - nq - nk + 1) * HEAD_DIM)
                v_ref[:, sl] = ph.astype(BF16)


def _qkv_rope(x, g, w_qkv, q_gain, k_gain, cos, sin, nq, nk, nv, half, seq_len, tm=512):
    n, d = x.shape
    tm = min(tm, seq_len)
    pos_blocks = seq_len // tm
    f = w_qkv.shape[1]
    row = pl.BlockSpec((tm, d), lambda i: (i, 0))
    tab = pl.BlockSpec((tm, HEAD_DIM), lambda i: (i % pos_blocks, 0))
    outs = tuple(jax.ShapeDtypeStruct((n, c * HEAD_DIM), BF16) for c in (nq, nk, nv))
    return pl.pallas_call(
        functools.partial(_qkv_rope_kernel, nq=nq, nk=nk, nv=nv, half=half),
        out_shape=outs,
        grid=(n // tm,),
        in_specs=[row, _const_spec((1, d)), _const_spec((d, f)),
                  _const_spec((1, HEAD_DIM)), _const_spec((1, HEAD_DIM)), tab, tab],
        out_specs=tuple(pl.BlockSpec((tm, c * HEAD_DIM), lambda i: (i, 0)) for c in (nq, nk, nv)),
        compiler_params=_params("parallel"),
    )(x, g.reshape(1, d), w_qkv.astype(BF16), q_gain.reshape(1, HEAD_DIM),
      k_gain.reshape(1, HEAD_DIM), cos, sin)


def _axial_tables(L):
    t = jnp.arange(L)
    r = (t // GRID_W).astype(F32)
    c = (t % GRID_W).astype(F32)
    nf = HEAD_DIM // 4
    inv = AXIAL_THETA ** (-(2.0 * jnp.arange(nf, dtype=F32)) / (2 * nf))
    ar, ac = r[:, None] * inv[None], c[:, None] * inv[None]
    cos = jnp.concatenate([jnp.cos(ar), jnp.cos(ar), jnp.cos(ac), jnp.cos(ac)], axis=-1)
    sin = jnp.concatenate([-jnp.sin(ar), jnp.sin(ar), -jnp.sin(ac), jnp.sin(ac)], axis=-1)
    return cos, sin


FLASH_SAFE_BOUND = 40.0
FLASH_BOUND_MARGIN = 1.001


def _flash_kernel(q_ref, k_ref, v_ref, o_ref, acc_scr, off_scr, lsum_scr, m_scr, l_scr, kn_scr,
                  *, tq, tk, tkf, seq_len, group):
    scale = HEAD_DIM ** -0.5
    c = scale * math.log2(math.e)

    @pl.when(pl.program_id(2) == 0)
    def _():
        def norm_step(j, mx):
            kt = k_ref[pl.ds(pl.multiple_of(j * tk, tk), tk), :].astype(F32)
            row = jnp.sum(kt * kt, axis=1, keepdims=True)
            return jnp.maximum(mx, jnp.max(row, axis=0, keepdims=True))

        kmax2 = lax.fori_loop(0, seq_len // tk, norm_step, jnp.zeros((1, 1), F32))
        kn_scr[...] = jnp.broadcast_to(kmax2, kn_scr.shape)

    kmax2 = kn_scr[0:1, 0:1]
    bmax = jnp.zeros((1, 1), F32)
    for h in range(group):
        qf = q_ref[:, h * HEAD_DIM:(h + 1) * HEAD_DIM].astype(F32)
        b = jnp.sqrt(jnp.sum(qf * qf, axis=1, keepdims=True) * kmax2) * FLASH_BOUND_MARGIN
        bmax = jnp.maximum(bmax, jnp.max(b, axis=0, keepdims=True))
        off_scr[h] = jnp.broadcast_to(b * c, (tq, HEAD_DIM))
    fast = (bmax * scale)[0, 0] <= FLASH_SAFE_BOUND

    @pl.when(fast)
    def _():
        acc_scr[...] = jnp.zeros_like(acc_scr)
        lsum_scr[...] = jnp.zeros_like(lsum_scr)
        n_lane_tiles = tkf // HEAD_DIM

        def body(j, carry):
            start = pl.multiple_of(j * tkf, tkf)
            kt = k_ref[pl.ds(start, tkf), :]
            vt = v_ref[pl.ds(start, tkf), :]
            for h in range(group):
                q = q_ref[:, h * HEAD_DIM:(h + 1) * HEAD_DIM]
                s = lax.dot_general(q, kt, (((1,), (1,)), ((), ())), preferred_element_type=F32)
                off = off_scr[h]
                p = jnp.exp2(s * c - jnp.concatenate([off] * n_lane_tiles, axis=1))
                part = p[:, 0:HEAD_DIM]
                for t in range(1, n_lane_tiles):
                    part = part + p[:, t * HEAD_DIM:(t + 1) * HEAD_DIM]
                lsum_scr[h] += part
                acc_scr[h] += jnp.dot(p.astype(BF16), vt, preferred_element_type=F32)
            return carry

        lax.fori_loop(0, seq_len // tkf, body, 0)
        for h in range(group):
            l = jnp.sum(lsum_scr[h], axis=1, keepdims=True)
            o_ref[:, h * HEAD_DIM:(h + 1) * HEAD_DIM] = (acc_scr[h] / l).astype(o_ref.dtype)

    @pl.when(jnp.logical_not(fast))
    def _():
        m_scr[...] = jnp.full(m_scr.shape, -jnp.inf, F32)
        l_scr[...] = jnp.zeros_like(l_scr)
        acc_scr[...] = jnp.zeros_like(acc_scr)

        def body(j, carry):
            start = pl.multiple_of(j * tk, tk)
            kt = k_ref[pl.ds(start, tk), :]
            vt = v_ref[pl.ds(start, tk), :]
            for h in range(group):
                q = q_ref[:, h * HEAD_DIM:(h + 1) * HEAD_DIM]
                s = lax.dot_general(q, kt, (((1,), (1,)), ((), ())), preferred_element_type=F32)
                m = m_scr[h]
                m_new = jnp.maximum(m, jnp.max(s, axis=1, keepdims=True))
                alpha = jnp.exp2((m - m_new) * c)
                p = jnp.exp2(s * c - m_new * c)
                l_scr[h] = alpha * l_scr[h] + jnp.sum(p, axis=1, keepdims=True)
                acc_scr[h] = alpha * acc_scr[h] + jnp.dot(p.astype(BF16), vt, preferred_element_type=F32)
                m_scr[h] = m_new
            return carry

        lax.fori_loop(0, seq_len // tk, body, 0)
        for h in range(group):
            o_ref[:, h * HEAD_DIM:(h + 1) * HEAD_DIM] = (acc_scr[h] / l_scr[h]).astype(o_ref.dtype)


def _flash_gqa(q, k, v, batch, seq_len, tq=1024, tk=512, tkf=2048):
    tq = min(tq, seq_len)
    tk = min(tk, seq_len)
    tkf = min(tkf, seq_len)
    nq = seq_len // tq
    gw = GA_GROUP * HEAD_DIM
    wide = pltpu.VMEM((GA_GROUP, tq, HEAD_DIM), F32)
    thin = pltpu.VMEM((GA_GROUP, tq, 1), F32)
    return pl.pallas_call(
        functools.partial(_flash_kernel, tq=tq, tk=tk, tkf=tkf, seq_len=seq_len, group=GA_GROUP),
        out_shape=jax.ShapeDtypeStruct(q.shape, BF16),
        grid=(batch, GA_KV_HEADS, nq),
        in_specs=[pl.BlockSpec((tq, gw), lambda b, kv, i: (b * nq + i, kv)),
                  pl.BlockSpec((seq_len, HEAD_DIM), lambda b, kv, i: (b, kv)),
                  pl.BlockSpec((seq_len, HEAD_DIM), lambda b, kv, i: (b, kv))],
        out_specs=pl.BlockSpec((tq, gw), lambda b, kv, i: (b * nq + i, kv)),
        scratch_shapes=[wide, wide, wide, thin, thin, pltpu.VMEM((HALO_ROWS, HEAD_DIM), F32)],
        compiler_params=_params("parallel", "parallel", "arbitrary"),
    )(q, k, v)


def _identity_prologue(o):
    return o


def _gqa_layer(x, batch, seq_len, g, p, route):
    cos, sin = _axial_tables(seq_len)
    q, k, v = _qkv_rope(x, g, p['w_qkv'], p['q_gain'], p['k_gain'], cos, sin,
                        GA_HEADS, GA_KV_HEADS, GA_KV_HEADS, HEAD_DIM // 4, seq_len)
    o = _flash_gqa(q, k, v, batch, seq_len)
    return _mm_res(_identity_prologue, [o], [], p['w_o'], None, x, route)


def _ml_in_kernel(x_ref, xp_ref, xn_ref, g_ref, w_ref, cw_ref, cb_ref, wq_ref, wk_ref, wv_ref,
                  wg_ref, bg_ref, q_ref, k_ref, v_ref, xc_ref, sz_ref, gate_ref,
                  *, seq_len, tm, inner, k_scale):
    keep_prev, keep_next = _edge_scales(pl.program_id(0), tm, seq_len)
    g = g_ref[...]
    xa = _rms(jnp.concatenate([x_ref[...], xp_ref[...], xn_ref[...]], axis=0), g).astype(BF16)
    xb = xa[:tm]
    rows = lax.broadcasted_iota(jnp.int32, (tm, 1), 0)
    gacc = jnp.zeros(gate_ref.shape, F32)
    for t in range(inner // MXU_DIM):
        sl = slice(t * MXU_DIM, (t + 1) * MXU_DIM)
        xma = jnp.dot(xa, w_ref[:, sl], preferred_element_type=F32)
        xm = xma[:tm]
        pprev = xma[tm + HALO_ROWS - 1:tm + HALO_ROWS, :] * keep_prev
        pnext = xma[tm + HALO_ROWS:tm + HALO_ROWS + 1, :] * keep_next
        xc = _conv3_rows(xm, pprev, pnext, cw_ref[:, sl], cb_ref[:, sl], rows, tm)
        xc = xc * jax.nn.sigmoid(xc)
        z = jnp.dot(xb, w_ref[:, inner + t * MXU_DIM:inner + (t + 1) * MXU_DIM],
                    preferred_element_type=F32)
        xcb = xc.astype(BF16)
        q = jnp.dot(xcb, wq_ref[t], preferred_element_type=F32)
        k = jnp.dot(xcb, wk_ref[t], preferred_element_type=F32)
        v = jnp.dot(xm.astype(BF16), wv_ref[t], preferred_element_type=F32)
        qb, kb, vb = q.astype(BF16), k.astype(BF16), v.astype(BF16)
        gacc += (jnp.dot(qb, wg_ref[0, sl, :], preferred_element_type=F32)
                 + jnp.dot(kb, wg_ref[1, sl, :], preferred_element_type=F32)
                 + jnp.dot(vb, wg_ref[2, sl, :], preferred_element_type=F32))
        q_ref[:, sl] = qb
        k_ref[:, sl] = (k * k_scale).astype(BF16)
        v_ref[:, sl] = vb
        xc_ref[:, sl] = xc
        sz_ref[:, sl] = z * jax.nn.sigmoid(z)
    gate_ref[...] = gacc + bg_ref[...]


def _block_diag_tiles(w):
    nb, c, _ = w.shape
    per = MXU_DIM // c
    wt = w.reshape(nb // per, per, c, c)
    eye = jnp.eye(per, dtype=w.dtype)
    full = jnp.einsum('tpcd,pq->tpcqd', wt, eye)
    return full.reshape(nb // per, MXU_DIM, MXU_DIM)


def _ml_in(x, g, p, seq_len, tm=512):
    n, d = x.shape
    tm = min(tm, seq_len)
    inner = p['w_up'].shape[1] // 2
    ng = 4 * ML_HEADS
    dh = inner // ML_HEADS
    wq, wk, wv = (_block_diag_tiles(p[nm]).astype(BF16) for nm in ('w_q', 'w_k', 'w_v'))
    wg = jnp.transpose(p['w_gate'], (1, 2, 0, 3)).reshape(3, inner, ng).astype(BF16)
    bg = p['b_gate'].reshape(1, ng)
    prev, nxt = _halo_specs(tm, d, n)
    row = pl.BlockSpec((tm, d), lambda i: (i, 0))
    wide = pl.BlockSpec((tm, inner), lambda i: (i, 0))
    nt = inner // MXU_DIM
    return pl.pallas_call(
        functools.partial(_ml_in_kernel, seq_len=seq_len, tm=tm, inner=inner, k_scale=dh ** -0.5),
        out_shape=(jax.ShapeDtypeStruct((n, inner), BF16),) * 3
        + (jax.ShapeDtypeStruct((n, inner), F32),) * 2
        + (jax.ShapeDtypeStruct((n, ng), F32),),
        grid=(n // tm,),
        in_specs=[row, prev, nxt, _const_spec((1, d)), _const_spec((d, 2 * inner), single=True),
                  _const_spec((3, inner)), _const_spec((1, inner)),
                  _const_spec((nt, MXU_DIM, MXU_DIM), single=True),
                  _const_spec((nt, MXU_DIM, MXU_DIM), single=True),
                  _const_spec((nt, MXU_DIM, MXU_DIM), single=True),
                  _const_spec((3, inner, ng), single=True), _const_spec((1, ng))],
        out_specs=(wide,) * 5 + (pl.BlockSpec((tm, ng), lambda i: (i, 0)),),
        compiler_params=_params("parallel"),
    )(x, x, x, g.reshape(1, d), p['w_up'].astype(BF16), p['conv_w'],
      p['conv_b'].reshape(1, inner), wq, wk, wv, wg, bg)


def _log_sigmoid(x):
    return jnp.minimum(x, 0.0) - jnp.log1p(jnp.exp(-jnp.abs(x)))


def _mlstm_chunk_kernel(qf_ref, kf_ref, vf_ref, gcf_ref, grf_ref, qb_ref, kb_ref, vb_ref, gcb_ref, grb_ref,
                        hf_ref, hb_ref, c_scr, n_scr, m_scr, *, lc):
    @pl.when(pl.program_id(2) == 0)
    def _():
        c_scr[...] = jnp.zeros_like(c_scr)
        n_scr[...] = jnp.zeros_like(n_scr)
        m_scr[...] = jnp.zeros_like(m_scr)

    _mlstm_step(qf_ref, kf_ref, vf_ref, gcf_ref, grf_ref, hf_ref, c_scr.at[0], n_scr.at[0], m_scr.at[0], lc, False)
    _mlstm_step(qb_ref, kb_ref, vb_ref, gcb_ref, grb_ref, hb_ref, c_scr.at[1], n_scr.at[1], m_scr.at[1], lc, True)


def _mlstm_step(q_ref, k_ref, v_ref, gc_ref, gr_ref, h_ref, c_scr, n_scr, m_scr, lc, backward):
    q = q_ref[...]
    k = k_ref[...]
    v = v_ref[...]
    gc = gc_ref[...]
    gr = gr_ref[...]
    i_col, g_col = gc[:, 0:1], gc[:, 1:2]
    i_row, g_row = gr[0:1, :], gr[1:2, :]

    jr = lax.broadcasted_iota(jnp.int32, (lc, lc), 0)
    sc = lax.broadcasted_iota(jnp.int32, (lc, lc), 1)
    seen = (sc >= jr) if backward else (sc <= jr)
    g_tot = g_row[:, 0:1] if backward else g_row[:, lc - 1:lc]
    m_old = m_scr[0:1, 0:1]

    dmat = jnp.where(seen, g_col - g_row + i_row, -jnp.inf)
    inter = g_col + m_old
    m_q = jnp.maximum(inter, jnp.max(dmat, axis=1, keepdims=True))
    s_qk = lax.dot_general(q, k, (((1,), (1,)), ((), ())), preferred_element_type=F32)
    a = s_qk * jnp.exp(dmat - m_q)
    w_int = jnp.exp(inter - m_q)
    q_c = jnp.dot(q, c_scr[...].astype(BF16), preferred_element_type=F32)
    num = jnp.dot(a.astype(BF16), v, preferred_element_type=F32) + q_c * w_int
    q_n = jnp.sum(q.astype(F32) * n_scr[...], axis=1, keepdims=True)
    den = jnp.sum(a, axis=1, keepdims=True) + w_int * q_n
    den = jnp.maximum(jnp.abs(den), jnp.exp(-m_q))
    h_ref[...] = num / den

    a_row = g_tot - g_row + i_row
    m_new = jnp.maximum(g_tot + m_old, jnp.max(a_row, axis=1, keepdims=True))
    ws_col = jnp.exp(g_tot - g_col + i_col - m_new)
    dec = jnp.exp(g_tot + m_old - m_new)
    kw = k.astype(F32) * ws_col
    upd = lax.dot_general(kw.astype(BF16), v, (((0,), (0,)), ((), ())), preferred_element_type=F32)
    c_scr[...] = dec * c_scr[...] + upd
    n_scr[...] = dec * n_scr[...] + jnp.sum(kw, axis=0, keepdims=True)
    m_scr[...] = jnp.broadcast_to(m_new, m_scr.shape)


def _ml_gate_kernel(g_ref, o_ref, *, lc, nh):
    g = g_ref[...]
    jr = lax.broadcasted_iota(jnp.int32, (lc, lc), 0)
    sc = lax.broadcasted_iota(jnp.int32, (lc, lc), 1)
    tril = jnp.where(sc <= jr, 1.0, 0.0).astype(BF16)
    triu = jnp.where(sc >= jr, 1.0, 0.0).astype(BF16)
    col = lax.broadcasted_iota(jnp.int32, (1, 4 * nh), 1)
    lf = _log_sigmoid(g)
    x1 = lf.astype(BF16)
    r1 = lf - x1.astype(F32)
    x2 = r1.astype(BF16)
    x3 = (r1 - x2.astype(F32)).astype(BF16)
    cum = lambda m: (jnp.dot(m, x1, preferred_element_type=F32) + jnp.dot(m, x2, preferred_element_type=F32)
                     + jnp.dot(m, x3, preferred_element_type=F32))
    gsum = jnp.where(col < 2 * nh, cum(tril), cum(triu))
    o_ref[...] = jnp.where((col % (2 * nh)) >= nh, gsum, g)


def _mlstm_chunks(q, k, v, gates, batch, seq_len):
    n, inner = q.shape
    dh = inner // ML_HEADS
    lc = min(ML_CHUNK, seq_len)
    nc = seq_len // lc
    ng = gates.shape[1]
    gates = pl.pallas_call(
        functools.partial(_ml_gate_kernel, lc=lc, nh=ML_HEADS),
        out_shape=jax.ShapeDtypeStruct((n, ng), F32),
        grid=(n // lc,),
        in_specs=[pl.BlockSpec((lc, ng), lambda i: (i, 0))],
        out_specs=pl.BlockSpec((lc, ng), lambda i: (i, 0)),
        compiler_params=_params("parallel"),
    )(gates)
    g4 = gates.reshape(n, 2, 2, ML_HEADS)
    gcol = jnp.transpose(g4, (1, 3, 0, 2))
    grow = jnp.transpose(g4, (1, 3, 2, 0))

    def specs(dd):
        chunk = (lambda b, c: b * nc + c) if dd == 0 else (lambda b, c: b * nc + nc - 1 - c)
        qkv = pl.BlockSpec((lc, dh), lambda b, h, c: (chunk(b, c), h))
        gc = pl.BlockSpec((None, None, lc, 2), lambda b, h, c: (dd, h, chunk(b, c), 0))
        gr = pl.BlockSpec((None, None, 2, lc), lambda b, h, c: (dd, h, 0, chunk(b, c)))
        return [qkv, qkv, qkv, gc, gr], qkv

    in_f, out_f = specs(0)
    in_b, out_b = specs(1)
    return pl.pallas_call(
        functools.partial(_mlstm_chunk_kernel, lc=lc),
        out_shape=(jax.ShapeDtypeStruct((n, inner), F32),) * 2,
        grid=(batch, ML_HEADS, nc),
        in_specs=in_f + in_b,
        out_specs=(out_f, out_b),
        scratch_shapes=[pltpu.VMEM((2, dh, dh), F32), pltpu.VMEM((2, 1, dh), F32),
                        pltpu.VMEM((2, HALO_ROWS, HEAD_DIM), F32)],
        compiler_params=_params("parallel", "parallel", "arbitrary"),
    )(q, k, v, gcol, grow, q, k, v, gcol, grow)


def _ml_out_prologue(hf, hb, xc, sz, gain, skip):
    h = hf + hb
    dh = h.shape[1] // ML_HEADS
    parts = []
    for i in range(ML_HEADS):
        sl = slice(i * dh, (i + 1) * dh)
        parts.append(_rms(h[:, sl], gain[:, sl]))
    hn = jnp.concatenate(parts, axis=1)
    return (hn + skip * xc) * sz


def _mlstm_layer(x, batch, seq_len, g, p, route):
    q, k, v, xc, sz, gates = _ml_in(x, g, p, seq_len)
    inner = q.shape[1]
    hf, hb = _mlstm_chunks(q, k, v, gates, batch, seq_len)
    return _mm_res(_ml_out_prologue, [hf, hb, xc, sz],
                   [p['norm_gain'].reshape(1, inner), p['skip'].reshape(1, inner)],
                   p['w_down'], None, x, route, tm=256)


def _rope_tables(L):
    inv = ROPE_THETA ** (-(2.0 * jnp.arange(ROPE_DIMS // 2, dtype=F32)) / ROPE_DIMS)
    ang = jnp.arange(L, dtype=F32)[:, None] * inv[None]
    pad = HEAD_DIM - ROPE_DIMS
    cos = jnp.concatenate([jnp.cos(ang), jnp.cos(ang), jnp.ones((L, pad), F32)], axis=-1)
    sin = jnp.concatenate([-jnp.sin(ang), jnp.sin(ang), jnp.zeros((L, pad), F32)], axis=-1)
    return cos, sin


def _band_kernel(q_ref, kp_ref, kc_ref, kn_ref, vp_ref, vc_ref, vn_ref, o_ref, lse_ref,
                 *, s_len, heads, subs):
    i = pl.program_id(2)
    qb = BAND_BLOCK
    w = qb + 2 * BAND_HALF
    a = lax.broadcasted_iota(jnp.int32, (qb, w), 0)
    c = lax.broadcasted_iota(jnp.int32, (qb, w), 1)
    in_band = jnp.abs(c - BAND_HALF - a) <= BAND_HALF
    scale = HEAD_DIM ** -0.5
    for h in range(heads):
        sl = slice(h * HEAD_DIM, (h + 1) * HEAD_DIM)
        kw = jnp.concatenate([kp_ref[qb - BAND_HALF:, sl], kc_ref[:, sl], kn_ref[:BAND_HALF, sl]], axis=0)
        vw = jnp.concatenate([vp_ref[qb - BAND_HALF:, sl], vc_ref[:, sl], vn_ref[:BAND_HALF, sl]], axis=0)
        for u in range(subs):
            rows = slice(u * qb, (u + 1) * qb)
            key_pos = (i * subs + u) * qb - BAND_HALF + c
            valid = in_band & (key_pos >= 0) & (key_pos < s_len)
            s = lax.dot_general(q_ref[rows, sl], kw[u * qb:u * qb + w], (((1,), (1,)), ((), ())),
                                preferred_element_type=F32) * scale
            s = jnp.where(valid, s, -jnp.inf)
            m = jnp.max(s, axis=1, keepdims=True)
            p = jnp.exp(s - m)
            l = jnp.sum(p, axis=1, keepdims=True)
            o = jnp.dot(p.astype(BF16), vw[u * qb:u * qb + w], preferred_element_type=F32)
            o_ref[rows, sl] = o / l
            lse_ref[rows, sl] = jnp.broadcast_to(m + jnp.log(l), (qb, HEAD_DIM))


def _da_qkv_kernel(x_ref, g_ref, w_ref, qg_ref, kg_ref, cos_ref, sin_ref, *refs, dils, tm, half):
    n_perm = sum(1 for d in dils if d > 1)
    perm_refs, out_refs = refs[:n_perm], refs[n_perm:]
    xb = _rms(x_ref[...], g_ref[...]).astype(BF16)
    cos, sin = cos_ref[...], sin_ref[...]
    gw = DA_HEADS_PER_GROUP * HEAD_DIM
    heads = DA_HEADS_PER_GROUP * len(dils)
    for kind in range(3):
        gain = (qg_ref, kg_ref, None)[kind]
        pi = 0
        for gi, dil in enumerate(dils):
            c0 = (kind * heads + gi * DA_HEADS_PER_GROUP) * HEAD_DIM
            ph = jnp.dot(xb, w_ref[:, c0:c0 + gw], preferred_element_type=F32)
            if gain is not None:
                ph = jnp.concatenate(
                    [_head_norm_rope(ph[:, h * HEAD_DIM:(h + 1) * HEAD_DIM], gain[...], cos, sin, half)
                     for h in range(DA_HEADS_PER_GROUP)], axis=1)
            val = ph.astype(BF16)
            o_ref = out_refs[kind * len(dils) + gi]
            if dil == 1:
                o_ref[...] = val
            else:
                pv = jnp.dot(perm_refs[pi][...], val, preferred_element_type=F32).astype(BF16)
                pi += 1
                rows = tm // dil
                for r in range(dil):
                    o_ref[:, r * gw:(r + 1) * gw] = pv[r * rows:(r + 1) * rows, :]


def _da_qkv(x, g, w_qkv, q_gain, k_gain, cos, sin, seq_len, tm=256):
    n, d = x.shape
    tm = min(tm, seq_len)
    pos_blocks = seq_len // tm
    dils = tuple(dil for _, dil in DA_GROUPS)
    gw = DA_HEADS_PER_GROUP * HEAD_DIM
    f = w_qkv.shape[1]
    perms = []
    for dil in dils:
        if dil > 1:
            perms.append(_dilation_perm(tm, dil))
    row = pl.BlockSpec((tm, d), lambda i: (i, 0))
    tab = pl.BlockSpec((tm, HEAD_DIM), lambda i: (i % pos_blocks, 0))
    out_shapes = tuple(jax.ShapeDtypeStruct((n // dil, dil * gw), BF16) for _ in range(3) for dil in dils)
    out_specs = tuple(pl.BlockSpec((tm // dil, dil * gw), lambda i: (i, 0)) for _ in range(3) for dil in dils)
    outs = pl.pallas_call(
        functools.partial(_da_qkv_kernel, dils=dils, tm=tm, half=ROPE_DIMS // 2),
        out_shape=out_shapes,
        grid=(n // tm,),
        in_specs=[row, _const_spec((1, d)), _const_spec((d, f), single=True), _const_spec((1, HEAD_DIM)),
                  _const_spec((1, HEAD_DIM)), tab, tab] + [_const_spec((tm, tm))] * len(perms),
        out_specs=out_specs,
        compiler_params=_params("parallel"),
    )(x, g.reshape(1, d), w_qkv.astype(BF16), q_gain.reshape(1, HEAD_DIM), k_gain.reshape(1, HEAD_DIM),
      cos, sin, *perms)
    ng = len(dils)
    return [(outs[gi], outs[ng + gi], outs[2 * ng + gi]) for gi in range(ng)]


def _band_attention(q, k, v, batch, seq_len, dil):
    n = batch * seq_len
    s_len = seq_len // dil
    nb = s_len // BAND_BLOCK
    subs = min(BAND_SUBBLOCKS, nb)
    hw = DA_HEADS_PER_GROUP * HEAD_DIM
    view = lambda a: a.reshape(batch, s_len, dil * hw)
    qv, kv, vv = view(q), view(k), view(v)

    main = pl.BlockSpec((None, subs * BAND_BLOCK, hw), lambda b, r, i: (b, i, r))
    prev = pl.BlockSpec((None, BAND_BLOCK, hw), lambda b, r, i: (b, jnp.maximum(i * subs - 1, 0), r))
    nxt = pl.BlockSpec((None, BAND_BLOCK, hw), lambda b, r, i: (b, jnp.minimum((i + 1) * subs, nb - 1), r))
    o, lse = pl.pallas_call(
        functools.partial(_band_kernel, s_len=s_len, heads=DA_HEADS_PER_GROUP, subs=subs),
        out_shape=(jax.ShapeDtypeStruct((batch, s_len, dil * hw), F32),) * 2,
        grid=(batch, dil, nb // subs),
        in_specs=[main, prev, main, nxt, prev, main, nxt],
        out_specs=(main, main),
        compiler_params=_params("parallel", "parallel", "parallel"),
    )(qv, kv, kv, kv, vv, vv, vv)
    return o.reshape(n // dil, dil * hw), lse.reshape(n // dil, dil * hw)


def _undilate(blk, perm_t):
    hw = DA_HEADS_PER_GROUP * HEAD_DIM
    dil = blk.shape[1] // hw
    stacked = jnp.concatenate([blk[:, r * hw:(r + 1) * hw] for r in range(dil)], axis=0)
    hi, lo = _split_bf16(stacked)
    return jnp.dot(perm_t, hi, preferred_element_type=F32) + jnp.dot(perm_t, lo, preferred_element_type=F32)


def _da_out_prologue(o0, o1, o2, l0, l1, l2, pt1, pt2):
    o1, l1 = _undilate(o1, pt1), _undilate(l1, pt1)
    o2, l2 = _undilate(o2, pt2), _undilate(l2, pt2)
    m = jnp.maximum(jnp.maximum(l0, l1), l2)
    e0, e1, e2 = jnp.exp(l0 - m), jnp.exp(l1 - m), jnp.exp(l2 - m)
    return (e0 * o0 + e1 * o1 + e2 * o2) / (e0 + e1 + e2)


def _dilation_perm(tm, dil):
    dst = jnp.arange(tm)
    src = (dst % (tm // dil)) * dil + dst // (tm // dil)
    return (src[:, None] == jnp.arange(tm)[None, :]).astype(BF16)


def _dilated_layer(x, batch, seq_len, g, p, route):
    cos, sin = _rope_tables(seq_len)
    qkv = _da_qkv(x, g, p['w_qkv'], p['q_gain'], p['k_gain'], cos, sin, seq_len)
    outs, lses = [], []
    for (q, k, v), (_, dil) in zip(qkv, DA_GROUPS):
        o, lse = _band_attention(q, k, v, batch, seq_len, dil)
        outs.append(o)
        lses.append(lse)
    tm = min(DA_OUT_ROWS, seq_len)
    perms_t = [_dilation_perm(tm, dil).T for _, dil in DA_GROUPS[1:]]
    return _mm_res(_da_out_prologue, outs + lses, perms_t, p['w_o'], None, x, route, tm=tm)


def _expert_ffn_kernel(xe_ref, gate_ref, w1_ref, w3_ref, w2_ref, ye_ref, w1b, w3b, w2b):
    @pl.when(pl.program_id(1) == 0)
    def _():
        w1b[...] = w1_ref[...].astype(BF16)
        w3b[...] = w3_ref[...].astype(BF16)
        w2b[...] = w2_ref[...].astype(BF16)

    xe = xe_ref[...]
    h1 = jnp.dot(xe, w1b[...], preferred_element_type=F32)
    h3 = jnp.dot(xe, w3b[...], preferred_element_type=F32)
    hid = (h1 * jax.nn.sigmoid(h1) * h3).astype(BF16)
    tm = xe.shape[0]
    diag = (lax.broadcasted_iota(jnp.int32, (tm, tm), 0) == lax.broadcasted_iota(jnp.int32, (tm, tm), 1))
    gate = jnp.sum(jnp.where(diag, gate_ref[...], 0.0), axis=1, keepdims=True)
    ye_ref[...] = (jnp.dot(hid, w2b[...], preferred_element_type=F32) * gate).astype(ye_ref.dtype)


def _expert_ffn(xe, gates, w1, w3, w2, layer, tm=512):
    e, c, d = xe.shape
    f = w1.shape[3]
    tm = min(tm, c)
    return pl.pallas_call(
        _expert_ffn_kernel,
        out_shape=jax.ShapeDtypeStruct((e, c, d), BF16),
        grid=(e, c // tm),
        in_specs=[pl.BlockSpec((None, tm, d), lambda ei, ci: (ei, ci, 0)),
                  pl.BlockSpec((None, 1, tm), lambda ei, ci: (ei, 0, ci)),
                  pl.BlockSpec((None, None, d, f), lambda ei, ci: (layer, ei, 0, 0)),
                  pl.BlockSpec((None, None, d, f), lambda ei, ci: (layer, ei, 0, 0)),
                  pl.BlockSpec((None, None, f, d), lambda ei, ci: (layer, ei, 0, 0))],
        out_specs=pl.BlockSpec((None, tm, d), lambda ei, ci: (ei, ci, 0)),
        scratch_shapes=[pltpu.VMEM((d, f), BF16), pltpu.VMEM((d, f), BF16), pltpu.VMEM((f, d), BF16)],
        compiler_params=_params("parallel", "arbitrary"),
    )(xe, gates[:, None, :], w1, w3, w2)


def _moe_layer(x, xn, aff_t, group_sizes, w1, w3, w2, layer, split_output):
    n, d = x.shape
    gts, idxs = [], []
    start = 0
    for ng in group_sizes:
        cap = EC_CAPACITY * ng // N_EXPERTS
        gates, idx = lax.top_k(aff_t[:, start:start + ng], cap)
        gts.append(gates)
        idxs.append(idx + start)
        start += ng
    idx_all = jnp.concatenate(idxs, axis=1)
    ye = _expert_ffn(xn[idx_all], jnp.concatenate(gts, axis=1), w1, w3, w2, layer)
    return _combine(x, ye.reshape(-1, d), idx_all.reshape(-1), group_sizes if split_output else (n,))


COMBINE_TOKENS = 512
COMBINE_ROWS = 512


def _combine_kernel(tile_ref, blk_ref, live_ref, x_ref, tok_ref, ye_ref, *o_refs, split_tiles):
    w = pl.program_id(0)
    tile = tile_ref[w]
    first = jnp.logical_or(w == 0, tile != tile_ref[jnp.maximum(w - 1, 0)])
    tt = x_ref.shape[0]
    rows = lax.broadcasted_iota(jnp.int32, (tt, COMBINE_ROWS), 0)
    onehot = jnp.where(rows == tok_ref[...] - tile * tt, 1.0, 0.0).astype(BF16)
    add = jnp.dot(onehot, ye_ref[...], preferred_element_type=F32) * live_ref[w].astype(F32)
    lo_tile = 0
    for o_ref, n_tiles in zip(o_refs, split_tiles):
        mine = jnp.logical_and(tile >= lo_tile, tile < lo_tile + n_tiles)

        @pl.when(jnp.logical_and(mine, first))
        def _(o_ref=o_ref):
            o_ref[...] = x_ref[...] + add

        @pl.when(jnp.logical_and(mine, jnp.logical_not(first)))
        def _(o_ref=o_ref):
            o_ref[...] += add

        lo_tile += n_tiles


def _combine(x, ye, tok, splits):
    n, d = x.shape
    p = tok.shape[0]
    tt = min(COMBINE_TOKENS, min(splits))
    rb = COMBINE_ROWS
    assert p % rb == 0 and all(s % tt == 0 for s in splits) and sum(splits) == n
    tiles, nblk = n // tt, p // rb
    split_tiles = tuple(s // tt for s in splits)
    order = jnp.argsort(tok)
    tok_sorted = tok[order].astype(jnp.int32)
    ye_sorted = ye[order]

    edges = jnp.arange(tiles + 1, dtype=jnp.int32) * tt
    bounds = jnp.searchsorted(tok_sorted, edges, method='compare_all').astype(jnp.int32)
    lo, hi = bounds[:-1], bounds[1:]
    first_blk = jnp.minimum(lo // rb, nblk - 1)
    last_blk = jnp.where(hi > lo, (hi - 1) // rb, first_blk)
    n_items = last_blk - first_blk + 1
    item_end = jnp.cumsum(n_items)
    item_start = item_end - n_items
    max_items = nblk + 2 * tiles
    w = jnp.arange(max_items, dtype=jnp.int32)
    tile = jnp.minimum(jnp.searchsorted(item_end, w, side='right', method='compare_all'),
                       tiles - 1).astype(jnp.int32)
    k = w - item_start[tile]
    live = (k < n_items[tile]).astype(jnp.int32)
    blk = jnp.minimum(first_blk[tile] + k, nblk - 1).astype(jnp.int32)

    def out_spec(lo_tile, n_tiles):
        return pl.BlockSpec(
            (tt, d), lambda i, tile_r, blk_r, live_r: (jnp.clip(tile_r[i] - lo_tile, 0, n_tiles - 1), 0))

    starts = [sum(split_tiles[:j]) for j in range(len(splits))]
    grid_spec = pltpu.PrefetchScalarGridSpec(
        num_scalar_prefetch=3,
        grid=(max_items,),
        in_specs=[pl.BlockSpec((tt, d), lambda i, tile_r, blk_r, live_r: (tile_r[i], 0)),
                  pl.BlockSpec((None, 1, rb), lambda i, tile_r, blk_r, live_r: (blk_r[i], 0, 0)),
                  pl.BlockSpec((rb, d), lambda i, tile_r, blk_r, live_r: (blk_r[i], 0))],
        out_specs=tuple(out_spec(s, t) for s, t in zip(starts, split_tiles)),
    )
    return pl.pallas_call(
        functools.partial(_combine_kernel, split_tiles=split_tiles),
        out_shape=tuple(jax.ShapeDtypeStruct((s, d), F32) for s in splits),
        grid_spec=grid_spec,
        compiler_params=_params("arbitrary"),
    )(tile, blk, live, x, tok_sorted.reshape(nblk, 1, rb), ye_sorted)


def _trunk(x, batch, seq_len, group_sizes, p):
    depth = p['norm_gain'].shape[0]
    layers = ((_hyena_layer, 'hy_'), (_gqa_layer, 'ga_'), (_mlstm_layer, 'ml_'), (_dilated_layer, 'da_'))
    for i in range(depth):
        layer, prefix = layers[i % 4]
        lp = {k[len(prefix):]: v[i // 4] for k, v in p.items() if k.startswith(prefix)}
        route = (p['norm_gain'][i, 1], p['moe_w_router'][i])
        x, xn, aff_t = layer(x, batch, seq_len, p['norm_gain'][i, 0], lp, route)
        last = i == depth - 1
        outs = _moe_layer(x, xn, aff_t, group_sizes, p['moe_w1'], p['moe_w3'], p['moe_w2'], i,
                          split_output=last)
        x = outs if last else outs[0]
    return x


def kernel(x_prompt, x_sample, norm_gain, hy_w_in, hy_b_in, hy_conv_w, hy_conv_b, hy_f_w1, hy_f_b1, hy_f_w2, hy_f_b2, hy_f_w3, hy_f_b3, hy_f_freq, hy_decay, hy_skip, hy_w_out, hy_b_out, ga_w_qkv, ga_q_gain, ga_k_gain, ga_w_o, ml_w_up, ml_conv_w, ml_conv_b, ml_w_q, ml_w_k, ml_w_v, ml_w_gate, ml_b_gate, ml_norm_gain, ml_skip, ml_w_down, da_w_qkv, da_q_gain, da_k_gain, da_w_o, moe_w_router, moe_w1, moe_w3, moe_w2):
    p = dict(
        norm_gain=norm_gain,
        hy_w_in=hy_w_in, hy_b_in=hy_b_in, hy_conv_w=hy_conv_w, hy_conv_b=hy_conv_b,
        hy_f_w1=hy_f_w1, hy_f_b1=hy_f_b1, hy_f_w2=hy_f_w2, hy_f_b2=hy_f_b2,
        hy_f_w3=hy_f_w3, hy_f_b3=hy_f_b3, hy_f_freq=hy_f_freq, hy_decay=hy_decay,
        hy_skip=hy_skip, hy_w_out=hy_w_out, hy_b_out=hy_b_out,
        ga_w_qkv=ga_w_qkv, ga_q_gain=ga_q_gain, ga_k_gain=ga_k_gain, ga_w_o=ga_w_o,
        ml_w_up=ml_w_up, ml_conv_w=ml_conv_w, ml_conv_b=ml_conv_b, ml_w_q=ml_w_q,
        ml_w_k=ml_w_k, ml_w_v=ml_w_v, ml_w_gate=ml_w_gate, ml_b_gate=ml_b_gate,
        ml_norm_gain=ml_norm_gain, ml_skip=ml_skip, ml_w_down=ml_w_down,
        da_w_qkv=da_w_qkv, da_q_gain=da_q_gain, da_k_gain=da_k_gain, da_w_o=da_w_o,
        moe_w_router=moe_w_router, moe_w1=moe_w1, moe_w3=moe_w3, moe_w2=moe_w2,
    )
    bp, seq_len, d = x_prompt.shape
    bs = x_sample.shape[0]
    assert x_sample.shape[1] == seq_len
    x = jnp.concatenate([x_prompt, x_sample], axis=0).reshape((bp + bs) * seq_len, d)
    y_prompt, y_sample = _trunk(x, bp + bs, seq_len, (bp * seq_len, bs * seq_len), p)
    return (y_prompt.reshape(bp, seq_len, d), y_sample.reshape(bs, seq_len, d))
```
